```python
import jax, jax.numpy as jnp
from jax import lax
import numpy as np

D_MODEL = 1024
BATCH = 16
SEQ = 4096
DEPTH = 1

D_MIX = D_MODEL
D_CONF = D_MIX // 2
D_SC = D_MIX - D_CONF
CONF_HEADS = 8
SC_HEADS = 8
CONF_KERNEL = 31
SC_KERNEL = 3
D_FF = 4 * D_MODEL
D_IN = 2 * D_CONF + 3 * D_SC
N_MOD = 6
EPS = 1e-6

kernel_name = "hybrid_conformer_shortconv_adaln_block"


def rms_norm(x, gain=None):
    xf = x.astype(jnp.float32)
    y = xf * lax.rsqrt(jnp.mean(xf * xf, axis=-1, keepdims=True) + EPS)
    if gain is not None:
        y = y * gain.astype(jnp.float32)
    return y.astype(x.dtype)


def layer_norm(x, gain, bias):
    xf = x.astype(jnp.float32)
    mu = jnp.mean(xf, axis=-1, keepdims=True)
    var = jnp.mean(jnp.square(xf - mu), axis=-1, keepdims=True)
    y = (xf - mu) * lax.rsqrt(var + EPS) * gain.astype(jnp.float32) + bias.astype(jnp.float32)
    return y.astype(x.dtype)


def causal_depthwise_conv(u, w):
    k = w.shape[0]
    return lax.conv_general_dilated(
        u, w[:, None, :].astype(u.dtype), window_strides=(1,), padding=[(k - 1, 0)],
        dimension_numbers=("NWC", "WIO", "NWC"), feature_group_count=u.shape[-1])


def modulate(h, shift, scale):
    return h * (1.0 + scale[:, None, :]) + shift[:, None, :]


def _fwd_setup_inputs(seed: int = 0) -> dict:
    key = jax.random.key(seed)
    ks = jax.random.split(key, 16)
    f = jnp.float32
    n = lambda k, shape, s: (jax.random.normal(k, shape, f) * s).astype(f)
    return {
        "x": n(ks[0], (BATCH, SEQ, D_MODEL), 1.0),
        "c": n(ks[1], (BATCH, D_MODEL), 1.0),
        "w_ada": n(ks[2], (DEPTH, D_MODEL, N_MOD * D_MODEL), 0.5 * D_MODEL ** -0.5),
        "b_ada": n(ks[3], (DEPTH, N_MOD * D_MODEL), 0.02),
        "w_in": n(ks[4], (DEPTH, D_MODEL, D_IN), D_MODEL ** -0.5),
        "conf_dw_w": n(ks[5], (DEPTH, CONF_KERNEL, D_CONF), CONF_KERNEL ** -0.5),
        "conf_dw_b": n(ks[6], (DEPTH, D_CONF), 0.02),
        "conf_ln_g": 1.0 + n(ks[7], (DEPTH, D_CONF), 0.02),
        "conf_ln_b": n(ks[8], (DEPTH, D_CONF), 0.02),
        "sc_conv_w": n(ks[9], (DEPTH, SC_KERNEL, D_SC), SC_KERNEL ** -0.5),
        "w_out": n(ks[10], (DEPTH, D_MIX, D_MODEL), D_MIX ** -0.5),
        "w_mlp1": n(ks[11], (DEPTH, D_MODEL, D_FF), D_MODEL ** -0.5),
        "w_mlp2": n(ks[12], (DEPTH, D_FF, D_MODEL), D_FF ** -0.5),
        "g_final": 1.0 + n(ks[13], (D_MODEL,), 0.02),
    }


def _fwd_reference(x, c, w_ada, b_ada, w_in, conf_dw_w, conf_dw_b, conf_ln_g, conf_ln_b,
              sc_conv_w, w_out, w_mlp1, w_mlp2, g_final):
    dt = x.dtype
    c_act = jax.nn.silu(c)
    for l in range(DEPTH):
        mod = jnp.einsum("bd,de->be", c_act, w_ada[l]) + b_ada[l]
        sh1, sc1, g1, sh2, sc2, g2 = jnp.split(mod.astype(dt), N_MOD, axis=-1)

        h = modulate(rms_norm(x), sh1, sc1)
        proj = jnp.einsum("btd,de->bte", h, w_in[l])
        conf_val, conf_gate, sc_b, sc_c, sc_h = jnp.split(
            proj, np.cumsum([D_CONF, D_CONF, D_SC, D_SC])[:4].tolist(), axis=-1)

        a = conf_val * jax.nn.sigmoid(conf_gate)
        a = causal_depthwise_conv(a, conf_dw_w[l]) + conf_dw_b[l].astype(dt)
        a = jax.nn.silu(layer_norm(a, conf_ln_g[l], conf_ln_b[l]))

        s = sc_b * causal_depthwise_conv(sc_c * sc_h, sc_conv_w[l])

        mixed = jnp.concatenate([a, s], axis=-1)
        y = jnp.einsum("bte,ed->btd", mixed, w_out[l])
        x = x + g1[:, None, :] * y

        h = modulate(rms_norm(x), sh2, sc2)
        u = jnp.square(jax.nn.relu(jnp.einsum("btd,df->btf", h, w_mlp1[l])))
        y = jnp.einsum("btf,fd->btd", u, w_mlp2[l])
        x = x + g2[:, None, :] * y

    return rms_norm(x, g_final)


import jax as _jax
import jax.numpy as _jnp

TWIN_FORMAT = 'train_step'
FWD_PARAMS = ['x', 'c', 'w_ada', 'b_ada', 'w_in', 'conf_dw_w', 'conf_dw_b', 'conf_ln_g', 'conf_ln_b', 'sc_conv_w', 'w_out', 'w_mlp1', 'w_mlp2', 'g_final']
TWIN_WEIGHTS = ['w_ada', 'b_ada', 'w_in', 'conf_dw_w', 'conf_dw_b', 'conf_ln_g', 'conf_ln_b', 'sc_conv_w', 'w_out', 'w_mlp1', 'w_mlp2', 'g_final']
TWIN_DIFF_INPUT = 'x'
TWIN_INPUTS = ['x', 'c', 'w_ada', 'b_ada', 'w_in', 'conf_dw_w', 'conf_dw_b', 'conf_ln_g', 'conf_ln_b', 'sc_conv_w', 'w_out', 'w_mlp1', 'w_mlp2', 'g_final', 'loss_target', 'm_w_ada', 'm_b_ada', 'm_w_in', 'm_conf_dw_w', 'm_conf_dw_b', 'm_conf_ln_g', 'm_conf_ln_b', 'm_sc_conv_w', 'm_w_out', 'm_w_mlp1', 'm_w_mlp2', 'm_g_final', 'v_w_ada', 'v_b_ada', 'v_w_in', 'v_conf_dw_w', 'v_conf_dw_b', 'v_conf_ln_g', 'v_conf_ln_b', 'v_sc_conv_w', 'v_w_out', 'v_w_mlp1', 'v_w_mlp2', 'v_g_final']
TWIN_OUTPUTS = ['loss', 'grad_x', 'grad_w_ada', 'grad_b_ada', 'grad_w_in', 'grad_conf_dw_w', 'grad_conf_dw_b', 'grad_conf_ln_g', 'grad_conf_ln_b', 'grad_sc_conv_w', 'grad_w_out', 'grad_w_mlp1', 'grad_w_mlp2', 'grad_g_final', 'delta_w_ada', 'delta_b_ada', 'delta_w_in', 'delta_conf_dw_w', 'delta_conf_dw_b', 'delta_conf_ln_g', 'delta_conf_ln_b', 'delta_sc_conv_w', 'delta_w_out', 'delta_w_mlp1', 'delta_w_mlp2', 'delta_g_final', 'new_m_w_ada', 'new_m_b_ada', 'new_m_w_in', 'new_m_conf_dw_w', 'new_m_conf_dw_b', 'new_m_conf_ln_g', 'new_m_conf_ln_b', 'new_m_sc_conv_w', 'new_m_w_out', 'new_m_w_mlp1', 'new_m_w_mlp2', 'new_m_g_final', 'new_v_w_ada', 'new_v_b_ada', 'new_v_w_in', 'new_v_conf_dw_w', 'new_v_conf_dw_b', 'new_v_conf_ln_g', 'new_v_conf_ln_b', 'new_v_sc_conv_w', 'new_v_w_out', 'new_v_w_mlp1', 'new_v_w_mlp2', 'new_v_g_final']
TWIN_LEAF_KINDS = {'loss': 'loss', 'grad_x': 'grad_x', 'grad_w_ada': 'grad_w', 'grad_b_ada': 'grad_w', 'grad_w_in': 'grad_w', 'grad_conf_dw_w': 'grad_w', 'grad_conf_dw_b': 'grad_w', 'grad_conf_ln_g': 'grad_w', 'grad_conf_ln_b': 'grad_w', 'grad_sc_conv_w': 'grad_w', 'grad_w_out': 'grad_w', 'grad_w_mlp1': 'grad_w', 'grad_w_mlp2': 'grad_w', 'grad_g_final': 'grad_w', 'delta_w_ada': 'delta_w', 'delta_b_ada': 'delta_w', 'delta_w_in': 'delta_w', 'delta_conf_dw_w': 'delta_w', 'delta_conf_dw_b': 'delta_w', 'delta_conf_ln_g': 'delta_w', 'delta_conf_ln_b': 'delta_w', 'delta_sc_conv_w': 'delta_w', 'delta_w_out': 'delta_w', 'delta_w_mlp1': 'delta_w', 'delta_w_mlp2': 'delta_w', 'delta_g_final': 'delta_w', 'new_m_w_ada': 'new_m', 'new_m_b_ada': 'new_m', 'new_m_w_in': 'new_m', 'new_m_conf_dw_w': 'new_m', 'new_m_conf_dw_b': 'new_m', 'new_m_conf_ln_g': 'new_m', 'new_m_conf_ln_b': 'new_m', 'new_m_sc_conv_w': 'new_m', 'new_m_w_out': 'new_m', 'new_m_w_mlp1': 'new_m', 'new_m_w_mlp2': 'new_m', 'new_m_g_final': 'new_m', 'new_v_w_ada': 'new_v', 'new_v_b_ada': 'new_v', 'new_v_w_in': 'new_v', 'new_v_conf_dw_w': 'new_v', 'new_v_conf_dw_b': 'new_v', 'new_v_conf_ln_g': 'new_v', 'new_v_conf_ln_b': 'new_v', 'new_v_sc_conv_w': 'new_v', 'new_v_w_out': 'new_v', 'new_v_w_mlp1': 'new_v', 'new_v_w_mlp2': 'new_v', 'new_v_g_final': 'new_v'}


def _forward(args):
    return _fwd_reference(*[args[k] for k in FWD_PARAMS])


def _output_shape():
    out = _jax.eval_shape(lambda: _forward(_fwd_setup_inputs(0)))
    return out.shape, out.dtype

N_MICROBATCH = 1
ADAM_LR = 0.001
ADAM_B1 = 0.9
ADAM_B2 = 0.999
ADAM_EPS = 1e-08
ADAM_WD = 0.01
ADAM_STEP = 10
PER_EXAMPLE_BATCH_AXIS = {'x': 0, 'c': 0, 'loss_target': 0}
SHARED_INPUTS = []
_WEIGHT_DTYPES = {'w_ada': _jnp.float32, 'b_ada': _jnp.float32, 'w_in': _jnp.float32, 'conf_dw_w': _jnp.float32, 'conf_dw_b': _jnp.float32, 'conf_ln_g': _jnp.float32, 'conf_ln_b': _jnp.float32, 'sc_conv_w': _jnp.float32, 'w_out': _jnp.float32, 'w_mlp1': _jnp.float32, 'w_mlp2': _jnp.float32, 'g_final': _jnp.float32}
MOMENT_SCALE = {'w_ada': 1.435500e-01, 'b_ada': 2.485717e-01, 'w_in': 8.158174e-02, 'conf_dw_w': 4.658797e-02, 'conf_dw_b': 9.037072e-02, 'conf_ln_g': 5.719076e-02, 'conf_ln_b': 4.905878e-02, 'sc_conv_w': 9.849222e-02, 'w_out': 7.827133e-02, 'w_mlp1': 5.701356e-02, 'w_mlp2': 1.051526e-01, 'g_final': 6.468171e+01}


def _to_microbatches(a, axis):
    t = _jnp.moveaxis(a, axis, 0)
    t = t.reshape((N_MICROBATCH, t.shape[0] // N_MICROBATCH) + t.shape[1:])
    return _jnp.moveaxis(t, 1, axis + 1)


def setup_inputs(seed: int = 0) -> dict:
    inp = _fwd_setup_inputs(seed)
    key = _jax.random.fold_in(_jax.random.key(seed), 7919)
    shape, _ = _output_shape()
    out = dict(inp)
    out["loss_target"] = _jax.random.normal(_jax.random.fold_in(key, 0), shape, _jnp.float32)
    for i, name in enumerate(TWIN_WEIGHTS):
        w = inp[name].astype(_jnp.float32)
        if MOMENT_SCALE is None:
            s = _jnp.sqrt(_jnp.mean(_jnp.square(w)) + 1e-30)
        else:
            s = MOMENT_SCALE[name]
        km, kv = _jax.random.split(_jax.random.fold_in(key, i + 1))
        out[name] = w
        out["m_" + name] = s * _jax.random.normal(km, w.shape, _jnp.float32)
        out["v_" + name] = (s * s) * _jax.random.uniform(kv, w.shape, _jnp.float32, 0.5, 1.5)
    if N_MICROBATCH > 1:
        for name, axis in PER_EXAMPLE_BATCH_AXIS.items():
            out[name] = _to_microbatches(out[name], axis)
    return {'x': out['x'], 'c': out['c'], 'w_ada': out['w_ada'], 'b_ada': out['b_ada'], 'w_in': out['w_in'], 'conf_dw_w': out['conf_dw_w'], 'conf_dw_b': out['conf_dw_b'], 'conf_ln_g': out['conf_ln_g'], 'conf_ln_b': out['conf_ln_b'], 'sc_conv_w': out['sc_conv_w'], 'w_out': out['w_out'], 'w_mlp1': out['w_mlp1'], 'w_mlp2': out['w_mlp2'], 'g_final': out['g_final'], 'loss_target': out['loss_target'], 'm_w_ada': out['m_w_ada'], 'm_b_ada': out['m_b_ada'], 'm_w_in': out['m_w_in'], 'm_conf_dw_w': out['m_conf_dw_w'], 'm_conf_dw_b': out['m_conf_dw_b'], 'm_conf_ln_g': out['m_conf_ln_g'], 'm_conf_ln_b': out['m_conf_ln_b'], 'm_sc_conv_w': out['m_sc_conv_w'], 'm_w_out': out['m_w_out'], 'm_w_mlp1': out['m_w_mlp1'], 'm_w_mlp2': out['m_w_mlp2'], 'm_g_final': out['m_g_final'], 'v_w_ada': out['v_w_ada'], 'v_b_ada': out['v_b_ada'], 'v_w_in': out['v_w_in'], 'v_conf_dw_w': out['v_conf_dw_w'], 'v_conf_dw_b': out['v_conf_dw_b'], 'v_conf_ln_g': out['v_conf_ln_g'], 'v_conf_ln_b': out['v_conf_ln_b'], 'v_sc_conv_w': out['v_sc_conv_w'], 'v_w_out': out['v_w_out'], 'v_w_mlp1': out['v_w_mlp1'], 'v_w_mlp2': out['v_w_mlp2'], 'v_g_final': out['v_g_final']}


def _loss(weights, diff, rest, loss_target):
    with _jax.named_scope("forward"):
        args = {**rest, TWIN_DIFF_INPUT: diff, **{k: w.astype(_WEIGHT_DTYPES[k]) for k, w in weights.items()}}
        y = _forward(args)
    with _jax.named_scope("loss_head"):
        err = _jnp.square(y.astype(_jnp.float32) - loss_target)
        return 0.5 * _jnp.sum(_jnp.mean(err, axis=-1)) if err.ndim else 0.5 * err


def _adamw(w, g, m, v):
    m = ADAM_B1 * m + (1.0 - ADAM_B1) * g
    v = ADAM_B2 * v + (1.0 - ADAM_B2) * _jnp.square(g)
    m_hat = m / (1.0 - ADAM_B1 ** ADAM_STEP)
    v_hat = v / (1.0 - ADAM_B2 ** ADAM_STEP)
    delta = -ADAM_LR * (m_hat / (_jnp.sqrt(v_hat) + ADAM_EPS) + ADAM_WD * w)
    return delta, m, v


def reference(x, c, w_ada, b_ada, w_in, conf_dw_w, conf_dw_b, conf_ln_g, conf_ln_b, sc_conv_w, w_out, w_mlp1, w_mlp2, g_final, loss_target, m_w_ada, m_b_ada, m_w_in, m_conf_dw_w, m_conf_dw_b, m_conf_ln_g, m_conf_ln_b, m_sc_conv_w, m_w_out, m_w_mlp1, m_w_mlp2, m_g_final, v_w_ada, v_b_ada, v_w_in, v_conf_dw_w, v_conf_dw_b, v_conf_ln_g, v_conf_ln_b, v_sc_conv_w, v_w_out, v_w_mlp1, v_w_mlp2, v_g_final):
    given = dict(x=x, c=c, w_ada=w_ada, b_ada=b_ada, w_in=w_in, conf_dw_w=conf_dw_w, conf_dw_b=conf_dw_b, conf_ln_g=conf_ln_g, conf_ln_b=conf_ln_b, sc_conv_w=sc_conv_w, w_out=w_out, w_mlp1=w_mlp1, w_mlp2=w_mlp2, g_final=g_final, loss_target=loss_target, m_w_ada=m_w_ada, m_b_ada=m_b_ada, m_w_in=m_w_in, m_conf_dw_w=m_conf_dw_w, m_conf_dw_b=m_conf_dw_b, m_conf_ln_g=m_conf_ln_g, m_conf_ln_b=m_conf_ln_b, m_sc_conv_w=m_sc_conv_w, m_w_out=m_w_out, m_w_mlp1=m_w_mlp1, m_w_mlp2=m_w_mlp2, m_g_final=m_g_final, v_w_ada=v_w_ada, v_b_ada=v_b_ada, v_w_in=v_w_in, v_conf_dw_w=v_conf_dw_w, v_conf_dw_b=v_conf_dw_b, v_conf_ln_g=v_conf_ln_g, v_conf_ln_b=v_conf_ln_b, v_sc_conv_w=v_sc_conv_w, v_w_out=v_w_out, v_w_mlp1=v_w_mlp1, v_w_mlp2=v_w_mlp2, v_g_final=v_g_final)
    weights = {n: given[n] for n in TWIN_WEIGHTS}
    shared = {n: given[n] for n in SHARED_INPUTS}
    per_example = {n: given[n] for n in ['x', 'c']}
    grad_fn = _jax.value_and_grad(_loss, argnums=(0, 1))

    def one_microbatch(ex, loss_target):
        ex = dict(ex)
        diff = ex.pop(TWIN_DIFF_INPUT)
        return grad_fn(weights, diff, {**shared, **ex}, loss_target)

    if N_MICROBATCH == 1:
        loss, (grad_w, grad_x) = one_microbatch(per_example, given["loss_target"])
    else:
        def body(carry, xs):
            loss_sum, grad_sum = carry
            l_k, (gw_k, gx_k) = one_microbatch(xs[0], xs[1])
            with _jax.named_scope("update"):
                return (loss_sum + l_k, _jax.tree.map(_jnp.add, grad_sum, gw_k)), gx_k

        init = (_jnp.zeros((), _jnp.float32), _jax.tree.map(_jnp.zeros_like, weights))
        (loss, grad_w), grad_x = _jax.lax.scan(body, init, (per_example, given["loss_target"]))
    with _jax.named_scope("update"):
        delta_w, new_m, new_v = {}, {}, {}
        for n in TWIN_WEIGHTS:
            delta_w[n], new_m[n], new_v[n] = _adamw(weights[n], grad_w[n], given["m_" + n], given["v_" + n])
    return (loss, grad_x, *[grad_w[n] for n in TWIN_WEIGHTS], *[delta_w[n] for n in TWIN_WEIGHTS],
            *[new_m[n] for n in TWIN_WEIGHTS], *[new_v[n] for n in TWIN_WEIGHTS])
```

```python
import functools

import jax
import jax.numpy as jnp
from jax import lax
from jax.experimental import pallas as pl
from jax.experimental.pallas import tpu as pltpu

F32 = jnp.float32
BF16 = jnp.bfloat16
MESH = pl.DeviceIdType.MESH
HIGHEST = lax.Precision.HIGHEST

EPS = 1e-6
CONF_K = 31
SC_K = 3
HALO = 32
N_CHIPS = 4
N_DEV = 8

ADAM_LR = 0.001
ADAM_B1 = 0.9
ADAM_B2 = 0.999
ADAM_EPS = 1e-08
ADAM_WD = 0.01
ADAM_STEP = 10

TM_MM = 512
TM_MIX = 256
TM_MLP = 256
TK_WG = 512
RB_CONV = 64
RB_WG = 32
CHIP_RELS = ((1, 0), (0, 1), (1, 1))

ANY = pl.BlockSpec(memory_space=pl.ANY)
VMEM = pl.BlockSpec(memory_space=pltpu.VMEM)


def _me():
    return lax.axis_index("x"), lax.axis_index("y"), lax.axis_index("c")


def _flip(v, f):
    return 1 - v if f else v


def _rows8(v):
    r, c = v.shape
    return v.reshape(r // 8, 8, c).sum(axis=0)


def _rms(x):
    r = lax.rsqrt(jnp.mean(x * x, axis=-1, keepdims=True) + EPS)
    return x * r, r


def _rms_bwd(dxn, xn, r):
    return r * (dxn - xn * jnp.mean(dxn * xn, axis=-1, keepdims=True))


def _sigmoid(x):
    return 1.0 / (1.0 + jnp.exp(-x))


def _dot(a, b):
    return jnp.dot(a, b, preferred_element_type=F32)


def _dot_nt(a, b):
    return lax.dot_general(a, b, (((1,), (1,)), ((), ())), preferred_element_type=F32)


def _dot_tn(a, b):
    return lax.dot_general(a, b, (((0,), (0,)), ((), ())), preferred_element_type=F32)


def _const(shape):
    nd = len(shape)
    return pl.BlockSpec(shape, lambda i: (0,) * nd)


def _resident(shape):
    nd = len(shape)
    return pl.BlockSpec(shape, lambda i: (0,) * nd, pipeline_mode=pl.Buffered(1))


def _rowblk(tm, c):
    return pl.BlockSpec((tm, c), lambda i: (i, 0))


def _modspec(tps, width):
    return pl.BlockSpec((None, 1, width), lambda i: (i // tps, 0, 0))


def _accspec(tps, c):
    return pl.BlockSpec((None, 8, c), lambda i: (i // tps, 0, 0))


def _acc_add(ref, first, v):
    @pl.when(first)
    def _():
        ref[...] = v

    @pl.when(jnp.logical_not(first))
    def _():
        ref[...] += v


def _all_gather8(arrs, name):
    n = len(arrs)

    def body(*refs):
        ins, outs = refs[:n], refs[n:2 * n]
        send, recv = refs[2 * n:]
        x, y, c = _me()
        dev = 4 * x + 2 * y + c
        for a in range(n):
            outs[a][dev] = ins[a][...]
        sends = []
        for r in range(1, N_DEV):
            fx, fy, fc = (r >> 2) & 1, (r >> 1) & 1, r & 1
            peer = (_flip(x, fx), _flip(y, fy), _flip(c, fc))
            for a in range(n):
                cp = pltpu.make_async_remote_copy(
                    src_ref=ins[a], dst_ref=outs[a].at[dev],
                    send_sem=send.at[r - 1, a], recv_sem=recv.at[r - 1, a],
                    device_id=peer, device_id_type=MESH)
                cp.start()
                sends.append(cp)
        for r in range(1, N_DEV):
            fx, fy, fc = (r >> 2) & 1, (r >> 1) & 1, r & 1
            pdev = 4 * _flip(x, fx) + 2 * _flip(y, fy) + _flip(c, fc)
            for a in range(n):
                pltpu.make_async_remote_copy(
                    src_ref=ins[a], dst_ref=outs[a].at[pdev],
                    send_sem=send.at[r - 1, a], recv_sem=recv.at[r - 1, a],
                    device_id=(x, y, c), device_id_type=MESH).wait_recv()
        for cp in sends:
            cp.wait_send()

    return pl.pallas_call(
        body, name=name,
        out_shape=[jax.ShapeDtypeStruct((N_DEV,) + a.shape, a.dtype) for a in arrs],
        in_specs=[VMEM] * n, out_specs=[VMEM] * n,
        scratch_shapes=[pltpu.SemaphoreType.DMA((N_DEV - 1, n)),
                        pltpu.SemaphoreType.DMA((N_DEV - 1, n))],
    )(*arrs)


def _gather_mod(mod_shard):
    nb, w = mod_shard.shape

    def body(in_ref, out_ref, send, recv):
        x, y, c = _me()
        chip = 2 * x + y
        out_ref[:, pl.ds(pl.multiple_of(chip * w, 128), w)] = in_ref[...]
        sends = []
        for r, (fx, fy) in enumerate(CHIP_RELS):
            cp = pltpu.make_async_remote_copy(
                src_ref=in_ref,
                dst_ref=out_ref.at[:, pl.ds(pl.multiple_of(chip * w, 128), w)],
                send_sem=send.at[r], recv_sem=recv.at[r],
                device_id=(_flip(x, fx), _flip(y, fy), c), device_id_type=MESH)
            cp.start()
            sends.append(cp)
        for r, (fx, fy) in enumerate(CHIP_RELS):
            pchip = 2 * _flip(x, fx) + _flip(y, fy)
            pltpu.make_async_remote_copy(
                src_ref=in_ref,
                dst_ref=out_ref.at[:, pl.ds(pl.multiple_of(pchip * w, 128), w)],
                send_sem=send.at[r], recv_sem=recv.at[r],
                device_id=(x, y, c), device_id_type=MESH).wait_recv()
        for cp in sends:
            cp.wait_send()

    return pl.pallas_call(
        body, name="gather_mod",
        out_shape=jax.ShapeDtypeStruct((nb, N_CHIPS * w), mod_shard.dtype),
        in_specs=[VMEM], out_specs=VMEM,
        scratch_shapes=[pltpu.SemaphoreType.DMA((3,)), pltpu.SemaphoreType.DMA((3,))],
    )(mod_shard)


def _wlayout(d):
    d_in = 5 * d // 2
    return (
        (d, d_in // N_CHIPS, True),
        (d // N_CHIPS, d, False),
        (d, 4 * d // N_CHIPS, True),
        (4 * d // N_CHIPS, d, False),
    )


def _full_shape(lay):
    r, c, by_col = lay
    return (r, c * N_CHIPS) if by_col else (r * N_CHIPS, c)


def _full_view(ref, lay, k, h):
    r, c, by_col = lay
    hr = r // 2
    if by_col:
        return ref.at[pl.ds(pl.multiple_of(h * hr, 16), hr), pl.ds(pl.multiple_of(k * c, 128), c)]
    return ref.at[pl.ds(pl.multiple_of(k * r + h * hr, 16), hr), :]


def _half_view(ref, lay, h):
    hr = lay[0] // 2
    return ref.at[pl.ds(pl.multiple_of(h * hr, 16), hr), :]


def _gather_weights(shards, d):
    lays = _wlayout(d)
    nw = len(lays)

    def body(*refs):
        sh, full = refs[:nw], refs[nw:2 * nw]
        loc, send1, recv1, send2, recv2 = refs[2 * nw:]
        x, y, c = _me()
        chip = 2 * x + y
        local = []
        for w in range(nw):
            for h in range(2):
                cp = pltpu.make_async_copy(_half_view(sh[w], lays[w], h),
                                           _full_view(full[w], lays[w], chip, h), loc.at[w, h])
                cp.start()
                local.append(cp)
        sends = []
        for r, (fx, fy) in enumerate(CHIP_RELS):
            for w in range(nw):
                cp = pltpu.make_async_remote_copy(
                    src_ref=_half_view(sh[w], lays[w], c),
                    dst_ref=_full_view(full[w], lays[w], chip, c),
                    send_sem=send1.at[r, w], recv_sem=recv1.at[r, w],
                    device_id=(_flip(x, fx), _flip(y, fy), c), device_id_type=MESH)
                cp.start()
                sends.append(cp)
        for r, (fx, fy) in enumerate(CHIP_RELS):
            pchip = 2 * _flip(x, fx) + _flip(y, fy)
            for w in range(nw):
                landed = _full_view(full[w], lays[w], pchip, c)
                pltpu.make_async_remote_copy(
                    src_ref=landed, dst_ref=landed,
                    send_sem=send1.at[r, w], recv_sem=recv1.at[r, w],
                    device_id=(x, y, c), device_id_type=MESH).wait_recv()
                cp = pltpu.make_async_remote_copy(
                    src_ref=landed, dst_ref=landed,
                    send_sem=send2.at[r, w], recv_sem=recv2.at[r, w],
                    device_id=(x, y, 1 - c), device_id_type=MESH)
                cp.start()
                sends.append(cp)
        for r, (fx, fy) in enumerate(CHIP_RELS):
            pchip = 2 * _flip(x, fx) + _flip(y, fy)
            for w in range(nw):
                other = _full_view(full[w], lays[w], pchip, 1 - c)
                pltpu.make_async_remote_copy(
                    src_ref=other, dst_ref=other,
                    send_sem=send2.at[r, w], recv_sem=recv2.at[r, w],
                    device_id=(x, y, c), device_id_type=MESH).wait_recv()
        for cp in sends:
            cp.wait_send()
        for cp in local:
            cp.wait()

    return pl.pallas_call(
        body, name="gather_weights",
        out_shape=[jax.ShapeDtypeStruct(_full_shape(l), BF16) for l in lays],
        in_specs=[ANY] * nw, out_specs=[ANY] * nw,
        scratch_shapes=[pltpu.SemaphoreType.DMA((nw, 2)),
                        pltpu.SemaphoreType.DMA((3, nw)), pltpu.SemaphoreType.DMA((3, nw)),
                        pltpu.SemaphoreType.DMA((3, nw)), pltpu.SemaphoreType.DMA((3, nw))],
    )(*shards)


def _half_shape(lay):
    return (lay[0] // 2, lay[1])


def _pair_exchange(grads, d):
    lays = _wlayout(d)
    nw = len(lays)

    def body(*refs):
        g, got = refs[:nw], refs[nw:2 * nw]
        send, recv = refs[2 * nw:]
        x, y, c = _me()
        sends = []
        for w in range(nw):
            for k in range(N_CHIPS):
                cp = pltpu.make_async_remote_copy(
                    src_ref=_full_view(g[w], lays[w], k, 1 - c), dst_ref=got[w].at[k],
                    send_sem=send.at[w, k], recv_sem=recv.at[w, k],
                    device_id=(x, y, 1 - c), device_id_type=MESH)
                cp.start()
                sends.append(cp)
        for cp in sends:
            cp.wait_recv()
        for cp in sends:
            cp.wait_send()

    return pl.pallas_call(
        body, name="grad_pair_exchange",
        out_shape=[jax.ShapeDtypeStruct((N_CHIPS,) + _half_shape(l), F32) for l in lays],
        in_specs=[ANY] * nw, out_specs=[ANY] * nw,
        scratch_shapes=[pltpu.SemaphoreType.DMA((nw, N_CHIPS)), pltpu.SemaphoreType.DMA((nw, N_CHIPS))],
    )(*grads)


def _pair_sum(ids, g, got, lay, name):
    r, c, by_col = lay
    hr = r // 2
    tr = min(hr, 256)
    nt = hr // tr

    def body(ids_ref, g_ref, got_ref, s32_ref, s16_ref):
        s = g_ref[...] + got_ref[...]
        s32_ref[...] = s
        s16_ref[...] = s.astype(BF16)

    if by_col:
        gspec = pl.BlockSpec((tr, c), lambda k, t, ids: (ids[1] * nt + t, k))
    else:
        gspec = pl.BlockSpec((tr, c), lambda k, t, ids: ((2 * k + ids[1]) * nt + t, 0))
    hspec = pl.BlockSpec((None, tr, c), lambda k, t, ids: (k, t, 0))
    return pl.pallas_call(
        body, name=name,
        grid_spec=pltpu.PrefetchScalarGridSpec(
            num_scalar_prefetch=1, grid=(N_CHIPS, nt),
            in_specs=[gspec, hspec], out_specs=[hspec, hspec]),
        out_shape=[jax.ShapeDtypeStruct((N_CHIPS, hr, c), F32),
                   jax.ShapeDtypeStruct((N_CHIPS, hr, c), BF16)],
    )(ids, g, got)


def _chip_exchange(s16, d):
    lays = _wlayout(d)
    nw = len(lays)

    def body(*refs):
        s, got = refs[:nw], refs[nw:2 * nw]
        send, recv = refs[2 * nw:]
        x, y, c = _me()
        sends = []
        for r, (fx, fy) in enumerate(CHIP_RELS):
            px, py = _flip(x, fx), _flip(y, fy)
            for w in range(nw):
                cp = pltpu.make_async_remote_copy(
                    src_ref=s[w].at[2 * px + py], dst_ref=got[w].at[r],
                    send_sem=send.at[r, w], recv_sem=recv.at[r, w],
                    device_id=(px, py, c), device_id_type=MESH)
                cp.start()
                sends.append(cp)
        for cp in sends:
            cp.wait_recv()
        for cp in sends:
            cp.wait_send()

    return pl.pallas_call(
        body, name="grad_chip_exchange",
        out_shape=[jax.ShapeDtypeStruct((3,) + _half_shape(l), BF16) for l in lays],
        in_specs=[ANY] * nw, out_specs=[ANY] * nw,
        scratch_shapes=[pltpu.SemaphoreType.DMA((3, nw)), pltpu.SemaphoreType.DMA((3, nw))],
    )(*s16)


def _chip_sum(ids, s32, got, lay, name):
    hr, c = _half_shape(lay)
    tr = min(hr, 256)

    def body(ids_ref, s_ref, got_ref, out_ref):
        t = s_ref[...]
        for r in range(3):
            t = t + got_ref[r].astype(F32)
        out_ref[...] = t

    return pl.pallas_call(
        body, name=name,
        grid_spec=pltpu.PrefetchScalarGridSpec(
            num_scalar_prefetch=1, grid=(hr // tr,),
            in_specs=[pl.BlockSpec((None, tr, c), lambda t, ids: (ids[0], t, 0)),
                      pl.BlockSpec((3, tr, c), lambda t, ids: (0, t, 0))],
            out_specs=pl.BlockSpec((tr, c), lambda t, ids: (t, 0))),
        out_shape=jax.ShapeDtypeStruct((hr, c), F32),
    )(ids, s32, got)


def _pair_share(halves, small, d):
    lays = _wlayout(d)
    nw = len(lays)
    ns = len(small)

    def body(*refs):
        hv = refs[:nw]
        sm = refs[nw:nw + ns]
        out = refs[nw + ns:2 * nw + ns]
        smo = refs[2 * nw + ns:2 * nw + 2 * ns]
        loc, send, recv, ssend, srecv = refs[2 * nw + 2 * ns:]
        x, y, c = _me()
        dev = 4 * x + 2 * y + c
        sends, local = [], []
        for w in range(nw):
            cp = pltpu.make_async_copy(hv[w], _half_view(out[w], lays[w], c), loc.at[w])
            cp.start()
            local.append(cp)
            cp = pltpu.make_async_remote_copy(
                src_ref=hv[w], dst_ref=_half_view(out[w], lays[w], c),
                send_sem=send.at[w], recv_sem=recv.at[w],
                device_id=(x, y, 1 - c), device_id_type=MESH)
            cp.start()
            sends.append(cp)
        for a in range(ns):
            cp = pltpu.make_async_copy(sm[a], smo[a].at[dev], loc.at[nw + a])
            cp.start()
            local.append(cp)
        for r in range(1, N_DEV):
            fx, fy, fc = (r >> 2) & 1, (r >> 1) & 1, r & 1
            for a in range(ns):
                cp = pltpu.make_async_remote_copy(
                    src_ref=sm[a], dst_ref=smo[a].at[dev],
                    send_sem=ssend.at[r - 1, a], recv_sem=srecv.at[r - 1, a],
                    device_id=(_flip(x, fx), _flip(y, fy), _flip(c, fc)), device_id_type=MESH)
                cp.start()
                sends.append(cp)
        for w in range(nw):
            pltpu.make_async_remote_copy(
                src_ref=hv[w], dst_ref=_half_view(out[w], lays[w], 1 - c),
                send_sem=send.at[w], recv_sem=recv.at[w],
                device_id=(x, y, c), device_id_type=MESH).wait_recv()
        for r in range(1, N_DEV):
            fx, fy, fc = (r >> 2) & 1, (r >> 1) & 1, r & 1
            pdev = 4 * _flip(x, fx) + 2 * _flip(y, fy) + _flip(c, fc)
            for a in range(ns):
                pltpu.make_async_remote_copy(
                    src_ref=sm[a], dst_ref=smo[a].at[pdev],
                    send_sem=ssend.at[r - 1, a], recv_sem=srecv.at[r - 1, a],
                    device_id=(x, y, c), device_id_type=MESH).wait_recv()
        for cp in sends:
            cp.wait_send()
        for cp in local:
            cp.wait()

    return pl.pallas_call(
        body, name="grad_pair_share",
        out_shape=[jax.ShapeDtypeStruct(l[:2], F32) for l in lays]
        + [jax.ShapeDtypeStruct((N_DEV,) + a.shape, a.dtype) for a in small],
        in_specs=[ANY] * (nw + ns), out_specs=[ANY] * (nw + ns),
        scratch_shapes=[pltpu.SemaphoreType.DMA((nw + ns,)),
                        pltpu.SemaphoreType.DMA((nw,)), pltpu.SemaphoreType.DMA((nw,)),
                        pltpu.SemaphoreType.DMA((N_DEV - 1, ns)), pltpu.SemaphoreType.DMA((N_DEV - 1, ns))],
    )(*halves, *small)


def _cast_shards(ws):
    def body(*refs):
        n = len(refs) // 2
        for i in range(n):
            refs[n + i][...] = refs[i][...].astype(BF16)

    return pl.pallas_call(
        body, name="cast_shards",
        out_shape=[jax.ShapeDtypeStruct(w.shape, BF16) for w in ws],
        in_specs=[VMEM] * len(ws), out_specs=[VMEM] * len(ws),
    )(*ws)


def _ada_mod(c_all, w_ada, b_ada):
    def body(c_ref, w_ref, b_ref, act_ref, mod_ref):
        cv = c_ref[...]
        act = cv * _sigmoid(cv)
        act_ref[...] = act
        mod_ref[...] = jnp.dot(act, w_ref[...], preferred_element_type=F32, precision=HIGHEST) + b_ref[...]

    nb = c_all.shape[0]
    return pl.pallas_call(
        body, name="ada_mod",
        out_shape=[jax.ShapeDtypeStruct(c_all.shape, F32),
                   jax.ShapeDtypeStruct((nb, w_ada.shape[1]), F32)],
        in_specs=[VMEM] * 3, out_specs=[VMEM] * 2,
    )(c_all, w_ada, b_ada)


def _in_proj(x, mod3, w_in, t):
    n, d = x.shape
    d_in = w_in.shape[1]
    tm = min(TM_MM, t)
    tps = t // tm

    def body(x_ref, mod_ref, w_ref, proj_ref, h_ref):
        xn, _ = _rms(x_ref[...])
        h = (xn * (1.0 + mod_ref[:, d:2 * d]) + mod_ref[:, 0:d]).astype(BF16)
        h_ref[...] = h
        proj_ref[...] = _dot(h, w_ref[...])

    return pl.pallas_call(
        body, name="in_proj", grid=(n // tm,),
        in_specs=[_rowblk(tm, d), _modspec(tps, 6 * d), _resident((d, d_in))],
        out_specs=[_rowblk(tm, d_in), _rowblk(tm, d)],
        out_shape=[jax.ShapeDtypeStruct((n, d_in), F32), jax.ShapeDtypeStruct((n, d), BF16)],
        compiler_params=pltpu.CompilerParams(dimension_semantics=("parallel",)),
    )(x, mod3, w_in)


def _shifted_copies(ext_ref, e_ref, rows):
    for r in range(8):
        e_ref[r, 0:rows, :] = ext_ref[r:r + rows, :]


def _mixer_fwd(proj, wb, dwb, lng, lnb, scw, t):
    n, d_in = proj.shape
    dc = d_in // 5
    tm = min(TM_MIX, t)
    tps = t // tm
    hb = tm // HALO
    rb = min(RB_CONV, tm)

    def body(p_ref, ph_ref, wb_ref, dwb_ref, lng_ref, lnb_ref, scw_ref,
             mixed_ref, a1_ref, q_ref, ext_ref, e_ref, extp_ref):
        first = pl.program_id(0) % tps == 0
        keep = jnp.where(first, 0.0, 1.0)
        a0 = p_ref[:, 0:dc] * _sigmoid(p_ref[:, dc:2 * dc])
        a0h = ph_ref[:, 0:dc] * _sigmoid(ph_ref[:, dc:2 * dc]) * keep
        ext_ref[0:HALO, :] = a0h
        ext_ref[HALO:HALO + tm, :] = a0
        ext_ref[HALO + tm:HALO + tm + 8, :] = jnp.zeros((8, dc), F32)
        _shifted_copies(ext_ref, e_ref, tm + HALO)
        for g in range(dc // 128):
            lanes = slice(128 * g, 128 * g + 128)

            def rows(j, carry, lanes=lanes):
                i0 = pl.multiple_of(j * rb, rb)
                acc = jnp.zeros((rb, 128), F32)
                for k in range(CONF_K):
                    m, r = divmod(k + HALO - CONF_K + 1, 8)
                    acc = acc + e_ref[r, pl.ds(i0 + 8 * m, rb), lanes] * wb_ref[k:k + 1, lanes]
                a1_ref[pl.ds(i0, rb), lanes] = acc + dwb_ref[:, lanes]
                return carry

            lax.fori_loop(0, tm // rb, rows, 0)
        a1 = a1_ref[...]
        mu = jnp.mean(a1, axis=-1, keepdims=True)
        ac = a1 - mu
        rstd = lax.rsqrt(jnp.mean(ac * ac, axis=-1, keepdims=True) + EPS)
        a2 = ac * rstd * lng_ref[...] + lnb_ref[...]
        mixed_ref[:, 0:dc] = (a2 * _sigmoid(a2)).astype(BF16)
        p = p_ref[:, 3 * dc:4 * dc] * p_ref[:, 4 * dc:5 * dc]
        ph = ph_ref[HALO - 8:HALO, 3 * dc:4 * dc] * ph_ref[HALO - 8:HALO, 4 * dc:5 * dc] * keep
        extp_ref[0:8, :] = ph
        extp_ref[8:8 + tm, :] = p
        q = jnp.zeros((tm, dc), F32)
        for k in range(SC_K):
            q = q + extp_ref[6 + k:6 + k + tm, :] * scw_ref[k:k + 1, :]
        q_ref[...] = q
        mixed_ref[:, dc:2 * dc] = (p_ref[:, 2 * dc:3 * dc] * q).astype(BF16)

    return pl.pallas_call(
        body, name="mixer_fwd", grid=(n // tm,),
        in_specs=[_rowblk(tm, d_in),
                  pl.BlockSpec((HALO, d_in), lambda i: (jnp.maximum(i * hb - 1, 0), 0)),
                  _const(wb.shape), _const(dwb.shape), _const(lng.shape), _const(lnb.shape), _const(scw.shape)],
        out_specs=[_rowblk(tm, 2 * dc), _rowblk(tm, dc), _rowblk(tm, dc)],
        out_shape=[jax.ShapeDtypeStruct((n, 2 * dc), BF16), jax.ShapeDtypeStruct((n, dc), F32),
                   jax.ShapeDtypeStruct((n, dc), F32)],
        scratch_shapes=[pltpu.VMEM((tm + HALO + 8, dc), F32), pltpu.VMEM((8, tm + HALO, dc), F32),
                        pltpu.VMEM((tm + 8, dc), F32)],
        compiler_params=pltpu.CompilerParams(dimension_semantics=("parallel",)),
    )(proj, proj, wb, dwb, lng, lnb, scw)


def _out_proj(mixed, w_out, x, mod3, t):
    n, d = x.shape
    tm = min(TM_MM, t)
    tps = t // tm

    def body(m_ref, w_ref, x_ref, mod_ref, x1_ref, y1_ref, h2_ref):
        y1 = _dot(m_ref[...], w_ref[...])
        y1_ref[...] = y1
        x1 = x_ref[...] + mod_ref[:, 2 * d:3 * d] * y1
        x1_ref[...] = x1
        xn, _ = _rms(x1)
        h2_ref[...] = (xn * (1.0 + mod_ref[:, 4 * d:5 * d]) + mod_ref[:, 3 * d:4 * d]).astype(BF16)

    return pl.pallas_call(
        body, name="out_proj", grid=(n // tm,),
        in_specs=[_rowblk(tm, d), _resident((d, d)), _rowblk(tm, d), _modspec(tps, 6 * d)],
        out_specs=[_rowblk(tm, d), _rowblk(tm, d), _rowblk(tm, d)],
        out_shape=[jax.ShapeDtypeStruct((n, d), F32), jax.ShapeDtypeStruct((n, d), F32),
                   jax.ShapeDtypeStruct((n, d), BF16)],
        compiler_params=pltpu.CompilerParams(dimension_semantics=("parallel",)),
    )(mixed, w_out, x, mod3)


def _mlp_fwd(h2, x1, tgt, mod3, gfin, w1, w2, t):
    n, d = x1.shape
    dff = w1.shape[1]
    tm = min(TM_MLP, t)
    tps = t // tm
    nt = n // tm

    def body(h_ref, x1_ref, tg_ref, mod_ref, gf_ref, w1_ref, w2_ref,
             z_ref, dx2_ref, dy2_ref, dg2_ref, ggf_ref, loss_ref):
        i = pl.program_id(0)
        z = _dot(h_ref[...], w1_ref[...])
        z_ref[...] = z.astype(BF16)
        zr = jnp.maximum(z, 0.0)
        y2 = _dot((zr * zr).astype(BF16), w2_ref[...])
        g2 = mod_ref[:, 5 * d:6 * d]
        x2n, r3 = _rms(x1_ref[...] + g2 * y2)
        gf = gf_ref[...]
        diff = x2n * gf - tg_ref[...]
        dout = diff * (1.0 / d)
        dx2 = _rms_bwd(dout * gf, x2n, r3)
        dx2_ref[...] = dx2
        dy2_ref[...] = (g2 * dx2).astype(BF16)
        _acc_add(dg2_ref, i % tps == 0, _rows8(dx2 * y2))
        _acc_add(ggf_ref, i == 0, _rows8(dout * x2n))
        _acc_add(loss_ref, i == 0, _rows8(diff * diff))

    return pl.pallas_call(
        body, name="mlp_fwd", grid=(nt,),
        in_specs=[_rowblk(tm, d), _rowblk(tm, d), _rowblk(tm, d), _modspec(tps, 6 * d), _const((1, d)),
                  _resident((d, dff)), _resident((dff, d))],
        out_specs=[_rowblk(tm, dff), _rowblk(tm, d), _rowblk(tm, d), _accspec(tps, d),
                   _const((8, d)), _const((8, d))],
        out_shape=[jax.ShapeDtypeStruct((n, dff), BF16), jax.ShapeDtypeStruct((n, d), F32),
                   jax.ShapeDtypeStruct((n, d), BF16), jax.ShapeDtypeStruct((n // t, 8, d), F32),
                   jax.ShapeDtypeStruct((8, d), F32), jax.ShapeDtypeStruct((8, d), F32)],
        compiler_params=pltpu.CompilerParams(dimension_semantics=("arbitrary",)),
    )(h2, x1, tgt, mod3, gfin, w1, w2)


def _mlp_bwd(dy2, z, x1, dx2, y1, mod3, w1, w2, t):
    n, d = x1.shape
    dff = w1.shape[1]
    tm = min(TM_MLP, t)
    tps = t // tm

    def body(dy2_ref, z_ref, x1_ref, dx2_ref, y1_ref, mod_ref, w1_ref, w2_ref,
             dz_ref, dx1_ref, dy1_ref, dsh_ref, dsc_ref, dg1_ref):
        first = pl.program_id(0) % tps == 0
        du = _dot_nt(dy2_ref[...], w2_ref[...])
        dz = (du * (2.0 * jnp.maximum(z_ref[...].astype(F32), 0.0))).astype(BF16)
        dz_ref[...] = dz
        dh2 = _dot_nt(dz, w1_ref[...])
        x1n, r2 = _rms(x1_ref[...])
        _acc_add(dsh_ref, first, _rows8(dh2))
        _acc_add(dsc_ref, first, _rows8(dh2 * x1n))
        dx1 = dx2_ref[...] + _rms_bwd(dh2 * (1.0 + mod_ref[:, 4 * d:5 * d]), x1n, r2)
        dx1_ref[...] = dx1
        dy1_ref[...] = (mod_ref[:, 2 * d:3 * d] * dx1).astype(BF16)
        _acc_add(dg1_ref, first, _rows8(dx1 * y1_ref[...]))

    acc = jax.ShapeDtypeStruct((n // t, 8, d), F32)
    return pl.pallas_call(
        body, name="mlp_bwd", grid=(n // tm,),
        in_specs=[_rowblk(tm, d), _rowblk(tm, dff), _rowblk(tm, d), _rowblk(tm, d), _rowblk(tm, d),
                  _modspec(tps, 6 * d), _resident((d, dff)), _resident((dff, d))],
        out_specs=[_rowblk(tm, dff), _rowblk(tm, d), _rowblk(tm, d),
                   _accspec(tps, d), _accspec(tps, d), _accspec(tps, d)],
        out_shape=[jax.ShapeDtypeStruct((n, dff), BF16), jax.ShapeDtypeStruct((n, d), F32),
                   jax.ShapeDtypeStruct((n, d), BF16), acc, acc, acc],
        compiler_params=pltpu.CompilerParams(dimension_semantics=("arbitrary",)),
    )(dy2, z, x1, dx2, y1, mod3, w1, w2)


def _wgrad(a, b, name, relu2=False, bn=None):
    n, ka = a.shape
    nb = b.shape[1]
    tk = min(TK_WG, n)
    bm = min(ka, 1024)
    bn = nb if bn is None else bn

    def body(a_ref, b_ref, out_ref):
        av = a_ref[...]
        if relu2:
            ar = jnp.maximum(av, 0.0)
            av = ar * ar
        p = _dot_tn(av, b_ref[...])
        _acc_add(out_ref, pl.program_id(2) == 0, p)

    return pl.pallas_call(
        body, name=name, grid=(ka // bm, nb // bn, n // tk),
        in_specs=[pl.BlockSpec((tk, bm), lambda i, j, k: (k, i)),
                  pl.BlockSpec((tk, bn), lambda i, j, k: (k, j))],
        out_specs=pl.BlockSpec((bm, bn), lambda i, j, k: (i, j)),
        out_shape=jax.ShapeDtypeStruct((ka, nb), F32),
        compiler_params=pltpu.CompilerParams(dimension_semantics=("parallel", "parallel", "arbitrary")),
    )(a, b)


def _out_proj_bwd(dy1, w_out, t):
    n, d = dy1.shape
    tm = min(TM_MM, t)

    def body(dy_ref, w_ref, dm_ref):
        dm_ref[...] = _dot_nt(dy_ref[...], w_ref[...])

    return pl.pallas_call(
        body, name="out_proj_bwd", grid=(n // tm,),
        in_specs=[_rowblk(tm, d), _resident(w_out.shape)],
        out_specs=_rowblk(tm, w_out.shape[0]),
        out_shape=jax.ShapeDtypeStruct((n, w_out.shape[0]), F32),
        compiler_params=pltpu.CompilerParams(dimension_semantics=("parallel",)),
    )(dy1, w_out)


SG_DWW = 0
SG_DWB = CONF_K
SG_LNG = CONF_K + 1
SG_LNB = CONF_K + 2
SG_SCW = CONF_K + 3
SG_N = CONF_K + 3 + SC_K


def _mixer_bwd(dmixed, a1, q, proj, wb, lng, lnb, scw, t):
    n, d_in = proj.shape
    dc = d_in // 5
    tm = min(TM_MIX, t)
    tps = t // tm
    hb = tm // HALO
    nh = n // HALO
    rb = min(RB_CONV, tm)
    rw = min(RB_WG, tm)

    def ln_bwd(a1v, da3, lng_v, lnb_v):
        mu = jnp.mean(a1v, axis=-1, keepdims=True)
        ac = a1v - mu
        rstd = lax.rsqrt(jnp.mean(ac * ac, axis=-1, keepdims=True) + EPS)
        ah = ac * rstd
        a2 = ah * lng_v + lnb_v
        s2 = _sigmoid(a2)
        da2 = da3 * (s2 * (1.0 + a2 * (1.0 - s2)))
        dah = da2 * lng_v
        da1 = rstd * (dah - jnp.mean(dah, axis=-1, keepdims=True)
                      - ah * jnp.mean(dah * ah, axis=-1, keepdims=True))
        return da1, da2, ah

    def body(dm_ref, dmn_ref, a1_ref, a1n_ref, q_ref, p_ref, pp_ref, pn_ref,
             wb_ref, lng_ref, lnb_ref, scw_ref, dproj_ref, sg_ref,
             exta_ref, ea_ref, extd_ref, ed_ref, da1_ref, da0_ref, extq_ref, extp_ref):
        i = pl.program_id(0)
        first = i % tps == 0
        last = i % tps == tps - 1
        keep_prev = jnp.where(first, 0.0, 1.0)
        keep_next = jnp.where(last, 0.0, 1.0)

        @pl.when(i == 0)
        def _():
            sg_ref[...] = jnp.zeros(sg_ref.shape, F32)

        lng_v, lnb_v = lng_ref[...], lnb_ref[...]
        val, sig = p_ref[:, 0:dc], _sigmoid(p_ref[:, dc:2 * dc])
        a0 = val * sig
        da1, da2, ah = ln_bwd(a1_ref[...], dm_ref[:, 0:dc], lng_v, lnb_v)
        da1n, _, _ = ln_bwd(a1n_ref[...], dmn_ref[:, 0:dc], lng_v, lnb_v)
        sg_ref[8 * SG_LNG:8 * SG_LNG + 8, :] += _rows8(da2 * ah)
        sg_ref[8 * SG_LNB:8 * SG_LNB + 8, :] += _rows8(da2)
        sg_ref[8 * SG_DWB:8 * SG_DWB + 8, :] += _rows8(da1)
        da1_ref[...] = da1
        extd_ref[0:tm, :] = da1
        extd_ref[tm:tm + HALO, :] = da1n * keep_next
        extd_ref[tm + HALO:tm + HALO + 8, :] = jnp.zeros((8, dc), F32)
        _shifted_copies(extd_ref, ed_ref, tm + HALO)
        exta_ref[0:HALO, :] = pp_ref[:, 0:dc] * _sigmoid(pp_ref[:, dc:2 * dc]) * keep_prev
        exta_ref[HALO:HALO + tm, :] = a0
        exta_ref[HALO + tm:HALO + tm + 8, :] = jnp.zeros((8, dc), F32)
        _shifted_copies(exta_ref, ea_ref, tm + HALO)
        for g in range(dc // 128):
            lanes = slice(128 * g, 128 * g + 128)

            def rows(j, carry, lanes=lanes):
                i0 = pl.multiple_of(j * rb, rb)
                acc = jnp.zeros((rb, 128), F32)
                for k in range(CONF_K):
                    m, r = divmod(CONF_K - 1 - k, 8)
                    acc = acc + ed_ref[r, pl.ds(i0 + 8 * m, rb), lanes] * wb_ref[k:k + 1, lanes]
                da0_ref[pl.ds(i0, rb), lanes] = acc
                return carry

            lax.fori_loop(0, tm // rb, rows, 0)

            def wrows(j, accs, lanes=lanes):
                i0 = pl.multiple_of(j * rw, rw)
                dv = da1_ref[pl.ds(i0, rw), lanes]
                new = []
                for k in range(CONF_K):
                    m, r = divmod(k + HALO - CONF_K + 1, 8)
                    new.append(accs[k] + _rows8(ea_ref[r, pl.ds(i0 + 8 * m, rw), lanes] * dv))
                return tuple(new)

            accs = lax.fori_loop(0, tm // rw, wrows,
                                 tuple(jnp.zeros((8, 128), F32) for _ in range(CONF_K)))
            for k in range(CONF_K):
                sg_ref[8 * (SG_DWW + k):8 * (SG_DWW + k) + 8, lanes] += accs[k]
        da0 = da0_ref[...]
        dproj_ref[:, 0:dc] = (da0 * sig).astype(BF16)
        dproj_ref[:, dc:2 * dc] = (da0 * a0 * (1.0 - sig)).astype(BF16)
        ds = dm_ref[:, dc:2 * dc]
        scb, scc, sch = p_ref[:, 2 * dc:3 * dc], p_ref[:, 3 * dc:4 * dc], p_ref[:, 4 * dc:5 * dc]
        dproj_ref[:, 2 * dc:3 * dc] = (ds * q_ref[...]).astype(BF16)
        dq = ds * scb
        extq_ref[0:tm, :] = dq
        extq_ref[tm:tm + 8, :] = dmn_ref[0:8, dc:2 * dc] * pn_ref[0:8, :] * keep_next
        dp = jnp.zeros((tm, dc), F32)
        for k in range(SC_K):
            dp = dp + extq_ref[SC_K - 1 - k:SC_K - 1 - k + tm, :] * scw_ref[k:k + 1, :]
        dproj_ref[:, 3 * dc:4 * dc] = (dp * sch).astype(BF16)
        dproj_ref[:, 4 * dc:5 * dc] = (dp * scc).astype(BF16)
        extp_ref[0:8, :] = pp_ref[HALO - 8:HALO, 3 * dc:4 * dc] * pp_ref[HALO - 8:HALO, 4 * dc:5 * dc] * keep_prev
        extp_ref[8:8 + tm, :] = scc * sch
        for k in range(SC_K):
            sg_ref[8 * (SG_SCW + k):8 * (SG_SCW + k) + 8, :] += _rows8(extp_ref[6 + k:6 + k + tm, :] * dq)

    nxt = lambda i: (jnp.minimum((i + 1) * hb, nh - 1), 0)
    prv = lambda i: (jnp.maximum(i * hb - 1, 0), 0)
    return pl.pallas_call(
        body, name="mixer_bwd", grid=(n // tm,),
        in_specs=[_rowblk(tm, 2 * dc), pl.BlockSpec((HALO, 2 * dc), nxt),
                  _rowblk(tm, dc), pl.BlockSpec((HALO, dc), nxt),
                  _rowblk(tm, dc),
                  _rowblk(tm, d_in), pl.BlockSpec((HALO, d_in), prv),
                  pl.BlockSpec((HALO, dc), lambda i: (jnp.minimum((i + 1) * hb, nh - 1), 2)),
                  _const(wb.shape), _const(lng.shape), _const(lnb.shape), _const(scw.shape)],
        out_specs=[_rowblk(tm, d_in), _const((8 * SG_N, dc))],
        out_shape=[jax.ShapeDtypeStruct((n, d_in), BF16), jax.ShapeDtypeStruct((8 * SG_N, dc), F32)],
        scratch_shapes=[pltpu.VMEM((tm + HALO + 8, dc), F32), pltpu.VMEM((8, tm + HALO, dc), F32),
                        pltpu.VMEM((tm + HALO + 8, dc), F32), pltpu.VMEM((8, tm + HALO, dc), F32),
                        pltpu.VMEM((tm, dc), F32), pltpu.VMEM((tm, dc), F32),
                        pltpu.VMEM((tm + 8, dc), F32), pltpu.VMEM((tm + 8, dc), F32)],
        compiler_params=pltpu.CompilerParams(dimension_semantics=("arbitrary",)),
    )(dmixed, dmixed, a1, a1, q, proj, proj, proj, wb, lng, lnb, scw)


def _in_proj_bwd(dproj, w_in, x, dx1, mod3, t):
    n, d = x.shape
    d_in = w_in.shape[1]
    tm = min(TM_MM, t)
    tps = t // tm

    def body(dp_ref, w_ref, x_ref, dx1_ref, mod_ref, gx_ref, dsh_ref, dsc_ref):
        first = pl.program_id(0) % tps == 0
        dh1 = _dot_nt(dp_ref[...], w_ref[...])
        xn, r1 = _rms(x_ref[...])
        _acc_add(dsh_ref, first, _rows8(dh1))
        _acc_add(dsc_ref, first, _rows8(dh1 * xn))
        gx_ref[...] = dx1_ref[...] + _rms_bwd(dh1 * (1.0 + mod_ref[:, d:2 * d]), xn, r1)

    acc = jax.ShapeDtypeStruct((n // t, 8, d), F32)
    return pl.pallas_call(
        body, name="in_proj_bwd", grid=(n // tm,),
        in_specs=[_rowblk(tm, d_in), _resident((d, d_in)), _rowblk(tm, d), _rowblk(tm, d),
                  _modspec(tps, 6 * d)],
        out_specs=[_rowblk(tm, d), _accspec(tps, d), _accspec(tps, d)],
        out_shape=[jax.ShapeDtypeStruct((n, d), F32), acc, acc],
        compiler_params=pltpu.CompilerParams(dimension_semantics=("arbitrary",)),
    )(dproj, w_in, x, dx1, mod3)


SMALL_ROWS = 40


def _pack_small(sg, ggf, loss, accs, d):
    dc = d // 2
    nb = accs[0].shape[0]

    def body(sg_ref, ggf_ref, loss_ref, dsh1, dsc1, dg1, dsh2, dsc2, dg2, pack_ref, dmod_ref):
        pack_ref[...] = jnp.zeros(pack_ref.shape, F32)
        for k in range(SG_N):
            pack_ref[k:k + 1, :] = jnp.sum(sg_ref[8 * k:8 * k + 8, :], axis=0, keepdims=True)
        gf = jnp.sum(ggf_ref[...], axis=0, keepdims=True)
        pack_ref[SG_N:SG_N + 1, :] = gf[:, 0:dc]
        pack_ref[SG_N + 1:SG_N + 2, :] = gf[:, dc:d]
        tot = jnp.sum(jnp.sum(loss_ref[...], axis=0, keepdims=True), axis=1, keepdims=True) * (0.5 / d)
        pack_ref[SG_N + 2:SG_N + 3, :] = jnp.broadcast_to(tot, (1, dc))
        dmod_ref[...] = jnp.zeros(dmod_ref.shape, F32)
        for j, ref in enumerate((dsh1, dsc1, dg1, dsh2, dsc2, dg2)):
            for b in range(nb):
                dmod_ref[b:b + 1, j * d:(j + 1) * d] = jnp.sum(ref[b], axis=0, keepdims=True)

    return pl.pallas_call(
        body, name="pack_small",
        out_shape=[jax.ShapeDtypeStruct((SMALL_ROWS, dc), F32), jax.ShapeDtypeStruct((8, 6 * d), F32)],
        in_specs=[VMEM] * 9, out_specs=[VMEM] * 2,
    )(sg, ggf, loss, *accs)


def _small_reduce(pack_all, dmod_all, nb):
    def body(pk_ref, dm_ref, red_ref, dmod_ref, gb_ref):
        tot = pk_ref[0]
        for dev in range(1, N_DEV):
            tot = tot + pk_ref[dev]
        red_ref[...] = tot
        gb = jnp.zeros((1, dm_ref.shape[2]), F32)
        for dev in range(N_DEV):
            for b in range(nb):
                row = dm_ref[dev, b:b + 1, :]
                dmod_ref[dev * nb + b:dev * nb + b + 1, :] = row
                gb = gb + row
        gb_ref[...] = gb

    return pl.pallas_call(
        body, name="small_reduce",
        out_shape=[jax.ShapeDtypeStruct(pack_all.shape[1:], F32),
                   jax.ShapeDtypeStruct((N_DEV * nb, dmod_all.shape[2]), F32),
                   jax.ShapeDtypeStruct((1, dmod_all.shape[2]), F32)],
        in_specs=[VMEM] * 2, out_specs=[VMEM] * 3,
    )(pack_all, dmod_all)


def _adam(w, g, m, v):
    m = ADAM_B1 * m + (1.0 - ADAM_B1) * g
    v = ADAM_B2 * v + (1.0 - ADAM_B2) * (g * g)
    m_hat = m / (1.0 - ADAM_B1 ** ADAM_STEP)
    v_hat = v / (1.0 - ADAM_B2 ** ADAM_STEP)
    delta = -ADAM_LR * (m_hat / (jnp.sqrt(v_hat) + ADAM_EPS) + ADAM_WD * w)
    return delta, m, v


def _adamw_big(w, g, m, v, name):
    r, c = w.shape
    tr = min(r, 256)

    def body(w_ref, g_ref, m_ref, v_ref, d_ref, nm_ref, nv_ref):
        d_ref[...], nm_ref[...], nv_ref[...] = _adam(w_ref[...], g_ref[...], m_ref[...], v_ref[...])

    s = jax.ShapeDtypeStruct((r, c), F32)
    return pl.pallas_call(
        body, name=name, grid=(r // tr,),
        in_specs=[_rowblk(tr, c)] * 4, out_specs=[_rowblk(tr, c)] * 3, out_shape=[s, s, s],
        compiler_params=pltpu.CompilerParams(dimension_semantics=("parallel",)),
    )(w, g, m, v)


def _adamw_ada(act_t, dmod_cols, w, m, v):
    r, c = w.shape
    tr = min(r, 256)
    nb = act_t.shape[1]

    def body(a_ref, dm_ref, w_ref, m_ref, v_ref, g_ref, d_ref, nm_ref, nv_ref):
        g = jnp.dot(a_ref[...], dm_ref[...], preferred_element_type=F32, precision=HIGHEST)
        g_ref[...] = g
        d_ref[...], nm_ref[...], nv_ref[...] = _adam(w_ref[...], g, m_ref[...], v_ref[...])

    s = jax.ShapeDtypeStruct((r, c), F32)
    return pl.pallas_call(
        body, name="adamw_w_ada", grid=(r // tr,),
        in_specs=[_rowblk(tr, nb), _const((nb, c))] + [_rowblk(tr, c)] * 3,
        out_specs=[_rowblk(tr, c)] * 4, out_shape=[s, s, s, s],
        compiler_params=pltpu.CompilerParams(dimension_semantics=("parallel",)),
    )(act_t, dmod_cols, w, m, v)


def _adamw_small(ws, gs, ms, vs):
    n = len(ws)

    def body(*refs):
        for i in range(n):
            w, g, m, v = (refs[j * n + i][...] for j in range(4))
            dl, nm, nv = _adam(w, g, m, v)
            refs[4 * n + i][...] = dl
            refs[5 * n + i][...] = nm
            refs[6 * n + i][...] = nv

    shapes = [jax.ShapeDtypeStruct(w.shape, F32) for w in ws]
    return pl.pallas_call(
        body, name="adamw_small", out_shape=shapes * 3,
        in_specs=[VMEM] * (4 * n), out_specs=[VMEM] * (3 * n),
    )(*ws, *gs, *ms, *vs)


def kernel(x, c, w_ada, b_ada, w_in, conf_dw_w, conf_dw_b, conf_ln_g, conf_ln_b, sc_conv_w, w_out, w_mlp1, w_mlp2, g_final, loss_target, m_w_ada, m_b_ada, m_w_in, m_conf_dw_w, m_conf_dw_b, m_conf_ln_g, m_conf_ln_b, m_sc_conv_w, m_w_out, m_w_mlp1, m_w_mlp2, m_g_final, v_w_ada, v_b_ada, v_w_in, v_conf_dw_w, v_conf_dw_b, v_conf_ln_g, v_conf_ln_b, v_sc_conv_w, v_w_out, v_w_mlp1, v_w_mlp2, v_g_final):
    nb, t, d = x.shape
    n = nb * t
    dc = d // 2
    ada_w = w_ada.shape[2]
    ax, ay, ac = _me()
    chip = 2 * ax + ay
    dev = 2 * chip + ac
    ids = jnp.stack([chip, ac]).astype(jnp.int32)

    c_pad = jnp.zeros((8, d), F32).at[0:nb].set(c)
    cw_pad = jnp.zeros((SMALL_ROWS, dc // N_CHIPS), F32)
    cw_pad = cw_pad.at[0:CONF_K].set(conf_dw_w[0]).at[HALO:HALO + SC_K].set(sc_conv_w[0])
    c_all8, cw_all8 = _all_gather8([c_pad, cw_pad], "gather_c")
    c_all = c_all8[:, 0:nb].reshape(N_DEV * nb, d)
    cw_full = jnp.concatenate([cw_all8[2 * k] for k in range(N_CHIPS)], axis=1)
    dww, scw = cw_full[0:CONF_K], cw_full[HALO:HALO + SC_K]
    b_cols = lax.dynamic_slice(b_ada, (0, chip * ada_w), (1, ada_w))
    c_act, mod_shard = _ada_mod(c_all, w_ada[0], b_cols)
    mod_all = _gather_mod(mod_shard)
    mod3 = lax.dynamic_slice(mod_all, (dev * nb, 0), (nb, 6 * d)).reshape(nb, 1, 6 * d)
    shards = _cast_shards([w_in[0], w_out[0], w_mlp1[0], w_mlp2[0]])
    wf_in, wf_out, wf_1, wf_2 = _gather_weights(shards, d)

    x2 = x.reshape(n, d)
    tgt = loss_target.reshape(n, d)
    proj, h1 = _in_proj(x2, mod3, wf_in, t)
    mixed, a1, q = _mixer_fwd(proj, dww, conf_dw_b, conf_ln_g, conf_ln_b, scw, t)
    x1, y1, h2 = _out_proj(mixed, wf_out, x2, mod3, t)
    z, dx2, dy2, dg2, ggf, loss_p = _mlp_fwd(h2, x1, tgt, mod3, g_final.reshape(1, d), wf_1, wf_2, t)

    dz, dx1, dy1, dsh2, dsc2, dg1 = _mlp_bwd(dy2, z, x1, dx2, y1, mod3, wf_1, wf_2, t)
    g_w2 = _wgrad(z, dy2, "wgrad_mlp2", relu2=True)
    g_w1 = _wgrad(h2, dz, "wgrad_mlp1")
    dmixed = _out_proj_bwd(dy1, wf_out, t)
    g_wout = _wgrad(mixed, dy1, "wgrad_out")
    dproj, sg = _mixer_bwd(dmixed, a1, q, proj, dww, conf_ln_g, conf_ln_b, scw, t)
    grad_x, dsh1, dsc1 = _in_proj_bwd(dproj, wf_in, x2, dx1, mod3, t)
    g_win = _wgrad(h1, dproj, "wgrad_in")

    lays = _wlayout(d)
    grads = [g_win, g_wout, g_w1, g_w2]
    got1 = _pair_exchange(grads, d)
    names = ("in", "out", "mlp1", "mlp2")
    sums = [_pair_sum(ids, grads[i], got1[i], lays[i], "pair_sum_" + names[i]) for i in range(4)]
    got2 = _chip_exchange([s[1] for s in sums], d)
    halves = [_chip_sum(ids, sums[i][0], got2[i], lays[i], "chip_sum_" + names[i]) for i in range(4)]
    pack, dmod8 = _pack_small(sg, ggf, loss_p, (dsh1, dsc1, dg1, dsh2, dsc2, dg2), d)
    gr_in, gr_out, gr_1, gr_2, pack_all, dmod_all8 = _pair_share(halves, [pack, dmod8], d)
    red, dmod_all, g_bada = _small_reduce(pack_all, dmod_all8, nb)

    dmod_cols = lax.dynamic_slice(dmod_all, (0, chip * ada_w), (N_DEV * nb, ada_w))
    g_wada, d_wada, nm_wada, nv_wada = _adamw_ada(c_act.T, dmod_cols, w_ada[0], m_w_ada[0], v_w_ada[0])
    big = {}
    for nm_, g_, w_, m_, v_ in (("w_in", gr_in, w_in, m_w_in, v_w_in), ("w_out", gr_out, w_out, m_w_out, v_w_out),
                                ("w_mlp1", gr_1, w_mlp1, m_w_mlp1, v_w_mlp1),
                                ("w_mlp2", gr_2, w_mlp2, m_w_mlp2, v_w_mlp2)):
        big[nm_] = _adamw_big(w_[0], g_, m_[0], v_[0], "adamw_" + nm_)
    cw = dc // N_CHIPS
    g_dww = lax.dynamic_slice(red[0:CONF_K], (0, chip * cw), (CONF_K, cw))
    g_scw = lax.dynamic_slice(red[SG_SCW:SG_SCW + SC_K], (0, chip * cw), (SC_K, cw))
    g_gfin = jnp.concatenate([red[SG_N:SG_N + 1], red[SG_N + 1:SG_N + 2]], axis=1)
    small_g = [g_bada, g_dww, red[SG_DWB:SG_DWB + 1], red[SG_LNG:SG_LNG + 1], red[SG_LNB:SG_LNB + 1], g_scw, g_gfin]
    small_w = [b_ada, conf_dw_w[0], conf_dw_b, conf_ln_g, conf_ln_b, sc_conv_w[0], g_final.reshape(1, d)]
    small_m = [m_b_ada, m_conf_dw_w[0], m_conf_dw_b, m_conf_ln_g, m_conf_ln_b, m_sc_conv_w[0], m_g_final.reshape(1, d)]
    small_v = [v_b_ada, v_conf_dw_w[0], v_conf_dw_b, v_conf_ln_g, v_conf_ln_b, v_sc_conv_w[0], v_g_final.reshape(1, d)]
    upd = _adamw_small(small_w, small_g, small_m, small_v)
    ns = len(small_w)
    s_delta, s_m, s_v = upd[0:ns], upd[ns:2 * ns], upd[2 * ns:3 * ns]

    loss = red[SG_N + 2, 0]

    def outs(kind_big, kind_small, wada):
        sm = kind_small
        return (wada[None], sm[0], kind_big["w_in"][None], sm[1][None], sm[2], sm[3], sm[4], sm[5][None],
                kind_big["w_out"][None], kind_big["w_mlp1"][None], kind_big["w_mlp2"][None], sm[6].reshape(d))

    grads_out = outs({"w_in": gr_in, "w_out": gr_out, "w_mlp1": gr_1, "w_mlp2": gr_2}, small_g, g_wada)
    delta_out = outs({k: v[0] for k, v in big.items()}, s_delta, d_wada)
    m_out = outs({k: v[1] for k, v in big.items()}, s_m, nm_wada)
    v_out = outs({k: v[2] for k, v in big.items()}, s_v, nv_wada)
    return (loss, grad_x.reshape(nb, t, d), *grads_out, *delta_out, *m_out, *v_out)
```

```python
import functools

import jax
import jax.numpy as jnp
from jax import lax
from jax.experimental import pallas as pl
from jax.experimental.pallas import tpu as pltpu

F32 = jnp.float32
BF16 = jnp.bfloat16
MESH = pl.DeviceIdType.MESH
HIGHEST = lax.Precision.HIGHEST

EPS = 1e-6
CONF_K = 31
SC_K = 3
HALO = 32
N_CHIPS = 4
N_DEV = 8

ADAM_LR = 0.001
ADAM_B1 = 0.9
ADAM_B2 = 0.999
ADAM_EPS = 1e-08
ADAM_WD = 0.01
ADAM_STEP = 10

TM_MM = 512
TM_MIX = 256
TM_MLP = 256
TK_WG = 512
RB_CONV = 64
RB_WG = 32
CHIP_RELS = ((1, 0), (0, 1), (1, 1))

ANY = pl.BlockSpec(memory_space=pl.ANY)
VMEM = pl.BlockSpec(memory_space=pltpu.VMEM)
HBM = pl.BlockSpec(memory_space=pltpu.HBM)
SEM = pl.BlockSpec(memory_space=pltpu.SEMAPHORE)
EFFECT = pltpu.SideEffectType.DATAFLOW_SIDE_EFFECTING


def _me():
    return lax.axis_index("x"), lax.axis_index("y"), lax.axis_index("c")


def _flip(v, f):
    return 1 - v if f else v


def _rows8(v):
    r, c = v.shape
    return v.reshape(r // 8, 8, c).sum(axis=0)


def _rms(x):
    r = lax.rsqrt(jnp.mean(x * x, axis=-1, keepdims=True) + EPS)
    return x * r, r


def _rms_bwd(dxn, xn, r):
    return r * (dxn - xn * jnp.mean(dxn * xn, axis=-1, keepdims=True))


def _sigmoid(x):
    return 1.0 / (1.0 + jnp.exp(-x))


def _dot(a, b):
    return jnp.dot(a, b, preferred_element_type=F32)


def _dot_nt(a, b):
    return lax.dot_general(a, b, (((1,), (1,)), ((), ())), preferred_element_type=F32)


def _dot_tn(a, b):
    return lax.dot_general(a, b, (((0,), (0,)), ((), ())), preferred_element_type=F32)


def _const(shape):
    nd = len(shape)
    return pl.BlockSpec(shape, lambda i: (0,) * nd)


def _resident(shape):
    nd = len(shape)
    return pl.BlockSpec(shape, lambda i: (0,) * nd, pipeline_mode=pl.Buffered(1))


def _rowblk(tm, c):
    return pl.BlockSpec((tm, c), lambda i: (i, 0))


def _modspec(tps, width):
    return pl.BlockSpec((None, 1, width), lambda i: (i // tps, 0, 0))


def _accspec(tps, c):
    return pl.BlockSpec((None, 8, c), lambda i: (i // tps, 0, 0))


def _acc_add(ref, first, v):
    @pl.when(first)
    def _():
        ref[...] = v

    @pl.when(jnp.logical_not(first))
    def _():
        ref[...] += v


def _all_gather8(arrs, name):
    n = len(arrs)

    def body(*refs):
        ins, outs = refs[:n], refs[n:2 * n]
        send, recv = refs[2 * n:]
        x, y, c = _me()
        dev = 4 * x + 2 * y + c
        for a in range(n):
            outs[a][dev] = ins[a][...]
        sends = []
        for r in range(1, N_DEV):
            fx, fy, fc = (r >> 2) & 1, (r >> 1) & 1, r & 1
            peer = (_flip(x, fx), _flip(y, fy), _flip(c, fc))
            for a in range(n):
                cp = pltpu.make_async_remote_copy(
                    src_ref=ins[a], dst_ref=outs[a].at[dev],
                    send_sem=send.at[r - 1, a], recv_sem=recv.at[r - 1, a],
                    device_id=peer, device_id_type=MESH)
                cp.start()
                sends.append(cp)
        for r in range(1, N_DEV):
            fx, fy, fc = (r >> 2) & 1, (r >> 1) & 1, r & 1
            pdev = 4 * _flip(x, fx) + 2 * _flip(y, fy) + _flip(c, fc)
            for a in range(n):
                pltpu.make_async_remote_copy(
                    src_ref=ins[a], dst_ref=outs[a].at[pdev],
                    send_sem=send.at[r - 1, a], recv_sem=recv.at[r - 1, a],
                    device_id=(x, y, c), device_id_type=MESH).wait_recv()
        for cp in sends:
            cp.wait_send()

    return pl.pallas_call(
        body, name=name,
        out_shape=[jax.ShapeDtypeStruct((N_DEV,) + a.shape, a.dtype) for a in arrs],
        in_specs=[VMEM] * n, out_specs=[VMEM] * n,
        scratch_shapes=[pltpu.SemaphoreType.DMA((N_DEV - 1, n)),
                        pltpu.SemaphoreType.DMA((N_DEV - 1, n))],
    )(*arrs)


def _gather_mod(mod_shard):
    nb, w = mod_shard.shape

    def body(in_ref, out_ref, send, recv):
        x, y, c = _me()
        chip = 2 * x + y
        out_ref[:, pl.ds(pl.multiple_of(chip * w, 128), w)] = in_ref[...]
        sends = []
        for r, (fx, fy) in enumerate(CHIP_RELS):
            cp = pltpu.make_async_remote_copy(
                src_ref=in_ref,
                dst_ref=out_ref.at[:, pl.ds(pl.multiple_of(chip * w, 128), w)],
                send_sem=send.at[r], recv_sem=recv.at[r],
                device_id=(_flip(x, fx), _flip(y, fy), c), device_id_type=MESH)
            cp.start()
            sends.append(cp)
        for r, (fx, fy) in enumerate(CHIP_RELS):
            pchip = 2 * _flip(x, fx) + _flip(y, fy)
            pltpu.make_async_remote_copy(
                src_ref=in_ref,
                dst_ref=out_ref.at[:, pl.ds(pl.multiple_of(pchip * w, 128), w)],
                send_sem=send.at[r], recv_sem=recv.at[r],
                device_id=(x, y, c), device_id_type=MESH).wait_recv()
        for cp in sends:
            cp.wait_send()

    return pl.pallas_call(
        body, name="gather_mod",
        out_shape=jax.ShapeDtypeStruct((nb, N_CHIPS * w), mod_shard.dtype),
        in_specs=[VMEM], out_specs=VMEM,
        scratch_shapes=[pltpu.SemaphoreType.DMA((3,)), pltpu.SemaphoreType.DMA((3,))],
    )(mod_shard)


def _wlayout(d):
    d_in = 5 * d // 2
    return (
        (d, d_in // N_CHIPS, True),
        (d // N_CHIPS, d, False),
        (d, 4 * d // N_CHIPS, True),
        (4 * d // N_CHIPS, d, False),
    )


def _full_shape(lay):
    r, c, by_col = lay
    return (r, c * N_CHIPS) if by_col else (r * N_CHIPS, c)


def _full_view(ref, lay, k, h):
    r, c, by_col = lay
    hr = r // 2
    if by_col:
        return ref.at[pl.ds(pl.multiple_of(h * hr, 16), hr), pl.ds(pl.multiple_of(k * c, 128), c)]
    return ref.at[pl.ds(pl.multiple_of(k * r + h * hr, 16), hr), :]


def _half_view(ref, lay, h):
    hr = lay[0] // 2
    return ref.at[pl.ds(pl.multiple_of(h * hr, 16), hr), :]


def _half_shape(lay):
    return (lay[0] // 2, lay[1])


def _hbm(a):
    return pltpu.with_memory_space_constraint(a, pltpu.HBM)


def _remote(src, dst, send, recv, idx, to):
    return lambda: pltpu.make_async_remote_copy(src_ref=src, dst_ref=dst, send_sem=send.at[idx], recv_sem=recv.at[idx],
                                                device_id=to, device_id_type=MESH)


def _copy_start(name, bufs, n_sems, plan):
    nb = len(bufs)

    def body(*refs):
        sends, _ = plan(refs[:nb], refs[nb], refs[nb + 1])
        for mk in sends:
            mk().start()
        refs[-1][...] = jnp.zeros((8, 128), F32)

    outs = pl.pallas_call(
        body, name=name,
        out_shape=(pltpu.SemaphoreType.DMA((n_sems,)), pltpu.SemaphoreType.DMA((n_sems,)))
        + tuple(pltpu.HBM(b.shape, b.dtype) for b in bufs) + (jax.ShapeDtypeStruct((8, 128), F32),),
        in_specs=(HBM,) * nb, out_specs=(SEM, SEM) + (HBM,) * nb + (VMEM,),
        input_output_aliases={i: 2 + i for i in range(nb)},
        compiler_params=pltpu.CompilerParams(has_side_effects=EFFECT),
    )(*[_hbm(b) for b in bufs])
    return (outs[0], outs[1]), list(outs[2:2 + nb]), outs[-1]


def _copy_wait(name, bufs, sems, plan, after):
    nb, na = len(bufs), len(after)

    def body(*refs):
        sends, recvs = plan(refs[:nb], refs[nb], refs[nb + 1])
        for mk in sends:
            mk().wait_send()
        for mk in recvs:
            mk().wait_recv()

    outs = pl.pallas_call(
        body, name=name,
        out_shape=tuple(pltpu.HBM(b.shape, b.dtype) for b in bufs),
        in_specs=(HBM,) * nb + (SEM, SEM) + (ANY,) * na, out_specs=(HBM,) * nb,
        input_output_aliases={i: i for i in range(nb)},
        compiler_params=pltpu.CompilerParams(has_side_effects=EFFECT),
    )(*bufs, *sems, *after)
    return list(outs)


def _copy_blocking(name, bufs, n_sems, plan):
    nb = len(bufs)

    def body(*refs):
        sends, recvs = plan(refs[:nb], refs[2 * nb], refs[2 * nb + 1])
        started = [mk() for mk in sends]
        for cp in started:
            cp.start()
        for mk in recvs:
            mk().wait_recv()
        for cp in started:
            cp.wait_send()

    return list(pl.pallas_call(
        body, name=name,
        out_shape=tuple(jax.ShapeDtypeStruct(b.shape, b.dtype) for b in bufs),
        in_specs=(ANY,) * nb, out_specs=(ANY,) * nb,
        input_output_aliases={i: i for i in range(nb)},
        scratch_shapes=[pltpu.SemaphoreType.DMA((n_sems,)), pltpu.SemaphoreType.DMA((n_sems,))],
    )(*bufs))


def _exchange(name, bufs, n_sems, plan, between):
    if between is None:
        return _copy_blocking(name, bufs, n_sems, plan)
    sems, bufs, tok = _copy_start(name + "_start", bufs, n_sems, plan)
    return _copy_wait(name + "_wait", bufs, sems, plan, between(tok))


def _gather_direct_plan(lays):
    def plan(full, send, recv):
        x, y, c = _me()
        chip = 2 * x + y
        sends, recvs = [], []
        for r, (fx, fy) in enumerate(CHIP_RELS):
            px, py = _flip(x, fx), _flip(y, fy)
            for i, lay in enumerate(lays):
                for q in range(2):
                    oc = _flip(c, q)
                    mine = _full_view(full[i], lay, chip, c)
                    idx = (r * len(lays) + i) * 2 + q
                    sends.append(_remote(mine, mine, send, recv, idx, (px, py, oc)))
                    theirs = _full_view(full[i], lay, 2 * px + py, oc)
                    recvs.append(_remote(theirs, theirs, send, recv, idx, (x, y, c)))
        return sends, recvs
    return plan


def _gather_chip_plan(lays):
    def plan(full, send, recv):
        x, y, c = _me()
        chip = 2 * x + y
        sends, recvs = [], []
        for r, (fx, fy) in enumerate(CHIP_RELS):
            px, py = _flip(x, fx), _flip(y, fy)
            for i, lay in enumerate(lays):
                mine = _full_view(full[i], lay, chip, c)
                sends.append(_remote(mine, mine, send, recv, r * len(lays) + i, (px, py, c)))
                theirs = _full_view(full[i], lay, 2 * px + py, c)
                recvs.append(_remote(theirs, theirs, send, recv, r * len(lays) + i, (x, y, c)))
        return sends, recvs
    return plan


def _gather_pass_plan(lays):
    def plan(full, send, recv):
        x, y, c = _me()
        sends, recvs = [], []
        for r, (fx, fy) in enumerate(CHIP_RELS):
            pchip = 2 * _flip(x, fx) + _flip(y, fy)
            for i, lay in enumerate(lays):
                landed = _full_view(full[i], lay, pchip, c)
                sends.append(_remote(landed, landed, send, recv, r * len(lays) + i, (x, y, 1 - c)))
                other = _full_view(full[i], lay, pchip, 1 - c)
                recvs.append(_remote(other, other, send, recv, r * len(lays) + i, (x, y, c)))
        return sends, recvs
    return plan


def _pair_exchange_plan(lays):
    nw = len(lays)

    def plan(bufs, send, recv):
        x, y, c = _me()
        sends, recvs = [], []
        for i, lay in enumerate(lays):
            for k in range(N_CHIPS):
                sends.append(_remote(_full_view(bufs[i], lay, k, 1 - c), bufs[nw + i].at[k],
                                     send, recv, i * N_CHIPS + k, (x, y, 1 - c)))
                recvs.append(_remote(_full_view(bufs[i], lay, k, c), bufs[nw + i].at[k],
                                     send, recv, i * N_CHIPS + k, (x, y, c)))
        return sends, recvs
    return plan


def _chip_exchange_plan(nw):
    def plan(bufs, send, recv):
        x, y, c = _me()
        sends, recvs = [], []
        for r, (fx, fy) in enumerate(CHIP_RELS):
            px, py = _flip(x, fx), _flip(y, fy)
            for i in range(nw):
                sends.append(_remote(bufs[i].at[2 * px + py], bufs[nw + i].at[r], send, recv, r * nw + i, (px, py, c)))
                recvs.append(_remote(bufs[i].at[2 * px + py], bufs[nw + i].at[r], send, recv, r * nw + i, (x, y, c)))
        return sends, recvs
    return plan


def _pair_share_plan(lays):
    def plan(bufs, send, recv):
        x, y, c = _me()
        sends, recvs = [], []
        for i, lay in enumerate(lays):
            mine = _half_view(bufs[i], lay, c)
            sends.append(_remote(mine, mine, send, recv, i, (x, y, 1 - c)))
            other = _half_view(bufs[i], lay, 1 - c)
            recvs.append(_remote(other, other, send, recv, i, (x, y, c)))
        return sends, recvs
    return plan


def _pair_sum(ids, g, got, lay, name):
    r, c, by_col = lay
    hr = r // 2
    tr = min(hr, 256)
    nt = hr // tr

    def body(ids_ref, g_ref, got_ref, s32_ref, s16_ref):
        s = g_ref[...] + got_ref[...]
        s32_ref[...] = s
        s16_ref[...] = s.astype(BF16)

    if by_col:
        gspec = pl.BlockSpec((tr, c), lambda k, t, ids: (ids[1] * nt + t, k))
    else:
        gspec = pl.BlockSpec((tr, c), lambda k, t, ids: ((2 * k + ids[1]) * nt + t, 0))
    hspec = pl.BlockSpec((None, tr, c), lambda k, t, ids: (k, t, 0))
    return pl.pallas_call(
        body, name=name,
        grid_spec=pltpu.PrefetchScalarGridSpec(
            num_scalar_prefetch=1, grid=(N_CHIPS, nt),
            in_specs=[gspec, hspec], out_specs=[hspec, hspec]),
        out_shape=[jax.ShapeDtypeStruct((N_CHIPS, hr, c), F32),
                   jax.ShapeDtypeStruct((N_CHIPS, hr, c), BF16)],
    )(ids, g, got)


def _chip_sum(ids, s32, got, lay, name):
    hr, c = _half_shape(lay)
    tr = min(hr, 256)
    nt = hr // tr

    def body(ids_ref, s_ref, got_ref, out_ref):
        t = s_ref[...]
        for r in range(3):
            t = t + got_ref[r].astype(F32)
        out_ref[...] = t

    return pl.pallas_call(
        body, name=name,
        grid_spec=pltpu.PrefetchScalarGridSpec(
            num_scalar_prefetch=1, grid=(nt,),
            in_specs=[pl.BlockSpec((None, tr, c), lambda t, ids: (ids[0], t, 0)),
                      pl.BlockSpec((3, tr, c), lambda t, ids: (0, t, 0))],
            out_specs=pl.BlockSpec((tr, c), lambda t, ids: (ids[1] * nt + t, 0))),
        out_shape=jax.ShapeDtypeStruct((2 * hr, c), F32),
    )(ids, s32, got)


def _reduce_scatter(ids, grads, lays, names, tag, between):
    nw = len(lays)
    got1 = [lax.empty((N_CHIPS,) + _half_shape(l), F32) for l in lays]
    bufs = _exchange("pair_exchange_" + tag, list(grads) + got1, nw * N_CHIPS, _pair_exchange_plan(lays), between[0])
    sums = [_pair_sum(ids, bufs[i], bufs[nw + i], lays[i], "pair_sum_" + names[i]) for i in range(nw)]
    got2 = [lax.empty((3,) + _half_shape(l), BF16) for l in lays]
    bufs = _exchange("chip_exchange_" + tag, [s[1] for s in sums] + got2, 3 * nw, _chip_exchange_plan(nw), between[1])
    mine = [_chip_sum(ids, sums[i][0], bufs[nw + i], lays[i], "chip_sum_" + names[i]) for i in range(nw)]
    return _exchange("pair_share_" + tag, mine, nw, _pair_share_plan(lays), between[2])


def _cast_place(ids, w, lay, name):
    r, c, by_col = lay

    def body(ids_ref, w_ref, out_ref):
        out_ref[...] = w_ref[...].astype(BF16)

    omap = (lambda i, ids: (0, ids[0])) if by_col else (lambda i, ids: (ids[0], 0))
    return pl.pallas_call(
        body, name=name,
        grid_spec=pltpu.PrefetchScalarGridSpec(
            num_scalar_prefetch=1, grid=(1,),
            in_specs=[pl.BlockSpec((r, c), lambda i, ids: (0, 0))],
            out_specs=pl.BlockSpec((r, c), omap)),
        out_shape=jax.ShapeDtypeStruct(_full_shape(lay), BF16),
    )(ids, w)


def _ada_mod(c_all, w_ada, b_ada):
    def body(c_ref, w_ref, b_ref, act_ref, mod_ref):
        cv = c_ref[...]
        act = cv * _sigmoid(cv)
        act_ref[...] = act
        mod_ref[...] = jnp.dot(act, w_ref[...], preferred_element_type=F32, precision=HIGHEST) + b_ref[...]

    nb = c_all.shape[0]
    return pl.pallas_call(
        body, name="ada_mod",
        out_shape=[jax.ShapeDtypeStruct(c_all.shape, F32),
                   jax.ShapeDtypeStruct((nb, w_ada.shape[1]), F32)],
        in_specs=[VMEM] * 3, out_specs=[VMEM] * 2,
    )(c_all, w_ada, b_ada)


def _in_proj(x, mod3, w_in, t):
    n, d = x.shape
    d_in = w_in.shape[1]
    tm = min(TM_MM, t)
    tps = t // tm

    def body(x_ref, mod_ref, w_ref, proj_ref, h_ref):
        xn, _ = _rms(x_ref[...])
        h = (xn * (1.0 + mod_ref[:, d:2 * d]) + mod_ref[:, 0:d]).astype(BF16)
        h_ref[...] = h
        proj_ref[...] = _dot(h, w_ref[...])

    return pl.pallas_call(
        body, name="in_proj", grid=(n // tm,),
        in_specs=[_rowblk(tm, d), _modspec(tps, 6 * d), _resident((d, d_in))],
        out_specs=[_rowblk(tm, d_in), _rowblk(tm, d)],
        out_shape=[jax.ShapeDtypeStruct((n, d_in), F32), jax.ShapeDtypeStruct((n, d), BF16)],
        compiler_params=pltpu.CompilerParams(dimension_semantics=("parallel",)),
    )(x, mod3, w_in)


def _shifted_copies(ext_ref, e_ref, rows):
    for r in range(8):
        e_ref[r, 0:rows, :] = ext_ref[r:r + rows, :]


def _mixer_fwd(proj, wb, dwb, lng, lnb, scw, t):
    n, d_in = proj.shape
    dc = d_in // 5
    tm = min(TM_MIX, t)
    tps = t // tm
    hb = tm // HALO
    rb = min(RB_CONV, tm)

    def body(p_ref, ph_ref, wb_ref, dwb_ref, lng_ref, lnb_ref, scw_ref,
             mixed_ref, a1_ref, q_ref, ext_ref, e_ref, extp_ref):
        first = pl.program_id(0) % tps == 0
        keep = jnp.where(first, 0.0, 1.0)
        a0 = p_ref[:, 0:dc] * _sigmoid(p_ref[:, dc:2 * dc])
        a0h = ph_ref[:, 0:dc] * _sigmoid(ph_ref[:, dc:2 * dc]) * keep
        ext_ref[0:HALO, :] = a0h
        ext_ref[HALO:HALO + tm, :] = a0
        ext_ref[HALO + tm:HALO + tm + 8, :] = jnp.zeros((8, dc), F32)
        _shifted_copies(ext_ref, e_ref, tm + HALO)
        for g in range(dc // 128):
            lanes = slice(128 * g, 128 * g + 128)

            def rows(j, carry, lanes=lanes):
                i0 = pl.multiple_of(j * rb, rb)
                acc = jnp.zeros((rb, 128), F32)
                for k in range(CONF_K):
                    m, r = divmod(k + HALO - CONF_K + 1, 8)
                    acc = acc + e_ref[r, pl.ds(i0 + 8 * m, rb), lanes] * wb_ref[k:k + 1, lanes]
                a1_ref[pl.ds(i0, rb), lanes] = acc + dwb_ref[:, lanes]
                return carry

            lax.fori_loop(0, tm // rb, rows, 0)
        a1 = a1_ref[...]
        mu = jnp.mean(a1, axis=-1, keepdims=True)
        ac = a1 - mu
        rstd = lax.rsqrt(jnp.mean(ac * ac, axis=-1, keepdims=True) + EPS)
        a2 = ac * rstd * lng_ref[...] + lnb_ref[...]
        mixed_ref[:, 0:dc] = (a2 * _sigmoid(a2)).astype(BF16)
        p = p_ref[:, 3 * dc:4 * dc] * p_ref[:, 4 * dc:5 * dc]
        ph = ph_ref[HALO - 8:HALO, 3 * dc:4 * dc] * ph_ref[HALO - 8:HALO, 4 * dc:5 * dc] * keep
        extp_ref[0:8, :] = ph
        extp_ref[8:8 + tm, :] = p
        q = jnp.zeros((tm, dc), F32)
        for k in range(SC_K):
            q = q + extp_ref[6 + k:6 + k + tm, :] * scw_ref[k:k + 1, :]
        q_ref[...] = q
        mixed_ref[:, dc:2 * dc] = (p_ref[:, 2 * dc:3 * dc] * q).astype(BF16)

    return pl.pallas_call(
        body, name="mixer_fwd", grid=(n // tm,),
        in_specs=[_rowblk(tm, d_in),
                  pl.BlockSpec((HALO, d_in), lambda i: (jnp.maximum(i * hb - 1, 0), 0)),
                  _const(wb.shape), _const(dwb.shape), _const(lng.shape), _const(lnb.shape), _const(scw.shape)],
        out_specs=[_rowblk(tm, 2 * dc), _rowblk(tm, dc), _rowblk(tm, dc)],
        out_shape=[jax.ShapeDtypeStruct((n, 2 * dc), BF16), jax.ShapeDtypeStruct((n, dc), F32),
                   jax.ShapeDtypeStruct((n, dc), F32)],
        scratch_shapes=[pltpu.VMEM((tm + HALO + 8, dc), F32), pltpu.VMEM((8, tm + HALO, dc), F32),
                        pltpu.VMEM((tm + 8, dc), F32)],
        compiler_params=pltpu.CompilerParams(dimension_semantics=("parallel",)),
    )(proj, proj, wb, dwb, lng, lnb, scw)


def _out_proj(mixed, w_out, x, mod3, t, dep):
    n, d = x.shape
    tm = min(TM_MM, t)
    tps = t // tm

    def body(m_ref, w_ref, x_ref, mod_ref, dep_ref, x1_ref, y1_ref, h2_ref):
        y1 = _dot(m_ref[...], w_ref[...])
        y1_ref[...] = y1
        x1 = x_ref[...] + mod_ref[:, 2 * d:3 * d] * y1
        x1_ref[...] = x1
        xn, _ = _rms(x1)
        h2_ref[...] = (xn * (1.0 + mod_ref[:, 4 * d:5 * d]) + mod_ref[:, 3 * d:4 * d]).astype(BF16)

    return pl.pallas_call(
        body, name="out_proj", grid=(n // tm,),
        in_specs=[_rowblk(tm, d), _resident((d, d)), _rowblk(tm, d), _modspec(tps, 6 * d), ANY],
        out_specs=[_rowblk(tm, d), _rowblk(tm, d), _rowblk(tm, d)],
        out_shape=[jax.ShapeDtypeStruct((n, d), F32), jax.ShapeDtypeStruct((n, d), F32),
                   jax.ShapeDtypeStruct((n, d), BF16)],
        compiler_params=pltpu.CompilerParams(dimension_semantics=("parallel",)),
    )(mixed, w_out, x, mod3, dep)


def _mlp_fwd(h2, x1, tgt, mod3, gfin, w1, w2, t):
    n, d = x1.shape
    dff = w1.shape[1]
    tm = min(TM_MLP, t)
    tps = t // tm
    nt = n // tm

    def body(h_ref, x1_ref, tg_ref, mod_ref, gf_ref, w1_ref, w2_ref,
             z_ref, dx2_ref, dy2_ref, dg2_ref, ggf_ref, loss_ref):
        i = pl.program_id(0)
        z = _dot(h_ref[...], w1_ref[...])
        z_ref[...] = z.astype(BF16)
        zr = jnp.maximum(z, 0.0)
        y2 = _dot((zr * zr).astype(BF16), w2_ref[...])
        g2 = mod_ref[:, 5 * d:6 * d]
        x2n, r3 = _rms(x1_ref[...] + g2 * y2)
        gf = gf_ref[...]
        diff = x2n * gf - tg_ref[...]
        dout = diff * (1.0 / d)
        dx2 = _rms_bwd(dout * gf, x2n, r3)
        dx2_ref[...] = dx2
        dy2_ref[...] = (g2 * dx2).astype(BF16)
        _acc_add(dg2_ref, i % tps == 0, _rows8(dx2 * y2))
        _acc_add(ggf_ref, i == 0, _rows8(dout * x2n))
        _acc_add(loss_ref, i == 0, _rows8(diff * diff))

    return pl.pallas_call(
        body, name="mlp_fwd", grid=(nt,),
        in_specs=[_rowblk(tm, d), _rowblk(tm, d), _rowblk(tm, d), _modspec(tps, 6 * d), _const((1, d)),
                  _resident((d, dff)), _resident((dff, d))],
        out_specs=[_rowblk(tm, dff), _rowblk(tm, d), _rowblk(tm, d), _accspec(tps, d),
                   _const((8, d)), _const((8, d))],
        out_shape=[jax.ShapeDtypeStruct((n, dff), BF16), jax.ShapeDtypeStruct((n, d), F32),
                   jax.ShapeDtypeStruct((n, d), BF16), jax.ShapeDtypeStruct((n // t, 8, d), F32),
                   jax.ShapeDtypeStruct((8, d), F32), jax.ShapeDtypeStruct((8, d), F32)],
        compiler_params=pltpu.CompilerParams(dimension_semantics=("arbitrary",)),
    )(h2, x1, tgt, mod3, gfin, w1, w2)


def _mlp_bwd(dy2, z, x1, dx2, y1, mod3, w1, w2, t):
    n, d = x1.shape
    dff = w1.shape[1]
    tm = min(TM_MLP, t)
    tps = t // tm

    def body(dy2_ref, z_ref, x1_ref, dx2_ref, y1_ref, mod_ref, w1_ref, w2_ref,
             dz_ref, dx1_ref, dy1_ref, dsh_ref, dsc_ref, dg1_ref):
        first = pl.program_id(0) % tps == 0
        du = _dot_nt(dy2_ref[...], w2_ref[...])
        dz = (du * (2.0 * jnp.maximum(z_ref[...].astype(F32), 0.0))).astype(BF16)
        dz_ref[...] = dz
        dh2 = _dot_nt(dz, w1_ref[...])
        x1n, r2 = _rms(x1_ref[...])
        _acc_add(dsh_ref, first, _rows8(dh2))
        _acc_add(dsc_ref, first, _rows8(dh2 * x1n))
        dx1 = dx2_ref[...] + _rms_bwd(dh2 * (1.0 + mod_ref[:, 4 * d:5 * d]), x1n, r2)
        dx1_ref[...] = dx1
        dy1_ref[...] = (mod_ref[:, 2 * d:3 * d] * dx1).astype(BF16)
        _acc_add(dg1_ref, first, _rows8(dx1 * y1_ref[...]))

    acc = jax.ShapeDtypeStruct((n // t, 8, d), F32)
    return pl.pallas_call(
        body, name="mlp_bwd", grid=(n // tm,),
        in_specs=[_rowblk(tm, d), _rowblk(tm, dff), _rowblk(tm, d), _rowblk(tm, d), _rowblk(tm, d),
                  _modspec(tps, 6 * d), _resident((d, dff)), _resident((dff, d))],
        out_specs=[_rowblk(tm, dff), _rowblk(tm, d), _rowblk(tm, d),
                   _accspec(tps, d), _accspec(tps, d), _accspec(tps, d)],
        out_shape=[jax.ShapeDtypeStruct((n, dff), BF16), jax.ShapeDtypeStruct((n, d), F32),
                   jax.ShapeDtypeStruct((n, d), BF16), acc, acc, acc],
        compiler_params=pltpu.CompilerParams(dimension_semantics=("arbitrary",)),
    )(dy2, z, x1, dx2, y1, mod3, w1, w2)


def _wgrad(a, b, name, relu2=False, bn=None):
    n, ka = a.shape
    nb = b.shape[1]
    tk = min(TK_WG, n)
    bm = min(ka, 1024)
    if bn is None:
        bn = nb if nb <= 2048 else nb // 2

    def body(a_ref, b_ref, out_ref):
        av = a_ref[...]
        if relu2:
            ar = jnp.maximum(av, 0.0)
            av = ar * ar
        p = _dot_tn(av, b_ref[...])
        _acc_add(out_ref, pl.program_id(2) == 0, p)

    return pl.pallas_call(
        body, name=name, grid=(ka // bm, nb // bn, n // tk),
        in_specs=[pl.BlockSpec((tk, bm), lambda i, j, k: (k, i)),
                  pl.BlockSpec((tk, bn), lambda i, j, k: (k, j))],
        out_specs=pl.BlockSpec((bm, bn), lambda i, j, k: (i, j)),
        out_shape=jax.ShapeDtypeStruct((ka, nb), F32),
        compiler_params=pltpu.CompilerParams(dimension_semantics=("parallel", "parallel", "arbitrary")),
    )(a, b)


def _out_proj_bwd(dy1, w_out, t, dep):
    n, d = dy1.shape
    tm = min(TM_MM, t)

    def body(dy_ref, w_ref, dep_ref, dm_ref):
        dm_ref[...] = _dot_nt(dy_ref[...], w_ref[...])

    return pl.pallas_call(
        body, name="out_proj_bwd", grid=(n // tm,),
        in_specs=[_rowblk(tm, d), _resident(w_out.shape), ANY],
        out_specs=_rowblk(tm, w_out.shape[0]),
        out_shape=jax.ShapeDtypeStruct((n, w_out.shape[0]), F32),
        compiler_params=pltpu.CompilerParams(dimension_semantics=("parallel",)),
    )(dy1, w_out, dep)


SG_DWW = 0
SG_DWB = CONF_K
SG_LNG = CONF_K + 1
SG_LNB = CONF_K + 2
SG_SCW = CONF_K + 3
SG_N = CONF_K + 3 + SC_K


def _mixer_bwd(dmixed, a1, q, proj, wb, lng, lnb, scw, t, dep):
    n, d_in = proj.shape
    dc = d_in // 5
    tm = min(TM_MIX, t)
    tps = t // tm
    hb = tm // HALO
    nh = n // HALO
    rb = min(RB_CONV, tm)
    rw = min(RB_WG, tm)

    def ln_bwd(a1v, da3, lng_v, lnb_v):
        mu = jnp.mean(a1v, axis=-1, keepdims=True)
        ac = a1v - mu
        rstd = lax.rsqrt(jnp.mean(ac * ac, axis=-1, keepdims=True) + EPS)
        ah = ac * rstd
        a2 = ah * lng_v + lnb_v
        s2 = _sigmoid(a2)
        da2 = da3 * (s2 * (1.0 + a2 * (1.0 - s2)))
        dah = da2 * lng_v
        da1 = rstd * (dah - jnp.mean(dah, axis=-1, keepdims=True)
                      - ah * jnp.mean(dah * ah, axis=-1, keepdims=True))
        return da1, da2, ah

    def body(dm_ref, dmn_ref, a1_ref, a1n_ref, q_ref, p_ref, pp_ref, pn_ref,
             wb_ref, lng_ref, lnb_ref, scw_ref, dep_ref, dproj_ref, sg_ref,
             exta_ref, ea_ref, extd_ref, ed_ref, da1_ref, da0_ref, extq_ref, extp_ref):
        i = pl.program_id(0)
        first = i % tps == 0
        last = i % tps == tps - 1
        keep_prev = jnp.where(first, 0.0, 1.0)
        keep_next = jnp.where(last, 0.0, 1.0)

        @pl.when(i == 0)
        def _():
            sg_ref[...] = jnp.zeros(sg_ref.shape, F32)

        lng_v, lnb_v = lng_ref[...], lnb_ref[...]
        val, sig = p_ref[:, 0:dc], _sigmoid(p_ref[:, dc:2 * dc])
        a0 = val * sig
        da1, da2, ah = ln_bwd(a1_ref[...], dm_ref[:, 0:dc], lng_v, lnb_v)
        da1n, _, _ = ln_bwd(a1n_ref[...], dmn_ref[:, 0:dc], lng_v, lnb_v)
        sg_ref[8 * SG_LNG:8 * SG_LNG + 8, :] += _rows8(da2 * ah)
        sg_ref[8 * SG_LNB:8 * SG_LNB + 8, :] += _rows8(da2)
        sg_ref[8 * SG_DWB:8 * SG_DWB + 8, :] += _rows8(da1)
        da1_ref[...] = da1
        extd_ref[0:tm, :] = da1
        extd_ref[tm:tm + HALO, :] = da1n * keep_next
        extd_ref[tm + HALO:tm + HALO + 8, :] = jnp.zeros((8, dc), F32)
        _shifted_copies(extd_ref, ed_ref, tm + HALO)
        exta_ref[0:HALO, :] = pp_ref[:, 0:dc] * _sigmoid(pp_ref[:, dc:2 * dc]) * keep_prev
        exta_ref[HALO:HALO + tm, :] = a0
        exta_ref[HALO + tm:HALO + tm + 8, :] = jnp.zeros((8, dc), F32)
        _shifted_copies(exta_ref, ea_ref, tm + HALO)
        for g in range(dc // 128):
            lanes = slice(128 * g, 128 * g + 128)

            def rows(j, carry, lanes=lanes):
                i0 = pl.multiple_of(j * rb, rb)
                acc = jnp.zeros((rb, 128), F32)
                for k in range(CONF_K):
                    m, r = divmod(CONF_K - 1 - k, 8)
                    acc = acc + ed_ref[r, pl.ds(i0 + 8 * m, rb), lanes] * wb_ref[k:k + 1, lanes]
                da0_ref[pl.ds(i0, rb), lanes] = acc
                return carry

            lax.fori_loop(0, tm // rb, rows, 0)

            def wrows(j, accs, lanes=lanes):
                i0 = pl.multiple_of(j * rw, rw)
                dv = da1_ref[pl.ds(i0, rw), lanes]
                new = []
                for k in range(CONF_K):
                    m, r = divmod(k + HALO - CONF_K + 1, 8)
                    new.append(accs[k] + _rows8(ea_ref[r, pl.ds(i0 + 8 * m, rw), lanes] * dv))
                return tuple(new)

            accs = lax.fori_loop(0, tm // rw, wrows,
                                 tuple(jnp.zeros((8, 128), F32) for _ in range(CONF_K)))
            for k in range(CONF_K):
                sg_ref[8 * (SG_DWW + k):8 * (SG_DWW + k) + 8, lanes] += accs[k]
        da0 = da0_ref[...]
        dproj_ref[:, 0:dc] = (da0 * sig).astype(BF16)
        dproj_ref[:, dc:2 * dc] = (da0 * a0 * (1.0 - sig)).astype(BF16)
        ds = dm_ref[:, dc:2 * dc]
        scb, scc, sch = p_ref[:, 2 * dc:3 * dc], p_ref[:, 3 * dc:4 * dc], p_ref[:, 4 * dc:5 * dc]
        dproj_ref[:, 2 * dc:3 * dc] = (ds * q_ref[...]).astype(BF16)
        dq = ds * scb
        extq_ref[0:tm, :] = dq
        extq_ref[tm:tm + 8, :] = dmn_ref[0:8, dc:2 * dc] * pn_ref[0:8, :] * keep_next
        dp = jnp.zeros((tm, dc), F32)
        for k in range(SC_K):
            dp = dp + extq_ref[SC_K - 1 - k:SC_K - 1 - k + tm, :] * scw_ref[k:k + 1, :]
        dproj_ref[:, 3 * dc:4 * dc] = (dp * sch).astype(BF16)
        dproj_ref[:, 4 * dc:5 * dc] = (dp * scc).astype(BF16)
        extp_ref[0:8, :] = pp_ref[HALO - 8:HALO, 3 * dc:4 * dc] * pp_ref[HALO - 8:HALO, 4 * dc:5 * dc] * keep_prev
        extp_ref[8:8 + tm, :] = scc * sch
        for k in range(SC_K):
            sg_ref[8 * (SG_SCW + k):8 * (SG_SCW + k) + 8, :] += _rows8(extp_ref[6 + k:6 + k + tm, :] * dq)

    nxt = lambda i: (jnp.minimum((i + 1) * hb, nh - 1), 0)
    prv = lambda i: (jnp.maximum(i * hb - 1, 0), 0)
    return pl.pallas_call(
        body, name="mixer_bwd", grid=(n // tm,),
        in_specs=[_rowblk(tm, 2 * dc), pl.BlockSpec((HALO, 2 * dc), nxt),
                  _rowblk(tm, dc), pl.BlockSpec((HALO, dc), nxt),
                  _rowblk(tm, dc),
                  _rowblk(tm, d_in), pl.BlockSpec((HALO, d_in), prv),
                  pl.BlockSpec((HALO, dc), lambda i: (jnp.minimum((i + 1) * hb, nh - 1), 2)),
                  _const(wb.shape), _const(lng.shape), _const(lnb.shape), _const(scw.shape), ANY],
        out_specs=[_rowblk(tm, d_in), _const((8 * SG_N, dc))],
        out_shape=[jax.ShapeDtypeStruct((n, d_in), BF16), jax.ShapeDtypeStruct((8 * SG_N, dc), F32)],
        scratch_shapes=[pltpu.VMEM((tm + HALO + 8, dc), F32), pltpu.VMEM((8, tm + HALO, dc), F32),
                        pltpu.VMEM((tm + HALO + 8, dc), F32), pltpu.VMEM((8, tm + HALO, dc), F32),
                        pltpu.VMEM((tm, dc), F32), pltpu.VMEM((tm, dc), F32),
                        pltpu.VMEM((tm + 8, dc), F32), pltpu.VMEM((tm + 8, dc), F32)],
        compiler_params=pltpu.CompilerParams(dimension_semantics=("arbitrary",)),
    )(dmixed, dmixed, a1, a1, q, proj, proj, proj, wb, lng, lnb, scw, dep)


def _in_proj_bwd(dproj, w_in, x, dx1, mod3, t, dep):
    n, d = x.shape
    d_in = w_in.shape[1]
    tm = min(TM_MM, t)
    tps = t // tm

    def body(dp_ref, w_ref, x_ref, dx1_ref, mod_ref, dep_ref, gx_ref, dsh_ref, dsc_ref):
        first = pl.program_id(0) % tps == 0
        dh1 = _dot_nt(dp_ref[...], w_ref[...])
        xn, r1 = _rms(x_ref[...])
        _acc_add(dsh_ref, first, _rows8(dh1))
        _acc_add(dsc_ref, first, _rows8(dh1 * xn))
        gx_ref[...] = dx1_ref[...] + _rms_bwd(dh1 * (1.0 + mod_ref[:, d:2 * d]), xn, r1)

    acc = jax.ShapeDtypeStruct((n // t, 8, d), F32)
    return pl.pallas_call(
        body, name="in_proj_bwd", grid=(n // tm,),
        in_specs=[_rowblk(tm, d_in), _resident((d, d_in)), _rowblk(tm, d), _rowblk(tm, d),
                  _modspec(tps, 6 * d), ANY],
        out_specs=[_rowblk(tm, d), _accspec(tps, d), _accspec(tps, d)],
        out_shape=[jax.ShapeDtypeStruct((n, d), F32), acc, acc],
        compiler_params=pltpu.CompilerParams(dimension_semantics=("arbitrary",)),
    )(dproj, w_in, x, dx1, mod3, dep)


SMALL_ROWS = 40


def _pack_small(sg, ggf, loss, accs, d):
    dc = d // 2
    nb = accs[0].shape[0]

    def body(sg_ref, ggf_ref, loss_ref, dsh1, dsc1, dg1, dsh2, dsc2, dg2, pack_ref, dmod_ref):
        pack_ref[...] = jnp.zeros(pack_ref.shape, F32)
        for k in range(SG_N):
            pack_ref[k:k + 1, :] = jnp.sum(sg_ref[8 * k:8 * k + 8, :], axis=0, keepdims=True)
        gf = jnp.sum(ggf_ref[...], axis=0, keepdims=True)
        pack_ref[SG_N:SG_N + 1, :] = gf[:, 0:dc]
        pack_ref[SG_N + 1:SG_N + 2, :] = gf[:, dc:d]
        tot = jnp.sum(jnp.sum(loss_ref[...], axis=0, keepdims=True), axis=1, keepdims=True) * (0.5 / d)
        pack_ref[SG_N + 2:SG_N + 3, :] = jnp.broadcast_to(tot, (1, dc))
        dmod_ref[...] = jnp.zeros(dmod_ref.shape, F32)
        for j, ref in enumerate((dsh1, dsc1, dg1, dsh2, dsc2, dg2)):
            for b in range(nb):
                dmod_ref[b:b + 1, j * d:(j + 1) * d] = jnp.sum(ref[b], axis=0, keepdims=True)

    return pl.pallas_call(
        body, name="pack_small",
        out_shape=[jax.ShapeDtypeStruct((SMALL_ROWS, dc), F32), jax.ShapeDtypeStruct((8, 6 * d), F32)],
        in_specs=[VMEM] * 9, out_specs=[VMEM] * 2,
    )(sg, ggf, loss, *accs)


def _small_reduce(pack_all, dmod_all, nb):
    def body(pk_ref, dm_ref, red_ref, dmod_ref, gb_ref):
        tot = pk_ref[0]
        for dev in range(1, N_DEV):
            tot = tot + pk_ref[dev]
        red_ref[...] = tot
        gb = jnp.zeros((1, dm_ref.shape[2]), F32)
        for dev in range(N_DEV):
            for b in range(nb):
                row = dm_ref[dev, b:b + 1, :]
                dmod_ref[dev * nb + b:dev * nb + b + 1, :] = row
                gb = gb + row
        gb_ref[...] = gb

    return pl.pallas_call(
        body, name="small_reduce",
        out_shape=[jax.ShapeDtypeStruct(pack_all.shape[1:], F32),
                   jax.ShapeDtypeStruct((N_DEV * nb, dmod_all.shape[2]), F32),
                   jax.ShapeDtypeStruct((1, dmod_all.shape[2]), F32)],
        in_specs=[VMEM] * 2, out_specs=[VMEM] * 3,
    )(pack_all, dmod_all)


def _adam(w, g, m, v):
    m = ADAM_B1 * m + (1.0 - ADAM_B1) * g
    v = ADAM_B2 * v + (1.0 - ADAM_B2) * (g * g)
    m_hat = m / (1.0 - ADAM_B1 ** ADAM_STEP)
    v_hat = v / (1.0 - ADAM_B2 ** ADAM_STEP)
    delta = -ADAM_LR * (m_hat / (jnp.sqrt(v_hat) + ADAM_EPS) + ADAM_WD * w)
    return delta, m, v


def _adamw_big(w, g, m, v, name):
    r, c = w.shape
    tr = min(r, 256)

    def body(w_ref, g_ref, m_ref, v_ref, d_ref, nm_ref, nv_ref):
        d_ref[...], nm_ref[...], nv_ref[...] = _adam(w_ref[...], g_ref[...], m_ref[...], v_ref[...])

    s = jax.ShapeDtypeStruct((r, c), F32)
    return pl.pallas_call(
        body, name=name, grid=(r // tr,),
        in_specs=[_rowblk(tr, c)] * 4, out_specs=[_rowblk(tr, c)] * 3, out_shape=[s, s, s],
        compiler_params=pltpu.CompilerParams(dimension_semantics=("parallel",)),
    )(w, g, m, v)


def _adamw_ada(act_t, dmod_cols, w, m, v):
    r, c = w.shape
    tr = min(r, 256)
    nb = act_t.shape[1]

    def body(a_ref, dm_ref, w_ref, m_ref, v_ref, g_ref, d_ref, nm_ref, nv_ref):
        g = jnp.dot(a_ref[...], dm_ref[...], preferred_element_type=F32, precision=HIGHEST)
        g_ref[...] = g
        d_ref[...], nm_ref[...], nv_ref[...] = _adam(w_ref[...], g, m_ref[...], v_ref[...])

    s = jax.ShapeDtypeStruct((r, c), F32)
    return pl.pallas_call(
        body, name="adamw_w_ada", grid=(r // tr,),
        in_specs=[_rowblk(tr, nb), _const((nb, c))] + [_rowblk(tr, c)] * 3,
        out_specs=[_rowblk(tr, c)] * 4, out_shape=[s, s, s, s],
        compiler_params=pltpu.CompilerParams(dimension_semantics=("parallel",)),
    )(act_t, dmod_cols, w, m, v)


def _adamw_small(ws, gs, ms, vs):
    n = len(ws)

    def body(*refs):
        for i in range(n):
            w, g, m, v = (refs[j * n + i][...] for j in range(4))
            dl, nm, nv = _adam(w, g, m, v)
            refs[4 * n + i][...] = dl
            refs[5 * n + i][...] = nm
            refs[6 * n + i][...] = nv

    shapes = [jax.ShapeDtypeStruct(w.shape, F32) for w in ws]
    return pl.pallas_call(
        body, name="adamw_small", out_shape=shapes * 3,
        in_specs=[VMEM] * (4 * n), out_specs=[VMEM] * (3 * n),
    )(*ws, *gs, *ms, *vs)


def kernel(x, c, w_ada, b_ada, w_in, conf_dw_w, conf_dw_b, conf_ln_g, conf_ln_b, sc_conv_w, w_out, w_mlp1, w_mlp2, g_final, loss_target, m_w_ada, m_b_ada, m_w_in, m_conf_dw_w, m_conf_dw_b, m_conf_ln_g, m_conf_ln_b, m_sc_conv_w, m_w_out, m_w_mlp1, m_w_mlp2, m_g_final, v_w_ada, v_b_ada, v_w_in, v_conf_dw_w, v_conf_dw_b, v_conf_ln_g, v_conf_ln_b, v_sc_conv_w, v_w_out, v_w_mlp1, v_w_mlp2, v_g_final):
    nb, t, d = x.shape
    n = nb * t
    dc = d // 2
    ada_w = w_ada.shape[2]
    ax, ay, ac = _me()
    chip = 2 * ax + ay
    dev = 2 * chip + ac
    ids = jnp.stack([chip, ac]).astype(jnp.int32)

    lays = _wlayout(d)
    names = ("in", "out", "mlp1", "mlp2")
    fulls = [_cast_place(ids, w[0], lays[i], "cast_" + names[i]) for i, w in enumerate((w_in, w_out, w_mlp1, w_mlp2))]
    plan_a, plan_b, plan_p = _gather_direct_plan(lays[0:2]), _gather_chip_plan(lays[2:4]), _gather_pass_plan(lays[2:4])
    sems_a, bufs_a, tok_a = _copy_start("gather_start_a", fulls[0:2], 12, plan_a)
    sems_b, bufs_b, tok_b = _copy_start("gather_start_b", fulls[2:4], 6, plan_b)

    c_pad = jnp.zeros((8, d), F32).at[0:nb].set(c) + (tok_a[0, 0] + tok_b[0, 0])
    cw_pad = jnp.zeros((SMALL_ROWS, dc // N_CHIPS), F32)
    cw_pad = cw_pad.at[0:CONF_K].set(conf_dw_w[0]).at[HALO:HALO + SC_K].set(sc_conv_w[0])
    c_all8, cw_all8 = _all_gather8([c_pad, cw_pad], "gather_c")
    c_all = c_all8[:, 0:nb].reshape(N_DEV * nb, d)
    cw_full = jnp.concatenate([cw_all8[2 * k] for k in range(N_CHIPS)], axis=1)
    dww, scw = cw_full[0:CONF_K], cw_full[HALO:HALO + SC_K]
    b_cols = lax.dynamic_slice(b_ada, (0, chip * ada_w), (1, ada_w))
    c_act, mod_shard = _ada_mod(c_all, w_ada[0], b_cols)
    mod_all = _gather_mod(mod_shard)
    mod3 = lax.dynamic_slice(mod_all, (dev * nb, 0), (nb, 6 * d)).reshape(nb, 1, 6 * d)

    x2 = x.reshape(n, d)
    tgt = loss_target.reshape(n, d)
    wf_in, wf_out = _copy_wait("gather_wait_a", bufs_a, sems_a, plan_a, [mod3])
    proj, h1 = _in_proj(x2, mod3, wf_in, t)
    mixed, a1, q = _mixer_fwd(proj, dww, conf_dw_b, conf_ln_g, conf_ln_b, scw, t)
    bufs_b = _copy_wait("gather_wait_b", bufs_b, sems_b, plan_b, [mixed])
    sems_p, bufs_p, tok_p = _copy_start("gather_pass_start", bufs_b, 6, plan_p)
    x1, y1, h2 = _out_proj(mixed, wf_out, x2, mod3, t, tok_p)
    wf_1, wf_2 = _copy_wait("gather_pass_wait", bufs_p, sems_p, plan_p, [h2])
    z, dx2, dy2, dg2, ggf, loss_p = _mlp_fwd(h2, x1, tgt, mod3, g_final.reshape(1, d), wf_1, wf_2, t)

    dz, dx1, dy1, dsh2, dsc2, dg1 = _mlp_bwd(dy2, z, x1, dx2, y1, mod3, wf_1, wf_2, t)
    g_w2 = _wgrad(z, dy2, "wgrad_mlp2", relu2=True)
    g_w1 = _wgrad(h2, dz, "wgrad_mlp1")
    g_wout = _wgrad(mixed, dy1, "wgrad_out")
    made = {}

    def behind_pair_exchange(tok):
        made["dmixed"] = _out_proj_bwd(dy1, wf_out, t, tok)
        return [made["dmixed"]]

    def behind_chip_exchange(tok):
        made["dproj"], made["sg"] = _mixer_bwd(made["dmixed"], a1, q, proj, dww, conf_ln_g, conf_ln_b, scw, t, tok)
        return [made["dproj"]]

    def behind_pair_share(tok):
        made["grad_x"], made["dsh1"], made["dsc1"] = _in_proj_bwd(made["dproj"], wf_in, x2, dx1, mod3, t, tok)
        made["g_win"] = _wgrad(h1, made["dproj"], "wgrad_in")
        return [made["grad_x"], made["g_win"]]

    gr_out, gr_1, gr_2 = _reduce_scatter(ids, [g_wout, g_w1, g_w2], lays[1:4], names[1:4], "m",
                                         (behind_pair_exchange, behind_chip_exchange, behind_pair_share))
    (gr_in,) = _reduce_scatter(ids, [made["g_win"]], lays[0:1], names[0:1], "i", (None, None, None))
    grad_x, sg = made["grad_x"], made["sg"]

    pack, dmod8 = _pack_small(sg, ggf, loss_p, (made["dsh1"], made["dsc1"], dg1, dsh2, dsc2, dg2), d)
    pack_all, dmod_all8 = _all_gather8([pack, dmod8], "gather_small")
    red, dmod_all, g_bada = _small_reduce(pack_all, dmod_all8, nb)

    dmod_cols = lax.dynamic_slice(dmod_all, (0, chip * ada_w), (N_DEV * nb, ada_w))
    g_wada, d_wada, nm_wada, nv_wada = _adamw_ada(c_act.T, dmod_cols, w_ada[0], m_w_ada[0], v_w_ada[0])
    big = {}
    for nm_, g_, w_, m_, v_ in (("w_in", gr_in, w_in, m_w_in, v_w_in), ("w_out", gr_out, w_out, m_w_out, v_w_out),
                                ("w_mlp1", gr_1, w_mlp1, m_w_mlp1, v_w_mlp1),
                                ("w_mlp2", gr_2, w_mlp2, m_w_mlp2, v_w_mlp2)):
        big[nm_] = _adamw_big(w_[0], g_, m_[0], v_[0], "adamw_" + nm_)
    cw = dc // N_CHIPS
    g_dww = lax.dynamic_slice(red[0:CONF_K], (0, chip * cw), (CONF_K, cw))
    g_scw = lax.dynamic_slice(red[SG_SCW:SG_SCW + SC_K], (0, chip * cw), (SC_K, cw))
    g_gfin = jnp.concatenate([red[SG_N:SG_N + 1], red[SG_N + 1:SG_N + 2]], axis=1)
    small_g = [g_bada, g_dww, red[SG_DWB:SG_DWB + 1], red[SG_LNG:SG_LNG + 1], red[SG_LNB:SG_LNB + 1], g_scw, g_gfin]
    small_w = [b_ada, conf_dw_w[0], conf_dw_b, conf_ln_g, conf_ln_b, sc_conv_w[0], g_final.reshape(1, d)]
    small_m = [m_b_ada, m_conf_dw_w[0], m_conf_dw_b, m_conf_ln_g, m_conf_ln_b, m_sc_conv_w[0], m_g_final.reshape(1, d)]
    small_v = [v_b_ada, v_conf_dw_w[0], v_conf_dw_b, v_conf_ln_g, v_conf_ln_b, v_sc_conv_w[0], v_g_final.reshape(1, d)]
    upd = _adamw_small(small_w, small_g, small_m, small_v)
    ns = len(small_w)
    s_delta, s_m, s_v = upd[0:ns], upd[ns:2 * ns], upd[2 * ns:3 * ns]

    loss = red[SG_N + 2, 0]

    def outs(kind_big, kind_small, wada):
        sm = kind_small
        return (wada[None], sm[0], kind_big["w_in"][None], sm[1][None], sm[2], sm[3], sm[4], sm[5][None],
                kind_big["w_out"][None], kind_big["w_mlp1"][None], kind_big["w_mlp2"][None], sm[6].reshape(d))

    grads_out = outs({"w_in": gr_in, "w_out": gr_out, "w_mlp1": gr_1, "w_mlp2": gr_2}, small_g, g_wada)
    delta_out = outs({k: v[0] for k, v in big.items()}, s_delta, d_wada)
    m_out = outs({k: v[1] for k, v in big.items()}, s_m, nm_wada)
    v_out = outs({k: v[2] for k, v in big.items()}, s_v, nv_wada)
    return (loss, grad_x.reshape(nb, t, d), *grads_out, *delta_out, *m_out, *v_out)
```

```python
import functools

import jax
import jax.numpy as jnp
from jax import lax
from jax.experimental import pallas as pl
from jax.experimental.pallas import tpu as pltpu

F32 = jnp.float32
BF16 = jnp.bfloat16
MESH = pl.DeviceIdType.MESH
HIGHEST = lax.Precision.HIGHEST

EPS = 1e-6
CONF_K = 31
SC_K = 3
HALO = 32
N_CHIPS = 4
N_DEV = 8

ADAM_LR = 0.001
ADAM_B1 = 0.9
ADAM_B2 = 0.999
ADAM_EPS = 1e-08
ADAM_WD = 0.01
ADAM_STEP = 10

TM_MM = 512
TM_MIX = 256
TM_MLP = 256
TK_WG = 2048
RB_CONV = 64
RB_WG = 32
CHIP_RELS = ((1, 0), (0, 1), (1, 1))

ANY = pl.BlockSpec(memory_space=pl.ANY)
VMEM = pl.BlockSpec(memory_space=pltpu.VMEM)
HBM = pl.BlockSpec(memory_space=pltpu.HBM)
SEM = pl.BlockSpec(memory_space=pltpu.SEMAPHORE)
EFFECT = pltpu.SideEffectType.DATAFLOW_SIDE_EFFECTING


def _me():
    return lax.axis_index("x"), lax.axis_index("y"), lax.axis_index("c")


def _flip(v, f):
    return 1 - v if f else v


def _rows8(v):
    r, c = v.shape
    return v.reshape(r // 8, 8, c).sum(axis=0)


def _rms(x):
    r = lax.rsqrt(jnp.mean(x * x, axis=-1, keepdims=True) + EPS)
    return x * r, r


def _rms_bwd(dxn, xn, r):
    return r * (dxn - xn * jnp.mean(dxn * xn, axis=-1, keepdims=True))


def _sigmoid(x):
    return 1.0 / (1.0 + jnp.exp(-x))


def _dot(a, b):
    return jnp.dot(a, b, preferred_element_type=F32)


def _dot_nt(a, b):
    return lax.dot_general(a, b, (((1,), (1,)), ((), ())), preferred_element_type=F32)


def _dot_tn(a, b):
    return lax.dot_general(a, b, (((0,), (0,)), ((), ())), preferred_element_type=F32)


def _const(shape):
    nd = len(shape)
    return pl.BlockSpec(shape, lambda i: (0,) * nd)


def _resident(shape):
    nd = len(shape)
    return pl.BlockSpec(shape, lambda i: (0,) * nd, pipeline_mode=pl.Buffered(1))


def _rowblk(tm, c):
    return pl.BlockSpec((tm, c), lambda i: (i, 0))


def _modspec(tps, width):
    return pl.BlockSpec((None, 1, width), lambda i: (i // tps, 0, 0))


def _accspec(tps, c):
    return pl.BlockSpec((None, 8, c), lambda i: (i // tps, 0, 0))


def _acc_add(ref, first, v):
    @pl.when(first)
    def _():
        ref[...] = v

    @pl.when(jnp.logical_not(first))
    def _():
        ref[...] += v


def _all_gather8(arrs, name):
    n = len(arrs)

    def body(*refs):
        ins, outs = refs[:n], refs[n:2 * n]
        send, recv = refs[2 * n:]
        x, y, c = _me()
        dev = 4 * x + 2 * y + c
        for a in range(n):
            outs[a][dev] = ins[a][...]
        sends = []
        for r in range(1, N_DEV):
            fx, fy, fc = (r >> 2) & 1, (r >> 1) & 1, r & 1
            peer = (_flip(x, fx), _flip(y, fy), _flip(c, fc))
            for a in range(n):
                cp = pltpu.make_async_remote_copy(
                    src_ref=ins[a], dst_ref=outs[a].at[dev],
                    send_sem=send.at[r - 1, a], recv_sem=recv.at[r - 1, a],
                    device_id=peer, device_id_type=MESH)
                cp.start()
                sends.append(cp)
        for r in range(1, N_DEV):
            fx, fy, fc = (r >> 2) & 1, (r >> 1) & 1, r & 1
            pdev = 4 * _flip(x, fx) + 2 * _flip(y, fy) + _flip(c, fc)
            for a in range(n):
                pltpu.make_async_remote_copy(
                    src_ref=ins[a], dst_ref=outs[a].at[pdev],
                    send_sem=send.at[r - 1, a], recv_sem=recv.at[r - 1, a],
                    device_id=(x, y, c), device_id_type=MESH).wait_recv()
        for cp in sends:
            cp.wait_send()

    return pl.pallas_call(
        body, name=name,
        out_shape=[jax.ShapeDtypeStruct((N_DEV,) + a.shape, a.dtype) for a in arrs],
        in_specs=[VMEM] * n, out_specs=[VMEM] * n,
        scratch_shapes=[pltpu.SemaphoreType.DMA((N_DEV - 1, n)),
                        pltpu.SemaphoreType.DMA((N_DEV - 1, n))],
    )(*arrs)


def _gather_mod(mod_shard):
    nb, w = mod_shard.shape

    def body(in_ref, out_ref, send, recv):
        x, y, c = _me()
        chip = 2 * x + y
        out_ref[:, pl.ds(pl.multiple_of(chip * w, 128), w)] = in_ref[...]
        sends = []
        for r, (fx, fy) in enumerate(CHIP_RELS):
            cp = pltpu.make_async_remote_copy(
                src_ref=in_ref,
                dst_ref=out_ref.at[:, pl.ds(pl.multiple_of(chip * w, 128), w)],
                send_sem=send.at[r], recv_sem=recv.at[r],
                device_id=(_flip(x, fx), _flip(y, fy), c), device_id_type=MESH)
            cp.start()
            sends.append(cp)
        for r, (fx, fy) in enumerate(CHIP_RELS):
            pchip = 2 * _flip(x, fx) + _flip(y, fy)
            pltpu.make_async_remote_copy(
                src_ref=in_ref,
                dst_ref=out_ref.at[:, pl.ds(pl.multiple_of(pchip * w, 128), w)],
                send_sem=send.at[r], recv_sem=recv.at[r],
                device_id=(x, y, c), device_id_type=MESH).wait_recv()
        for cp in sends:
            cp.wait_send()

    return pl.pallas_call(
        body, name="gather_mod",
        out_shape=jax.ShapeDtypeStruct((nb, N_CHIPS * w), mod_shard.dtype),
        in_specs=[VMEM], out_specs=VMEM,
        scratch_shapes=[pltpu.SemaphoreType.DMA((3,)), pltpu.SemaphoreType.DMA((3,))],
    )(mod_shard)


def _wlayout(d):
    d_in = 5 * d // 2
    return (
        (d, d_in // N_CHIPS, True),
        (d // N_CHIPS, d, False),
        (d, 4 * d // N_CHIPS, True),
        (4 * d // N_CHIPS, d, False),
    )


def _full_shape(lay):
    r, c, by_col = lay
    return (r, c * N_CHIPS) if by_col else (r * N_CHIPS, c)


def _full_view(ref, lay, k, h):
    r, c, by_col = lay
    hr = r // 2
    if by_col:
        return ref.at[pl.ds(pl.multiple_of(h * hr, 16), hr), pl.ds(pl.multiple_of(k * c, 128), c)]
    return ref.at[pl.ds(pl.multiple_of(k * r + h * hr, 16), hr), :]


def _half_view(ref, lay, h):
    hr = lay[0] // 2
    return ref.at[pl.ds(pl.multiple_of(h * hr, 16), hr), :]


def _half_shape(lay):
    return (lay[0] // 2, lay[1])


def _hbm(a):
    return pltpu.with_memory_space_constraint(a, pltpu.HBM)


def _remote(src, dst, send, recv, idx, to):
    return lambda: pltpu.make_async_remote_copy(src_ref=src, dst_ref=dst, send_sem=send.at[idx], recv_sem=recv.at[idx],
                                                device_id=to, device_id_type=MESH)


def _copy_start(name, bufs, n_sems, plan, after=()):
    nb, na = len(bufs), len(after)

    def body(*refs):
        sends, _ = plan(refs[:nb], refs[nb + na], refs[nb + na + 1])
        for mk in sends:
            mk().start()
        refs[-1][...] = jnp.zeros((8, 128), F32)

    outs = pl.pallas_call(
        body, name=name,
        out_shape=(pltpu.SemaphoreType.DMA((n_sems,)), pltpu.SemaphoreType.DMA((n_sems,)))
        + tuple(pltpu.HBM(b.shape, b.dtype) for b in bufs) + (jax.ShapeDtypeStruct((8, 128), F32),),
        in_specs=(HBM,) * nb + (ANY,) * na, out_specs=(SEM, SEM) + (HBM,) * nb + (VMEM,),
        input_output_aliases={i: 2 + i for i in range(nb)},
        compiler_params=pltpu.CompilerParams(has_side_effects=EFFECT),
    )(*[_hbm(b) for b in bufs], *after)
    return (outs[0], outs[1]), list(outs[2:2 + nb]), outs[-1]


def _copy_wait(name, bufs, sems, plan, after):
    nb, na = len(bufs), len(after)

    def body(*refs):
        sends, recvs = plan(refs[:nb], refs[nb], refs[nb + 1])
        for mk in sends:
            mk().wait_send()
        for mk in recvs:
            mk().wait_recv()

    outs = pl.pallas_call(
        body, name=name,
        out_shape=tuple(pltpu.HBM(b.shape, b.dtype) for b in bufs),
        in_specs=(HBM,) * nb + (SEM, SEM) + (ANY,) * na, out_specs=(HBM,) * nb,
        input_output_aliases={i: i for i in range(nb)},
        compiler_params=pltpu.CompilerParams(has_side_effects=EFFECT),
    )(*bufs, *sems, *after)
    return list(outs)


def _copy_blocking(name, bufs, n_sems, plan, after=()):
    nb, na = len(bufs), len(after)

    def body(*refs):
        sends, recvs = plan(refs[:nb], refs[2 * nb + na], refs[2 * nb + na + 1])
        started = [mk() for mk in sends]
        for cp in started:
            cp.start()
        for mk in recvs:
            mk().wait_recv()
        for cp in started:
            cp.wait_send()

    return list(pl.pallas_call(
        body, name=name,
        out_shape=tuple(jax.ShapeDtypeStruct(b.shape, b.dtype) for b in bufs),
        in_specs=(ANY,) * (nb + na), out_specs=(ANY,) * nb,
        input_output_aliases={i: i for i in range(nb)},
        scratch_shapes=[pltpu.SemaphoreType.DMA((n_sems,)), pltpu.SemaphoreType.DMA((n_sems,))],
    )(*bufs, *after))


def _exchange(name, bufs, n_sems, plan, between):
    if between is None:
        return _copy_blocking(name, bufs, n_sems, plan)
    sems, bufs, tok = _copy_start(name + "_start", bufs, n_sems, plan)
    return _copy_wait(name + "_wait", bufs, sems, plan, between(tok))


def _gather_direct_plan(lays):
    def plan(full, send, recv):
        x, y, c = _me()
        chip = 2 * x + y
        sends, recvs = [], []
        for r, (fx, fy) in enumerate(CHIP_RELS):
            px, py = _flip(x, fx), _flip(y, fy)
            for i, lay in enumerate(lays):
                for q in range(2):
                    oc = _flip(c, q)
                    mine = _full_view(full[i], lay, chip, c)
                    idx = (r * len(lays) + i) * 2 + q
                    sends.append(_remote(mine, mine, send, recv, idx, (px, py, oc)))
                    theirs = _full_view(full[i], lay, 2 * px + py, oc)
                    recvs.append(_remote(theirs, theirs, send, recv, idx, (x, y, c)))
        return sends, recvs
    return plan


def _gather_chip_plan(lays):
    def plan(full, send, recv):
        x, y, c = _me()
        chip = 2 * x + y
        sends, recvs = [], []
        for r, (fx, fy) in enumerate(CHIP_RELS):
            px, py = _flip(x, fx), _flip(y, fy)
            for i, lay in enumerate(lays):
                mine = _full_view(full[i], lay, chip, c)
                sends.append(_remote(mine, mine, send, recv, r * len(lays) + i, (px, py, c)))
                theirs = _full_view(full[i], lay, 2 * px + py, c)
                recvs.append(_remote(theirs, theirs, send, recv, r * len(lays) + i, (x, y, c)))
        return sends, recvs
    return plan


def _gather_pass_plan(lays):
    def plan(full, send, recv):
        x, y, c = _me()
        sends, recvs = [], []
        for r, (fx, fy) in enumerate(CHIP_RELS):
            pchip = 2 * _flip(x, fx) + _flip(y, fy)
            for i, lay in enumerate(lays):
                landed = _full_view(full[i], lay, pchip, c)
                sends.append(_remote(landed, landed, send, recv, r * len(lays) + i, (x, y, 1 - c)))
                other = _full_view(full[i], lay, pchip, 1 - c)
                recvs.append(_remote(other, other, send, recv, r * len(lays) + i, (x, y, c)))
        return sends, recvs
    return plan


def _pair_exchange_plan(lays):
    nw = len(lays)

    def plan(bufs, send, recv):
        x, y, c = _me()
        sends, recvs = [], []
        for i, lay in enumerate(lays):
            for k in range(N_CHIPS):
                sends.append(_remote(_full_view(bufs[i], lay, k, 1 - c), bufs[nw + i].at[k],
                                     send, recv, i * N_CHIPS + k, (x, y, 1 - c)))
                recvs.append(_remote(_full_view(bufs[i], lay, k, c), bufs[nw + i].at[k],
                                     send, recv, i * N_CHIPS + k, (x, y, c)))
        return sends, recvs
    return plan


def _chip_exchange_plan(nw):
    def plan(bufs, send, recv):
        x, y, c = _me()
        sends, recvs = [], []
        for r, (fx, fy) in enumerate(CHIP_RELS):
            px, py = _flip(x, fx), _flip(y, fy)
            for i in range(nw):
                sends.append(_remote(bufs[i].at[2 * px + py], bufs[nw + i].at[r], send, recv, r * nw + i, (px, py, c)))
                recvs.append(_remote(bufs[i].at[2 * px + py], bufs[nw + i].at[r], send, recv, r * nw + i, (x, y, c)))
        return sends, recvs
    return plan


def _pair_share_plan(lays):
    def plan(bufs, send, recv):
        x, y, c = _me()
        sends, recvs = [], []
        for i, lay in enumerate(lays):
            mine = _half_view(bufs[i], lay, c)
            sends.append(_remote(mine, mine, send, recv, i, (x, y, 1 - c)))
            other = _half_view(bufs[i], lay, 1 - c)
            recvs.append(_remote(other, other, send, recv, i, (x, y, c)))
        return sends, recvs
    return plan


def _pair_sum(ids, g, got, lay, name):
    r, c, by_col = lay
    hr = r // 2
    tr = min(hr, 256)
    nt = hr // tr

    def body(ids_ref, g_ref, got_ref, s32_ref, s16_ref):
        s = g_ref[...] + got_ref[...]
        s32_ref[...] = s
        s16_ref[...] = s.astype(BF16)

    if by_col:
        gspec = pl.BlockSpec((tr, c), lambda k, t, ids: (ids[1] * nt + t, k))
    else:
        gspec = pl.BlockSpec((tr, c), lambda k, t, ids: ((2 * k + ids[1]) * nt + t, 0))
    hspec = pl.BlockSpec((None, tr, c), lambda k, t, ids: (k, t, 0))
    return pl.pallas_call(
        body, name=name,
        grid_spec=pltpu.PrefetchScalarGridSpec(
            num_scalar_prefetch=1, grid=(N_CHIPS, nt),
            in_specs=[gspec, hspec], out_specs=[hspec, hspec]),
        out_shape=[jax.ShapeDtypeStruct((N_CHIPS, hr, c), F32),
                   jax.ShapeDtypeStruct((N_CHIPS, hr, c), BF16)],
    )(ids, g, got)


def _chip_sum(ids, s32, got, lay, name):
    hr, c = _half_shape(lay)
    tr = min(hr, 256)
    nt = hr // tr

    def body(ids_ref, s_ref, got_ref, out_ref):
        t = s_ref[...]
        for r in range(3):
            t = t + got_ref[r].astype(F32)
        out_ref[...] = t

    return pl.pallas_call(
        body, name=name,
        grid_spec=pltpu.PrefetchScalarGridSpec(
            num_scalar_prefetch=1, grid=(nt,),
            in_specs=[pl.BlockSpec((None, tr, c), lambda t, ids: (ids[0], t, 0)),
                      pl.BlockSpec((3, tr, c), lambda t, ids: (0, t, 0))],
            out_specs=pl.BlockSpec((tr, c), lambda t, ids: (ids[1] * nt + t, 0))),
        out_shape=jax.ShapeDtypeStruct((2 * hr, c), F32),
    )(ids, s32, got)


def _reduce_scatter(ids, grads, lays, names, tag, between):
    nw = len(lays)
    got1 = [lax.empty((N_CHIPS,) + _half_shape(l), F32) for l in lays]
    bufs = _exchange("pair_exchange_" + tag, list(grads) + got1, nw * N_CHIPS, _pair_exchange_plan(lays), between[0])
    sums = [_pair_sum(ids, bufs[i], bufs[nw + i], lays[i], "pair_sum_" + names[i]) for i in range(nw)]
    got2 = [lax.empty((3,) + _half_shape(l), BF16) for l in lays]
    bufs = _exchange("chip_exchange_" + tag, [s[1] for s in sums] + got2, 3 * nw, _chip_exchange_plan(nw), between[1])
    mine = [_chip_sum(ids, sums[i][0], bufs[nw + i], lays[i], "chip_sum_" + names[i]) for i in range(nw)]
    return _exchange("pair_share_" + tag, mine, nw, _pair_share_plan(lays), between[2])


def _cast_place(ids, w, lay, name):
    r, c, by_col = lay

    def body(ids_ref, w_ref, out_ref):
        out_ref[...] = w_ref[...].astype(BF16)

    omap = (lambda i, ids: (0, ids[0])) if by_col else (lambda i, ids: (ids[0], 0))
    return pl.pallas_call(
        body, name=name,
        grid_spec=pltpu.PrefetchScalarGridSpec(
            num_scalar_prefetch=1, grid=(1,),
            in_specs=[pl.BlockSpec((r, c), lambda i, ids: (0, 0))],
            out_specs=pl.BlockSpec((r, c), omap)),
        out_shape=jax.ShapeDtypeStruct(_full_shape(lay), BF16),
    )(ids, w)


def _ada_mod(c_all, w_ada, b_ada):
    def body(c_ref, w_ref, b_ref, act_ref, mod_ref):
        cv = c_ref[...]
        act = cv * _sigmoid(cv)
        act_ref[...] = act
        mod_ref[...] = jnp.dot(act, w_ref[...], preferred_element_type=F32, precision=HIGHEST) + b_ref[...]

    nb = c_all.shape[0]
    return pl.pallas_call(
        body, name="ada_mod",
        out_shape=[jax.ShapeDtypeStruct(c_all.shape, F32),
                   jax.ShapeDtypeStruct((nb, w_ada.shape[1]), F32)],
        in_specs=[VMEM] * 3, out_specs=[VMEM] * 2,
    )(c_all, w_ada, b_ada)


def _in_proj(x, mod3, w_in, t, dep):
    n, d = x.shape
    d_in = w_in.shape[1]
    tm = min(TM_MM, t)
    tps = t // tm

    def body(x_ref, mod_ref, w_ref, dep_ref, proj_ref, h_ref):
        xn, _ = _rms(x_ref[...])
        h = (xn * (1.0 + mod_ref[:, d:2 * d]) + mod_ref[:, 0:d]).astype(BF16)
        h_ref[...] = h
        proj_ref[...] = _dot(h, w_ref[...])

    return pl.pallas_call(
        body, name="in_proj", grid=(n // tm,),
        in_specs=[_rowblk(tm, d), _modspec(tps, 6 * d), _resident((d, d_in)), ANY],
        out_specs=[_rowblk(tm, d_in), _rowblk(tm, d)],
        out_shape=[jax.ShapeDtypeStruct((n, d_in), F32), jax.ShapeDtypeStruct((n, d), BF16)],
        compiler_params=pltpu.CompilerParams(dimension_semantics=("parallel",)),
    )(x, mod3, w_in, dep)


def _shifted_copies(ext_ref, e_ref, rows):
    for g in range(e_ref.shape[0]):
        for r in range(8):
            e_ref[g, r, 0:rows, :] = ext_ref[r:r + rows, 128 * g:128 * g + 128]


def _shift_scratch(tm, dc):
    return pltpu.VMEM((dc // 128, 8, tm + HALO + 8, 128), F32)


def _mixer_fwd(proj, wb, dwb, lng, lnb, scw, t):
    n, d_in = proj.shape
    dc = d_in // 5
    tm = min(TM_MIX, t)
    tps = t // tm
    hb = tm // HALO
    rb = min(RB_CONV, tm)

    def body(p_ref, ph_ref, wb_ref, dwb_ref, lng_ref, lnb_ref, scw_ref,
             mixed_ref, a1_ref, q_ref, ext_ref, e_ref, extp_ref):
        first = pl.program_id(0) % tps == 0
        keep = jnp.where(first, 0.0, 1.0)
        a0 = p_ref[:, 0:dc] * _sigmoid(p_ref[:, dc:2 * dc])
        a0h = ph_ref[:, 0:dc] * _sigmoid(ph_ref[:, dc:2 * dc]) * keep
        ext_ref[0:HALO, :] = a0h
        ext_ref[HALO:HALO + tm, :] = a0
        ext_ref[HALO + tm:HALO + tm + 8, :] = jnp.zeros((8, dc), F32)
        _shifted_copies(ext_ref, e_ref, tm + HALO)
        for g in range(dc // 128):
            lanes = slice(128 * g, 128 * g + 128)

            def rows(j, carry, lanes=lanes, g=g):
                i0 = pl.multiple_of(j * rb, rb)
                acc = jnp.zeros((rb, 128), F32)
                for k in range(CONF_K):
                    m, r = divmod(k + HALO - CONF_K + 1, 8)
                    acc = acc + e_ref[g, r, pl.ds(i0 + 8 * m, rb), :] * wb_ref[k:k + 1, lanes]
                a1_ref[pl.ds(i0, rb), lanes] = acc + dwb_ref[:, lanes]
                return carry

            lax.fori_loop(0, tm // rb, rows, 0)
        a1 = a1_ref[...]
        mu = jnp.mean(a1, axis=-1, keepdims=True)
        ac = a1 - mu
        rstd = lax.rsqrt(jnp.mean(ac * ac, axis=-1, keepdims=True) + EPS)
        a2 = ac * rstd * lng_ref[...] + lnb_ref[...]
        mixed_ref[:, 0:dc] = (a2 * _sigmoid(a2)).astype(BF16)
        p = p_ref[:, 3 * dc:4 * dc] * p_ref[:, 4 * dc:5 * dc]
        ph = ph_ref[HALO - 8:HALO, 3 * dc:4 * dc] * ph_ref[HALO - 8:HALO, 4 * dc:5 * dc] * keep
        extp_ref[0:8, :] = ph
        extp_ref[8:8 + tm, :] = p
        q = jnp.zeros((tm, dc), F32)
        for k in range(SC_K):
            q = q + extp_ref[6 + k:6 + k + tm, :] * scw_ref[k:k + 1, :]
        q_ref[...] = q
        mixed_ref[:, dc:2 * dc] = (p_ref[:, 2 * dc:3 * dc] * q).astype(BF16)

    return pl.pallas_call(
        body, name="mixer_fwd", grid=(n // tm,),
        in_specs=[_rowblk(tm, d_in),
                  pl.BlockSpec((HALO, d_in), lambda i: (jnp.maximum(i * hb - 1, 0), 0)),
                  _const(wb.shape), _const(dwb.shape), _const(lng.shape), _const(lnb.shape), _const(scw.shape)],
        out_specs=[_rowblk(tm, 2 * dc), _rowblk(tm, dc), _rowblk(tm, dc)],
        out_shape=[jax.ShapeDtypeStruct((n, 2 * dc), BF16), jax.ShapeDtypeStruct((n, dc), F32),
                   jax.ShapeDtypeStruct((n, dc), F32)],
        scratch_shapes=[pltpu.VMEM((tm + HALO + 8, dc), F32), _shift_scratch(tm, dc),
                        pltpu.VMEM((tm + 8, dc), F32)],
        compiler_params=pltpu.CompilerParams(dimension_semantics=("parallel",)),
    )(proj, proj, wb, dwb, lng, lnb, scw)


def _out_proj(mixed, w_out, x, mod3, t, dep):
    n, d = x.shape
    tm = min(TM_MM, t)
    tps = t // tm

    def body(m_ref, w_ref, x_ref, mod_ref, dep_ref, x1_ref, y1_ref, h2_ref):
        y1 = _dot(m_ref[...], w_ref[...])
        y1_ref[...] = y1
        x1 = x_ref[...] + mod_ref[:, 2 * d:3 * d] * y1
        x1_ref[...] = x1
        xn, _ = _rms(x1)
        h2_ref[...] = (xn * (1.0 + mod_ref[:, 4 * d:5 * d]) + mod_ref[:, 3 * d:4 * d]).astype(BF16)

    return pl.pallas_call(
        body, name="out_proj", grid=(n // tm,),
        in_specs=[_rowblk(tm, d), _resident((d, d)), _rowblk(tm, d), _modspec(tps, 6 * d), ANY],
        out_specs=[_rowblk(tm, d), _rowblk(tm, d), _rowblk(tm, d)],
        out_shape=[jax.ShapeDtypeStruct((n, d), F32), jax.ShapeDtypeStruct((n, d), F32),
                   jax.ShapeDtypeStruct((n, d), BF16)],
        compiler_params=pltpu.CompilerParams(dimension_semantics=("parallel",)),
    )(mixed, w_out, x, mod3, dep)


def _mlp_fwd(h2, x1, tgt, mod3, gfin, w1, w2, t):
    n, d = x1.shape
    dff = w1.shape[1]
    tm = min(TM_MLP, t)
    tps = t // tm
    nt = n // tm

    def body(h_ref, x1_ref, tg_ref, mod_ref, gf_ref, w1_ref, w2_ref,
             z_ref, dx2_ref, dy2_ref, dg2_ref, ggf_ref, loss_ref):
        i = pl.program_id(0)
        z = _dot(h_ref[...], w1_ref[...])
        z_ref[...] = z.astype(BF16)
        zr = jnp.maximum(z, 0.0)
        y2 = _dot((zr * zr).astype(BF16), w2_ref[...])
        g2 = mod_ref[:, 5 * d:6 * d]
        x2n, r3 = _rms(x1_ref[...] + g2 * y2)
        gf = gf_ref[...]
        diff = x2n * gf - tg_ref[...]
        dout = diff * (1.0 / d)
        dx2 = _rms_bwd(dout * gf, x2n, r3)
        dx2_ref[...] = dx2
        dy2_ref[...] = (g2 * dx2).astype(BF16)
        _acc_add(dg2_ref, i % tps == 0, _rows8(dx2 * y2))
        _acc_add(ggf_ref, i == 0, _rows8(dout * x2n))
        _acc_add(loss_ref, i == 0, _rows8(diff * diff))

    return pl.pallas_call(
        body, name="mlp_fwd", grid=(nt,),
        in_specs=[_rowblk(tm, d), _rowblk(tm, d), _rowblk(tm, d), _modspec(tps, 6 * d), _const((1, d)),
                  _resident((d, dff)), _resident((dff, d))],
        out_specs=[_rowblk(tm, dff), _rowblk(tm, d), _rowblk(tm, d), _accspec(tps, d),
                   _const((8, d)), _const((8, d))],
        out_shape=[jax.ShapeDtypeStruct((n, dff), BF16), jax.ShapeDtypeStruct((n, d), F32),
                   jax.ShapeDtypeStruct((n, d), BF16), jax.ShapeDtypeStruct((n // t, 8, d), F32),
                   jax.ShapeDtypeStruct((8, d), F32), jax.ShapeDtypeStruct((8, d), F32)],
        compiler_params=pltpu.CompilerParams(dimension_semantics=("arbitrary",)),
    )(h2, x1, tgt, mod3, gfin, w1, w2)


def _mlp_bwd(dy2, z, x1, dx2, y1, mod3, w1, w2, t):
    n, d = x1.shape
    dff = w1.shape[1]
    tm = min(TM_MLP, t)
    tps = t // tm

    def body(dy2_ref, z_ref, x1_ref, dx2_ref, y1_ref, mod_ref, w1_ref, w2_ref,
             dz_ref, dx1_ref, dy1_ref, dsh_ref, dsc_ref, dg1_ref):
        first = pl.program_id(0) % tps == 0
        du = _dot_nt(dy2_ref[...], w2_ref[...])
        dz = (du * (2.0 * jnp.maximum(z_ref[...].astype(F32), 0.0))).astype(BF16)
        dz_ref[...] = dz
        dh2 = _dot_nt(dz, w1_ref[...])
        x1n, r2 = _rms(x1_ref[...])
        _acc_add(dsh_ref, first, _rows8(dh2))
        _acc_add(dsc_ref, first, _rows8(dh2 * x1n))
        dx1 = dx2_ref[...] + _rms_bwd(dh2 * (1.0 + mod_ref[:, 4 * d:5 * d]), x1n, r2)
        dx1_ref[...] = dx1
        dy1_ref[...] = (mod_ref[:, 2 * d:3 * d] * dx1).astype(BF16)
        _acc_add(dg1_ref, first, _rows8(dx1 * y1_ref[...]))

    acc = jax.ShapeDtypeStruct((n // t, 8, d), F32)
    return pl.pallas_call(
        body, name="mlp_bwd", grid=(n // tm,),
        in_specs=[_rowblk(tm, d), _rowblk(tm, dff), _rowblk(tm, d), _rowblk(tm, d), _rowblk(tm, d),
                  _modspec(tps, 6 * d), _resident((d, dff)), _resident((dff, d))],
        out_specs=[_rowblk(tm, dff), _rowblk(tm, d), _rowblk(tm, d),
                   _accspec(tps, d), _accspec(tps, d), _accspec(tps, d)],
        out_shape=[jax.ShapeDtypeStruct((n, dff), BF16), jax.ShapeDtypeStruct((n, d), F32),
                   jax.ShapeDtypeStruct((n, d), BF16), acc, acc, acc],
        compiler_params=pltpu.CompilerParams(dimension_semantics=("arbitrary",)),
    )(dy2, z, x1, dx2, y1, mod3, w1, w2)


def _wgrad(a, b, name, relu2=False, bn=None):
    n, ka = a.shape
    nb = b.shape[1]
    tk = min(TK_WG, n)
    bm = min(ka, 1024)
    if bn is None:
        bn = nb if nb <= 2048 else nb // 2

    def body(a_ref, b_ref, out_ref):
        av = a_ref[...]
        if relu2:
            ar = jnp.maximum(av, 0.0)
            av = ar * ar
        p = _dot_tn(av, b_ref[...])
        _acc_add(out_ref, pl.program_id(2) == 0, p)

    return pl.pallas_call(
        body, name=name, grid=(ka // bm, nb // bn, n // tk),
        in_specs=[pl.BlockSpec((tk, bm), lambda i, j, k: (k, i)),
                  pl.BlockSpec((tk, bn), lambda i, j, k: (k, j))],
        out_specs=pl.BlockSpec((bm, bn), lambda i, j, k: (i, j)),
        out_shape=jax.ShapeDtypeStruct((ka, nb), F32),
        compiler_params=pltpu.CompilerParams(dimension_semantics=("parallel", "parallel", "arbitrary")),
    )(a, b)


def _out_proj_bwd(dy1, w_out, t, dep):
    n, d = dy1.shape
    tm = min(TM_MM, t)

    def body(dy_ref, w_ref, dep_ref, dm_ref):
        dm_ref[...] = _dot_nt(dy_ref[...], w_ref[...])

    return pl.pallas_call(
        body, name="out_proj_bwd", grid=(n // tm,),
        in_specs=[_rowblk(tm, d), _resident(w_out.shape), ANY],
        out_specs=_rowblk(tm, w_out.shape[0]),
        out_shape=jax.ShapeDtypeStruct((n, w_out.shape[0]), F32),
        compiler_params=pltpu.CompilerParams(dimension_semantics=("parallel",)),
    )(dy1, w_out, dep)


SG_DWW = 0
SG_DWB = CONF_K
SG_LNG = CONF_K + 1
SG_LNB = CONF_K + 2
SG_SCW = CONF_K + 3
SG_N = CONF_K + 3 + SC_K


def _mixer_bwd(dmixed, a1, q, proj, wb, lng, lnb, scw, t, dep):
    n, d_in = proj.shape
    dc = d_in // 5
    tm = min(TM_MIX, t)
    tps = t // tm
    hb = tm // HALO
    nh = n // HALO
    rb = min(RB_CONV, tm)
    rw = min(RB_WG, tm)

    def ln_bwd(a1v, da3, lng_v, lnb_v):
        mu = jnp.mean(a1v, axis=-1, keepdims=True)
        ac = a1v - mu
        rstd = lax.rsqrt(jnp.mean(ac * ac, axis=-1, keepdims=True) + EPS)
        ah = ac * rstd
        a2 = ah * lng_v + lnb_v
        s2 = _sigmoid(a2)
        da2 = da3 * (s2 * (1.0 + a2 * (1.0 - s2)))
        dah = da2 * lng_v
        da1 = rstd * (dah - jnp.mean(dah, axis=-1, keepdims=True)
                      - ah * jnp.mean(dah * ah, axis=-1, keepdims=True))
        return da1, da2, ah

    def body(dm_ref, dmn_ref, a1_ref, a1n_ref, q_ref, p_ref, pp_ref, pn_ref,
             wb_ref, lng_ref, lnb_ref, scw_ref, dep_ref, dproj_ref, sg_ref,
             exta_ref, ea_ref, extd_ref, ed_ref, da1_ref, da0_ref, extq_ref, extp_ref):
        i = pl.program_id(0)
        first = i % tps == 0
        last = i % tps == tps - 1
        keep_prev = jnp.where(first, 0.0, 1.0)
        keep_next = jnp.where(last, 0.0, 1.0)

        @pl.when(i == 0)
        def _():
            sg_ref[...] = jnp.zeros(sg_ref.shape, F32)

        lng_v, lnb_v = lng_ref[...], lnb_ref[...]
        val, sig = p_ref[:, 0:dc], _sigmoid(p_ref[:, dc:2 * dc])
        a0 = val * sig
        da1, da2, ah = ln_bwd(a1_ref[...], dm_ref[:, 0:dc], lng_v, lnb_v)
        da1n, _, _ = ln_bwd(a1n_ref[...], dmn_ref[:, 0:dc], lng_v, lnb_v)
        sg_ref[8 * SG_LNG:8 * SG_LNG + 8, :] += _rows8(da2 * ah)
        sg_ref[8 * SG_LNB:8 * SG_LNB + 8, :] += _rows8(da2)
        sg_ref[8 * SG_DWB:8 * SG_DWB + 8, :] += _rows8(da1)
        da1_ref[...] = da1
        extd_ref[0:tm, :] = da1
        extd_ref[tm:tm + HALO, :] = da1n * keep_next
        extd_ref[tm + HALO:tm + HALO + 8, :] = jnp.zeros((8, dc), F32)
        _shifted_copies(extd_ref, ed_ref, tm + HALO)
        exta_ref[0:HALO, :] = pp_ref[:, 0:dc] * _sigmoid(pp_ref[:, dc:2 * dc]) * keep_prev
        exta_ref[HALO:HALO + tm, :] = a0
        exta_ref[HALO + tm:HALO + tm + 8, :] = jnp.zeros((8, dc), F32)
        _shifted_copies(exta_ref, ea_ref, tm + HALO)
        for g in range(dc // 128):
            lanes = slice(128 * g, 128 * g + 128)

            def rows(j, carry, lanes=lanes, g=g):
                i0 = pl.multiple_of(j * rb, rb)
                acc = jnp.zeros((rb, 128), F32)
                for k in range(CONF_K):
                    m, r = divmod(CONF_K - 1 - k, 8)
                    acc = acc + ed_ref[g, r, pl.ds(i0 + 8 * m, rb), :] * wb_ref[k:k + 1, lanes]
                da0_ref[pl.ds(i0, rb), lanes] = acc
                return carry

            lax.fori_loop(0, tm // rb, rows, 0)

            def wrows(j, accs, lanes=lanes, g=g):
                i0 = pl.multiple_of(j * rw, rw)
                dv = da1_ref[pl.ds(i0, rw), lanes]
                new = []
                for k in range(CONF_K):
                    m, r = divmod(k + HALO - CONF_K + 1, 8)
                    new.append(accs[k] + _rows8(ea_ref[g, r, pl.ds(i0 + 8 * m, rw), :] * dv))
                return tuple(new)

            accs = lax.fori_loop(0, tm // rw, wrows,
                                 tuple(jnp.zeros((8, 128), F32) for _ in range(CONF_K)))
            for k in range(CONF_K):
                sg_ref[8 * (SG_DWW + k):8 * (SG_DWW + k) + 8, lanes] += accs[k]
        da0 = da0_ref[...]
        dproj_ref[:, 0:dc] = (da0 * sig).astype(BF16)
        dproj_ref[:, dc:2 * dc] = (da0 * a0 * (1.0 - sig)).astype(BF16)
        ds = dm_ref[:, dc:2 * dc]
        scb, scc, sch = p_ref[:, 2 * dc:3 * dc], p_ref[:, 3 * dc:4 * dc], p_ref[:, 4 * dc:5 * dc]
        dproj_ref[:, 2 * dc:3 * dc] = (ds * q_ref[...]).astype(BF16)
        dq = ds * scb
        extq_ref[0:tm, :] = dq
        extq_ref[tm:tm + 8, :] = dmn_ref[0:8, dc:2 * dc] * pn_ref[0:8, :] * keep_next
        dp = jnp.zeros((tm, dc), F32)
        for k in range(SC_K):
            dp = dp + extq_ref[SC_K - 1 - k:SC_K - 1 - k + tm, :] * scw_ref[k:k + 1, :]
        dproj_ref[:, 3 * dc:4 * dc] = (dp * sch).astype(BF16)
        dproj_ref[:, 4 * dc:5 * dc] = (dp * scc).astype(BF16)
        extp_ref[0:8, :] = pp_ref[HALO - 8:HALO, 3 * dc:4 * dc] * pp_ref[HALO - 8:HALO, 4 * dc:5 * dc] * keep_prev
        extp_ref[8:8 + tm, :] = scc * sch
        for k in range(SC_K):
            sg_ref[8 * (SG_SCW + k):8 * (SG_SCW + k) + 8, :] += _rows8(extp_ref[6 + k:6 + k + tm, :] * dq)

    nxt = lambda i: (jnp.minimum((i + 1) * hb, nh - 1), 0)
    prv = lambda i: (jnp.maximum(i * hb - 1, 0), 0)
    return pl.pallas_call(
        body, name="mixer_bwd", grid=(n // tm,),
        in_specs=[_rowblk(tm, 2 * dc), pl.BlockSpec((HALO, 2 * dc), nxt),
                  _rowblk(tm, dc), pl.BlockSpec((HALO, dc), nxt),
                  _rowblk(tm, dc),
                  _rowblk(tm, d_in), pl.BlockSpec((HALO, d_in), prv),
                  pl.BlockSpec((HALO, dc), lambda i: (jnp.minimum((i + 1) * hb, nh - 1), 2)),
                  _const(wb.shape), _const(lng.shape), _const(lnb.shape), _const(scw.shape), ANY],
        out_specs=[_rowblk(tm, d_in), _const((8 * SG_N, dc))],
        out_shape=[jax.ShapeDtypeStruct((n, d_in), BF16), jax.ShapeDtypeStruct((8 * SG_N, dc), F32)],
        scratch_shapes=[pltpu.VMEM((tm + HALO + 8, dc), F32), _shift_scratch(tm, dc),
                        pltpu.VMEM((tm + HALO + 8, dc), F32), _shift_scratch(tm, dc),
                        pltpu.VMEM((tm, dc), F32), pltpu.VMEM((tm, dc), F32),
                        pltpu.VMEM((tm + 8, dc), F32), pltpu.VMEM((tm + 8, dc), F32)],
        compiler_params=pltpu.CompilerParams(dimension_semantics=("arbitrary",)),
    )(dmixed, dmixed, a1, a1, q, proj, proj, proj, wb, lng, lnb, scw, dep)


def _in_proj_bwd(dproj, w_in, x, dx1, mod3, t, dep):
    n, d = x.shape
    d_in = w_in.shape[1]
    tm = min(TM_MM, t)
    tps = t // tm

    def body(dp_ref, w_ref, x_ref, dx1_ref, mod_ref, dep_ref, gx_ref, dsh_ref, dsc_ref):
        first = pl.program_id(0) % tps == 0
        dh1 = _dot_nt(dp_ref[...], w_ref[...])
        xn, r1 = _rms(x_ref[...])
        _acc_add(dsh_ref, first, _rows8(dh1))
        _acc_add(dsc_ref, first, _rows8(dh1 * xn))
        gx_ref[...] = dx1_ref[...] + _rms_bwd(dh1 * (1.0 + mod_ref[:, d:2 * d]), xn, r1)

    acc = jax.ShapeDtypeStruct((n // t, 8, d), F32)
    return pl.pallas_call(
        body, name="in_proj_bwd", grid=(n // tm,),
        in_specs=[_rowblk(tm, d_in), _resident((d, d_in)), _rowblk(tm, d), _rowblk(tm, d),
                  _modspec(tps, 6 * d), ANY],
        out_specs=[_rowblk(tm, d), _accspec(tps, d), _accspec(tps, d)],
        out_shape=[jax.ShapeDtypeStruct((n, d), F32), acc, acc],
        compiler_params=pltpu.CompilerParams(dimension_semantics=("arbitrary",)),
    )(dproj, w_in, x, dx1, mod3, dep)


SMALL_ROWS = 40


def _pack_small(sg, ggf, loss, accs, d):
    dc = d // 2
    nb = accs[0].shape[0]

    def body(sg_ref, ggf_ref, loss_ref, dsh1, dsc1, dg1, dsh2, dsc2, dg2, pack_ref, dmod_ref):
        pack_ref[...] = jnp.zeros(pack_ref.shape, F32)
        for k in range(SG_N):
            pack_ref[k:k + 1, :] = jnp.sum(sg_ref[8 * k:8 * k + 8, :], axis=0, keepdims=True)
        gf = jnp.sum(ggf_ref[...], axis=0, keepdims=True)
        pack_ref[SG_N:SG_N + 1, :] = gf[:, 0:dc]
        pack_ref[SG_N + 1:SG_N + 2, :] = gf[:, dc:d]
        tot = jnp.sum(jnp.sum(loss_ref[...], axis=0, keepdims=True), axis=1, keepdims=True) * (0.5 / d)
        pack_ref[SG_N + 2:SG_N + 3, :] = jnp.broadcast_to(tot, (1, dc))
        dmod_ref[...] = jnp.zeros(dmod_ref.shape, F32)
        for j, ref in enumerate((dsh1, dsc1, dg1, dsh2, dsc2, dg2)):
            for b in range(nb):
                dmod_ref[b:b + 1, j * d:(j + 1) * d] = jnp.sum(ref[b], axis=0, keepdims=True)

    return pl.pallas_call(
        body, name="pack_small",
        out_shape=[jax.ShapeDtypeStruct((SMALL_ROWS, dc), F32), jax.ShapeDtypeStruct((8, 6 * d), F32)],
        in_specs=[VMEM] * 9, out_specs=[VMEM] * 2,
    )(sg, ggf, loss, *accs)


def _small_reduce(pack_all, dmod_all, nb):
    def body(pk_ref, dm_ref, red_ref, dmod_ref, gb_ref):
        tot = pk_ref[0]
        for dev in range(1, N_DEV):
            tot = tot + pk_ref[dev]
        red_ref[...] = tot
        gb = jnp.zeros((1, dm_ref.shape[2]), F32)
        for dev in range(N_DEV):
            for b in range(nb):
                row = dm_ref[dev, b:b + 1, :]
                dmod_ref[dev * nb + b:dev * nb + b + 1, :] = row
                gb = gb + row
        gb_ref[...] = gb

    return pl.pallas_call(
        body, name="small_reduce",
        out_shape=[jax.ShapeDtypeStruct(pack_all.shape[1:], F32),
                   jax.ShapeDtypeStruct((N_DEV * nb, dmod_all.shape[2]), F32),
                   jax.ShapeDtypeStruct((1, dmod_all.shape[2]), F32)],
        in_specs=[VMEM] * 2, out_specs=[VMEM] * 3,
    )(pack_all, dmod_all)


def _adam(w, g, m, v):
    m = ADAM_B1 * m + (1.0 - ADAM_B1) * g
    v = ADAM_B2 * v + (1.0 - ADAM_B2) * (g * g)
    m_hat = m / (1.0 - ADAM_B1 ** ADAM_STEP)
    v_hat = v / (1.0 - ADAM_B2 ** ADAM_STEP)
    delta = -ADAM_LR * (m_hat / (jnp.sqrt(v_hat) + ADAM_EPS) + ADAM_WD * w)
    return delta, m, v


def _adamw_big(w, g, m, v, name):
    r, c = w.shape
    tr = min(r, 256)

    def body(w_ref, g_ref, m_ref, v_ref, d_ref, nm_ref, nv_ref):
        d_ref[...], nm_ref[...], nv_ref[...] = _adam(w_ref[...], g_ref[...], m_ref[...], v_ref[...])

    s = jax.ShapeDtypeStruct((r, c), F32)
    return pl.pallas_call(
        body, name=name, grid=(r // tr,),
        in_specs=[_rowblk(tr, c)] * 4, out_specs=[_rowblk(tr, c)] * 3, out_shape=[s, s, s],
        compiler_params=pltpu.CompilerParams(dimension_semantics=("parallel",)),
    )(w, g, m, v)


def _adamw_ada(act_t, dmod_cols, w, m, v):
    r, c = w.shape
    tr = min(r, 256)
    nb = act_t.shape[1]

    def body(a_ref, dm_ref, w_ref, m_ref, v_ref, g_ref, d_ref, nm_ref, nv_ref):
        g = jnp.dot(a_ref[...], dm_ref[...], preferred_element_type=F32, precision=HIGHEST)
        g_ref[...] = g
        d_ref[...], nm_ref[...], nv_ref[...] = _adam(w_ref[...], g, m_ref[...], v_ref[...])

    s = jax.ShapeDtypeStruct((r, c), F32)
    return pl.pallas_call(
        body, name="adamw_w_ada", grid=(r // tr,),
        in_specs=[_rowblk(tr, nb), _const((nb, c))] + [_rowblk(tr, c)] * 3,
        out_specs=[_rowblk(tr, c)] * 4, out_shape=[s, s, s, s],
        compiler_params=pltpu.CompilerParams(dimension_semantics=("parallel",)),
    )(act_t, dmod_cols, w, m, v)


def _adamw_small(ws, gs, ms, vs):
    n = len(ws)

    def body(*refs):
        for i in range(n):
            w, g, m, v = (refs[j * n + i][...] for j in range(4))
            dl, nm, nv = _adam(w, g, m, v)
            refs[4 * n + i][...] = dl
            refs[5 * n + i][...] = nm
            refs[6 * n + i][...] = nv

    shapes = [jax.ShapeDtypeStruct(w.shape, F32) for w in ws]
    return pl.pallas_call(
        body, name="adamw_small", out_shape=shapes * 3,
        in_specs=[VMEM] * (4 * n), out_specs=[VMEM] * (3 * n),
    )(*ws, *gs, *ms, *vs)


def kernel(x, c, w_ada, b_ada, w_in, conf_dw_w, conf_dw_b, conf_ln_g, conf_ln_b, sc_conv_w, w_out, w_mlp1, w_mlp2, g_final, loss_target, m_w_ada, m_b_ada, m_w_in, m_conf_dw_w, m_conf_dw_b, m_conf_ln_g, m_conf_ln_b, m_sc_conv_w, m_w_out, m_w_mlp1, m_w_mlp2, m_g_final, v_w_ada, v_b_ada, v_w_in, v_conf_dw_w, v_conf_dw_b, v_conf_ln_g, v_conf_ln_b, v_sc_conv_w, v_w_out, v_w_mlp1, v_w_mlp2, v_g_final):
    nb, t, d = x.shape
    n = nb * t
    dc = d // 2
    ada_w = w_ada.shape[2]
    ax, ay, ac = _me()
    chip = 2 * ax + ay
    dev = 2 * chip + ac
    ids = jnp.stack([chip, ac]).astype(jnp.int32)

    lays = _wlayout(d)
    names = ("in", "out", "mlp1", "mlp2")
    fulls = [_cast_place(ids, w[0], lays[i], "cast_" + names[i]) for i, w in enumerate((w_in, w_out, w_mlp1, w_mlp2))]

    c_pad = jnp.zeros((8, d), F32).at[0:nb].set(c)
    cw_pad = jnp.zeros((SMALL_ROWS, dc // N_CHIPS), F32)
    cw_pad = cw_pad.at[0:CONF_K].set(conf_dw_w[0]).at[HALO:HALO + SC_K].set(sc_conv_w[0])
    c_all8, cw_all8 = _all_gather8([c_pad, cw_pad], "gather_c")
    c_all = c_all8[:, 0:nb].reshape(N_DEV * nb, d)
    cw_full = jnp.concatenate([cw_all8[2 * k] for k in range(N_CHIPS)], axis=1)
    dww, scw = cw_full[0:CONF_K], cw_full[HALO:HALO + SC_K]
    b_cols = lax.dynamic_slice(b_ada, (0, chip * ada_w), (1, ada_w))
    c_act, mod_shard = _ada_mod(c_all, w_ada[0], b_cols)
    mod_all = _gather_mod(mod_shard)
    mod3 = lax.dynamic_slice(mod_all, (dev * nb, 0), (nb, 6 * d)).reshape(nb, 1, 6 * d)

    x2 = x.reshape(n, d)
    tgt = loss_target.reshape(n, d)
    (wf_in,) = _copy_blocking("gather_in_chip", [fulls[0]], 3, _gather_chip_plan(lays[0:1]), [mod3])
    (wf_in,) = _copy_blocking("gather_in_pass", [wf_in], 3, _gather_pass_plan(lays[0:1]))
    plan_o, plan_b, plan_p = _gather_direct_plan(lays[1:2]), _gather_chip_plan(lays[2:4]), _gather_pass_plan(lays[2:4])
    sems_o, bufs_o, tok_o = _copy_start("gather_out_start", [fulls[1]], 6, plan_o, [wf_in, mod3])
    sems_b, bufs_b, tok_b = _copy_start("gather_mlp_start", fulls[2:4], 6, plan_b, [tok_o])
    proj, h1 = _in_proj(x2, mod3, wf_in, t, tok_b)
    mixed, a1, q = _mixer_fwd(proj, dww, conf_dw_b, conf_ln_g, conf_ln_b, scw, t)
    (wf_out,) = _copy_wait("gather_out_wait", bufs_o, sems_o, plan_o, [mixed])
    bufs_b = _copy_wait("gather_mlp_wait", bufs_b, sems_b, plan_b, [mixed])
    sems_p, bufs_p, tok_p = _copy_start("gather_pass_start", bufs_b, 6, plan_p)
    x1, y1, h2 = _out_proj(mixed, wf_out, x2, mod3, t, tok_p)
    wf_1, wf_2 = _copy_wait("gather_pass_wait", bufs_p, sems_p, plan_p, [h2])
    z, dx2, dy2, dg2, ggf, loss_p = _mlp_fwd(h2, x1, tgt, mod3, g_final.reshape(1, d), wf_1, wf_2, t)

    dz, dx1, dy1, dsh2, dsc2, dg1 = _mlp_bwd(dy2, z, x1, dx2, y1, mod3, wf_1, wf_2, t)
    g_w2 = _wgrad(z, dy2, "wgrad_mlp2", relu2=True)
    g_w1 = _wgrad(h2, dz, "wgrad_mlp1")
    g_wout = _wgrad(mixed, dy1, "wgrad_out")
    made = {}

    def behind_pair_exchange(tok):
        made["dmixed"] = _out_proj_bwd(dy1, wf_out, t, tok)
        return [made["dmixed"]]

    def behind_chip_exchange(tok):
        made["dproj"], made["sg"] = _mixer_bwd(made["dmixed"], a1, q, proj, dww, conf_ln_g, conf_ln_b, scw, t, tok)
        return [made["dproj"]]

    def behind_pair_share(tok):
        made["grad_x"], made["dsh1"], made["dsc1"] = _in_proj_bwd(made["dproj"], wf_in, x2, dx1, mod3, t, tok)
        made["g_win"] = _wgrad(h1, made["dproj"], "wgrad_in")
        return [made["grad_x"], made["g_win"]]

    gr_out, gr_1, gr_2 = _reduce_scatter(ids, [g_wout, g_w1, g_w2], lays[1:4], names[1:4], "m",
                                         (behind_pair_exchange, behind_chip_exchange, behind_pair_share))
    (gr_in,) = _reduce_scatter(ids, [made["g_win"]], lays[0:1], names[0:1], "i", (None, None, None))
    grad_x, sg = made["grad_x"], made["sg"]

    pack, dmod8 = _pack_small(sg, ggf, loss_p, (made["dsh1"], made["dsc1"], dg1, dsh2, dsc2, dg2), d)
    pack_all, dmod_all8 = _all_gather8([pack, dmod8], "gather_small")
    red, dmod_all, g_bada = _small_reduce(pack_all, dmod_all8, nb)

    dmod_cols = lax.dynamic_slice(dmod_all, (0, chip * ada_w), (N_DEV * nb, ada_w))
    g_wada, d_wada, nm_wada, nv_wada = _adamw_ada(c_act.T, dmod_cols, w_ada[0], m_w_ada[0], v_w_ada[0])
    big = {}
    for nm_, g_, w_, m_, v_ in (("w_in", gr_in, w_in, m_w_in, v_w_in), ("w_out", gr_out, w_out, m_w_out, v_w_out),
                                ("w_mlp1", gr_1, w_mlp1, m_w_mlp1, v_w_mlp1),
                                ("w_mlp2", gr_2, w_mlp2, m_w_mlp2, v_w_mlp2)):
        big[nm_] = _adamw_big(w_[0], g_, m_[0], v_[0], "adamw_" + nm_)
    cw = dc // N_CHIPS
    g_dww = lax.dynamic_slice(red[0:CONF_K], (0, chip * cw), (CONF_K, cw))
    g_scw = lax.dynamic_slice(red[SG_SCW:SG_SCW + SC_K], (0, chip * cw), (SC_K, cw))
    g_gfin = jnp.concatenate([red[SG_N:SG_N + 1], red[SG_N + 1:SG_N + 2]], axis=1)
    small_g = [g_bada, g_dww, red[SG_DWB:SG_DWB + 1], red[SG_LNG:SG_LNG + 1], red[SG_LNB:SG_LNB + 1], g_scw, g_gfin]
    small_w = [b_ada, conf_dw_w[0], conf_dw_b, conf_ln_g, conf_ln_b, sc_conv_w[0], g_final.reshape(1, d)]
    small_m = [m_b_ada, m_conf_dw_w[0], m_conf_dw_b, m_conf_ln_g, m_conf_ln_b, m_sc_conv_w[0], m_g_final.reshape(1, d)]
    small_v = [v_b_ada, v_conf_dw_w[0], v_conf_dw_b, v_conf_ln_g, v_conf_ln_b, v_sc_conv_w[0], v_g_final.reshape(1, d)]
    upd = _adamw_small(small_w, small_g, small_m, small_v)
    ns = len(small_w)
    s_delta, s_m, s_v = upd[0:ns], upd[ns:2 * ns], upd[2 * ns:3 * ns]

    loss = red[SG_N + 2, 0]

    def outs(kind_big, kind_small, wada):
        sm = kind_small
        return (wada[None], sm[0], kind_big["w_in"][None], sm[1][None], sm[2], sm[3], sm[4], sm[5][None],
                kind_big["w_out"][None], kind_big["w_mlp1"][None], kind_big["w_mlp2"][None], sm[6].reshape(d))

    grads_out = outs({"w_in": gr_in, "w_out": gr_out, "w_mlp1": gr_1, "w_mlp2": gr_2}, small_g, g_wada)
    delta_out = outs({k: v[0] for k, v in big.items()}, s_delta, d_wada)
    m_out = outs({k: v[1] for k, v in big.items()}, s_m, nm_wada)
    v_out = outs({k: v[2] for k, v in big.items()}, s_v, nv_wada)
    return (loss, grad_x.reshape(nb, t, d), *grads_out, *delta_out, *m_out, *v_out)
```

```python
import functools

import jax
import jax.numpy as jnp
from jax import lax
from jax.experimental import pallas as pl
from jax.experimental.pallas import tpu as pltpu

F32 = jnp.float32
BF16 = jnp.bfloat16
MESH = pl.DeviceIdType.MESH
HIGHEST = lax.Precision.HIGHEST

EPS = 1e-6
CONF_K = 31
SC_K = 3
HALO = 32
N_CHIPS = 4
N_DEV = 8

ADAM_LR = 0.001
ADAM_B1 = 0.9
ADAM_B2 = 0.999
ADAM_EPS = 1e-08
ADAM_WD = 0.01
ADAM_STEP = 10

TM_MM = 512
TM_MIX = 256
TM_MLP = 512
FF_CHUNK = 1024
TK_WG = 2048
RB_CONV = 64
RB_WG = 32
CHIP_RELS = ((1, 0), (0, 1), (1, 1))

ANY = pl.BlockSpec(memory_space=pl.ANY)
VMEM = pl.BlockSpec(memory_space=pltpu.VMEM)
HBM = pl.BlockSpec(memory_space=pltpu.HBM)
SEM = pl.BlockSpec(memory_space=pltpu.SEMAPHORE)
EFFECT = pltpu.SideEffectType.DATAFLOW_SIDE_EFFECTING


def _me():
    return lax.axis_index("x"), lax.axis_index("y"), lax.axis_index("c")


def _flip(v, f):
    return 1 - v if f else v


def _rows8(v):
    r, c = v.shape
    return v.reshape(r // 8, 8, c).sum(axis=0)


def _rms(x):
    r = lax.rsqrt(jnp.mean(x * x, axis=-1, keepdims=True) + EPS)
    return x * r, r


def _rms_bwd(dxn, xn, r):
    return r * (dxn - xn * jnp.mean(dxn * xn, axis=-1, keepdims=True))


def _sigmoid(x):
    return 1.0 / (1.0 + jnp.exp(-x))


def _dot(a, b):
    return jnp.dot(a, b, preferred_element_type=F32)


def _dot_nt(a, b):
    return lax.dot_general(a, b, (((1,), (1,)), ((), ())), preferred_element_type=F32)


def _dot_tn(a, b):
    return lax.dot_general(a, b, (((0,), (0,)), ((), ())), preferred_element_type=F32)


def _const(shape):
    nd = len(shape)
    return pl.BlockSpec(shape, lambda i: (0,) * nd)


def _resident(shape):
    nd = len(shape)
    return pl.BlockSpec(shape, lambda i: (0,) * nd, pipeline_mode=pl.Buffered(1))


def _rowblk(tm, c):
    return pl.BlockSpec((tm, c), lambda i: (i, 0))


def _modspec(tps, width):
    return pl.BlockSpec((None, 1, width), lambda i: (i // tps, 0, 0))


def _accspec(tps, c):
    return pl.BlockSpec((None, 8, c), lambda i: (i // tps, 0, 0))


def _acc_add(ref, first, v):
    @pl.when(first)
    def _():
        ref[...] = v

    @pl.when(jnp.logical_not(first))
    def _():
        ref[...] += v


def _all_gather8(arrs, name):
    n = len(arrs)

    def body(*refs):
        ins, outs = refs[:n], refs[n:2 * n]
        send, recv = refs[2 * n:]
        x, y, c = _me()
        dev = 4 * x + 2 * y + c
        for a in range(n):
            outs[a][dev] = ins[a][...]
        sends = []
        for r in range(1, N_DEV):
            fx, fy, fc = (r >> 2) & 1, (r >> 1) & 1, r & 1
            peer = (_flip(x, fx), _flip(y, fy), _flip(c, fc))
            for a in range(n):
                cp = pltpu.make_async_remote_copy(
                    src_ref=ins[a], dst_ref=outs[a].at[dev],
                    send_sem=send.at[r - 1, a], recv_sem=recv.at[r - 1, a],
                    device_id=peer, device_id_type=MESH)
                cp.start()
                sends.append(cp)
        for r in range(1, N_DEV):
            fx, fy, fc = (r >> 2) & 1, (r >> 1) & 1, r & 1
            pdev = 4 * _flip(x, fx) + 2 * _flip(y, fy) + _flip(c, fc)
            for a in range(n):
                pltpu.make_async_remote_copy(
                    src_ref=ins[a], dst_ref=outs[a].at[pdev],
                    send_sem=send.at[r - 1, a], recv_sem=recv.at[r - 1, a],
                    device_id=(x, y, c), device_id_type=MESH).wait_recv()
        for cp in sends:
            cp.wait_send()

    return pl.pallas_call(
        body, name=name,
        out_shape=[jax.ShapeDtypeStruct((N_DEV,) + a.shape, a.dtype) for a in arrs],
        in_specs=[VMEM] * n, out_specs=[VMEM] * n,
        scratch_shapes=[pltpu.SemaphoreType.DMA((N_DEV - 1, n)),
                        pltpu.SemaphoreType.DMA((N_DEV - 1, n))],
    )(*arrs)


def _gather_mod(mod_shard):
    nb, w = mod_shard.shape

    def body(in_ref, out_ref, send, recv):
        x, y, c = _me()
        chip = 2 * x + y
        out_ref[:, pl.ds(pl.multiple_of(chip * w, 128), w)] = in_ref[...]
        sends = []
        for r, (fx, fy) in enumerate(CHIP_RELS):
            cp = pltpu.make_async_remote_copy(
                src_ref=in_ref,
                dst_ref=out_ref.at[:, pl.ds(pl.multiple_of(chip * w, 128), w)],
                send_sem=send.at[r], recv_sem=recv.at[r],
                device_id=(_flip(x, fx), _flip(y, fy), c), device_id_type=MESH)
            cp.start()
            sends.append(cp)
        for r, (fx, fy) in enumerate(CHIP_RELS):
            pchip = 2 * _flip(x, fx) + _flip(y, fy)
            pltpu.make_async_remote_copy(
                src_ref=in_ref,
                dst_ref=out_ref.at[:, pl.ds(pl.multiple_of(pchip * w, 128), w)],
                send_sem=send.at[r], recv_sem=recv.at[r],
                device_id=(x, y, c), device_id_type=MESH).wait_recv()
        for cp in sends:
            cp.wait_send()

    return pl.pallas_call(
        body, name="gather_mod",
        out_shape=jax.ShapeDtypeStruct((nb, N_CHIPS * w), mod_shard.dtype),
        in_specs=[VMEM], out_specs=VMEM,
        scratch_shapes=[pltpu.SemaphoreType.DMA((3,)), pltpu.SemaphoreType.DMA((3,))],
    )(mod_shard)


def _wlayout(d):
    d_in = 5 * d // 2
    return (
        (d, d_in // N_CHIPS, True),
        (d // N_CHIPS, d, False),
        (d, 4 * d // N_CHIPS, True),
        (4 * d // N_CHIPS, d, False),
    )


def _full_shape(lay):
    r, c, by_col = lay
    return (r, c * N_CHIPS) if by_col else (r * N_CHIPS, c)


def _full_view(ref, lay, k, h):
    r, c, by_col = lay
    hr = r // 2
    if by_col:
        return ref.at[pl.ds(pl.multiple_of(h * hr, 16), hr), pl.ds(pl.multiple_of(k * c, 128), c)]
    return ref.at[pl.ds(pl.multiple_of(k * r + h * hr, 16), hr), :]


def _half_view(ref, lay, h):
    hr = lay[0] // 2
    return ref.at[pl.ds(pl.multiple_of(h * hr, 16), hr), :]


def _half_shape(lay):
    return (lay[0] // 2, lay[1])


def _hbm(a):
    return pltpu.with_memory_space_constraint(a, pltpu.HBM)


def _remote(src, dst, send, recv, idx, to):
    return lambda: pltpu.make_async_remote_copy(src_ref=src, dst_ref=dst, send_sem=send.at[idx], recv_sem=recv.at[idx],
                                                device_id=to, device_id_type=MESH)


def _copy_start(name, bufs, n_sems, plan, after=()):
    nb, na = len(bufs), len(after)

    def body(*refs):
        sends, _ = plan(refs[:nb], refs[nb + na], refs[nb + na + 1])
        for mk in sends:
            mk().start()
        refs[-1][...] = jnp.zeros((8, 128), F32)

    outs = pl.pallas_call(
        body, name=name,
        out_shape=(pltpu.SemaphoreType.DMA((n_sems,)), pltpu.SemaphoreType.DMA((n_sems,)))
        + tuple(pltpu.HBM(b.shape, b.dtype) for b in bufs) + (jax.ShapeDtypeStruct((8, 128), F32),),
        in_specs=(HBM,) * nb + (ANY,) * na, out_specs=(SEM, SEM) + (HBM,) * nb + (VMEM,),
        input_output_aliases={i: 2 + i for i in range(nb)},
        compiler_params=pltpu.CompilerParams(has_side_effects=EFFECT),
    )(*[_hbm(b) for b in bufs], *after)
    return (outs[0], outs[1]), list(outs[2:2 + nb]), outs[-1]


def _copy_wait(name, bufs, sems, plan, after):
    nb, na = len(bufs), len(after)

    def body(*refs):
        sends, recvs = plan(refs[:nb], refs[nb], refs[nb + 1])
        for mk in sends:
            mk().wait_send()
        for mk in recvs:
            mk().wait_recv()

    outs = pl.pallas_call(
        body, name=name,
        out_shape=tuple(pltpu.HBM(b.shape, b.dtype) for b in bufs),
        in_specs=(HBM,) * nb + (SEM, SEM) + (ANY,) * na, out_specs=(HBM,) * nb,
        input_output_aliases={i: i for i in range(nb)},
        compiler_params=pltpu.CompilerParams(has_side_effects=EFFECT),
    )(*bufs, *sems, *after)
    return list(outs)


def _copy_blocking(name, bufs, n_sems, plan, after=()):
    nb, na = len(bufs), len(after)

    def body(*refs):
        sends, recvs = plan(refs[:nb], refs[2 * nb + na], refs[2 * nb + na + 1])
        started = [mk() for mk in sends]
        for cp in started:
            cp.start()
        for mk in recvs:
            mk().wait_recv()
        for cp in started:
            cp.wait_send()

    return list(pl.pallas_call(
        body, name=name,
        out_shape=tuple(jax.ShapeDtypeStruct(b.shape, b.dtype) for b in bufs),
        in_specs=(ANY,) * (nb + na), out_specs=(ANY,) * nb,
        input_output_aliases={i: i for i in range(nb)},
        scratch_shapes=[pltpu.SemaphoreType.DMA((n_sems,)), pltpu.SemaphoreType.DMA((n_sems,))],
    )(*bufs, *after))


def _exchange(name, bufs, n_sems, plan, between):
    if between is None:
        return _copy_blocking(name, bufs, n_sems, plan)
    sems, bufs, tok = _copy_start(name + "_start", bufs, n_sems, plan)
    return _copy_wait(name + "_wait", bufs, sems, plan, between(tok))


def _gather_direct_plan(lays):
    def plan(full, send, recv):
        x, y, c = _me()
        chip = 2 * x + y
        sends, recvs = [], []
        for r, (fx, fy) in enumerate(CHIP_RELS):
            px, py = _flip(x, fx), _flip(y, fy)
            for i, lay in enumerate(lays):
                for q in range(2):
                    oc = _flip(c, q)
                    mine = _full_view(full[i], lay, chip, c)
                    idx = (r * len(lays) + i) * 2 + q
                    sends.append(_remote(mine, mine, send, recv, idx, (px, py, oc)))
                    theirs = _full_view(full[i], lay, 2 * px + py, oc)
                    recvs.append(_remote(theirs, theirs, send, recv, idx, (x, y, c)))
        return sends, recvs
    return plan


def _gather_chip_plan(lays):
    def plan(full, send, recv):
        x, y, c = _me()
        chip = 2 * x + y
        sends, recvs = [], []
        for r, (fx, fy) in enumerate(CHIP_RELS):
            px, py = _flip(x, fx), _flip(y, fy)
            for i, lay in enumerate(lays):
                mine = _full_view(full[i], lay, chip, c)
                sends.append(_remote(mine, mine, send, recv, r * len(lays) + i, (px, py, c)))
                theirs = _full_view(full[i], lay, 2 * px + py, c)
                recvs.append(_remote(theirs, theirs, send, recv, r * len(lays) + i, (x, y, c)))
        return sends, recvs
    return plan


def _gather_pass_plan(lays):
    def plan(full, send, recv):
        x, y, c = _me()
        sends, recvs = [], []
        for r, (fx, fy) in enumerate(CHIP_RELS):
            pchip = 2 * _flip(x, fx) + _flip(y, fy)
            for i, lay in enumerate(lays):
                landed = _full_view(full[i], lay, pchip, c)
                sends.append(_remote(landed, landed, send, recv, r * len(lays) + i, (x, y, 1 - c)))
                other = _full_view(full[i], lay, pchip, 1 - c)
                recvs.append(_remote(other, other, send, recv, r * len(lays) + i, (x, y, c)))
        return sends, recvs
    return plan


def _pair_exchange_plan(lays):
    nw = len(lays)

    def plan(bufs, send, recv):
        x, y, c = _me()
        sends, recvs = [], []
        for i, lay in enumerate(lays):
            for k in range(N_CHIPS):
                sends.append(_remote(_full_view(bufs[i], lay, k, 1 - c), bufs[nw + i].at[k],
                                     send, recv, i * N_CHIPS + k, (x, y, 1 - c)))
                recvs.append(_remote(_full_view(bufs[i], lay, k, c), bufs[nw + i].at[k],
                                     send, recv, i * N_CHIPS + k, (x, y, c)))
        return sends, recvs
    return plan


def _chip_exchange_plan(nw):
    def plan(bufs, send, recv):
        x, y, c = _me()
        sends, recvs = [], []
        for r, (fx, fy) in enumerate(CHIP_RELS):
            px, py = _flip(x, fx), _flip(y, fy)
            for i in range(nw):
                sends.append(_remote(bufs[i].at[2 * px + py], bufs[nw + i].at[r], send, recv, r * nw + i, (px, py, c)))
                recvs.append(_remote(bufs[i].at[2 * px + py], bufs[nw + i].at[r], send, recv, r * nw + i, (x, y, c)))
        return sends, recvs
    return plan


def _pair_share_plan(lays):
    def plan(bufs, send, recv):
        x, y, c = _me()
        sends, recvs = [], []
        for i, lay in enumerate(lays):
            mine = _half_view(bufs[i], lay, c)
            sends.append(_remote(mine, mine, send, recv, i, (x, y, 1 - c)))
            other = _half_view(bufs[i], lay, 1 - c)
            recvs.append(_remote(other, other, send, recv, i, (x, y, c)))
        return sends, recvs
    return plan


def _pair_sum(ids, g, got, lay, name):
    r, c, by_col = lay
    hr = r // 2
    tr = min(hr, 256)
    nt = hr // tr

    def body(ids_ref, g_ref, got_ref, s32_ref, s16_ref):
        s = g_ref[...] + got_ref[...]
        s32_ref[...] = s
        s16_ref[...] = s.astype(BF16)

    if by_col:
        gspec = pl.BlockSpec((tr, c), lambda k, t, ids: (ids[1] * nt + t, k))
    else:
        gspec = pl.BlockSpec((tr, c), lambda k, t, ids: ((2 * k + ids[1]) * nt + t, 0))
    hspec = pl.BlockSpec((None, tr, c), lambda k, t, ids: (k, t, 0))
    return pl.pallas_call(
        body, name=name,
        grid_spec=pltpu.PrefetchScalarGridSpec(
            num_scalar_prefetch=1, grid=(N_CHIPS, nt),
            in_specs=[gspec, hspec], out_specs=[hspec, hspec]),
        out_shape=[jax.ShapeDtypeStruct((N_CHIPS, hr, c), F32),
                   jax.ShapeDtypeStruct((N_CHIPS, hr, c), BF16)],
    )(ids, g, got)


def _chip_sum(ids, s32, got, lay, name):
    hr, c = _half_shape(lay)
    tr = min(hr, 256)
    nt = hr // tr

    def body(ids_ref, s_ref, got_ref, out_ref):
        t = s_ref[...]
        for r in range(3):
            t = t + got_ref[r].astype(F32)
        out_ref[...] = t

    return pl.pallas_call(
        body, name=name,
        grid_spec=pltpu.PrefetchScalarGridSpec(
            num_scalar_prefetch=1, grid=(nt,),
            in_specs=[pl.BlockSpec((None, tr, c), lambda t, ids: (ids[0], t, 0)),
                      pl.BlockSpec((3, tr, c), lambda t, ids: (0, t, 0))],
            out_specs=pl.BlockSpec((tr, c), lambda t, ids: (ids[1] * nt + t, 0))),
        out_shape=jax.ShapeDtypeStruct((2 * hr, c), F32),
    )(ids, s32, got)


def _reduce_scatter(ids, grads, lays, names, tag, between):
    nw = len(lays)
    got1 = [lax.empty((N_CHIPS,) + _half_shape(l), F32) for l in lays]
    bufs = _exchange("pair_exchange_" + tag, list(grads) + got1, nw * N_CHIPS, _pair_exchange_plan(lays), between[0])
    sums = [_pair_sum(ids, bufs[i], bufs[nw + i], lays[i], "pair_sum_" + names[i]) for i in range(nw)]
    got2 = [lax.empty((3,) + _half_shape(l), BF16) for l in lays]
    bufs = _exchange("chip_exchange_" + tag, [s[1] for s in sums] + got2, 3 * nw, _chip_exchange_plan(nw), between[1])
    mine = [_chip_sum(ids, sums[i][0], bufs[nw + i], lays[i], "chip_sum_" + names[i]) for i in range(nw)]
    return _exchange("pair_share_" + tag, mine, nw, _pair_share_plan(lays), between[2])


def _cast_place(ids, w, lay, name):
    r, c, by_col = lay

    def body(ids_ref, w_ref, out_ref):
        out_ref[...] = w_ref[...].astype(BF16)

    omap = (lambda i, ids: (0, ids[0])) if by_col else (lambda i, ids: (ids[0], 0))
    return pl.pallas_call(
        body, name=name,
        grid_spec=pltpu.PrefetchScalarGridSpec(
            num_scalar_prefetch=1, grid=(1,),
            in_specs=[pl.BlockSpec((r, c), lambda i, ids: (0, 0))],
            out_specs=pl.BlockSpec((r, c), omap)),
        out_shape=jax.ShapeDtypeStruct(_full_shape(lay), BF16),
    )(ids, w)


def _ada_mod(c_all, w_ada, b_ada):
    def body(c_ref, w_ref, b_ref, act_ref, mod_ref):
        cv = c_ref[...]
        act = cv * _sigmoid(cv)
        act_ref[...] = act
        mod_ref[...] = jnp.dot(act, w_ref[...], preferred_element_type=F32, precision=HIGHEST) + b_ref[...]

    nb = c_all.shape[0]
    return pl.pallas_call(
        body, name="ada_mod",
        out_shape=[jax.ShapeDtypeStruct(c_all.shape, F32),
                   jax.ShapeDtypeStruct((nb, w_ada.shape[1]), F32)],
        in_specs=[VMEM] * 3, out_specs=[VMEM] * 2,
    )(c_all, w_ada, b_ada)


def _in_proj(x, mod3, w_in, t, dep):
    n, d = x.shape
    d_in = w_in.shape[1]
    tm = min(TM_MM, t)
    tps = t // tm

    def body(x_ref, mod_ref, w_ref, dep_ref, proj_ref, h_ref):
        xn, _ = _rms(x_ref[...])
        h = (xn * (1.0 + mod_ref[:, d:2 * d]) + mod_ref[:, 0:d]).astype(BF16)
        h_ref[...] = h
        proj_ref[...] = _dot(h, w_ref[...])

    return pl.pallas_call(
        body, name="in_proj", grid=(n // tm,),
        in_specs=[_rowblk(tm, d), _modspec(tps, 6 * d), _resident((d, d_in)), ANY],
        out_specs=[_rowblk(tm, d_in), _rowblk(tm, d)],
        out_shape=[jax.ShapeDtypeStruct((n, d_in), F32), jax.ShapeDtypeStruct((n, d), BF16)],
        compiler_params=pltpu.CompilerParams(dimension_semantics=("parallel",)),
    )(x, mod3, w_in, dep)


def _shifted_copies(ext_ref, e_ref, rows):
    for g in range(e_ref.shape[0]):
        for r in range(8):
            e_ref[g, r, 0:rows, :] = ext_ref[r:r + rows, 128 * g:128 * g + 128]


def _shift_scratch(tm, dc):
    return pltpu.VMEM((dc // 128, 8, tm + HALO + 8, 128), F32)


def _mixer_fwd(proj, wb, dwb, lng, lnb, scw, t):
    n, d_in = proj.shape
    dc = d_in // 5
    tm = min(TM_MIX, t)
    tps = t // tm
    hb = tm // HALO
    rb = min(RB_CONV, tm)

    def body(p_ref, ph_ref, wb_ref, dwb_ref, lng_ref, lnb_ref, scw_ref,
             mixed_ref, a1_ref, q_ref, ext_ref, e_ref, extp_ref):
        first = pl.program_id(0) % tps == 0
        keep = jnp.where(first, 0.0, 1.0)
        a0 = p_ref[:, 0:dc] * _sigmoid(p_ref[:, dc:2 * dc])
        a0h = ph_ref[:, 0:dc] * _sigmoid(ph_ref[:, dc:2 * dc]) * keep
        ext_ref[0:HALO, :] = a0h
        ext_ref[HALO:HALO + tm, :] = a0
        ext_ref[HALO + tm:HALO + tm + 8, :] = jnp.zeros((8, dc), F32)
        _shifted_copies(ext_ref, e_ref, tm + HALO)
        for g in range(dc // 128):
            lanes = slice(128 * g, 128 * g + 128)

            def rows(j, carry, lanes=lanes, g=g):
                i0 = pl.multiple_of(j * rb, rb)
                acc = jnp.zeros((rb, 128), F32)
                for k in range(CONF_K):
                    m, r = divmod(k + HALO - CONF_K + 1, 8)
                    acc = acc + e_ref[g, r, pl.ds(i0 + 8 * m, rb), :] * wb_ref[k:k + 1, lanes]
                a1_ref[pl.ds(i0, rb), lanes] = acc + dwb_ref[:, lanes]
                return carry

            lax.fori_loop(0, tm // rb, rows, 0)
        a1 = a1_ref[...]
        mu = jnp.mean(a1, axis=-1, keepdims=True)
        ac = a1 - mu
        rstd = lax.rsqrt(jnp.mean(ac * ac, axis=-1, keepdims=True) + EPS)
        a2 = ac * rstd * lng_ref[...] + lnb_ref[...]
        mixed_ref[:, 0:dc] = (a2 * _sigmoid(a2)).astype(BF16)
        p = p_ref[:, 3 * dc:4 * dc] * p_ref[:, 4 * dc:5 * dc]
        ph = ph_ref[HALO - 8:HALO, 3 * dc:4 * dc] * ph_ref[HALO - 8:HALO, 4 * dc:5 * dc] * keep
        extp_ref[0:8, :] = ph
        extp_ref[8:8 + tm, :] = p
        q = jnp.zeros((tm, dc), F32)
        for k in range(SC_K):
            q = q + extp_ref[6 + k:6 + k + tm, :] * scw_ref[k:k + 1, :]
        q_ref[...] = q
        mixed_ref[:, dc:2 * dc] = (p_ref[:, 2 * dc:3 * dc] * q).astype(BF16)

    return pl.pallas_call(
        body, name="mixer_fwd", grid=(n // tm,),
        in_specs=[_rowblk(tm, d_in),
                  pl.BlockSpec((HALO, d_in), lambda i: (jnp.maximum(i * hb - 1, 0), 0)),
                  _const(wb.shape), _const(dwb.shape), _const(lng.shape), _const(lnb.shape), _const(scw.shape)],
        out_specs=[_rowblk(tm, 2 * dc), _rowblk(tm, dc), _rowblk(tm, dc)],
        out_shape=[jax.ShapeDtypeStruct((n, 2 * dc), BF16), jax.ShapeDtypeStruct((n, dc), F32),
                   jax.ShapeDtypeStruct((n, dc), F32)],
        scratch_shapes=[pltpu.VMEM((tm + HALO + 8, dc), F32), _shift_scratch(tm, dc),
                        pltpu.VMEM((tm + 8, dc), F32)],
        compiler_params=pltpu.CompilerParams(dimension_semantics=("parallel",)),
    )(proj, proj, wb, dwb, lng, lnb, scw)


def _out_proj(mixed, w_out, x, mod3, t, dep):
    n, d = x.shape
    tm = min(TM_MM, t)
    tps = t // tm

    def body(m_ref, w_ref, x_ref, mod_ref, dep_ref, x1_ref, y1_ref, h2_ref):
        y1 = _dot(m_ref[...], w_ref[...])
        y1_ref[...] = y1
        x1 = x_ref[...] + mod_ref[:, 2 * d:3 * d] * y1
        x1_ref[...] = x1
        xn, _ = _rms(x1)
        h2_ref[...] = (xn * (1.0 + mod_ref[:, 4 * d:5 * d]) + mod_ref[:, 3 * d:4 * d]).astype(BF16)

    return pl.pallas_call(
        body, name="out_proj", grid=(n // tm,),
        in_specs=[_rowblk(tm, d), _resident((d, d)), _rowblk(tm, d), _modspec(tps, 6 * d), ANY],
        out_specs=[_rowblk(tm, d), _rowblk(tm, d), _rowblk(tm, d)],
        out_shape=[jax.ShapeDtypeStruct((n, d), F32), jax.ShapeDtypeStruct((n, d), F32),
                   jax.ShapeDtypeStruct((n, d), BF16)],
        compiler_params=pltpu.CompilerParams(dimension_semantics=("parallel",)),
    )(mixed, w_out, x, mod3, dep)


def _mlp_fwd(h2, x1, tgt, mod3, gfin, w1, w2, t):
    n, d = x1.shape
    dff = w1.shape[1]
    tm = min(TM_MLP, t)
    tps = t // tm
    nt = n // tm

    def body(h_ref, x1_ref, tg_ref, mod_ref, gf_ref, w1_ref, w2_ref,
             z_ref, dx2_ref, dy2_ref, dg2_ref, ggf_ref, loss_ref):
        i = pl.program_id(0)
        hv = h_ref[...]
        y2 = jnp.zeros((tm, d), F32)
        for j in range(dff // FF_CHUNK):
            cols = slice(j * FF_CHUNK, (j + 1) * FF_CHUNK)
            z = _dot(hv, w1_ref[:, cols])
            z_ref[:, cols] = z.astype(BF16)
            zr = jnp.maximum(z, 0.0)
            y2 = y2 + _dot((zr * zr).astype(BF16), w2_ref[cols, :])
        g2 = mod_ref[:, 5 * d:6 * d]
        x2n, r3 = _rms(x1_ref[...] + g2 * y2)
        gf = gf_ref[...]
        diff = x2n * gf - tg_ref[...]
        dout = diff * (1.0 / d)
        dx2 = _rms_bwd(dout * gf, x2n, r3)
        dx2_ref[...] = dx2
        dy2_ref[...] = (g2 * dx2).astype(BF16)
        _acc_add(dg2_ref, i % tps == 0, _rows8(dx2 * y2))
        _acc_add(ggf_ref, i == 0, _rows8(dout * x2n))
        _acc_add(loss_ref, i == 0, _rows8(diff * diff))

    return pl.pallas_call(
        body, name="mlp_fwd", grid=(nt,),
        in_specs=[_rowblk(tm, d), _rowblk(tm, d), _rowblk(tm, d), _modspec(tps, 6 * d), _const((1, d)),
                  _resident((d, dff)), _resident((dff, d))],
        out_specs=[_rowblk(tm, dff), _rowblk(tm, d), _rowblk(tm, d), _accspec(tps, d),
                   _const((8, d)), _const((8, d))],
        out_shape=[jax.ShapeDtypeStruct((n, dff), BF16), jax.ShapeDtypeStruct((n, d), F32),
                   jax.ShapeDtypeStruct((n, d), BF16), jax.ShapeDtypeStruct((n // t, 8, d), F32),
                   jax.ShapeDtypeStruct((8, d), F32), jax.ShapeDtypeStruct((8, d), F32)],
        compiler_params=pltpu.CompilerParams(dimension_semantics=("arbitrary",)),
    )(h2, x1, tgt, mod3, gfin, w1, w2)


def _mlp_bwd(dy2, z, x1, dx2, y1, mod3, w1, w2, t):
    n, d = x1.shape
    dff = w1.shape[1]
    tm = min(TM_MLP, t)
    tps = t // tm

    def body(dy2_ref, z_ref, x1_ref, dx2_ref, y1_ref, mod_ref, w1_ref, w2_ref,
             dz_ref, dx1_ref, dy1_ref, dsh_ref, dsc_ref, dg1_ref):
        first = pl.program_id(0) % tps == 0
        dy2 = dy2_ref[...]
        dh2 = jnp.zeros((tm, d), F32)
        for j in range(dff // FF_CHUNK):
            cols = slice(j * FF_CHUNK, (j + 1) * FF_CHUNK)
            du = _dot_nt(dy2, w2_ref[cols, :])
            dz = (du * (2.0 * jnp.maximum(z_ref[:, cols].astype(F32), 0.0))).astype(BF16)
            dz_ref[:, cols] = dz
            dh2 = dh2 + _dot_nt(dz, w1_ref[:, cols])
        x1n, r2 = _rms(x1_ref[...])
        _acc_add(dsh_ref, first, _rows8(dh2))
        _acc_add(dsc_ref, first, _rows8(dh2 * x1n))
        dx1 = dx2_ref[...] + _rms_bwd(dh2 * (1.0 + mod_ref[:, 4 * d:5 * d]), x1n, r2)
        dx1_ref[...] = dx1
        dy1_ref[...] = (mod_ref[:, 2 * d:3 * d] * dx1).astype(BF16)
        _acc_add(dg1_ref, first, _rows8(dx1 * y1_ref[...]))

    acc = jax.ShapeDtypeStruct((n // t, 8, d), F32)
    return pl.pallas_call(
        body, name="mlp_bwd", grid=(n // tm,),
        in_specs=[_rowblk(tm, d), _rowblk(tm, dff), _rowblk(tm, d), _rowblk(tm, d), _rowblk(tm, d),
                  _modspec(tps, 6 * d), _resident((d, dff)), _resident((dff, d))],
        out_specs=[_rowblk(tm, dff), _rowblk(tm, d), _rowblk(tm, d),
                   _accspec(tps, d), _accspec(tps, d), _accspec(tps, d)],
        out_shape=[jax.ShapeDtypeStruct((n, dff), BF16), jax.ShapeDtypeStruct((n, d), F32),
                   jax.ShapeDtypeStruct((n, d), BF16), acc, acc, acc],
        compiler_params=pltpu.CompilerParams(dimension_semantics=("arbitrary",)),
    )(dy2, z, x1, dx2, y1, mod3, w1, w2)


def _wgrad(a, b, name, relu2=False, bn=None, deps=()):
    n, ka = a.shape
    nb = b.shape[1]
    tk = min(TK_WG, n)
    bm = min(ka, 1024)
    if bn is None:
        bn = nb if nb <= 2048 else nb // 2

    def body(a_ref, b_ref, *rest):
        out_ref = rest[-1]
        av = a_ref[...]
        if relu2:
            ar = jnp.maximum(av, 0.0)
            av = ar * ar
        p = _dot_tn(av, b_ref[...])
        _acc_add(out_ref, pl.program_id(2) == 0, p)

    return pl.pallas_call(
        body, name=name, grid=(ka // bm, nb // bn, n // tk),
        in_specs=[pl.BlockSpec((tk, bm), lambda i, j, k: (k, i)),
                  pl.BlockSpec((tk, bn), lambda i, j, k: (k, j))] + [ANY] * len(deps),
        out_specs=pl.BlockSpec((bm, bn), lambda i, j, k: (i, j)),
        out_shape=jax.ShapeDtypeStruct((ka, nb), F32),
        compiler_params=pltpu.CompilerParams(dimension_semantics=("parallel", "parallel", "arbitrary")),
    )(a, b, *deps)


def _out_proj_bwd(dy1, w_out, t, dep):
    n, d = dy1.shape
    tm = min(TM_MM, t)

    def body(dy_ref, w_ref, dep_ref, dm_ref):
        dm_ref[...] = _dot_nt(dy_ref[...], w_ref[...])

    return pl.pallas_call(
        body, name="out_proj_bwd", grid=(n // tm,),
        in_specs=[_rowblk(tm, d), _resident(w_out.shape), ANY],
        out_specs=_rowblk(tm, w_out.shape[0]),
        out_shape=jax.ShapeDtypeStruct((n, w_out.shape[0]), F32),
        compiler_params=pltpu.CompilerParams(dimension_semantics=("parallel",)),
    )(dy1, w_out, dep)


SG_DWW = 0
SG_DWB = CONF_K
SG_LNG = CONF_K + 1
SG_LNB = CONF_K + 2
SG_SCW = CONF_K + 3
SG_N = CONF_K + 3 + SC_K


def _mixer_bwd(dmixed, a1, q, proj, wb, lng, lnb, scw, t, dep):
    n, d_in = proj.shape
    dc = d_in // 5
    tm = min(TM_MIX, t)
    tps = t // tm
    hb = tm // HALO
    nh = n // HALO
    rw = min(RB_WG, tm)

    def ln_bwd(a1v, da3, lng_v, lnb_v):
        mu = jnp.mean(a1v, axis=-1, keepdims=True)
        ac = a1v - mu
        rstd = lax.rsqrt(jnp.mean(ac * ac, axis=-1, keepdims=True) + EPS)
        ah = ac * rstd
        a2 = ah * lng_v + lnb_v
        s2 = _sigmoid(a2)
        da2 = da3 * (s2 * (1.0 + a2 * (1.0 - s2)))
        dah = da2 * lng_v
        da1 = rstd * (dah - jnp.mean(dah, axis=-1, keepdims=True)
                      - ah * jnp.mean(dah * ah, axis=-1, keepdims=True))
        return da1, da2, ah

    def body(dm_ref, dmn_ref, a1_ref, a1n_ref, q_ref, p_ref, pn_ref,
             wb_ref, lng_ref, lnb_ref, scw_ref, dep_ref, dproj_ref, sg_ref,
             extd_ref, ed_ref, a0_ref, da0_ref, extq_ref):
        i = pl.program_id(0)
        last = i % tps == tps - 1
        keep_next = jnp.where(last, 0.0, 1.0)

        @pl.when(i == 0)
        def _():
            sg_ref[...] = jnp.zeros(sg_ref.shape, F32)

        lng_v, lnb_v = lng_ref[...], lnb_ref[...]
        val, sig = p_ref[:, 0:dc], _sigmoid(p_ref[:, dc:2 * dc])
        a0 = val * sig
        da1, da2, ah = ln_bwd(a1_ref[...], dm_ref[:, 0:dc], lng_v, lnb_v)
        da1n, _, _ = ln_bwd(a1n_ref[...], dmn_ref[:, 0:dc], lng_v, lnb_v)
        sg_ref[8 * SG_LNG:8 * SG_LNG + 8, :] += _rows8(da2 * ah)
        sg_ref[8 * SG_LNB:8 * SG_LNB + 8, :] += _rows8(da2)
        sg_ref[8 * SG_DWB:8 * SG_DWB + 8, :] += _rows8(da1)
        a0_ref[...] = a0
        extd_ref[0:tm, :] = da1
        extd_ref[tm:tm + HALO, :] = da1n * keep_next
        extd_ref[tm + HALO:tm + HALO + 8, :] = jnp.zeros((8, dc), F32)
        _shifted_copies(extd_ref, ed_ref, tm + HALO)
        for g in range(dc // 128):
            lanes = slice(128 * g, 128 * g + 128)

            def rows(j, accs, lanes=lanes, g=g):
                i0 = pl.multiple_of(j * rw, rw)
                a0v = a0_ref[pl.ds(i0, rw), lanes]
                acc = jnp.zeros((rw, 128), F32)
                new = []
                for s in range(CONF_K):
                    m, r = divmod(s, 8)
                    e = ed_ref[g, r, pl.ds(i0 + 8 * m, rw), :]
                    acc = acc + e * wb_ref[CONF_K - 1 - s:CONF_K - s, lanes]
                    new.append(accs[s] + _rows8(e * a0v))
                da0_ref[pl.ds(i0, rw), lanes] = acc
                return tuple(new)

            accs = lax.fori_loop(0, tm // rw, rows,
                                 tuple(jnp.zeros((8, 128), F32) for _ in range(CONF_K)))
            for s in range(CONF_K):
                k = CONF_K - 1 - s
                sg_ref[8 * (SG_DWW + k):8 * (SG_DWW + k) + 8, lanes] += accs[s]
        da0 = da0_ref[...]
        dproj_ref[:, 0:dc] = (da0 * sig).astype(BF16)
        dproj_ref[:, dc:2 * dc] = (da0 * a0 * (1.0 - sig)).astype(BF16)
        ds = dm_ref[:, dc:2 * dc]
        scb, scc, sch = p_ref[:, 2 * dc:3 * dc], p_ref[:, 3 * dc:4 * dc], p_ref[:, 4 * dc:5 * dc]
        dproj_ref[:, 2 * dc:3 * dc] = (ds * q_ref[...]).astype(BF16)
        dq = ds * scb
        extq_ref[0:tm, :] = dq
        extq_ref[tm:tm + 8, :] = dmn_ref[0:8, dc:2 * dc] * pn_ref[0:8, :] * keep_next
        pv = scc * sch
        dp = jnp.zeros((tm, dc), F32)
        for k in range(SC_K):
            dqs = extq_ref[SC_K - 1 - k:SC_K - 1 - k + tm, :]
            dp = dp + dqs * scw_ref[k:k + 1, :]
            sg_ref[8 * (SG_SCW + k):8 * (SG_SCW + k) + 8, :] += _rows8(pv * dqs)
        dproj_ref[:, 3 * dc:4 * dc] = (dp * sch).astype(BF16)
        dproj_ref[:, 4 * dc:5 * dc] = (dp * scc).astype(BF16)

    nxt = lambda i: (jnp.minimum((i + 1) * hb, nh - 1), 0)
    return pl.pallas_call(
        body, name="mixer_bwd", grid=(n // tm,),
        in_specs=[_rowblk(tm, 2 * dc), pl.BlockSpec((HALO, 2 * dc), nxt),
                  _rowblk(tm, dc), pl.BlockSpec((HALO, dc), nxt),
                  _rowblk(tm, dc),
                  _rowblk(tm, d_in),
                  pl.BlockSpec((HALO, dc), lambda i: (jnp.minimum((i + 1) * hb, nh - 1), 2)),
                  _const(wb.shape), _const(lng.shape), _const(lnb.shape), _const(scw.shape), ANY],
        out_specs=[_rowblk(tm, d_in), _const((8 * SG_N, dc))],
        out_shape=[jax.ShapeDtypeStruct((n, d_in), BF16), jax.ShapeDtypeStruct((8 * SG_N, dc), F32)],
        scratch_shapes=[pltpu.VMEM((tm + HALO + 8, dc), F32), _shift_scratch(tm, dc),
                        pltpu.VMEM((tm, dc), F32), pltpu.VMEM((tm, dc), F32),
                        pltpu.VMEM((tm + 8, dc), F32)],
        compiler_params=pltpu.CompilerParams(dimension_semantics=("arbitrary",)),
    )(dmixed, dmixed, a1, a1, q, proj, proj, wb, lng, lnb, scw, dep)


def _in_proj_bwd(dproj, w_in, x, dx1, mod3, t, dep):
    n, d = x.shape
    d_in = w_in.shape[1]
    tm = min(TM_MM, t)
    tps = t // tm

    def body(dp_ref, w_ref, x_ref, dx1_ref, mod_ref, dep_ref, gx_ref, dsh_ref, dsc_ref):
        first = pl.program_id(0) % tps == 0
        dh1 = _dot_nt(dp_ref[...], w_ref[...])
        xn, r1 = _rms(x_ref[...])
        _acc_add(dsh_ref, first, _rows8(dh1))
        _acc_add(dsc_ref, first, _rows8(dh1 * xn))
        gx_ref[...] = dx1_ref[...] + _rms_bwd(dh1 * (1.0 + mod_ref[:, d:2 * d]), xn, r1)

    acc = jax.ShapeDtypeStruct((n // t, 8, d), F32)
    return pl.pallas_call(
        body, name="in_proj_bwd", grid=(n // tm,),
        in_specs=[_rowblk(tm, d_in), _resident((d, d_in)), _rowblk(tm, d), _rowblk(tm, d),
                  _modspec(tps, 6 * d), ANY],
        out_specs=[_rowblk(tm, d), _accspec(tps, d), _accspec(tps, d)],
        out_shape=[jax.ShapeDtypeStruct((n, d), F32), acc, acc],
        compiler_params=pltpu.CompilerParams(dimension_semantics=("arbitrary",)),
    )(dproj, w_in, x, dx1, mod3, dep)


SMALL_ROWS = 40


def _pack_small(sg, ggf, loss, accs, d, dep):
    dc = d // 2
    nb = accs[0].shape[0]

    def body(sg_ref, ggf_ref, loss_ref, dsh1, dsc1, dg1, dsh2, dsc2, dg2, dep_ref, pack_ref, dmod_ref):
        pack_ref[...] = jnp.zeros(pack_ref.shape, F32)
        for k in range(SG_N):
            pack_ref[k:k + 1, :] = jnp.sum(sg_ref[8 * k:8 * k + 8, :], axis=0, keepdims=True)
        gf = jnp.sum(ggf_ref[...], axis=0, keepdims=True)
        pack_ref[SG_N:SG_N + 1, :] = gf[:, 0:dc]
        pack_ref[SG_N + 1:SG_N + 2, :] = gf[:, dc:d]
        tot = jnp.sum(jnp.sum(loss_ref[...], axis=0, keepdims=True), axis=1, keepdims=True) * (0.5 / d)
        pack_ref[SG_N + 2:SG_N + 3, :] = jnp.broadcast_to(tot, (1, dc))
        dmod_ref[...] = jnp.zeros(dmod_ref.shape, F32)
        for j, ref in enumerate((dsh1, dsc1, dg1, dsh2, dsc2, dg2)):
            for b in range(nb):
                dmod_ref[b:b + 1, j * d:(j + 1) * d] = jnp.sum(ref[b], axis=0, keepdims=True)

    return pl.pallas_call(
        body, name="pack_small",
        out_shape=[jax.ShapeDtypeStruct((SMALL_ROWS, dc), F32), jax.ShapeDtypeStruct((8, 6 * d), F32)],
        in_specs=[VMEM] * 9 + [ANY], out_specs=[VMEM] * 2,
    )(sg, ggf, loss, *accs, dep)


def _small_reduce(pack_all, dmod_all, nb):
    def body(pk_ref, dm_ref, red_ref, dmod_ref, gb_ref):
        tot = pk_ref[0]
        for dev in range(1, N_DEV):
            tot = tot + pk_ref[dev]
        red_ref[...] = tot
        gb = jnp.zeros((1, dm_ref.shape[2]), F32)
        for dev in range(N_DEV):
            for b in range(nb):
                row = dm_ref[dev, b:b + 1, :]
                dmod_ref[dev * nb + b:dev * nb + b + 1, :] = row
                gb = gb + row
        gb_ref[...] = gb

    return pl.pallas_call(
        body, name="small_reduce",
        out_shape=[jax.ShapeDtypeStruct(pack_all.shape[1:], F32),
                   jax.ShapeDtypeStruct((N_DEV * nb, dmod_all.shape[2]), F32),
                   jax.ShapeDtypeStruct((1, dmod_all.shape[2]), F32)],
        in_specs=[VMEM] * 2, out_specs=[VMEM] * 3,
    )(pack_all, dmod_all)


def _adam(w, g, m, v):
    m = ADAM_B1 * m + (1.0 - ADAM_B1) * g
    v = ADAM_B2 * v + (1.0 - ADAM_B2) * (g * g)
    m_hat = m / (1.0 - ADAM_B1 ** ADAM_STEP)
    v_hat = v / (1.0 - ADAM_B2 ** ADAM_STEP)
    delta = -ADAM_LR * (m_hat / (jnp.sqrt(v_hat) + ADAM_EPS) + ADAM_WD * w)
    return delta, m, v


def _adamw_big(w, g, m, v, name):
    r, c = w.shape
    tr = min(r, 256)

    def body(w_ref, g_ref, m_ref, v_ref, d_ref, nm_ref, nv_ref):
        d_ref[...], nm_ref[...], nv_ref[...] = _adam(w_ref[...], g_ref[...], m_ref[...], v_ref[...])

    s = jax.ShapeDtypeStruct((r, c), F32)
    return pl.pallas_call(
        body, name=name, grid=(r // tr,),
        in_specs=[_rowblk(tr, c)] * 4, out_specs=[_rowblk(tr, c)] * 3, out_shape=[s, s, s],
        compiler_params=pltpu.CompilerParams(dimension_semantics=("parallel",)),
    )(w, g, m, v)


def _adamw_ada(act_t, dmod_cols, w, m, v):
    r, c = w.shape
    tr = min(r, 256)
    nb = act_t.shape[1]

    def body(a_ref, dm_ref, w_ref, m_ref, v_ref, g_ref, d_ref, nm_ref, nv_ref):
        g = jnp.dot(a_ref[...], dm_ref[...], preferred_element_type=F32, precision=HIGHEST)
        g_ref[...] = g
        d_ref[...], nm_ref[...], nv_ref[...] = _adam(w_ref[...], g, m_ref[...], v_ref[...])

    s = jax.ShapeDtypeStruct((r, c), F32)
    return pl.pallas_call(
        body, name="adamw_w_ada", grid=(r // tr,),
        in_specs=[_rowblk(tr, nb), _const((nb, c))] + [_rowblk(tr, c)] * 3,
        out_specs=[_rowblk(tr, c)] * 4, out_shape=[s, s, s, s],
        compiler_params=pltpu.CompilerParams(dimension_semantics=("parallel",)),
    )(act_t, dmod_cols, w, m, v)


def _adamw_small(ws, gs, ms, vs):
    n = len(ws)

    def body(*refs):
        for i in range(n):
            w, g, m, v = (refs[j * n + i][...] for j in range(4))
            dl, nm, nv = _adam(w, g, m, v)
            refs[4 * n + i][...] = dl
            refs[5 * n + i][...] = nm
            refs[6 * n + i][...] = nv

    shapes = [jax.ShapeDtypeStruct(w.shape, F32) for w in ws]
    return pl.pallas_call(
        body, name="adamw_small", out_shape=shapes * 3,
        in_specs=[VMEM] * (4 * n), out_specs=[VMEM] * (3 * n),
    )(*ws, *gs, *ms, *vs)


def kernel(x, c, w_ada, b_ada, w_in, conf_dw_w, conf_dw_b, conf_ln_g, conf_ln_b, sc_conv_w, w_out, w_mlp1, w_mlp2, g_final, loss_target, m_w_ada, m_b_ada, m_w_in, m_conf_dw_w, m_conf_dw_b, m_conf_ln_g, m_conf_ln_b, m_sc_conv_w, m_w_out, m_w_mlp1, m_w_mlp2, m_g_final, v_w_ada, v_b_ada, v_w_in, v_conf_dw_w, v_conf_dw_b, v_conf_ln_g, v_conf_ln_b, v_sc_conv_w, v_w_out, v_w_mlp1, v_w_mlp2, v_g_final):
    nb, t, d = x.shape
    n = nb * t
    dc = d // 2
    ada_w = w_ada.shape[2]
    ax, ay, ac = _me()
    chip = 2 * ax + ay
    dev = 2 * chip + ac
    ids = jnp.stack([chip, ac]).astype(jnp.int32)

    lays = _wlayout(d)
    names = ("in", "out", "mlp1", "mlp2")
    fulls = [_cast_place(ids, w[0], lays[i], "cast_" + names[i]) for i, w in enumerate((w_in, w_out, w_mlp1, w_mlp2))]

    c_pad = jnp.zeros((8, d), F32).at[0:nb].set(c)
    cw_pad = jnp.zeros((SMALL_ROWS, dc // N_CHIPS), F32)
    cw_pad = cw_pad.at[0:CONF_K].set(conf_dw_w[0]).at[HALO:HALO + SC_K].set(sc_conv_w[0])
    c_all8, cw_all8 = _all_gather8([c_pad, cw_pad], "gather_c")
    plan_i = _gather_chip_plan(lays[0:1])
    sems_i, bufs_i, tok_i = _copy_start("gather_in_start", [fulls[0]], 3, plan_i, [c_all8])
    c_all = c_all8[:, 0:nb].reshape(N_DEV * nb, d) + tok_i[0, 0]
    cw_full = jnp.concatenate([cw_all8[2 * k] for k in range(N_CHIPS)], axis=1)
    dww, scw = cw_full[0:CONF_K], cw_full[HALO:HALO + SC_K]
    b_cols = lax.dynamic_slice(b_ada, (0, chip * ada_w), (1, ada_w))
    c_act, mod_shard = _ada_mod(c_all, w_ada[0], b_cols)
    mod_all = _gather_mod(mod_shard)
    mod3 = lax.dynamic_slice(mod_all, (dev * nb, 0), (nb, 6 * d)).reshape(nb, 1, 6 * d)

    x2 = x.reshape(n, d)
    tgt = loss_target.reshape(n, d)
    (wf_in,) = _copy_wait("gather_in_wait", bufs_i, sems_i, plan_i, [mod3])
    (wf_in,) = _copy_blocking("gather_in_pass", [wf_in], 3, _gather_pass_plan(lays[0:1]))
    plan_o, plan_b, plan_p = _gather_direct_plan(lays[1:2]), _gather_chip_plan(lays[2:4]), _gather_pass_plan(lays[2:4])
    sems_o, bufs_o, tok_o = _copy_start("gather_out_start", [fulls[1]], 6, plan_o, [wf_in, mod3])
    sems_b, bufs_b, tok_b = _copy_start("gather_mlp_start", fulls[2:4], 6, plan_b, [tok_o])
    proj, h1 = _in_proj(x2, mod3, wf_in, t, tok_b)
    mixed, a1, q = _mixer_fwd(proj, dww, conf_dw_b, conf_ln_g, conf_ln_b, scw, t)
    (wf_out,) = _copy_wait("gather_out_wait", bufs_o, sems_o, plan_o, [mixed])
    bufs_b = _copy_wait("gather_mlp_wait", bufs_b, sems_b, plan_b, [mixed])
    sems_p, bufs_p, tok_p = _copy_start("gather_pass_start", bufs_b, 6, plan_p)
    x1, y1, h2 = _out_proj(mixed, wf_out, x2, mod3, t, tok_p)
    wf_1, wf_2 = _copy_wait("gather_pass_wait", bufs_p, sems_p, plan_p, [h2])
    z, dx2, dy2, dg2, ggf, loss_p = _mlp_fwd(h2, x1, tgt, mod3, g_final.reshape(1, d), wf_1, wf_2, t)

    dz, dx1, dy1, dsh2, dsc2, dg1 = _mlp_bwd(dy2, z, x1, dx2, y1, mod3, wf_1, wf_2, t)
    g_w2 = _wgrad(z, dy2, "wgrad_mlp2", relu2=True)
    g_w1 = _wgrad(h2, dz, "wgrad_mlp1")
    g_wout = _wgrad(mixed, dy1, "wgrad_out")
    made = {}

    def behind_pair_exchange(tok):
        made["dmixed"] = _out_proj_bwd(dy1, wf_out, t, tok)
        return [made["dmixed"]]

    def behind_chip_exchange(tok):
        made["dproj"], made["sg"] = _mixer_bwd(made["dmixed"], a1, q, proj, dww, conf_ln_g, conf_ln_b, scw, t, tok)
        return [made["dproj"]]

    def behind_pair_share(tok):
        made["g_win"] = _wgrad(h1, made["dproj"], "wgrad_in", deps=[tok])
        return [made["g_win"]]

    gr_out, gr_1, gr_2 = _reduce_scatter(ids, [g_wout, g_w1, g_w2], lays[1:4], names[1:4], "m",
                                         (behind_pair_exchange, behind_chip_exchange, behind_pair_share))

    big = {}

    def behind_pair_exchange_in(tok):
        made["grad_x"], made["dsh1"], made["dsc1"] = _in_proj_bwd(made["dproj"], wf_in, x2, dx1, mod3, t, tok)
        return [made["grad_x"]]

    def behind_chip_exchange_in(tok):
        pack, dmod8 = _pack_small(made["sg"], ggf, loss_p, (made["dsh1"], made["dsc1"], dg1, dsh2, dsc2, dg2), d, tok)
        pack_all, dmod_all8 = _all_gather8([pack, dmod8], "gather_small")
        red, dmod_all, g_bada = _small_reduce(pack_all, dmod_all8, nb)
        dmod_cols = lax.dynamic_slice(dmod_all, (0, chip * ada_w), (N_DEV * nb, ada_w))
        made["ada"] = _adamw_ada(c_act.T, dmod_cols, w_ada[0], m_w_ada[0], v_w_ada[0])
        for nm_, g_, w_, m_, v_ in (("w_out", gr_out, w_out, m_w_out, v_w_out),
                                    ("w_mlp1", gr_1, w_mlp1, m_w_mlp1, v_w_mlp1),
                                    ("w_mlp2", gr_2, w_mlp2, m_w_mlp2, v_w_mlp2)):
            big[nm_] = _adamw_big(w_[0], g_, m_[0], v_[0], "adamw_" + nm_)
        cw = dc // N_CHIPS
        g_dww = lax.dynamic_slice(red[0:CONF_K], (0, chip * cw), (CONF_K, cw))
        g_scw = lax.dynamic_slice(red[SG_SCW:SG_SCW + SC_K], (0, chip * cw), (SC_K, cw))
        g_gfin = jnp.concatenate([red[SG_N:SG_N + 1], red[SG_N + 1:SG_N + 2]], axis=1)
        made["small_g"] = [g_bada, g_dww, red[SG_DWB:SG_DWB + 1], red[SG_LNG:SG_LNG + 1], red[SG_LNB:SG_LNB + 1],
                           g_scw, g_gfin]
        small_w = [b_ada, conf_dw_w[0], conf_dw_b, conf_ln_g, conf_ln_b, sc_conv_w[0], g_final.reshape(1, d)]
        small_m = [m_b_ada, m_conf_dw_w[0], m_conf_dw_b, m_conf_ln_g, m_conf_ln_b, m_sc_conv_w[0],
                   m_g_final.reshape(1, d)]
        small_v = [v_b_ada, v_conf_dw_w[0], v_conf_dw_b, v_conf_ln_g, v_conf_ln_b, v_sc_conv_w[0],
                   v_g_final.reshape(1, d)]
        made["upd"] = _adamw_small(small_w, made["small_g"], small_m, small_v)
        made["loss"] = red[SG_N + 2, 0]
        return [big["w_mlp2"][0], made["upd"][0]]

    (gr_in,) = _reduce_scatter(ids, [made["g_win"]], lays[0:1], names[0:1], "i",
                               (behind_pair_exchange_in, behind_chip_exchange_in, None))
    big["w_in"] = _adamw_big(w_in[0], gr_in, m_w_in[0], v_w_in[0], "adamw_w_in")
    grad_x, small_g, upd, loss = made["grad_x"], made["small_g"], made["upd"], made["loss"]
    g_wada, d_wada, nm_wada, nv_wada = made["ada"]
    ns = len(small_g)
    s_delta, s_m, s_v = upd[0:ns], upd[ns:2 * ns], upd[2 * ns:3 * ns]

    def outs(kind_big, kind_small, wada):
        sm = kind_small
        return (wada[None], sm[0], kind_big["w_in"][None], sm[1][None], sm[2], sm[3], sm[4], sm[5][None],
                kind_big["w_out"][None], kind_big["w_mlp1"][None], kind_big["w_mlp2"][None], sm[6].reshape(d))

    grads_out = outs({"w_in": gr_in, "w_out": gr_out, "w_mlp1": gr_1, "w_mlp2": gr_2}, small_g, g_wada)
    delta_out = outs({k: v[0] for k, v in big.items()}, s_delta, d_wada)
    m_out = outs({k: v[1] for k, v in big.items()}, s_m, nm_wada)
    v_out = outs({k: v[2] for k, v in big.items()}, s_v, nv_wada)
    return (loss, grad_x.reshape(nb, t, d), *grads_out, *delta_out, *m_out, *v_out)
```

```python
import functools

import jax
import jax.numpy as jnp
from jax import lax
from jax.experimental import pallas as pl
from jax.experimental.pallas import tpu as pltpu

F32 = jnp.float32
BF16 = jnp.bfloat16
MESH = pl.DeviceIdType.MESH
HIGHEST = lax.Precision.HIGHEST

EPS = 1e-6
CONF_K = 31
SC_K = 3
HALO = 32
N_CHIPS = 4
N_DEV = 8

ADAM_LR = 0.001
ADAM_B1 = 0.9
ADAM_B2 = 0.999
ADAM_EPS = 1e-08
ADAM_WD = 0.01
ADAM_STEP = 10

TM_MM = 512
TM_MIX = 256
TM_MLP = 512
FF_CHUNK = 1024
MLP_SUB = 2
TK_WG = 2048
RB_CONV = 64
RB_WG = 32
CHIP_RELS = ((1, 0), (0, 1), (1, 1))

ANY = pl.BlockSpec(memory_space=pl.ANY)
VMEM = pl.BlockSpec(memory_space=pltpu.VMEM)
HBM = pl.BlockSpec(memory_space=pltpu.HBM)
SEM = pl.BlockSpec(memory_space=pltpu.SEMAPHORE)
EFFECT = pltpu.SideEffectType.DATAFLOW_SIDE_EFFECTING


def _me():
    return lax.axis_index("x"), lax.axis_index("y"), lax.axis_index("c")


def _flip(v, f):
    return 1 - v if f else v


def _rows8(v):
    r, c = v.shape
    return v.reshape(r // 8, 8, c).sum(axis=0)


def _rms(x):
    r = lax.rsqrt(jnp.mean(x * x, axis=-1, keepdims=True) + EPS)
    return x * r, r


def _rms_bwd(dxn, xn, r):
    return r * (dxn - xn * jnp.mean(dxn * xn, axis=-1, keepdims=True))


def _sigmoid(x):
    return 1.0 / (1.0 + jnp.exp(-x))


def _dot(a, b):
    return jnp.dot(a, b, preferred_element_type=F32)


def _dot_nt(a, b):
    return lax.dot_general(a, b, (((1,), (1,)), ((), ())), preferred_element_type=F32)


def _dot_tn(a, b):
    return lax.dot_general(a, b, (((0,), (0,)), ((), ())), preferred_element_type=F32)


def _const(shape):
    nd = len(shape)
    return pl.BlockSpec(shape, lambda i: (0,) * nd)


def _resident(shape):
    nd = len(shape)
    return pl.BlockSpec(shape, lambda i: (0,) * nd, pipeline_mode=pl.Buffered(1))


def _rowblk(tm, c):
    return pl.BlockSpec((tm, c), lambda i: (i, 0))


def _modspec(tps, width):
    return pl.BlockSpec((None, 1, width), lambda i: (i // tps, 0, 0))


def _accspec(tps, c):
    return pl.BlockSpec((None, 8, c), lambda i: (i // tps, 0, 0))


def _acc_add(ref, first, v):
    @pl.when(first)
    def _():
        ref[...] = v

    @pl.when(jnp.logical_not(first))
    def _():
        ref[...] += v


def _all_gather8(arrs, name):
    n = len(arrs)

    def body(*refs):
        ins, outs = refs[:n], refs[n:2 * n]
        send, recv = refs[2 * n:]
        x, y, c = _me()
        dev = 4 * x + 2 * y + c
        for a in range(n):
            outs[a][dev] = ins[a][...]
        sends = []
        for r in range(1, N_DEV):
            fx, fy, fc = (r >> 2) & 1, (r >> 1) & 1, r & 1
            peer = (_flip(x, fx), _flip(y, fy), _flip(c, fc))
            for a in range(n):
                cp = pltpu.make_async_remote_copy(
                    src_ref=ins[a], dst_ref=outs[a].at[dev],
                    send_sem=send.at[r - 1, a], recv_sem=recv.at[r - 1, a],
                    device_id=peer, device_id_type=MESH)
                cp.start()
                sends.append(cp)
        for r in range(1, N_DEV):
            fx, fy, fc = (r >> 2) & 1, (r >> 1) & 1, r & 1
            pdev = 4 * _flip(x, fx) + 2 * _flip(y, fy) + _flip(c, fc)
            for a in range(n):
                pltpu.make_async_remote_copy(
                    src_ref=ins[a], dst_ref=outs[a].at[pdev],
                    send_sem=send.at[r - 1, a], recv_sem=recv.at[r - 1, a],
                    device_id=(x, y, c), device_id_type=MESH).wait_recv()
        for cp in sends:
            cp.wait_send()

    return pl.pallas_call(
        body, name=name,
        out_shape=[jax.ShapeDtypeStruct((N_DEV,) + a.shape, a.dtype) for a in arrs],
        in_specs=[VMEM] * n, out_specs=[VMEM] * n,
        scratch_shapes=[pltpu.SemaphoreType.DMA((N_DEV - 1, n)),
                        pltpu.SemaphoreType.DMA((N_DEV - 1, n))],
    )(*arrs)


def _gather_mod(mod_shard):
    nb, w = mod_shard.shape

    def body(in_ref, out_ref, send, recv):
        x, y, c = _me()
        chip = 2 * x + y
        out_ref[:, pl.ds(pl.multiple_of(chip * w, 128), w)] = in_ref[...]
        sends = []
        for r, (fx, fy) in enumerate(CHIP_RELS):
            cp = pltpu.make_async_remote_copy(
                src_ref=in_ref,
                dst_ref=out_ref.at[:, pl.ds(pl.multiple_of(chip * w, 128), w)],
                send_sem=send.at[r], recv_sem=recv.at[r],
                device_id=(_flip(x, fx), _flip(y, fy), c), device_id_type=MESH)
            cp.start()
            sends.append(cp)
        for r, (fx, fy) in enumerate(CHIP_RELS):
            pchip = 2 * _flip(x, fx) + _flip(y, fy)
            pltpu.make_async_remote_copy(
                src_ref=in_ref,
                dst_ref=out_ref.at[:, pl.ds(pl.multiple_of(pchip * w, 128), w)],
                send_sem=send.at[r], recv_sem=recv.at[r],
                device_id=(x, y, c), device_id_type=MESH).wait_recv()
        for cp in sends:
            cp.wait_send()

    return pl.pallas_call(
        body, name="gather_mod",
        out_shape=jax.ShapeDtypeStruct((nb, N_CHIPS * w), mod_shard.dtype),
        in_specs=[VMEM], out_specs=VMEM,
        scratch_shapes=[pltpu.SemaphoreType.DMA((3,)), pltpu.SemaphoreType.DMA((3,))],
    )(mod_shard)


def _wlayout(d):
    d_in = 5 * d // 2
    return (
        (d, d_in // N_CHIPS, True),
        (d // N_CHIPS, d, False),
        (d, 4 * d // N_CHIPS, True),
        (4 * d // N_CHIPS, d, False),
    )


def _full_shape(lay):
    r, c, by_col = lay
    return (r, c * N_CHIPS) if by_col else (r * N_CHIPS, c)


def _full_view(ref, lay, k, h):
    r, c, by_col = lay
    hr = r // 2
    if by_col:
        return ref.at[pl.ds(pl.multiple_of(h * hr, 16), hr), pl.ds(pl.multiple_of(k * c, 128), c)]
    return ref.at[pl.ds(pl.multiple_of(k * r + h * hr, 16), hr), :]


def _half_view(ref, lay, h):
    hr = lay[0] // 2
    return ref.at[pl.ds(pl.multiple_of(h * hr, 16), hr), :]


def _half_shape(lay):
    return (lay[0] // 2, lay[1])


def _hbm(a):
    return pltpu.with_memory_space_constraint(a, pltpu.HBM)


def _remote(src, dst, send, recv, idx, to):
    return lambda: pltpu.make_async_remote_copy(src_ref=src, dst_ref=dst, send_sem=send.at[idx], recv_sem=recv.at[idx],
                                                device_id=to, device_id_type=MESH)


def _copy_start(name, bufs, n_sems, plan, after=()):
    nb, na = len(bufs), len(after)

    def body(*refs):
        sends, _ = plan(refs[:nb], refs[nb + na], refs[nb + na + 1])
        for mk in sends:
            mk().start()
        refs[-1][...] = jnp.zeros((8, 128), F32)

    outs = pl.pallas_call(
        body, name=name,
        out_shape=(pltpu.SemaphoreType.DMA((n_sems,)), pltpu.SemaphoreType.DMA((n_sems,)))
        + tuple(pltpu.HBM(b.shape, b.dtype) for b in bufs) + (jax.ShapeDtypeStruct((8, 128), F32),),
        in_specs=(HBM,) * nb + (ANY,) * na, out_specs=(SEM, SEM) + (HBM,) * nb + (VMEM,),
        input_output_aliases={i: 2 + i for i in range(nb)},
        compiler_params=pltpu.CompilerParams(has_side_effects=EFFECT),
    )(*[_hbm(b) for b in bufs], *after)
    return (outs[0], outs[1]), list(outs[2:2 + nb]), outs[-1]


def _copy_wait(name, bufs, sems, plan, after):
    nb, na = len(bufs), len(after)

    def body(*refs):
        sends, recvs = plan(refs[:nb], refs[nb], refs[nb + 1])
        for mk in sends:
            mk().wait_send()
        for mk in recvs:
            mk().wait_recv()

    outs = pl.pallas_call(
        body, name=name,
        out_shape=tuple(pltpu.HBM(b.shape, b.dtype) for b in bufs),
        in_specs=(HBM,) * nb + (SEM, SEM) + (ANY,) * na, out_specs=(HBM,) * nb,
        input_output_aliases={i: i for i in range(nb)},
        compiler_params=pltpu.CompilerParams(has_side_effects=EFFECT),
    )(*bufs, *sems, *after)
    return list(outs)


def _copy_blocking(name, bufs, n_sems, plan, after=()):
    nb, na = len(bufs), len(after)

    def body(*refs):
        sends, recvs = plan(refs[:nb], refs[2 * nb + na], refs[2 * nb + na + 1])
        started = [mk() for mk in sends]
        for cp in started:
            cp.start()
        for mk in recvs:
            mk().wait_recv()
        for cp in started:
            cp.wait_send()

    return list(pl.pallas_call(
        body, name=name,
        out_shape=tuple(jax.ShapeDtypeStruct(b.shape, b.dtype) for b in bufs),
        in_specs=(ANY,) * (nb + na), out_specs=(ANY,) * nb,
        input_output_aliases={i: i for i in range(nb)},
        scratch_shapes=[pltpu.SemaphoreType.DMA((n_sems,)), pltpu.SemaphoreType.DMA((n_sems,))],
    )(*bufs, *after))


def _exchange(name, bufs, n_sems, plan, between):
    if between is None:
        return _copy_blocking(name, bufs, n_sems, plan)
    sems, bufs, tok = _copy_start(name + "_start", bufs, n_sems, plan)
    return _copy_wait(name + "_wait", bufs, sems, plan, between(tok))


def _gather_direct_plan(lays):
    def plan(full, send, recv):
        x, y, c = _me()
        chip = 2 * x + y
        sends, recvs = [], []
        for r, (fx, fy) in enumerate(CHIP_RELS):
            px, py = _flip(x, fx), _flip(y, fy)
            for i, lay in enumerate(lays):
                for q in range(2):
                    oc = _flip(c, q)
                    mine = _full_view(full[i], lay, chip, c)
                    idx = (r * len(lays) + i) * 2 + q
                    sends.append(_remote(mine, mine, send, recv, idx, (px, py, oc)))
                    theirs = _full_view(full[i], lay, 2 * px + py, oc)
                    recvs.append(_remote(theirs, theirs, send, recv, idx, (x, y, c)))
        return sends, recvs
    return plan


def _gather_chip_plan(lays):
    def plan(full, send, recv):
        x, y, c = _me()
        chip = 2 * x + y
        sends, recvs = [], []
        for r, (fx, fy) in enumerate(CHIP_RELS):
            px, py = _flip(x, fx), _flip(y, fy)
            for i, lay in enumerate(lays):
                mine = _full_view(full[i], lay, chip, c)
                sends.append(_remote(mine, mine, send, recv, r * len(lays) + i, (px, py, c)))
                theirs = _full_view(full[i], lay, 2 * px + py, c)
                recvs.append(_remote(theirs, theirs, send, recv, r * len(lays) + i, (x, y, c)))
        return sends, recvs
    return plan


def _gather_pass_plan(lays):
    def plan(full, send, recv):
        x, y, c = _me()
        sends, recvs = [], []
        for r, (fx, fy) in enumerate(CHIP_RELS):
            pchip = 2 * _flip(x, fx) + _flip(y, fy)
            for i, lay in enumerate(lays):
                landed = _full_view(full[i], lay, pchip, c)
                sends.append(_remote(landed, landed, send, recv, r * len(lays) + i, (x, y, 1 - c)))
                other = _full_view(full[i], lay, pchip, 1 - c)
                recvs.append(_remote(other, other, send, recv, r * len(lays) + i, (x, y, c)))
        return sends, recvs
    return plan


def _pair_exchange_plan(lays):
    nw = len(lays)

    def plan(bufs, send, recv):
        x, y, c = _me()
        sends, recvs = [], []
        for i, lay in enumerate(lays):
            for k in range(N_CHIPS):
                sends.append(_remote(_full_view(bufs[i], lay, k, 1 - c), bufs[nw + i].at[k],
                                     send, recv, i * N_CHIPS + k, (x, y, 1 - c)))
                recvs.append(_remote(_full_view(bufs[i], lay, k, c), bufs[nw + i].at[k],
                                     send, recv, i * N_CHIPS + k, (x, y, c)))
        return sends, recvs
    return plan


def _chip_exchange_plan(nw):
    def plan(bufs, send, recv):
        x, y, c = _me()
        sends, recvs = [], []
        for r, (fx, fy) in enumerate(CHIP_RELS):
            px, py = _flip(x, fx), _flip(y, fy)
            for i in range(nw):
                sends.append(_remote(bufs[i].at[2 * px + py], bufs[nw + i].at[r], send, recv, r * nw + i, (px, py, c)))
                recvs.append(_remote(bufs[i].at[2 * px + py], bufs[nw + i].at[r], send, recv, r * nw + i, (x, y, c)))
        return sends, recvs
    return plan


def _pair_share_plan(lays):
    def plan(bufs, send, recv):
        x, y, c = _me()
        sends, recvs = [], []
        for i, lay in enumerate(lays):
            mine = _half_view(bufs[i], lay, c)
            sends.append(_remote(mine, mine, send, recv, i, (x, y, 1 - c)))
            other = _half_view(bufs[i], lay, 1 - c)
            recvs.append(_remote(other, other, send, recv, i, (x, y, c)))
        return sends, recvs
    return plan


def _pair_sum(ids, g, got, lay, name):
    r, c, by_col = lay
    hr = r // 2
    tr = min(hr, 256)
    nt = hr // tr

    def body(ids_ref, g_ref, got_ref, s32_ref, s16_ref):
        s = g_ref[...] + got_ref[...]
        s32_ref[...] = s
        s16_ref[...] = s.astype(BF16)

    if by_col:
        gspec = pl.BlockSpec((tr, c), lambda k, t, ids: (ids[1] * nt + t, k))
    else:
        gspec = pl.BlockSpec((tr, c), lambda k, t, ids: ((2 * k + ids[1]) * nt + t, 0))
    hspec = pl.BlockSpec((None, tr, c), lambda k, t, ids: (k, t, 0))
    return pl.pallas_call(
        body, name=name,
        grid_spec=pltpu.PrefetchScalarGridSpec(
            num_scalar_prefetch=1, grid=(N_CHIPS, nt),
            in_specs=[gspec, hspec], out_specs=[hspec, hspec]),
        out_shape=[jax.ShapeDtypeStruct((N_CHIPS, hr, c), F32),
                   jax.ShapeDtypeStruct((N_CHIPS, hr, c), BF16)],
    )(ids, g, got)


def _chip_sum(ids, s32, got, lay, name):
    hr, c = _half_shape(lay)
    tr = min(hr, 256)
    nt = hr // tr

    def body(ids_ref, s_ref, got_ref, out_ref):
        t = s_ref[...]
        for r in range(3):
            t = t + got_ref[r].astype(F32)
        out_ref[...] = t

    return pl.pallas_call(
        body, name=name,
        grid_spec=pltpu.PrefetchScalarGridSpec(
            num_scalar_prefetch=1, grid=(nt,),
            in_specs=[pl.BlockSpec((None, tr, c), lambda t, ids: (ids[0], t, 0)),
                      pl.BlockSpec((3, tr, c), lambda t, ids: (0, t, 0))],
            out_specs=pl.BlockSpec((tr, c), lambda t, ids: (ids[1] * nt + t, 0))),
        out_shape=jax.ShapeDtypeStruct((2 * hr, c), F32),
    )(ids, s32, got)


def _reduce_scatter(ids, grads, lays, names, tag, between):
    nw = len(lays)
    got1 = [lax.empty((N_CHIPS,) + _half_shape(l), F32) for l in lays]
    bufs = _exchange("pair_exchange_" + tag, list(grads) + got1, nw * N_CHIPS, _pair_exchange_plan(lays), between[0])
    sums = [_pair_sum(ids, bufs[i], bufs[nw + i], lays[i], "pair_sum_" + names[i]) for i in range(nw)]
    got2 = [lax.empty((3,) + _half_shape(l), BF16) for l in lays]
    bufs = _exchange("chip_exchange_" + tag, [s[1] for s in sums] + got2, 3 * nw, _chip_exchange_plan(nw), between[1])
    mine = [_chip_sum(ids, sums[i][0], bufs[nw + i], lays[i], "chip_sum_" + names[i]) for i in range(nw)]
    return _exchange("pair_share_" + tag, mine, nw, _pair_share_plan(lays), between[2])


def _cast_place(ids, w, lay, name):
    r, c, by_col = lay

    def body(ids_ref, w_ref, out_ref):
        out_ref[...] = w_ref[...].astype(BF16)

    omap = (lambda i, ids: (0, ids[0])) if by_col else (lambda i, ids: (ids[0], 0))
    return pl.pallas_call(
        body, name=name,
        grid_spec=pltpu.PrefetchScalarGridSpec(
            num_scalar_prefetch=1, grid=(1,),
            in_specs=[pl.BlockSpec((r, c), lambda i, ids: (0, 0))],
            out_specs=pl.BlockSpec((r, c), omap)),
        out_shape=jax.ShapeDtypeStruct(_full_shape(lay), BF16),
    )(ids, w)


def _ada_mod(c_all, w_ada, b_ada):
    def body(c_ref, w_ref, b_ref, act_ref, mod_ref):
        cv = c_ref[...]
        act = cv * _sigmoid(cv)
        act_ref[...] = act
        mod_ref[...] = jnp.dot(act, w_ref[...], preferred_element_type=F32, precision=HIGHEST) + b_ref[...]

    nb = c_all.shape[0]
    return pl.pallas_call(
        body, name="ada_mod",
        out_shape=[jax.ShapeDtypeStruct(c_all.shape, F32),
                   jax.ShapeDtypeStruct((nb, w_ada.shape[1]), F32)],
        in_specs=[VMEM] * 3, out_specs=[VMEM] * 2,
    )(c_all, w_ada, b_ada)


def _run_units(*unit_lists):
    total = max(len(u) for u in unit_lists)
    done = [0] * len(unit_lists)
    for step in range(1, total + 1):
        for li, units in enumerate(unit_lists):
            upto = (step * len(units) + total - 1) // total
            while done[li] < upto:
                units[done[li]]()
                done[li] += 1


def _in_mix_fwd(x, mod3, w_in, wb, dwb, lng, lnb, scw, t, dep):
    n, d = x.shape
    d_in = w_in.shape[1]
    dc = d_in // 5
    tm = min(TM_MIX, t)
    tps = t // tm
    nt = n // tm
    rb = min(RB_CONV, tm)
    ncol = 256
    ng = dc // 128
    assert tps % 2 == 0 and nt % 2 == 0

    def proj_units(x_ref, mod_ref, w_ref, p_ref, h_ref):
        def head():
            xn, _ = _rms(x_ref[...])
            h_ref[...] = (xn * (1.0 + mod_ref[:, d:2 * d]) + mod_ref[:, 0:d]).astype(BF16)
        units = [head]
        for c0 in range(0, d_in, ncol):
            def chunk(c0=c0):
                p_ref[:, c0:c0 + ncol] = _dot(h_ref[...], w_ref[:, c0:c0 + ncol])
            units.append(chunk)
        return units

    def mix_units(first, p_ref, h_ref, r0, wb_ref, dwb_ref, lng_ref, lnb_ref, scw_ref,
                  proj_ref, h1_ref, mixed_ref, a1_ref, q_ref, ext_ref, e_ref, extp_ref, a1s_ref):
        rows = slice(r0, r0 + tm)
        units = []

        def glu():
            halo = ext_ref[tm:tm + HALO, :]
            ext_ref[0:HALO, :] = halo if first is False else jnp.where(first, 0.0, halo)
            ext_ref[HALO:HALO + tm, :] = p_ref[:, 0:dc] * _sigmoid(p_ref[:, dc:2 * dc])
        units.append(glu)
        for g in range(ng):
            def shift(g=g):
                for r in range(8):
                    e_ref[g, r, 0:tm + HALO, :] = ext_ref[r:r + tm + HALO, 128 * g:128 * g + 128]
            units.append(shift)
        for g in range(ng):
            lanes = slice(128 * g, 128 * g + 128)
            for i0 in range(0, tm, rb):
                def conv(g=g, lanes=lanes, i0=i0):
                    acc = jnp.zeros((rb, 128), F32)
                    for k in range(CONF_K):
                        m, r = divmod(k + HALO - CONF_K + 1, 8)
                        acc = acc + e_ref[g, r, i0 + 8 * m:i0 + 8 * m + rb, :] * wb_ref[k:k + 1, lanes]
                    a1s_ref[i0:i0 + rb, lanes] = acc + dwb_ref[:, lanes]
                units.append(conv)

        def norm():
            a1 = a1s_ref[...]
            a1_ref[rows, :] = a1
            mu = jnp.mean(a1, axis=-1, keepdims=True)
            ac = a1 - mu
            rstd = lax.rsqrt(jnp.mean(ac * ac, axis=-1, keepdims=True) + EPS)
            a2 = ac * rstd * lng_ref[...] + lnb_ref[...]
            mixed_ref[rows, 0:dc] = (a2 * _sigmoid(a2)).astype(BF16)
        units.append(norm)

        def short():
            halo = extp_ref[tm:tm + 8, :]
            extp_ref[0:8, :] = halo if first is False else jnp.where(first, 0.0, halo)
            extp_ref[8:8 + tm, :] = p_ref[:, 3 * dc:4 * dc] * p_ref[:, 4 * dc:5 * dc]
            q = jnp.zeros((tm, dc), F32)
            for k in range(SC_K):
                q = q + extp_ref[6 + k:6 + k + tm, :] * scw_ref[k:k + 1, :]
            q_ref[rows, :] = q
            mixed_ref[rows, dc:2 * dc] = (p_ref[:, 2 * dc:3 * dc] * q).astype(BF16)
        units.append(short)

        def keep():
            proj_ref[rows, :] = p_ref[...]
            h1_ref[rows, :] = h_ref[...]
        units.append(keep)
        return units

    def body(x0_ref, xa_ref, xb_ref, mod0_ref, moda_ref, modb_ref, w_ref,
             wb_ref, dwb_ref, lng_ref, lnb_ref, scw_ref, dep_ref,
             proj_ref, h1_ref, mixed_ref, a1_ref, q_ref,
             p0_ref, p1_ref, h0_ref, hh1_ref, ext_ref, e_ref, extp_ref, a1s_ref):
        j = pl.program_id(0)

        @pl.when(j == 0)
        def _():
            ext_ref[...] = jnp.zeros(ext_ref.shape, F32)
            extp_ref[...] = jnp.zeros(extp_ref.shape, F32)
            _run_units(proj_units(x0_ref, mod0_ref, w_ref, p0_ref, h0_ref))

        common = (wb_ref, dwb_ref, lng_ref, lnb_ref, scw_ref, proj_ref, h1_ref, mixed_ref, a1_ref, q_ref,
                  ext_ref, e_ref, extp_ref, a1s_ref)
        _run_units(mix_units((2 * j) % tps == 0, p0_ref, h0_ref, 0, *common),
                   proj_units(xa_ref, moda_ref, w_ref, p1_ref, hh1_ref))
        _run_units(mix_units(False, p1_ref, hh1_ref, tm, *common),
                   proj_units(xb_ref, modb_ref, w_ref, p0_ref, h0_ref))

    last = nt - 1
    xspec = lambda f: pl.BlockSpec((tm, d), lambda j: (f(j), 0))
    mspec = lambda f: pl.BlockSpec((None, 1, 6 * d), lambda j: (f(j) // tps, 0, 0))
    out2 = lambda c: pl.BlockSpec((2 * tm, c), lambda j: (j, 0))
    return pl.pallas_call(
        body, name="in_mix_fwd", grid=(nt // 2,),
        in_specs=[xspec(lambda j: 0), xspec(lambda j: 2 * j + 1), xspec(lambda j: jnp.minimum(2 * j + 2, last)),
                  mspec(lambda j: 0), mspec(lambda j: 2 * j + 1), mspec(lambda j: jnp.minimum(2 * j + 2, last)),
                  _resident((d, d_in)),
                  _const(wb.shape), _const(dwb.shape), _const(lng.shape), _const(lnb.shape), _const(scw.shape), ANY],
        out_specs=[out2(d_in), out2(d), out2(2 * dc), out2(dc), out2(dc)],
        out_shape=[jax.ShapeDtypeStruct((n, d_in), F32), jax.ShapeDtypeStruct((n, d), BF16),
                   jax.ShapeDtypeStruct((n, 2 * dc), BF16), jax.ShapeDtypeStruct((n, dc), F32),
                   jax.ShapeDtypeStruct((n, dc), F32)],
        scratch_shapes=[pltpu.VMEM((tm, d_in), F32), pltpu.VMEM((tm, d_in), F32),
                        pltpu.VMEM((tm, d), BF16), pltpu.VMEM((tm, d), BF16),
                        pltpu.VMEM((tm + HALO + 8, dc), F32), _shift_scratch(tm, dc),
                        pltpu.VMEM((tm + 8, dc), F32), pltpu.VMEM((tm, dc), F32)],
        compiler_params=pltpu.CompilerParams(dimension_semantics=("arbitrary",)),
    )(x, x, x, mod3, mod3, mod3, w_in, wb, dwb, lng, lnb, scw, dep)


def _shifted_copies(ext_ref, e_ref, rows):
    for g in range(e_ref.shape[0]):
        for r in range(8):
            e_ref[g, r, 0:rows, :] = ext_ref[r:r + rows, 128 * g:128 * g + 128]


def _shift_scratch(tm, dc):
    return pltpu.VMEM((dc // 128, 8, tm + HALO + 8, 128), F32)


def _out_proj(mixed, w_out, x, mod3, t, dep):
    n, d = x.shape
    tm = min(TM_MM, t)
    tps = t // tm

    def body(m_ref, w_ref, x_ref, mod_ref, dep_ref, x1_ref, y1_ref, h2_ref):
        y1 = _dot(m_ref[...], w_ref[...])
        y1_ref[...] = y1.astype(BF16)
        x1 = x_ref[...] + mod_ref[:, 2 * d:3 * d] * y1
        x1_ref[...] = x1
        xn, _ = _rms(x1)
        h2_ref[...] = (xn * (1.0 + mod_ref[:, 4 * d:5 * d]) + mod_ref[:, 3 * d:4 * d]).astype(BF16)

    return pl.pallas_call(
        body, name="out_proj", grid=(n // tm,),
        in_specs=[_rowblk(tm, d), _resident((d, d)), _rowblk(tm, d), _modspec(tps, 6 * d), ANY],
        out_specs=[_rowblk(tm, d), _rowblk(tm, d), _rowblk(tm, d)],
        out_shape=[jax.ShapeDtypeStruct((n, d), F32), jax.ShapeDtypeStruct((n, d), BF16),
                   jax.ShapeDtypeStruct((n, d), BF16)],
        compiler_params=pltpu.CompilerParams(dimension_semantics=("parallel",)),
    )(mixed, w_out, x, mod3, dep)


def _mlp_fwd(h2, x1, tgt, mod3, gfin, w1, w2, t):
    n, d = x1.shape
    dff = w1.shape[1]
    tm = min(TM_MLP, t)
    tps = t // tm
    nt = n // tm

    def body(h_ref, x1_ref, tg_ref, mod_ref, gf_ref, w1_ref, w2_ref,
             z_ref, dx2_ref, dy2_ref, dg2_ref, ggf_ref, loss_ref):
        i = pl.program_id(0)
        g2 = mod_ref[:, 5 * d:6 * d]
        gf = gf_ref[...]
        sub = tm // MLP_SUB
        sums = None
        for part in range(MLP_SUB):
            rs = slice(part * sub, (part + 1) * sub)
            hv = h_ref[rs, :]
            y2 = jnp.zeros((sub, d), F32)
            for j in range(dff // FF_CHUNK):
                cols = slice(j * FF_CHUNK, (j + 1) * FF_CHUNK)
                z = _dot(hv, w1_ref[:, cols])
                z_ref[rs, cols] = z.astype(BF16)
                zr = jnp.maximum(z, 0.0)
                y2 = y2 + _dot((zr * zr).astype(BF16), w2_ref[cols, :])
            x2n, r3 = _rms(x1_ref[rs, :] + g2 * y2)
            diff = x2n * gf - tg_ref[rs, :]
            dout = diff * (1.0 / d)
            dx2 = _rms_bwd(dout * gf, x2n, r3)
            dx2_ref[rs, :] = dx2
            dy2_ref[rs, :] = (g2 * dx2).astype(BF16)
            p = (_rows8(dx2 * y2), _rows8(dout * x2n), _rows8(diff * diff))
            sums = p if sums is None else tuple(a + b for a, b in zip(sums, p))
        _acc_add(dg2_ref, i % tps == 0, sums[0])
        _acc_add(ggf_ref, i == 0, sums[1])
        _acc_add(loss_ref, i == 0, sums[2])

    return pl.pallas_call(
        body, name="mlp_fwd", grid=(nt,),
        in_specs=[_rowblk(tm, d), _rowblk(tm, d), _rowblk(tm, d), _modspec(tps, 6 * d), _const((1, d)),
                  _resident((d, dff)), _resident((dff, d))],
        out_specs=[_rowblk(tm, dff), _rowblk(tm, d), _rowblk(tm, d), _accspec(tps, d),
                   _const((8, d)), _const((8, d))],
        out_shape=[jax.ShapeDtypeStruct((n, dff), BF16), jax.ShapeDtypeStruct((n, d), F32),
                   jax.ShapeDtypeStruct((n, d), BF16), jax.ShapeDtypeStruct((n // t, 8, d), F32),
                   jax.ShapeDtypeStruct((8, d), F32), jax.ShapeDtypeStruct((8, d), F32)],
        compiler_params=pltpu.CompilerParams(dimension_semantics=("arbitrary",)),
    )(h2, x1, tgt, mod3, gfin, w1, w2)


def _mlp_bwd(dy2, z, x1, dx2, y1, mod3, w1, w2, t):
    n, d = x1.shape
    dff = w1.shape[1]
    tm = min(TM_MLP, t)
    tps = t // tm

    def body(dy2_ref, z_ref, x1_ref, dx2_ref, y1_ref, mod_ref, w1_ref, w2_ref,
             dz_ref, dx1_ref, dy1_ref, dsh_ref, dsc_ref, dg1_ref):
        first = pl.program_id(0) % tps == 0
        sub = tm // MLP_SUB
        sums = None
        for part in range(MLP_SUB):
            rs = slice(part * sub, (part + 1) * sub)
            dy2 = dy2_ref[rs, :]
            dh2 = jnp.zeros((sub, d), F32)
            for j in range(dff // FF_CHUNK):
                cols = slice(j * FF_CHUNK, (j + 1) * FF_CHUNK)
                du = _dot_nt(dy2, w2_ref[cols, :])
                dz = (du * (2.0 * jnp.maximum(z_ref[rs, cols].astype(F32), 0.0))).astype(BF16)
                dz_ref[rs, cols] = dz
                dh2 = dh2 + _dot_nt(dz, w1_ref[:, cols])
            x1n, r2 = _rms(x1_ref[rs, :])
            dx1 = dx2_ref[rs, :] + _rms_bwd(dh2 * (1.0 + mod_ref[:, 4 * d:5 * d]), x1n, r2)
            dx1_ref[rs, :] = dx1
            dy1_ref[rs, :] = (mod_ref[:, 2 * d:3 * d] * dx1).astype(BF16)
            p = (_rows8(dh2), _rows8(dh2 * x1n), _rows8(dx1 * y1_ref[rs, :].astype(F32)))
            sums = p if sums is None else tuple(a + b for a, b in zip(sums, p))
        _acc_add(dsh_ref, first, sums[0])
        _acc_add(dsc_ref, first, sums[1])
        _acc_add(dg1_ref, first, sums[2])

    acc = jax.ShapeDtypeStruct((n // t, 8, d), F32)
    return pl.pallas_call(
        body, name="mlp_bwd", grid=(n // tm,),
        in_specs=[_rowblk(tm, d), _rowblk(tm, dff), _rowblk(tm, d), _rowblk(tm, d), _rowblk(tm, d),
                  _modspec(tps, 6 * d), _resident((d, dff)), _resident((dff, d))],
        out_specs=[_rowblk(tm, dff), _rowblk(tm, d), _rowblk(tm, d),
                   _accspec(tps, d), _accspec(tps, d), _accspec(tps, d)],
        out_shape=[jax.ShapeDtypeStruct((n, dff), BF16), jax.ShapeDtypeStruct((n, d), F32),
                   jax.ShapeDtypeStruct((n, d), BF16), acc, acc, acc],
        compiler_params=pltpu.CompilerParams(dimension_semantics=("arbitrary",)),
    )(dy2, z, x1, dx2, y1, mod3, w1, w2)


def _wgrad(a, b, name, relu2=False, bn=None, deps=()):
    n, ka = a.shape
    nb = b.shape[1]
    tk = min(TK_WG, n)
    bm = min(ka, 1024)
    if bn is None:
        bn = nb if nb <= 2048 else nb // 2

    def body(a_ref, b_ref, *rest):
        out_ref = rest[-1]
        av = a_ref[...]
        if relu2:
            ar = jnp.maximum(av, 0.0)
            av = ar * ar
        p = _dot_tn(av, b_ref[...])
        _acc_add(out_ref, pl.program_id(2) == 0, p)

    return pl.pallas_call(
        body, name=name, grid=(ka // bm, nb // bn, n // tk),
        in_specs=[pl.BlockSpec((tk, bm), lambda i, j, k: (k, i)),
                  pl.BlockSpec((tk, bn), lambda i, j, k: (k, j))] + [ANY] * len(deps),
        out_specs=pl.BlockSpec((bm, bn), lambda i, j, k: (i, j)),
        out_shape=jax.ShapeDtypeStruct((ka, nb), F32),
        compiler_params=pltpu.CompilerParams(dimension_semantics=("parallel", "parallel", "arbitrary")),
    )(a, b, *deps)


def _out_proj_bwd(dy1, w_out, t, dep):
    n, d = dy1.shape
    tm = min(TM_MM, t)

    def body(dy_ref, w_ref, dep_ref, dm_ref):
        dm_ref[...] = _dot_nt(dy_ref[...], w_ref[...])

    return pl.pallas_call(
        body, name="out_proj_bwd", grid=(n // tm,),
        in_specs=[_rowblk(tm, d), _resident(w_out.shape), ANY],
        out_specs=_rowblk(tm, w_out.shape[0]),
        out_shape=jax.ShapeDtypeStruct((n, w_out.shape[0]), F32),
        compiler_params=pltpu.CompilerParams(dimension_semantics=("parallel",)),
    )(dy1, w_out, dep)


SG_DWW = 0
SG_DWB = CONF_K
SG_LNG = CONF_K + 1
SG_LNB = CONF_K + 2
SG_SCW = CONF_K + 3
SG_N = CONF_K + 3 + SC_K


def _mixer_bwd(dmixed, a1, q, proj, wb, lng, lnb, scw, t, dep):
    n, d_in = proj.shape
    dc = d_in // 5
    tm = min(TM_MIX, t)
    tps = t // tm
    hb = tm // HALO
    nh = n // HALO
    rw = min(RB_WG, tm)

    def ln_bwd(a1v, da3, lng_v, lnb_v):
        mu = jnp.mean(a1v, axis=-1, keepdims=True)
        ac = a1v - mu
        rstd = lax.rsqrt(jnp.mean(ac * ac, axis=-1, keepdims=True) + EPS)
        ah = ac * rstd
        a2 = ah * lng_v + lnb_v
        s2 = _sigmoid(a2)
        da2 = da3 * (s2 * (1.0 + a2 * (1.0 - s2)))
        dah = da2 * lng_v
        da1 = rstd * (dah - jnp.mean(dah, axis=-1, keepdims=True)
                      - ah * jnp.mean(dah * ah, axis=-1, keepdims=True))
        return da1, da2, ah

    def body(dm_ref, dmn_ref, a1_ref, a1n_ref, q_ref, p_ref, pn_ref,
             wb_ref, lng_ref, lnb_ref, scw_ref, dep_ref, dproj_ref, sg_ref,
             extd_ref, ed_ref, a0_ref, da0_ref, extq_ref):
        i = pl.program_id(0)
        last = i % tps == tps - 1
        keep_next = jnp.where(last, 0.0, 1.0)

        @pl.when(i == 0)
        def _():
            sg_ref[...] = jnp.zeros(sg_ref.shape, F32)

        lng_v, lnb_v = lng_ref[...], lnb_ref[...]
        val, sig = p_ref[:, 0:dc], _sigmoid(p_ref[:, dc:2 * dc])
        a0 = val * sig
        da1, da2, ah = ln_bwd(a1_ref[...], dm_ref[:, 0:dc], lng_v, lnb_v)
        da1n, _, _ = ln_bwd(a1n_ref[...], dmn_ref[:, 0:dc], lng_v, lnb_v)
        sg_ref[8 * SG_LNG:8 * SG_LNG + 8, :] += _rows8(da2 * ah)
        sg_ref[8 * SG_LNB:8 * SG_LNB + 8, :] += _rows8(da2)
        sg_ref[8 * SG_DWB:8 * SG_DWB + 8, :] += _rows8(da1)
        a0_ref[...] = a0
        extd_ref[0:tm, :] = da1
        extd_ref[tm:tm + HALO, :] = da1n * keep_next
        extd_ref[tm + HALO:tm + HALO + 8, :] = jnp.zeros((8, dc), F32)
        _shifted_copies(extd_ref, ed_ref, tm + HALO)
        for g in range(dc // 128):
            lanes = slice(128 * g, 128 * g + 128)

            def rows(j, accs, lanes=lanes, g=g):
                i0 = pl.multiple_of(j * rw, rw)
                a0v = a0_ref[pl.ds(i0, rw), lanes]
                acc = jnp.zeros((rw, 128), F32)
                new = []
                for s in range(CONF_K):
                    m, r = divmod(s, 8)
                    e = ed_ref[g, r, pl.ds(i0 + 8 * m, rw), :]
                    acc = acc + e * wb_ref[CONF_K - 1 - s:CONF_K - s, lanes]
                    new.append(accs[s] + _rows8(e * a0v))
                da0_ref[pl.ds(i0, rw), lanes] = acc
                return tuple(new)

            accs = lax.fori_loop(0, tm // rw, rows,
                                 tuple(jnp.zeros((8, 128), F32) for _ in range(CONF_K)))
            for s in range(CONF_K):
                k = CONF_K - 1 - s
                sg_ref[8 * (SG_DWW + k):8 * (SG_DWW + k) + 8, lanes] += accs[s]
        da0 = da0_ref[...]
        dproj_ref[:, 0:dc] = (da0 * sig).astype(BF16)
        dproj_ref[:, dc:2 * dc] = (da0 * a0 * (1.0 - sig)).astype(BF16)
        ds = dm_ref[:, dc:2 * dc]
        scb, scc, sch = p_ref[:, 2 * dc:3 * dc], p_ref[:, 3 * dc:4 * dc], p_ref[:, 4 * dc:5 * dc]
        dproj_ref[:, 2 * dc:3 * dc] = (ds * q_ref[...]).astype(BF16)
        dq = ds * scb
        extq_ref[0:tm, :] = dq
        extq_ref[tm:tm + 8, :] = dmn_ref[0:8, dc:2 * dc] * pn_ref[0:8, :] * keep_next
        pv = scc * sch
        dp = jnp.zeros((tm, dc), F32)
        for k in range(SC_K):
            dqs = extq_ref[SC_K - 1 - k:SC_K - 1 - k + tm, :]
            dp = dp + dqs * scw_ref[k:k + 1, :]
            sg_ref[8 * (SG_SCW + k):8 * (SG_SCW + k) + 8, :] += _rows8(pv * dqs)
        dproj_ref[:, 3 * dc:4 * dc] = (dp * sch).astype(BF16)
        dproj_ref[:, 4 * dc:5 * dc] = (dp * scc).astype(BF16)

    nxt = lambda i: (jnp.minimum((i + 1) * hb, nh - 1), 0)
    return pl.pallas_call(
        body, name="mixer_bwd", grid=(n // tm,),
        in_specs=[_rowblk(tm, 2 * dc), pl.BlockSpec((HALO, 2 * dc), nxt),
                  _rowblk(tm, dc), pl.BlockSpec((HALO, dc), nxt),
                  _rowblk(tm, dc),
                  _rowblk(tm, d_in),
                  pl.BlockSpec((HALO, dc), lambda i: (jnp.minimum((i + 1) * hb, nh - 1), 2)),
                  _const(wb.shape), _const(lng.shape), _const(lnb.shape), _const(scw.shape), ANY],
        out_specs=[_rowblk(tm, d_in), _const((8 * SG_N, dc))],
        out_shape=[jax.ShapeDtypeStruct((n, d_in), BF16), jax.ShapeDtypeStruct((8 * SG_N, dc), F32)],
        scratch_shapes=[pltpu.VMEM((tm + HALO + 8, dc), F32), _shift_scratch(tm, dc),
                        pltpu.VMEM((tm, dc), F32), pltpu.VMEM((tm, dc), F32),
                        pltpu.VMEM((tm + 8, dc), F32)],
        compiler_params=pltpu.CompilerParams(dimension_semantics=("arbitrary",)),
    )(dmixed, dmixed, a1, a1, q, proj, proj, wb, lng, lnb, scw, dep)


def _in_proj_bwd(dproj, w_in, x, dx1, mod3, t, dep):
    n, d = x.shape
    d_in = w_in.shape[1]
    tm = min(TM_MM, t)
    tps = t // tm

    def body(dp_ref, w_ref, x_ref, dx1_ref, mod_ref, dep_ref, gx_ref, dsh_ref, dsc_ref):
        first = pl.program_id(0) % tps == 0
        dh1 = _dot_nt(dp_ref[...], w_ref[...])
        xn, r1 = _rms(x_ref[...])
        _acc_add(dsh_ref, first, _rows8(dh1))
        _acc_add(dsc_ref, first, _rows8(dh1 * xn))
        gx_ref[...] = dx1_ref[...] + _rms_bwd(dh1 * (1.0 + mod_ref[:, d:2 * d]), xn, r1)

    acc = jax.ShapeDtypeStruct((n // t, 8, d), F32)
    return pl.pallas_call(
        body, name="in_proj_bwd", grid=(n // tm,),
        in_specs=[_rowblk(tm, d_in), _resident((d, d_in)), _rowblk(tm, d), _rowblk(tm, d),
                  _modspec(tps, 6 * d), ANY],
        out_specs=[_rowblk(tm, d), _accspec(tps, d), _accspec(tps, d)],
        out_shape=[jax.ShapeDtypeStruct((n, d), F32), acc, acc],
        compiler_params=pltpu.CompilerParams(dimension_semantics=("arbitrary",)),
    )(dproj, w_in, x, dx1, mod3, dep)


SMALL_ROWS = 40


def _pack_small(sg, ggf, loss, accs, d, dep):
    dc = d // 2
    nb = accs[0].shape[0]

    def body(sg_ref, ggf_ref, loss_ref, dsh1, dsc1, dg1, dsh2, dsc2, dg2, dep_ref, pack_ref, dmod_ref):
        pack_ref[...] = jnp.zeros(pack_ref.shape, F32)
        for k in range(SG_N):
            pack_ref[k:k + 1, :] = jnp.sum(sg_ref[8 * k:8 * k + 8, :], axis=0, keepdims=True)
        gf = jnp.sum(ggf_ref[...], axis=0, keepdims=True)
        pack_ref[SG_N:SG_N + 1, :] = gf[:, 0:dc]
        pack_ref[SG_N + 1:SG_N + 2, :] = gf[:, dc:d]
        tot = jnp.sum(jnp.sum(loss_ref[...], axis=0, keepdims=True), axis=1, keepdims=True) * (0.5 / d)
        pack_ref[SG_N + 2:SG_N + 3, :] = jnp.broadcast_to(tot, (1, dc))
        dmod_ref[...] = jnp.zeros(dmod_ref.shape, F32)
        for j, ref in enumerate((dsh1, dsc1, dg1, dsh2, dsc2, dg2)):
            for b in range(nb):
                dmod_ref[b:b + 1, j * d:(j + 1) * d] = jnp.sum(ref[b], axis=0, keepdims=True)

    return pl.pallas_call(
        body, name="pack_small",
        out_shape=[jax.ShapeDtypeStruct((SMALL_ROWS, dc), F32), jax.ShapeDtypeStruct((8, 6 * d), F32)],
        in_specs=[VMEM] * 9 + [ANY], out_specs=[VMEM] * 2,
    )(sg, ggf, loss, *accs, dep)


def _small_reduce(pack_all, dmod_all, nb, dep):
    def body(pk_ref, dm_ref, dep_ref, red_ref, dmod_ref, gb_ref):
        tot = pk_ref[0]
        for dev in range(1, N_DEV):
            tot = tot + pk_ref[dev]
        red_ref[...] = tot
        gb = jnp.zeros((1, dm_ref.shape[2]), F32)
        for dev in range(N_DEV):
            for b in range(nb):
                row = dm_ref[dev, b:b + 1, :]
                dmod_ref[dev * nb + b:dev * nb + b + 1, :] = row
                gb = gb + row
        gb_ref[...] = gb

    return pl.pallas_call(
        body, name="small_reduce",
        out_shape=[jax.ShapeDtypeStruct(pack_all.shape[1:], F32),
                   jax.ShapeDtypeStruct((N_DEV * nb, dmod_all.shape[2]), F32),
                   jax.ShapeDtypeStruct((1, dmod_all.shape[2]), F32)],
        in_specs=[VMEM] * 2 + [ANY], out_specs=[VMEM] * 3,
    )(pack_all, dmod_all, dep)


def _adam(w, g, m, v):
    m = ADAM_B1 * m + (1.0 - ADAM_B1) * g
    v = ADAM_B2 * v + (1.0 - ADAM_B2) * (g * g)
    m_hat = m / (1.0 - ADAM_B1 ** ADAM_STEP)
    v_hat = v / (1.0 - ADAM_B2 ** ADAM_STEP)
    delta = -ADAM_LR * (m_hat / (jnp.sqrt(v_hat) + ADAM_EPS) + ADAM_WD * w)
    return delta, m, v


def _adamw_big(w, g, m, v, name):
    r, c = w.shape
    tr = min(r, 256)

    def body(w_ref, g_ref, m_ref, v_ref, d_ref, nm_ref, nv_ref):
        d_ref[...], nm_ref[...], nv_ref[...] = _adam(w_ref[...], g_ref[...], m_ref[...], v_ref[...])

    s = jax.ShapeDtypeStruct((r, c), F32)
    return pl.pallas_call(
        body, name=name, grid=(r // tr,),
        in_specs=[_rowblk(tr, c)] * 4, out_specs=[_rowblk(tr, c)] * 3, out_shape=[s, s, s],
        compiler_params=pltpu.CompilerParams(dimension_semantics=("parallel",)),
    )(w, g, m, v)


def _adamw_ada(act_t, dmod_cols, w, m, v):
    r, c = w.shape
    tr = min(r, 256)
    nb = act_t.shape[1]

    def body(a_ref, dm_ref, w_ref, m_ref, v_ref, g_ref, d_ref, nm_ref, nv_ref):
        g = jnp.dot(a_ref[...], dm_ref[...], preferred_element_type=F32, precision=HIGHEST)
        g_ref[...] = g
        d_ref[...], nm_ref[...], nv_ref[...] = _adam(w_ref[...], g, m_ref[...], v_ref[...])

    s = jax.ShapeDtypeStruct((r, c), F32)
    return pl.pallas_call(
        body, name="adamw_w_ada", grid=(r // tr,),
        in_specs=[_rowblk(tr, nb), _const((nb, c))] + [_rowblk(tr, c)] * 3,
        out_specs=[_rowblk(tr, c)] * 4, out_shape=[s, s, s, s],
        compiler_params=pltpu.CompilerParams(dimension_semantics=("parallel",)),
    )(act_t, dmod_cols, w, m, v)


def _adamw_small(ws, gs, ms, vs):
    n = len(ws)

    def body(*refs):
        for i in range(n):
            w, g, m, v = (refs[j * n + i][...] for j in range(4))
            dl, nm, nv = _adam(w, g, m, v)
            refs[4 * n + i][...] = dl
            refs[5 * n + i][...] = nm
            refs[6 * n + i][...] = nv

    shapes = [jax.ShapeDtypeStruct(w.shape, F32) for w in ws]
    return pl.pallas_call(
        body, name="adamw_small", out_shape=shapes * 3,
        in_specs=[VMEM] * (4 * n), out_specs=[VMEM] * (3 * n),
    )(*ws, *gs, *ms, *vs)


def kernel(x, c, w_ada, b_ada, w_in, conf_dw_w, conf_dw_b, conf_ln_g, conf_ln_b, sc_conv_w, w_out, w_mlp1, w_mlp2, g_final, loss_target, m_w_ada, m_b_ada, m_w_in, m_conf_dw_w, m_conf_dw_b, m_conf_ln_g, m_conf_ln_b, m_sc_conv_w, m_w_out, m_w_mlp1, m_w_mlp2, m_g_final, v_w_ada, v_b_ada, v_w_in, v_conf_dw_w, v_conf_dw_b, v_conf_ln_g, v_conf_ln_b, v_sc_conv_w, v_w_out, v_w_mlp1, v_w_mlp2, v_g_final):
    nb, t, d = x.shape
    n = nb * t
    dc = d // 2
    ada_w = w_ada.shape[2]
    ax, ay, ac = _me()
    chip = 2 * ax + ay
    dev = 2 * chip + ac
    ids = jnp.stack([chip, ac]).astype(jnp.int32)

    lays = _wlayout(d)
    names = ("in", "out", "mlp1", "mlp2")
    fulls = [_cast_place(ids, w[0], lays[i], "cast_" + names[i]) for i, w in enumerate((w_in, w_out, w_mlp1, w_mlp2))]

    c_pad = jnp.zeros((8, d), F32).at[0:nb].set(c)
    cw_pad = jnp.zeros((SMALL_ROWS, dc // N_CHIPS), F32)
    cw_pad = cw_pad.at[0:CONF_K].set(conf_dw_w[0]).at[HALO:HALO + SC_K].set(sc_conv_w[0])
    c_all8, cw_all8 = _all_gather8([c_pad, cw_pad], "gather_c")
    plan_i = _gather_chip_plan(lays[0:1])
    sems_i, bufs_i, tok_i = _copy_start("gather_in_start", [fulls[0]], 3, plan_i, [c_all8])
    c_all = c_all8[:, 0:nb].reshape(N_DEV * nb, d) + tok_i[0, 0]
    cw_full = jnp.concatenate([cw_all8[2 * k] for k in range(N_CHIPS)], axis=1)
    dww, scw = cw_full[0:CONF_K], cw_full[HALO:HALO + SC_K]
    b_cols = lax.dynamic_slice(b_ada, (0, chip * ada_w), (1, ada_w))
    c_act, mod_shard = _ada_mod(c_all, w_ada[0], b_cols)
    mod_all = _gather_mod(mod_shard)
    mod3 = lax.dynamic_slice(mod_all, (dev * nb, 0), (nb, 6 * d)).reshape(nb, 1, 6 * d)

    x2 = x.reshape(n, d)
    tgt = loss_target.reshape(n, d)
    (wf_in,) = _copy_wait("gather_in_wait", bufs_i, sems_i, plan_i, [mod3])
    (wf_in,) = _copy_blocking("gather_in_pass", [wf_in], 3, _gather_pass_plan(lays[0:1]))
    plan_o, plan_b, plan_p = _gather_direct_plan(lays[1:2]), _gather_chip_plan(lays[2:4]), _gather_pass_plan(lays[2:4])
    sems_o, bufs_o, tok_o = _copy_start("gather_out_start", [fulls[1]], 6, plan_o, [wf_in, mod3])
    sems_b, bufs_b, tok_b = _copy_start("gather_mlp_start", fulls[2:4], 6, plan_b, [tok_o])
    proj, h1, mixed, a1, q = _in_mix_fwd(x2, mod3, wf_in, dww, conf_dw_b, conf_ln_g, conf_ln_b, scw, t, tok_b)
    (wf_out,) = _copy_wait("gather_out_wait", bufs_o, sems_o, plan_o, [mixed])
    bufs_b = _copy_wait("gather_mlp_wait", bufs_b, sems_b, plan_b, [mixed])
    sems_p, bufs_p, tok_p = _copy_start("gather_pass_start", bufs_b, 6, plan_p)
    x1, y1, h2 = _out_proj(mixed, wf_out, x2, mod3, t, tok_p)
    wf_1, wf_2 = _copy_wait("gather_pass_wait", bufs_p, sems_p, plan_p, [h2])
    z, dx2, dy2, dg2, ggf, loss_p = _mlp_fwd(h2, x1, tgt, mod3, g_final.reshape(1, d), wf_1, wf_2, t)

    dz, dx1, dy1, dsh2, dsc2, dg1 = _mlp_bwd(dy2, z, x1, dx2, y1, mod3, wf_1, wf_2, t)
    g_w2 = _wgrad(z, dy2, "wgrad_mlp2", relu2=True)
    g_w1 = _wgrad(h2, dz, "wgrad_mlp1")
    g_wout = _wgrad(mixed, dy1, "wgrad_out")
    made = {}

    def behind_pair_exchange(tok):
        made["dmixed"] = _out_proj_bwd(dy1, wf_out, t, tok)
        return [made["dmixed"]]

    def behind_chip_exchange(tok):
        made["dproj"], made["sg"] = _mixer_bwd(made["dmixed"], a1, q, proj, dww, conf_ln_g, conf_ln_b, scw, t, tok)
        return [made["dproj"]]

    def behind_pair_share(tok):
        made["g_win"] = _wgrad(h1, made["dproj"], "wgrad_in", deps=[tok])
        return [made["g_win"]]

    gr_out, gr_1, gr_2 = _reduce_scatter(ids, [g_wout, g_w1, g_w2], lays[1:4], names[1:4], "m",
                                         (behind_pair_exchange, behind_chip_exchange, behind_pair_share))

    big = {}

    def behind_pair_exchange_in(tok):
        made["grad_x"], made["dsh1"], made["dsc1"] = _in_proj_bwd(made["dproj"], wf_in, x2, dx1, mod3, t, tok)
        pack, dmod8 = _pack_small(made["sg"], ggf, loss_p, (made["dsh1"], made["dsc1"], dg1, dsh2, dsc2, dg2), d,
                                  made["grad_x"])
        made["gathered"] = _all_gather8([pack, dmod8], "gather_small")
        return list(made["gathered"])

    def behind_chip_exchange_in(tok):
        red, dmod_all, g_bada = _small_reduce(*made["gathered"], nb, tok)
        dmod_cols = lax.dynamic_slice(dmod_all, (0, chip * ada_w), (N_DEV * nb, ada_w))
        made["ada"] = _adamw_ada(c_act.T, dmod_cols, w_ada[0], m_w_ada[0], v_w_ada[0])
        for nm_, g_, w_, m_, v_ in (("w_out", gr_out, w_out, m_w_out, v_w_out),
                                    ("w_mlp1", gr_1, w_mlp1, m_w_mlp1, v_w_mlp1),
                                    ("w_mlp2", gr_2, w_mlp2, m_w_mlp2, v_w_mlp2)):
            big[nm_] = _adamw_big(w_[0], g_, m_[0], v_[0], "adamw_" + nm_)
        cw = dc // N_CHIPS
        g_dww = lax.dynamic_slice(red[0:CONF_K], (0, chip * cw), (CONF_K, cw))
        g_scw = lax.dynamic_slice(red[SG_SCW:SG_SCW + SC_K], (0, chip * cw), (SC_K, cw))
        g_gfin = jnp.concatenate([red[SG_N:SG_N + 1], red[SG_N + 1:SG_N + 2]], axis=1)
        made["small_g"] = [g_bada, g_dww, red[SG_DWB:SG_DWB + 1], red[SG_LNG:SG_LNG + 1], red[SG_LNB:SG_LNB + 1],
                           g_scw, g_gfin]
        small_w = [b_ada, conf_dw_w[0], conf_dw_b, conf_ln_g, conf_ln_b, sc_conv_w[0], g_final.reshape(1, d)]
        small_m = [m_b_ada, m_conf_dw_w[0], m_conf_dw_b, m_conf_ln_g, m_conf_ln_b, m_sc_conv_w[0],
                   m_g_final.reshape(1, d)]
        small_v = [v_b_ada, v_conf_dw_w[0], v_conf_dw_b, v_conf_ln_g, v_conf_ln_b, v_sc_conv_w[0],
                   v_g_final.reshape(1, d)]
        made["upd"] = _adamw_small(small_w, made["small_g"], small_m, small_v)
        made["loss"] = red[SG_N + 2, 0]
        return [big["w_mlp2"][0], made["upd"][0]]

    (gr_in,) = _reduce_scatter(ids, [made["g_win"]], lays[0:1], names[0:1], "i",
                               (behind_pair_exchange_in, behind_chip_exchange_in, None))
    big["w_in"] = _adamw_big(w_in[0], gr_in, m_w_in[0], v_w_in[0], "adamw_w_in")
    grad_x, small_g, upd, loss = made["grad_x"], made["small_g"], made["upd"], made["loss"]
    g_wada, d_wada, nm_wada, nv_wada = made["ada"]
    ns = len(small_g)
    s_delta, s_m, s_v = upd[0:ns], upd[ns:2 * ns], upd[2 * ns:3 * ns]

    def outs(kind_big, kind_small, wada):
        sm = kind_small
        return (wada[None], sm[0], kind_big["w_in"][None], sm[1][None], sm[2], sm[3], sm[4], sm[5][None],
                kind_big["w_out"][None], kind_big["w_mlp1"][None], kind_big["w_mlp2"][None], sm[6].reshape(d))

    grads_out = outs({"w_in": gr_in, "w_out": gr_out, "w_mlp1": gr_1, "w_mlp2": gr_2}, small_g, g_wada)
    delta_out = outs({k: v[0] for k, v in big.items()}, s_delta, d_wada)
    m_out = outs({k: v[1] for k, v in big.items()}, s_m, nm_wada)
    v_out = outs({k: v[2] for k, v in big.items()}, s_v, nv_wada)
    return (loss, grad_x.reshape(nb, t, d), *grads_out, *delta_out, *m_out, *v_out)
```

```python
import functools

import jax
import jax.numpy as jnp
from jax import lax
from jax.experimental import pallas as pl
from jax.experimental.pallas import tpu as pltpu

F32 = jnp.float32
BF16 = jnp.bfloat16
MESH = pl.DeviceIdType.MESH
HIGHEST = lax.Precision.HIGHEST

EPS = 1e-6
CONF_K = 31
SC_K = 3
HALO = 32
N_CHIPS = 4
N_DEV = 8

ADAM_LR = 0.001
ADAM_B1 = 0.9
ADAM_B2 = 0.999
ADAM_EPS = 1e-08
ADAM_WD = 0.01
ADAM_STEP = 10

TM_MM = 512
TM_MIX = 256
TM_MLP = 512
FF_CHUNK = 1024
MLP_SUB = 2
TK_WG = 2048
RB_CONV = 64
RB_WG = 32
CHIP_RELS = ((1, 0), (0, 1), (1, 1))

ANY = pl.BlockSpec(memory_space=pl.ANY)
VMEM = pl.BlockSpec(memory_space=pltpu.VMEM)
HBM = pl.BlockSpec(memory_space=pltpu.HBM)
SEM = pl.BlockSpec(memory_space=pltpu.SEMAPHORE)
EFFECT = pltpu.SideEffectType.DATAFLOW_SIDE_EFFECTING


def _me():
    return lax.axis_index("x"), lax.axis_index("y"), lax.axis_index("c")


def _flip(v, f):
    return 1 - v if f else v


def _rows8(v):
    r, c = v.shape
    return v.reshape(r // 8, 8, c).sum(axis=0)


def _rms(x):
    r = lax.rsqrt(jnp.mean(x * x, axis=-1, keepdims=True) + EPS)
    return x * r, r


def _rms_bwd(dxn, xn, r):
    return r * (dxn - xn * jnp.mean(dxn * xn, axis=-1, keepdims=True))


def _sigmoid(x):
    return 1.0 / (1.0 + jnp.exp(-x))


def _dot(a, b):
    return jnp.dot(a, b, preferred_element_type=F32)


def _dot_nt(a, b):
    return lax.dot_general(a, b, (((1,), (1,)), ((), ())), preferred_element_type=F32)


def _dot_tn(a, b):
    return lax.dot_general(a, b, (((0,), (0,)), ((), ())), preferred_element_type=F32)


def _const(shape):
    nd = len(shape)
    return pl.BlockSpec(shape, lambda i: (0,) * nd)


def _resident(shape):
    nd = len(shape)
    return pl.BlockSpec(shape, lambda i: (0,) * nd, pipeline_mode=pl.Buffered(1))


def _rowblk(tm, c):
    return pl.BlockSpec((tm, c), lambda i: (i, 0))


def _modspec(tps, width):
    return pl.BlockSpec((None, 1, width), lambda i: (i // tps, 0, 0))


def _accspec(tps, c):
    return pl.BlockSpec((None, 8, c), lambda i: (i // tps, 0, 0))


def _acc_add(ref, first, v):
    @pl.when(first)
    def _():
        ref[...] = v

    @pl.when(jnp.logical_not(first))
    def _():
        ref[...] += v


def _all_gather8(arrs, name):
    n = len(arrs)

    def body(*refs):
        ins, outs = refs[:n], refs[n:2 * n]
        send, recv = refs[2 * n:]
        x, y, c = _me()
        dev = 4 * x + 2 * y + c
        for a in range(n):
            outs[a][dev] = ins[a][...]
        sends = []
        for r in range(1, N_DEV):
            fx, fy, fc = (r >> 2) & 1, (r >> 1) & 1, r & 1
            peer = (_flip(x, fx), _flip(y, fy), _flip(c, fc))
            for a in range(n):
                cp = pltpu.make_async_remote_copy(
                    src_ref=ins[a], dst_ref=outs[a].at[dev],
                    send_sem=send.at[r - 1, a], recv_sem=recv.at[r - 1, a],
                    device_id=peer, device_id_type=MESH)
                cp.start()
                sends.append(cp)
        for r in range(1, N_DEV):
            fx, fy, fc = (r >> 2) & 1, (r >> 1) & 1, r & 1
            pdev = 4 * _flip(x, fx) + 2 * _flip(y, fy) + _flip(c, fc)
            for a in range(n):
                pltpu.make_async_remote_copy(
                    src_ref=ins[a], dst_ref=outs[a].at[pdev],
                    send_sem=send.at[r - 1, a], recv_sem=recv.at[r - 1, a],
                    device_id=(x, y, c), device_id_type=MESH).wait_recv()
        for cp in sends:
            cp.wait_send()

    return pl.pallas_call(
        body, name=name,
        out_shape=[jax.ShapeDtypeStruct((N_DEV,) + a.shape, a.dtype) for a in arrs],
        in_specs=[VMEM] * n, out_specs=[VMEM] * n,
        scratch_shapes=[pltpu.SemaphoreType.DMA((N_DEV - 1, n)),
                        pltpu.SemaphoreType.DMA((N_DEV - 1, n))],
    )(*arrs)


def _gather_mod(mod_shard):
    nb, w = mod_shard.shape

    def body(in_ref, out_ref, send, recv):
        x, y, c = _me()
        chip = 2 * x + y
        out_ref[:, pl.ds(pl.multiple_of(chip * w, 128), w)] = in_ref[...]
        sends = []
        for r, (fx, fy) in enumerate(CHIP_RELS):
            cp = pltpu.make_async_remote_copy(
                src_ref=in_ref,
                dst_ref=out_ref.at[:, pl.ds(pl.multiple_of(chip * w, 128), w)],
                send_sem=send.at[r], recv_sem=recv.at[r],
                device_id=(_flip(x, fx), _flip(y, fy), c), device_id_type=MESH)
            cp.start()
            sends.append(cp)
        for r, (fx, fy) in enumerate(CHIP_RELS):
            pchip = 2 * _flip(x, fx) + _flip(y, fy)
            pltpu.make_async_remote_copy(
                src_ref=in_ref,
                dst_ref=out_ref.at[:, pl.ds(pl.multiple_of(pchip * w, 128), w)],
                send_sem=send.at[r], recv_sem=recv.at[r],
                device_id=(x, y, c), device_id_type=MESH).wait_recv()
        for cp in sends:
            cp.wait_send()

    return pl.pallas_call(
        body, name="gather_mod",
        out_shape=jax.ShapeDtypeStruct((nb, N_CHIPS * w), mod_shard.dtype),
        in_specs=[VMEM], out_specs=VMEM,
        scratch_shapes=[pltpu.SemaphoreType.DMA((3,)), pltpu.SemaphoreType.DMA((3,))],
    )(mod_shard)


def _wlayout(d):
    d_in = 5 * d // 2
    return (
        (d, d_in // N_CHIPS, True),
        (d // N_CHIPS, d, False),
        (d, 4 * d // N_CHIPS, True),
        (4 * d // N_CHIPS, d, False),
    )


def _full_shape(lay):
    r, c, by_col = lay
    return (r, c * N_CHIPS) if by_col else (r * N_CHIPS, c)


def _full_view(ref, lay, k, h):
    r, c, by_col = lay
    hr = r // 2
    if by_col:
        return ref.at[pl.ds(pl.multiple_of(h * hr, 16), hr), pl.ds(pl.multiple_of(k * c, 128), c)]
    return ref.at[pl.ds(pl.multiple_of(k * r + h * hr, 16), hr), :]


def _half_view(ref, lay, h):
    hr = lay[0] // 2
    return ref.at[pl.ds(pl.multiple_of(h * hr, 16), hr), :]


def _half_shape(lay):
    return (lay[0] // 2, lay[1])


def _hbm(a):
    return pltpu.with_memory_space_constraint(a, pltpu.HBM)


def _remote(src, dst, send, recv, idx, to):
    return lambda: pltpu.make_async_remote_copy(src_ref=src, dst_ref=dst, send_sem=send.at[idx], recv_sem=recv.at[idx],
                                                device_id=to, device_id_type=MESH)


def _copy_start(name, bufs, n_sems, plan, after=()):
    nb, na = len(bufs), len(after)

    def body(*refs):
        sends, _ = plan(refs[:nb], refs[nb + na], refs[nb + na + 1])
        for mk in sends:
            mk().start()
        refs[-1][...] = jnp.zeros((8, 128), F32)

    outs = pl.pallas_call(
        body, name=name,
        out_shape=(pltpu.SemaphoreType.DMA((n_sems,)), pltpu.SemaphoreType.DMA((n_sems,)))
        + tuple(pltpu.HBM(b.shape, b.dtype) for b in bufs) + (jax.ShapeDtypeStruct((8, 128), F32),),
        in_specs=(HBM,) * nb + (ANY,) * na, out_specs=(SEM, SEM) + (HBM,) * nb + (VMEM,),
        input_output_aliases={i: 2 + i for i in range(nb)},
        compiler_params=pltpu.CompilerParams(has_side_effects=EFFECT),
    )(*[_hbm(b) for b in bufs], *after)
    return (outs[0], outs[1]), list(outs[2:2 + nb]), outs[-1]


def _copy_wait(name, bufs, sems, plan, after):
    nb, na = len(bufs), len(after)

    def body(*refs):
        sends, recvs = plan(refs[:nb], refs[nb], refs[nb + 1])
        for mk in sends:
            mk().wait_send()
        for mk in recvs:
            mk().wait_recv()

    outs = pl.pallas_call(
        body, name=name,
        out_shape=tuple(pltpu.HBM(b.shape, b.dtype) for b in bufs),
        in_specs=(HBM,) * nb + (SEM, SEM) + (ANY,) * na, out_specs=(HBM,) * nb,
        input_output_aliases={i: i for i in range(nb)},
        compiler_params=pltpu.CompilerParams(has_side_effects=EFFECT),
    )(*bufs, *sems, *after)
    return list(outs)


def _copy_blocking(name, bufs, n_sems, plan, after=()):
    nb, na = len(bufs), len(after)

    def body(*refs):
        sends, recvs = plan(refs[:nb], refs[2 * nb + na], refs[2 * nb + na + 1])
        started = [mk() for mk in sends]
        for cp in started:
            cp.start()
        for mk in recvs:
            mk().wait_recv()
        for cp in started:
            cp.wait_send()

    return list(pl.pallas_call(
        body, name=name,
        out_shape=tuple(jax.ShapeDtypeStruct(b.shape, b.dtype) for b in bufs),
        in_specs=(ANY,) * (nb + na), out_specs=(ANY,) * nb,
        input_output_aliases={i: i for i in range(nb)},
        scratch_shapes=[pltpu.SemaphoreType.DMA((n_sems,)), pltpu.SemaphoreType.DMA((n_sems,))],
    )(*bufs, *after))


def _exchange(name, bufs, n_sems, plan, between):
    if between is None:
        return _copy_blocking(name, bufs, n_sems, plan)
    sems, bufs, tok = _copy_start(name + "_start", bufs, n_sems, plan)
    return _copy_wait(name + "_wait", bufs, sems, plan, between(tok))


def _gather_direct_plan(lays):
    def plan(full, send, recv):
        x, y, c = _me()
        chip = 2 * x + y
        sends, recvs = [], []
        for r, (fx, fy) in enumerate(CHIP_RELS):
            px, py = _flip(x, fx), _flip(y, fy)
            for i, lay in enumerate(lays):
                for q in range(2):
                    oc = _flip(c, q)
                    mine = _full_view(full[i], lay, chip, c)
                    idx = (r * len(lays) + i) * 2 + q
                    sends.append(_remote(mine, mine, send, recv, idx, (px, py, oc)))
                    theirs = _full_view(full[i], lay, 2 * px + py, oc)
                    recvs.append(_remote(theirs, theirs, send, recv, idx, (x, y, c)))
        return sends, recvs
    return plan


def _gather_chip_plan(lays):
    def plan(full, send, recv):
        x, y, c = _me()
        chip = 2 * x + y
        sends, recvs = [], []
        for r, (fx, fy) in enumerate(CHIP_RELS):
            px, py = _flip(x, fx), _flip(y, fy)
            for i, lay in enumerate(lays):
                mine = _full_view(full[i], lay, chip, c)
                sends.append(_remote(mine, mine, send, recv, r * len(lays) + i, (px, py, c)))
                theirs = _full_view(full[i], lay, 2 * px + py, c)
                recvs.append(_remote(theirs, theirs, send, recv, r * len(lays) + i, (x, y, c)))
        return sends, recvs
    return plan


def _gather_pass_plan(lays):
    def plan(full, send, recv):
        x, y, c = _me()
        sends, recvs = [], []
        for r, (fx, fy) in enumerate(CHIP_RELS):
            pchip = 2 * _flip(x, fx) + _flip(y, fy)
            for i, lay in enumerate(lays):
                landed = _full_view(full[i], lay, pchip, c)
                sends.append(_remote(landed, landed, send, recv, r * len(lays) + i, (x, y, 1 - c)))
                other = _full_view(full[i], lay, pchip, 1 - c)
                recvs.append(_remote(other, other, send, recv, r * len(lays) + i, (x, y, c)))
        return sends, recvs
    return plan


def _pair_exchange_plan(lays):
    nw = len(lays)

    def plan(bufs, send, recv):
        x, y, c = _me()
        sends, recvs = [], []
        for i, lay in enumerate(lays):
            for k in range(N_CHIPS):
                sends.append(_remote(_full_view(bufs[i], lay, k, 1 - c), bufs[nw + i].at[k],
                                     send, recv, i * N_CHIPS + k, (x, y, 1 - c)))
                recvs.append(_remote(_full_view(bufs[i], lay, k, c), bufs[nw + i].at[k],
                                     send, recv, i * N_CHIPS + k, (x, y, c)))
        return sends, recvs
    return plan


def _chip_exchange_plan(nw):
    def plan(bufs, send, recv):
        x, y, c = _me()
        sends, recvs = [], []
        for r, (fx, fy) in enumerate(CHIP_RELS):
            px, py = _flip(x, fx), _flip(y, fy)
            for i in range(nw):
                sends.append(_remote(bufs[i].at[2 * px + py], bufs[nw + i].at[r], send, recv, r * nw + i, (px, py, c)))
                recvs.append(_remote(bufs[i].at[2 * px + py], bufs[nw + i].at[r], send, recv, r * nw + i, (x, y, c)))
        return sends, recvs
    return plan


def _pair_share_plan(lays):
    def plan(bufs, send, recv):
        x, y, c = _me()
        sends, recvs = [], []
        for i, lay in enumerate(lays):
            mine = _half_view(bufs[i], lay, c)
            sends.append(_remote(mine, mine, send, recv, i, (x, y, 1 - c)))
            other = _half_view(bufs[i], lay, 1 - c)
            recvs.append(_remote(other, other, send, recv, i, (x, y, c)))
        return sends, recvs
    return plan


def _pair_sum(ids, g, got, lay, name):
    r, c, by_col = lay
    hr = r // 2
    tr = min(hr, 256)
    nt = hr // tr

    def body(ids_ref, g_ref, got_ref, s32_ref, s16_ref):
        s = g_ref[...] + got_ref[...]
        s32_ref[...] = s
        s16_ref[...] = s.astype(BF16)

    if by_col:
        gspec = pl.BlockSpec((tr, c), lambda k, t, ids: (ids[1] * nt + t, k))
    else:
        gspec = pl.BlockSpec((tr, c), lambda k, t, ids: ((2 * k + ids[1]) * nt + t, 0))
    hspec = pl.BlockSpec((None, tr, c), lambda k, t, ids: (k, t, 0))
    return pl.pallas_call(
        body, name=name,
        grid_spec=pltpu.PrefetchScalarGridSpec(
            num_scalar_prefetch=1, grid=(N_CHIPS, nt),
            in_specs=[gspec, hspec], out_specs=[hspec, hspec]),
        out_shape=[jax.ShapeDtypeStruct((N_CHIPS, hr, c), F32),
                   jax.ShapeDtypeStruct((N_CHIPS, hr, c), BF16)],
    )(ids, g, got)


def _chip_sum(ids, s32, got, lay, name):
    hr, c = _half_shape(lay)
    tr = min(hr, 256)
    nt = hr // tr

    def body(ids_ref, s_ref, got_ref, out_ref):
        t = s_ref[...]
        for r in range(3):
            t = t + got_ref[r].astype(F32)
        out_ref[...] = t

    return pl.pallas_call(
        body, name=name,
        grid_spec=pltpu.PrefetchScalarGridSpec(
            num_scalar_prefetch=1, grid=(nt,),
            in_specs=[pl.BlockSpec((None, tr, c), lambda t, ids: (ids[0], t, 0)),
                      pl.BlockSpec((3, tr, c), lambda t, ids: (0, t, 0))],
            out_specs=pl.BlockSpec((tr, c), lambda t, ids: (ids[1] * nt + t, 0))),
        out_shape=jax.ShapeDtypeStruct((2 * hr, c), F32),
    )(ids, s32, got)


def _reduce_scatter(ids, grads, lays, names, tag, between):
    nw = len(lays)
    got1 = [lax.empty((N_CHIPS,) + _half_shape(l), F32) for l in lays]
    bufs = _exchange("pair_exchange_" + tag, list(grads) + got1, nw * N_CHIPS, _pair_exchange_plan(lays), between[0])
    sums = [_pair_sum(ids, bufs[i], bufs[nw + i], lays[i], "pair_sum_" + names[i]) for i in range(nw)]
    got2 = [lax.empty((3,) + _half_shape(l), BF16) for l in lays]
    bufs = _exchange("chip_exchange_" + tag, [s[1] for s in sums] + got2, 3 * nw, _chip_exchange_plan(nw), between[1])
    mine = [_chip_sum(ids, sums[i][0], bufs[nw + i], lays[i], "chip_sum_" + names[i]) for i in range(nw)]
    return _exchange("pair_share_" + tag, mine, nw, _pair_share_plan(lays), between[2])


def _cast_place(ids, w, lay, name):
    r, c, by_col = lay

    def body(ids_ref, w_ref, out_ref):
        out_ref[...] = w_ref[...].astype(BF16)

    omap = (lambda i, ids: (0, ids[0])) if by_col else (lambda i, ids: (ids[0], 0))
    return pl.pallas_call(
        body, name=name,
        grid_spec=pltpu.PrefetchScalarGridSpec(
            num_scalar_prefetch=1, grid=(1,),
            in_specs=[pl.BlockSpec((r, c), lambda i, ids: (0, 0))],
            out_specs=pl.BlockSpec((r, c), omap)),
        out_shape=jax.ShapeDtypeStruct(_full_shape(lay), BF16),
    )(ids, w)


def _ada_mod(c_all, w_ada, b_ada):
    def body(c_ref, w_ref, b_ref, act_ref, mod_ref):
        cv = c_ref[...]
        act = cv * _sigmoid(cv)
        act_ref[...] = act
        mod_ref[...] = jnp.dot(act, w_ref[...], preferred_element_type=F32, precision=HIGHEST) + b_ref[...]

    nb = c_all.shape[0]
    return pl.pallas_call(
        body, name="ada_mod",
        out_shape=[jax.ShapeDtypeStruct(c_all.shape, F32),
                   jax.ShapeDtypeStruct((nb, w_ada.shape[1]), F32)],
        in_specs=[VMEM] * 3, out_specs=[VMEM] * 2,
    )(c_all, w_ada, b_ada)


def _run_units(*unit_lists):
    total = max(len(u) for u in unit_lists)
    done = [0] * len(unit_lists)
    for step in range(1, total + 1):
        for li, units in enumerate(unit_lists):
            upto = (step * len(units) + total - 1) // total
            while done[li] < upto:
                units[done[li]]()
                done[li] += 1


def _in_mix_fwd(x, mod3, w_in, wb, dwb, lng, lnb, scw, t, dep):
    n, d = x.shape
    d_in = w_in.shape[1]
    dc = d_in // 5
    tm = min(TM_MIX, t)
    tps = t // tm
    nt = n // tm
    rb = min(RB_CONV, tm)
    ncol = 256
    ng = dc // 128
    assert tps % 2 == 0 and nt % 2 == 0

    def proj_units(x_ref, mod_ref, w_ref, p_ref, h_ref):
        def head():
            xn, _ = _rms(x_ref[...])
            h_ref[...] = (xn * (1.0 + mod_ref[:, d:2 * d]) + mod_ref[:, 0:d]).astype(BF16)
        units = [head]
        for c0 in range(0, d_in, ncol):
            def chunk(c0=c0):
                p_ref[:, c0:c0 + ncol] = _dot(h_ref[...], w_ref[:, c0:c0 + ncol])
            units.append(chunk)
        return units

    def mix_units(first, p_ref, h_ref, r0, wb_ref, dwb_ref, lng_ref, lnb_ref, scw_ref,
                  proj_ref, h1_ref, mixed_ref, a1_ref, q_ref, ext_ref, e_ref, extp_ref, a1s_ref):
        rows = slice(r0, r0 + tm)
        units = []

        def glu():
            halo = ext_ref[tm:tm + HALO, :]
            ext_ref[0:HALO, :] = halo if first is False else jnp.where(first, 0.0, halo)
            ext_ref[HALO:HALO + tm, :] = p_ref[:, 0:dc] * _sigmoid(p_ref[:, dc:2 * dc])
        units.append(glu)
        for g in range(ng):
            def shift(g=g):
                for r in range(8):
                    e_ref[g, r, 0:tm + HALO, :] = ext_ref[r:r + tm + HALO, 128 * g:128 * g + 128]
            units.append(shift)
        for g in range(ng):
            lanes = slice(128 * g, 128 * g + 128)
            for i0 in range(0, tm, rb):
                def conv(g=g, lanes=lanes, i0=i0):
                    acc = jnp.zeros((rb, 128), F32)
                    for k in range(CONF_K):
                        m, r = divmod(k + HALO - CONF_K + 1, 8)
                        acc = acc + e_ref[g, r, i0 + 8 * m:i0 + 8 * m + rb, :] * wb_ref[k:k + 1, lanes]
                    a1s_ref[i0:i0 + rb, lanes] = acc + dwb_ref[:, lanes]
                units.append(conv)

        def norm():
            a1 = a1s_ref[...]
            a1_ref[rows, :] = a1
            mu = jnp.mean(a1, axis=-1, keepdims=True)
            ac = a1 - mu
            rstd = lax.rsqrt(jnp.mean(ac * ac, axis=-1, keepdims=True) + EPS)
            a2 = ac * rstd * lng_ref[...] + lnb_ref[...]
            mixed_ref[rows, 0:dc] = (a2 * _sigmoid(a2)).astype(BF16)
        units.append(norm)

        def short():
            halo = extp_ref[tm:tm + 8, :]
            extp_ref[0:8, :] = halo if first is False else jnp.where(first, 0.0, halo)
            extp_ref[8:8 + tm, :] = p_ref[:, 3 * dc:4 * dc] * p_ref[:, 4 * dc:5 * dc]
            q = jnp.zeros((tm, dc), F32)
            for k in range(SC_K):
                q = q + extp_ref[6 + k:6 + k + tm, :] * scw_ref[k:k + 1, :]
            q_ref[rows, :] = q
            mixed_ref[rows, dc:2 * dc] = (p_ref[:, 2 * dc:3 * dc] * q).astype(BF16)
        units.append(short)

        def keep():
            proj_ref[rows, :] = p_ref[...]
            h1_ref[rows, :] = h_ref[...]
        units.append(keep)
        return units

    def body(x0_ref, xa_ref, xb_ref, mod0_ref, moda_ref, modb_ref, w_ref,
             wb_ref, dwb_ref, lng_ref, lnb_ref, scw_ref, dep_ref,
             proj_ref, h1_ref, mixed_ref, a1_ref, q_ref,
             p0_ref, p1_ref, h0_ref, hh1_ref, ext_ref, e_ref, extp_ref, a1s_ref):
        j = pl.program_id(0)

        @pl.when(j == 0)
        def _():
            ext_ref[...] = jnp.zeros(ext_ref.shape, F32)
            extp_ref[...] = jnp.zeros(extp_ref.shape, F32)
            _run_units(proj_units(x0_ref, mod0_ref, w_ref, p0_ref, h0_ref))

        common = (wb_ref, dwb_ref, lng_ref, lnb_ref, scw_ref, proj_ref, h1_ref, mixed_ref, a1_ref, q_ref,
                  ext_ref, e_ref, extp_ref, a1s_ref)
        _run_units(mix_units((2 * j) % tps == 0, p0_ref, h0_ref, 0, *common),
                   proj_units(xa_ref, moda_ref, w_ref, p1_ref, hh1_ref))
        _run_units(mix_units(False, p1_ref, hh1_ref, tm, *common),
                   proj_units(xb_ref, modb_ref, w_ref, p0_ref, h0_ref))

    last = nt - 1
    xspec = lambda f: pl.BlockSpec((tm, d), lambda j: (f(j), 0))
    mspec = lambda f: pl.BlockSpec((None, 1, 6 * d), lambda j: (f(j) // tps, 0, 0))
    out2 = lambda c: pl.BlockSpec((2 * tm, c), lambda j: (j, 0))
    return pl.pallas_call(
        body, name="in_mix_fwd", grid=(nt // 2,),
        in_specs=[xspec(lambda j: 0), xspec(lambda j: 2 * j + 1), xspec(lambda j: jnp.minimum(2 * j + 2, last)),
                  mspec(lambda j: 0), mspec(lambda j: 2 * j + 1), mspec(lambda j: jnp.minimum(2 * j + 2, last)),
                  _resident((d, d_in)),
                  _const(wb.shape), _const(dwb.shape), _const(lng.shape), _const(lnb.shape), _const(scw.shape), ANY],
        out_specs=[out2(d_in), out2(d), out2(2 * dc), out2(dc), out2(dc)],
        out_shape=[jax.ShapeDtypeStruct((n, d_in), F32), jax.ShapeDtypeStruct((n, d), BF16),
                   jax.ShapeDtypeStruct((n, 2 * dc), BF16), jax.ShapeDtypeStruct((n, dc), F32),
                   jax.ShapeDtypeStruct((n, dc), F32)],
        scratch_shapes=[pltpu.VMEM((tm, d_in), F32), pltpu.VMEM((tm, d_in), F32),
                        pltpu.VMEM((tm, d), BF16), pltpu.VMEM((tm, d), BF16),
                        pltpu.VMEM((tm + HALO + 8, dc), F32), _shift_scratch(tm, dc),
                        pltpu.VMEM((tm + 8, dc), F32), pltpu.VMEM((tm, dc), F32)],
        compiler_params=pltpu.CompilerParams(dimension_semantics=("arbitrary",)),
    )(x, x, x, mod3, mod3, mod3, w_in, wb, dwb, lng, lnb, scw, dep)


def _shift_scratch(tm, dc):
    return pltpu.VMEM((dc // 128, 8, tm + HALO + 8, 128), F32)


def _out_proj(mixed, w_out, x, mod3, t, dep):
    n, d = x.shape
    tm = min(TM_MM, t)
    tps = t // tm

    def body(m_ref, w_ref, x_ref, mod_ref, dep_ref, x1_ref, y1_ref, h2_ref):
        y1 = _dot(m_ref[...], w_ref[...])
        y1_ref[...] = y1.astype(BF16)
        x1 = x_ref[...] + mod_ref[:, 2 * d:3 * d] * y1
        x1_ref[...] = x1
        xn, _ = _rms(x1)
        h2_ref[...] = (xn * (1.0 + mod_ref[:, 4 * d:5 * d]) + mod_ref[:, 3 * d:4 * d]).astype(BF16)

    return pl.pallas_call(
        body, name="out_proj", grid=(n // tm,),
        in_specs=[_rowblk(tm, d), _resident((d, d)), _rowblk(tm, d), _modspec(tps, 6 * d), ANY],
        out_specs=[_rowblk(tm, d), _rowblk(tm, d), _rowblk(tm, d)],
        out_shape=[jax.ShapeDtypeStruct((n, d), F32), jax.ShapeDtypeStruct((n, d), BF16),
                   jax.ShapeDtypeStruct((n, d), BF16)],
        compiler_params=pltpu.CompilerParams(dimension_semantics=("parallel",)),
    )(mixed, w_out, x, mod3, dep)


def _mlp_fwd(h2, x1, tgt, mod3, gfin, w1, w2, t):
    n, d = x1.shape
    dff = w1.shape[1]
    tm = min(TM_MLP, t)
    tps = t // tm
    nt = n // tm

    def body(h_ref, x1_ref, tg_ref, mod_ref, gf_ref, w1_ref, w2_ref,
             z_ref, dx2_ref, dy2_ref, dg2_ref, ggf_ref, loss_ref):
        i = pl.program_id(0)
        g2 = mod_ref[:, 5 * d:6 * d]
        gf = gf_ref[...]
        sub = tm // MLP_SUB
        sums = None
        for part in range(MLP_SUB):
            rs = slice(part * sub, (part + 1) * sub)
            hv = h_ref[rs, :]
            y2 = jnp.zeros((sub, d), F32)
            for j in range(dff // FF_CHUNK):
                cols = slice(j * FF_CHUNK, (j + 1) * FF_CHUNK)
                z = _dot(hv, w1_ref[:, cols])
                z_ref[rs, cols] = z.astype(BF16)
                zr = jnp.maximum(z, 0.0)
                y2 = y2 + _dot((zr * zr).astype(BF16), w2_ref[cols, :])
            x2n, r3 = _rms(x1_ref[rs, :] + g2 * y2)
            diff = x2n * gf - tg_ref[rs, :]
            dout = diff * (1.0 / d)
            dx2 = _rms_bwd(dout * gf, x2n, r3)
            dx2_ref[rs, :] = dx2
            dy2_ref[rs, :] = (g2 * dx2).astype(BF16)
            p = (_rows8(dx2 * y2), _rows8(dout * x2n), _rows8(diff * diff))
            sums = p if sums is None else tuple(a + b for a, b in zip(sums, p))
        _acc_add(dg2_ref, i % tps == 0, sums[0])
        _acc_add(ggf_ref, i == 0, sums[1])
        _acc_add(loss_ref, i == 0, sums[2])

    return pl.pallas_call(
        body, name="mlp_fwd", grid=(nt,),
        in_specs=[_rowblk(tm, d), _rowblk(tm, d), _rowblk(tm, d), _modspec(tps, 6 * d), _const((1, d)),
                  _resident((d, dff)), _resident((dff, d))],
        out_specs=[_rowblk(tm, dff), _rowblk(tm, d), _rowblk(tm, d), _accspec(tps, d),
                   _const((8, d)), _const((8, d))],
        out_shape=[jax.ShapeDtypeStruct((n, dff), BF16), jax.ShapeDtypeStruct((n, d), F32),
                   jax.ShapeDtypeStruct((n, d), BF16), jax.ShapeDtypeStruct((n // t, 8, d), F32),
                   jax.ShapeDtypeStruct((8, d), F32), jax.ShapeDtypeStruct((8, d), F32)],
        compiler_params=pltpu.CompilerParams(dimension_semantics=("arbitrary",)),
    )(h2, x1, tgt, mod3, gfin, w1, w2)


def _mlp_bwd(dy2, z, x1, dx2, y1, mod3, w1, w2, t):
    n, d = x1.shape
    dff = w1.shape[1]
    tm = min(TM_MLP, t)
    tps = t // tm

    def body(dy2_ref, z_ref, x1_ref, dx2_ref, y1_ref, mod_ref, w1_ref, w2_ref,
             dz_ref, dx1_ref, dy1_ref, dsh_ref, dsc_ref, dg1_ref):
        first = pl.program_id(0) % tps == 0
        sub = tm // MLP_SUB
        sums = None
        for part in range(MLP_SUB):
            rs = slice(part * sub, (part + 1) * sub)
            dy2 = dy2_ref[rs, :]
            dh2 = jnp.zeros((sub, d), F32)
            for j in range(dff // FF_CHUNK):
                cols = slice(j * FF_CHUNK, (j + 1) * FF_CHUNK)
                du = _dot_nt(dy2, w2_ref[cols, :])
                dz = (du * (2.0 * jnp.maximum(z_ref[rs, cols].astype(F32), 0.0))).astype(BF16)
                dz_ref[rs, cols] = dz
                dh2 = dh2 + _dot_nt(dz, w1_ref[:, cols])
            x1n, r2 = _rms(x1_ref[rs, :])
            dx1 = dx2_ref[rs, :] + _rms_bwd(dh2 * (1.0 + mod_ref[:, 4 * d:5 * d]), x1n, r2)
            dx1_ref[rs, :] = dx1
            dy1_ref[rs, :] = (mod_ref[:, 2 * d:3 * d] * dx1).astype(BF16)
            p = (_rows8(dh2), _rows8(dh2 * x1n), _rows8(dx1 * y1_ref[rs, :].astype(F32)))
            sums = p if sums is None else tuple(a + b for a, b in zip(sums, p))
        _acc_add(dsh_ref, first, sums[0])
        _acc_add(dsc_ref, first, sums[1])
        _acc_add(dg1_ref, first, sums[2])

    acc = jax.ShapeDtypeStruct((n // t, 8, d), F32)
    return pl.pallas_call(
        body, name="mlp_bwd", grid=(n // tm,),
        in_specs=[_rowblk(tm, d), _rowblk(tm, dff), _rowblk(tm, d), _rowblk(tm, d), _rowblk(tm, d),
                  _modspec(tps, 6 * d), _resident((d, dff)), _resident((dff, d))],
        out_specs=[_rowblk(tm, dff), _rowblk(tm, d), _rowblk(tm, d),
                   _accspec(tps, d), _accspec(tps, d), _accspec(tps, d)],
        out_shape=[jax.ShapeDtypeStruct((n, dff), BF16), jax.ShapeDtypeStruct((n, d), F32),
                   jax.ShapeDtypeStruct((n, d), BF16), acc, acc, acc],
        compiler_params=pltpu.CompilerParams(dimension_semantics=("arbitrary",)),
    )(dy2, z, x1, dx2, y1, mod3, w1, w2)


def _wgrad(a, b, name, relu2=False, bn=None, deps=()):
    n, ka = a.shape
    nb = b.shape[1]
    tk = min(TK_WG, n)
    bm = min(ka, 1024)
    if bn is None:
        bn = nb if nb <= 2048 else nb // 2

    def body(a_ref, b_ref, *rest):
        out_ref = rest[-1]
        av = a_ref[...]
        if relu2:
            ar = jnp.maximum(av, 0.0)
            av = ar * ar
        p = _dot_tn(av, b_ref[...])
        _acc_add(out_ref, pl.program_id(2) == 0, p)

    return pl.pallas_call(
        body, name=name, grid=(ka // bm, nb // bn, n // tk),
        in_specs=[pl.BlockSpec((tk, bm), lambda i, j, k: (k, i)),
                  pl.BlockSpec((tk, bn), lambda i, j, k: (k, j))] + [ANY] * len(deps),
        out_specs=pl.BlockSpec((bm, bn), lambda i, j, k: (i, j)),
        out_shape=jax.ShapeDtypeStruct((ka, nb), F32),
        compiler_params=pltpu.CompilerParams(dimension_semantics=("parallel", "parallel", "arbitrary")),
    )(a, b, *deps)


def _out_proj_bwd(dy1, w_out, t, dep):
    n, d = dy1.shape
    tm = min(TM_MM, t)

    def body(dy_ref, w_ref, dep_ref, dm_ref):
        dm_ref[...] = _dot_nt(dy_ref[...], w_ref[...])

    return pl.pallas_call(
        body, name="out_proj_bwd", grid=(n // tm,),
        in_specs=[_rowblk(tm, d), _resident(w_out.shape), ANY],
        out_specs=_rowblk(tm, w_out.shape[0]),
        out_shape=jax.ShapeDtypeStruct((n, w_out.shape[0]), F32),
        compiler_params=pltpu.CompilerParams(dimension_semantics=("parallel",)),
    )(dy1, w_out, dep)


SG_DWW = 0
SG_DWB = CONF_K
SG_LNG = CONF_K + 1
SG_LNB = CONF_K + 2
SG_SCW = CONF_K + 3
SG_N = CONF_K + 3 + SC_K


def _mix_in_bwd(dmixed, a1, q, proj, wb, lng, lnb, scw, w_in, x, dx1, mod3, t, dep):
    n, d_in = proj.shape
    d = x.shape[1]
    dc = d_in // 5
    tm = min(TM_MIX, t)
    tps = t // tm
    nt = n // tm
    hb = tm // HALO
    nh = n // HALO
    rw = min(RB_WG, tm)
    ng = dc // 128
    ncol = 256

    def ln_bwd(a1v, da3, lng_v, lnb_v):
        mu = jnp.mean(a1v, axis=-1, keepdims=True)
        ac = a1v - mu
        rstd = lax.rsqrt(jnp.mean(ac * ac, axis=-1, keepdims=True) + EPS)
        ah = ac * rstd
        a2 = ah * lng_v + lnb_v
        s2 = _sigmoid(a2)
        da2 = da3 * (s2 * (1.0 + a2 * (1.0 - s2)))
        dah = da2 * lng_v
        da1 = rstd * (dah - jnp.mean(dah, axis=-1, keepdims=True)
                      - ah * jnp.mean(dah * ah, axis=-1, keepdims=True))
        return da1, da2, ah

    def mix_units(k, dm_ref, dmn_ref, a1_ref, a1n_ref, q_ref, p_ref, pn_ref, wb_ref, lng_ref, lnb_ref, scw_ref,
                  dproj_ref, sg_ref, extd_ref, ed_ref, a0_ref, da0_ref, extq_ref):
        keep_next = jnp.where(k % tps == tps - 1, 0.0, 1.0)
        units = []

        def head():
            lng_v, lnb_v = lng_ref[...], lnb_ref[...]
            da1, da2, ah = ln_bwd(a1_ref[...], dm_ref[:, 0:dc], lng_v, lnb_v)
            da1n, _, _ = ln_bwd(a1n_ref[...], dmn_ref[:, 0:dc], lng_v, lnb_v)
            sg_ref[8 * SG_LNG:8 * SG_LNG + 8, :] += _rows8(da2 * ah)
            sg_ref[8 * SG_LNB:8 * SG_LNB + 8, :] += _rows8(da2)
            sg_ref[8 * SG_DWB:8 * SG_DWB + 8, :] += _rows8(da1)
            a0_ref[...] = p_ref[:, 0:dc] * _sigmoid(p_ref[:, dc:2 * dc])
            extd_ref[0:tm, :] = da1
            extd_ref[tm:tm + HALO, :] = da1n * keep_next
            extd_ref[tm + HALO:tm + HALO + 8, :] = jnp.zeros((8, dc), F32)
        units.append(head)
        for g in range(ng):
            def shift(g=g):
                for r in range(8):
                    ed_ref[g, r, 0:tm + HALO, :] = extd_ref[r:r + tm + HALO, 128 * g:128 * g + 128]
            units.append(shift)
        for g in range(ng):
            lanes = slice(128 * g, 128 * g + 128)
            accs = [None] * CONF_K
            for i0 in range(0, tm, rw):
                def conv(g=g, lanes=lanes, i0=i0, accs=accs):
                    a0v = a0_ref[i0:i0 + rw, lanes]
                    acc = jnp.zeros((rw, 128), F32)
                    for s in range(CONF_K):
                        m, r = divmod(s, 8)
                        e = ed_ref[g, r, i0 + 8 * m:i0 + 8 * m + rw, :]
                        acc = acc + e * wb_ref[CONF_K - 1 - s:CONF_K - s, lanes]
                        part = _rows8(e * a0v)
                        accs[s] = part if accs[s] is None else accs[s] + part
                    da0_ref[i0:i0 + rw, lanes] = acc
                units.append(conv)

            def flush(lanes=lanes, accs=accs):
                for s in range(CONF_K):
                    kk = CONF_K - 1 - s
                    sg_ref[8 * (SG_DWW + kk):8 * (SG_DWW + kk) + 8, lanes] += accs[s]
            units.append(flush)

        def emit():
            da0 = da0_ref[...]
            sig = _sigmoid(p_ref[:, dc:2 * dc])
            dproj_ref[:, 0:dc] = (da0 * sig).astype(BF16)
            dproj_ref[:, dc:2 * dc] = (da0 * a0_ref[...] * (1.0 - sig)).astype(BF16)
        units.append(emit)

        def short():
            ds = dm_ref[:, dc:2 * dc]
            scb, scc, sch = p_ref[:, 2 * dc:3 * dc], p_ref[:, 3 * dc:4 * dc], p_ref[:, 4 * dc:5 * dc]
            dproj_ref[:, 2 * dc:3 * dc] = (ds * q_ref[...]).astype(BF16)
            extq_ref[0:tm, :] = ds * scb
            extq_ref[tm:tm + 8, :] = dmn_ref[0:8, dc:2 * dc] * pn_ref[0:8, :] * keep_next
            pv = scc * sch
            dp = jnp.zeros((tm, dc), F32)
            for kk in range(SC_K):
                dqs = extq_ref[SC_K - 1 - kk:SC_K - 1 - kk + tm, :]
                dp = dp + dqs * scw_ref[kk:kk + 1, :]
                sg_ref[8 * (SG_SCW + kk):8 * (SG_SCW + kk) + 8, :] += _rows8(pv * dqs)
            dproj_ref[:, 3 * dc:4 * dc] = (dp * sch).astype(BF16)
            dproj_ref[:, 4 * dc:5 * dc] = (dp * scc).astype(BF16)
        units.append(short)
        return units

    def proj_units(first, dp_ref, w_ref, x_ref, dx1_ref, mod_ref, dh_ref, gx_ref, dsh_ref, dsc_ref):
        units = []
        for c0 in range(0, d, ncol):
            def chunk(c0=c0):
                dh_ref[:, c0:c0 + ncol] = _dot_nt(dp_ref[...], w_ref[c0:c0 + ncol, :])
            units.append(chunk)

        def tail():
            dh1 = dh_ref[...]
            xn, r1 = _rms(x_ref[...])
            v1, v2 = _rows8(dh1), _rows8(dh1 * xn)
            dsh_ref[...] = jnp.where(first, v1, dsh_ref[...] + v1)
            dsc_ref[...] = jnp.where(first, v2, dsc_ref[...] + v2)
            gx_ref[...] = dx1_ref[...] + _rms_bwd(dh1 * (1.0 + mod_ref[:, d:2 * d]), xn, r1)
        units.append(tail)
        return units

    def body(dm_ref, dmn_ref, a1_ref, a1n_ref, q_ref, p_ref, pn_ref, wb_ref, lng_ref, lnb_ref, scw_ref,
             w_ref, x_ref, dx1_ref, mod_ref, dep_ref,
             dproj_ref, sg_ref, gx_ref, dsh_ref, dsc_ref,
             extd_ref, ed_ref, a0_ref, da0_ref, extq_ref, dp_ref, dh_ref):
        k = pl.program_id(0)

        @pl.when(k == 0)
        def _():
            sg_ref[...] = jnp.zeros(sg_ref.shape, F32)
            dp_ref[...] = jnp.zeros(dp_ref.shape, BF16)

        first = jnp.logical_or(k <= 1, (k - 1) % tps == 0)
        after = lambda: proj_units(first, dp_ref, w_ref, x_ref, dx1_ref, mod_ref, dh_ref, gx_ref, dsh_ref, dsc_ref)

        @pl.when(k < nt)
        def _():
            _run_units(mix_units(k, dm_ref, dmn_ref, a1_ref, a1n_ref, q_ref, p_ref, pn_ref, wb_ref, lng_ref, lnb_ref,
                                 scw_ref, dproj_ref, sg_ref, extd_ref, ed_ref, a0_ref, da0_ref, extq_ref), after())
            dp_ref[...] = dproj_ref[...]

        @pl.when(k == nt)
        def _():
            _run_units(after())

    cur = lambda k: jnp.minimum(k, nt - 1)
    prev = lambda k: jnp.maximum(k - 1, 0)
    nxt = lambda k: jnp.minimum((cur(k) + 1) * hb, nh - 1)
    acc = jax.ShapeDtypeStruct((n // t, 8, d), F32)
    return pl.pallas_call(
        body, name="mix_in_bwd", grid=(nt + 1,),
        in_specs=[pl.BlockSpec((tm, 2 * dc), lambda k: (cur(k), 0)), pl.BlockSpec((HALO, 2 * dc), lambda k: (nxt(k), 0)),
                  pl.BlockSpec((tm, dc), lambda k: (cur(k), 0)), pl.BlockSpec((HALO, dc), lambda k: (nxt(k), 0)),
                  pl.BlockSpec((tm, dc), lambda k: (cur(k), 0)),
                  pl.BlockSpec((tm, d_in), lambda k: (cur(k), 0)),
                  pl.BlockSpec((HALO, dc), lambda k: (nxt(k), 2)),
                  _const(wb.shape), _const(lng.shape), _const(lnb.shape), _const(scw.shape),
                  _resident((d, d_in)),
                  pl.BlockSpec((tm, d), lambda k: (prev(k), 0)), pl.BlockSpec((tm, d), lambda k: (prev(k), 0)),
                  pl.BlockSpec((None, 1, 6 * d), lambda k: (prev(k) // tps, 0, 0)), ANY],
        out_specs=[pl.BlockSpec((tm, d_in), lambda k: (cur(k), 0)), _const((8 * SG_N, dc)),
                   pl.BlockSpec((tm, d), lambda k: (prev(k), 0)),
                   pl.BlockSpec((None, 8, d), lambda k: (prev(k) // tps, 0, 0)),
                   pl.BlockSpec((None, 8, d), lambda k: (prev(k) // tps, 0, 0))],
        out_shape=[jax.ShapeDtypeStruct((n, d_in), BF16), jax.ShapeDtypeStruct((8 * SG_N, dc), F32),
                   jax.ShapeDtypeStruct((n, d), F32), acc, acc],
        scratch_shapes=[pltpu.VMEM((tm + HALO + 8, dc), F32), _shift_scratch(tm, dc),
                        pltpu.VMEM((tm, dc), F32), pltpu.VMEM((tm, dc), F32),
                        pltpu.VMEM((tm + 8, dc), F32),
                        pltpu.VMEM((tm, d_in), BF16), pltpu.VMEM((tm, d), F32)],
        compiler_params=pltpu.CompilerParams(dimension_semantics=("arbitrary",)),
    )(dmixed, dmixed, a1, a1, q, proj, proj, wb, lng, lnb, scw, w_in, x, dx1, mod3, dep)


SMALL_ROWS = 40


def _pack_small(sg, ggf, loss, accs, d, dep):
    dc = d // 2
    nb = accs[0].shape[0]

    def body(sg_ref, ggf_ref, loss_ref, dsh1, dsc1, dg1, dsh2, dsc2, dg2, dep_ref, pack_ref, dmod_ref):
        pack_ref[...] = jnp.zeros(pack_ref.shape, F32)
        for k in range(SG_N):
            pack_ref[k:k + 1, :] = jnp.sum(sg_ref[8 * k:8 * k + 8, :], axis=0, keepdims=True)
        gf = jnp.sum(ggf_ref[...], axis=0, keepdims=True)
        pack_ref[SG_N:SG_N + 1, :] = gf[:, 0:dc]
        pack_ref[SG_N + 1:SG_N + 2, :] = gf[:, dc:d]
        tot = jnp.sum(jnp.sum(loss_ref[...], axis=0, keepdims=True), axis=1, keepdims=True) * (0.5 / d)
        pack_ref[SG_N + 2:SG_N + 3, :] = jnp.broadcast_to(tot, (1, dc))
        dmod_ref[...] = jnp.zeros(dmod_ref.shape, F32)
        for j, ref in enumerate((dsh1, dsc1, dg1, dsh2, dsc2, dg2)):
            for b in range(nb):
                dmod_ref[b:b + 1, j * d:(j + 1) * d] = jnp.sum(ref[b], axis=0, keepdims=True)

    return pl.pallas_call(
        body, name="pack_small",
        out_shape=[jax.ShapeDtypeStruct((SMALL_ROWS, dc), F32), jax.ShapeDtypeStruct((8, 6 * d), F32)],
        in_specs=[VMEM] * 9 + [ANY], out_specs=[VMEM] * 2,
    )(sg, ggf, loss, *accs, dep)


def _small_reduce(pack_all, dmod_all, nb, dep):
    def body(pk_ref, dm_ref, dep_ref, red_ref, dmod_ref, gb_ref):
        tot = pk_ref[0]
        for dev in range(1, N_DEV):
            tot = tot + pk_ref[dev]
        red_ref[...] = tot
        gb = jnp.zeros((1, dm_ref.shape[2]), F32)
        for dev in range(N_DEV):
            for b in range(nb):
                row = dm_ref[dev, b:b + 1, :]
                dmod_ref[dev * nb + b:dev * nb + b + 1, :] = row
                gb = gb + row
        gb_ref[...] = gb

    return pl.pallas_call(
        body, name="small_reduce",
        out_shape=[jax.ShapeDtypeStruct(pack_all.shape[1:], F32),
                   jax.ShapeDtypeStruct((N_DEV * nb, dmod_all.shape[2]), F32),
                   jax.ShapeDtypeStruct((1, dmod_all.shape[2]), F32)],
        in_specs=[VMEM] * 2 + [ANY], out_specs=[VMEM] * 3,
    )(pack_all, dmod_all, dep)


def _adam(w, g, m, v):
    m = ADAM_B1 * m + (1.0 - ADAM_B1) * g
    v = ADAM_B2 * v + (1.0 - ADAM_B2) * (g * g)
    m_hat = m / (1.0 - ADAM_B1 ** ADAM_STEP)
    v_hat = v / (1.0 - ADAM_B2 ** ADAM_STEP)
    delta = -ADAM_LR * (m_hat / (jnp.sqrt(v_hat) + ADAM_EPS) + ADAM_WD * w)
    return delta, m, v


def _adamw_big(w, g, m, v, name):
    r, c = w.shape
    tr = min(r, 256)

    def body(w_ref, g_ref, m_ref, v_ref, d_ref, nm_ref, nv_ref):
        d_ref[...], nm_ref[...], nv_ref[...] = _adam(w_ref[...], g_ref[...], m_ref[...], v_ref[...])

    s = jax.ShapeDtypeStruct((r, c), F32)
    return pl.pallas_call(
        body, name=name, grid=(r // tr,),
        in_specs=[_rowblk(tr, c)] * 4, out_specs=[_rowblk(tr, c)] * 3, out_shape=[s, s, s],
        compiler_params=pltpu.CompilerParams(dimension_semantics=("parallel",)),
    )(w, g, m, v)


def _adamw_ada(act_t, dmod_cols, w, m, v):
    r, c = w.shape
    tr = min(r, 256)
    nb = act_t.shape[1]

    def body(a_ref, dm_ref, w_ref, m_ref, v_ref, g_ref, d_ref, nm_ref, nv_ref):
        g = jnp.dot(a_ref[...], dm_ref[...], preferred_element_type=F32, precision=HIGHEST)
        g_ref[...] = g
        d_ref[...], nm_ref[...], nv_ref[...] = _adam(w_ref[...], g, m_ref[...], v_ref[...])

    s = jax.ShapeDtypeStruct((r, c), F32)
    return pl.pallas_call(
        body, name="adamw_w_ada", grid=(r // tr,),
        in_specs=[_rowblk(tr, nb), _const((nb, c))] + [_rowblk(tr, c)] * 3,
        out_specs=[_rowblk(tr, c)] * 4, out_shape=[s, s, s, s],
        compiler_params=pltpu.CompilerParams(dimension_semantics=("parallel",)),
    )(act_t, dmod_cols, w, m, v)


def _adamw_small(ws, gs, ms, vs):
    n = len(ws)

    def body(*refs):
        for i in range(n):
            w, g, m, v = (refs[j * n + i][...] for j in range(4))
            dl, nm, nv = _adam(w, g, m, v)
            refs[4 * n + i][...] = dl
            refs[5 * n + i][...] = nm
            refs[6 * n + i][...] = nv

    shapes = [jax.ShapeDtypeStruct(w.shape, F32) for w in ws]
    return pl.pallas_call(
        body, name="adamw_small", out_shape=shapes * 3,
        in_specs=[VMEM] * (4 * n), out_specs=[VMEM] * (3 * n),
    )(*ws, *gs, *ms, *vs)


def kernel(x, c, w_ada, b_ada, w_in, conf_dw_w, conf_dw_b, conf_ln_g, conf_ln_b, sc_conv_w, w_out, w_mlp1, w_mlp2, g_final, loss_target, m_w_ada, m_b_ada, m_w_in, m_conf_dw_w, m_conf_dw_b, m_conf_ln_g, m_conf_ln_b, m_sc_conv_w, m_w_out, m_w_mlp1, m_w_mlp2, m_g_final, v_w_ada, v_b_ada, v_w_in, v_conf_dw_w, v_conf_dw_b, v_conf_ln_g, v_conf_ln_b, v_sc_conv_w, v_w_out, v_w_mlp1, v_w_mlp2, v_g_final):
    nb, t, d = x.shape
    n = nb * t
    dc = d // 2
    ada_w = w_ada.shape[2]
    ax, ay, ac = _me()
    chip = 2 * ax + ay
    dev = 2 * chip + ac
    ids = jnp.stack([chip, ac]).astype(jnp.int32)

    lays = _wlayout(d)
    names = ("in", "out", "mlp1", "mlp2")
    fulls = [_cast_place(ids, w[0], lays[i], "cast_" + names[i]) for i, w in enumerate((w_in, w_out, w_mlp1, w_mlp2))]

    c_pad = jnp.zeros((8, d), F32).at[0:nb].set(c)
    cw_pad = jnp.zeros((SMALL_ROWS, dc // N_CHIPS), F32)
    cw_pad = cw_pad.at[0:CONF_K].set(conf_dw_w[0]).at[HALO:HALO + SC_K].set(sc_conv_w[0])
    c_all8, cw_all8 = _all_gather8([c_pad, cw_pad], "gather_c")
    plan_i = _gather_chip_plan(lays[0:1])
    sems_i, bufs_i, tok_i = _copy_start("gather_in_start", [fulls[0]], 3, plan_i, [c_all8])
    c_all = c_all8[:, 0:nb].reshape(N_DEV * nb, d) + tok_i[0, 0]
    cw_full = jnp.concatenate([cw_all8[2 * k] for k in range(N_CHIPS)], axis=1)
    dww, scw = cw_full[0:CONF_K], cw_full[HALO:HALO + SC_K]
    b_cols = lax.dynamic_slice(b_ada, (0, chip * ada_w), (1, ada_w))
    c_act, mod_shard = _ada_mod(c_all, w_ada[0], b_cols)
    mod_all = _gather_mod(mod_shard)
    mod3 = lax.dynamic_slice(mod_all, (dev * nb, 0), (nb, 6 * d)).reshape(nb, 1, 6 * d)

    x2 = x.reshape(n, d)
    tgt = loss_target.reshape(n, d)
    (wf_in,) = _copy_wait("gather_in_wait", bufs_i, sems_i, plan_i, [mod3])
    (wf_in,) = _copy_blocking("gather_in_pass", [wf_in], 3, _gather_pass_plan(lays[0:1]))
    plan_o, plan_b, plan_p = _gather_direct_plan(lays[1:2]), _gather_chip_plan(lays[2:4]), _gather_pass_plan(lays[2:4])
    sems_o, bufs_o, tok_o = _copy_start("gather_out_start", [fulls[1]], 6, plan_o, [wf_in, mod3])
    sems_b, bufs_b, tok_b = _copy_start("gather_mlp_start", fulls[2:4], 6, plan_b, [tok_o])
    proj, h1, mixed, a1, q = _in_mix_fwd(x2, mod3, wf_in, dww, conf_dw_b, conf_ln_g, conf_ln_b, scw, t, tok_b)
    (wf_out,) = _copy_wait("gather_out_wait", bufs_o, sems_o, plan_o, [mixed])
    bufs_b = _copy_wait("gather_mlp_wait", bufs_b, sems_b, plan_b, [mixed])
    sems_p, bufs_p, tok_p = _copy_start("gather_pass_start", bufs_b, 6, plan_p)
    x1, y1, h2 = _out_proj(mixed, wf_out, x2, mod3, t, tok_p)
    wf_1, wf_2 = _copy_wait("gather_pass_wait", bufs_p, sems_p, plan_p, [h2])
    z, dx2, dy2, dg2, ggf, loss_p = _mlp_fwd(h2, x1, tgt, mod3, g_final.reshape(1, d), wf_1, wf_2, t)

    dz, dx1, dy1, dsh2, dsc2, dg1 = _mlp_bwd(dy2, z, x1, dx2, y1, mod3, wf_1, wf_2, t)
    g_w2 = _wgrad(z, dy2, "wgrad_mlp2", relu2=True)
    g_w1 = _wgrad(h2, dz, "wgrad_mlp1")
    g_wout = _wgrad(mixed, dy1, "wgrad_out")
    made = {}

    def behind_pair_exchange(tok):
        made["dmixed"] = _out_proj_bwd(dy1, wf_out, t, tok)
        return [made["dmixed"]]

    def behind_chip_exchange(tok):
        made["dproj"], made["sg"], made["grad_x"], made["dsh1"], made["dsc1"] = _mix_in_bwd(
            made["dmixed"], a1, q, proj, dww, conf_ln_g, conf_ln_b, scw, wf_in, x2, dx1, mod3, t, tok)
        return [made["dproj"]]

    def behind_pair_share(tok):
        made["g_win"] = _wgrad(h1, made["dproj"], "wgrad_in", deps=[tok])
        return [made["g_win"]]

    gr_out, gr_1, gr_2 = _reduce_scatter(ids, [g_wout, g_w1, g_w2], lays[1:4], names[1:4], "m",
                                         (behind_pair_exchange, behind_chip_exchange, behind_pair_share))

    big = {}

    def behind_pair_exchange_in(tok):
        pack, dmod8 = _pack_small(made["sg"], ggf, loss_p, (made["dsh1"], made["dsc1"], dg1, dsh2, dsc2, dg2), d, tok)
        made["gathered"] = _all_gather8([pack, dmod8], "gather_small")
        return list(made["gathered"])

    def behind_chip_exchange_in(tok):
        red, dmod_all, g_bada = _small_reduce(*made["gathered"], nb, tok)
        dmod_cols = lax.dynamic_slice(dmod_all, (0, chip * ada_w), (N_DEV * nb, ada_w))
        made["ada"] = _adamw_ada(c_act.T, dmod_cols, w_ada[0], m_w_ada[0], v_w_ada[0])
        for nm_, g_, w_, m_, v_ in (("w_out", gr_out, w_out, m_w_out, v_w_out),
                                    ("w_mlp1", gr_1, w_mlp1, m_w_mlp1, v_w_mlp1),
                                    ("w_mlp2", gr_2, w_mlp2, m_w_mlp2, v_w_mlp2)):
            big[nm_] = _adamw_big(w_[0], g_, m_[0], v_[0], "adamw_" + nm_)
        cw = dc // N_CHIPS
        g_dww = lax.dynamic_slice(red[0:CONF_K], (0, chip * cw), (CONF_K, cw))
        g_scw = lax.dynamic_slice(red[SG_SCW:SG_SCW + SC_K], (0, chip * cw), (SC_K, cw))
        g_gfin = jnp.concatenate([red[SG_N:SG_N + 1], red[SG_N + 1:SG_N + 2]], axis=1)
        made["small_g"] = [g_bada, g_dww, red[SG_DWB:SG_DWB + 1], red[SG_LNG:SG_LNG + 1], red[SG_LNB:SG_LNB + 1],
                           g_scw, g_gfin]
        small_w = [b_ada, conf_dw_w[0], conf_dw_b, conf_ln_g, conf_ln_b, sc_conv_w[0], g_final.reshape(1, d)]
        small_m = [m_b_ada, m_conf_dw_w[0], m_conf_dw_b, m_conf_ln_g, m_conf_ln_b, m_sc_conv_w[0],
                   m_g_final.reshape(1, d)]
        small_v = [v_b_ada, v_conf_dw_w[0], v_conf_dw_b, v_conf_ln_g, v_conf_ln_b, v_sc_conv_w[0],
                   v_g_final.reshape(1, d)]
        made["upd"] = _adamw_small(small_w, made["small_g"], small_m, small_v)
        made["loss"] = red[SG_N + 2, 0]
        return [big["w_mlp2"][0], made["upd"][0]]

    (gr_in,) = _reduce_scatter(ids, [made["g_win"]], lays[0:1], names[0:1], "i",
                               (behind_pair_exchange_in, behind_chip_exchange_in, None))
    big["w_in"] = _adamw_big(w_in[0], gr_in, m_w_in[0], v_w_in[0], "adamw_w_in")
    grad_x, small_g, upd, loss = made["grad_x"], made["small_g"], made["upd"], made["loss"]
    g_wada, d_wada, nm_wada, nv_wada = made["ada"]
    ns = len(small_g)
    s_delta, s_m, s_v = upd[0:ns], upd[ns:2 * ns], upd[2 * ns:3 * ns]

    def outs(kind_big, kind_small, wada):
        sm = kind_small
        return (wada[None], sm[0], kind_big["w_in"][None], sm[1][None], sm[2], sm[3], sm[4], sm[5][None],
                kind_big["w_out"][None], kind_big["w_mlp1"][None], kind_big["w_mlp2"][None], sm[6].reshape(d))

    grads_out = outs({"w_in": gr_in, "w_out": gr_out, "w_mlp1": gr_1, "w_mlp2": gr_2}, small_g, g_wada)
    delta_out = outs({k: v[0] for k, v in big.items()}, s_delta, d_wada)
    m_out = outs({k: v[1] for k, v in big.items()}, s_m, nm_wada)
    v_out = outs({k: v[2] for k, v in big.items()}, s_v, nv_wada)
    return (loss, grad_x.reshape(nb, t, d), *grads_out, *delta_out, *m_out, *v_out)
```

```python
import functools

import jax
import jax.numpy as jnp
from jax import lax
from jax.experimental import pallas as pl
from jax.experimental.pallas import tpu as pltpu

F32 = jnp.float32
BF16 = jnp.bfloat16
MESH = pl.DeviceIdType.MESH
HIGHEST = lax.Precision.HIGHEST

EPS = 1e-6
CONF_K = 31
SC_K = 3
HALO = 32
N_CHIPS = 4
N_DEV = 8

ADAM_LR = 0.001
ADAM_B1 = 0.9
ADAM_B2 = 0.999
ADAM_EPS = 1e-08
ADAM_WD = 0.01
ADAM_STEP = 10

TM_MM = 512
TM_MIX = 256
TM_MLP = 512
FF_CHUNK = 1024
MLP_SUB = 2
TK_WG = 2048
RB_CONV = 64
RB_WG = 32
CHIP_RELS = ((1, 0), (0, 1), (1, 1))

ANY = pl.BlockSpec(memory_space=pl.ANY)
VMEM = pl.BlockSpec(memory_space=pltpu.VMEM)
HBM = pl.BlockSpec(memory_space=pltpu.HBM)
SEM = pl.BlockSpec(memory_space=pltpu.SEMAPHORE)
EFFECT = pltpu.SideEffectType.DATAFLOW_SIDE_EFFECTING


def _me():
    return lax.axis_index("x"), lax.axis_index("y"), lax.axis_index("c")


def _flip(v, f):
    return 1 - v if f else v


def _rows8(v):
    r, c = v.shape
    return v.reshape(r // 8, 8, c).sum(axis=0)


def _rms(x):
    r = lax.rsqrt(jnp.mean(x * x, axis=-1, keepdims=True) + EPS)
    return x * r, r


def _rms_bwd(dxn, xn, r):
    return r * (dxn - xn * jnp.mean(dxn * xn, axis=-1, keepdims=True))


def _sigmoid(x):
    return 1.0 / (1.0 + jnp.exp(-x))


def _dot(a, b):
    return jnp.dot(a, b, preferred_element_type=F32)


def _dot_nt(a, b):
    return lax.dot_general(a, b, (((1,), (1,)), ((), ())), preferred_element_type=F32)


def _dot_tn(a, b):
    return lax.dot_general(a, b, (((0,), (0,)), ((), ())), preferred_element_type=F32)


def _const(shape):
    nd = len(shape)
    return pl.BlockSpec(shape, lambda i: (0,) * nd)


def _resident(shape):
    nd = len(shape)
    return pl.BlockSpec(shape, lambda i: (0,) * nd, pipeline_mode=pl.Buffered(1))


def _rowblk(tm, c):
    return pl.BlockSpec((tm, c), lambda i: (i, 0))


def _modspec(tps, width):
    return pl.BlockSpec((None, 1, width), lambda i: (i // tps, 0, 0))


def _accspec(tps, c):
    return pl.BlockSpec((None, 8, c), lambda i: (i // tps, 0, 0))


def _acc_add(ref, first, v):
    @pl.when(first)
    def _():
        ref[...] = v

    @pl.when(jnp.logical_not(first))
    def _():
        ref[...] += v


def _all_gather8(arrs, name):
    n = len(arrs)

    def body(*refs):
        ins, outs = refs[:n], refs[n:2 * n]
        send, recv = refs[2 * n:]
        x, y, c = _me()
        dev = 4 * x + 2 * y + c
        for a in range(n):
            outs[a][dev] = ins[a][...]
        sends = []
        for r in range(1, N_DEV):
            fx, fy, fc = (r >> 2) & 1, (r >> 1) & 1, r & 1
            peer = (_flip(x, fx), _flip(y, fy), _flip(c, fc))
            for a in range(n):
                cp = pltpu.make_async_remote_copy(
                    src_ref=ins[a], dst_ref=outs[a].at[dev],
                    send_sem=send.at[r - 1, a], recv_sem=recv.at[r - 1, a],
                    device_id=peer, device_id_type=MESH)
                cp.start()
                sends.append(cp)
        for r in range(1, N_DEV):
            fx, fy, fc = (r >> 2) & 1, (r >> 1) & 1, r & 1
            pdev = 4 * _flip(x, fx) + 2 * _flip(y, fy) + _flip(c, fc)
            for a in range(n):
                pltpu.make_async_remote_copy(
                    src_ref=ins[a], dst_ref=outs[a].at[pdev],
                    send_sem=send.at[r - 1, a], recv_sem=recv.at[r - 1, a],
                    device_id=(x, y, c), device_id_type=MESH).wait_recv()
        for cp in sends:
            cp.wait_send()

    return pl.pallas_call(
        body, name=name,
        out_shape=[jax.ShapeDtypeStruct((N_DEV,) + a.shape, a.dtype) for a in arrs],
        in_specs=[VMEM] * n, out_specs=[VMEM] * n,
        scratch_shapes=[pltpu.SemaphoreType.DMA((N_DEV - 1, n)),
                        pltpu.SemaphoreType.DMA((N_DEV - 1, n))],
    )(*arrs)


def _gather_mod(mod_shard):
    nb, w = mod_shard.shape

    def body(in_ref, out_ref, send, recv):
        x, y, c = _me()
        chip = 2 * x + y
        out_ref[:, pl.ds(pl.multiple_of(chip * w, 128), w)] = in_ref[...]
        sends = []
        for r, (fx, fy) in enumerate(CHIP_RELS):
            cp = pltpu.make_async_remote_copy(
                src_ref=in_ref,
                dst_ref=out_ref.at[:, pl.ds(pl.multiple_of(chip * w, 128), w)],
                send_sem=send.at[r], recv_sem=recv.at[r],
                device_id=(_flip(x, fx), _flip(y, fy), c), device_id_type=MESH)
            cp.start()
            sends.append(cp)
        for r, (fx, fy) in enumerate(CHIP_RELS):
            pchip = 2 * _flip(x, fx) + _flip(y, fy)
            pltpu.make_async_remote_copy(
                src_ref=in_ref,
                dst_ref=out_ref.at[:, pl.ds(pl.multiple_of(pchip * w, 128), w)],
                send_sem=send.at[r], recv_sem=recv.at[r],
                device_id=(x, y, c), device_id_type=MESH).wait_recv()
        for cp in sends:
            cp.wait_send()

    return pl.pallas_call(
        body, name="gather_mod",
        out_shape=jax.ShapeDtypeStruct((nb, N_CHIPS * w), mod_shard.dtype),
        in_specs=[VMEM], out_specs=VMEM,
        scratch_shapes=[pltpu.SemaphoreType.DMA((3,)), pltpu.SemaphoreType.DMA((3,))],
    )(mod_shard)


def _wlayout(d):
    d_in = 5 * d // 2
    return (
        (d, d_in // N_CHIPS, True),
        (d // N_CHIPS, d, False),
        (d, 4 * d // N_CHIPS, True),
        (4 * d // N_CHIPS, d, False),
    )


def _full_shape(lay):
    r, c, by_col = lay
    return (r, c * N_CHIPS) if by_col else (r * N_CHIPS, c)


def _full_view(ref, lay, k, h):
    r, c, by_col = lay
    hr = r // 2
    if by_col:
        return ref.at[pl.ds(pl.multiple_of(h * hr, 16), hr), pl.ds(pl.multiple_of(k * c, 128), c)]
    return ref.at[pl.ds(pl.multiple_of(k * r + h * hr, 16), hr), :]


def _half_view(ref, lay, h):
    hr = lay[0] // 2
    return ref.at[pl.ds(pl.multiple_of(h * hr, 16), hr), :]


def _half_shape(lay):
    return (lay[0] // 2, lay[1])


def _hbm(a):
    return pltpu.with_memory_space_constraint(a, pltpu.HBM)


def _remote(src, dst, send, recv, idx, to):
    return lambda: pltpu.make_async_remote_copy(src_ref=src, dst_ref=dst, send_sem=send.at[idx], recv_sem=recv.at[idx],
                                                device_id=to, device_id_type=MESH)


def _copy_start(name, bufs, n_sems, plan, after=()):
    nb, na = len(bufs), len(after)

    def body(*refs):
        sends, _ = plan(refs[:nb], refs[nb + na], refs[nb + na + 1])
        for mk in sends:
            mk().start()
        refs[-1][...] = jnp.zeros((8, 128), F32)

    outs = pl.pallas_call(
        body, name=name,
        out_shape=(pltpu.SemaphoreType.DMA((n_sems,)), pltpu.SemaphoreType.DMA((n_sems,)))
        + tuple(pltpu.HBM(b.shape, b.dtype) for b in bufs) + (jax.ShapeDtypeStruct((8, 128), F32),),
        in_specs=(HBM,) * nb + (ANY,) * na, out_specs=(SEM, SEM) + (HBM,) * nb + (VMEM,),
        input_output_aliases={i: 2 + i for i in range(nb)},
        compiler_params=pltpu.CompilerParams(has_side_effects=EFFECT),
    )(*[_hbm(b) for b in bufs], *after)
    return (outs[0], outs[1]), list(outs[2:2 + nb]), outs[-1]


def _copy_wait(name, bufs, sems, plan, after):
    nb, na = len(bufs), len(after)

    def body(*refs):
        sends, recvs = plan(refs[:nb], refs[nb], refs[nb + 1])
        for mk in sends:
            mk().wait_send()
        for mk in recvs:
            mk().wait_recv()

    outs = pl.pallas_call(
        body, name=name,
        out_shape=tuple(pltpu.HBM(b.shape, b.dtype) for b in bufs),
        in_specs=(HBM,) * nb + (SEM, SEM) + (ANY,) * na, out_specs=(HBM,) * nb,
        input_output_aliases={i: i for i in range(nb)},
        compiler_params=pltpu.CompilerParams(has_side_effects=EFFECT),
    )(*bufs, *sems, *after)
    return list(outs)


def _copy_blocking(name, bufs, n_sems, plan, after=()):
    nb, na = len(bufs), len(after)

    def body(*refs):
        sends, recvs = plan(refs[:nb], refs[2 * nb + na], refs[2 * nb + na + 1])
        started = [mk() for mk in sends]
        for cp in started:
            cp.start()
        for mk in recvs:
            mk().wait_recv()
        for cp in started:
            cp.wait_send()

    return list(pl.pallas_call(
        body, name=name,
        out_shape=tuple(jax.ShapeDtypeStruct(b.shape, b.dtype) for b in bufs),
        in_specs=(ANY,) * (nb + na), out_specs=(ANY,) * nb,
        input_output_aliases={i: i for i in range(nb)},
        scratch_shapes=[pltpu.SemaphoreType.DMA((n_sems,)), pltpu.SemaphoreType.DMA((n_sems,))],
    )(*bufs, *after))


def _exchange(name, bufs, n_sems, plan, between):
    if between is None:
        return _copy_blocking(name, bufs, n_sems, plan)
    sems, bufs, tok = _copy_start(name + "_start", bufs, n_sems, plan)
    return _copy_wait(name + "_wait", bufs, sems, plan, between(tok))


def _gather_direct_plan(lays):
    def plan(full, send, recv):
        x, y, c = _me()
        chip = 2 * x + y
        sends, recvs = [], []
        for r, (fx, fy) in enumerate(CHIP_RELS):
            px, py = _flip(x, fx), _flip(y, fy)
            for i, lay in enumerate(lays):
                for q in range(2):
                    oc = _flip(c, q)
                    mine = _full_view(full[i], lay, chip, c)
                    idx = (r * len(lays) + i) * 2 + q
                    sends.append(_remote(mine, mine, send, recv, idx, (px, py, oc)))
                    theirs = _full_view(full[i], lay, 2 * px + py, oc)
                    recvs.append(_remote(theirs, theirs, send, recv, idx, (x, y, c)))
        return sends, recvs
    return plan


def _gather_chip_plan(lays):
    def plan(full, send, recv):
        x, y, c = _me()
        chip = 2 * x + y
        sends, recvs = [], []
        for r, (fx, fy) in enumerate(CHIP_RELS):
            px, py = _flip(x, fx), _flip(y, fy)
            for i, lay in enumerate(lays):
                mine = _full_view(full[i], lay, chip, c)
                sends.append(_remote(mine, mine, send, recv, r * len(lays) + i, (px, py, c)))
                theirs = _full_view(full[i], lay, 2 * px + py, c)
                recvs.append(_remote(theirs, theirs, send, recv, r * len(lays) + i, (x, y, c)))
        return sends, recvs
    return plan


def _gather_pass_plan(lays):
    def plan(full, send, recv):
        x, y, c = _me()
        sends, recvs = [], []
        for r, (fx, fy) in enumerate(CHIP_RELS):
            pchip = 2 * _flip(x, fx) + _flip(y, fy)
            for i, lay in enumerate(lays):
                landed = _full_view(full[i], lay, pchip, c)
                sends.append(_remote(landed, landed, send, recv, r * len(lays) + i, (x, y, 1 - c)))
                other = _full_view(full[i], lay, pchip, 1 - c)
                recvs.append(_remote(other, other, send, recv, r * len(lays) + i, (x, y, c)))
        return sends, recvs
    return plan


def _pair_exchange_plan(lays):
    nw = len(lays)

    def plan(bufs, send, recv):
        x, y, c = _me()
        sends, recvs = [], []
        for i, lay in enumerate(lays):
            for k in range(N_CHIPS):
                sends.append(_remote(_full_view(bufs[i], lay, k, 1 - c), bufs[nw + i].at[k],
                                     send, recv, i * N_CHIPS + k, (x, y, 1 - c)))
                recvs.append(_remote(_full_view(bufs[i], lay, k, c), bufs[nw + i].at[k],
                                     send, recv, i * N_CHIPS + k, (x, y, c)))
        return sends, recvs
    return plan


def _chip_exchange_plan(nw):
    def plan(bufs, send, recv):
        x, y, c = _me()
        sends, recvs = [], []
        for r, (fx, fy) in enumerate(CHIP_RELS):
            px, py = _flip(x, fx), _flip(y, fy)
            for i in range(nw):
                sends.append(_remote(bufs[i].at[2 * px + py], bufs[nw + i].at[r], send, recv, r * nw + i, (px, py, c)))
                recvs.append(_remote(bufs[i].at[2 * px + py], bufs[nw + i].at[r], send, recv, r * nw + i, (x, y, c)))
        return sends, recvs
    return plan


def _pair_share_plan(lays):
    def plan(bufs, send, recv):
        x, y, c = _me()
        sends, recvs = [], []
        for i, lay in enumerate(lays):
            mine = _half_view(bufs[i], lay, c)
            sends.append(_remote(mine, mine, send, recv, i, (x, y, 1 - c)))
            other = _half_view(bufs[i], lay, 1 - c)
            recvs.append(_remote(other, other, send, recv, i, (x, y, c)))
        return sends, recvs
    return plan


def _pair_sum(ids, g, got, lay, name):
    r, c, by_col = lay
    hr = r // 2
    tr = min(hr, 256)
    nt = hr // tr

    def body(ids_ref, g_ref, got_ref, s32_ref, s16_ref):
        s = g_ref[...] + got_ref[...]
        s32_ref[...] = s
        s16_ref[...] = s.astype(BF16)

    if by_col:
        gspec = pl.BlockSpec((tr, c), lambda k, t, ids: (ids[1] * nt + t, k))
    else:
        gspec = pl.BlockSpec((tr, c), lambda k, t, ids: ((2 * k + ids[1]) * nt + t, 0))
    hspec = pl.BlockSpec((None, tr, c), lambda k, t, ids: (k, t, 0))
    return pl.pallas_call(
        body, name=name,
        grid_spec=pltpu.PrefetchScalarGridSpec(
            num_scalar_prefetch=1, grid=(N_CHIPS, nt),
            in_specs=[gspec, hspec], out_specs=[hspec, hspec]),
        out_shape=[jax.ShapeDtypeStruct((N_CHIPS, hr, c), F32),
                   jax.ShapeDtypeStruct((N_CHIPS, hr, c), BF16)],
    )(ids, g, got)


def _chip_sum(ids, s32, got, lay, name):
    hr, c = _half_shape(lay)
    tr = min(hr, 256)
    nt = hr // tr

    def body(ids_ref, s_ref, got_ref, out_ref):
        t = s_ref[...]
        for r in range(3):
            t = t + got_ref[r].astype(F32)
        out_ref[...] = t

    return pl.pallas_call(
        body, name=name,
        grid_spec=pltpu.PrefetchScalarGridSpec(
            num_scalar_prefetch=1, grid=(nt,),
            in_specs=[pl.BlockSpec((None, tr, c), lambda t, ids: (ids[0], t, 0)),
                      pl.BlockSpec((3, tr, c), lambda t, ids: (0, t, 0))],
            out_specs=pl.BlockSpec((tr, c), lambda t, ids: (ids[1] * nt + t, 0))),
        out_shape=jax.ShapeDtypeStruct((2 * hr, c), F32),
    )(ids, s32, got)


def _reduce_scatter(ids, grads, lays, names, tag, between):
    nw = len(lays)
    got1 = [lax.empty((N_CHIPS,) + _half_shape(l), F32) for l in lays]
    bufs = _exchange("pair_exchange_" + tag, list(grads) + got1, nw * N_CHIPS, _pair_exchange_plan(lays), between[0])
    sums = [_pair_sum(ids, bufs[i], bufs[nw + i], lays[i], "pair_sum_" + names[i]) for i in range(nw)]
    got2 = [lax.empty((3,) + _half_shape(l), BF16) for l in lays]
    bufs = _exchange("chip_exchange_" + tag, [s[1] for s in sums] + got2, 3 * nw, _chip_exchange_plan(nw), between[1])
    mine = [_chip_sum(ids, sums[i][0], bufs[nw + i], lays[i], "chip_sum_" + names[i]) for i in range(nw)]
    return _exchange("pair_share_" + tag, mine, nw, _pair_share_plan(lays), between[2])


def _cast_place(ids, w, lay, name):
    r, c, by_col = lay

    def body(ids_ref, w_ref, out_ref):
        out_ref[...] = w_ref[...].astype(BF16)

    omap = (lambda i, ids: (0, ids[0])) if by_col else (lambda i, ids: (ids[0], 0))
    return pl.pallas_call(
        body, name=name,
        grid_spec=pltpu.PrefetchScalarGridSpec(
            num_scalar_prefetch=1, grid=(1,),
            in_specs=[pl.BlockSpec((r, c), lambda i, ids: (0, 0))],
            out_specs=pl.BlockSpec((r, c), omap)),
        out_shape=jax.ShapeDtypeStruct(_full_shape(lay), BF16),
    )(ids, w)


def _ada_mod(c_all, w_ada, b_ada):
    def body(c_ref, w_ref, b_ref, act_ref, mod_ref):
        cv = c_ref[...]
        act = cv * _sigmoid(cv)
        act_ref[...] = act
        mod_ref[...] = jnp.dot(act, w_ref[...], preferred_element_type=F32, precision=HIGHEST) + b_ref[...]

    nb = c_all.shape[0]
    return pl.pallas_call(
        body, name="ada_mod",
        out_shape=[jax.ShapeDtypeStruct(c_all.shape, F32),
                   jax.ShapeDtypeStruct((nb, w_ada.shape[1]), F32)],
        in_specs=[VMEM] * 3, out_specs=[VMEM] * 2,
    )(c_all, w_ada, b_ada)


def _run_units(*unit_lists):
    total = max(len(u) for u in unit_lists)
    done = [0] * len(unit_lists)
    for step in range(1, total + 1):
        for li, units in enumerate(unit_lists):
            upto = (step * len(units) + total - 1) // total
            while done[li] < upto:
                units[done[li]]()
                done[li] += 1


def _in_mix_fwd(x, mod3, w_in, wb, dwb, lng, lnb, scw, t, dep):
    n, d = x.shape
    d_in = w_in.shape[1]
    dc = d_in // 5
    tm = min(TM_MIX, t)
    tps = t // tm
    nt = n // tm
    rb = min(RB_CONV, tm)
    ncol = 256
    ng = dc // 128
    assert tps % 2 == 0 and nt % 2 == 0

    def proj_units(x_ref, mod_ref, w_ref, p_ref, h_ref):
        def head():
            xn, _ = _rms(x_ref[...])
            h_ref[...] = (xn * (1.0 + mod_ref[:, d:2 * d]) + mod_ref[:, 0:d]).astype(BF16)
        units = [head]
        for c0 in range(0, d_in, ncol):
            def chunk(c0=c0):
                p_ref[:, c0:c0 + ncol] = _dot(h_ref[...], w_ref[:, c0:c0 + ncol])
            units.append(chunk)
        return units

    def mix_units(first, p_ref, h_ref, r0, wb_ref, dwb_ref, lng_ref, lnb_ref, scw_ref,
                  proj_ref, h1_ref, mixed_ref, a1_ref, q_ref, ext_ref, e_ref, extp_ref, a1s_ref):
        rows = slice(r0, r0 + tm)
        units = []

        def glu():
            halo = ext_ref[tm:tm + HALO, :]
            ext_ref[0:HALO, :] = halo if first is False else jnp.where(first, 0.0, halo)
            ext_ref[HALO:HALO + tm, :] = p_ref[:, 0:dc] * _sigmoid(p_ref[:, dc:2 * dc])
        units.append(glu)
        for g in range(ng):
            def shift(g=g):
                for r in range(8):
                    e_ref[g, r, 0:tm + HALO, :] = ext_ref[r:r + tm + HALO, 128 * g:128 * g + 128]
            units.append(shift)
        for g in range(ng):
            lanes = slice(128 * g, 128 * g + 128)
            for i0 in range(0, tm, rb):
                def conv(g=g, lanes=lanes, i0=i0):
                    acc = jnp.zeros((rb, 128), F32)
                    for k in range(CONF_K):
                        m, r = divmod(k + HALO - CONF_K + 1, 8)
                        acc = acc + e_ref[g, r, i0 + 8 * m:i0 + 8 * m + rb, :] * wb_ref[k:k + 1, lanes]
                    a1s_ref[i0:i0 + rb, lanes] = acc + dwb_ref[:, lanes]
                units.append(conv)

        def norm():
            a1 = a1s_ref[...]
            a1_ref[rows, :] = a1
            mu = jnp.mean(a1, axis=-1, keepdims=True)
            ac = a1 - mu
            rstd = lax.rsqrt(jnp.mean(ac * ac, axis=-1, keepdims=True) + EPS)
            a2 = ac * rstd * lng_ref[...] + lnb_ref[...]
            mixed_ref[rows, 0:dc] = (a2 * _sigmoid(a2)).astype(BF16)
        units.append(norm)

        def short():
            halo = extp_ref[tm:tm + 8, :]
            extp_ref[0:8, :] = halo if first is False else jnp.where(first, 0.0, halo)
            extp_ref[8:8 + tm, :] = p_ref[:, 3 * dc:4 * dc] * p_ref[:, 4 * dc:5 * dc]
            q = jnp.zeros((tm, dc), F32)
            for k in range(SC_K):
                q = q + extp_ref[6 + k:6 + k + tm, :] * scw_ref[k:k + 1, :]
            q_ref[rows, :] = q
            mixed_ref[rows, dc:2 * dc] = (p_ref[:, 2 * dc:3 * dc] * q).astype(BF16)
        units.append(short)

        def keep():
            proj_ref[rows, :] = p_ref[...]
            h1_ref[rows, :] = h_ref[...]
        units.append(keep)
        return units

    def body(x0_ref, xa_ref, xb_ref, mod0_ref, moda_ref, modb_ref, w_ref,
             wb_ref, dwb_ref, lng_ref, lnb_ref, scw_ref, dep_ref,
             proj_ref, h1_ref, mixed_ref, a1_ref, q_ref,
             p0_ref, p1_ref, h0_ref, hh1_ref, ext_ref, e_ref, extp_ref, a1s_ref):
        j = pl.program_id(0)

        @pl.when(j == 0)
        def _():
            ext_ref[...] = jnp.zeros(ext_ref.shape, F32)
            extp_ref[...] = jnp.zeros(extp_ref.shape, F32)
            _run_units(proj_units(x0_ref, mod0_ref, w_ref, p0_ref, h0_ref))

        common = (wb_ref, dwb_ref, lng_ref, lnb_ref, scw_ref, proj_ref, h1_ref, mixed_ref, a1_ref, q_ref,
                  ext_ref, e_ref, extp_ref, a1s_ref)
        _run_units(mix_units((2 * j) % tps == 0, p0_ref, h0_ref, 0, *common),
                   proj_units(xa_ref, moda_ref, w_ref, p1_ref, hh1_ref))
        _run_units(mix_units(False, p1_ref, hh1_ref, tm, *common),
                   proj_units(xb_ref, modb_ref, w_ref, p0_ref, h0_ref))

    last = nt - 1
    xspec = lambda f: pl.BlockSpec((tm, d), lambda j: (f(j), 0))
    mspec = lambda f: pl.BlockSpec((None, 1, 6 * d), lambda j: (f(j) // tps, 0, 0))
    out2 = lambda c: pl.BlockSpec((2 * tm, c), lambda j: (j, 0))
    return pl.pallas_call(
        body, name="in_mix_fwd", grid=(nt // 2,),
        in_specs=[xspec(lambda j: 0), xspec(lambda j: 2 * j + 1), xspec(lambda j: jnp.minimum(2 * j + 2, last)),
                  mspec(lambda j: 0), mspec(lambda j: 2 * j + 1), mspec(lambda j: jnp.minimum(2 * j + 2, last)),
                  _resident((d, d_in)),
                  _const(wb.shape), _const(dwb.shape), _const(lng.shape), _const(lnb.shape), _const(scw.shape), ANY],
        out_specs=[out2(d_in), out2(d), out2(2 * dc), out2(dc), out2(dc)],
        out_shape=[jax.ShapeDtypeStruct((n, d_in), F32), jax.ShapeDtypeStruct((n, d), BF16),
                   jax.ShapeDtypeStruct((n, 2 * dc), BF16), jax.ShapeDtypeStruct((n, dc), F32),
                   jax.ShapeDtypeStruct((n, dc), F32)],
        scratch_shapes=[pltpu.VMEM((tm, d_in), F32), pltpu.VMEM((tm, d_in), F32),
                        pltpu.VMEM((tm, d), BF16), pltpu.VMEM((tm, d), BF16),
                        pltpu.VMEM((tm + HALO + 8, dc), F32), _shift_scratch(tm, dc),
                        pltpu.VMEM((tm + 8, dc), F32), pltpu.VMEM((tm, dc), F32)],
        compiler_params=pltpu.CompilerParams(dimension_semantics=("arbitrary",)),
    )(x, x, x, mod3, mod3, mod3, w_in, wb, dwb, lng, lnb, scw, dep)


def _shift_scratch(tm, dc):
    return pltpu.VMEM((dc // 128, 8, tm + HALO + 8, 128), F32)


def _out_proj(mixed, w_out, x, mod3, t, dep):
    n, d = x.shape
    tm = min(TM_MM, t)
    tps = t // tm

    def body(m_ref, w_ref, x_ref, mod_ref, dep_ref, x1_ref, y1_ref, h2_ref):
        y1 = _dot(m_ref[...], w_ref[...])
        y1_ref[...] = y1.astype(BF16)
        x1 = x_ref[...] + mod_ref[:, 2 * d:3 * d] * y1
        x1_ref[...] = x1
        xn, _ = _rms(x1)
        h2_ref[...] = (xn * (1.0 + mod_ref[:, 4 * d:5 * d]) + mod_ref[:, 3 * d:4 * d]).astype(BF16)

    return pl.pallas_call(
        body, name="out_proj", grid=(n // tm,),
        in_specs=[_rowblk(tm, d), _resident((d, d)), _rowblk(tm, d), _modspec(tps, 6 * d), ANY],
        out_specs=[_rowblk(tm, d), _rowblk(tm, d), _rowblk(tm, d)],
        out_shape=[jax.ShapeDtypeStruct((n, d), F32), jax.ShapeDtypeStruct((n, d), BF16),
                   jax.ShapeDtypeStruct((n, d), BF16)],
        compiler_params=pltpu.CompilerParams(dimension_semantics=("parallel",)),
    )(mixed, w_out, x, mod3, dep)


def _mlp_fwd(h2, x1, tgt, mod3, gfin, w1, w2, t):
    n, d = x1.shape
    dff = w1.shape[1]
    tm = min(TM_MLP, t)
    tps = t // tm
    nt = n // tm

    def body(h_ref, x1_ref, tg_ref, mod_ref, gf_ref, w1_ref, w2_ref,
             z_ref, dx2_ref, dy2_ref, dg2_ref, ggf_ref, loss_ref):
        i = pl.program_id(0)
        g2 = mod_ref[:, 5 * d:6 * d]
        gf = gf_ref[...]
        sub = tm // MLP_SUB
        sums = None
        for part in range(MLP_SUB):
            rs = slice(part * sub, (part + 1) * sub)
            hv = h_ref[rs, :]
            y2 = jnp.zeros((sub, d), F32)
            for j in range(dff // FF_CHUNK):
                cols = slice(j * FF_CHUNK, (j + 1) * FF_CHUNK)
                z = _dot(hv, w1_ref[:, cols])
                z_ref[rs, cols] = z.astype(BF16)
                zr = jnp.maximum(z, 0.0)
                y2 = y2 + _dot((zr * zr).astype(BF16), w2_ref[cols, :])
            x2n, r3 = _rms(x1_ref[rs, :] + g2 * y2)
            diff = x2n * gf - tg_ref[rs, :]
            dout = diff * (1.0 / d)
            dx2 = _rms_bwd(dout * gf, x2n, r3)
            dx2_ref[rs, :] = dx2
            dy2_ref[rs, :] = (g2 * dx2).astype(BF16)
            p = (_rows8(dx2 * y2), _rows8(dout * x2n), _rows8(diff * diff))
            sums = p if sums is None else tuple(a + b for a, b in zip(sums, p))
        _acc_add(dg2_ref, i % tps == 0, sums[0])
        _acc_add(ggf_ref, i == 0, sums[1])
        _acc_add(loss_ref, i == 0, sums[2])

    return pl.pallas_call(
        body, name="mlp_fwd", grid=(nt,),
        in_specs=[_rowblk(tm, d), _rowblk(tm, d), _rowblk(tm, d), _modspec(tps, 6 * d), _const((1, d)),
                  _resident((d, dff)), _resident((dff, d))],
        out_specs=[_rowblk(tm, dff), _rowblk(tm, d), _rowblk(tm, d), _accspec(tps, d),
                   _const((8, d)), _const((8, d))],
        out_shape=[jax.ShapeDtypeStruct((n, dff), BF16), jax.ShapeDtypeStruct((n, d), F32),
                   jax.ShapeDtypeStruct((n, d), BF16), jax.ShapeDtypeStruct((n // t, 8, d), F32),
                   jax.ShapeDtypeStruct((8, d), F32), jax.ShapeDtypeStruct((8, d), F32)],
        compiler_params=pltpu.CompilerParams(dimension_semantics=("arbitrary",)),
    )(h2, x1, tgt, mod3, gfin, w1, w2)


def _mlp_bwd(dy2, z, x1, dx2, y1, mod3, w1, w2, t):
    n, d = x1.shape
    dff = w1.shape[1]
    tm = min(TM_MLP, t)
    tps = t // tm

    def body(dy2_ref, z_ref, x1_ref, dx2_ref, y1_ref, mod_ref, w1_ref, w2_ref,
             dz_ref, dx1_ref, dy1_ref, dsh_ref, dsc_ref, dg1_ref):
        first = pl.program_id(0) % tps == 0
        sub = tm // MLP_SUB
        sums = None
        for part in range(MLP_SUB):
            rs = slice(part * sub, (part + 1) * sub)
            dy2 = dy2_ref[rs, :]
            dh2 = jnp.zeros((sub, d), F32)
            for j in range(dff // FF_CHUNK):
                cols = slice(j * FF_CHUNK, (j + 1) * FF_CHUNK)
                du = _dot_nt(dy2, w2_ref[cols, :])
                dz = (du * (2.0 * jnp.maximum(z_ref[rs, cols].astype(F32), 0.0))).astype(BF16)
                dz_ref[rs, cols] = dz
                dh2 = dh2 + _dot_nt(dz, w1_ref[:, cols])
            x1n, r2 = _rms(x1_ref[rs, :])
            dx1 = dx2_ref[rs, :] + _rms_bwd(dh2 * (1.0 + mod_ref[:, 4 * d:5 * d]), x1n, r2)
            dx1_ref[rs, :] = dx1
            dy1_ref[rs, :] = (mod_ref[:, 2 * d:3 * d] * dx1).astype(BF16)
            p = (_rows8(dh2), _rows8(dh2 * x1n), _rows8(dx1 * y1_ref[rs, :].astype(F32)))
            sums = p if sums is None else tuple(a + b for a, b in zip(sums, p))
        _acc_add(dsh_ref, first, sums[0])
        _acc_add(dsc_ref, first, sums[1])
        _acc_add(dg1_ref, first, sums[2])

    acc = jax.ShapeDtypeStruct((n // t, 8, d), F32)
    return pl.pallas_call(
        body, name="mlp_bwd", grid=(n // tm,),
        in_specs=[_rowblk(tm, d), _rowblk(tm, dff), _rowblk(tm, d), _rowblk(tm, d), _rowblk(tm, d),
                  _modspec(tps, 6 * d), _resident((d, dff)), _resident((dff, d))],
        out_specs=[_rowblk(tm, dff), _rowblk(tm, d), _rowblk(tm, d),
                   _accspec(tps, d), _accspec(tps, d), _accspec(tps, d)],
        out_shape=[jax.ShapeDtypeStruct((n, dff), BF16), jax.ShapeDtypeStruct((n, d), F32),
                   jax.ShapeDtypeStruct((n, d), BF16), acc, acc, acc],
        compiler_params=pltpu.CompilerParams(dimension_semantics=("arbitrary",)),
    )(dy2, z, x1, dx2, y1, mod3, w1, w2)


def _wgrad(a, b, name, relu2=False, bn=None, deps=()):
    n, ka = a.shape
    nb = b.shape[1]
    tk = min(TK_WG, n)
    bm = min(ka, 1024)
    if bn is None:
        bn = nb if nb <= 2048 else nb // 2

    def body(a_ref, b_ref, *rest):
        out_ref = rest[-1]
        av = a_ref[...]
        if relu2:
            ar = jnp.maximum(av, 0.0)
            av = ar * ar
        p = _dot_tn(av, b_ref[...])
        _acc_add(out_ref, pl.program_id(2) == 0, p)

    return pl.pallas_call(
        body, name=name, grid=(ka // bm, nb // bn, n // tk),
        in_specs=[pl.BlockSpec((tk, bm), lambda i, j, k: (k, i)),
                  pl.BlockSpec((tk, bn), lambda i, j, k: (k, j))] + [ANY] * len(deps),
        out_specs=pl.BlockSpec((bm, bn), lambda i, j, k: (i, j)),
        out_shape=jax.ShapeDtypeStruct((ka, nb), F32),
        compiler_params=pltpu.CompilerParams(dimension_semantics=("parallel", "parallel", "arbitrary")),
    )(a, b, *deps)


SG_DWW = 0
SG_DWB = CONF_K
SG_LNG = CONF_K + 1
SG_LNB = CONF_K + 2
SG_SCW = CONF_K + 3
SG_N = CONF_K + 3 + SC_K


def _mix_in_bwd(dy1, w_out, a1, q, proj, wb, lng, lnb, scw, w_in, x, dx1, mod3, t, dep):
    n, d_in = proj.shape
    d = x.shape[1]
    dc = d_in // 5
    tm = min(TM_MIX, t)
    tps = t // tm
    nt = n // tm
    rw = min(RB_WG, tm)
    ng = dc // 128
    ncol = 256

    def ln_bwd(a1v, da3, lng_v, lnb_v):
        mu = jnp.mean(a1v, axis=-1, keepdims=True)
        ac = a1v - mu
        rstd = lax.rsqrt(jnp.mean(ac * ac, axis=-1, keepdims=True) + EPS)
        ah = ac * rstd
        a2 = ah * lng_v + lnb_v
        s2 = _sigmoid(a2)
        da2 = da3 * (s2 * (1.0 + a2 * (1.0 - s2)))
        dah = da2 * lng_v
        da1 = rstd * (dah - jnp.mean(dah, axis=-1, keepdims=True)
                      - ah * jnp.mean(dah * ah, axis=-1, keepdims=True))
        return da1, da2, ah

    def dmixed_units(dy_ref, wo_ref, dm_ref):
        units = []
        for c0 in range(0, 2 * dc, ncol):
            def chunk(c0=c0):
                dm_ref[:, c0:c0 + ncol] = _dot_nt(dy_ref[...], wo_ref[c0:c0 + ncol, :])
            units.append(chunk)
        return units

    def mix_units(k, dm_ref, a1_ref, q_ref, p_ref, wb_ref, lng_ref, lnb_ref, scw_ref,
                  dproj_ref, sg_ref, extd_ref, ed_ref, a0_ref, da0_ref, extq_ref, cda_ref, cdq_ref):
        keep_next = jnp.where(k % tps == tps - 1, 0.0, 1.0)
        units = []

        def head():
            lng_v, lnb_v = lng_ref[...], lnb_ref[...]
            da1, da2, ah = ln_bwd(a1_ref[...], dm_ref[:, 0:dc], lng_v, lnb_v)
            sg_ref[8 * SG_LNG:8 * SG_LNG + 8, :] += _rows8(da2 * ah)
            sg_ref[8 * SG_LNB:8 * SG_LNB + 8, :] += _rows8(da2)
            sg_ref[8 * SG_DWB:8 * SG_DWB + 8, :] += _rows8(da1)
            a0_ref[...] = p_ref[:, 0:dc] * _sigmoid(p_ref[:, dc:2 * dc])
            extd_ref[0:tm, :] = da1
            extd_ref[tm:tm + HALO, :] = cda_ref[...] * keep_next
            extd_ref[tm + HALO:tm + HALO + 8, :] = jnp.zeros((8, dc), F32)
            cda_ref[...] = da1[0:HALO, :]
            ds = dm_ref[:, dc:2 * dc]
            dproj_ref[:, 2 * dc:3 * dc] = (ds * q_ref[...]).astype(BF16)
            dq = ds * p_ref[:, 2 * dc:3 * dc]
            extq_ref[0:tm, :] = dq
            extq_ref[tm:tm + 8, :] = cdq_ref[...] * keep_next
            cdq_ref[...] = dq[0:8, :]
        units.append(head)
        for g in range(ng):
            def shift(g=g):
                for r in range(8):
                    ed_ref[g, r, 0:tm + HALO, :] = extd_ref[r:r + tm + HALO, 128 * g:128 * g + 128]
            units.append(shift)
        for g in range(ng):
            lanes = slice(128 * g, 128 * g + 128)
            accs = [None] * CONF_K
            for i0 in range(0, tm, rw):
                def conv(g=g, lanes=lanes, i0=i0, accs=accs):
                    a0v = a0_ref[i0:i0 + rw, lanes]
                    acc = jnp.zeros((rw, 128), F32)
                    for s in range(CONF_K):
                        m, r = divmod(s, 8)
                        e = ed_ref[g, r, i0 + 8 * m:i0 + 8 * m + rw, :]
                        acc = acc + e * wb_ref[CONF_K - 1 - s:CONF_K - s, lanes]
                        part = _rows8(e * a0v)
                        accs[s] = part if accs[s] is None else accs[s] + part
                    da0_ref[i0:i0 + rw, lanes] = acc
                units.append(conv)

            def flush(lanes=lanes, accs=accs):
                for s in range(CONF_K):
                    kk = CONF_K - 1 - s
                    sg_ref[8 * (SG_DWW + kk):8 * (SG_DWW + kk) + 8, lanes] += accs[s]
            units.append(flush)

        def emit():
            da0 = da0_ref[...]
            sig = _sigmoid(p_ref[:, dc:2 * dc])
            dproj_ref[:, 0:dc] = (da0 * sig).astype(BF16)
            dproj_ref[:, dc:2 * dc] = (da0 * a0_ref[...] * (1.0 - sig)).astype(BF16)
        units.append(emit)

        def short():
            scc, sch = p_ref[:, 3 * dc:4 * dc], p_ref[:, 4 * dc:5 * dc]
            pv = scc * sch
            dp = jnp.zeros((tm, dc), F32)
            for kk in range(SC_K):
                dqs = extq_ref[SC_K - 1 - kk:SC_K - 1 - kk + tm, :]
                dp = dp + dqs * scw_ref[kk:kk + 1, :]
                sg_ref[8 * (SG_SCW + kk):8 * (SG_SCW + kk) + 8, :] += _rows8(pv * dqs)
            dproj_ref[:, 3 * dc:4 * dc] = (dp * sch).astype(BF16)
            dproj_ref[:, 4 * dc:5 * dc] = (dp * scc).astype(BF16)
        units.append(short)
        return units

    def proj_units(first, dp_ref, w_ref, x_ref, dx1_ref, mod_ref, dh_ref, gx_ref, dsh_ref, dsc_ref):
        units = []
        for c0 in range(0, d, ncol):
            def chunk(c0=c0):
                dh_ref[:, c0:c0 + ncol] = _dot_nt(dp_ref[...], w_ref[c0:c0 + ncol, :])
            units.append(chunk)

        def tail():
            dh1 = dh_ref[...]
            xn, r1 = _rms(x_ref[...])
            v1, v2 = _rows8(dh1), _rows8(dh1 * xn)
            dsh_ref[...] = jnp.where(first, v1, dsh_ref[...] + v1)
            dsc_ref[...] = jnp.where(first, v2, dsc_ref[...] + v2)
            gx_ref[...] = dx1_ref[...] + _rms_bwd(dh1 * (1.0 + mod_ref[:, d:2 * d]), xn, r1)
        units.append(tail)
        return units

    def body(dyl_ref, dy_ref, wo_ref, a1_ref, q_ref, p_ref, wb_ref, lng_ref, lnb_ref, scw_ref,
             w_ref, x_ref, dx1_ref, mod_ref, dep_ref,
             dproj_ref, sg_ref, gx_ref, dsh_ref, dsc_ref,
             extd_ref, ed_ref, a0_ref, da0_ref, extq_ref, dp_ref, dh_ref, dm_ref, cda_ref, cdq_ref):
        k = pl.program_id(0)

        @pl.when(k == 0)
        def _():
            sg_ref[...] = jnp.zeros(sg_ref.shape, F32)
            dp_ref[...] = jnp.zeros(dp_ref.shape, BF16)
            cda_ref[...] = jnp.zeros(cda_ref.shape, F32)
            cdq_ref[...] = jnp.zeros(cdq_ref.shape, F32)
            _run_units(dmixed_units(dyl_ref, wo_ref, dm_ref))

        first = jnp.logical_or(k <= 1, (nt - k) % tps == tps - 1)
        after = lambda: proj_units(first, dp_ref, w_ref, x_ref, dx1_ref, mod_ref, dh_ref, gx_ref, dsh_ref, dsc_ref)

        @pl.when(k < nt)
        def _():
            _run_units(mix_units(nt - 1 - k, dm_ref, a1_ref, q_ref, p_ref, wb_ref, lng_ref, lnb_ref, scw_ref,
                                 dproj_ref, sg_ref, extd_ref, ed_ref, a0_ref, da0_ref, extq_ref, cda_ref, cdq_ref),
                       after(), dmixed_units(dy_ref, wo_ref, dm_ref))
            dp_ref[...] = dproj_ref[...]

        @pl.when(k == nt)
        def _():
            _run_units(after())

    cur = lambda k: jnp.maximum(nt - 1 - k, 0)
    prev = lambda k: jnp.minimum(nt - k, nt - 1)
    acc = jax.ShapeDtypeStruct((n // t, 8, d), F32)
    return pl.pallas_call(
        body, name="mix_in_bwd", grid=(nt + 1,),
        in_specs=[pl.BlockSpec((tm, d), lambda k: (nt - 1, 0)),
                  pl.BlockSpec((tm, d), lambda k: (jnp.maximum(nt - 2 - k, 0), 0)), _resident(w_out.shape),
                  pl.BlockSpec((tm, dc), lambda k: (cur(k), 0)),
                  pl.BlockSpec((tm, dc), lambda k: (cur(k), 0)),
                  pl.BlockSpec((tm, d_in), lambda k: (cur(k), 0)),
                  _const(wb.shape), _const(lng.shape), _const(lnb.shape), _const(scw.shape),
                  _resident((d, d_in)),
                  pl.BlockSpec((tm, d), lambda k: (prev(k), 0)), pl.BlockSpec((tm, d), lambda k: (prev(k), 0)),
                  pl.BlockSpec((None, 1, 6 * d), lambda k: (prev(k) // tps, 0, 0)), ANY],
        out_specs=[pl.BlockSpec((tm, d_in), lambda k: (cur(k), 0)), _const((8 * SG_N, dc)),
                   pl.BlockSpec((tm, d), lambda k: (prev(k), 0)),
                   pl.BlockSpec((None, 8, d), lambda k: (prev(k) // tps, 0, 0)),
                   pl.BlockSpec((None, 8, d), lambda k: (prev(k) // tps, 0, 0))],
        out_shape=[jax.ShapeDtypeStruct((n, d_in), BF16), jax.ShapeDtypeStruct((8 * SG_N, dc), F32),
                   jax.ShapeDtypeStruct((n, d), F32), acc, acc],
        scratch_shapes=[pltpu.VMEM((tm + HALO + 8, dc), F32), _shift_scratch(tm, dc),
                        pltpu.VMEM((tm, dc), F32), pltpu.VMEM((tm, dc), F32),
                        pltpu.VMEM((tm + 8, dc), F32),
                        pltpu.VMEM((tm, d_in), BF16), pltpu.VMEM((tm, d), F32), pltpu.VMEM((tm, 2 * dc), F32),
                        pltpu.VMEM((HALO, dc), F32), pltpu.VMEM((8, dc), F32)],
        compiler_params=pltpu.CompilerParams(dimension_semantics=("arbitrary",)),
    )(dy1, dy1, w_out, a1, q, proj, wb, lng, lnb, scw, w_in, x, dx1, mod3, dep)


SMALL_ROWS = 40


def _pack_small(sg, ggf, loss, accs, d, dep):
    dc = d // 2
    nb = accs[0].shape[0]

    def body(sg_ref, ggf_ref, loss_ref, dsh1, dsc1, dg1, dsh2, dsc2, dg2, dep_ref, pack_ref, dmod_ref):
        pack_ref[...] = jnp.zeros(pack_ref.shape, F32)
        for k in range(SG_N):
            pack_ref[k:k + 1, :] = jnp.sum(sg_ref[8 * k:8 * k + 8, :], axis=0, keepdims=True)
        gf = jnp.sum(ggf_ref[...], axis=0, keepdims=True)
        pack_ref[SG_N:SG_N + 1, :] = gf[:, 0:dc]
        pack_ref[SG_N + 1:SG_N + 2, :] = gf[:, dc:d]
        tot = jnp.sum(jnp.sum(loss_ref[...], axis=0, keepdims=True), axis=1, keepdims=True) * (0.5 / d)
        pack_ref[SG_N + 2:SG_N + 3, :] = jnp.broadcast_to(tot, (1, dc))
        dmod_ref[...] = jnp.zeros(dmod_ref.shape, F32)
        for j, ref in enumerate((dsh1, dsc1, dg1, dsh2, dsc2, dg2)):
            for b in range(nb):
                dmod_ref[b:b + 1, j * d:(j + 1) * d] = jnp.sum(ref[b], axis=0, keepdims=True)

    return pl.pallas_call(
        body, name="pack_small",
        out_shape=[jax.ShapeDtypeStruct((SMALL_ROWS, dc), F32), jax.ShapeDtypeStruct((8, 6 * d), F32)],
        in_specs=[VMEM] * 9 + [ANY], out_specs=[VMEM] * 2,
    )(sg, ggf, loss, *accs, dep)


def _small_reduce(pack_all, dmod_all, nb, dep):
    def body(pk_ref, dm_ref, dep_ref, red_ref, dmod_ref, gb_ref):
        tot = pk_ref[0]
        for dev in range(1, N_DEV):
            tot = tot + pk_ref[dev]
        red_ref[...] = tot
        gb = jnp.zeros((1, dm_ref.shape[2]), F32)
        for dev in range(N_DEV):
            for b in range(nb):
                row = dm_ref[dev, b:b + 1, :]
                dmod_ref[dev * nb + b:dev * nb + b + 1, :] = row
                gb = gb + row
        gb_ref[...] = gb

    return pl.pallas_call(
        body, name="small_reduce",
        out_shape=[jax.ShapeDtypeStruct(pack_all.shape[1:], F32),
                   jax.ShapeDtypeStruct((N_DEV * nb, dmod_all.shape[2]), F32),
                   jax.ShapeDtypeStruct((1, dmod_all.shape[2]), F32)],
        in_specs=[VMEM] * 2 + [ANY], out_specs=[VMEM] * 3,
    )(pack_all, dmod_all, dep)


def _adam(w, g, m, v):
    m = ADAM_B1 * m + (1.0 - ADAM_B1) * g
    v = ADAM_B2 * v + (1.0 - ADAM_B2) * (g * g)
    m_hat = m / (1.0 - ADAM_B1 ** ADAM_STEP)
    v_hat = v / (1.0 - ADAM_B2 ** ADAM_STEP)
    delta = -ADAM_LR * (m_hat / (jnp.sqrt(v_hat) + ADAM_EPS) + ADAM_WD * w)
    return delta, m, v


def _adamw_big(w, g, m, v, name):
    r, c = w.shape
    tr = min(r, 256)

    def body(w_ref, g_ref, m_ref, v_ref, d_ref, nm_ref, nv_ref):
        d_ref[...], nm_ref[...], nv_ref[...] = _adam(w_ref[...], g_ref[...], m_ref[...], v_ref[...])

    s = jax.ShapeDtypeStruct((r, c), F32)
    return pl.pallas_call(
        body, name=name, grid=(r // tr,),
        in_specs=[_rowblk(tr, c)] * 4, out_specs=[_rowblk(tr, c)] * 3, out_shape=[s, s, s],
        compiler_params=pltpu.CompilerParams(dimension_semantics=("parallel",)),
    )(w, g, m, v)


def _adamw_ada(act_t, dmod_cols, w, m, v):
    r, c = w.shape
    tr = min(r, 256)
    nb = act_t.shape[1]

    def body(a_ref, dm_ref, w_ref, m_ref, v_ref, g_ref, d_ref, nm_ref, nv_ref):
        g = jnp.dot(a_ref[...], dm_ref[...], preferred_element_type=F32, precision=HIGHEST)
        g_ref[...] = g
        d_ref[...], nm_ref[...], nv_ref[...] = _adam(w_ref[...], g, m_ref[...], v_ref[...])

    s = jax.ShapeDtypeStruct((r, c), F32)
    return pl.pallas_call(
        body, name="adamw_w_ada", grid=(r // tr,),
        in_specs=[_rowblk(tr, nb), _const((nb, c))] + [_rowblk(tr, c)] * 3,
        out_specs=[_rowblk(tr, c)] * 4, out_shape=[s, s, s, s],
        compiler_params=pltpu.CompilerParams(dimension_semantics=("parallel",)),
    )(act_t, dmod_cols, w, m, v)


def _adamw_small(ws, gs, ms, vs):
    n = len(ws)

    def body(*refs):
        for i in range(n):
            w, g, m, v = (refs[j * n + i][...] for j in range(4))
            dl, nm, nv = _adam(w, g, m, v)
            refs[4 * n + i][...] = dl
            refs[5 * n + i][...] = nm
            refs[6 * n + i][...] = nv

    shapes = [jax.ShapeDtypeStruct(w.shape, F32) for w in ws]
    return pl.pallas_call(
        body, name="adamw_small", out_shape=shapes * 3,
        in_specs=[VMEM] * (4 * n), out_specs=[VMEM] * (3 * n),
    )(*ws, *gs, *ms, *vs)


def kernel(x, c, w_ada, b_ada, w_in, conf_dw_w, conf_dw_b, conf_ln_g, conf_ln_b, sc_conv_w, w_out, w_mlp1, w_mlp2, g_final, loss_target, m_w_ada, m_b_ada, m_w_in, m_conf_dw_w, m_conf_dw_b, m_conf_ln_g, m_conf_ln_b, m_sc_conv_w, m_w_out, m_w_mlp1, m_w_mlp2, m_g_final, v_w_ada, v_b_ada, v_w_in, v_conf_dw_w, v_conf_dw_b, v_conf_ln_g, v_conf_ln_b, v_sc_conv_w, v_w_out, v_w_mlp1, v_w_mlp2, v_g_final):
    nb, t, d = x.shape
    n = nb * t
    dc = d // 2
    ada_w = w_ada.shape[2]
    ax, ay, ac = _me()
    chip = 2 * ax + ay
    dev = 2 * chip + ac
    ids = jnp.stack([chip, ac]).astype(jnp.int32)

    lays = _wlayout(d)
    names = ("in", "out", "mlp1", "mlp2")
    fulls = [_cast_place(ids, w[0], lays[i], "cast_" + names[i]) for i, w in enumerate((w_in, w_out, w_mlp1, w_mlp2))]

    c_pad = jnp.zeros((8, d), F32).at[0:nb].set(c)
    cw_pad = jnp.zeros((SMALL_ROWS, dc // N_CHIPS), F32)
    cw_pad = cw_pad.at[0:CONF_K].set(conf_dw_w[0]).at[HALO:HALO + SC_K].set(sc_conv_w[0])
    c_all8, cw_all8 = _all_gather8([c_pad, cw_pad], "gather_c")
    plan_i = _gather_chip_plan(lays[0:1])
    sems_i, bufs_i, tok_i = _copy_start("gather_in_start", [fulls[0]], 3, plan_i, [c_all8])
    c_all = c_all8[:, 0:nb].reshape(N_DEV * nb, d) + tok_i[0, 0]
    cw_full = jnp.concatenate([cw_all8[2 * k] for k in range(N_CHIPS)], axis=1)
    dww, scw = cw_full[0:CONF_K], cw_full[HALO:HALO + SC_K]
    b_cols = lax.dynamic_slice(b_ada, (0, chip * ada_w), (1, ada_w))
    c_act, mod_shard = _ada_mod(c_all, w_ada[0], b_cols)
    mod_all = _gather_mod(mod_shard)
    mod3 = lax.dynamic_slice(mod_all, (dev * nb, 0), (nb, 6 * d)).reshape(nb, 1, 6 * d)

    x2 = x.reshape(n, d)
    tgt = loss_target.reshape(n, d)
    (wf_in,) = _copy_wait("gather_in_wait", bufs_i, sems_i, plan_i, [mod3])
    (wf_in,) = _copy_blocking("gather_in_pass", [wf_in], 3, _gather_pass_plan(lays[0:1]))
    plan_o, plan_b, plan_p = _gather_direct_plan(lays[1:2]), _gather_chip_plan(lays[2:4]), _gather_pass_plan(lays[2:4])
    sems_o, bufs_o, tok_o = _copy_start("gather_out_start", [fulls[1]], 6, plan_o, [wf_in, mod3])
    sems_b, bufs_b, tok_b = _copy_start("gather_mlp_start", fulls[2:4], 6, plan_b, [tok_o])
    proj, h1, mixed, a1, q = _in_mix_fwd(x2, mod3, wf_in, dww, conf_dw_b, conf_ln_g, conf_ln_b, scw, t, tok_b)
    (wf_out,) = _copy_wait("gather_out_wait", bufs_o, sems_o, plan_o, [mixed])
    bufs_b = _copy_wait("gather_mlp_wait", bufs_b, sems_b, plan_b, [mixed])
    sems_p, bufs_p, tok_p = _copy_start("gather_pass_start", bufs_b, 6, plan_p)
    x1, y1, h2 = _out_proj(mixed, wf_out, x2, mod3, t, tok_p)
    wf_1, wf_2 = _copy_wait("gather_pass_wait", bufs_p, sems_p, plan_p, [h2])
    z, dx2, dy2, dg2, ggf, loss_p = _mlp_fwd(h2, x1, tgt, mod3, g_final.reshape(1, d), wf_1, wf_2, t)

    dz, dx1, dy1, dsh2, dsc2, dg1 = _mlp_bwd(dy2, z, x1, dx2, y1, mod3, wf_1, wf_2, t)
    g_w2 = _wgrad(z, dy2, "wgrad_mlp2", relu2=True)
    g_w1 = _wgrad(h2, dz, "wgrad_mlp1")
    made = {}

    def behind_pair_exchange(tok):
        made["g_wout"] = _wgrad(mixed, dy1, "wgrad_out", deps=[tok])
        return [made["g_wout"]]

    def behind_chip_exchange(tok):
        made["dproj"], made["sg"], made["grad_x"], made["dsh1"], made["dsc1"] = _mix_in_bwd(
            dy1, wf_out, a1, q, proj, dww, conf_ln_g, conf_ln_b, scw, wf_in, x2, dx1, mod3, t, tok)
        return [made["dproj"]]

    def behind_pair_share(tok):
        made["g_win"] = _wgrad(h1, made["dproj"], "wgrad_in", deps=[tok])
        return [made["g_win"]]

    gr_1, gr_2 = _reduce_scatter(ids, [g_w1, g_w2], lays[2:4], names[2:4], "m",
                                 (behind_pair_exchange, behind_chip_exchange, behind_pair_share))

    big = {}

    def behind_pair_exchange_in(tok):
        pack, dmod8 = _pack_small(made["sg"], ggf, loss_p, (made["dsh1"], made["dsc1"], dg1, dsh2, dsc2, dg2), d, tok)
        made["gathered"] = _all_gather8([pack, dmod8], "gather_small")
        return list(made["gathered"])

    def behind_chip_exchange_in(tok):
        red, dmod_all, g_bada = _small_reduce(*made["gathered"], nb, tok)
        dmod_cols = lax.dynamic_slice(dmod_all, (0, chip * ada_w), (N_DEV * nb, ada_w))
        made["ada"] = _adamw_ada(c_act.T, dmod_cols, w_ada[0], m_w_ada[0], v_w_ada[0])
        for nm_, g_, w_, m_, v_ in (("w_mlp1", gr_1, w_mlp1, m_w_mlp1, v_w_mlp1),
                                    ("w_mlp2", gr_2, w_mlp2, m_w_mlp2, v_w_mlp2)):
            big[nm_] = _adamw_big(w_[0], g_, m_[0], v_[0], "adamw_" + nm_)
        cw = dc // N_CHIPS
        g_dww = lax.dynamic_slice(red[0:CONF_K], (0, chip * cw), (CONF_K, cw))
        g_scw = lax.dynamic_slice(red[SG_SCW:SG_SCW + SC_K], (0, chip * cw), (SC_K, cw))
        g_gfin = jnp.concatenate([red[SG_N:SG_N + 1], red[SG_N + 1:SG_N + 2]], axis=1)
        made["small_g"] = [g_bada, g_dww, red[SG_DWB:SG_DWB + 1], red[SG_LNG:SG_LNG + 1], red[SG_LNB:SG_LNB + 1],
                           g_scw, g_gfin]
        small_w = [b_ada, conf_dw_w[0], conf_dw_b, conf_ln_g, conf_ln_b, sc_conv_w[0], g_final.reshape(1, d)]
        small_m = [m_b_ada, m_conf_dw_w[0], m_conf_dw_b, m_conf_ln_g, m_conf_ln_b, m_sc_conv_w[0],
                   m_g_final.reshape(1, d)]
        small_v = [v_b_ada, v_conf_dw_w[0], v_conf_dw_b, v_conf_ln_g, v_conf_ln_b, v_sc_conv_w[0],
                   v_g_final.reshape(1, d)]
        made["upd"] = _adamw_small(small_w, made["small_g"], small_m, small_v)
        made["loss"] = red[SG_N + 2, 0]
        return [big["w_mlp2"][0], made["upd"][0]]

    gr_in, gr_out = _reduce_scatter(ids, [made["g_win"], made["g_wout"]], lays[0:2], names[0:2], "i",
                                    (behind_pair_exchange_in, behind_chip_exchange_in, None))
    big["w_in"] = _adamw_big(w_in[0], gr_in, m_w_in[0], v_w_in[0], "adamw_w_in")
    big["w_out"] = _adamw_big(w_out[0], gr_out, m_w_out[0], v_w_out[0], "adamw_w_out")
    grad_x, small_g, upd, loss = made["grad_x"], made["small_g"], made["upd"], made["loss"]
    g_wada, d_wada, nm_wada, nv_wada = made["ada"]
    ns = len(small_g)
    s_delta, s_m, s_v = upd[0:ns], upd[ns:2 * ns], upd[2 * ns:3 * ns]

    def outs(kind_big, kind_small, wada):
        sm = kind_small
        return (wada[None], sm[0], kind_big["w_in"][None], sm[1][None], sm[2], sm[3], sm[4], sm[5][None],
                kind_big["w_out"][None], kind_big["w_mlp1"][None], kind_big["w_mlp2"][None], sm[6].reshape(d))

    grads_out = outs({"w_in": gr_in, "w_out": gr_out, "w_mlp1": gr_1, "w_mlp2": gr_2}, small_g, g_wada)
    delta_out = outs({k: v[0] for k, v in big.items()}, s_delta, d_wada)
    m_out = outs({k: v[1] for k, v in big.items()}, s_m, nm_wada)
    v_out = outs({k: v[2] for k, v in big.items()}, s_v, nv_wada)
    return (loss, grad_x.reshape(nb, t, d), *grads_out, *delta_out, *m_out, *v_out)
```

```python
import functools

import jax
import jax.numpy as jnp
from jax import lax
from jax.experimental import pallas as pl
from jax.experimental.pallas import tpu as pltpu

F32 = jnp.float32
BF16 = jnp.bfloat16
MESH = pl.DeviceIdType.MESH
HIGHEST = lax.Precision.HIGHEST

EPS = 1e-6
CONF_K = 31
SC_K = 3
HALO = 32
N_CHIPS = 4
N_DEV = 8

ADAM_LR = 0.001
ADAM_B1 = 0.9
ADAM_B2 = 0.999
ADAM_EPS = 1e-08
ADAM_WD = 0.01
ADAM_STEP = 10

TM_MM = 512
TM_MIX = 256
TM_MLP = 512
FF_CHUNK = 1024
MLP_SUB = 2
TK_WG = 2048
RB_CONV = 64
RB_WG = 32
CHIP_RELS = ((1, 0), (0, 1), (1, 1))

ANY = pl.BlockSpec(memory_space=pl.ANY)
VMEM = pl.BlockSpec(memory_space=pltpu.VMEM)
HBM = pl.BlockSpec(memory_space=pltpu.HBM)
SEM = pl.BlockSpec(memory_space=pltpu.SEMAPHORE)
EFFECT = pltpu.SideEffectType.DATAFLOW_SIDE_EFFECTING


def _me():
    return lax.axis_index("x"), lax.axis_index("y"), lax.axis_index("c")


def _flip(v, f):
    return 1 - v if f else v


def _rows8(v):
    r, c = v.shape
    return v.reshape(r // 8, 8, c).sum(axis=0)


def _rms(x):
    r = lax.rsqrt(jnp.mean(x * x, axis=-1, keepdims=True) + EPS)
    return x * r, r


def _rms_bwd(dxn, xn, r):
    return r * (dxn - xn * jnp.mean(dxn * xn, axis=-1, keepdims=True))


def _sigmoid(x):
    return 1.0 / (1.0 + jnp.exp(-x))


def _dot(a, b):
    return jnp.dot(a, b, preferred_element_type=F32)


def _dot_nt(a, b):
    return lax.dot_general(a, b, (((1,), (1,)), ((), ())), preferred_element_type=F32)


def _dot_tn(a, b):
    return lax.dot_general(a, b, (((0,), (0,)), ((), ())), preferred_element_type=F32)


def _const(shape):
    nd = len(shape)
    return pl.BlockSpec(shape, lambda i: (0,) * nd)


def _resident(shape):
    nd = len(shape)
    return pl.BlockSpec(shape, lambda i: (0,) * nd, pipeline_mode=pl.Buffered(1))


def _rowblk(tm, c):
    return pl.BlockSpec((tm, c), lambda i: (i, 0))


def _modspec(tps, width):
    return pl.BlockSpec((None, 1, width), lambda i: (i // tps, 0, 0))


def _accspec(tps, c):
    return pl.BlockSpec((None, 8, c), lambda i: (i // tps, 0, 0))


def _acc_add(ref, first, v):
    @pl.when(first)
    def _():
        ref[...] = v

    @pl.when(jnp.logical_not(first))
    def _():
        ref[...] += v


def _all_gather8(arrs, name):
    n = len(arrs)

    def body(*refs):
        ins, outs = refs[:n], refs[n:2 * n]
        send, recv = refs[2 * n:]
        x, y, c = _me()
        dev = 4 * x + 2 * y + c
        for a in range(n):
            outs[a][dev] = ins[a][...]
        sends = []
        for r in range(1, N_DEV):
            fx, fy, fc = (r >> 2) & 1, (r >> 1) & 1, r & 1
            peer = (_flip(x, fx), _flip(y, fy), _flip(c, fc))
            for a in range(n):
                cp = pltpu.make_async_remote_copy(
                    src_ref=ins[a], dst_ref=outs[a].at[dev],
                    send_sem=send.at[r - 1, a], recv_sem=recv.at[r - 1, a],
                    device_id=peer, device_id_type=MESH)
                cp.start()
                sends.append(cp)
        for r in range(1, N_DEV):
            fx, fy, fc = (r >> 2) & 1, (r >> 1) & 1, r & 1
            pdev = 4 * _flip(x, fx) + 2 * _flip(y, fy) + _flip(c, fc)
            for a in range(n):
                pltpu.make_async_remote_copy(
                    src_ref=ins[a], dst_ref=outs[a].at[pdev],
                    send_sem=send.at[r - 1, a], recv_sem=recv.at[r - 1, a],
                    device_id=(x, y, c), device_id_type=MESH).wait_recv()
        for cp in sends:
            cp.wait_send()

    return pl.pallas_call(
        body, name=name,
        out_shape=[jax.ShapeDtypeStruct((N_DEV,) + a.shape, a.dtype) for a in arrs],
        in_specs=[VMEM] * n, out_specs=[VMEM] * n,
        scratch_shapes=[pltpu.SemaphoreType.DMA((N_DEV - 1, n)),
                        pltpu.SemaphoreType.DMA((N_DEV - 1, n))],
    )(*arrs)


def _gather_mod(mod_shard):
    nb, w = mod_shard.shape

    def body(in_ref, out_ref, send, recv):
        x, y, c = _me()
        chip = 2 * x + y
        out_ref[:, pl.ds(pl.multiple_of(chip * w, 128), w)] = in_ref[...]
        sends = []
        for r, (fx, fy) in enumerate(CHIP_RELS):
            cp = pltpu.make_async_remote_copy(
                src_ref=in_ref,
                dst_ref=out_ref.at[:, pl.ds(pl.multiple_of(chip * w, 128), w)],
                send_sem=send.at[r], recv_sem=recv.at[r],
                device_id=(_flip(x, fx), _flip(y, fy), c), device_id_type=MESH)
            cp.start()
            sends.append(cp)
        for r, (fx, fy) in enumerate(CHIP_RELS):
            pchip = 2 * _flip(x, fx) + _flip(y, fy)
            pltpu.make_async_remote_copy(
                src_ref=in_ref,
                dst_ref=out_ref.at[:, pl.ds(pl.multiple_of(pchip * w, 128), w)],
                send_sem=send.at[r], recv_sem=recv.at[r],
                device_id=(x, y, c), device_id_type=MESH).wait_recv()
        for cp in sends:
            cp.wait_send()

    return pl.pallas_call(
        body, name="gather_mod",
        out_shape=jax.ShapeDtypeStruct((nb, N_CHIPS * w), mod_shard.dtype),
        in_specs=[VMEM], out_specs=VMEM,
        scratch_shapes=[pltpu.SemaphoreType.DMA((3,)), pltpu.SemaphoreType.DMA((3,))],
    )(mod_shard)


def _wlayout(d):
    d_in = 5 * d // 2
    return (
        (d, d_in // N_CHIPS, True),
        (d // N_CHIPS, d, False),
        (d, 4 * d // N_CHIPS, True),
        (4 * d // N_CHIPS, d, False),
    )


def _full_shape(lay):
    r, c, by_col = lay
    return (r, c * N_CHIPS) if by_col else (r * N_CHIPS, c)


def _full_view(ref, lay, k, h):
    r, c, by_col = lay
    hr = r // 2
    if by_col:
        return ref.at[pl.ds(pl.multiple_of(h * hr, 16), hr), pl.ds(pl.multiple_of(k * c, 128), c)]
    return ref.at[pl.ds(pl.multiple_of(k * r + h * hr, 16), hr), :]


def _half_view(ref, lay, h):
    hr = lay[0] // 2
    return ref.at[pl.ds(pl.multiple_of(h * hr, 16), hr), :]


def _half_shape(lay):
    return (lay[0] // 2, lay[1])


def _hbm(a):
    return pltpu.with_memory_space_constraint(a, pltpu.HBM)


def _remote(src, dst, send, recv, idx, to):
    return lambda: pltpu.make_async_remote_copy(src_ref=src, dst_ref=dst, send_sem=send.at[idx], recv_sem=recv.at[idx],
                                                device_id=to, device_id_type=MESH)


def _copy_start(name, bufs, n_sems, plan, after=()):
    nb, na = len(bufs), len(after)

    def body(*refs):
        sends, _ = plan(refs[:nb], refs[nb + na], refs[nb + na + 1])
        for mk in sends:
            mk().start()
        refs[-1][...] = jnp.zeros((8, 128), F32)

    outs = pl.pallas_call(
        body, name=name,
        out_shape=(pltpu.SemaphoreType.DMA((n_sems,)), pltpu.SemaphoreType.DMA((n_sems,)))
        + tuple(pltpu.HBM(b.shape, b.dtype) for b in bufs) + (jax.ShapeDtypeStruct((8, 128), F32),),
        in_specs=(HBM,) * nb + (ANY,) * na, out_specs=(SEM, SEM) + (HBM,) * nb + (VMEM,),
        input_output_aliases={i: 2 + i for i in range(nb)},
        compiler_params=pltpu.CompilerParams(has_side_effects=EFFECT),
    )(*[_hbm(b) for b in bufs], *after)
    return (outs[0], outs[1]), list(outs[2:2 + nb]), outs[-1]


def _copy_wait(name, bufs, sems, plan, after):
    nb, na = len(bufs), len(after)

    def body(*refs):
        sends, recvs = plan(refs[:nb], refs[nb], refs[nb + 1])
        for mk in sends:
            mk().wait_send()
        for mk in recvs:
            mk().wait_recv()

    outs = pl.pallas_call(
        body, name=name,
        out_shape=tuple(pltpu.HBM(b.shape, b.dtype) for b in bufs),
        in_specs=(HBM,) * nb + (SEM, SEM) + (ANY,) * na, out_specs=(HBM,) * nb,
        input_output_aliases={i: i for i in range(nb)},
        compiler_params=pltpu.CompilerParams(has_side_effects=EFFECT),
    )(*bufs, *sems, *after)
    return list(outs)


def _copy_blocking(name, bufs, n_sems, plan, after=()):
    nb, na = len(bufs), len(after)

    def body(*refs):
        sends, recvs = plan(refs[:nb], refs[2 * nb + na], refs[2 * nb + na + 1])
        started = [mk() for mk in sends]
        for cp in started:
            cp.start()
        for mk in recvs:
            mk().wait_recv()
        for cp in started:
            cp.wait_send()

    return list(pl.pallas_call(
        body, name=name,
        out_shape=tuple(jax.ShapeDtypeStruct(b.shape, b.dtype) for b in bufs),
        in_specs=(ANY,) * (nb + na), out_specs=(ANY,) * nb,
        input_output_aliases={i: i for i in range(nb)},
        scratch_shapes=[pltpu.SemaphoreType.DMA((n_sems,)), pltpu.SemaphoreType.DMA((n_sems,))],
    )(*bufs, *after))


def _exchange(name, bufs, n_sems, plan, between):
    if between is None:
        return _copy_blocking(name, bufs, n_sems, plan)
    sems, bufs, tok = _copy_start(name + "_start", bufs, n_sems, plan)
    return _copy_wait(name + "_wait", bufs, sems, plan, between(tok))


def _gather_direct_plan(lays):
    def plan(full, send, recv):
        x, y, c = _me()
        chip = 2 * x + y
        sends, recvs = [], []
        for r, (fx, fy) in enumerate(CHIP_RELS):
            px, py = _flip(x, fx), _flip(y, fy)
            for i, lay in enumerate(lays):
                for q in range(2):
                    oc = _flip(c, q)
                    mine = _full_view(full[i], lay, chip, c)
                    idx = (r * len(lays) + i) * 2 + q
                    sends.append(_remote(mine, mine, send, recv, idx, (px, py, oc)))
                    theirs = _full_view(full[i], lay, 2 * px + py, oc)
                    recvs.append(_remote(theirs, theirs, send, recv, idx, (x, y, c)))
        return sends, recvs
    return plan


def _gather_chip_plan(lays):
    def plan(full, send, recv):
        x, y, c = _me()
        chip = 2 * x + y
        sends, recvs = [], []
        for r, (fx, fy) in enumerate(CHIP_RELS):
            px, py = _flip(x, fx), _flip(y, fy)
            for i, lay in enumerate(lays):
                mine = _full_view(full[i], lay, chip, c)
                sends.append(_remote(mine, mine, send, recv, r * len(lays) + i, (px, py, c)))
                theirs = _full_view(full[i], lay, 2 * px + py, c)
                recvs.append(_remote(theirs, theirs, send, recv, r * len(lays) + i, (x, y, c)))
        return sends, recvs
    return plan


def _gather_pass_plan(lays):
    def plan(full, send, recv):
        x, y, c = _me()
        sends, recvs = [], []
        for r, (fx, fy) in enumerate(CHIP_RELS):
            pchip = 2 * _flip(x, fx) + _flip(y, fy)
            for i, lay in enumerate(lays):
                landed = _full_view(full[i], lay, pchip, c)
                sends.append(_remote(landed, landed, send, recv, r * len(lays) + i, (x, y, 1 - c)))
                other = _full_view(full[i], lay, pchip, 1 - c)
                recvs.append(_remote(other, other, send, recv, r * len(lays) + i, (x, y, c)))
        return sends, recvs
    return plan


def _pair_exchange_plan(lays):
    nw = len(lays)

    def plan(bufs, send, recv):
        x, y, c = _me()
        sends, recvs = [], []
        for i, lay in enumerate(lays):
            for k in range(N_CHIPS):
                sends.append(_remote(_full_view(bufs[i], lay, k, 1 - c), bufs[nw + i].at[k],
                                     send, recv, i * N_CHIPS + k, (x, y, 1 - c)))
                recvs.append(_remote(_full_view(bufs[i], lay, k, c), bufs[nw + i].at[k],
                                     send, recv, i * N_CHIPS + k, (x, y, c)))
        return sends, recvs
    return plan


def _chip_exchange_plan(nw):
    def plan(bufs, send, recv):
        x, y, c = _me()
        sends, recvs = [], []
        for r, (fx, fy) in enumerate(CHIP_RELS):
            px, py = _flip(x, fx), _flip(y, fy)
            for i in range(nw):
                sends.append(_remote(bufs[i].at[2 * px + py], bufs[nw + i].at[r], send, recv, r * nw + i, (px, py, c)))
                recvs.append(_remote(bufs[i].at[2 * px + py], bufs[nw + i].at[r], send, recv, r * nw + i, (x, y, c)))
        return sends, recvs
    return plan


def _pair_share_plan(lays):
    def plan(bufs, send, recv):
        x, y, c = _me()
        sends, recvs = [], []
        for i, lay in enumerate(lays):
            mine = _half_view(bufs[i], lay, c)
            sends.append(_remote(mine, mine, send, recv, i, (x, y, 1 - c)))
            other = _half_view(bufs[i], lay, 1 - c)
            recvs.append(_remote(other, other, send, recv, i, (x, y, c)))
        return sends, recvs
    return plan


def _pair_sum(ids, g, got, lay, name):
    r, c, by_col = lay
    hr = r // 2
    tr = min(hr, 256)
    nt = hr // tr

    def body(ids_ref, g_ref, got_ref, s32_ref, s16_ref):
        s = g_ref[...] + got_ref[...]
        s32_ref[...] = s
        s16_ref[...] = s.astype(BF16)

    if by_col:
        gspec = pl.BlockSpec((tr, c), lambda k, t, ids: (ids[1] * nt + t, k))
    else:
        gspec = pl.BlockSpec((tr, c), lambda k, t, ids: ((2 * k + ids[1]) * nt + t, 0))
    hspec = pl.BlockSpec((None, tr, c), lambda k, t, ids: (k, t, 0))
    return pl.pallas_call(
        body, name=name,
        grid_spec=pltpu.PrefetchScalarGridSpec(
            num_scalar_prefetch=1, grid=(N_CHIPS, nt),
            in_specs=[gspec, hspec], out_specs=[hspec, hspec]),
        out_shape=[jax.ShapeDtypeStruct((N_CHIPS, hr, c), F32),
                   jax.ShapeDtypeStruct((N_CHIPS, hr, c), BF16)],
    )(ids, g, got)


def _chip_sum(ids, s32, got, lay, name):
    hr, c = _half_shape(lay)
    tr = min(hr, 256)
    nt = hr // tr

    def body(ids_ref, s_ref, got_ref, out_ref):
        t = s_ref[...]
        for r in range(3):
            t = t + got_ref[r].astype(F32)
        out_ref[...] = t

    return pl.pallas_call(
        body, name=name,
        grid_spec=pltpu.PrefetchScalarGridSpec(
            num_scalar_prefetch=1, grid=(nt,),
            in_specs=[pl.BlockSpec((None, tr, c), lambda t, ids: (ids[0], t, 0)),
                      pl.BlockSpec((3, tr, c), lambda t, ids: (0, t, 0))],
            out_specs=pl.BlockSpec((tr, c), lambda t, ids: (ids[1] * nt + t, 0))),
        out_shape=jax.ShapeDtypeStruct((2 * hr, c), F32),
    )(ids, s32, got)


def _reduce_scatter(ids, grads, lays, names, tag, between):
    nw = len(lays)
    got1 = [lax.empty((N_CHIPS,) + _half_shape(l), F32) for l in lays]
    bufs = _exchange("pair_exchange_" + tag, list(grads) + got1, nw * N_CHIPS, _pair_exchange_plan(lays), between[0])
    sums = [_pair_sum(ids, bufs[i], bufs[nw + i], lays[i], "pair_sum_" + names[i]) for i in range(nw)]
    got2 = [lax.empty((3,) + _half_shape(l), BF16) for l in lays]
    bufs = _exchange("chip_exchange_" + tag, [s[1] for s in sums] + got2, 3 * nw, _chip_exchange_plan(nw), between[1])
    mine = [_chip_sum(ids, sums[i][0], bufs[nw + i], lays[i], "chip_sum_" + names[i]) for i in range(nw)]
    return _exchange("pair_share_" + tag, mine, nw, _pair_share_plan(lays), between[2])


def _cast_place(ids, ws, lays):
    nw = len(ws)

    def body(ids_ref, *refs):
        for i in range(nw):
            refs[nw + i][...] = refs[i][...].astype(BF16)

    by_col_map = lambda i, ids: (0, ids[0])
    by_row_map = lambda i, ids: (ids[0], 0)
    return pl.pallas_call(
        body, name="cast_place",
        grid_spec=pltpu.PrefetchScalarGridSpec(
            num_scalar_prefetch=1, grid=(1,),
            in_specs=[pl.BlockSpec(l[:2], lambda i, ids: (0, 0)) for l in lays],
            out_specs=[pl.BlockSpec(l[:2], by_col_map if l[2] else by_row_map) for l in lays]),
        out_shape=[jax.ShapeDtypeStruct(_full_shape(l), BF16) for l in lays],
    )(ids, *ws)


def _ada_mod(c_all, w_ada, b_ada):
    def body(c_ref, w_ref, b_ref, act_ref, mod_ref):
        cv = c_ref[...]
        act = cv * _sigmoid(cv)
        act_ref[...] = act
        mod_ref[...] = jnp.dot(act, w_ref[...], preferred_element_type=F32, precision=HIGHEST) + b_ref[...]

    nb = c_all.shape[0]
    return pl.pallas_call(
        body, name="ada_mod",
        out_shape=[jax.ShapeDtypeStruct(c_all.shape, F32),
                   jax.ShapeDtypeStruct((nb, w_ada.shape[1]), F32)],
        in_specs=[VMEM] * 3, out_specs=[VMEM] * 2,
    )(c_all, w_ada, b_ada)


def _run_units(*unit_lists):
    total = max(len(u) for u in unit_lists)
    done = [0] * len(unit_lists)
    for step in range(1, total + 1):
        for li, units in enumerate(unit_lists):
            upto = (step * len(units) + total - 1) // total
            while done[li] < upto:
                units[done[li]]()
                done[li] += 1


def _in_mix_fwd(x, mod3, w_in, wb, dwb, lng, lnb, scw, t, dep):
    n, d = x.shape
    d_in = w_in.shape[1]
    dc = d_in // 5
    tm = min(TM_MIX, t)
    tps = t // tm
    nt = n // tm
    rb = min(RB_CONV, tm)
    ncol = 256
    ng = dc // 128
    assert tps % 2 == 0 and nt % 2 == 0

    def proj_units(x_ref, mod_ref, w_ref, p_ref, h_ref):
        def head():
            xn, _ = _rms(x_ref[...])
            h_ref[...] = (xn * (1.0 + mod_ref[:, d:2 * d]) + mod_ref[:, 0:d]).astype(BF16)
        units = [head]
        for c0 in range(0, d_in, ncol):
            def chunk(c0=c0):
                p_ref[:, c0:c0 + ncol] = _dot(h_ref[...], w_ref[:, c0:c0 + ncol])
            units.append(chunk)
        return units

    def mix_units(first, p_ref, h_ref, r0, wb_ref, dwb_ref, lng_ref, lnb_ref, scw_ref,
                  proj_ref, h1_ref, mixed_ref, a1_ref, q_ref, ext_ref, e_ref, extp_ref, a1s_ref):
        rows = slice(r0, r0 + tm)
        units = []

        def glu():
            halo = ext_ref[tm:tm + HALO, :]
            ext_ref[0:HALO, :] = halo if first is False else jnp.where(first, 0.0, halo)
            ext_ref[HALO:HALO + tm, :] = p_ref[:, 0:dc] * _sigmoid(p_ref[:, dc:2 * dc])
        units.append(glu)
        for g in range(ng):
            def shift(g=g):
                for r in range(8):
                    e_ref[g, r, 0:tm + HALO, :] = ext_ref[r:r + tm + HALO, 128 * g:128 * g + 128]
            units.append(shift)
        for g in range(ng):
            lanes = slice(128 * g, 128 * g + 128)
            for i0 in range(0, tm, rb):
                def conv(g=g, lanes=lanes, i0=i0):
                    acc = jnp.zeros((rb, 128), F32)
                    for k in range(CONF_K):
                        m, r = divmod(k + HALO - CONF_K + 1, 8)
                        acc = acc + e_ref[g, r, i0 + 8 * m:i0 + 8 * m + rb, :] * wb_ref[k:k + 1, lanes]
                    a1s_ref[i0:i0 + rb, lanes] = acc + dwb_ref[:, lanes]
                units.append(conv)

        def norm():
            a1 = a1s_ref[...]
            a1_ref[rows, :] = a1
            mu = jnp.mean(a1, axis=-1, keepdims=True)
            ac = a1 - mu
            rstd = lax.rsqrt(jnp.mean(ac * ac, axis=-1, keepdims=True) + EPS)
            a2 = ac * rstd * lng_ref[...] + lnb_ref[...]
            mixed_ref[rows, 0:dc] = (a2 * _sigmoid(a2)).astype(BF16)
        units.append(norm)

        def short():
            halo = extp_ref[tm:tm + 8, :]
            extp_ref[0:8, :] = halo if first is False else jnp.where(first, 0.0, halo)
            extp_ref[8:8 + tm, :] = p_ref[:, 3 * dc:4 * dc] * p_ref[:, 4 * dc:5 * dc]
            q = jnp.zeros((tm, dc), F32)
            for k in range(SC_K):
                q = q + extp_ref[6 + k:6 + k + tm, :] * scw_ref[k:k + 1, :]
            q_ref[rows, :] = q
            mixed_ref[rows, dc:2 * dc] = (p_ref[:, 2 * dc:3 * dc] * q).astype(BF16)
        units.append(short)

        def keep():
            proj_ref[rows, :] = p_ref[...]
            h1_ref[rows, :] = h_ref[...]
        units.append(keep)
        return units

    def body(x0_ref, xa_ref, xb_ref, mod0_ref, moda_ref, modb_ref, w_ref,
             wb_ref, dwb_ref, lng_ref, lnb_ref, scw_ref, dep_ref,
             proj_ref, h1_ref, mixed_ref, a1_ref, q_ref,
             p0_ref, p1_ref, h0_ref, hh1_ref, ext_ref, e_ref, extp_ref, a1s_ref):
        j = pl.program_id(0)

        @pl.when(j == 0)
        def _():
            ext_ref[...] = jnp.zeros(ext_ref.shape, F32)
            extp_ref[...] = jnp.zeros(extp_ref.shape, F32)
            _run_units(proj_units(x0_ref, mod0_ref, w_ref, p0_ref, h0_ref))

        common = (wb_ref, dwb_ref, lng_ref, lnb_ref, scw_ref, proj_ref, h1_ref, mixed_ref, a1_ref, q_ref,
                  ext_ref, e_ref, extp_ref, a1s_ref)
        _run_units(mix_units((2 * j) % tps == 0, p0_ref, h0_ref, 0, *common),
                   proj_units(xa_ref, moda_ref, w_ref, p1_ref, hh1_ref))
        _run_units(mix_units(False, p1_ref, hh1_ref, tm, *common),
                   proj_units(xb_ref, modb_ref, w_ref, p0_ref, h0_ref))

    last = nt - 1
    xspec = lambda f: pl.BlockSpec((tm, d), lambda j: (f(j), 0))
    mspec = lambda f: pl.BlockSpec((None, 1, 6 * d), lambda j: (f(j) // tps, 0, 0))
    out2 = lambda c: pl.BlockSpec((2 * tm, c), lambda j: (j, 0))
    return pl.pallas_call(
        body, name="in_mix_fwd", grid=(nt // 2,),
        in_specs=[xspec(lambda j: 0), xspec(lambda j: 2 * j + 1), xspec(lambda j: jnp.minimum(2 * j + 2, last)),
                  mspec(lambda j: 0), mspec(lambda j: 2 * j + 1), mspec(lambda j: jnp.minimum(2 * j + 2, last)),
                  _resident((d, d_in)),
                  _const(wb.shape), _const(dwb.shape), _const(lng.shape), _const(lnb.shape), _const(scw.shape), ANY],
        out_specs=[out2(d_in), out2(d), out2(2 * dc), out2(dc), out2(dc)],
        out_shape=[jax.ShapeDtypeStruct((n, d_in), F32), jax.ShapeDtypeStruct((n, d), BF16),
                   jax.ShapeDtypeStruct((n, 2 * dc), BF16), jax.ShapeDtypeStruct((n, dc), F32),
                   jax.ShapeDtypeStruct((n, dc), F32)],
        scratch_shapes=[pltpu.VMEM((tm, d_in), F32), pltpu.VMEM((tm, d_in), F32),
                        pltpu.VMEM((tm, d), BF16), pltpu.VMEM((tm, d), BF16),
                        pltpu.VMEM((tm + HALO + 8, dc), F32), _shift_scratch(tm, dc),
                        pltpu.VMEM((tm + 8, dc), F32), pltpu.VMEM((tm, dc), F32)],
        compiler_params=pltpu.CompilerParams(dimension_semantics=("arbitrary",)),
    )(x, x, x, mod3, mod3, mod3, w_in, wb, dwb, lng, lnb, scw, dep)


def _shift_scratch(tm, dc):
    return pltpu.VMEM((dc // 128, 8, tm + HALO + 8, 128), F32)


def _out_proj(mixed, w_out, x, mod3, t, dep):
    n, d = x.shape
    tm = min(TM_MM, t)
    tps = t // tm

    def body(m_ref, w_ref, x_ref, mod_ref, dep_ref, x1_ref, y1_ref, h2_ref):
        y1 = _dot(m_ref[...], w_ref[...])
        y1_ref[...] = y1.astype(BF16)
        x1 = x_ref[...] + mod_ref[:, 2 * d:3 * d] * y1
        x1_ref[...] = x1
        xn, _ = _rms(x1)
        h2_ref[...] = (xn * (1.0 + mod_ref[:, 4 * d:5 * d]) + mod_ref[:, 3 * d:4 * d]).astype(BF16)

    return pl.pallas_call(
        body, name="out_proj", grid=(n // tm,),
        in_specs=[_rowblk(tm, d), _resident((d, d)), _rowblk(tm, d), _modspec(tps, 6 * d), ANY],
        out_specs=[_rowblk(tm, d), _rowblk(tm, d), _rowblk(tm, d)],
        out_shape=[jax.ShapeDtypeStruct((n, d), F32), jax.ShapeDtypeStruct((n, d), BF16),
                   jax.ShapeDtypeStruct((n, d), BF16)],
        compiler_params=pltpu.CompilerParams(dimension_semantics=("parallel",)),
    )(mixed, w_out, x, mod3, dep)


def _mlp_fwd(h2, x1, tgt, mod3, gfin, w1, w2, t):
    n, d = x1.shape
    dff = w1.shape[1]
    tm = min(TM_MLP, t)
    tps = t // tm
    nt = n // tm

    def body(h_ref, x1_ref, tg_ref, mod_ref, gf_ref, w1_ref, w2_ref,
             z_ref, dx2_ref, dy2_ref, dg2_ref, ggf_ref, loss_ref):
        i = pl.program_id(0)
        g2 = mod_ref[:, 5 * d:6 * d]
        gf = gf_ref[...]
        sub = tm // MLP_SUB
        sums = None
        for part in range(MLP_SUB):
            rs = slice(part * sub, (part + 1) * sub)
            hv = h_ref[rs, :]
            y2 = jnp.zeros((sub, d), F32)
            for j in range(dff // FF_CHUNK):
                cols = slice(j * FF_CHUNK, (j + 1) * FF_CHUNK)
                z = _dot(hv, w1_ref[:, cols])
                z_ref[rs, cols] = z.astype(BF16)
                zr = jnp.maximum(z, 0.0)
                y2 = y2 + _dot((zr * zr).astype(BF16), w2_ref[cols, :])
            x2n, r3 = _rms(x1_ref[rs, :] + g2 * y2)
            diff = x2n * gf - tg_ref[rs, :]
            dout = diff * (1.0 / d)
            dx2 = _rms_bwd(dout * gf, x2n, r3)
            dx2_ref[rs, :] = dx2
            dy2_ref[rs, :] = (g2 * dx2).astype(BF16)
            p = (_rows8(dx2 * y2), _rows8(dout * x2n), _rows8(diff * diff))
            sums = p if sums is None else tuple(a + b for a, b in zip(sums, p))
        _acc_add(dg2_ref, i % tps == 0, sums[0])
        _acc_add(ggf_ref, i == 0, sums[1])
        _acc_add(loss_ref, i == 0, sums[2])

    return pl.pallas_call(
        body, name="mlp_fwd", grid=(nt,),
        in_specs=[_rowblk(tm, d), _rowblk(tm, d), _rowblk(tm, d), _modspec(tps, 6 * d), _const((1, d)),
                  _resident((d, dff)), _resident((dff, d))],
        out_specs=[_rowblk(tm, dff), _rowblk(tm, d), _rowblk(tm, d), _accspec(tps, d),
                   _const((8, d)), _const((8, d))],
        out_shape=[jax.ShapeDtypeStruct((n, dff), BF16), jax.ShapeDtypeStruct((n, d), F32),
                   jax.ShapeDtypeStruct((n, d), BF16), jax.ShapeDtypeStruct((n // t, 8, d), F32),
                   jax.ShapeDtypeStruct((8, d), F32), jax.ShapeDtypeStruct((8, d), F32)],
        compiler_params=pltpu.CompilerParams(dimension_semantics=("arbitrary",)),
    )(h2, x1, tgt, mod3, gfin, w1, w2)


def _mlp_bwd(dy2, z, x1, dx2, y1, mod3, w1, w2, t):
    n, d = x1.shape
    dff = w1.shape[1]
    tm = min(TM_MLP, t)
    tps = t // tm

    def body(dy2_ref, z_ref, x1_ref, dx2_ref, y1_ref, mod_ref, w1_ref, w2_ref,
             dz_ref, dx1_ref, dy1_ref, dsh_ref, dsc_ref, dg1_ref):
        first = pl.program_id(0) % tps == 0
        sub = tm // MLP_SUB
        sums = None
        for part in range(MLP_SUB):
            rs = slice(part * sub, (part + 1) * sub)
            dy2 = dy2_ref[rs, :]
            dh2 = jnp.zeros((sub, d), F32)
            for j in range(dff // FF_CHUNK):
                cols = slice(j * FF_CHUNK, (j + 1) * FF_CHUNK)
                du = _dot_nt(dy2, w2_ref[cols, :])
                dz = (du * (2.0 * jnp.maximum(z_ref[rs, cols].astype(F32), 0.0))).astype(BF16)
                dz_ref[rs, cols] = dz
                dh2 = dh2 + _dot_nt(dz, w1_ref[:, cols])
            x1n, r2 = _rms(x1_ref[rs, :])
            dx1 = dx2_ref[rs, :] + _rms_bwd(dh2 * (1.0 + mod_ref[:, 4 * d:5 * d]), x1n, r2)
            dx1_ref[rs, :] = dx1
            dy1_ref[rs, :] = (mod_ref[:, 2 * d:3 * d] * dx1).astype(BF16)
            p = (_rows8(dh2), _rows8(dh2 * x1n), _rows8(dx1 * y1_ref[rs, :].astype(F32)))
            sums = p if sums is None else tuple(a + b for a, b in zip(sums, p))
        _acc_add(dsh_ref, first, sums[0])
        _acc_add(dsc_ref, first, sums[1])
        _acc_add(dg1_ref, first, sums[2])

    acc = jax.ShapeDtypeStruct((n // t, 8, d), F32)
    return pl.pallas_call(
        body, name="mlp_bwd", grid=(n // tm,),
        in_specs=[_rowblk(tm, d), _rowblk(tm, dff), _rowblk(tm, d), _rowblk(tm, d), _rowblk(tm, d),
                  _modspec(tps, 6 * d), _resident((d, dff)), _resident((dff, d))],
        out_specs=[_rowblk(tm, dff), _rowblk(tm, d), _rowblk(tm, d),
                   _accspec(tps, d), _accspec(tps, d), _accspec(tps, d)],
        out_shape=[jax.ShapeDtypeStruct((n, dff), BF16), jax.ShapeDtypeStruct((n, d), F32),
                   jax.ShapeDtypeStruct((n, d), BF16), acc, acc, acc],
        compiler_params=pltpu.CompilerParams(dimension_semantics=("arbitrary",)),
    )(dy2, z, x1, dx2, y1, mod3, w1, w2)


def _wgrad(a, b, name, relu2=False, bn=None, deps=()):
    n, ka = a.shape
    nb = b.shape[1]
    tk = min(TK_WG, n)
    bm = min(ka, 1024)
    if bn is None:
        bn = nb if nb <= 2048 else nb // 2

    def body(a_ref, b_ref, *rest):
        out_ref = rest[-1]
        av = a_ref[...]
        if relu2:
            ar = jnp.maximum(av, 0.0)
            av = ar * ar
        p = _dot_tn(av, b_ref[...])
        _acc_add(out_ref, pl.program_id(2) == 0, p)

    return pl.pallas_call(
        body, name=name, grid=(ka // bm, nb // bn, n // tk),
        in_specs=[pl.BlockSpec((tk, bm), lambda i, j, k: (k, i)),
                  pl.BlockSpec((tk, bn), lambda i, j, k: (k, j))] + [ANY] * len(deps),
        out_specs=pl.BlockSpec((bm, bn), lambda i, j, k: (i, j)),
        out_shape=jax.ShapeDtypeStruct((ka, nb), F32),
        compiler_params=pltpu.CompilerParams(dimension_semantics=("parallel", "parallel", "arbitrary")),
    )(a, b, *deps)


SG_DWW = 0
SG_DWB = CONF_K
SG_LNG = CONF_K + 1
SG_LNB = CONF_K + 2
SG_SCW = CONF_K + 3
SG_N = CONF_K + 3 + SC_K


def _mix_in_bwd(dy1, w_out, a1, q, proj, wb, lng, lnb, scw, w_in, x, dx1, mod3, t, dep):
    n, d_in = proj.shape
    d = x.shape[1]
    dc = d_in // 5
    tm = min(TM_MIX, t)
    tps = t // tm
    nt = n // tm
    rw = min(RB_WG, tm)
    ng = dc // 128
    ncol = 256

    def ln_bwd(a1v, da3, lng_v, lnb_v):
        mu = jnp.mean(a1v, axis=-1, keepdims=True)
        ac = a1v - mu
        rstd = lax.rsqrt(jnp.mean(ac * ac, axis=-1, keepdims=True) + EPS)
        ah = ac * rstd
        a2 = ah * lng_v + lnb_v
        s2 = _sigmoid(a2)
        da2 = da3 * (s2 * (1.0 + a2 * (1.0 - s2)))
        dah = da2 * lng_v
        da1 = rstd * (dah - jnp.mean(dah, axis=-1, keepdims=True)
                      - ah * jnp.mean(dah * ah, axis=-1, keepdims=True))
        return da1, da2, ah

    def dmixed_units(dy_ref, wo_ref, dm_ref):
        units = []
        for c0 in range(0, 2 * dc, ncol):
            def chunk(c0=c0):
                dm_ref[:, c0:c0 + ncol] = _dot_nt(dy_ref[...], wo_ref[c0:c0 + ncol, :])
            units.append(chunk)
        return units

    def mix_units(k, dm_ref, a1_ref, q_ref, p_ref, wb_ref, lng_ref, lnb_ref, scw_ref,
                  dproj_ref, sg_ref, extd_ref, ed_ref, a0_ref, da0_ref, extq_ref, cda_ref, cdq_ref):
        keep_next = jnp.where(k % tps == tps - 1, 0.0, 1.0)
        units = []

        def head():
            lng_v, lnb_v = lng_ref[...], lnb_ref[...]
            da1, da2, ah = ln_bwd(a1_ref[...], dm_ref[:, 0:dc], lng_v, lnb_v)
            sg_ref[8 * SG_LNG:8 * SG_LNG + 8, :] += _rows8(da2 * ah)
            sg_ref[8 * SG_LNB:8 * SG_LNB + 8, :] += _rows8(da2)
            sg_ref[8 * SG_DWB:8 * SG_DWB + 8, :] += _rows8(da1)
            a0_ref[...] = p_ref[:, 0:dc] * _sigmoid(p_ref[:, dc:2 * dc])
            extd_ref[0:tm, :] = da1
            extd_ref[tm:tm + HALO, :] = cda_ref[...] * keep_next
            extd_ref[tm + HALO:tm + HALO + 8, :] = jnp.zeros((8, dc), F32)
            cda_ref[...] = da1[0:HALO, :]
            ds = dm_ref[:, dc:2 * dc]
            dproj_ref[:, 2 * dc:3 * dc] = (ds * q_ref[...]).astype(BF16)
            dq = ds * p_ref[:, 2 * dc:3 * dc]
            extq_ref[0:tm, :] = dq
            extq_ref[tm:tm + 8, :] = cdq_ref[...] * keep_next
            cdq_ref[...] = dq[0:8, :]
        units.append(head)
        for g in range(ng):
            def shift(g=g):
                for r in range(8):
                    ed_ref[g, r, 0:tm + HALO, :] = extd_ref[r:r + tm + HALO, 128 * g:128 * g + 128]
            units.append(shift)
        for g in range(ng):
            lanes = slice(128 * g, 128 * g + 128)
            accs = [None] * CONF_K
            for i0 in range(0, tm, rw):
                def conv(g=g, lanes=lanes, i0=i0, accs=accs):
                    a0v = a0_ref[i0:i0 + rw, lanes]
                    acc = jnp.zeros((rw, 128), F32)
                    for s in range(CONF_K):
                        m, r = divmod(s, 8)
                        e = ed_ref[g, r, i0 + 8 * m:i0 + 8 * m + rw, :]
                        acc = acc + e * wb_ref[CONF_K - 1 - s:CONF_K - s, lanes]
                        part = _rows8(e * a0v)
                        accs[s] = part if accs[s] is None else accs[s] + part
                    da0_ref[i0:i0 + rw, lanes] = acc
                units.append(conv)

            def flush(lanes=lanes, accs=accs):
                for s in range(CONF_K):
                    kk = CONF_K - 1 - s
                    sg_ref[8 * (SG_DWW + kk):8 * (SG_DWW + kk) + 8, lanes] += accs[s]
            units.append(flush)

        def emit():
            da0 = da0_ref[...]
            sig = _sigmoid(p_ref[:, dc:2 * dc])
            dproj_ref[:, 0:dc] = (da0 * sig).astype(BF16)
            dproj_ref[:, dc:2 * dc] = (da0 * a0_ref[...] * (1.0 - sig)).astype(BF16)
        units.append(emit)

        def short():
            scc, sch = p_ref[:, 3 * dc:4 * dc], p_ref[:, 4 * dc:5 * dc]
            pv = scc * sch
            dp = jnp.zeros((tm, dc), F32)
            for kk in range(SC_K):
                dqs = extq_ref[SC_K - 1 - kk:SC_K - 1 - kk + tm, :]
                dp = dp + dqs * scw_ref[kk:kk + 1, :]
                sg_ref[8 * (SG_SCW + kk):8 * (SG_SCW + kk) + 8, :] += _rows8(pv * dqs)
            dproj_ref[:, 3 * dc:4 * dc] = (dp * sch).astype(BF16)
            dproj_ref[:, 4 * dc:5 * dc] = (dp * scc).astype(BF16)
        units.append(short)
        return units

    def proj_units(first, dp_ref, w_ref, x_ref, dx1_ref, mod_ref, dh_ref, gx_ref, dsh_ref, dsc_ref):
        units = []
        for c0 in range(0, d, ncol):
            def chunk(c0=c0):
                dh_ref[:, c0:c0 + ncol] = _dot_nt(dp_ref[...], w_ref[c0:c0 + ncol, :])
            units.append(chunk)

        def tail():
            dh1 = dh_ref[...]
            xn, r1 = _rms(x_ref[...])
            v1, v2 = _rows8(dh1), _rows8(dh1 * xn)
            dsh_ref[...] = jnp.where(first, v1, dsh_ref[...] + v1)
            dsc_ref[...] = jnp.where(first, v2, dsc_ref[...] + v2)
            gx_ref[...] = dx1_ref[...] + _rms_bwd(dh1 * (1.0 + mod_ref[:, d:2 * d]), xn, r1)
        units.append(tail)
        return units

    def body(dyl_ref, dy_ref, wo_ref, a1_ref, q_ref, p_ref, wb_ref, lng_ref, lnb_ref, scw_ref,
             w_ref, x_ref, dx1_ref, mod_ref, dep_ref,
             dproj_ref, sg_ref, gx_ref, dsh_ref, dsc_ref,
             extd_ref, ed_ref, a0_ref, da0_ref, extq_ref, dp_ref, dh_ref, dm_ref, cda_ref, cdq_ref):
        k = pl.program_id(0)

        @pl.when(k == 0)
        def _():
            sg_ref[...] = jnp.zeros(sg_ref.shape, F32)
            dp_ref[...] = jnp.zeros(dp_ref.shape, BF16)
            cda_ref[...] = jnp.zeros(cda_ref.shape, F32)
            cdq_ref[...] = jnp.zeros(cdq_ref.shape, F32)
            _run_units(dmixed_units(dyl_ref, wo_ref, dm_ref))

        first = jnp.logical_or(k <= 1, (nt - k) % tps == tps - 1)
        after = lambda: proj_units(first, dp_ref, w_ref, x_ref, dx1_ref, mod_ref, dh_ref, gx_ref, dsh_ref, dsc_ref)

        @pl.when(k < nt)
        def _():
            _run_units(mix_units(nt - 1 - k, dm_ref, a1_ref, q_ref, p_ref, wb_ref, lng_ref, lnb_ref, scw_ref,
                                 dproj_ref, sg_ref, extd_ref, ed_ref, a0_ref, da0_ref, extq_ref, cda_ref, cdq_ref),
                       after(), dmixed_units(dy_ref, wo_ref, dm_ref))
            dp_ref[...] = dproj_ref[...]

        @pl.when(k == nt)
        def _():
            _run_units(after())

    cur = lambda k: jnp.maximum(nt - 1 - k, 0)
    prev = lambda k: jnp.minimum(nt - k, nt - 1)
    acc = jax.ShapeDtypeStruct((n // t, 8, d), F32)
    return pl.pallas_call(
        body, name="mix_in_bwd", grid=(nt + 1,),
        in_specs=[pl.BlockSpec((tm, d), lambda k: (nt - 1, 0)),
                  pl.BlockSpec((tm, d), lambda k: (jnp.maximum(nt - 2 - k, 0), 0)), _resident(w_out.shape),
                  pl.BlockSpec((tm, dc), lambda k: (cur(k), 0)),
                  pl.BlockSpec((tm, dc), lambda k: (cur(k), 0)),
                  pl.BlockSpec((tm, d_in), lambda k: (cur(k), 0)),
                  _const(wb.shape), _const(lng.shape), _const(lnb.shape), _const(scw.shape),
                  _resident((d, d_in)),
                  pl.BlockSpec((tm, d), lambda k: (prev(k), 0)), pl.BlockSpec((tm, d), lambda k: (prev(k), 0)),
                  pl.BlockSpec((None, 1, 6 * d), lambda k: (prev(k) // tps, 0, 0)), ANY],
        out_specs=[pl.BlockSpec((tm, d_in), lambda k: (cur(k), 0)), _const((8 * SG_N, dc)),
                   pl.BlockSpec((tm, d), lambda k: (prev(k), 0)),
                   pl.BlockSpec((None, 8, d), lambda k: (prev(k) // tps, 0, 0)),
                   pl.BlockSpec((None, 8, d), lambda k: (prev(k) // tps, 0, 0))],
        out_shape=[jax.ShapeDtypeStruct((n, d_in), BF16), jax.ShapeDtypeStruct((8 * SG_N, dc), F32),
                   jax.ShapeDtypeStruct((n, d), F32), acc, acc],
        scratch_shapes=[pltpu.VMEM((tm + HALO + 8, dc), F32), _shift_scratch(tm, dc),
                        pltpu.VMEM((tm, dc), F32), pltpu.VMEM((tm, dc), F32),
                        pltpu.VMEM((tm + 8, dc), F32),
                        pltpu.VMEM((tm, d_in), BF16), pltpu.VMEM((tm, d), F32), pltpu.VMEM((tm, 2 * dc), F32),
                        pltpu.VMEM((HALO, dc), F32), pltpu.VMEM((8, dc), F32)],
        compiler_params=pltpu.CompilerParams(dimension_semantics=("arbitrary",)),
    )(dy1, dy1, w_out, a1, q, proj, wb, lng, lnb, scw, w_in, x, dx1, mod3, dep)


SMALL_ROWS = 40


def _pack_small(sg, ggf, loss, accs, d, dep):
    dc = d // 2
    nb = accs[0].shape[0]

    def body(sg_ref, ggf_ref, loss_ref, dsh1, dsc1, dg1, dsh2, dsc2, dg2, dep_ref, pack_ref, dmod_ref):
        pack_ref[...] = jnp.zeros(pack_ref.shape, F32)
        for k in range(SG_N):
            pack_ref[k:k + 1, :] = jnp.sum(sg_ref[8 * k:8 * k + 8, :], axis=0, keepdims=True)
        gf = jnp.sum(ggf_ref[...], axis=0, keepdims=True)
        pack_ref[SG_N:SG_N + 1, :] = gf[:, 0:dc]
        pack_ref[SG_N + 1:SG_N + 2, :] = gf[:, dc:d]
        tot = jnp.sum(jnp.sum(loss_ref[...], axis=0, keepdims=True), axis=1, keepdims=True) * (0.5 / d)
        pack_ref[SG_N + 2:SG_N + 3, :] = jnp.broadcast_to(tot, (1, dc))
        for b in range(nb):
            row = jnp.concatenate([jnp.sum(ref[b], axis=0, keepdims=True)
                                   for ref in (dsh1, dsc1, dg1, dsh2, dsc2, dg2)], axis=1)
            for f in range(fold):
                dmod_ref[b * fold + f:b * fold + f + 1, :] = row[:, f * wf:(f + 1) * wf]

    fold = 8 // nb
    wf = 6 * d // fold
    assert nb * fold == 8 and wf % 128 == 0
    return pl.pallas_call(
        body, name="pack_small",
        out_shape=[jax.ShapeDtypeStruct((SMALL_ROWS, dc), F32), jax.ShapeDtypeStruct((8, wf), F32)],
        in_specs=[VMEM] * 9 + [ANY], out_specs=[VMEM] * 2,
    )(sg, ggf, loss, *accs, dep)


def _small_reduce(pack_all, dmod_all, nb, dep):
    def body(pk_ref, dm_ref, dep_ref, red_ref, dmod_ref, gb_ref):
        tot = pk_ref[0]
        for dev in range(1, N_DEV):
            tot = tot + pk_ref[dev]
        red_ref[...] = tot
        for f in range(fold):
            gb = jnp.zeros((1, wf), F32)
            for dev in range(N_DEV):
                for b in range(nb):
                    seg = dm_ref[dev, b * fold + f:b * fold + f + 1, :]
                    dmod_ref[dev * nb + b:dev * nb + b + 1, f * wf:(f + 1) * wf] = seg
                    gb = gb + seg
            gb_ref[:, f * wf:(f + 1) * wf] = gb

    fold = 8 // nb
    wf = dmod_all.shape[2]
    return pl.pallas_call(
        body, name="small_reduce",
        out_shape=[jax.ShapeDtypeStruct(pack_all.shape[1:], F32),
                   jax.ShapeDtypeStruct((N_DEV * nb, fold * wf), F32),
                   jax.ShapeDtypeStruct((1, fold * wf), F32)],
        in_specs=[VMEM] * 2 + [ANY], out_specs=[VMEM] * 3,
    )(pack_all, dmod_all, dep)


def _adam(w, g, m, v):
    m = ADAM_B1 * m + (1.0 - ADAM_B1) * g
    v = ADAM_B2 * v + (1.0 - ADAM_B2) * (g * g)
    m_hat = m / (1.0 - ADAM_B1 ** ADAM_STEP)
    v_hat = v / (1.0 - ADAM_B2 ** ADAM_STEP)
    delta = -ADAM_LR * (m_hat / (jnp.sqrt(v_hat) + ADAM_EPS) + ADAM_WD * w)
    return delta, m, v


def _adamw_big(w, g, m, v, name):
    r, c = w.shape
    tr = min(r, 256)

    def body(w_ref, g_ref, m_ref, v_ref, d_ref, nm_ref, nv_ref):
        d_ref[...], nm_ref[...], nv_ref[...] = _adam(w_ref[...], g_ref[...], m_ref[...], v_ref[...])

    s = jax.ShapeDtypeStruct((r, c), F32)
    return pl.pallas_call(
        body, name=name, grid=(r // tr,),
        in_specs=[_rowblk(tr, c)] * 4, out_specs=[_rowblk(tr, c)] * 3, out_shape=[s, s, s],
        compiler_params=pltpu.CompilerParams(dimension_semantics=("parallel",)),
    )(w, g, m, v)


def _adamw_ada(act_t, dmod_cols, w, m, v):
    r, c = w.shape
    tr = min(r, 256)
    nb = act_t.shape[1]

    def body(a_ref, dm_ref, w_ref, m_ref, v_ref, g_ref, d_ref, nm_ref, nv_ref):
        g = jnp.dot(a_ref[...], dm_ref[...], preferred_element_type=F32, precision=HIGHEST)
        g_ref[...] = g
        d_ref[...], nm_ref[...], nv_ref[...] = _adam(w_ref[...], g, m_ref[...], v_ref[...])

    s = jax.ShapeDtypeStruct((r, c), F32)
    return pl.pallas_call(
        body, name="adamw_w_ada", grid=(r // tr,),
        in_specs=[_rowblk(tr, nb), _const((nb, c))] + [_rowblk(tr, c)] * 3,
        out_specs=[_rowblk(tr, c)] * 4, out_shape=[s, s, s, s],
        compiler_params=pltpu.CompilerParams(dimension_semantics=("parallel",)),
    )(act_t, dmod_cols, w, m, v)


def _adamw_small(ws, gs, ms, vs):
    n = len(ws)

    def body(*refs):
        for i in range(n):
            w, g, m, v = (refs[j * n + i][...] for j in range(4))
            dl, nm, nv = _adam(w, g, m, v)
            refs[4 * n + i][...] = dl
            refs[5 * n + i][...] = nm
            refs[6 * n + i][...] = nv

    shapes = [jax.ShapeDtypeStruct(w.shape, F32) for w in ws]
    return pl.pallas_call(
        body, name="adamw_small", out_shape=shapes * 3,
        in_specs=[VMEM] * (4 * n), out_specs=[VMEM] * (3 * n),
    )(*ws, *gs, *ms, *vs)


def kernel(x, c, w_ada, b_ada, w_in, conf_dw_w, conf_dw_b, conf_ln_g, conf_ln_b, sc_conv_w, w_out, w_mlp1, w_mlp2, g_final, loss_target, m_w_ada, m_b_ada, m_w_in, m_conf_dw_w, m_conf_dw_b, m_conf_ln_g, m_conf_ln_b, m_sc_conv_w, m_w_out, m_w_mlp1, m_w_mlp2, m_g_final, v_w_ada, v_b_ada, v_w_in, v_conf_dw_w, v_conf_dw_b, v_conf_ln_g, v_conf_ln_b, v_sc_conv_w, v_w_out, v_w_mlp1, v_w_mlp2, v_g_final):
    nb, t, d = x.shape
    n = nb * t
    dc = d // 2
    ada_w = w_ada.shape[2]
    ax, ay, ac = _me()
    chip = 2 * ax + ay
    dev = 2 * chip + ac
    ids = jnp.stack([chip, ac]).astype(jnp.int32)

    lays = _wlayout(d)
    names = ("in", "out", "mlp1", "mlp2")
    fulls = _cast_place(ids, [w_in[0], w_out[0], w_mlp1[0], w_mlp2[0]], lays)

    c_pad = jnp.zeros((8, d), F32).at[0:nb].set(c)
    cw_pad = jnp.zeros((SMALL_ROWS, dc // N_CHIPS), F32)
    cw_pad = cw_pad.at[0:CONF_K].set(conf_dw_w[0]).at[HALO:HALO + SC_K].set(sc_conv_w[0])
    c_all8, cw_all8 = _all_gather8([c_pad, cw_pad], "gather_c")
    plan_i = _gather_chip_plan(lays[0:1])
    sems_i, bufs_i, tok_i = _copy_start("gather_in_start", [fulls[0]], 3, plan_i, [c_all8])
    c_all = c_all8[:, 0:nb].reshape(N_DEV * nb, d) + tok_i[0, 0]
    cw_full = jnp.concatenate([cw_all8[2 * k] for k in range(N_CHIPS)], axis=1)
    dww, scw = cw_full[0:CONF_K], cw_full[HALO:HALO + SC_K]
    b_cols = lax.dynamic_slice(b_ada, (0, chip * ada_w), (1, ada_w))
    c_act, mod_shard = _ada_mod(c_all, w_ada[0], b_cols)
    mod_all = _gather_mod(mod_shard)
    mod3 = lax.dynamic_slice(mod_all, (dev * nb, 0), (nb, 6 * d)).reshape(nb, 1, 6 * d)

    x2 = x.reshape(n, d)
    tgt = loss_target.reshape(n, d)
    (wf_in,) = _copy_wait("gather_in_wait", bufs_i, sems_i, plan_i, [mod3])
    (wf_in,) = _copy_blocking("gather_in_pass", [wf_in], 3, _gather_pass_plan(lays[0:1]))
    plan_o, plan_b, plan_p = _gather_direct_plan(lays[1:2]), _gather_chip_plan(lays[2:4]), _gather_pass_plan(lays[2:4])
    sems_o, bufs_o, tok_o = _copy_start("gather_out_start", [fulls[1]], 6, plan_o, [wf_in, mod3])
    sems_b, bufs_b, tok_b = _copy_start("gather_mlp_start", fulls[2:4], 6, plan_b, [tok_o])
    proj, h1, mixed, a1, q = _in_mix_fwd(x2, mod3, wf_in, dww, conf_dw_b, conf_ln_g, conf_ln_b, scw, t, tok_b)
    (wf_out,) = _copy_wait("gather_out_wait", bufs_o, sems_o, plan_o, [mixed])
    bufs_b = _copy_wait("gather_mlp_wait", bufs_b, sems_b, plan_b, [mixed])
    sems_p, bufs_p, tok_p = _copy_start("gather_pass_start", bufs_b, 6, plan_p)
    x1, y1, h2 = _out_proj(mixed, wf_out, x2, mod3, t, tok_p)
    wf_1, wf_2 = _copy_wait("gather_pass_wait", bufs_p, sems_p, plan_p, [h2])
    z, dx2, dy2, dg2, ggf, loss_p = _mlp_fwd(h2, x1, tgt, mod3, g_final.reshape(1, d), wf_1, wf_2, t)

    dz, dx1, dy1, dsh2, dsc2, dg1 = _mlp_bwd(dy2, z, x1, dx2, y1, mod3, wf_1, wf_2, t)
    g_w2 = _wgrad(z, dy2, "wgrad_mlp2", relu2=True)
    g_w1 = _wgrad(h2, dz, "wgrad_mlp1")
    made = {}

    def behind_pair_exchange(tok):
        made["g_wout"] = _wgrad(mixed, dy1, "wgrad_out", deps=[tok])
        return [made["g_wout"]]

    def behind_chip_exchange(tok):
        made["dproj"], made["sg"], made["grad_x"], made["dsh1"], made["dsc1"] = _mix_in_bwd(
            dy1, wf_out, a1, q, proj, dww, conf_ln_g, conf_ln_b, scw, wf_in, x2, dx1, mod3, t, tok)
        return [made["dproj"]]

    def behind_pair_share(tok):
        made["g_win"] = _wgrad(h1, made["dproj"], "wgrad_in", deps=[tok])
        return [made["g_win"]]

    gr_1, gr_2 = _reduce_scatter(ids, [g_w1, g_w2], lays[2:4], names[2:4], "m",
                                 (behind_pair_exchange, behind_chip_exchange, behind_pair_share))

    big = {}

    def behind_pair_exchange_in(tok):
        pack, dmod8 = _pack_small(made["sg"], ggf, loss_p, (made["dsh1"], made["dsc1"], dg1, dsh2, dsc2, dg2), d, tok)
        made["gathered"] = _all_gather8([pack, dmod8], "gather_small")
        return list(made["gathered"])

    def behind_chip_exchange_in(tok):
        red, dmod_all, g_bada = _small_reduce(*made["gathered"], nb, tok)
        dmod_cols = lax.dynamic_slice(dmod_all, (0, chip * ada_w), (N_DEV * nb, ada_w))
        made["ada"] = _adamw_ada(c_act.T, dmod_cols, w_ada[0], m_w_ada[0], v_w_ada[0])
        for nm_, g_, w_, m_, v_ in (("w_mlp1", gr_1, w_mlp1, m_w_mlp1, v_w_mlp1),
                                    ("w_mlp2", gr_2, w_mlp2, m_w_mlp2, v_w_mlp2)):
            big[nm_] = _adamw_big(w_[0], g_, m_[0], v_[0], "adamw_" + nm_)
        cw = dc // N_CHIPS
        g_dww = lax.dynamic_slice(red[0:CONF_K], (0, chip * cw), (CONF_K, cw))
        g_scw = lax.dynamic_slice(red[SG_SCW:SG_SCW + SC_K], (0, chip * cw), (SC_K, cw))
        g_gfin = jnp.concatenate([red[SG_N:SG_N + 1], red[SG_N + 1:SG_N + 2]], axis=1)
        made["small_g"] = [g_bada, g_dww, red[SG_DWB:SG_DWB + 1], red[SG_LNG:SG_LNG + 1], red[SG_LNB:SG_LNB + 1],
                           g_scw, g_gfin]
        small_w = [b_ada, conf_dw_w[0], conf_dw_b, conf_ln_g, conf_ln_b, sc_conv_w[0], g_final.reshape(1, d)]
        small_m = [m_b_ada, m_conf_dw_w[0], m_conf_dw_b, m_conf_ln_g, m_conf_ln_b, m_sc_conv_w[0],
                   m_g_final.reshape(1, d)]
        small_v = [v_b_ada, v_conf_dw_w[0], v_conf_dw_b, v_conf_ln_g, v_conf_ln_b, v_sc_conv_w[0],
                   v_g_final.reshape(1, d)]
        made["upd"] = _adamw_small(small_w, made["small_g"], small_m, small_v)
        made["loss"] = red[SG_N + 2, 0]
        return [big["w_mlp2"][0], made["upd"][0]]

    gr_in, gr_out = _reduce_scatter(ids, [made["g_win"], made["g_wout"]], lays[0:2], names[0:2], "i",
                                    (behind_pair_exchange_in, behind_chip_exchange_in, None))
    big["w_in"] = _adamw_big(w_in[0], gr_in, m_w_in[0], v_w_in[0], "adamw_w_in")
    big["w_out"] = _adamw_big(w_out[0], gr_out, m_w_out[0], v_w_out[0], "adamw_w_out")
    grad_x, small_g, upd, loss = made["grad_x"], made["small_g"], made["upd"], made["loss"]
    g_wada, d_wada, nm_wada, nv_wada = made["ada"]
    ns = len(small_g)
    s_delta, s_m, s_v = upd[0:ns], upd[ns:2 * ns], upd[2 * ns:3 * ns]

    def outs(kind_big, kind_small, wada):
        sm = kind_small
        return (wada[None], sm[0], kind_big["w_in"][None], sm[1][None], sm[2], sm[3], sm[4], sm[5][None],
                kind_big["w_out"][None], kind_big["w_mlp1"][None], kind_big["w_mlp2"][None], sm[6].reshape(d))

    grads_out = outs({"w_in": gr_in, "w_out": gr_out, "w_mlp1": gr_1, "w_mlp2": gr_2}, small_g, g_wada)
    delta_out = outs({k: v[0] for k, v in big.items()}, s_delta, d_wada)
    m_out = outs({k: v[1] for k, v in big.items()}, s_m, nm_wada)
    v_out = outs({k: v[2] for k, v in big.items()}, s_v, nv_wada)
    return (loss, grad_x.reshape(nb, t, d), *grads_out, *delta_out, *m_out, *v_out)
```

```python
import functools

import jax
import jax.numpy as jnp
from jax import lax
from jax.experimental import pallas as pl
from jax.experimental.pallas import tpu as pltpu

F32 = jnp.float32
BF16 = jnp.bfloat16
MESH = pl.DeviceIdType.MESH
HIGHEST = lax.Precision.HIGHEST

EPS = 1e-6
CONF_K = 31
SC_K = 3
HALO = 32
N_CHIPS = 4
N_DEV = 8

ADAM_LR = 0.001
ADAM_B1 = 0.9
ADAM_B2 = 0.999
ADAM_EPS = 1e-08
ADAM_WD = 0.01
ADAM_STEP = 10

TM_MM = 512
TM_MIX = 256
TM_MLP = 512
FF_CHUNK = 1024
MLP_SUB = 2
TK_WG = 2048
RB_CONV = 64
RB_WG = 32
CHIP_RELS = ((1, 0), (0, 1), (1, 1))

ANY = pl.BlockSpec(memory_space=pl.ANY)
VMEM = pl.BlockSpec(memory_space=pltpu.VMEM)
HBM = pl.BlockSpec(memory_space=pltpu.HBM)
SEM = pl.BlockSpec(memory_space=pltpu.SEMAPHORE)
EFFECT = pltpu.SideEffectType.DATAFLOW_SIDE_EFFECTING


def _me():
    return lax.axis_index("x"), lax.axis_index("y"), lax.axis_index("c")


def _flip(v, f):
    return 1 - v if f else v


def _rows8(v):
    r, c = v.shape
    return v.reshape(r // 8, 8, c).sum(axis=0)


def _rms(x):
    r = lax.rsqrt(jnp.mean(x * x, axis=-1, keepdims=True) + EPS)
    return x * r, r


def _rms_bwd(dxn, xn, r):
    return r * (dxn - xn * jnp.mean(dxn * xn, axis=-1, keepdims=True))


def _sigmoid(x):
    return 1.0 / (1.0 + jnp.exp(-x))


def _dot(a, b):
    return jnp.dot(a, b, preferred_element_type=F32)


def _dot_nt(a, b):
    return lax.dot_general(a, b, (((1,), (1,)), ((), ())), preferred_element_type=F32)


def _dot_tn(a, b):
    return lax.dot_general(a, b, (((0,), (0,)), ((), ())), preferred_element_type=F32)


def _const(shape):
    nd = len(shape)
    return pl.BlockSpec(shape, lambda i: (0,) * nd)


def _resident(shape):
    nd = len(shape)
    return pl.BlockSpec(shape, lambda i: (0,) * nd, pipeline_mode=pl.Buffered(1))


def _rowblk(tm, c):
    return pl.BlockSpec((tm, c), lambda i: (i, 0))


def _modspec(tps, width):
    return pl.BlockSpec((None, 1, width), lambda i: (i // tps, 0, 0))


def _accspec(tps, c):
    return pl.BlockSpec((None, 8, c), lambda i: (i // tps, 0, 0))


def _acc_add(ref, first, v):
    @pl.when(first)
    def _():
        ref[...] = v

    @pl.when(jnp.logical_not(first))
    def _():
        ref[...] += v


def _all_gather8(arrs, name):
    n = len(arrs)

    def body(*refs):
        ins, outs = refs[:n], refs[n:2 * n]
        send, recv = refs[2 * n:]
        x, y, c = _me()
        dev = 4 * x + 2 * y + c
        for a in range(n):
            outs[a][dev] = ins[a][...]
        sends = []
        for r in range(1, N_DEV):
            fx, fy, fc = (r >> 2) & 1, (r >> 1) & 1, r & 1
            peer = (_flip(x, fx), _flip(y, fy), _flip(c, fc))
            for a in range(n):
                cp = pltpu.make_async_remote_copy(
                    src_ref=ins[a], dst_ref=outs[a].at[dev],
                    send_sem=send.at[r - 1, a], recv_sem=recv.at[r - 1, a],
                    device_id=peer, device_id_type=MESH)
                cp.start()
                sends.append(cp)
        for r in range(1, N_DEV):
            fx, fy, fc = (r >> 2) & 1, (r >> 1) & 1, r & 1
            pdev = 4 * _flip(x, fx) + 2 * _flip(y, fy) + _flip(c, fc)
            for a in range(n):
                pltpu.make_async_remote_copy(
                    src_ref=ins[a], dst_ref=outs[a].at[pdev],
                    send_sem=send.at[r - 1, a], recv_sem=recv.at[r - 1, a],
                    device_id=(x, y, c), device_id_type=MESH).wait_recv()
        for cp in sends:
            cp.wait_send()

    return pl.pallas_call(
        body, name=name,
        out_shape=[jax.ShapeDtypeStruct((N_DEV,) + a.shape, a.dtype) for a in arrs],
        in_specs=[VMEM] * n, out_specs=[VMEM] * n,
        scratch_shapes=[pltpu.SemaphoreType.DMA((N_DEV - 1, n)),
                        pltpu.SemaphoreType.DMA((N_DEV - 1, n))],
    )(*arrs)


def _gather_mod(mod_shard):
    nb, w = mod_shard.shape

    def body(in_ref, out_ref, send, recv):
        x, y, c = _me()
        chip = 2 * x + y
        out_ref[:, pl.ds(pl.multiple_of(chip * w, 128), w)] = in_ref[...]
        sends = []
        for r, (fx, fy) in enumerate(CHIP_RELS):
            cp = pltpu.make_async_remote_copy(
                src_ref=in_ref,
                dst_ref=out_ref.at[:, pl.ds(pl.multiple_of(chip * w, 128), w)],
                send_sem=send.at[r], recv_sem=recv.at[r],
                device_id=(_flip(x, fx), _flip(y, fy), c), device_id_type=MESH)
            cp.start()
            sends.append(cp)
        for r, (fx, fy) in enumerate(CHIP_RELS):
            pchip = 2 * _flip(x, fx) + _flip(y, fy)
            pltpu.make_async_remote_copy(
                src_ref=in_ref,
                dst_ref=out_ref.at[:, pl.ds(pl.multiple_of(pchip * w, 128), w)],
                send_sem=send.at[r], recv_sem=recv.at[r],
                device_id=(x, y, c), device_id_type=MESH).wait_recv()
        for cp in sends:
            cp.wait_send()

    return pl.pallas_call(
        body, name="gather_mod",
        out_shape=jax.ShapeDtypeStruct((nb, N_CHIPS * w), mod_shard.dtype),
        in_specs=[VMEM], out_specs=VMEM,
        scratch_shapes=[pltpu.SemaphoreType.DMA((3,)), pltpu.SemaphoreType.DMA((3,))],
    )(mod_shard)


def _wlayout(d):
    d_in = 5 * d // 2
    return (
        (d, d_in // N_CHIPS, True),
        (d // N_CHIPS, d, False),
        (d, 4 * d // N_CHIPS, True),
        (4 * d // N_CHIPS, d, False),
    )


def _full_shape(lay):
    r, c, by_col = lay
    return (r, c * N_CHIPS) if by_col else (r * N_CHIPS, c)


def _full_view(ref, lay, k, h):
    r, c, by_col = lay
    hr = r // 2
    if by_col:
        return ref.at[pl.ds(pl.multiple_of(h * hr, 16), hr), pl.ds(pl.multiple_of(k * c, 128), c)]
    return ref.at[pl.ds(pl.multiple_of(k * r + h * hr, 16), hr), :]


def _half_view(ref, lay, h):
    hr = lay[0] // 2
    return ref.at[pl.ds(pl.multiple_of(h * hr, 16), hr), :]


def _half_shape(lay):
    return (lay[0] // 2, lay[1])


def _hbm(a):
    return pltpu.with_memory_space_constraint(a, pltpu.HBM)


def _remote(src, dst, send, recv, idx, to):
    return lambda: pltpu.make_async_remote_copy(src_ref=src, dst_ref=dst, send_sem=send.at[idx], recv_sem=recv.at[idx],
                                                device_id=to, device_id_type=MESH)


def _copy_start(name, bufs, n_sems, plan, after=()):
    nb, na = len(bufs), len(after)

    def body(*refs):
        sends, _ = plan(refs[:nb], refs[nb + na], refs[nb + na + 1])
        for mk in sends:
            mk().start()
        refs[-1][...] = jnp.zeros((8, 128), F32)

    outs = pl.pallas_call(
        body, name=name,
        out_shape=(pltpu.SemaphoreType.DMA((n_sems,)), pltpu.SemaphoreType.DMA((n_sems,)))
        + tuple(pltpu.HBM(b.shape, b.dtype) for b in bufs) + (jax.ShapeDtypeStruct((8, 128), F32),),
        in_specs=(HBM,) * nb + (ANY,) * na, out_specs=(SEM, SEM) + (HBM,) * nb + (VMEM,),
        input_output_aliases={i: 2 + i for i in range(nb)},
        compiler_params=pltpu.CompilerParams(has_side_effects=EFFECT),
    )(*[_hbm(b) for b in bufs], *after)
    return (outs[0], outs[1]), list(outs[2:2 + nb]), outs[-1]


def _copy_wait(name, bufs, sems, plan, after):
    nb, na = len(bufs), len(after)

    def body(*refs):
        sends, recvs = plan(refs[:nb], refs[nb], refs[nb + 1])
        for mk in sends:
            mk().wait_send()
        for mk in recvs:
            mk().wait_recv()

    outs = pl.pallas_call(
        body, name=name,
        out_shape=tuple(pltpu.HBM(b.shape, b.dtype) for b in bufs),
        in_specs=(HBM,) * nb + (SEM, SEM) + (ANY,) * na, out_specs=(HBM,) * nb,
        input_output_aliases={i: i for i in range(nb)},
        compiler_params=pltpu.CompilerParams(has_side_effects=EFFECT),
    )(*bufs, *sems, *after)
    return list(outs)


def _copy_blocking(name, bufs, n_sems, plan, after=()):
    nb, na = len(bufs), len(after)

    def body(*refs):
        sends, recvs = plan(refs[:nb], refs[2 * nb + na], refs[2 * nb + na + 1])
        started = [mk() for mk in sends]
        for cp in started:
            cp.start()
        for mk in recvs:
            mk().wait_recv()
        for cp in started:
            cp.wait_send()

    return list(pl.pallas_call(
        body, name=name,
        out_shape=tuple(jax.ShapeDtypeStruct(b.shape, b.dtype) for b in bufs),
        in_specs=(ANY,) * (nb + na), out_specs=(ANY,) * nb,
        input_output_aliases={i: i for i in range(nb)},
        scratch_shapes=[pltpu.SemaphoreType.DMA((n_sems,)), pltpu.SemaphoreType.DMA((n_sems,))],
    )(*bufs, *after))


def _exchange(name, bufs, n_sems, plan, between):
    if between is None:
        return _copy_blocking(name, bufs, n_sems, plan)
    sems, bufs, tok = _copy_start(name + "_start", bufs, n_sems, plan)
    return _copy_wait(name + "_wait", bufs, sems, plan, between(tok))


def _gather_direct_plan(lays):
    def plan(full, send, recv):
        x, y, c = _me()
        chip = 2 * x + y
        sends, recvs = [], []
        for r, (fx, fy) in enumerate(CHIP_RELS):
            px, py = _flip(x, fx), _flip(y, fy)
            for i, lay in enumerate(lays):
                for q in range(2):
                    oc = _flip(c, q)
                    mine = _full_view(full[i], lay, chip, c)
                    idx = (r * len(lays) + i) * 2 + q
                    sends.append(_remote(mine, mine, send, recv, idx, (px, py, oc)))
                    theirs = _full_view(full[i], lay, 2 * px + py, oc)
                    recvs.append(_remote(theirs, theirs, send, recv, idx, (x, y, c)))
        return sends, recvs
    return plan


def _gather_chip_plan(lays):
    def plan(full, send, recv):
        x, y, c = _me()
        chip = 2 * x + y
        sends, recvs = [], []
        for r, (fx, fy) in enumerate(CHIP_RELS):
            px, py = _flip(x, fx), _flip(y, fy)
            for i, lay in enumerate(lays):
                mine = _full_view(full[i], lay, chip, c)
                sends.append(_remote(mine, mine, send, recv, r * len(lays) + i, (px, py, c)))
                theirs = _full_view(full[i], lay, 2 * px + py, c)
                recvs.append(_remote(theirs, theirs, send, recv, r * len(lays) + i, (x, y, c)))
        return sends, recvs
    return plan


def _gather_pass_plan(lays):
    def plan(full, send, recv):
        x, y, c = _me()
        sends, recvs = [], []
        for r, (fx, fy) in enumerate(CHIP_RELS):
            pchip = 2 * _flip(x, fx) + _flip(y, fy)
            for i, lay in enumerate(lays):
                landed = _full_view(full[i], lay, pchip, c)
                sends.append(_remote(landed, landed, send, recv, r * len(lays) + i, (x, y, 1 - c)))
                other = _full_view(full[i], lay, pchip, 1 - c)
                recvs.append(_remote(other, other, send, recv, r * len(lays) + i, (x, y, c)))
        return sends, recvs
    return plan


def _gather8_plan(na):
    def plan(bufs, send, recv):
        x, y, c = _me()
        dev = 4 * x + 2 * y + c
        sends, recvs = [], []
        for r in range(1, N_DEV):
            fx, fy, fc = (r >> 2) & 1, (r >> 1) & 1, r & 1
            px, py, pc = _flip(x, fx), _flip(y, fy), _flip(c, fc)
            for a in range(na):
                idx = (r - 1) * na + a
                sends.append(_remote(bufs[a].at[dev], bufs[a].at[dev], send, recv, idx, (px, py, pc)))
                theirs = bufs[a].at[4 * px + 2 * py + pc]
                recvs.append(_remote(theirs, theirs, send, recv, idx, (x, y, c)))
        return sends, recvs
    return plan


def _pair_exchange_plan(lays):
    nw = len(lays)

    def plan(bufs, send, recv):
        x, y, c = _me()
        sends, recvs = [], []
        for i, lay in enumerate(lays):
            for k in range(N_CHIPS):
                sends.append(_remote(_full_view(bufs[i], lay, k, 1 - c), bufs[nw + i].at[k],
                                     send, recv, i * N_CHIPS + k, (x, y, 1 - c)))
                recvs.append(_remote(_full_view(bufs[i], lay, k, c), bufs[nw + i].at[k],
                                     send, recv, i * N_CHIPS + k, (x, y, c)))
        return sends, recvs
    return plan


def _chip_exchange_plan(nw):
    def plan(bufs, send, recv):
        x, y, c = _me()
        sends, recvs = [], []
        for r, (fx, fy) in enumerate(CHIP_RELS):
            px, py = _flip(x, fx), _flip(y, fy)
            for i in range(nw):
                sends.append(_remote(bufs[i].at[2 * px + py], bufs[nw + i].at[r], send, recv, r * nw + i, (px, py, c)))
                recvs.append(_remote(bufs[i].at[2 * px + py], bufs[nw + i].at[r], send, recv, r * nw + i, (x, y, c)))
        return sends, recvs
    return plan


def _pair_share_plan(lays):
    def plan(bufs, send, recv):
        x, y, c = _me()
        sends, recvs = [], []
        for i, lay in enumerate(lays):
            mine = _half_view(bufs[i], lay, c)
            sends.append(_remote(mine, mine, send, recv, i, (x, y, 1 - c)))
            other = _half_view(bufs[i], lay, 1 - c)
            recvs.append(_remote(other, other, send, recv, i, (x, y, c)))
        return sends, recvs
    return plan


def _pair_sum(ids, g, got, lay, name):
    r, c, by_col = lay
    hr = r // 2
    tr = min(hr, 256)
    nt = hr // tr

    def body(ids_ref, g_ref, got_ref, s32_ref, s16_ref):
        s = g_ref[...] + got_ref[...]
        s32_ref[...] = s
        s16_ref[...] = s.astype(BF16)

    if by_col:
        gspec = pl.BlockSpec((tr, c), lambda k, t, ids: (ids[1] * nt + t, k))
    else:
        gspec = pl.BlockSpec((tr, c), lambda k, t, ids: ((2 * k + ids[1]) * nt + t, 0))
    hspec = pl.BlockSpec((None, tr, c), lambda k, t, ids: (k, t, 0))
    return pl.pallas_call(
        body, name=name,
        grid_spec=pltpu.PrefetchScalarGridSpec(
            num_scalar_prefetch=1, grid=(N_CHIPS, nt),
            in_specs=[gspec, hspec], out_specs=[hspec, hspec]),
        out_shape=[jax.ShapeDtypeStruct((N_CHIPS, hr, c), F32),
                   jax.ShapeDtypeStruct((N_CHIPS, hr, c), BF16)],
    )(ids, g, got)


def _chip_sum(ids, s32, got, lay, name):
    hr, c = _half_shape(lay)
    tr = min(hr, 256)
    nt = hr // tr

    def body(ids_ref, s_ref, got_ref, out_ref):
        t = s_ref[...]
        for r in range(3):
            t = t + got_ref[r].astype(F32)
        out_ref[...] = t

    return pl.pallas_call(
        body, name=name,
        grid_spec=pltpu.PrefetchScalarGridSpec(
            num_scalar_prefetch=1, grid=(nt,),
            in_specs=[pl.BlockSpec((None, tr, c), lambda t, ids: (ids[0], t, 0)),
                      pl.BlockSpec((3, tr, c), lambda t, ids: (0, t, 0))],
            out_specs=pl.BlockSpec((tr, c), lambda t, ids: (ids[1] * nt + t, 0))),
        out_shape=jax.ShapeDtypeStruct((2 * hr, c), F32),
    )(ids, s32, got)


def _reduce_scatter(ids, grads, lays, names, tag, between):
    nw = len(lays)
    got1 = [lax.empty((N_CHIPS,) + _half_shape(l), F32) for l in lays]
    bufs = _exchange("pair_exchange_" + tag, list(grads) + got1, nw * N_CHIPS, _pair_exchange_plan(lays), between[0])
    sums = [_pair_sum(ids, bufs[i], bufs[nw + i], lays[i], "pair_sum_" + names[i]) for i in range(nw)]
    got2 = [lax.empty((3,) + _half_shape(l), BF16) for l in lays]
    bufs = _exchange("chip_exchange_" + tag, [s[1] for s in sums] + got2, 3 * nw, _chip_exchange_plan(nw), between[1])
    mine = [_chip_sum(ids, sums[i][0], bufs[nw + i], lays[i], "chip_sum_" + names[i]) for i in range(nw)]
    return _exchange("pair_share_" + tag, mine, nw, _pair_share_plan(lays), between[2])


def _cast_place(ids, ws, lays):
    nw = len(ws)

    def body(ids_ref, *refs):
        for i in range(nw):
            refs[nw + i][...] = refs[i][...].astype(BF16)

    by_col_map = lambda i, ids: (0, ids[0])
    by_row_map = lambda i, ids: (ids[0], 0)
    return pl.pallas_call(
        body, name="cast_place",
        grid_spec=pltpu.PrefetchScalarGridSpec(
            num_scalar_prefetch=1, grid=(1,),
            in_specs=[pl.BlockSpec(l[:2], lambda i, ids: (0, 0)) for l in lays],
            out_specs=[pl.BlockSpec(l[:2], by_col_map if l[2] else by_row_map) for l in lays]),
        out_shape=[jax.ShapeDtypeStruct(_full_shape(l), BF16) for l in lays],
    )(ids, *ws)


def _ada_mod(c_all, w_ada, b_ada):
    def body(c_ref, w_ref, b_ref, act_ref, mod_ref):
        cv = c_ref[...]
        act = cv * _sigmoid(cv)
        act_ref[...] = act
        mod_ref[...] = jnp.dot(act, w_ref[...], preferred_element_type=F32, precision=HIGHEST) + b_ref[...]

    nb = c_all.shape[0]
    return pl.pallas_call(
        body, name="ada_mod",
        out_shape=[jax.ShapeDtypeStruct(c_all.shape, F32),
                   jax.ShapeDtypeStruct((nb, w_ada.shape[1]), F32)],
        in_specs=[VMEM] * 3, out_specs=[VMEM] * 2,
    )(c_all, w_ada, b_ada)


def _run_units(*unit_lists):
    total = max(len(u) for u in unit_lists)
    done = [0] * len(unit_lists)
    for step in range(1, total + 1):
        for li, units in enumerate(unit_lists):
            upto = (step * len(units) + total - 1) // total
            while done[li] < upto:
                units[done[li]]()
                done[li] += 1


def _in_mix_fwd(x, mod3, w_in, wb, dwb, lng, lnb, scw, t, dep):
    n, d = x.shape
    d_in = w_in.shape[1]
    dc = d_in // 5
    tm = min(TM_MIX, t)
    tps = t // tm
    nt = n // tm
    rb = min(RB_CONV, tm)
    ncol = 256
    ng = dc // 128
    assert tps % 2 == 0 and nt % 2 == 0

    def proj_units(x_ref, mod_ref, w_ref, p_ref, h_ref):
        def head():
            xn, _ = _rms(x_ref[...])
            h_ref[...] = (xn * (1.0 + mod_ref[:, d:2 * d]) + mod_ref[:, 0:d]).astype(BF16)
        units = [head]
        for c0 in range(0, d_in, ncol):
            def chunk(c0=c0):
                p_ref[:, c0:c0 + ncol] = _dot(h_ref[...], w_ref[:, c0:c0 + ncol])
            units.append(chunk)
        return units

    def mix_units(first, p_ref, h_ref, r0, wb_ref, dwb_ref, lng_ref, lnb_ref, scw_ref,
                  proj_ref, h1_ref, mixed_ref, a1_ref, q_ref, ext_ref, e_ref, extp_ref, a1s_ref):
        rows = slice(r0, r0 + tm)
        units = []

        def glu():
            halo = ext_ref[tm:tm + HALO, :]
            ext_ref[0:HALO, :] = halo if first is False else jnp.where(first, 0.0, halo)
            ext_ref[HALO:HALO + tm, :] = p_ref[:, 0:dc] * _sigmoid(p_ref[:, dc:2 * dc])
        units.append(glu)
        for g in range(ng):
            def shift(g=g):
                for r in range(8):
                    e_ref[g, r, 0:tm + HALO, :] = ext_ref[r:r + tm + HALO, 128 * g:128 * g + 128]
            units.append(shift)
        for g in range(ng):
            lanes = slice(128 * g, 128 * g + 128)
            for i0 in range(0, tm, rb):
                def conv(g=g, lanes=lanes, i0=i0):
                    acc = jnp.zeros((rb, 128), F32)
                    for k in range(CONF_K):
                        m, r = divmod(k + HALO - CONF_K + 1, 8)
                        acc = acc + e_ref[g, r, i0 + 8 * m:i0 + 8 * m + rb, :] * wb_ref[k:k + 1, lanes]
                    a1s_ref[i0:i0 + rb, lanes] = acc + dwb_ref[:, lanes]
                units.append(conv)

        def norm():
            a1 = a1s_ref[...]
            a1_ref[rows, :] = a1
            mu = jnp.mean(a1, axis=-1, keepdims=True)
            ac = a1 - mu
            rstd = lax.rsqrt(jnp.mean(ac * ac, axis=-1, keepdims=True) + EPS)
            a2 = ac * rstd * lng_ref[...] + lnb_ref[...]
            mixed_ref[rows, 0:dc] = (a2 * _sigmoid(a2)).astype(BF16)
        units.append(norm)

        def short():
            halo = extp_ref[tm:tm + 8, :]
            extp_ref[0:8, :] = halo if first is False else jnp.where(first, 0.0, halo)
            extp_ref[8:8 + tm, :] = p_ref[:, 3 * dc:4 * dc] * p_ref[:, 4 * dc:5 * dc]
            q = jnp.zeros((tm, dc), F32)
            for k in range(SC_K):
                q = q + extp_ref[6 + k:6 + k + tm, :] * scw_ref[k:k + 1, :]
            q_ref[rows, :] = q
            mixed_ref[rows, dc:2 * dc] = (p_ref[:, 2 * dc:3 * dc] * q).astype(BF16)
        units.append(short)

        def keep():
            proj_ref[rows, :] = p_ref[...]
            h1_ref[rows, :] = h_ref[...]
        units.append(keep)
        return units

    def body(x0_ref, xa_ref, xb_ref, mod0_ref, moda_ref, modb_ref, w_ref,
             wb_ref, dwb_ref, lng_ref, lnb_ref, scw_ref, dep_ref,
             proj_ref, h1_ref, mixed_ref, a1_ref, q_ref,
             p0_ref, p1_ref, h0_ref, hh1_ref, ext_ref, e_ref, extp_ref, a1s_ref):
        j = pl.program_id(0)

        @pl.when(j == 0)
        def _():
            ext_ref[...] = jnp.zeros(ext_ref.shape, F32)
            extp_ref[...] = jnp.zeros(extp_ref.shape, F32)
            _run_units(proj_units(x0_ref, mod0_ref, w_ref, p0_ref, h0_ref))

        common = (wb_ref, dwb_ref, lng_ref, lnb_ref, scw_ref, proj_ref, h1_ref, mixed_ref, a1_ref, q_ref,
                  ext_ref, e_ref, extp_ref, a1s_ref)
        _run_units(mix_units((2 * j) % tps == 0, p0_ref, h0_ref, 0, *common),
                   proj_units(xa_ref, moda_ref, w_ref, p1_ref, hh1_ref))
        _run_units(mix_units(False, p1_ref, hh1_ref, tm, *common),
                   proj_units(xb_ref, modb_ref, w_ref, p0_ref, h0_ref))

    last = nt - 1
    xspec = lambda f: pl.BlockSpec((tm, d), lambda j: (f(j), 0))
    mspec = lambda f: pl.BlockSpec((None, 1, 6 * d), lambda j: (f(j) // tps, 0, 0))
    out2 = lambda c: pl.BlockSpec((2 * tm, c), lambda j: (j, 0))
    return pl.pallas_call(
        body, name="in_mix_fwd", grid=(nt // 2,),
        in_specs=[xspec(lambda j: 0), xspec(lambda j: 2 * j + 1), xspec(lambda j: jnp.minimum(2 * j + 2, last)),
                  mspec(lambda j: 0), mspec(lambda j: 2 * j + 1), mspec(lambda j: jnp.minimum(2 * j + 2, last)),
                  _resident((d, d_in)),
                  _const(wb.shape), _const(dwb.shape), _const(lng.shape), _const(lnb.shape), _const(scw.shape), ANY],
        out_specs=[out2(d_in), out2(d), out2(2 * dc), out2(dc), out2(dc)],
        out_shape=[jax.ShapeDtypeStruct((n, d_in), F32), jax.ShapeDtypeStruct((n, d), BF16),
                   jax.ShapeDtypeStruct((n, 2 * dc), BF16), jax.ShapeDtypeStruct((n, dc), F32),
                   jax.ShapeDtypeStruct((n, dc), F32)],
        scratch_shapes=[pltpu.VMEM((tm, d_in), F32), pltpu.VMEM((tm, d_in), F32),
                        pltpu.VMEM((tm, d), BF16), pltpu.VMEM((tm, d), BF16),
                        pltpu.VMEM((tm + HALO + 8, dc), F32), _shift_scratch(tm, dc),
                        pltpu.VMEM((tm + 8, dc), F32), pltpu.VMEM((tm, dc), F32)],
        compiler_params=pltpu.CompilerParams(dimension_semantics=("arbitrary",)),
    )(x, x, x, mod3, mod3, mod3, w_in, wb, dwb, lng, lnb, scw, dep)


def _shift_scratch(tm, dc):
    return pltpu.VMEM((dc // 128, 8, tm + HALO + 8, 128), F32)


def _out_proj(mixed, w_out, x, mod3, t, dep):
    n, d = x.shape
    tm = min(TM_MM, t)
    tps = t // tm

    def body(m_ref, w_ref, x_ref, mod_ref, dep_ref, x1_ref, y1_ref, h2_ref):
        y1 = _dot(m_ref[...], w_ref[...])
        y1_ref[...] = y1.astype(BF16)
        x1 = x_ref[...] + mod_ref[:, 2 * d:3 * d] * y1
        x1_ref[...] = x1
        xn, _ = _rms(x1)
        h2_ref[...] = (xn * (1.0 + mod_ref[:, 4 * d:5 * d]) + mod_ref[:, 3 * d:4 * d]).astype(BF16)

    return pl.pallas_call(
        body, name="out_proj", grid=(n // tm,),
        in_specs=[_rowblk(tm, d), _resident((d, d)), _rowblk(tm, d), _modspec(tps, 6 * d), ANY],
        out_specs=[_rowblk(tm, d), _rowblk(tm, d), _rowblk(tm, d)],
        out_shape=[jax.ShapeDtypeStruct((n, d), F32), jax.ShapeDtypeStruct((n, d), BF16),
                   jax.ShapeDtypeStruct((n, d), BF16)],
        compiler_params=pltpu.CompilerParams(dimension_semantics=("parallel",)),
    )(mixed, w_out, x, mod3, dep)


def _mlp_fwd(h2, x1, tgt, mod3, gfin, w1, w2, t):
    n, d = x1.shape
    dff = w1.shape[1]
    tm = min(TM_MLP, t)
    tps = t // tm
    nt = n // tm

    def body(h_ref, x1_ref, tg_ref, mod_ref, gf_ref, w1_ref, w2_ref,
             z_ref, dx2_ref, dy2_ref, dg2_ref, ggf_ref, loss_ref):
        i = pl.program_id(0)
        g2 = mod_ref[:, 5 * d:6 * d]
        gf = gf_ref[...]
        sub = tm // MLP_SUB
        sums = None
        for part in range(MLP_SUB):
            rs = slice(part * sub, (part + 1) * sub)
            hv = h_ref[rs, :]
            y2 = jnp.zeros((sub, d), F32)
            for j in range(dff // FF_CHUNK):
                cols = slice(j * FF_CHUNK, (j + 1) * FF_CHUNK)
                z = _dot(hv, w1_ref[:, cols])
                z_ref[rs, cols] = z.astype(BF16)
                zr = jnp.maximum(z, 0.0)
                y2 = y2 + _dot((zr * zr).astype(BF16), w2_ref[cols, :])
            x2n, r3 = _rms(x1_ref[rs, :] + g2 * y2)
            diff = x2n * gf - tg_ref[rs, :]
            dout = diff * (1.0 / d)
            dx2 = _rms_bwd(dout * gf, x2n, r3)
            dx2_ref[rs, :] = dx2
            dy2_ref[rs, :] = (g2 * dx2).astype(BF16)
            p = (_rows8(dx2 * y2), _rows8(dout * x2n), _rows8(diff * diff))
            sums = p if sums is None else tuple(a + b for a, b in zip(sums, p))
        _acc_add(dg2_ref, i % tps == 0, sums[0])
        _acc_add(ggf_ref, i == 0, sums[1])
        _acc_add(loss_ref, i == 0, sums[2])

    return pl.pallas_call(
        body, name="mlp_fwd", grid=(nt,),
        in_specs=[_rowblk(tm, d), _rowblk(tm, d), _rowblk(tm, d), _modspec(tps, 6 * d), _const((1, d)),
                  _resident((d, dff)), _resident((dff, d))],
        out_specs=[_rowblk(tm, dff), _rowblk(tm, d), _rowblk(tm, d), _accspec(tps, d),
                   _const((8, d)), _const((8, d))],
        out_shape=[jax.ShapeDtypeStruct((n, dff), BF16), jax.ShapeDtypeStruct((n, d), F32),
                   jax.ShapeDtypeStruct((n, d), BF16), jax.ShapeDtypeStruct((n // t, 8, d), F32),
                   jax.ShapeDtypeStruct((8, d), F32), jax.ShapeDtypeStruct((8, d), F32)],
        compiler_params=pltpu.CompilerParams(dimension_semantics=("arbitrary",)),
    )(h2, x1, tgt, mod3, gfin, w1, w2)


def _mlp_bwd(dy2, z, x1, dx2, y1, mod3, w1, w2, t):
    n, d = x1.shape
    dff = w1.shape[1]
    tm = min(TM_MLP, t)
    tps = t // tm

    def body(dy2_ref, z_ref, x1_ref, dx2_ref, y1_ref, mod_ref, w1_ref, w2_ref,
             dz_ref, dx1_ref, dy1_ref, dsh_ref, dsc_ref, dg1_ref):
        first = pl.program_id(0) % tps == 0
        sub = tm // MLP_SUB
        sums = None
        for part in range(MLP_SUB):
            rs = slice(part * sub, (part + 1) * sub)
            dy2 = dy2_ref[rs, :]
            dh2 = jnp.zeros((sub, d), F32)
            for j in range(dff // FF_CHUNK):
                cols = slice(j * FF_CHUNK, (j + 1) * FF_CHUNK)
                du = _dot_nt(dy2, w2_ref[cols, :])
                dz = (du * (2.0 * jnp.maximum(z_ref[rs, cols].astype(F32), 0.0))).astype(BF16)
                dz_ref[rs, cols] = dz
                dh2 = dh2 + _dot_nt(dz, w1_ref[:, cols])
            x1n, r2 = _rms(x1_ref[rs, :])
            dx1 = dx2_ref[rs, :] + _rms_bwd(dh2 * (1.0 + mod_ref[:, 4 * d:5 * d]), x1n, r2)
            dx1_ref[rs, :] = dx1
            dy1_ref[rs, :] = (mod_ref[:, 2 * d:3 * d] * dx1).astype(BF16)
            p = (_rows8(dh2), _rows8(dh2 * x1n), _rows8(dx1 * y1_ref[rs, :].astype(F32)))
            sums = p if sums is None else tuple(a + b for a, b in zip(sums, p))
        _acc_add(dsh_ref, first, sums[0])
        _acc_add(dsc_ref, first, sums[1])
        _acc_add(dg1_ref, first, sums[2])

    acc = jax.ShapeDtypeStruct((n // t, 8, d), F32)
    return pl.pallas_call(
        body, name="mlp_bwd", grid=(n // tm,),
        in_specs=[_rowblk(tm, d), _rowblk(tm, dff), _rowblk(tm, d), _rowblk(tm, d), _rowblk(tm, d),
                  _modspec(tps, 6 * d), _resident((d, dff)), _resident((dff, d))],
        out_specs=[_rowblk(tm, dff), _rowblk(tm, d), _rowblk(tm, d),
                   _accspec(tps, d), _accspec(tps, d), _accspec(tps, d)],
        out_shape=[jax.ShapeDtypeStruct((n, dff), BF16), jax.ShapeDtypeStruct((n, d), F32),
                   jax.ShapeDtypeStruct((n, d), BF16), acc, acc, acc],
        compiler_params=pltpu.CompilerParams(dimension_semantics=("arbitrary",)),
    )(dy2, z, x1, dx2, y1, mod3, w1, w2)


def _wgrad(a, b, name, relu2=False, bn=None, deps=()):
    n, ka = a.shape
    nb = b.shape[1]
    tk = min(TK_WG, n)
    bm = min(ka, 1024)
    if bn is None:
        bn = nb if nb <= 2048 else nb // 2

    def body(a_ref, b_ref, *rest):
        out_ref = rest[-1]
        av = a_ref[...]
        if relu2:
            ar = jnp.maximum(av, 0.0)
            av = ar * ar
        p = _dot_tn(av, b_ref[...])
        _acc_add(out_ref, pl.program_id(2) == 0, p)

    return pl.pallas_call(
        body, name=name, grid=(ka // bm, nb // bn, n // tk),
        in_specs=[pl.BlockSpec((tk, bm), lambda i, j, k: (k, i)),
                  pl.BlockSpec((tk, bn), lambda i, j, k: (k, j))] + [ANY] * len(deps),
        out_specs=pl.BlockSpec((bm, bn), lambda i, j, k: (i, j)),
        out_shape=jax.ShapeDtypeStruct((ka, nb), F32),
        compiler_params=pltpu.CompilerParams(dimension_semantics=("parallel", "parallel", "arbitrary")),
    )(a, b, *deps)


SG_DWW = 0
SG_DWB = CONF_K
SG_LNG = CONF_K + 1
SG_LNB = CONF_K + 2
SG_SCW = CONF_K + 3
SG_N = CONF_K + 3 + SC_K


def _mix_in_bwd(dy1, w_out, a1, q, proj, wb, lng, lnb, scw, w_in, x, dx1, mod3, t, dep):
    n, d_in = proj.shape
    d = x.shape[1]
    dc = d_in // 5
    tm = min(TM_MIX, t)
    tps = t // tm
    nt = n // tm
    rw = min(RB_WG, tm)
    ng = dc // 128
    ncol = 256

    def ln_bwd(a1v, da3, lng_v, lnb_v):
        mu = jnp.mean(a1v, axis=-1, keepdims=True)
        ac = a1v - mu
        rstd = lax.rsqrt(jnp.mean(ac * ac, axis=-1, keepdims=True) + EPS)
        ah = ac * rstd
        a2 = ah * lng_v + lnb_v
        s2 = _sigmoid(a2)
        da2 = da3 * (s2 * (1.0 + a2 * (1.0 - s2)))
        dah = da2 * lng_v
        da1 = rstd * (dah - jnp.mean(dah, axis=-1, keepdims=True)
                      - ah * jnp.mean(dah * ah, axis=-1, keepdims=True))
        return da1, da2, ah

    def dmixed_units(dy_ref, wo_ref, dm_ref):
        units = []
        for c0 in range(0, 2 * dc, ncol):
            def chunk(c0=c0):
                dm_ref[:, c0:c0 + ncol] = _dot_nt(dy_ref[...], wo_ref[c0:c0 + ncol, :])
            units.append(chunk)
        return units

    def mix_units(k, dm_ref, a1_ref, q_ref, p_ref, wb_ref, lng_ref, lnb_ref, scw_ref,
                  dproj_ref, sg_ref, extd_ref, ed_ref, a0_ref, da0_ref, extq_ref, cda_ref, cdq_ref):
        keep_next = jnp.where(k % tps == tps - 1, 0.0, 1.0)
        units = []

        def head():
            lng_v, lnb_v = lng_ref[...], lnb_ref[...]
            da1, da2, ah = ln_bwd(a1_ref[...], dm_ref[:, 0:dc], lng_v, lnb_v)
            sg_ref[8 * SG_LNG:8 * SG_LNG + 8, :] += _rows8(da2 * ah)
            sg_ref[8 * SG_LNB:8 * SG_LNB + 8, :] += _rows8(da2)
            sg_ref[8 * SG_DWB:8 * SG_DWB + 8, :] += _rows8(da1)
            a0_ref[...] = p_ref[:, 0:dc] * _sigmoid(p_ref[:, dc:2 * dc])
            extd_ref[0:tm, :] = da1
            extd_ref[tm:tm + HALO, :] = cda_ref[...] * keep_next
            extd_ref[tm + HALO:tm + HALO + 8, :] = jnp.zeros((8, dc), F32)
            cda_ref[...] = da1[0:HALO, :]
            ds = dm_ref[:, dc:2 * dc]
            dproj_ref[:, 2 * dc:3 * dc] = (ds * q_ref[...]).astype(BF16)
            dq = ds * p_ref[:, 2 * dc:3 * dc]
            extq_ref[0:tm, :] = dq
            extq_ref[tm:tm + 8, :] = cdq_ref[...] * keep_next
            cdq_ref[...] = dq[0:8, :]
        units.append(head)
        for g in range(ng):
            def shift(g=g):
                for r in range(8):
                    ed_ref[g, r, 0:tm + HALO, :] = extd_ref[r:r + tm + HALO, 128 * g:128 * g + 128]
            units.append(shift)
        for g in range(ng):
            lanes = slice(128 * g, 128 * g + 128)
            accs = [None] * CONF_K
            for i0 in range(0, tm, rw):
                def conv(g=g, lanes=lanes, i0=i0, accs=accs):
                    a0v = a0_ref[i0:i0 + rw, lanes]
                    acc = jnp.zeros((rw, 128), F32)
                    for s in range(CONF_K):
                        m, r = divmod(s, 8)
                        e = ed_ref[g, r, i0 + 8 * m:i0 + 8 * m + rw, :]
                        acc = acc + e * wb_ref[CONF_K - 1 - s:CONF_K - s, lanes]
                        part = _rows8(e * a0v)
                        accs[s] = part if accs[s] is None else accs[s] + part
                    da0_ref[i0:i0 + rw, lanes] = acc
                units.append(conv)

            def flush(lanes=lanes, accs=accs):
                for s in range(CONF_K):
                    kk = CONF_K - 1 - s
                    sg_ref[8 * (SG_DWW + kk):8 * (SG_DWW + kk) + 8, lanes] += accs[s]
            units.append(flush)

        def emit():
            da0 = da0_ref[...]
            sig = _sigmoid(p_ref[:, dc:2 * dc])
            dproj_ref[:, 0:dc] = (da0 * sig).astype(BF16)
            dproj_ref[:, dc:2 * dc] = (da0 * a0_ref[...] * (1.0 - sig)).astype(BF16)
        units.append(emit)

        def short():
            scc, sch = p_ref[:, 3 * dc:4 * dc], p_ref[:, 4 * dc:5 * dc]
            pv = scc * sch
            dp = jnp.zeros((tm, dc), F32)
            for kk in range(SC_K):
                dqs = extq_ref[SC_K - 1 - kk:SC_K - 1 - kk + tm, :]
                dp = dp + dqs * scw_ref[kk:kk + 1, :]
                sg_ref[8 * (SG_SCW + kk):8 * (SG_SCW + kk) + 8, :] += _rows8(pv * dqs)
            dproj_ref[:, 3 * dc:4 * dc] = (dp * sch).astype(BF16)
            dproj_ref[:, 4 * dc:5 * dc] = (dp * scc).astype(BF16)
        units.append(short)
        return units

    def proj_units(first, dp_ref, w_ref, x_ref, dx1_ref, mod_ref, dh_ref, gx_ref, dsh_ref, dsc_ref):
        units = []
        for c0 in range(0, d, ncol):
            def chunk(c0=c0):
                dh_ref[:, c0:c0 + ncol] = _dot_nt(dp_ref[...], w_ref[c0:c0 + ncol, :])
            units.append(chunk)

        def tail():
            dh1 = dh_ref[...]
            xn, r1 = _rms(x_ref[...])
            v1, v2 = _rows8(dh1), _rows8(dh1 * xn)
            dsh_ref[...] = jnp.where(first, v1, dsh_ref[...] + v1)
            dsc_ref[...] = jnp.where(first, v2, dsc_ref[...] + v2)
            gx_ref[...] = dx1_ref[...] + _rms_bwd(dh1 * (1.0 + mod_ref[:, d:2 * d]), xn, r1)
        units.append(tail)
        return units

    def body(dyl_ref, dy_ref, wo_ref, a1_ref, q_ref, p_ref, wb_ref, lng_ref, lnb_ref, scw_ref,
             w_ref, x_ref, dx1_ref, mod_ref, dep_ref,
             dproj_ref, sg_ref, gx_ref, dsh_ref, dsc_ref,
             extd_ref, ed_ref, a0_ref, da0_ref, extq_ref, dp_ref, dh_ref, dm_ref, cda_ref, cdq_ref):
        k = pl.program_id(0)

        @pl.when(k == 0)
        def _():
            sg_ref[...] = jnp.zeros(sg_ref.shape, F32)
            dp_ref[...] = jnp.zeros(dp_ref.shape, BF16)
            cda_ref[...] = jnp.zeros(cda_ref.shape, F32)
            cdq_ref[...] = jnp.zeros(cdq_ref.shape, F32)
            _run_units(dmixed_units(dyl_ref, wo_ref, dm_ref))

        first = jnp.logical_or(k <= 1, (nt - k) % tps == tps - 1)
        after = lambda: proj_units(first, dp_ref, w_ref, x_ref, dx1_ref, mod_ref, dh_ref, gx_ref, dsh_ref, dsc_ref)

        @pl.when(k < nt)
        def _():
            _run_units(mix_units(nt - 1 - k, dm_ref, a1_ref, q_ref, p_ref, wb_ref, lng_ref, lnb_ref, scw_ref,
                                 dproj_ref, sg_ref, extd_ref, ed_ref, a0_ref, da0_ref, extq_ref, cda_ref, cdq_ref),
                       after(), dmixed_units(dy_ref, wo_ref, dm_ref))
            dp_ref[...] = dproj_ref[...]

        @pl.when(k == nt)
        def _():
            _run_units(after())

    cur = lambda k: jnp.maximum(nt - 1 - k, 0)
    prev = lambda k: jnp.minimum(nt - k, nt - 1)
    acc = jax.ShapeDtypeStruct((n // t, 8, d), F32)
    return pl.pallas_call(
        body, name="mix_in_bwd", grid=(nt + 1,),
        in_specs=[pl.BlockSpec((tm, d), lambda k: (nt - 1, 0)),
                  pl.BlockSpec((tm, d), lambda k: (jnp.maximum(nt - 2 - k, 0), 0)), _resident(w_out.shape),
                  pl.BlockSpec((tm, dc), lambda k: (cur(k), 0)),
                  pl.BlockSpec((tm, dc), lambda k: (cur(k), 0)),
                  pl.BlockSpec((tm, d_in), lambda k: (cur(k), 0)),
                  _const(wb.shape), _const(lng.shape), _const(lnb.shape), _const(scw.shape),
                  _resident((d, d_in)),
                  pl.BlockSpec((tm, d), lambda k: (prev(k), 0)), pl.BlockSpec((tm, d), lambda k: (prev(k), 0)),
                  pl.BlockSpec((None, 1, 6 * d), lambda k: (prev(k) // tps, 0, 0)), ANY],
        out_specs=[pl.BlockSpec((tm, d_in), lambda k: (cur(k), 0)), _const((8 * SG_N, dc)),
                   pl.BlockSpec((tm, d), lambda k: (prev(k), 0)),
                   pl.BlockSpec((None, 8, d), lambda k: (prev(k) // tps, 0, 0)),
                   pl.BlockSpec((None, 8, d), lambda k: (prev(k) // tps, 0, 0))],
        out_shape=[jax.ShapeDtypeStruct((n, d_in), BF16), jax.ShapeDtypeStruct((8 * SG_N, dc), F32),
                   jax.ShapeDtypeStruct((n, d), F32), acc, acc],
        scratch_shapes=[pltpu.VMEM((tm + HALO + 8, dc), F32), _shift_scratch(tm, dc),
                        pltpu.VMEM((tm, dc), F32), pltpu.VMEM((tm, dc), F32),
                        pltpu.VMEM((tm + 8, dc), F32),
                        pltpu.VMEM((tm, d_in), BF16), pltpu.VMEM((tm, d), F32), pltpu.VMEM((tm, 2 * dc), F32),
                        pltpu.VMEM((HALO, dc), F32), pltpu.VMEM((8, dc), F32)],
        compiler_params=pltpu.CompilerParams(dimension_semantics=("arbitrary",)),
    )(dy1, dy1, w_out, a1, q, proj, wb, lng, lnb, scw, w_in, x, dx1, mod3, dep)


SMALL_ROWS = 40


def _pack_small(ids, sg, ggf, loss, accs, d, dep):
    dc = d // 2
    nb = accs[0].shape[0]

    def body(ids_ref, sg_ref, ggf_ref, loss_ref, dsh1, dsc1, dg1, dsh2, dsc2, dg2, dep_ref, pack_ref, dmod_ref):
        pack_ref[...] = jnp.zeros(pack_ref.shape, F32)
        for k in range(SG_N):
            pack_ref[k:k + 1, :] = jnp.sum(sg_ref[8 * k:8 * k + 8, :], axis=0, keepdims=True)
        gf = jnp.sum(ggf_ref[...], axis=0, keepdims=True)
        pack_ref[SG_N:SG_N + 1, :] = gf[:, 0:dc]
        pack_ref[SG_N + 1:SG_N + 2, :] = gf[:, dc:d]
        tot = jnp.sum(jnp.sum(loss_ref[...], axis=0, keepdims=True), axis=1, keepdims=True) * (0.5 / d)
        pack_ref[SG_N + 2:SG_N + 3, :] = jnp.broadcast_to(tot, (1, dc))
        for b in range(nb):
            row = jnp.concatenate([jnp.sum(ref[b], axis=0, keepdims=True)
                                   for ref in (dsh1, dsc1, dg1, dsh2, dsc2, dg2)], axis=1)
            for f in range(fold):
                dmod_ref[b * fold + f:b * fold + f + 1, :] = row[:, f * wf:(f + 1) * wf]

    fold = 8 // nb
    wf = 6 * d // fold
    assert nb * fold == 8 and wf % 128 == 0
    whole = lambda a: pl.BlockSpec(a.shape, lambda i, ids: (0,) * a.ndim)
    mine = lambda r, c: pl.BlockSpec((None, r, c), lambda i, ids: (ids[2], 0, 0))
    return pl.pallas_call(
        body, name="pack_small",
        grid_spec=pltpu.PrefetchScalarGridSpec(
            num_scalar_prefetch=1, grid=(1,),
            in_specs=[whole(a) for a in (sg, ggf, loss, *accs)] + [ANY],
            out_specs=[mine(SMALL_ROWS, dc), mine(8, wf)]),
        out_shape=[jax.ShapeDtypeStruct((N_DEV, SMALL_ROWS, dc), F32), jax.ShapeDtypeStruct((N_DEV, 8, wf), F32)],
    )(ids, sg, ggf, loss, *accs, dep)


def _small_reduce(pack_all, dmod_all, nb, dep):
    def body(pk_ref, dm_ref, dep_ref, red_ref, dmod_ref, gb_ref):
        tot = pk_ref[0]
        for dev in range(1, N_DEV):
            tot = tot + pk_ref[dev]
        red_ref[...] = tot
        for f in range(fold):
            gb = jnp.zeros((1, wf), F32)
            for dev in range(N_DEV):
                for b in range(nb):
                    seg = dm_ref[dev, b * fold + f:b * fold + f + 1, :]
                    dmod_ref[dev * nb + b:dev * nb + b + 1, f * wf:(f + 1) * wf] = seg
                    gb = gb + seg
            gb_ref[:, f * wf:(f + 1) * wf] = gb

    fold = 8 // nb
    wf = dmod_all.shape[2]
    return pl.pallas_call(
        body, name="small_reduce",
        out_shape=[jax.ShapeDtypeStruct(pack_all.shape[1:], F32),
                   jax.ShapeDtypeStruct((N_DEV * nb, fold * wf), F32),
                   jax.ShapeDtypeStruct((1, fold * wf), F32)],
        in_specs=[VMEM] * 2 + [ANY], out_specs=[VMEM] * 3,
    )(pack_all, dmod_all, dep)


def _adam(w, g, m, v):
    m = ADAM_B1 * m + (1.0 - ADAM_B1) * g
    v = ADAM_B2 * v + (1.0 - ADAM_B2) * (g * g)
    m_hat = m / (1.0 - ADAM_B1 ** ADAM_STEP)
    v_hat = v / (1.0 - ADAM_B2 ** ADAM_STEP)
    delta = -ADAM_LR * (m_hat / (jnp.sqrt(v_hat) + ADAM_EPS) + ADAM_WD * w)
    return delta, m, v


def _adamw_big(w, g, m, v, name):
    r, c = w.shape
    tr = min(r, 256)

    def body(w_ref, g_ref, m_ref, v_ref, d_ref, nm_ref, nv_ref):
        d_ref[...], nm_ref[...], nv_ref[...] = _adam(w_ref[...], g_ref[...], m_ref[...], v_ref[...])

    s = jax.ShapeDtypeStruct((r, c), F32)
    return pl.pallas_call(
        body, name=name, grid=(r // tr,),
        in_specs=[_rowblk(tr, c)] * 4, out_specs=[_rowblk(tr, c)] * 3, out_shape=[s, s, s],
        compiler_params=pltpu.CompilerParams(dimension_semantics=("parallel",)),
    )(w, g, m, v)


def _adamw_ada(act_t, dmod_cols, w, m, v):
    r, c = w.shape
    tr = min(r, 256)
    nb = act_t.shape[1]

    def body(a_ref, dm_ref, w_ref, m_ref, v_ref, g_ref, d_ref, nm_ref, nv_ref):
        g = jnp.dot(a_ref[...], dm_ref[...], preferred_element_type=F32, precision=HIGHEST)
        g_ref[...] = g
        d_ref[...], nm_ref[...], nv_ref[...] = _adam(w_ref[...], g, m_ref[...], v_ref[...])

    s = jax.ShapeDtypeStruct((r, c), F32)
    return pl.pallas_call(
        body, name="adamw_w_ada", grid=(r // tr,),
        in_specs=[_rowblk(tr, nb), _const((nb, c))] + [_rowblk(tr, c)] * 3,
        out_specs=[_rowblk(tr, c)] * 4, out_shape=[s, s, s, s],
        compiler_params=pltpu.CompilerParams(dimension_semantics=("parallel",)),
    )(act_t, dmod_cols, w, m, v)


def _adamw_small(ws, gs, ms, vs):
    n = len(ws)

    def body(*refs):
        for i in range(n):
            w, g, m, v = (refs[j * n + i][...] for j in range(4))
            dl, nm, nv = _adam(w, g, m, v)
            refs[4 * n + i][...] = dl
            refs[5 * n + i][...] = nm
            refs[6 * n + i][...] = nv

    shapes = [jax.ShapeDtypeStruct(w.shape, F32) for w in ws]
    return pl.pallas_call(
        body, name="adamw_small", out_shape=shapes * 3,
        in_specs=[VMEM] * (4 * n), out_specs=[VMEM] * (3 * n),
    )(*ws, *gs, *ms, *vs)


def kernel(x, c, w_ada, b_ada, w_in, conf_dw_w, conf_dw_b, conf_ln_g, conf_ln_b, sc_conv_w, w_out, w_mlp1, w_mlp2, g_final, loss_target, m_w_ada, m_b_ada, m_w_in, m_conf_dw_w, m_conf_dw_b, m_conf_ln_g, m_conf_ln_b, m_sc_conv_w, m_w_out, m_w_mlp1, m_w_mlp2, m_g_final, v_w_ada, v_b_ada, v_w_in, v_conf_dw_w, v_conf_dw_b, v_conf_ln_g, v_conf_ln_b, v_sc_conv_w, v_w_out, v_w_mlp1, v_w_mlp2, v_g_final):
    nb, t, d = x.shape
    n = nb * t
    dc = d // 2
    ada_w = w_ada.shape[2]
    ax, ay, ac = _me()
    chip = 2 * ax + ay
    dev = 2 * chip + ac
    ids = jnp.stack([chip, ac, dev]).astype(jnp.int32)

    lays = _wlayout(d)
    names = ("in", "out", "mlp1", "mlp2")
    fulls = _cast_place(ids, [w_in[0], w_out[0], w_mlp1[0], w_mlp2[0]], lays)

    c_pad = jnp.zeros((8, d), F32).at[0:nb].set(c)
    cw_pad = jnp.zeros((SMALL_ROWS, dc // N_CHIPS), F32)
    cw_pad = cw_pad.at[0:CONF_K].set(conf_dw_w[0]).at[HALO:HALO + SC_K].set(sc_conv_w[0])
    c_all8, cw_all8 = _all_gather8([c_pad, cw_pad], "gather_c")
    plan_i = _gather_chip_plan(lays[0:1])
    sems_i, bufs_i, tok_i = _copy_start("gather_in_start", [fulls[0]], 3, plan_i, [c_all8])
    c_all = c_all8[:, 0:nb].reshape(N_DEV * nb, d) + tok_i[0, 0]
    cw_full = jnp.concatenate([cw_all8[2 * k] for k in range(N_CHIPS)], axis=1)
    dww, scw = cw_full[0:CONF_K], cw_full[HALO:HALO + SC_K]
    b_cols = lax.dynamic_slice(b_ada, (0, chip * ada_w), (1, ada_w))
    c_act, mod_shard = _ada_mod(c_all, w_ada[0], b_cols)
    mod_all = _gather_mod(mod_shard)
    mod3 = lax.dynamic_slice(mod_all, (dev * nb, 0), (nb, 6 * d)).reshape(nb, 1, 6 * d)

    x2 = x.reshape(n, d)
    tgt = loss_target.reshape(n, d)
    (wf_in,) = _copy_wait("gather_in_wait", bufs_i, sems_i, plan_i, [mod3])
    (wf_in,) = _copy_blocking("gather_in_pass", [wf_in], 3, _gather_pass_plan(lays[0:1]))
    plan_o, plan_b, plan_p = _gather_direct_plan(lays[1:2]), _gather_chip_plan(lays[2:4]), _gather_pass_plan(lays[2:4])
    sems_o, bufs_o, tok_o = _copy_start("gather_out_start", [fulls[1]], 6, plan_o, [wf_in, mod3])
    sems_b, bufs_b, tok_b = _copy_start("gather_mlp_start", fulls[2:4], 6, plan_b, [tok_o])
    proj, h1, mixed, a1, q = _in_mix_fwd(x2, mod3, wf_in, dww, conf_dw_b, conf_ln_g, conf_ln_b, scw, t, tok_b)
    (wf_out,) = _copy_wait("gather_out_wait", bufs_o, sems_o, plan_o, [mixed])
    bufs_b = _copy_wait("gather_mlp_wait", bufs_b, sems_b, plan_b, [mixed])
    sems_p, bufs_p, tok_p = _copy_start("gather_pass_start", bufs_b, 6, plan_p)
    x1, y1, h2 = _out_proj(mixed, wf_out, x2, mod3, t, tok_p)
    wf_1, wf_2 = _copy_wait("gather_pass_wait", bufs_p, sems_p, plan_p, [h2])
    z, dx2, dy2, dg2, ggf, loss_p = _mlp_fwd(h2, x1, tgt, mod3, g_final.reshape(1, d), wf_1, wf_2, t)

    dz, dx1, dy1, dsh2, dsc2, dg1 = _mlp_bwd(dy2, z, x1, dx2, y1, mod3, wf_1, wf_2, t)
    g_w2 = _wgrad(z, dy2, "wgrad_mlp2", relu2=True)
    g_w1 = _wgrad(h2, dz, "wgrad_mlp1")
    made = {}

    def behind_pair_exchange(tok):
        made["g_wout"] = _wgrad(mixed, dy1, "wgrad_out", deps=[tok])
        return [made["g_wout"]]

    def behind_chip_exchange(tok):
        made["dproj"], made["sg"], made["grad_x"], made["dsh1"], made["dsc1"] = _mix_in_bwd(
            dy1, wf_out, a1, q, proj, dww, conf_ln_g, conf_ln_b, scw, wf_in, x2, dx1, mod3, t, tok)
        return [made["dproj"]]

    def behind_pair_share(tok):
        made["g_win"] = _wgrad(h1, made["dproj"], "wgrad_in", deps=[tok])
        return [made["g_win"]]

    gr_1, gr_2 = _reduce_scatter(ids, [g_w1, g_w2], lays[2:4], names[2:4], "m",
                                 (behind_pair_exchange, behind_chip_exchange, behind_pair_share))

    big = {}

    plan_g = _gather8_plan(2)

    def behind_pair_exchange_in(tok):
        packs = _pack_small(ids, made["sg"], ggf, loss_p, (made["dsh1"], made["dsc1"], dg1, dsh2, dsc2, dg2), d, tok)
        made["gather"] = _copy_start("gather_small_start", list(packs), 2 * (N_DEV - 1), plan_g)
        return [made["gather"][2]]

    def behind_chip_exchange_in(tok):
        sems_g, bufs_g, _ = made["gather"]
        gathered = _copy_wait("gather_small_wait", bufs_g, sems_g, plan_g, [tok])
        red, dmod_all, g_bada = _small_reduce(*gathered, nb, tok)
        dmod_cols = lax.dynamic_slice(dmod_all, (0, chip * ada_w), (N_DEV * nb, ada_w))
        made["ada"] = _adamw_ada(c_act.T, dmod_cols, w_ada[0], m_w_ada[0], v_w_ada[0])
        for nm_, g_, w_, m_, v_ in (("w_mlp1", gr_1, w_mlp1, m_w_mlp1, v_w_mlp1),
                                    ("w_mlp2", gr_2, w_mlp2, m_w_mlp2, v_w_mlp2)):
            big[nm_] = _adamw_big(w_[0], g_, m_[0], v_[0], "adamw_" + nm_)
        cw = dc // N_CHIPS
        g_dww = lax.dynamic_slice(red[0:CONF_K], (0, chip * cw), (CONF_K, cw))
        g_scw = lax.dynamic_slice(red[SG_SCW:SG_SCW + SC_K], (0, chip * cw), (SC_K, cw))
        g_gfin = jnp.concatenate([red[SG_N:SG_N + 1], red[SG_N + 1:SG_N + 2]], axis=1)
        made["small_g"] = [g_bada, g_dww, red[SG_DWB:SG_DWB + 1], red[SG_LNG:SG_LNG + 1], red[SG_LNB:SG_LNB + 1],
                           g_scw, g_gfin]
        small_w = [b_ada, conf_dw_w[0], conf_dw_b, conf_ln_g, conf_ln_b, sc_conv_w[0], g_final.reshape(1, d)]
        small_m = [m_b_ada, m_conf_dw_w[0], m_conf_dw_b, m_conf_ln_g, m_conf_ln_b, m_sc_conv_w[0],
                   m_g_final.reshape(1, d)]
        small_v = [v_b_ada, v_conf_dw_w[0], v_conf_dw_b, v_conf_ln_g, v_conf_ln_b, v_sc_conv_w[0],
                   v_g_final.reshape(1, d)]
        made["upd"] = _adamw_small(small_w, made["small_g"], small_m, small_v)
        made["loss"] = red[SG_N + 2, 0]
        return [big["w_mlp2"][0], made["upd"][0]]

    gr_in, gr_out = _reduce_scatter(ids, [made["g_win"], made["g_wout"]], lays[0:2], names[0:2], "i",
                                    (behind_pair_exchange_in, behind_chip_exchange_in, None))
    big["w_in"] = _adamw_big(w_in[0], gr_in, m_w_in[0], v_w_in[0], "adamw_w_in")
    big["w_out"] = _adamw_big(w_out[0], gr_out, m_w_out[0], v_w_out[0], "adamw_w_out")
    grad_x, small_g, upd, loss = made["grad_x"], made["small_g"], made["upd"], made["loss"]
    g_wada, d_wada, nm_wada, nv_wada = made["ada"]
    ns = len(small_g)
    s_delta, s_m, s_v = upd[0:ns], upd[ns:2 * ns], upd[2 * ns:3 * ns]

    def outs(kind_big, kind_small, wada):
        sm = kind_small
        return (wada[None], sm[0], kind_big["w_in"][None], sm[1][None], sm[2], sm[3], sm[4], sm[5][None],
                kind_big["w_out"][None], kind_big["w_mlp1"][None], kind_big["w_mlp2"][None], sm[6].reshape(d))

    grads_out = outs({"w_in": gr_in, "w_out": gr_out, "w_mlp1": gr_1, "w_mlp2": gr_2}, small_g, g_wada)
    delta_out = outs({k: v[0] for k, v in big.items()}, s_delta, d_wada)
    m_out = outs({k: v[1] for k, v in big.items()}, s_m, nm_wada)
    v_out = outs({k: v[2] for k, v in big.items()}, s_v, nv_wada)
    return (loss, grad_x.reshape(nb, t, d), *grads_out, *delta_out, *m_out, *v_out)
```

```python
import functools

import jax
import jax.numpy as jnp
from jax import lax
from jax.experimental import pallas as pl
from jax.experimental.pallas import tpu as pltpu

F32 = jnp.float32
BF16 = jnp.bfloat16
MESH = pl.DeviceIdType.MESH
HIGHEST = lax.Precision.HIGHEST

EPS = 1e-6
CONF_K = 31
SC_K = 3
HALO = 32
N_CHIPS = 4
N_DEV = 8

ADAM_LR = 0.001
ADAM_B1 = 0.9
ADAM_B2 = 0.999
ADAM_EPS = 1e-08
ADAM_WD = 0.01
ADAM_STEP = 10

TM_MM = 1024
TM_MIX = 256
TM_MLP = 512
FF_CHUNK = 1024
MLP_SUB = 2
TK_WG = 2048
RB_CONV = 64
RB_WG = 32
CHIP_RELS = ((1, 0), (0, 1), (1, 1))

ANY = pl.BlockSpec(memory_space=pl.ANY)
VMEM = pl.BlockSpec(memory_space=pltpu.VMEM)
HBM = pl.BlockSpec(memory_space=pltpu.HBM)
SEM = pl.BlockSpec(memory_space=pltpu.SEMAPHORE)
EFFECT = pltpu.SideEffectType.DATAFLOW_SIDE_EFFECTING


def _me():
    return lax.axis_index("x"), lax.axis_index("y"), lax.axis_index("c")


def _flip(v, f):
    return 1 - v if f else v


def _rows8(v):
    r, c = v.shape
    return v.reshape(r // 8, 8, c).sum(axis=0)


def _rms(x):
    r = lax.rsqrt(jnp.mean(x * x, axis=-1, keepdims=True) + EPS)
    return x * r, r


def _rms_bwd(dxn, xn, r):
    return r * (dxn - xn * jnp.mean(dxn * xn, axis=-1, keepdims=True))


def _sigmoid(x):
    return 1.0 / (1.0 + jnp.exp(-x))


def _dot(a, b):
    return jnp.dot(a, b, preferred_element_type=F32)


def _dot_nt(a, b):
    return lax.dot_general(a, b, (((1,), (1,)), ((), ())), preferred_element_type=F32)


def _dot_tn(a, b):
    return lax.dot_general(a, b, (((0,), (0,)), ((), ())), preferred_element_type=F32)


def _const(shape):
    nd = len(shape)
    return pl.BlockSpec(shape, lambda i: (0,) * nd)


def _resident(shape):
    nd = len(shape)
    return pl.BlockSpec(shape, lambda i: (0,) * nd, pipeline_mode=pl.Buffered(1))


def _rowblk(tm, c):
    return pl.BlockSpec((tm, c), lambda i: (i, 0))


def _modspec(tps, width):
    return pl.BlockSpec((None, 1, width), lambda i: (i // tps, 0, 0))


def _accspec(tps, c):
    return pl.BlockSpec((None, 8, c), lambda i: (i // tps, 0, 0))


def _acc_add(ref, first, v):
    @pl.when(first)
    def _():
        ref[...] = v

    @pl.when(jnp.logical_not(first))
    def _():
        ref[...] += v


def _all_gather8(arrs, name):
    n = len(arrs)

    def body(*refs):
        ins, outs = refs[:n], refs[n:2 * n]
        send, recv = refs[2 * n:]
        x, y, c = _me()
        dev = 4 * x + 2 * y + c
        for a in range(n):
            outs[a][dev] = ins[a][...]
        sends = []
        for r in range(1, N_DEV):
            fx, fy, fc = (r >> 2) & 1, (r >> 1) & 1, r & 1
            peer = (_flip(x, fx), _flip(y, fy), _flip(c, fc))
            for a in range(n):
                cp = pltpu.make_async_remote_copy(
                    src_ref=ins[a], dst_ref=outs[a].at[dev],
                    send_sem=send.at[r - 1, a], recv_sem=recv.at[r - 1, a],
                    device_id=peer, device_id_type=MESH)
                cp.start()
                sends.append(cp)
        for r in range(1, N_DEV):
            fx, fy, fc = (r >> 2) & 1, (r >> 1) & 1, r & 1
            pdev = 4 * _flip(x, fx) + 2 * _flip(y, fy) + _flip(c, fc)
            for a in range(n):
                pltpu.make_async_remote_copy(
                    src_ref=ins[a], dst_ref=outs[a].at[pdev],
                    send_sem=send.at[r - 1, a], recv_sem=recv.at[r - 1, a],
                    device_id=(x, y, c), device_id_type=MESH).wait_recv()
        for cp in sends:
            cp.wait_send()

    return pl.pallas_call(
        body, name=name,
        out_shape=[jax.ShapeDtypeStruct((N_DEV,) + a.shape, a.dtype) for a in arrs],
        in_specs=[VMEM] * n, out_specs=[VMEM] * n,
        scratch_shapes=[pltpu.SemaphoreType.DMA((N_DEV - 1, n)),
                        pltpu.SemaphoreType.DMA((N_DEV - 1, n))],
    )(*arrs)


def _gather_mod(mod_shard):
    nb, w = mod_shard.shape

    def body(in_ref, out_ref, send, recv):
        x, y, c = _me()
        chip = 2 * x + y
        out_ref[:, pl.ds(pl.multiple_of(chip * w, 128), w)] = in_ref[...]
        sends = []
        for r, (fx, fy) in enumerate(CHIP_RELS):
            cp = pltpu.make_async_remote_copy(
                src_ref=in_ref,
                dst_ref=out_ref.at[:, pl.ds(pl.multiple_of(chip * w, 128), w)],
                send_sem=send.at[r], recv_sem=recv.at[r],
                device_id=(_flip(x, fx), _flip(y, fy), c), device_id_type=MESH)
            cp.start()
            sends.append(cp)
        for r, (fx, fy) in enumerate(CHIP_RELS):
            pchip = 2 * _flip(x, fx) + _flip(y, fy)
            pltpu.make_async_remote_copy(
                src_ref=in_ref,
                dst_ref=out_ref.at[:, pl.ds(pl.multiple_of(pchip * w, 128), w)],
                send_sem=send.at[r], recv_sem=recv.at[r],
                device_id=(x, y, c), device_id_type=MESH).wait_recv()
        for cp in sends:
            cp.wait_send()

    return pl.pallas_call(
        body, name="gather_mod",
        out_shape=jax.ShapeDtypeStruct((nb, N_CHIPS * w), mod_shard.dtype),
        in_specs=[VMEM], out_specs=VMEM,
        scratch_shapes=[pltpu.SemaphoreType.DMA((3,)), pltpu.SemaphoreType.DMA((3,))],
    )(mod_shard)


def _wlayout(d):
    d_in = 5 * d // 2
    return (
        (d, d_in // N_CHIPS, True),
        (d // N_CHIPS, d, False),
        (d, 4 * d // N_CHIPS, True),
        (4 * d // N_CHIPS, d, False),
    )


def _full_shape(lay):
    r, c, by_col = lay
    return (r, c * N_CHIPS) if by_col else (r * N_CHIPS, c)


def _full_view(ref, lay, k, h):
    r, c, by_col = lay
    hr = r // 2
    if by_col:
        return ref.at[pl.ds(pl.multiple_of(h * hr, 16), hr), pl.ds(pl.multiple_of(k * c, 128), c)]
    return ref.at[pl.ds(pl.multiple_of(k * r + h * hr, 16), hr), :]


def _half_view(ref, lay, h):
    hr = lay[0] // 2
    return ref.at[pl.ds(pl.multiple_of(h * hr, 16), hr), :]


def _half_shape(lay):
    return (lay[0] // 2, lay[1])


def _hbm(a):
    return pltpu.with_memory_space_constraint(a, pltpu.HBM)


def _remote(src, dst, send, recv, idx, to):
    return lambda: pltpu.make_async_remote_copy(src_ref=src, dst_ref=dst, send_sem=send.at[idx], recv_sem=recv.at[idx],
                                                device_id=to, device_id_type=MESH)


def _copy_start(name, bufs, n_sems, plan, after=()):
    nb, na = len(bufs), len(after)

    def body(*refs):
        sends, _ = plan(refs[:nb], refs[nb + na], refs[nb + na + 1])
        for mk in sends:
            mk().start()
        refs[-1][...] = jnp.zeros((8, 128), F32)

    outs = pl.pallas_call(
        body, name=name,
        out_shape=(pltpu.SemaphoreType.DMA((n_sems,)), pltpu.SemaphoreType.DMA((n_sems,)))
        + tuple(pltpu.HBM(b.shape, b.dtype) for b in bufs) + (jax.ShapeDtypeStruct((8, 128), F32),),
        in_specs=(HBM,) * nb + (ANY,) * na, out_specs=(SEM, SEM) + (HBM,) * nb + (VMEM,),
        input_output_aliases={i: 2 + i for i in range(nb)},
        compiler_params=pltpu.CompilerParams(has_side_effects=EFFECT),
    )(*[_hbm(b) for b in bufs], *after)
    return (outs[0], outs[1]), list(outs[2:2 + nb]), outs[-1]


def _copy_wait(name, bufs, sems, plan, after):
    nb, na = len(bufs), len(after)

    def body(*refs):
        sends, recvs = plan(refs[:nb], refs[nb], refs[nb + 1])
        for mk in sends:
            mk().wait_send()
        for mk in recvs:
            mk().wait_recv()

    outs = pl.pallas_call(
        body, name=name,
        out_shape=tuple(pltpu.HBM(b.shape, b.dtype) for b in bufs),
        in_specs=(HBM,) * nb + (SEM, SEM) + (ANY,) * na, out_specs=(HBM,) * nb,
        input_output_aliases={i: i for i in range(nb)},
        compiler_params=pltpu.CompilerParams(has_side_effects=EFFECT),
    )(*bufs, *sems, *after)
    return list(outs)


def _copy_blocking(name, bufs, n_sems, plan, after=()):
    nb, na = len(bufs), len(after)

    def body(*refs):
        sends, recvs = plan(refs[:nb], refs[2 * nb + na], refs[2 * nb + na + 1])
        started = [mk() for mk in sends]
        for cp in started:
            cp.start()
        for mk in recvs:
            mk().wait_recv()
        for cp in started:
            cp.wait_send()

    return list(pl.pallas_call(
        body, name=name,
        out_shape=tuple(jax.ShapeDtypeStruct(b.shape, b.dtype) for b in bufs),
        in_specs=(ANY,) * (nb + na), out_specs=(ANY,) * nb,
        input_output_aliases={i: i for i in range(nb)},
        scratch_shapes=[pltpu.SemaphoreType.DMA((n_sems,)), pltpu.SemaphoreType.DMA((n_sems,))],
    )(*bufs, *after))


def _exchange(name, bufs, n_sems, plan, between):
    if between is None:
        return _copy_blocking(name, bufs, n_sems, plan)
    sems, bufs, tok = _copy_start(name + "_start", bufs, n_sems, plan)
    return _copy_wait(name + "_wait", bufs, sems, plan, between(tok))


def _gather_direct_plan(lays):
    def plan(full, send, recv):
        x, y, c = _me()
        chip = 2 * x + y
        sends, recvs = [], []
        for r, (fx, fy) in enumerate(CHIP_RELS):
            px, py = _flip(x, fx), _flip(y, fy)
            for i, lay in enumerate(lays):
                for q in range(2):
                    oc = _flip(c, q)
                    mine = _full_view(full[i], lay, chip, c)
                    idx = (r * len(lays) + i) * 2 + q
                    sends.append(_remote(mine, mine, send, recv, idx, (px, py, oc)))
                    theirs = _full_view(full[i], lay, 2 * px + py, oc)
                    recvs.append(_remote(theirs, theirs, send, recv, idx, (x, y, c)))
        return sends, recvs
    return plan


def _gather_chip_plan(lays):
    def plan(full, send, recv):
        x, y, c = _me()
        chip = 2 * x + y
        sends, recvs = [], []
        for r, (fx, fy) in enumerate(CHIP_RELS):
            px, py = _flip(x, fx), _flip(y, fy)
            for i, lay in enumerate(lays):
                mine = _full_view(full[i], lay, chip, c)
                sends.append(_remote(mine, mine, send, recv, r * len(lays) + i, (px, py, c)))
                theirs = _full_view(full[i], lay, 2 * px + py, c)
                recvs.append(_remote(theirs, theirs, send, recv, r * len(lays) + i, (x, y, c)))
        return sends, recvs
    return plan


def _gather_pass_plan(lays):
    def plan(full, send, recv):
        x, y, c = _me()
        sends, recvs = [], []
        for r, (fx, fy) in enumerate(CHIP_RELS):
            pchip = 2 * _flip(x, fx) + _flip(y, fy)
            for i, lay in enumerate(lays):
                landed = _full_view(full[i], lay, pchip, c)
                sends.append(_remote(landed, landed, send, recv, r * len(lays) + i, (x, y, 1 - c)))
                other = _full_view(full[i], lay, pchip, 1 - c)
                recvs.append(_remote(other, other, send, recv, r * len(lays) + i, (x, y, c)))
        return sends, recvs
    return plan


def _gather8_plan(na):
    def plan(bufs, send, recv):
        x, y, c = _me()
        dev = 4 * x + 2 * y + c
        sends, recvs = [], []
        for r in range(1, N_DEV):
            fx, fy, fc = (r >> 2) & 1, (r >> 1) & 1, r & 1
            px, py, pc = _flip(x, fx), _flip(y, fy), _flip(c, fc)
            for a in range(na):
                idx = (r - 1) * na + a
                sends.append(_remote(bufs[a].at[dev], bufs[a].at[dev], send, recv, idx, (px, py, pc)))
                theirs = bufs[a].at[4 * px + 2 * py + pc]
                recvs.append(_remote(theirs, theirs, send, recv, idx, (x, y, c)))
        return sends, recvs
    return plan


def _pair_exchange_plan(lays):
    nw = len(lays)

    def plan(bufs, send, recv):
        x, y, c = _me()
        sends, recvs = [], []
        for i, lay in enumerate(lays):
            for k in range(N_CHIPS):
                sends.append(_remote(_full_view(bufs[i], lay, k, 1 - c), bufs[nw + i].at[k],
                                     send, recv, i * N_CHIPS + k, (x, y, 1 - c)))
                recvs.append(_remote(_full_view(bufs[i], lay, k, c), bufs[nw + i].at[k],
                                     send, recv, i * N_CHIPS + k, (x, y, c)))
        return sends, recvs
    return plan


def _chip_exchange_plan(nw):
    def plan(bufs, send, recv):
        x, y, c = _me()
        sends, recvs = [], []
        for r, (fx, fy) in enumerate(CHIP_RELS):
            px, py = _flip(x, fx), _flip(y, fy)
            for i in range(nw):
                sends.append(_remote(bufs[i].at[2 * px + py], bufs[nw + i].at[r], send, recv, r * nw + i, (px, py, c)))
                recvs.append(_remote(bufs[i].at[2 * px + py], bufs[nw + i].at[r], send, recv, r * nw + i, (x, y, c)))
        return sends, recvs
    return plan


def _pair_share_plan(lays):
    def plan(bufs, send, recv):
        x, y, c = _me()
        sends, recvs = [], []
        for i, lay in enumerate(lays):
            mine = _half_view(bufs[i], lay, c)
            sends.append(_remote(mine, mine, send, recv, i, (x, y, 1 - c)))
            other = _half_view(bufs[i], lay, 1 - c)
            recvs.append(_remote(other, other, send, recv, i, (x, y, c)))
        return sends, recvs
    return plan


def _pair_sum(ids, g, got, lay, name):
    r, c, by_col = lay
    hr = r // 2
    tr = min(hr, 256)
    nt = hr // tr

    def body(ids_ref, g_ref, got_ref, s32_ref, s16_ref):
        s = g_ref[...] + got_ref[...]
        s32_ref[...] = s
        s16_ref[...] = s.astype(BF16)

    if by_col:
        gspec = pl.BlockSpec((tr, c), lambda k, t, ids: (ids[1] * nt + t, k))
    else:
        gspec = pl.BlockSpec((tr, c), lambda k, t, ids: ((2 * k + ids[1]) * nt + t, 0))
    hspec = pl.BlockSpec((None, tr, c), lambda k, t, ids: (k, t, 0))
    return pl.pallas_call(
        body, name=name,
        grid_spec=pltpu.PrefetchScalarGridSpec(
            num_scalar_prefetch=1, grid=(N_CHIPS, nt),
            in_specs=[gspec, hspec], out_specs=[hspec, hspec]),
        out_shape=[jax.ShapeDtypeStruct((N_CHIPS, hr, c), F32),
                   jax.ShapeDtypeStruct((N_CHIPS, hr, c), BF16)],
    )(ids, g, got)


def _chip_sum(ids, s32, got, lay, name):
    hr, c = _half_shape(lay)
    tr = min(hr, 256)
    nt = hr // tr

    def body(ids_ref, s_ref, got_ref, out_ref):
        t = s_ref[...]
        for r in range(3):
            t = t + got_ref[r].astype(F32)
        out_ref[...] = t

    return pl.pallas_call(
        body, name=name,
        grid_spec=pltpu.PrefetchScalarGridSpec(
            num_scalar_prefetch=1, grid=(nt,),
            in_specs=[pl.BlockSpec((None, tr, c), lambda t, ids: (ids[0], t, 0)),
                      pl.BlockSpec((3, tr, c), lambda t, ids: (0, t, 0))],
            out_specs=pl.BlockSpec((tr, c), lambda t, ids: (ids[1] * nt + t, 0))),
        out_shape=jax.ShapeDtypeStruct((2 * hr, c), F32),
    )(ids, s32, got)


def _reduce_scatter(ids, grads, lays, names, tag, between):
    nw = len(lays)
    got1 = [lax.empty((N_CHIPS,) + _half_shape(l), F32) for l in lays]
    bufs = _exchange("pair_exchange_" + tag, list(grads) + got1, nw * N_CHIPS, _pair_exchange_plan(lays), between[0])
    sums = [_pair_sum(ids, bufs[i], bufs[nw + i], lays[i], "pair_sum_" + names[i]) for i in range(nw)]
    got2 = [lax.empty((3,) + _half_shape(l), BF16) for l in lays]
    bufs = _exchange("chip_exchange_" + tag, [s[1] for s in sums] + got2, 3 * nw, _chip_exchange_plan(nw), between[1])
    mine = [_chip_sum(ids, sums[i][0], bufs[nw + i], lays[i], "chip_sum_" + names[i]) for i in range(nw)]
    return _exchange("pair_share_" + tag, mine, nw, _pair_share_plan(lays), between[2])


def _cast_place(ids, ws, lays):
    nw = len(ws)

    def body(ids_ref, *refs):
        for i in range(nw):
            refs[nw + i][...] = refs[i][...].astype(BF16)

    by_col_map = lambda i, ids: (0, ids[0])
    by_row_map = lambda i, ids: (ids[0], 0)
    return pl.pallas_call(
        body, name="cast_place",
        grid_spec=pltpu.PrefetchScalarGridSpec(
            num_scalar_prefetch=1, grid=(1,),
            in_specs=[pl.BlockSpec(l[:2], lambda i, ids: (0, 0)) for l in lays],
            out_specs=[pl.BlockSpec(l[:2], by_col_map if l[2] else by_row_map) for l in lays]),
        out_shape=[jax.ShapeDtypeStruct(_full_shape(l), BF16) for l in lays],
    )(ids, *ws)


def _ada_mod(c_all, w_ada, b_ada):
    def body(c_ref, w_ref, b_ref, act_ref, mod_ref):
        cv = c_ref[...]
        act = cv * _sigmoid(cv)
        act_ref[...] = act
        mod_ref[...] = jnp.dot(act, w_ref[...], preferred_element_type=F32, precision=HIGHEST) + b_ref[...]

    nb = c_all.shape[0]
    return pl.pallas_call(
        body, name="ada_mod",
        out_shape=[jax.ShapeDtypeStruct(c_all.shape, F32),
                   jax.ShapeDtypeStruct((nb, w_ada.shape[1]), F32)],
        in_specs=[VMEM] * 3, out_specs=[VMEM] * 2,
    )(c_all, w_ada, b_ada)


def _run_units(*unit_lists):
    total = max(len(u) for u in unit_lists)
    done = [0] * len(unit_lists)
    for step in range(1, total + 1):
        for li, units in enumerate(unit_lists):
            upto = (step * len(units) + total - 1) // total
            while done[li] < upto:
                units[done[li]]()
                done[li] += 1


def _in_mix_fwd(x, mod3, w_in, wb, dwb, lng, lnb, scw, t, dep):
    n, d = x.shape
    d_in = w_in.shape[1]
    dc = d_in // 5
    tm = min(TM_MIX, t)
    tps = t // tm
    nt = n // tm
    rb = min(RB_CONV, tm)
    ncol = 256
    ng = dc // 128
    assert tps % 2 == 0 and nt % 2 == 0

    def proj_units(x_ref, mod_ref, w_ref, p_ref, h_ref):
        def head():
            xn, _ = _rms(x_ref[...])
            h_ref[...] = (xn * (1.0 + mod_ref[:, d:2 * d]) + mod_ref[:, 0:d]).astype(BF16)
        units = [head]
        for c0 in range(0, d_in, ncol):
            def chunk(c0=c0):
                p_ref[:, c0:c0 + ncol] = _dot(h_ref[...], w_ref[:, c0:c0 + ncol])
            units.append(chunk)
        return units

    def mix_units(first, p_ref, h_ref, r0, wb_ref, dwb_ref, lng_ref, lnb_ref, scw_ref,
                  proj_ref, h1_ref, mixed_ref, a1_ref, q_ref, ext_ref, e_ref, extp_ref, a1s_ref):
        rows = slice(r0, r0 + tm)
        units = []

        def glu():
            halo = ext_ref[tm:tm + HALO, :]
            ext_ref[0:HALO, :] = halo if first is False else jnp.where(first, 0.0, halo)
            ext_ref[HALO:HALO + tm, :] = p_ref[:, 0:dc] * _sigmoid(p_ref[:, dc:2 * dc])
        units.append(glu)
        for g in range(ng):
            def shift(g=g):
                for r in range(8):
                    e_ref[g, r, 0:tm + HALO, :] = ext_ref[r:r + tm + HALO, 128 * g:128 * g + 128]
            units.append(shift)
        for g in range(ng):
            lanes = slice(128 * g, 128 * g + 128)
            for i0 in range(0, tm, rb):
                def conv(g=g, lanes=lanes, i0=i0):
                    acc = jnp.zeros((rb, 128), F32)
                    for k in range(CONF_K):
                        m, r = divmod(k + HALO - CONF_K + 1, 8)
                        acc = acc + e_ref[g, r, i0 + 8 * m:i0 + 8 * m + rb, :] * wb_ref[k:k + 1, lanes]
                    a1s_ref[i0:i0 + rb, lanes] = acc + dwb_ref[:, lanes]
                units.append(conv)

        def norm():
            a1 = a1s_ref[...]
            a1_ref[rows, :] = a1
            mu = jnp.mean(a1, axis=-1, keepdims=True)
            ac = a1 - mu
            rstd = lax.rsqrt(jnp.mean(ac * ac, axis=-1, keepdims=True) + EPS)
            a2 = ac * rstd * lng_ref[...] + lnb_ref[...]
            mixed_ref[rows, 0:dc] = (a2 * _sigmoid(a2)).astype(BF16)
        units.append(norm)

        def short():
            halo = extp_ref[tm:tm + 8, :]
            extp_ref[0:8, :] = halo if first is False else jnp.where(first, 0.0, halo)
            extp_ref[8:8 + tm, :] = p_ref[:, 3 * dc:4 * dc] * p_ref[:, 4 * dc:5 * dc]
            q = jnp.zeros((tm, dc), F32)
            for k in range(SC_K):
                q = q + extp_ref[6 + k:6 + k + tm, :] * scw_ref[k:k + 1, :]
            q_ref[rows, :] = q
            mixed_ref[rows, dc:2 * dc] = (p_ref[:, 2 * dc:3 * dc] * q).astype(BF16)
        units.append(short)

        def keep():
            proj_ref[rows, :] = p_ref[...]
            h1_ref[rows, :] = h_ref[...]
        units.append(keep)
        return units

    def body(x0_ref, xa_ref, xb_ref, mod0_ref, moda_ref, modb_ref, w_ref,
             wb_ref, dwb_ref, lng_ref, lnb_ref, scw_ref, dep_ref,
             proj_ref, h1_ref, mixed_ref, a1_ref, q_ref,
             p0_ref, p1_ref, h0_ref, hh1_ref, ext_ref, e_ref, extp_ref, a1s_ref):
        j = pl.program_id(0)

        @pl.when(j == 0)
        def _():
            ext_ref[...] = jnp.zeros(ext_ref.shape, F32)
            extp_ref[...] = jnp.zeros(extp_ref.shape, F32)
            _run_units(proj_units(x0_ref, mod0_ref, w_ref, p0_ref, h0_ref))

        common = (wb_ref, dwb_ref, lng_ref, lnb_ref, scw_ref, proj_ref, h1_ref, mixed_ref, a1_ref, q_ref,
                  ext_ref, e_ref, extp_ref, a1s_ref)
        _run_units(mix_units((2 * j) % tps == 0, p0_ref, h0_ref, 0, *common),
                   proj_units(xa_ref, moda_ref, w_ref, p1_ref, hh1_ref))
        _run_units(mix_units(False, p1_ref, hh1_ref, tm, *common),
                   proj_units(xb_ref, modb_ref, w_ref, p0_ref, h0_ref))

    last = nt - 1
    xspec = lambda f: pl.BlockSpec((tm, d), lambda j: (f(j), 0))
    mspec = lambda f: pl.BlockSpec((None, 1, 6 * d), lambda j: (f(j) // tps, 0, 0))
    out2 = lambda c: pl.BlockSpec((2 * tm, c), lambda j: (j, 0))
    return pl.pallas_call(
        body, name="in_mix_fwd", grid=(nt // 2,),
        in_specs=[xspec(lambda j: 0), xspec(lambda j: 2 * j + 1), xspec(lambda j: jnp.minimum(2 * j + 2, last)),
                  mspec(lambda j: 0), mspec(lambda j: 2 * j + 1), mspec(lambda j: jnp.minimum(2 * j + 2, last)),
                  _resident((d, d_in)),
                  _const(wb.shape), _const(dwb.shape), _const(lng.shape), _const(lnb.shape), _const(scw.shape), ANY],
        out_specs=[out2(d_in), out2(d), out2(2 * dc), out2(dc), out2(dc)],
        out_shape=[jax.ShapeDtypeStruct((n, d_in), F32), jax.ShapeDtypeStruct((n, d), BF16),
                   jax.ShapeDtypeStruct((n, 2 * dc), BF16), jax.ShapeDtypeStruct((n, dc), F32),
                   jax.ShapeDtypeStruct((n, dc), F32)],
        scratch_shapes=[pltpu.VMEM((tm, d_in), F32), pltpu.VMEM((tm, d_in), F32),
                        pltpu.VMEM((tm, d), BF16), pltpu.VMEM((tm, d), BF16),
                        pltpu.VMEM((tm + HALO + 8, dc), F32), _shift_scratch(tm, dc),
                        pltpu.VMEM((tm + 8, dc), F32), pltpu.VMEM((tm, dc), F32)],
        compiler_params=pltpu.CompilerParams(dimension_semantics=("arbitrary",)),
    )(x, x, x, mod3, mod3, mod3, w_in, wb, dwb, lng, lnb, scw, dep)


def _shift_scratch(tm, dc):
    return pltpu.VMEM((dc // 128, 8, tm + HALO + 8, 128), F32)


def _out_proj(mixed, w_out, x, mod3, t, dep):
    n, d = x.shape
    tm = min(TM_MM, t)
    tps = t // tm

    def body(m_ref, w_ref, x_ref, mod_ref, dep_ref, x1_ref, y1_ref, h2_ref):
        y1 = _dot(m_ref[...], w_ref[...])
        y1_ref[...] = y1.astype(BF16)
        x1 = x_ref[...] + mod_ref[:, 2 * d:3 * d] * y1
        x1_ref[...] = x1
        xn, _ = _rms(x1)
        h2_ref[...] = (xn * (1.0 + mod_ref[:, 4 * d:5 * d]) + mod_ref[:, 3 * d:4 * d]).astype(BF16)

    return pl.pallas_call(
        body, name="out_proj", grid=(n // tm,),
        in_specs=[_rowblk(tm, d), _resident((d, d)), _rowblk(tm, d), _modspec(tps, 6 * d), ANY],
        out_specs=[_rowblk(tm, d), _rowblk(tm, d), _rowblk(tm, d)],
        out_shape=[jax.ShapeDtypeStruct((n, d), F32), jax.ShapeDtypeStruct((n, d), BF16),
                   jax.ShapeDtypeStruct((n, d), BF16)],
        compiler_params=pltpu.CompilerParams(dimension_semantics=("parallel",)),
    )(mixed, w_out, x, mod3, dep)


def _mlp_fwd(h2, x1, tgt, mod3, gfin, w1, w2, t):
    n, d = x1.shape
    dff = w1.shape[1]
    tm = min(TM_MLP, t)
    tps = t // tm
    nt = n // tm

    def body(h_ref, x1_ref, tg_ref, mod_ref, gf_ref, w1_ref, w2_ref,
             z_ref, dx2_ref, dy2_ref, dg2_ref, ggf_ref, loss_ref):
        i = pl.program_id(0)
        g2 = mod_ref[:, 5 * d:6 * d]
        gf = gf_ref[...]
        sub = tm // MLP_SUB
        sums = None
        for part in range(MLP_SUB):
            rs = slice(part * sub, (part + 1) * sub)
            hv = h_ref[rs, :]
            y2 = jnp.zeros((sub, d), F32)
            for j in range(dff // FF_CHUNK):
                cols = slice(j * FF_CHUNK, (j + 1) * FF_CHUNK)
                z = _dot(hv, w1_ref[:, cols])
                z_ref[rs, cols] = z.astype(BF16)
                zr = jnp.maximum(z, 0.0)
                y2 = y2 + _dot((zr * zr).astype(BF16), w2_ref[cols, :])
            x2n, r3 = _rms(x1_ref[rs, :] + g2 * y2)
            diff = x2n * gf - tg_ref[rs, :]
            dout = diff * (1.0 / d)
            dx2 = _rms_bwd(dout * gf, x2n, r3)
            dx2_ref[rs, :] = dx2
            dy2_ref[rs, :] = (g2 * dx2).astype(BF16)
            p = (_rows8(dx2 * y2), _rows8(dout * x2n), _rows8(diff * diff))
            sums = p if sums is None else tuple(a + b for a, b in zip(sums, p))
        _acc_add(dg2_ref, i % tps == 0, sums[0])
        _acc_add(ggf_ref, i == 0, sums[1])
        _acc_add(loss_ref, i == 0, sums[2])

    return pl.pallas_call(
        body, name="mlp_fwd", grid=(nt,),
        in_specs=[_rowblk(tm, d), _rowblk(tm, d), _rowblk(tm, d), _modspec(tps, 6 * d), _const((1, d)),
                  _resident((d, dff)), _resident((dff, d))],
        out_specs=[_rowblk(tm, dff), _rowblk(tm, d), _rowblk(tm, d), _accspec(tps, d),
                   _const((8, d)), _const((8, d))],
        out_shape=[jax.ShapeDtypeStruct((n, dff), BF16), jax.ShapeDtypeStruct((n, d), F32),
                   jax.ShapeDtypeStruct((n, d), BF16), jax.ShapeDtypeStruct((n // t, 8, d), F32),
                   jax.ShapeDtypeStruct((8, d), F32), jax.ShapeDtypeStruct((8, d), F32)],
        compiler_params=pltpu.CompilerParams(dimension_semantics=("arbitrary",)),
    )(h2, x1, tgt, mod3, gfin, w1, w2)


def _mlp_bwd(dy2, z, x1, dx2, y1, mod3, w1, w2, t):
    n, d = x1.shape
    dff = w1.shape[1]
    tm = min(TM_MLP, t)
    tps = t // tm

    def body(dy2_ref, z_ref, x1_ref, dx2_ref, y1_ref, mod_ref, w1_ref, w2_ref,
             dz_ref, dx1_ref, dy1_ref, dsh_ref, dsc_ref, dg1_ref):
        first = pl.program_id(0) % tps == 0
        sub = tm // MLP_SUB
        sums = None
        for part in range(MLP_SUB):
            rs = slice(part * sub, (part + 1) * sub)
            dy2 = dy2_ref[rs, :]
            dh2 = jnp.zeros((sub, d), F32)
            for j in range(dff // FF_CHUNK):
                cols = slice(j * FF_CHUNK, (j + 1) * FF_CHUNK)
                du = _dot_nt(dy2, w2_ref[cols, :])
                dz = (du * (2.0 * jnp.maximum(z_ref[rs, cols].astype(F32), 0.0))).astype(BF16)
                dz_ref[rs, cols] = dz
                dh2 = dh2 + _dot_nt(dz, w1_ref[:, cols])
            x1n, r2 = _rms(x1_ref[rs, :])
            dx1 = dx2_ref[rs, :] + _rms_bwd(dh2 * (1.0 + mod_ref[:, 4 * d:5 * d]), x1n, r2)
            dx1_ref[rs, :] = dx1
            dy1_ref[rs, :] = (mod_ref[:, 2 * d:3 * d] * dx1).astype(BF16)
            p = (_rows8(dh2), _rows8(dh2 * x1n), _rows8(dx1 * y1_ref[rs, :].astype(F32)))
            sums = p if sums is None else tuple(a + b for a, b in zip(sums, p))
        _acc_add(dsh_ref, first, sums[0])
        _acc_add(dsc_ref, first, sums[1])
        _acc_add(dg1_ref, first, sums[2])

    acc = jax.ShapeDtypeStruct((n // t, 8, d), F32)
    return pl.pallas_call(
        body, name="mlp_bwd", grid=(n // tm,),
        in_specs=[_rowblk(tm, d), _rowblk(tm, dff), _rowblk(tm, d), _rowblk(tm, d), _rowblk(tm, d),
                  _modspec(tps, 6 * d), _resident((d, dff)), _resident((dff, d))],
        out_specs=[_rowblk(tm, dff), _rowblk(tm, d), _rowblk(tm, d),
                   _accspec(tps, d), _accspec(tps, d), _accspec(tps, d)],
        out_shape=[jax.ShapeDtypeStruct((n, dff), BF16), jax.ShapeDtypeStruct((n, d), F32),
                   jax.ShapeDtypeStruct((n, d), BF16), acc, acc, acc],
        compiler_params=pltpu.CompilerParams(dimension_semantics=("arbitrary",)),
    )(dy2, z, x1, dx2, y1, mod3, w1, w2)


def _wgrad(a, b, name, relu2=False, bn=None, deps=()):
    n, ka = a.shape
    nb = b.shape[1]
    tk = min(TK_WG, n)
    bm = min(ka, 1024)
    if bn is None:
        bn = nb if nb <= 2048 else nb // 2

    def body(a_ref, b_ref, *rest):
        out_ref = rest[-1]
        av = a_ref[...]
        if relu2:
            ar = jnp.maximum(av, 0.0)
            av = ar * ar
        p = _dot_tn(av, b_ref[...])
        _acc_add(out_ref, pl.program_id(2) == 0, p)

    return pl.pallas_call(
        body, name=name, grid=(ka // bm, nb // bn, n // tk),
        in_specs=[pl.BlockSpec((tk, bm), lambda i, j, k: (k, i)),
                  pl.BlockSpec((tk, bn), lambda i, j, k: (k, j))] + [ANY] * len(deps),
        out_specs=pl.BlockSpec((bm, bn), lambda i, j, k: (i, j)),
        out_shape=jax.ShapeDtypeStruct((ka, nb), F32),
        compiler_params=pltpu.CompilerParams(dimension_semantics=("parallel", "parallel", "arbitrary")),
    )(a, b, *deps)


SG_DWW = 0
SG_DWB = CONF_K
SG_LNG = CONF_K + 1
SG_LNB = CONF_K + 2
SG_SCW = CONF_K + 3
SG_N = CONF_K + 3 + SC_K


def _mix_in_bwd(dy1, w_out, a1, q, proj, wb, lng, lnb, scw, w_in, x, dx1, mod3, t, dep):
    n, d_in = proj.shape
    d = x.shape[1]
    dc = d_in // 5
    tm = min(TM_MIX, t)
    tps = t // tm
    nt = n // tm
    rw = min(RB_WG, tm)
    ng = dc // 128
    ncol = 256

    def ln_bwd(a1v, da3, lng_v, lnb_v):
        mu = jnp.mean(a1v, axis=-1, keepdims=True)
        ac = a1v - mu
        rstd = lax.rsqrt(jnp.mean(ac * ac, axis=-1, keepdims=True) + EPS)
        ah = ac * rstd
        a2 = ah * lng_v + lnb_v
        s2 = _sigmoid(a2)
        da2 = da3 * (s2 * (1.0 + a2 * (1.0 - s2)))
        dah = da2 * lng_v
        da1 = rstd * (dah - jnp.mean(dah, axis=-1, keepdims=True)
                      - ah * jnp.mean(dah * ah, axis=-1, keepdims=True))
        return da1, da2, ah

    def dmixed_units(dy_ref, wo_ref, dm_ref):
        units = []
        for c0 in range(0, 2 * dc, ncol):
            def chunk(c0=c0):
                dm_ref[:, c0:c0 + ncol] = _dot_nt(dy_ref[...], wo_ref[c0:c0 + ncol, :])
            units.append(chunk)
        return units

    def mix_units(k, dm_ref, a1_ref, q_ref, p_ref, wb_ref, lng_ref, lnb_ref, scw_ref,
                  dproj_ref, sg_ref, extd_ref, ed_ref, a0_ref, da0_ref, extq_ref, cda_ref, cdq_ref):
        keep_next = jnp.where(k % tps == tps - 1, 0.0, 1.0)
        units = []

        def head():
            lng_v, lnb_v = lng_ref[...], lnb_ref[...]
            da1, da2, ah = ln_bwd(a1_ref[...], dm_ref[:, 0:dc], lng_v, lnb_v)
            sg_ref[8 * SG_LNG:8 * SG_LNG + 8, :] += _rows8(da2 * ah)
            sg_ref[8 * SG_LNB:8 * SG_LNB + 8, :] += _rows8(da2)
            sg_ref[8 * SG_DWB:8 * SG_DWB + 8, :] += _rows8(da1)
            a0_ref[...] = p_ref[:, 0:dc] * _sigmoid(p_ref[:, dc:2 * dc])
            extd_ref[0:tm, :] = da1
            extd_ref[tm:tm + HALO, :] = cda_ref[...] * keep_next
            extd_ref[tm + HALO:tm + HALO + 8, :] = jnp.zeros((8, dc), F32)
            cda_ref[...] = da1[0:HALO, :]
            ds = dm_ref[:, dc:2 * dc]
            dproj_ref[:, 2 * dc:3 * dc] = (ds * q_ref[...]).astype(BF16)
            dq = ds * p_ref[:, 2 * dc:3 * dc]
            extq_ref[0:tm, :] = dq
            extq_ref[tm:tm + 8, :] = cdq_ref[...] * keep_next
            cdq_ref[...] = dq[0:8, :]
        units.append(head)
        for g in range(ng):
            def shift(g=g):
                for r in range(8):
                    ed_ref[g, r, 0:tm + HALO, :] = extd_ref[r:r + tm + HALO, 128 * g:128 * g + 128]
            units.append(shift)
        for g in range(ng):
            lanes = slice(128 * g, 128 * g + 128)
            accs = [None] * CONF_K
            for i0 in range(0, tm, rw):
                def conv(g=g, lanes=lanes, i0=i0, accs=accs):
                    a0v = a0_ref[i0:i0 + rw, lanes]
                    acc = jnp.zeros((rw, 128), F32)
                    for s in range(CONF_K):
                        m, r = divmod(s, 8)
                        e = ed_ref[g, r, i0 + 8 * m:i0 + 8 * m + rw, :]
                        acc = acc + e * wb_ref[CONF_K - 1 - s:CONF_K - s, lanes]
                        part = _rows8(e * a0v)
                        accs[s] = part if accs[s] is None else accs[s] + part
                    da0_ref[i0:i0 + rw, lanes] = acc
                units.append(conv)

            def flush(lanes=lanes, accs=accs):
                for s in range(CONF_K):
                    kk = CONF_K - 1 - s
                    sg_ref[8 * (SG_DWW + kk):8 * (SG_DWW + kk) + 8, lanes] += accs[s]
            units.append(flush)

        def emit():
            da0 = da0_ref[...]
            sig = _sigmoid(p_ref[:, dc:2 * dc])
            dproj_ref[:, 0:dc] = (da0 * sig).astype(BF16)
            dproj_ref[:, dc:2 * dc] = (da0 * a0_ref[...] * (1.0 - sig)).astype(BF16)
        units.append(emit)

        def short():
            scc, sch = p_ref[:, 3 * dc:4 * dc], p_ref[:, 4 * dc:5 * dc]
            pv = scc * sch
            dp = jnp.zeros((tm, dc), F32)
            for kk in range(SC_K):
                dqs = extq_ref[SC_K - 1 - kk:SC_K - 1 - kk + tm, :]
                dp = dp + dqs * scw_ref[kk:kk + 1, :]
                sg_ref[8 * (SG_SCW + kk):8 * (SG_SCW + kk) + 8, :] += _rows8(pv * dqs)
            dproj_ref[:, 3 * dc:4 * dc] = (dp * sch).astype(BF16)
            dproj_ref[:, 4 * dc:5 * dc] = (dp * scc).astype(BF16)
        units.append(short)
        return units

    def proj_units(first, dp_ref, w_ref, x_ref, dx1_ref, mod_ref, dh_ref, gx_ref, dsh_ref, dsc_ref):
        units = []
        for c0 in range(0, d, ncol):
            def chunk(c0=c0):
                dh_ref[:, c0:c0 + ncol] = _dot_nt(dp_ref[...], w_ref[c0:c0 + ncol, :])
            units.append(chunk)

        def tail():
            dh1 = dh_ref[...]
            xn, r1 = _rms(x_ref[...])
            v1, v2 = _rows8(dh1), _rows8(dh1 * xn)
            dsh_ref[...] = jnp.where(first, v1, dsh_ref[...] + v1)
            dsc_ref[...] = jnp.where(first, v2, dsc_ref[...] + v2)
            gx_ref[...] = dx1_ref[...] + _rms_bwd(dh1 * (1.0 + mod_ref[:, d:2 * d]), xn, r1)
        units.append(tail)
        return units

    def body(dyl_ref, dy_ref, wo_ref, a1_ref, q_ref, p_ref, wb_ref, lng_ref, lnb_ref, scw_ref,
             w_ref, x_ref, dx1_ref, mod_ref, dep_ref,
             dproj_ref, sg_ref, gx_ref, dsh_ref, dsc_ref,
             extd_ref, ed_ref, a0_ref, da0_ref, extq_ref, dp_ref, dh_ref, dm_ref, cda_ref, cdq_ref):
        k = pl.program_id(0)

        @pl.when(k == 0)
        def _():
            sg_ref[...] = jnp.zeros(sg_ref.shape, F32)
            dp_ref[...] = jnp.zeros(dp_ref.shape, BF16)
            cda_ref[...] = jnp.zeros(cda_ref.shape, F32)
            cdq_ref[...] = jnp.zeros(cdq_ref.shape, F32)
            _run_units(dmixed_units(dyl_ref, wo_ref, dm_ref))

        first = jnp.logical_or(k <= 1, (nt - k) % tps == tps - 1)
        after = lambda: proj_units(first, dp_ref, w_ref, x_ref, dx1_ref, mod_ref, dh_ref, gx_ref, dsh_ref, dsc_ref)

        @pl.when(k < nt)
        def _():
            _run_units(mix_units(nt - 1 - k, dm_ref, a1_ref, q_ref, p_ref, wb_ref, lng_ref, lnb_ref, scw_ref,
                                 dproj_ref, sg_ref, extd_ref, ed_ref, a0_ref, da0_ref, extq_ref, cda_ref, cdq_ref),
                       after(), dmixed_units(dy_ref, wo_ref, dm_ref))
            dp_ref[...] = dproj_ref[...]

        @pl.when(k == nt)
        def _():
            _run_units(after())

    cur = lambda k: jnp.maximum(nt - 1 - k, 0)
    prev = lambda k: jnp.minimum(nt - k, nt - 1)
    acc = jax.ShapeDtypeStruct((n // t, 8, d), F32)
    return pl.pallas_call(
        body, name="mix_in_bwd", grid=(nt + 1,),
        in_specs=[pl.BlockSpec((tm, d), lambda k: (nt - 1, 0)),
                  pl.BlockSpec((tm, d), lambda k: (jnp.maximum(nt - 2 - k, 0), 0)), _resident(w_out.shape),
                  pl.BlockSpec((tm, dc), lambda k: (cur(k), 0)),
                  pl.BlockSpec((tm, dc), lambda k: (cur(k), 0)),
                  pl.BlockSpec((tm, d_in), lambda k: (cur(k), 0)),
                  _const(wb.shape), _const(lng.shape), _const(lnb.shape), _const(scw.shape),
                  _resident((d, d_in)),
                  pl.BlockSpec((tm, d), lambda k: (prev(k), 0)), pl.BlockSpec((tm, d), lambda k: (prev(k), 0)),
                  pl.BlockSpec((None, 1, 6 * d), lambda k: (prev(k) // tps, 0, 0)), ANY],
        out_specs=[pl.BlockSpec((tm, d_in), lambda k: (cur(k), 0)), _const((8 * SG_N, dc)),
                   pl.BlockSpec((tm, d), lambda k: (prev(k), 0)),
                   pl.BlockSpec((None, 8, d), lambda k: (prev(k) // tps, 0, 0)),
                   pl.BlockSpec((None, 8, d), lambda k: (prev(k) // tps, 0, 0))],
        out_shape=[jax.ShapeDtypeStruct((n, d_in), BF16), jax.ShapeDtypeStruct((8 * SG_N, dc), F32),
                   jax.ShapeDtypeStruct((n, d), F32), acc, acc],
        scratch_shapes=[pltpu.VMEM((tm + HALO + 8, dc), F32), _shift_scratch(tm, dc),
                        pltpu.VMEM((tm, dc), F32), pltpu.VMEM((tm, dc), F32),
                        pltpu.VMEM((tm + 8, dc), F32),
                        pltpu.VMEM((tm, d_in), BF16), pltpu.VMEM((tm, d), F32), pltpu.VMEM((tm, 2 * dc), F32),
                        pltpu.VMEM((HALO, dc), F32), pltpu.VMEM((8, dc), F32)],
        compiler_params=pltpu.CompilerParams(dimension_semantics=("arbitrary",)),
    )(dy1, dy1, w_out, a1, q, proj, wb, lng, lnb, scw, w_in, x, dx1, mod3, dep)


SMALL_ROWS = 40


def _pack_small(ids, sg, ggf, loss, accs, d, dep):
    dc = d // 2
    nb = accs[0].shape[0]

    def body(ids_ref, sg_ref, ggf_ref, loss_ref, dsh1, dsc1, dg1, dsh2, dsc2, dg2, dep_ref, pack_ref, dmod_ref):
        pack_ref[...] = jnp.zeros(pack_ref.shape, F32)
        for k in range(SG_N):
            pack_ref[k:k + 1, :] = jnp.sum(sg_ref[8 * k:8 * k + 8, :], axis=0, keepdims=True)
        gf = jnp.sum(ggf_ref[...], axis=0, keepdims=True)
        pack_ref[SG_N:SG_N + 1, :] = gf[:, 0:dc]
        pack_ref[SG_N + 1:SG_N + 2, :] = gf[:, dc:d]
        tot = jnp.sum(jnp.sum(loss_ref[...], axis=0, keepdims=True), axis=1, keepdims=True) * (0.5 / d)
        pack_ref[SG_N + 2:SG_N + 3, :] = jnp.broadcast_to(tot, (1, dc))
        for b in range(nb):
            row = jnp.concatenate([jnp.sum(ref[b], axis=0, keepdims=True)
                                   for ref in (dsh1, dsc1, dg1, dsh2, dsc2, dg2)], axis=1)
            for f in range(fold):
                dmod_ref[b * fold + f:b * fold + f + 1, :] = row[:, f * wf:(f + 1) * wf]

    fold = 8 // nb
    wf = 6 * d // fold
    assert nb * fold == 8 and wf % 128 == 0
    whole = lambda a: pl.BlockSpec(a.shape, lambda i, ids: (0,) * a.ndim)
    mine = lambda r, c: pl.BlockSpec((None, r, c), lambda i, ids: (ids[2], 0, 0))
    return pl.pallas_call(
        body, name="pack_small",
        grid_spec=pltpu.PrefetchScalarGridSpec(
            num_scalar_prefetch=1, grid=(1,),
            in_specs=[whole(a) for a in (sg, ggf, loss, *accs)] + [ANY],
            out_specs=[mine(SMALL_ROWS, dc), mine(8, wf)]),
        out_shape=[jax.ShapeDtypeStruct((N_DEV, SMALL_ROWS, dc), F32), jax.ShapeDtypeStruct((N_DEV, 8, wf), F32)],
    )(ids, sg, ggf, loss, *accs, dep)


def _small_reduce(pack_all, dmod_all, nb, dep):
    def body(pk_ref, dm_ref, dep_ref, red_ref, dmod_ref, gb_ref):
        tot = pk_ref[0]
        for dev in range(1, N_DEV):
            tot = tot + pk_ref[dev]
        red_ref[...] = tot
        for f in range(fold):
            gb = jnp.zeros((1, wf), F32)
            for dev in range(N_DEV):
                for b in range(nb):
                    seg = dm_ref[dev, b * fold + f:b * fold + f + 1, :]
                    dmod_ref[dev * nb + b:dev * nb + b + 1, f * wf:(f + 1) * wf] = seg
                    gb = gb + seg
            gb_ref[:, f * wf:(f + 1) * wf] = gb

    fold = 8 // nb
    wf = dmod_all.shape[2]
    return pl.pallas_call(
        body, name="small_reduce",
        out_shape=[jax.ShapeDtypeStruct(pack_all.shape[1:], F32),
                   jax.ShapeDtypeStruct((N_DEV * nb, fold * wf), F32),
                   jax.ShapeDtypeStruct((1, fold * wf), F32)],
        in_specs=[VMEM] * 2 + [ANY], out_specs=[VMEM] * 3,
    )(pack_all, dmod_all, dep)


def _adam(w, g, m, v):
    m = ADAM_B1 * m + (1.0 - ADAM_B1) * g
    v = ADAM_B2 * v + (1.0 - ADAM_B2) * (g * g)
    m_hat = m / (1.0 - ADAM_B1 ** ADAM_STEP)
    v_hat = v / (1.0 - ADAM_B2 ** ADAM_STEP)
    delta = -ADAM_LR * (m_hat / (jnp.sqrt(v_hat) + ADAM_EPS) + ADAM_WD * w)
    return delta, m, v


def _adamw_big(w, g, m, v, name):
    r, c = w.shape
    tr = min(r, 256)

    def body(w_ref, g_ref, m_ref, v_ref, d_ref, nm_ref, nv_ref):
        d_ref[...], nm_ref[...], nv_ref[...] = _adam(w_ref[...], g_ref[...], m_ref[...], v_ref[...])

    s = jax.ShapeDtypeStruct((r, c), F32)
    return pl.pallas_call(
        body, name=name, grid=(r // tr,),
        in_specs=[_rowblk(tr, c)] * 4, out_specs=[_rowblk(tr, c)] * 3, out_shape=[s, s, s],
        compiler_params=pltpu.CompilerParams(dimension_semantics=("parallel",)),
    )(w, g, m, v)


def _adamw_ada(act_t, dmod_cols, w, m, v):
    r, c = w.shape
    tr = min(r, 256)
    nb = act_t.shape[1]

    def body(a_ref, dm_ref, w_ref, m_ref, v_ref, g_ref, d_ref, nm_ref, nv_ref):
        g = jnp.dot(a_ref[...], dm_ref[...], preferred_element_type=F32, precision=HIGHEST)
        g_ref[...] = g
        d_ref[...], nm_ref[...], nv_ref[...] = _adam(w_ref[...], g, m_ref[...], v_ref[...])

    s = jax.ShapeDtypeStruct((r, c), F32)
    return pl.pallas_call(
        body, name="adamw_w_ada", grid=(r // tr,),
        in_specs=[_rowblk(tr, nb), _const((nb, c))] + [_rowblk(tr, c)] * 3,
        out_specs=[_rowblk(tr, c)] * 4, out_shape=[s, s, s, s],
        compiler_params=pltpu.CompilerParams(dimension_semantics=("parallel",)),
    )(act_t, dmod_cols, w, m, v)


def _adamw_small(ws, gs, ms, vs):
    n = len(ws)

    def body(*refs):
        for i in range(n):
            w, g, m, v = (refs[j * n + i][...] for j in range(4))
            dl, nm, nv = _adam(w, g, m, v)
            refs[4 * n + i][...] = dl
            refs[5 * n + i][...] = nm
            refs[6 * n + i][...] = nv

    shapes = [jax.ShapeDtypeStruct(w.shape, F32) for w in ws]
    return pl.pallas_call(
        body, name="adamw_small", out_shape=shapes * 3,
        in_specs=[VMEM] * (4 * n), out_specs=[VMEM] * (3 * n),
    )(*ws, *gs, *ms, *vs)


def kernel(x, c, w_ada, b_ada, w_in, conf_dw_w, conf_dw_b, conf_ln_g, conf_ln_b, sc_conv_w, w_out, w_mlp1, w_mlp2, g_final, loss_target, m_w_ada, m_b_ada, m_w_in, m_conf_dw_w, m_conf_dw_b, m_conf_ln_g, m_conf_ln_b, m_sc_conv_w, m_w_out, m_w_mlp1, m_w_mlp2, m_g_final, v_w_ada, v_b_ada, v_w_in, v_conf_dw_w, v_conf_dw_b, v_conf_ln_g, v_conf_ln_b, v_sc_conv_w, v_w_out, v_w_mlp1, v_w_mlp2, v_g_final):
    nb, t, d = x.shape
    n = nb * t
    dc = d // 2
    ada_w = w_ada.shape[2]
    ax, ay, ac = _me()
    chip = 2 * ax + ay
    dev = 2 * chip + ac
    ids = jnp.stack([chip, ac, dev]).astype(jnp.int32)

    lays = _wlayout(d)
    names = ("in", "out", "mlp1", "mlp2")
    fulls = _cast_place(ids, [w_in[0], w_out[0], w_mlp1[0], w_mlp2[0]], lays)

    c_pad = jnp.zeros((8, d), F32).at[0:nb].set(c)
    cw_pad = jnp.zeros((SMALL_ROWS, dc // N_CHIPS), F32)
    cw_pad = cw_pad.at[0:CONF_K].set(conf_dw_w[0]).at[HALO:HALO + SC_K].set(sc_conv_w[0])
    c_all8, cw_all8 = _all_gather8([c_pad, cw_pad], "gather_c")
    plan_i = _gather_chip_plan(lays[0:1])
    sems_i, bufs_i, tok_i = _copy_start("gather_in_start", [fulls[0]], 3, plan_i, [c_all8])
    c_all = c_all8[:, 0:nb].reshape(N_DEV * nb, d) + tok_i[0, 0]
    cw_full = jnp.concatenate([cw_all8[2 * k] for k in range(N_CHIPS)], axis=1)
    dww, scw = cw_full[0:CONF_K], cw_full[HALO:HALO + SC_K]
    b_cols = lax.dynamic_slice(b_ada, (0, chip * ada_w), (1, ada_w))
    c_act, mod_shard = _ada_mod(c_all, w_ada[0], b_cols)
    mod_all = _gather_mod(mod_shard)
    mod3 = lax.dynamic_slice(mod_all, (dev * nb, 0), (nb, 6 * d)).reshape(nb, 1, 6 * d)

    x2 = x.reshape(n, d)
    tgt = loss_target.reshape(n, d)
    (wf_in,) = _copy_wait("gather_in_wait", bufs_i, sems_i, plan_i, [mod3])
    (wf_in,) = _copy_blocking("gather_in_pass", [wf_in], 3, _gather_pass_plan(lays[0:1]))
    plan_o, plan_b, plan_p = _gather_direct_plan(lays[1:2]), _gather_chip_plan(lays[2:4]), _gather_pass_plan(lays[2:4])
    sems_o, bufs_o, tok_o = _copy_start("gather_out_start", [fulls[1]], 6, plan_o, [wf_in, mod3])
    sems_b, bufs_b, tok_b = _copy_start("gather_mlp_start", fulls[2:4], 6, plan_b, [tok_o])
    proj, h1, mixed, a1, q = _in_mix_fwd(x2, mod3, wf_in, dww, conf_dw_b, conf_ln_g, conf_ln_b, scw, t, tok_b)
    (wf_out,) = _copy_wait("gather_out_wait", bufs_o, sems_o, plan_o, [mixed])
    bufs_b = _copy_wait("gather_mlp_wait", bufs_b, sems_b, plan_b, [mixed])
    sems_p, bufs_p, tok_p = _copy_start("gather_pass_start", bufs_b, 6, plan_p)
    x1, y1, h2 = _out_proj(mixed, wf_out, x2, mod3, t, tok_p)
    wf_1, wf_2 = _copy_wait("gather_pass_wait", bufs_p, sems_p, plan_p, [h2])
    z, dx2, dy2, dg2, ggf, loss_p = _mlp_fwd(h2, x1, tgt, mod3, g_final.reshape(1, d), wf_1, wf_2, t)

    dz, dx1, dy1, dsh2, dsc2, dg1 = _mlp_bwd(dy2, z, x1, dx2, y1, mod3, wf_1, wf_2, t)
    g_w2 = _wgrad(z, dy2, "wgrad_mlp2", relu2=True)
    g_w1 = _wgrad(h2, dz, "wgrad_mlp1")
    made = {}

    def behind_pair_exchange(tok):
        made["g_wout"] = _wgrad(mixed, dy1, "wgrad_out", deps=[tok])
        return [made["g_wout"]]

    def behind_chip_exchange(tok):
        made["dproj"], made["sg"], made["grad_x"], made["dsh1"], made["dsc1"] = _mix_in_bwd(
            dy1, wf_out, a1, q, proj, dww, conf_ln_g, conf_ln_b, scw, wf_in, x2, dx1, mod3, t, tok)
        return [made["dproj"]]

    def behind_pair_share(tok):
        made["g_win"] = _wgrad(h1, made["dproj"], "wgrad_in", deps=[tok])
        return [made["g_win"]]

    gr_1, gr_2 = _reduce_scatter(ids, [g_w1, g_w2], lays[2:4], names[2:4], "m",
                                 (behind_pair_exchange, behind_chip_exchange, behind_pair_share))

    big = {}

    plan_g = _gather8_plan(2)

    def behind_pair_exchange_in(tok):
        packs = _pack_small(ids, made["sg"], ggf, loss_p, (made["dsh1"], made["dsc1"], dg1, dsh2, dsc2, dg2), d, tok)
        made["gather"] = _copy_start("gather_small_start", list(packs), 2 * (N_DEV - 1), plan_g)
        return [made["gather"][2]]

    def behind_chip_exchange_in(tok):
        sems_g, bufs_g, _ = made["gather"]
        gathered = _copy_wait("gather_small_wait", bufs_g, sems_g, plan_g, [tok])
        red, dmod_all, g_bada = _small_reduce(*gathered, nb, tok)
        dmod_cols = lax.dynamic_slice(dmod_all, (0, chip * ada_w), (N_DEV * nb, ada_w))
        made["ada"] = _adamw_ada(c_act.T, dmod_cols, w_ada[0], m_w_ada[0], v_w_ada[0])
        for nm_, g_, w_, m_, v_ in (("w_mlp1", gr_1, w_mlp1, m_w_mlp1, v_w_mlp1),
                                    ("w_mlp2", gr_2, w_mlp2, m_w_mlp2, v_w_mlp2)):
            big[nm_] = _adamw_big(w_[0], g_, m_[0], v_[0], "adamw_" + nm_)
        cw = dc // N_CHIPS
        g_dww = lax.dynamic_slice(red[0:CONF_K], (0, chip * cw), (CONF_K, cw))
        g_scw = lax.dynamic_slice(red[SG_SCW:SG_SCW + SC_K], (0, chip * cw), (SC_K, cw))
        g_gfin = jnp.concatenate([red[SG_N:SG_N + 1], red[SG_N + 1:SG_N + 2]], axis=1)
        made["small_g"] = [g_bada, g_dww, red[SG_DWB:SG_DWB + 1], red[SG_LNG:SG_LNG + 1], red[SG_LNB:SG_LNB + 1],
                           g_scw, g_gfin]
        small_w = [b_ada, conf_dw_w[0], conf_dw_b, conf_ln_g, conf_ln_b, sc_conv_w[0], g_final.reshape(1, d)]
        small_m = [m_b_ada, m_conf_dw_w[0], m_conf_dw_b, m_conf_ln_g, m_conf_ln_b, m_sc_conv_w[0],
                   m_g_final.reshape(1, d)]
        small_v = [v_b_ada, v_conf_dw_w[0], v_conf_dw_b, v_conf_ln_g, v_conf_ln_b, v_sc_conv_w[0],
                   v_g_final.reshape(1, d)]
        made["upd"] = _adamw_small(small_w, made["small_g"], small_m, small_v)
        made["loss"] = red[SG_N + 2, 0]
        return [big["w_mlp2"][0], made["upd"][0]]

    gr_in, gr_out = _reduce_scatter(ids, [made["g_win"], made["g_wout"]], lays[0:2], names[0:2], "i",
                                    (behind_pair_exchange_in, behind_chip_exchange_in, None))
    big["w_in"] = _adamw_big(w_in[0], gr_in, m_w_in[0], v_w_in[0], "adamw_w_in")
    big["w_out"] = _adamw_big(w_out[0], gr_out, m_w_out[0], v_w_out[0], "adamw_w_out")
    grad_x, small_g, upd, loss = made["grad_x"], made["small_g"], made["upd"], made["loss"]
    g_wada, d_wada, nm_wada, nv_wada = made["ada"]
    ns = len(small_g)
    s_delta, s_m, s_v = upd[0:ns], upd[ns:2 * ns], upd[2 * ns:3 * ns]

    def outs(kind_big, kind_small, wada):
        sm = kind_small
        return (wada[None], sm[0], kind_big["w_in"][None], sm[1][None], sm[2], sm[3], sm[4], sm[5][None],
                kind_big["w_out"][None], kind_big["w_mlp1"][None], kind_big["w_mlp2"][None], sm[6].reshape(d))

    grads_out = outs({"w_in": gr_in, "w_out": gr_out, "w_mlp1": gr_1, "w_mlp2": gr_2}, small_g, g_wada)
    delta_out = outs({k: v[0] for k, v in big.items()}, s_delta, d_wada)
    m_out = outs({k: v[1] for k, v in big.items()}, s_m, nm_wada)
    v_out = outs({k: v[2] for k, v in big.items()}, s_v, nv_wada)
    return (loss, grad_x.reshape(nb, t, d), *grads_out, *delta_out, *m_out, *v_out)
```

```python
import functools

import jax
import jax.numpy as jnp
from jax import lax
from jax.experimental import pallas as pl
from jax.experimental.pallas import tpu as pltpu

F32 = jnp.float32
BF16 = jnp.bfloat16
MESH = pl.DeviceIdType.MESH
HIGHEST = lax.Precision.HIGHEST

EPS = 1e-6
CONF_K = 31
SC_K = 3
HALO = 32
N_CHIPS = 4
N_DEV = 8

ADAM_LR = 0.001
ADAM_B1 = 0.9
ADAM_B2 = 0.999
ADAM_EPS = 1e-08
ADAM_WD = 0.01
ADAM_STEP = 10

TM_MM = 1024
TM_MIX = 256
TM_MLP = 512
FF_CHUNK = 1024
MLP_SUB = 2
TK_WG = 2048
RB_CONV = 64
RB_WG = 32
CHIP_RELS = ((1, 0), (0, 1), (1, 1))

ANY = pl.BlockSpec(memory_space=pl.ANY)
VMEM = pl.BlockSpec(memory_space=pltpu.VMEM)
HBM = pl.BlockSpec(memory_space=pltpu.HBM)
SEM = pl.BlockSpec(memory_space=pltpu.SEMAPHORE)
EFFECT = pltpu.SideEffectType.DATAFLOW_SIDE_EFFECTING


def _me():
    return lax.axis_index("x"), lax.axis_index("y"), lax.axis_index("c")


def _flip(v, f):
    return 1 - v if f else v


def _rows8(v):
    r, c = v.shape
    return v.reshape(r // 8, 8, c).sum(axis=0)


def _rms(x):
    r = lax.rsqrt(jnp.mean(x * x, axis=-1, keepdims=True) + EPS)
    return x * r, r


def _rms_bwd(dxn, xn, r):
    return r * (dxn - xn * jnp.mean(dxn * xn, axis=-1, keepdims=True))


def _sigmoid(x):
    return 1.0 / (1.0 + jnp.exp(-x))


def _dot(a, b):
    return jnp.dot(a, b, preferred_element_type=F32)


def _dot_nt(a, b):
    return lax.dot_general(a, b, (((1,), (1,)), ((), ())), preferred_element_type=F32)


def _dot_tn(a, b):
    return lax.dot_general(a, b, (((0,), (0,)), ((), ())), preferred_element_type=F32)


def _const(shape):
    nd = len(shape)
    return pl.BlockSpec(shape, lambda i: (0,) * nd)


def _resident(shape):
    nd = len(shape)
    return pl.BlockSpec(shape, lambda i: (0,) * nd, pipeline_mode=pl.Buffered(1))


def _rowblk(tm, c):
    return pl.BlockSpec((tm, c), lambda i: (i, 0))


def _modspec(tps, width):
    return pl.BlockSpec((None, 1, width), lambda i: (i // tps, 0, 0))


def _accspec(tps, c):
    return pl.BlockSpec((None, 8, c), lambda i: (i // tps, 0, 0))


def _acc_add(ref, first, v):
    @pl.when(first)
    def _():
        ref[...] = v

    @pl.when(jnp.logical_not(first))
    def _():
        ref[...] += v


def _all_gather8(arrs, name):
    n = len(arrs)

    def body(*refs):
        ins, outs = refs[:n], refs[n:2 * n]
        send, recv = refs[2 * n:]
        x, y, c = _me()
        dev = 4 * x + 2 * y + c
        for a in range(n):
            outs[a][dev] = ins[a][...]
        sends = []
        for r in range(1, N_DEV):
            fx, fy, fc = (r >> 2) & 1, (r >> 1) & 1, r & 1
            peer = (_flip(x, fx), _flip(y, fy), _flip(c, fc))
            for a in range(n):
                cp = pltpu.make_async_remote_copy(
                    src_ref=ins[a], dst_ref=outs[a].at[dev],
                    send_sem=send.at[r - 1, a], recv_sem=recv.at[r - 1, a],
                    device_id=peer, device_id_type=MESH)
                cp.start()
                sends.append(cp)
        for r in range(1, N_DEV):
            fx, fy, fc = (r >> 2) & 1, (r >> 1) & 1, r & 1
            pdev = 4 * _flip(x, fx) + 2 * _flip(y, fy) + _flip(c, fc)
            for a in range(n):
                pltpu.make_async_remote_copy(
                    src_ref=ins[a], dst_ref=outs[a].at[pdev],
                    send_sem=send.at[r - 1, a], recv_sem=recv.at[r - 1, a],
                    device_id=(x, y, c), device_id_type=MESH).wait_recv()
        for cp in sends:
            cp.wait_send()

    return pl.pallas_call(
        body, name=name,
        out_shape=[jax.ShapeDtypeStruct((N_DEV,) + a.shape, a.dtype) for a in arrs],
        in_specs=[VMEM] * n, out_specs=[VMEM] * n,
        scratch_shapes=[pltpu.SemaphoreType.DMA((N_DEV - 1, n)),
                        pltpu.SemaphoreType.DMA((N_DEV - 1, n))],
    )(*arrs)


def _gather_mod(mod_shard):
    nb, w = mod_shard.shape

    def body(in_ref, out_ref, send, recv):
        x, y, c = _me()
        chip = 2 * x + y
        out_ref[:, pl.ds(pl.multiple_of(chip * w, 128), w)] = in_ref[...]
        sends = []
        for r, (fx, fy) in enumerate(CHIP_RELS):
            cp = pltpu.make_async_remote_copy(
                src_ref=in_ref,
                dst_ref=out_ref.at[:, pl.ds(pl.multiple_of(chip * w, 128), w)],
                send_sem=send.at[r], recv_sem=recv.at[r],
                device_id=(_flip(x, fx), _flip(y, fy), c), device_id_type=MESH)
            cp.start()
            sends.append(cp)
        for r, (fx, fy) in enumerate(CHIP_RELS):
            pchip = 2 * _flip(x, fx) + _flip(y, fy)
            pltpu.make_async_remote_copy(
                src_ref=in_ref,
                dst_ref=out_ref.at[:, pl.ds(pl.multiple_of(pchip * w, 128), w)],
                send_sem=send.at[r], recv_sem=recv.at[r],
                device_id=(x, y, c), device_id_type=MESH).wait_recv()
        for cp in sends:
            cp.wait_send()

    return pl.pallas_call(
        body, name="gather_mod",
        out_shape=jax.ShapeDtypeStruct((nb, N_CHIPS * w), mod_shard.dtype),
        in_specs=[VMEM], out_specs=VMEM,
        scratch_shapes=[pltpu.SemaphoreType.DMA((3,)), pltpu.SemaphoreType.DMA((3,))],
    )(mod_shard)


def _wlayout(d):
    d_in = 5 * d // 2
    return (
        (d, d_in // N_CHIPS, True),
        (d // N_CHIPS, d, False),
        (d, 4 * d // N_CHIPS, True),
        (4 * d // N_CHIPS, d, False),
    )


def _full_shape(lay):
    r, c, by_col = lay
    return (r, c * N_CHIPS) if by_col else (r * N_CHIPS, c)


def _full_view(ref, lay, k, h):
    r, c, by_col = lay
    hr = r // 2
    if by_col:
        return ref.at[pl.ds(pl.multiple_of(h * hr, 16), hr), pl.ds(pl.multiple_of(k * c, 128), c)]
    return ref.at[pl.ds(pl.multiple_of(k * r + h * hr, 16), hr), :]


def _half_view(ref, lay, h):
    hr = lay[0] // 2
    return ref.at[pl.ds(pl.multiple_of(h * hr, 16), hr), :]


def _half_shape(lay):
    return (lay[0] // 2, lay[1])


def _hbm(a):
    return pltpu.with_memory_space_constraint(a, pltpu.HBM)


def _remote(src, dst, send, recv, idx, to):
    return lambda: pltpu.make_async_remote_copy(src_ref=src, dst_ref=dst, send_sem=send.at[idx], recv_sem=recv.at[idx],
                                                device_id=to, device_id_type=MESH)


def _copy_start(name, bufs, n_sems, plan, after=()):
    nb, na = len(bufs), len(after)

    def body(*refs):
        sends, _ = plan(refs[:nb], refs[nb + na], refs[nb + na + 1])
        for mk in sends:
            mk().start()
        refs[-1][...] = jnp.zeros((8, 128), F32)

    outs = pl.pallas_call(
        body, name=name,
        out_shape=(pltpu.SemaphoreType.DMA((n_sems,)), pltpu.SemaphoreType.DMA((n_sems,)))
        + tuple(pltpu.HBM(b.shape, b.dtype) for b in bufs) + (jax.ShapeDtypeStruct((8, 128), F32),),
        in_specs=(HBM,) * nb + (ANY,) * na, out_specs=(SEM, SEM) + (HBM,) * nb + (VMEM,),
        input_output_aliases={i: 2 + i for i in range(nb)},
        compiler_params=pltpu.CompilerParams(has_side_effects=EFFECT),
    )(*[_hbm(b) for b in bufs], *after)
    return (outs[0], outs[1]), list(outs[2:2 + nb]), outs[-1]


def _copy_wait(name, bufs, sems, plan, after):
    nb, na = len(bufs), len(after)

    def body(*refs):
        sends, recvs = plan(refs[:nb], refs[nb], refs[nb + 1])
        for mk in sends:
            mk().wait_send()
        for mk in recvs:
            mk().wait_recv()

    outs = pl.pallas_call(
        body, name=name,
        out_shape=tuple(pltpu.HBM(b.shape, b.dtype) for b in bufs),
        in_specs=(HBM,) * nb + (SEM, SEM) + (ANY,) * na, out_specs=(HBM,) * nb,
        input_output_aliases={i: i for i in range(nb)},
        compiler_params=pltpu.CompilerParams(has_side_effects=EFFECT),
    )(*bufs, *sems, *after)
    return list(outs)


def _copy_blocking(name, bufs, n_sems, plan, after=()):
    nb, na = len(bufs), len(after)

    def body(*refs):
        sends, recvs = plan(refs[:nb], refs[2 * nb + na], refs[2 * nb + na + 1])
        started = [mk() for mk in sends]
        for cp in started:
            cp.start()
        for mk in recvs:
            mk().wait_recv()
        for cp in started:
            cp.wait_send()

    return list(pl.pallas_call(
        body, name=name,
        out_shape=tuple(jax.ShapeDtypeStruct(b.shape, b.dtype) for b in bufs),
        in_specs=(ANY,) * (nb + na), out_specs=(ANY,) * nb,
        input_output_aliases={i: i for i in range(nb)},
        scratch_shapes=[pltpu.SemaphoreType.DMA((n_sems,)), pltpu.SemaphoreType.DMA((n_sems,))],
    )(*bufs, *after))


def _exchange(name, bufs, n_sems, plan, between):
    if between is None:
        return _copy_blocking(name, bufs, n_sems, plan)
    sems, bufs, tok = _copy_start(name + "_start", bufs, n_sems, plan)
    return _copy_wait(name + "_wait", bufs, sems, plan, between(tok))


def _gather_direct_plan(lays):
    def plan(full, send, recv):
        x, y, c = _me()
        chip = 2 * x + y
        sends, recvs = [], []
        for r, (fx, fy) in enumerate(CHIP_RELS):
            px, py = _flip(x, fx), _flip(y, fy)
            for i, lay in enumerate(lays):
                for q in range(2):
                    oc = _flip(c, q)
                    mine = _full_view(full[i], lay, chip, c)
                    idx = (r * len(lays) + i) * 2 + q
                    sends.append(_remote(mine, mine, send, recv, idx, (px, py, oc)))
                    theirs = _full_view(full[i], lay, 2 * px + py, oc)
                    recvs.append(_remote(theirs, theirs, send, recv, idx, (x, y, c)))
        return sends, recvs
    return plan


def _gather_chip_plan(lays):
    def plan(full, send, recv):
        x, y, c = _me()
        chip = 2 * x + y
        sends, recvs = [], []
        for r, (fx, fy) in enumerate(CHIP_RELS):
            px, py = _flip(x, fx), _flip(y, fy)
            for i, lay in enumerate(lays):
                mine = _full_view(full[i], lay, chip, c)
                sends.append(_remote(mine, mine, send, recv, r * len(lays) + i, (px, py, c)))
                theirs = _full_view(full[i], lay, 2 * px + py, c)
                recvs.append(_remote(theirs, theirs, send, recv, r * len(lays) + i, (x, y, c)))
        return sends, recvs
    return plan


def _gather_pass_plan(lays):
    def plan(full, send, recv):
        x, y, c = _me()
        sends, recvs = [], []
        for r, (fx, fy) in enumerate(CHIP_RELS):
            pchip = 2 * _flip(x, fx) + _flip(y, fy)
            for i, lay in enumerate(lays):
                landed = _full_view(full[i], lay, pchip, c)
                sends.append(_remote(landed, landed, send, recv, r * len(lays) + i, (x, y, 1 - c)))
                other = _full_view(full[i], lay, pchip, 1 - c)
                recvs.append(_remote(other, other, send, recv, r * len(lays) + i, (x, y, c)))
        return sends, recvs
    return plan


def _gather8_plan(na):
    def plan(bufs, send, recv):
        x, y, c = _me()
        dev = 4 * x + 2 * y + c
        sends, recvs = [], []
        for r in range(1, N_DEV):
            fx, fy, fc = (r >> 2) & 1, (r >> 1) & 1, r & 1
            px, py, pc = _flip(x, fx), _flip(y, fy), _flip(c, fc)
            for a in range(na):
                idx = (r - 1) * na + a
                sends.append(_remote(bufs[a].at[dev], bufs[a].at[dev], send, recv, idx, (px, py, pc)))
                theirs = bufs[a].at[4 * px + 2 * py + pc]
                recvs.append(_remote(theirs, theirs, send, recv, idx, (x, y, c)))
        return sends, recvs
    return plan


def _pair_exchange_plan(lays):
    nw = len(lays)

    def plan(bufs, send, recv):
        x, y, c = _me()
        sends, recvs = [], []
        for i, lay in enumerate(lays):
            for k in range(N_CHIPS):
                sends.append(_remote(_full_view(bufs[i], lay, k, 1 - c), bufs[nw + i].at[k],
                                     send, recv, i * N_CHIPS + k, (x, y, 1 - c)))
                recvs.append(_remote(_full_view(bufs[i], lay, k, c), bufs[nw + i].at[k],
                                     send, recv, i * N_CHIPS + k, (x, y, c)))
        return sends, recvs
    return plan


def _chip_exchange_plan(nw):
    def plan(bufs, send, recv):
        x, y, c = _me()
        sends, recvs = [], []
        for r, (fx, fy) in enumerate(CHIP_RELS):
            px, py = _flip(x, fx), _flip(y, fy)
            for i in range(nw):
                sends.append(_remote(bufs[i].at[2 * px + py], bufs[nw + i].at[r], send, recv, r * nw + i, (px, py, c)))
                recvs.append(_remote(bufs[i].at[2 * px + py], bufs[nw + i].at[r], send, recv, r * nw + i, (x, y, c)))
        return sends, recvs
    return plan


def _pair_share_plan(lays):
    def plan(bufs, send, recv):
        x, y, c = _me()
        sends, recvs = [], []
        for i, lay in enumerate(lays):
            mine = _half_view(bufs[i], lay, c)
            sends.append(_remote(mine, mine, send, recv, i, (x, y, 1 - c)))
            other = _half_view(bufs[i], lay, 1 - c)
            recvs.append(_remote(other, other, send, recv, i, (x, y, c)))
        return sends, recvs
    return plan


def _pair_sum(ids, g, got, lay, name):
    r, c, by_col = lay
    hr = r // 2
    tr = min(hr, 256)
    nt = hr // tr

    def body(ids_ref, g_ref, got_ref, s32_ref, s16_ref):
        s = g_ref[...] + got_ref[...]
        s32_ref[...] = s
        s16_ref[...] = s.astype(BF16)

    if by_col:
        gspec = pl.BlockSpec((tr, c), lambda k, t, ids: (ids[1] * nt + t, k))
    else:
        gspec = pl.BlockSpec((tr, c), lambda k, t, ids: ((2 * k + ids[1]) * nt + t, 0))
    hspec = pl.BlockSpec((None, tr, c), lambda k, t, ids: (k, t, 0))
    return pl.pallas_call(
        body, name=name,
        grid_spec=pltpu.PrefetchScalarGridSpec(
            num_scalar_prefetch=1, grid=(N_CHIPS, nt),
            in_specs=[gspec, hspec], out_specs=[hspec, hspec]),
        out_shape=[jax.ShapeDtypeStruct((N_CHIPS, hr, c), F32),
                   jax.ShapeDtypeStruct((N_CHIPS, hr, c), BF16)],
    )(ids, g, got)


def _chip_sum(ids, s32, got, lay, name):
    hr, c = _half_shape(lay)
    tr = min(hr, 256)
    nt = hr // tr

    def body(ids_ref, s_ref, got_ref, out_ref):
        t = s_ref[...]
        for r in range(3):
            t = t + got_ref[r].astype(F32)
        out_ref[...] = t

    return pl.pallas_call(
        body, name=name,
        grid_spec=pltpu.PrefetchScalarGridSpec(
            num_scalar_prefetch=1, grid=(nt,),
            in_specs=[pl.BlockSpec((None, tr, c), lambda t, ids: (ids[0], t, 0)),
                      pl.BlockSpec((3, tr, c), lambda t, ids: (0, t, 0))],
            out_specs=pl.BlockSpec((tr, c), lambda t, ids: (ids[1] * nt + t, 0))),
        out_shape=jax.ShapeDtypeStruct((2 * hr, c), F32),
    )(ids, s32, got)


def _reduce_scatter(ids, grads, lays, names, tag, between):
    nw = len(lays)
    got1 = [lax.empty((N_CHIPS,) + _half_shape(l), F32) for l in lays]
    bufs = _exchange("pair_exchange_" + tag, list(grads) + got1, nw * N_CHIPS, _pair_exchange_plan(lays), between[0])
    sums = [_pair_sum(ids, bufs[i], bufs[nw + i], lays[i], "pair_sum_" + names[i]) for i in range(nw)]
    got2 = [lax.empty((3,) + _half_shape(l), BF16) for l in lays]
    bufs = _exchange("chip_exchange_" + tag, [s[1] for s in sums] + got2, 3 * nw, _chip_exchange_plan(nw), between[1])
    mine = [_chip_sum(ids, sums[i][0], bufs[nw + i], lays[i], "chip_sum_" + names[i]) for i in range(nw)]
    return _exchange("pair_share_" + tag, mine, nw, _pair_share_plan(lays), between[2])


def _cast_place(ids, ws, lays):
    nw = len(ws)

    def body(ids_ref, *refs):
        for i in range(nw):
            refs[nw + i][...] = refs[i][...].astype(BF16)

    by_col_map = lambda i, ids: (0, ids[0])
    by_row_map = lambda i, ids: (ids[0], 0)
    return pl.pallas_call(
        body, name="cast_place",
        grid_spec=pltpu.PrefetchScalarGridSpec(
            num_scalar_prefetch=1, grid=(1,),
            in_specs=[pl.BlockSpec(l[:2], lambda i, ids: (0, 0)) for l in lays],
            out_specs=[pl.BlockSpec(l[:2], by_col_map if l[2] else by_row_map) for l in lays]),
        out_shape=[jax.ShapeDtypeStruct(_full_shape(l), BF16) for l in lays],
    )(ids, *ws)


def _ada_mod(c_all, w_ada, b_ada):
    def body(c_ref, w_ref, b_ref, act_ref, mod_ref):
        cv = c_ref[...]
        act = cv * _sigmoid(cv)
        act_ref[...] = act
        mod_ref[...] = jnp.dot(act, w_ref[...], preferred_element_type=F32, precision=HIGHEST) + b_ref[...]

    nb = c_all.shape[0]
    return pl.pallas_call(
        body, name="ada_mod",
        out_shape=[jax.ShapeDtypeStruct(c_all.shape, F32),
                   jax.ShapeDtypeStruct((nb, w_ada.shape[1]), F32)],
        in_specs=[VMEM] * 3, out_specs=[VMEM] * 2,
    )(c_all, w_ada, b_ada)


def _run_units(*unit_lists):
    total = max(len(u) for u in unit_lists)
    done = [0] * len(unit_lists)
    for step in range(1, total + 1):
        for li, units in enumerate(unit_lists):
            upto = (step * len(units) + total - 1) // total
            while done[li] < upto:
                units[done[li]]()
                done[li] += 1


def _in_mix_fwd(x, mod3, w_in, wb, dwb, lng, lnb, scw, t, dep):
    n, d = x.shape
    d_in = w_in.shape[1]
    dc = d_in // 5
    tm = min(TM_MIX, t)
    tps = t // tm
    nt = n // tm
    rb = min(RB_CONV, tm)
    ncol = 256
    ng = dc // 128
    assert tps % 2 == 0 and nt % 2 == 0

    def proj_units(x_ref, mod_ref, w_ref, p_ref, h_ref):
        def head():
            xn, _ = _rms(x_ref[...])
            h_ref[...] = (xn * (1.0 + mod_ref[:, d:2 * d]) + mod_ref[:, 0:d]).astype(BF16)
        units = [head]
        for c0 in range(0, d_in, ncol):
            def chunk(c0=c0):
                p_ref[:, c0:c0 + ncol] = _dot(h_ref[...], w_ref[:, c0:c0 + ncol])
            units.append(chunk)
        return units

    def mix_units(first, p_ref, h_ref, r0, wb_ref, dwb_ref, lng_ref, lnb_ref, scw_ref,
                  proj_ref, h1_ref, mixed_ref, a1_ref, q_ref, ext_ref, e_ref, extp_ref, a1s_ref):
        rows = slice(r0, r0 + tm)
        units = []

        def glu():
            halo = ext_ref[tm:tm + HALO, :]
            ext_ref[0:HALO, :] = halo if first is False else jnp.where(first, 0.0, halo)
            ext_ref[HALO:HALO + tm, :] = p_ref[:, 0:dc] * _sigmoid(p_ref[:, dc:2 * dc])
        units.append(glu)
        for g in range(ng):
            def shift(g=g):
                for r in range(8):
                    e_ref[g, r, 0:tm + HALO, :] = ext_ref[r:r + tm + HALO, 128 * g:128 * g + 128]
            units.append(shift)
        for g in range(ng):
            lanes = slice(128 * g, 128 * g + 128)
            for i0 in range(0, tm, rb):
                def conv(g=g, lanes=lanes, i0=i0):
                    acc = jnp.zeros((rb, 128), F32)
                    for k in range(CONF_K):
                        m, r = divmod(k + HALO - CONF_K + 1, 8)
                        acc = acc + e_ref[g, r, i0 + 8 * m:i0 + 8 * m + rb, :] * wb_ref[k:k + 1, lanes]
                    a1s_ref[i0:i0 + rb, lanes] = acc + dwb_ref[:, lanes]
                units.append(conv)

        def norm():
            a1 = a1s_ref[...]
            a1_ref[rows, :] = a1
            mu = jnp.mean(a1, axis=-1, keepdims=True)
            ac = a1 - mu
            rstd = lax.rsqrt(jnp.mean(ac * ac, axis=-1, keepdims=True) + EPS)
            a2 = ac * rstd * lng_ref[...] + lnb_ref[...]
            mixed_ref[rows, 0:dc] = (a2 * _sigmoid(a2)).astype(BF16)
        units.append(norm)

        def short():
            halo = extp_ref[tm:tm + 8, :]
            extp_ref[0:8, :] = halo if first is False else jnp.where(first, 0.0, halo)
            extp_ref[8:8 + tm, :] = p_ref[:, 3 * dc:4 * dc] * p_ref[:, 4 * dc:5 * dc]
            q = jnp.zeros((tm, dc), F32)
            for k in range(SC_K):
                q = q + extp_ref[6 + k:6 + k + tm, :] * scw_ref[k:k + 1, :]
            q_ref[rows, :] = q
            mixed_ref[rows, dc:2 * dc] = (p_ref[:, 2 * dc:3 * dc] * q).astype(BF16)
        units.append(short)

        def keep():
            proj_ref[rows, :] = p_ref[...]
            h1_ref[rows, :] = h_ref[...]
        units.append(keep)
        return units

    def body(x0_ref, xa_ref, xb_ref, mod0_ref, moda_ref, modb_ref, w_ref,
             wb_ref, dwb_ref, lng_ref, lnb_ref, scw_ref, dep_ref,
             proj_ref, h1_ref, mixed_ref, a1_ref, q_ref,
             p0_ref, p1_ref, h0_ref, hh1_ref, ext_ref, e_ref, extp_ref, a1s_ref):
        j = pl.program_id(0)

        @pl.when(j == 0)
        def _():
            ext_ref[...] = jnp.zeros(ext_ref.shape, F32)
            extp_ref[...] = jnp.zeros(extp_ref.shape, F32)
            _run_units(proj_units(x0_ref, mod0_ref, w_ref, p0_ref, h0_ref))

        common = (wb_ref, dwb_ref, lng_ref, lnb_ref, scw_ref, proj_ref, h1_ref, mixed_ref, a1_ref, q_ref,
                  ext_ref, e_ref, extp_ref, a1s_ref)
        _run_units(mix_units((2 * j) % tps == 0, p0_ref, h0_ref, 0, *common),
                   proj_units(xa_ref, moda_ref, w_ref, p1_ref, hh1_ref))
        _run_units(mix_units(False, p1_ref, hh1_ref, tm, *common),
                   proj_units(xb_ref, modb_ref, w_ref, p0_ref, h0_ref))

    last = nt - 1
    xspec = lambda f: pl.BlockSpec((tm, d), lambda j: (f(j), 0))
    mspec = lambda f: pl.BlockSpec((None, 1, 6 * d), lambda j: (f(j) // tps, 0, 0))
    out2 = lambda c: pl.BlockSpec((2 * tm, c), lambda j: (j, 0))
    return pl.pallas_call(
        body, name="in_mix_fwd", grid=(nt // 2,),
        in_specs=[xspec(lambda j: 0), xspec(lambda j: 2 * j + 1), xspec(lambda j: jnp.minimum(2 * j + 2, last)),
                  mspec(lambda j: 0), mspec(lambda j: 2 * j + 1), mspec(lambda j: jnp.minimum(2 * j + 2, last)),
                  _resident((d, d_in)),
                  _const(wb.shape), _const(dwb.shape), _const(lng.shape), _const(lnb.shape), _const(scw.shape), ANY],
        out_specs=[out2(d_in), out2(d), out2(2 * dc), out2(dc), out2(dc)],
        out_shape=[jax.ShapeDtypeStruct((n, d_in), F32), jax.ShapeDtypeStruct((n, d), BF16),
                   jax.ShapeDtypeStruct((n, 2 * dc), BF16), jax.ShapeDtypeStruct((n, dc), F32),
                   jax.ShapeDtypeStruct((n, dc), F32)],
        scratch_shapes=[pltpu.VMEM((tm, d_in), F32), pltpu.VMEM((tm, d_in), F32),
                        pltpu.VMEM((tm, d), BF16), pltpu.VMEM((tm, d), BF16),
                        pltpu.VMEM((tm + HALO + 8, dc), F32), _shift_scratch(tm, dc),
                        pltpu.VMEM((tm + 8, dc), F32), pltpu.VMEM((tm, dc), F32)],
        compiler_params=pltpu.CompilerParams(dimension_semantics=("arbitrary",)),
    )(x, x, x, mod3, mod3, mod3, w_in, wb, dwb, lng, lnb, scw, dep)


def _shift_scratch(tm, dc):
    return pltpu.VMEM((dc // 128, 8, tm + HALO + 8, 128), F32)


def _out_proj(mixed, w_out, x, mod3, t, dep):
    n, d = x.shape
    tm = min(TM_MM, t)
    tps = t // tm

    def body(m_ref, w_ref, x_ref, mod_ref, dep_ref, x1_ref, y1_ref, h2_ref):
        y1 = _dot(m_ref[...], w_ref[...])
        y1_ref[...] = y1.astype(BF16)
        x1 = x_ref[...] + mod_ref[:, 2 * d:3 * d] * y1
        x1_ref[...] = x1
        xn, _ = _rms(x1)
        h2_ref[...] = (xn * (1.0 + mod_ref[:, 4 * d:5 * d]) + mod_ref[:, 3 * d:4 * d]).astype(BF16)

    return pl.pallas_call(
        body, name="out_proj", grid=(n // tm,),
        in_specs=[_rowblk(tm, d), _resident((d, d)), _rowblk(tm, d), _modspec(tps, 6 * d), ANY],
        out_specs=[_rowblk(tm, d), _rowblk(tm, d), _rowblk(tm, d)],
        out_shape=[jax.ShapeDtypeStruct((n, d), F32), jax.ShapeDtypeStruct((n, d), BF16),
                   jax.ShapeDtypeStruct((n, d), BF16)],
        compiler_params=pltpu.CompilerParams(dimension_semantics=("parallel",)),
    )(mixed, w_out, x, mod3, dep)


def _mlp_fwd(h2, x1, tgt, mod3, gfin, w1, w2, t):
    n, d = x1.shape
    dff = w1.shape[1]
    tm = min(TM_MLP, t)
    tps = t // tm
    nt = n // tm

    def body(h_ref, x1_ref, tg_ref, mod_ref, gf_ref, w1_ref, w2_ref,
             z_ref, dx2_ref, dy2_ref, dg2_ref, ggf_ref, loss_ref):
        i = pl.program_id(0)
        g2 = mod_ref[:, 5 * d:6 * d]
        gf = gf_ref[...]
        sub = tm // MLP_SUB
        sums = None
        for part in range(MLP_SUB):
            rs = slice(part * sub, (part + 1) * sub)
            hv = h_ref[rs, :]
            y2 = jnp.zeros((sub, d), F32)
            for j in range(dff // FF_CHUNK):
                cols = slice(j * FF_CHUNK, (j + 1) * FF_CHUNK)
                z = _dot(hv, w1_ref[:, cols])
                z_ref[rs, cols] = z.astype(BF16)
                zr = jnp.maximum(z, 0.0)
                y2 = y2 + _dot((zr * zr).astype(BF16), w2_ref[cols, :])
            x2n, r3 = _rms(x1_ref[rs, :] + g2 * y2)
            diff = x2n * gf - tg_ref[rs, :]
            dout = diff * (1.0 / d)
            dx2 = _rms_bwd(dout * gf, x2n, r3)
            dx2_ref[rs, :] = dx2
            dy2_ref[rs, :] = (g2 * dx2).astype(BF16)
            p = (_rows8(dx2 * y2), _rows8(dout * x2n), _rows8(diff * diff))
            sums = p if sums is None else tuple(a + b for a, b in zip(sums, p))
        _acc_add(dg2_ref, i % tps == 0, sums[0])
        _acc_add(ggf_ref, i == 0, sums[1])
        _acc_add(loss_ref, i == 0, sums[2])

    return pl.pallas_call(
        body, name="mlp_fwd", grid=(nt,),
        in_specs=[_rowblk(tm, d), _rowblk(tm, d), _rowblk(tm, d), _modspec(tps, 6 * d), _const((1, d)),
                  _resident((d, dff)), _resident((dff, d))],
        out_specs=[_rowblk(tm, dff), _rowblk(tm, d), _rowblk(tm, d), _accspec(tps, d),
                   _const((8, d)), _const((8, d))],
        out_shape=[jax.ShapeDtypeStruct((n, dff), BF16), jax.ShapeDtypeStruct((n, d), F32),
                   jax.ShapeDtypeStruct((n, d), BF16), jax.ShapeDtypeStruct((n // t, 8, d), F32),
                   jax.ShapeDtypeStruct((8, d), F32), jax.ShapeDtypeStruct((8, d), F32)],
        compiler_params=pltpu.CompilerParams(dimension_semantics=("arbitrary",)),
    )(h2, x1, tgt, mod3, gfin, w1, w2)


def _mlp_bwd(dy2, z, x1, dx2, y1, mod3, w1, w2, t):
    n, d = x1.shape
    dff = w1.shape[1]
    tm = min(TM_MLP, t)
    tps = t // tm

    def body(dy2_ref, z_ref, x1_ref, dx2_ref, y1_ref, mod_ref, w1_ref, w2_ref,
             dz_ref, dx1_ref, dy1_ref, dsh_ref, dsc_ref, dg1_ref):
        first = pl.program_id(0) % tps == 0
        sub = tm // MLP_SUB
        sums = None
        for part in range(MLP_SUB):
            rs = slice(part * sub, (part + 1) * sub)
            dy2 = dy2_ref[rs, :]
            dh2 = jnp.zeros((sub, d), F32)
            for j in range(dff // FF_CHUNK):
                cols = slice(j * FF_CHUNK, (j + 1) * FF_CHUNK)
                du = _dot_nt(dy2, w2_ref[cols, :])
                dz = (du * (2.0 * jnp.maximum(z_ref[rs, cols].astype(F32), 0.0))).astype(BF16)
                dz_ref[rs, cols] = dz
                dh2 = dh2 + _dot_nt(dz, w1_ref[:, cols])
            x1n, r2 = _rms(x1_ref[rs, :])
            dx1 = dx2_ref[rs, :] + _rms_bwd(dh2 * (1.0 + mod_ref[:, 4 * d:5 * d]), x1n, r2)
            dx1_ref[rs, :] = dx1
            dy1_ref[rs, :] = (mod_ref[:, 2 * d:3 * d] * dx1).astype(BF16)
            p = (_rows8(dh2), _rows8(dh2 * x1n), _rows8(dx1 * y1_ref[rs, :].astype(F32)))
            sums = p if sums is None else tuple(a + b for a, b in zip(sums, p))
        _acc_add(dsh_ref, first, sums[0])
        _acc_add(dsc_ref, first, sums[1])
        _acc_add(dg1_ref, first, sums[2])

    acc = jax.ShapeDtypeStruct((n // t, 8, d), F32)
    return pl.pallas_call(
        body, name="mlp_bwd", grid=(n // tm,),
        in_specs=[_rowblk(tm, d), _rowblk(tm, dff), _rowblk(tm, d), _rowblk(tm, d), _rowblk(tm, d),
                  _modspec(tps, 6 * d), _resident((d, dff)), _resident((dff, d))],
        out_specs=[_rowblk(tm, dff), _rowblk(tm, d), _rowblk(tm, d),
                   _accspec(tps, d), _accspec(tps, d), _accspec(tps, d)],
        out_shape=[jax.ShapeDtypeStruct((n, dff), BF16), jax.ShapeDtypeStruct((n, d), F32),
                   jax.ShapeDtypeStruct((n, d), BF16), acc, acc, acc],
        compiler_params=pltpu.CompilerParams(dimension_semantics=("arbitrary",)),
    )(dy2, z, x1, dx2, y1, mod3, w1, w2)


def _wgrad(a, b, name, relu2=False, bn=None, deps=()):
    n, ka = a.shape
    nb = b.shape[1]
    tk = min(TK_WG, n)
    bm = min(ka, 1024)
    if bn is None:
        bn = nb if nb <= 2048 else nb // 2

    def body(a_ref, b_ref, *rest):
        out_ref = rest[-1]
        av = a_ref[...]
        if relu2:
            ar = jnp.maximum(av, 0.0)
            av = ar * ar
        p = _dot_tn(av, b_ref[...])
        _acc_add(out_ref, pl.program_id(2) == 0, p)

    return pl.pallas_call(
        body, name=name, grid=(ka // bm, nb // bn, n // tk),
        in_specs=[pl.BlockSpec((tk, bm), lambda i, j, k: (k, i)),
                  pl.BlockSpec((tk, bn), lambda i, j, k: (k, j))] + [ANY] * len(deps),
        out_specs=pl.BlockSpec((bm, bn), lambda i, j, k: (i, j)),
        out_shape=jax.ShapeDtypeStruct((ka, nb), F32),
        compiler_params=pltpu.CompilerParams(dimension_semantics=("parallel", "parallel", "arbitrary")),
    )(a, b, *deps)


SG_DWW = 0
SG_DWB = CONF_K
SG_LNG = CONF_K + 1
SG_LNB = CONF_K + 2
SG_SCW = CONF_K + 3
SG_N = CONF_K + 3 + SC_K


def _mix_in_bwd(dy1, w_out, a1, q, proj, wb, lng, lnb, scw, w_in, x, dx1, mod3, t, dep):
    n, d_in = proj.shape
    d = x.shape[1]
    dc = d_in // 5
    tm = min(TM_MIX, t)
    tps = t // tm
    nt = n // tm
    rw = min(RB_WG, tm)
    ng = dc // 128
    ncol = 256

    def ln_bwd(a1v, da3, lng_v, lnb_v):
        mu = jnp.mean(a1v, axis=-1, keepdims=True)
        ac = a1v - mu
        rstd = lax.rsqrt(jnp.mean(ac * ac, axis=-1, keepdims=True) + EPS)
        ah = ac * rstd
        a2 = ah * lng_v + lnb_v
        s2 = _sigmoid(a2)
        da2 = da3 * (s2 * (1.0 + a2 * (1.0 - s2)))
        dah = da2 * lng_v
        da1 = rstd * (dah - jnp.mean(dah, axis=-1, keepdims=True)
                      - ah * jnp.mean(dah * ah, axis=-1, keepdims=True))
        return da1, da2, ah

    def dmixed_units(dy_ref, wo_ref, dm_ref):
        units = []
        for c0 in range(0, 2 * dc, ncol):
            def chunk(c0=c0):
                dm_ref[:, c0:c0 + ncol] = _dot_nt(dy_ref[...], wo_ref[c0:c0 + ncol, :])
            units.append(chunk)
        return units

    def mix_units(k, dm_ref, a1_ref, q_ref, p_ref, wb_ref, lng_ref, lnb_ref, scw_ref,
                  dproj_ref, sg_ref, extd_ref, ed_ref, a0_ref, da0_ref, extq_ref, cda_ref, cdq_ref):
        keep_next = jnp.where(k % tps == tps - 1, 0.0, 1.0)
        units = []

        def head():
            lng_v, lnb_v = lng_ref[...], lnb_ref[...]
            da1, da2, ah = ln_bwd(a1_ref[...], dm_ref[:, 0:dc], lng_v, lnb_v)
            sg_ref[8 * SG_LNG:8 * SG_LNG + 8, :] += _rows8(da2 * ah)
            sg_ref[8 * SG_LNB:8 * SG_LNB + 8, :] += _rows8(da2)
            sg_ref[8 * SG_DWB:8 * SG_DWB + 8, :] += _rows8(da1)
            a0_ref[...] = p_ref[:, 0:dc] * _sigmoid(p_ref[:, dc:2 * dc])
            extd_ref[0:tm, :] = da1
            extd_ref[tm:tm + HALO, :] = cda_ref[...] * keep_next
            extd_ref[tm + HALO:tm + HALO + 8, :] = jnp.zeros((8, dc), F32)
            cda_ref[...] = da1[0:HALO, :]
            ds = dm_ref[:, dc:2 * dc]
            dproj_ref[:, 2 * dc:3 * dc] = (ds * q_ref[...]).astype(BF16)
            dq = ds * p_ref[:, 2 * dc:3 * dc]
            extq_ref[0:tm, :] = dq
            extq_ref[tm:tm + 8, :] = cdq_ref[...] * keep_next
            cdq_ref[...] = dq[0:8, :]
        units.append(head)
        for g in range(ng):
            def shift(g=g):
                for r in range(8):
                    ed_ref[g, r, 0:tm + HALO, :] = extd_ref[r:r + tm + HALO, 128 * g:128 * g + 128]
            units.append(shift)
        for g in range(ng):
            lanes = slice(128 * g, 128 * g + 128)
            accs = [None] * CONF_K
            for i0 in range(0, tm, rw):
                def conv(g=g, lanes=lanes, i0=i0, accs=accs):
                    a0v = a0_ref[i0:i0 + rw, lanes]
                    acc = jnp.zeros((rw, 128), F32)
                    for s in range(CONF_K):
                        m, r = divmod(s, 8)
                        e = ed_ref[g, r, i0 + 8 * m:i0 + 8 * m + rw, :]
                        acc = acc + e * wb_ref[CONF_K - 1 - s:CONF_K - s, lanes]
                        part = _rows8(e * a0v)
                        accs[s] = part if accs[s] is None else accs[s] + part
                    da0_ref[i0:i0 + rw, lanes] = acc
                units.append(conv)

            def flush(lanes=lanes, accs=accs):
                for s in range(CONF_K):
                    kk = CONF_K - 1 - s
                    sg_ref[8 * (SG_DWW + kk):8 * (SG_DWW + kk) + 8, lanes] += accs[s]
            units.append(flush)

        def emit():
            da0 = da0_ref[...]
            sig = _sigmoid(p_ref[:, dc:2 * dc])
            dproj_ref[:, 0:dc] = (da0 * sig).astype(BF16)
            dproj_ref[:, dc:2 * dc] = (da0 * a0_ref[...] * (1.0 - sig)).astype(BF16)
        units.append(emit)

        def short():
            scc, sch = p_ref[:, 3 * dc:4 * dc], p_ref[:, 4 * dc:5 * dc]
            pv = scc * sch
            dp = jnp.zeros((tm, dc), F32)
            for kk in range(SC_K):
                dqs = extq_ref[SC_K - 1 - kk:SC_K - 1 - kk + tm, :]
                dp = dp + dqs * scw_ref[kk:kk + 1, :]
                sg_ref[8 * (SG_SCW + kk):8 * (SG_SCW + kk) + 8, :] += _rows8(pv * dqs)
            dproj_ref[:, 3 * dc:4 * dc] = (dp * sch).astype(BF16)
            dproj_ref[:, 4 * dc:5 * dc] = (dp * scc).astype(BF16)
        units.append(short)
        return units

    def proj_units(first, dp_ref, w_ref, x_ref, dx1_ref, mod_ref, dh_ref, gx_ref, dsh_ref, dsc_ref):
        units = []
        for c0 in range(0, d, ncol):
            def chunk(c0=c0):
                dh_ref[:, c0:c0 + ncol] = _dot_nt(dp_ref[...], w_ref[c0:c0 + ncol, :])
            units.append(chunk)

        def tail():
            dh1 = dh_ref[...]
            xn, r1 = _rms(x_ref[...])
            v1, v2 = _rows8(dh1), _rows8(dh1 * xn)
            dsh_ref[...] = jnp.where(first, v1, dsh_ref[...] + v1)
            dsc_ref[...] = jnp.where(first, v2, dsc_ref[...] + v2)
            gx_ref[...] = dx1_ref[...] + _rms_bwd(dh1 * (1.0 + mod_ref[:, d:2 * d]), xn, r1)
        units.append(tail)
        return units

    def body(dyl_ref, dy_ref, wo_ref, a1_ref, q_ref, p_ref, wb_ref, lng_ref, lnb_ref, scw_ref,
             w_ref, x_ref, dx1_ref, mod_ref, dep_ref,
             dproj_ref, sg_ref, gx_ref, dsh_ref, dsc_ref,
             extd_ref, ed_ref, a0_ref, da0_ref, extq_ref, dp_ref, dh_ref, dm_ref, cda_ref, cdq_ref):
        k = pl.program_id(0)

        @pl.when(k == 0)
        def _():
            sg_ref[...] = jnp.zeros(sg_ref.shape, F32)
            dp_ref[...] = jnp.zeros(dp_ref.shape, BF16)
            cda_ref[...] = jnp.zeros(cda_ref.shape, F32)
            cdq_ref[...] = jnp.zeros(cdq_ref.shape, F32)
            _run_units(dmixed_units(dyl_ref, wo_ref, dm_ref))

        first = jnp.logical_or(k <= 1, (nt - k) % tps == tps - 1)
        after = lambda: proj_units(first, dp_ref, w_ref, x_ref, dx1_ref, mod_ref, dh_ref, gx_ref, dsh_ref, dsc_ref)

        @pl.when(k < nt)
        def _():
            _run_units(mix_units(nt - 1 - k, dm_ref, a1_ref, q_ref, p_ref, wb_ref, lng_ref, lnb_ref, scw_ref,
                                 dproj_ref, sg_ref, extd_ref, ed_ref, a0_ref, da0_ref, extq_ref, cda_ref, cdq_ref),
                       after(), dmixed_units(dy_ref, wo_ref, dm_ref))
            dp_ref[...] = dproj_ref[...]

        @pl.when(k == nt)
        def _():
            _run_units(after())

    cur = lambda k: jnp.maximum(nt - 1 - k, 0)
    prev = lambda k: jnp.minimum(nt - k, nt - 1)
    acc = jax.ShapeDtypeStruct((n // t, 8, d), F32)
    return pl.pallas_call(
        body, name="mix_in_bwd", grid=(nt + 1,),
        in_specs=[pl.BlockSpec((tm, d), lambda k: (nt - 1, 0)),
                  pl.BlockSpec((tm, d), lambda k: (jnp.maximum(nt - 2 - k, 0), 0)), _resident(w_out.shape),
                  pl.BlockSpec((tm, dc), lambda k: (cur(k), 0)),
                  pl.BlockSpec((tm, dc), lambda k: (cur(k), 0)),
                  pl.BlockSpec((tm, d_in), lambda k: (cur(k), 0)),
                  _const(wb.shape), _const(lng.shape), _const(lnb.shape), _const(scw.shape),
                  _resident((d, d_in)),
                  pl.BlockSpec((tm, d), lambda k: (prev(k), 0)), pl.BlockSpec((tm, d), lambda k: (prev(k), 0)),
                  pl.BlockSpec((None, 1, 6 * d), lambda k: (prev(k) // tps, 0, 0)), ANY],
        out_specs=[pl.BlockSpec((tm, d_in), lambda k: (cur(k), 0)), _const((8 * SG_N, dc)),
                   pl.BlockSpec((tm, d), lambda k: (prev(k), 0)),
                   pl.BlockSpec((None, 8, d), lambda k: (prev(k) // tps, 0, 0)),
                   pl.BlockSpec((None, 8, d), lambda k: (prev(k) // tps, 0, 0))],
        out_shape=[jax.ShapeDtypeStruct((n, d_in), BF16), jax.ShapeDtypeStruct((8 * SG_N, dc), F32),
                   jax.ShapeDtypeStruct((n, d), F32), acc, acc],
        scratch_shapes=[pltpu.VMEM((tm + HALO + 8, dc), F32), _shift_scratch(tm, dc),
                        pltpu.VMEM((tm, dc), F32), pltpu.VMEM((tm, dc), F32),
                        pltpu.VMEM((tm + 8, dc), F32),
                        pltpu.VMEM((tm, d_in), BF16), pltpu.VMEM((tm, d), F32), pltpu.VMEM((tm, 2 * dc), F32),
                        pltpu.VMEM((HALO, dc), F32), pltpu.VMEM((8, dc), F32)],
        compiler_params=pltpu.CompilerParams(dimension_semantics=("arbitrary",)),
    )(dy1, dy1, w_out, a1, q, proj, wb, lng, lnb, scw, w_in, x, dx1, mod3, dep)


SMALL_ROWS = 40


def _pack_small(ids, sg, ggf, loss, accs, d, dep):
    dc = d // 2
    nb = accs[0].shape[0]

    def body(ids_ref, sg_ref, ggf_ref, loss_ref, dsh1, dsc1, dg1, dsh2, dsc2, dg2, dep_ref, pack_ref, dmod_ref):
        pack_ref[...] = jnp.zeros(pack_ref.shape, F32)
        for k in range(SG_N):
            pack_ref[k:k + 1, :] = jnp.sum(sg_ref[8 * k:8 * k + 8, :], axis=0, keepdims=True)
        gf = jnp.sum(ggf_ref[...], axis=0, keepdims=True)
        pack_ref[SG_N:SG_N + 1, :] = gf[:, 0:dc]
        pack_ref[SG_N + 1:SG_N + 2, :] = gf[:, dc:d]
        tot = jnp.sum(jnp.sum(loss_ref[...], axis=0, keepdims=True), axis=1, keepdims=True) * (0.5 / d)
        pack_ref[SG_N + 2:SG_N + 3, :] = jnp.broadcast_to(tot, (1, dc))
        for b in range(nb):
            row = jnp.concatenate([jnp.sum(ref[b], axis=0, keepdims=True)
                                   for ref in (dsh1, dsc1, dg1, dsh2, dsc2, dg2)], axis=1)
            for f in range(fold):
                dmod_ref[b * fold + f:b * fold + f + 1, :] = row[:, f * wf:(f + 1) * wf]

    fold = 8 // nb
    wf = 6 * d // fold
    assert nb * fold == 8 and wf % 128 == 0
    whole = lambda a: pl.BlockSpec(a.shape, lambda i, ids: (0,) * a.ndim)
    mine = lambda r, c: pl.BlockSpec((None, r, c), lambda i, ids: (ids[2], 0, 0))
    return pl.pallas_call(
        body, name="pack_small",
        grid_spec=pltpu.PrefetchScalarGridSpec(
            num_scalar_prefetch=1, grid=(1,),
            in_specs=[whole(a) for a in (sg, ggf, loss, *accs)] + [ANY],
            out_specs=[mine(SMALL_ROWS, dc), mine(8, wf)]),
        out_shape=[jax.ShapeDtypeStruct((N_DEV, SMALL_ROWS, dc), F32), jax.ShapeDtypeStruct((N_DEV, 8, wf), F32)],
    )(ids, sg, ggf, loss, *accs, dep)


def _small_reduce(pack_all, dmod_all, nb, dep):
    def body(pk_ref, dm_ref, dep_ref, red_ref, dmod_ref, gb_ref):
        tot = pk_ref[0]
        for dev in range(1, N_DEV):
            tot = tot + pk_ref[dev]
        red_ref[...] = tot
        for f in range(fold):
            gb = jnp.zeros((1, wf), F32)
            for dev in range(N_DEV):
                for b in range(nb):
                    seg = dm_ref[dev, b * fold + f:b * fold + f + 1, :]
                    dmod_ref[dev * nb + b:dev * nb + b + 1, f * wf:(f + 1) * wf] = seg
                    gb = gb + seg
            gb_ref[:, f * wf:(f + 1) * wf] = gb

    fold = 8 // nb
    wf = dmod_all.shape[2]
    return pl.pallas_call(
        body, name="small_reduce",
        out_shape=[jax.ShapeDtypeStruct(pack_all.shape[1:], F32),
                   jax.ShapeDtypeStruct((N_DEV * nb, fold * wf), F32),
                   jax.ShapeDtypeStruct((1, fold * wf), F32)],
        in_specs=[VMEM] * 2 + [ANY], out_specs=[VMEM] * 3,
    )(pack_all, dmod_all, dep)


def _adam(w, g, m, v):
    m = ADAM_B1 * m + (1.0 - ADAM_B1) * g
    v = ADAM_B2 * v + (1.0 - ADAM_B2) * (g * g)
    m_hat = m / (1.0 - ADAM_B1 ** ADAM_STEP)
    v_hat = v / (1.0 - ADAM_B2 ** ADAM_STEP)
    delta = -ADAM_LR * (m_hat / (jnp.sqrt(v_hat) + ADAM_EPS) + ADAM_WD * w)
    return delta, m, v


def _adamw_big(w, g, m, v, name):
    r, c = w.shape
    tr = min(r, 256)

    def body(w_ref, g_ref, m_ref, v_ref, go_ref, d_ref, nm_ref, nv_ref):
        g = g_ref[...]
        go_ref[...] = g
        d_ref[...], nm_ref[...], nv_ref[...] = _adam(w_ref[...], g, m_ref[...], v_ref[...])

    s = jax.ShapeDtypeStruct((r, c), F32)
    return pl.pallas_call(
        body, name=name, grid=(r // tr,),
        in_specs=[_rowblk(tr, c)] * 4, out_specs=[_rowblk(tr, c)] * 4, out_shape=[s, s, s, s],
        compiler_params=pltpu.CompilerParams(dimension_semantics=("parallel",)),
    )(w, g, m, v)


def _adamw_ada(act_t, dmod_cols, w, m, v):
    r, c = w.shape
    tr = min(r, 256)
    nb = act_t.shape[1]

    def body(a_ref, dm_ref, w_ref, m_ref, v_ref, g_ref, d_ref, nm_ref, nv_ref):
        g = jnp.dot(a_ref[...], dm_ref[...], preferred_element_type=F32, precision=HIGHEST)
        g_ref[...] = g
        d_ref[...], nm_ref[...], nv_ref[...] = _adam(w_ref[...], g, m_ref[...], v_ref[...])

    s = jax.ShapeDtypeStruct((r, c), F32)
    return pl.pallas_call(
        body, name="adamw_w_ada", grid=(r // tr,),
        in_specs=[_rowblk(tr, nb), _const((nb, c))] + [_rowblk(tr, c)] * 3,
        out_specs=[_rowblk(tr, c)] * 4, out_shape=[s, s, s, s],
        compiler_params=pltpu.CompilerParams(dimension_semantics=("parallel",)),
    )(act_t, dmod_cols, w, m, v)


def _adamw_small(ids, red, g_bada, ws, ms, vs):
    n = len(ws)
    cw = ws[1].shape[1]

    def body(ids_ref, red_ref, gb_ref, *refs):
        cols = pl.ds(pl.multiple_of(ids_ref[0] * cw, 128), cw)
        gs = [gb_ref[...], red_ref[SG_DWW:SG_DWW + CONF_K, cols], red_ref[SG_DWB:SG_DWB + 1, :],
              red_ref[SG_LNG:SG_LNG + 1, :], red_ref[SG_LNB:SG_LNB + 1, :], red_ref[SG_SCW:SG_SCW + SC_K, cols],
              jnp.concatenate([red_ref[SG_N:SG_N + 1, :], red_ref[SG_N + 1:SG_N + 2, :]], axis=1)]
        for i in range(n):
            w, m, v = (refs[j * n + i][...] for j in range(3))
            dl, nm, nv = _adam(w, gs[i], m, v)
            refs[3 * n + i][...] = gs[i]
            refs[4 * n + i][...] = dl
            refs[5 * n + i][...] = nm
            refs[6 * n + i][...] = nv

    whole = lambda a: pl.BlockSpec(a.shape, lambda i, ids: (0,) * a.ndim)
    shapes = [jax.ShapeDtypeStruct(w.shape, F32) for w in ws]
    return pl.pallas_call(
        body, name="adamw_small",
        grid_spec=pltpu.PrefetchScalarGridSpec(
            num_scalar_prefetch=1, grid=(1,),
            in_specs=[whole(a) for a in (red, g_bada, *ws, *ms, *vs)],
            out_specs=[whole(a) for a in ws] * 4),
        out_shape=shapes * 4,
    )(ids, red, g_bada, *ws, *ms, *vs)


def kernel(x, c, w_ada, b_ada, w_in, conf_dw_w, conf_dw_b, conf_ln_g, conf_ln_b, sc_conv_w, w_out, w_mlp1, w_mlp2, g_final, loss_target, m_w_ada, m_b_ada, m_w_in, m_conf_dw_w, m_conf_dw_b, m_conf_ln_g, m_conf_ln_b, m_sc_conv_w, m_w_out, m_w_mlp1, m_w_mlp2, m_g_final, v_w_ada, v_b_ada, v_w_in, v_conf_dw_w, v_conf_dw_b, v_conf_ln_g, v_conf_ln_b, v_sc_conv_w, v_w_out, v_w_mlp1, v_w_mlp2, v_g_final):
    nb, t, d = x.shape
    n = nb * t
    dc = d // 2
    ada_w = w_ada.shape[2]
    ax, ay, ac = _me()
    chip = 2 * ax + ay
    dev = 2 * chip + ac
    ids = jnp.stack([chip, ac, dev]).astype(jnp.int32)

    lays = _wlayout(d)
    names = ("in", "out", "mlp1", "mlp2")
    fulls = _cast_place(ids, [w_in[0], w_out[0], w_mlp1[0], w_mlp2[0]], lays)

    c_pad = jnp.zeros((8, d), F32).at[0:nb].set(c)
    cw_pad = jnp.zeros((SMALL_ROWS, dc // N_CHIPS), F32)
    cw_pad = cw_pad.at[0:CONF_K].set(conf_dw_w[0]).at[HALO:HALO + SC_K].set(sc_conv_w[0])
    c_all8, cw_all8 = _all_gather8([c_pad, cw_pad], "gather_c")
    plan_i = _gather_chip_plan(lays[0:1])
    sems_i, bufs_i, tok_i = _copy_start("gather_in_start", [fulls[0]], 3, plan_i, [c_all8])
    c_all = c_all8[:, 0:nb].reshape(N_DEV * nb, d) + tok_i[0, 0]
    cw_full = jnp.concatenate([cw_all8[2 * k] for k in range(N_CHIPS)], axis=1)
    dww, scw = cw_full[0:CONF_K], cw_full[HALO:HALO + SC_K]
    b_cols = lax.dynamic_slice(b_ada, (0, chip * ada_w), (1, ada_w))
    c_act, mod_shard = _ada_mod(c_all, w_ada[0], b_cols)
    mod_all = _gather_mod(mod_shard)
    mod3 = lax.dynamic_slice(mod_all, (dev * nb, 0), (nb, 6 * d)).reshape(nb, 1, 6 * d)

    x2 = x.reshape(n, d)
    tgt = loss_target.reshape(n, d)
    (wf_in,) = _copy_wait("gather_in_wait", bufs_i, sems_i, plan_i, [mod3])
    (wf_in,) = _copy_blocking("gather_in_pass", [wf_in], 3, _gather_pass_plan(lays[0:1]))
    plan_o, plan_b, plan_p = _gather_direct_plan(lays[1:2]), _gather_chip_plan(lays[2:4]), _gather_pass_plan(lays[2:4])
    sems_o, bufs_o, tok_o = _copy_start("gather_out_start", [fulls[1]], 6, plan_o, [wf_in, mod3])
    sems_b, bufs_b, tok_b = _copy_start("gather_mlp_start", fulls[2:4], 6, plan_b, [tok_o])
    proj, h1, mixed, a1, q = _in_mix_fwd(x2, mod3, wf_in, dww, conf_dw_b, conf_ln_g, conf_ln_b, scw, t, tok_b)
    (wf_out,) = _copy_wait("gather_out_wait", bufs_o, sems_o, plan_o, [mixed])
    bufs_b = _copy_wait("gather_mlp_wait", bufs_b, sems_b, plan_b, [mixed])
    sems_p, bufs_p, tok_p = _copy_start("gather_pass_start", bufs_b, 6, plan_p)
    x1, y1, h2 = _out_proj(mixed, wf_out, x2, mod3, t, tok_p)
    wf_1, wf_2 = _copy_wait("gather_pass_wait", bufs_p, sems_p, plan_p, [h2])
    z, dx2, dy2, dg2, ggf, loss_p = _mlp_fwd(h2, x1, tgt, mod3, g_final.reshape(1, d), wf_1, wf_2, t)

    dz, dx1, dy1, dsh2, dsc2, dg1 = _mlp_bwd(dy2, z, x1, dx2, y1, mod3, wf_1, wf_2, t)
    g_w2 = _wgrad(z, dy2, "wgrad_mlp2", relu2=True)
    g_w1 = _wgrad(h2, dz, "wgrad_mlp1")
    made = {}

    def behind_pair_exchange(tok):
        made["g_wout"] = _wgrad(mixed, dy1, "wgrad_out", deps=[tok])
        return [made["g_wout"]]

    def behind_chip_exchange(tok):
        made["dproj"], made["sg"], made["grad_x"], made["dsh1"], made["dsc1"] = _mix_in_bwd(
            dy1, wf_out, a1, q, proj, dww, conf_ln_g, conf_ln_b, scw, wf_in, x2, dx1, mod3, t, tok)
        return [made["dproj"]]

    def behind_pair_share(tok):
        made["g_win"] = _wgrad(h1, made["dproj"], "wgrad_in", deps=[tok])
        return [made["g_win"]]

    gr_1, gr_2 = _reduce_scatter(ids, [g_w1, g_w2], lays[2:4], names[2:4], "m",
                                 (behind_pair_exchange, behind_chip_exchange, behind_pair_share))

    big = {}

    plan_g = _gather8_plan(2)

    def behind_pair_exchange_in(tok):
        packs = _pack_small(ids, made["sg"], ggf, loss_p, (made["dsh1"], made["dsc1"], dg1, dsh2, dsc2, dg2), d, tok)
        made["gather"] = _copy_start("gather_small_start", list(packs), 2 * (N_DEV - 1), plan_g)
        return [made["gather"][2]]

    def behind_chip_exchange_in(tok):
        sems_g, bufs_g, _ = made["gather"]
        gathered = _copy_wait("gather_small_wait", bufs_g, sems_g, plan_g, [tok])
        red, dmod_all, g_bada = _small_reduce(*gathered, nb, tok)
        dmod_cols = lax.dynamic_slice(dmod_all, (0, chip * ada_w), (N_DEV * nb, ada_w))
        made["ada"] = _adamw_ada(c_act.T, dmod_cols, w_ada[0], m_w_ada[0], v_w_ada[0])
        for nm_, g_, w_, m_, v_ in (("w_mlp1", gr_1, w_mlp1, m_w_mlp1, v_w_mlp1),
                                    ("w_mlp2", gr_2, w_mlp2, m_w_mlp2, v_w_mlp2)):
            big[nm_] = _adamw_big(w_[0], g_, m_[0], v_[0], "adamw_" + nm_)
        small_w = [b_ada, conf_dw_w[0], conf_dw_b, conf_ln_g, conf_ln_b, sc_conv_w[0], g_final.reshape(1, d)]
        small_m = [m_b_ada, m_conf_dw_w[0], m_conf_dw_b, m_conf_ln_g, m_conf_ln_b, m_sc_conv_w[0],
                   m_g_final.reshape(1, d)]
        small_v = [v_b_ada, v_conf_dw_w[0], v_conf_dw_b, v_conf_ln_g, v_conf_ln_b, v_sc_conv_w[0],
                   v_g_final.reshape(1, d)]
        made["upd"] = _adamw_small(ids, red, g_bada, small_w, small_m, small_v)
        made["loss"] = red[SG_N + 2, 0]
        return [big["w_mlp2"][0], made["upd"][0]]

    gr_in, gr_out = _reduce_scatter(ids, [made["g_win"], made["g_wout"]], lays[0:2], names[0:2], "i",
                                    (behind_pair_exchange_in, behind_chip_exchange_in, None))
    big["w_in"] = _adamw_big(w_in[0], gr_in, m_w_in[0], v_w_in[0], "adamw_w_in")
    big["w_out"] = _adamw_big(w_out[0], gr_out, m_w_out[0], v_w_out[0], "adamw_w_out")
    grad_x, upd, loss = made["grad_x"], made["upd"], made["loss"]
    g_wada, d_wada, nm_wada, nv_wada = made["ada"]
    ns = len(upd) // 4
    small_g, s_delta, s_m, s_v = upd[0:ns], upd[ns:2 * ns], upd[2 * ns:3 * ns], upd[3 * ns:4 * ns]

    def outs(kind_big, kind_small, wada):
        sm = kind_small
        return (wada[None], sm[0], kind_big["w_in"][None], sm[1][None], sm[2], sm[3], sm[4], sm[5][None],
                kind_big["w_out"][None], kind_big["w_mlp1"][None], kind_big["w_mlp2"][None], sm[6].reshape(d))

    grads_out = outs({k: v[0] for k, v in big.items()}, small_g, g_wada)
    delta_out = outs({k: v[1] for k, v in big.items()}, s_delta, d_wada)
    m_out = outs({k: v[2] for k, v in big.items()}, s_m, nm_wada)
    v_out = outs({k: v[3] for k, v in big.items()}, s_v, nv_wada)
    return (loss, grad_x.reshape(nb, t, d), *grads_out, *delta_out, *m_out, *v_out)
```

```python
import functools

import jax
import jax.numpy as jnp
from jax import lax
from jax.experimental import pallas as pl
from jax.experimental.pallas import tpu as pltpu

F32 = jnp.float32
BF16 = jnp.bfloat16
MESH = pl.DeviceIdType.MESH
HIGHEST = lax.Precision.HIGHEST

EPS = 1e-6
CONF_K = 31
SC_K = 3
HALO = 32
N_CHIPS = 4
N_DEV = 8

ADAM_LR = 0.001
ADAM_B1 = 0.9
ADAM_B2 = 0.999
ADAM_EPS = 1e-08
ADAM_WD = 0.01
ADAM_STEP = 10

TM_MM = 1024
TM_MIX = 256
TM_MLP = 512
FF_CHUNK = 1024
MLP_SUB = 2
TK_WG = 2048
RB_CONV = 64
RB_WG = 32
CHIP_RELS = ((1, 0), (0, 1), (1, 1))

ANY = pl.BlockSpec(memory_space=pl.ANY)
VMEM = pl.BlockSpec(memory_space=pltpu.VMEM)
HBM = pl.BlockSpec(memory_space=pltpu.HBM)
SEM = pl.BlockSpec(memory_space=pltpu.SEMAPHORE)
EFFECT = pltpu.SideEffectType.DATAFLOW_SIDE_EFFECTING


def _me():
    return lax.axis_index("x"), lax.axis_index("y"), lax.axis_index("c")


def _flip(v, f):
    return 1 - v if f else v


def _rows8(v):
    r, c = v.shape
    return v.reshape(r // 8, 8, c).sum(axis=0)


def _rms(x):
    r = lax.rsqrt(jnp.mean(x * x, axis=-1, keepdims=True) + EPS)
    return x * r, r


def _rms_bwd(dxn, xn, r):
    return r * (dxn - xn * jnp.mean(dxn * xn, axis=-1, keepdims=True))


def _sigmoid(x):
    return 1.0 / (1.0 + jnp.exp(-x))


def _dot(a, b):
    return jnp.dot(a, b, preferred_element_type=F32)


def _dot_nt(a, b):
    return lax.dot_general(a, b, (((1,), (1,)), ((), ())), preferred_element_type=F32)


def _dot_tn(a, b):
    return lax.dot_general(a, b, (((0,), (0,)), ((), ())), preferred_element_type=F32)


def _const(shape):
    nd = len(shape)
    return pl.BlockSpec(shape, lambda i: (0,) * nd)


def _resident(shape):
    nd = len(shape)
    return pl.BlockSpec(shape, lambda i: (0,) * nd, pipeline_mode=pl.Buffered(1))


def _rowblk(tm, c):
    return pl.BlockSpec((tm, c), lambda i: (i, 0))


def _modspec(tps, width):
    return pl.BlockSpec((None, 1, width), lambda i: (i // tps, 0, 0))


def _accspec(tps, c):
    return pl.BlockSpec((None, 8, c), lambda i: (i // tps, 0, 0))


def _acc_add(ref, first, v):
    @pl.when(first)
    def _():
        ref[...] = v

    @pl.when(jnp.logical_not(first))
    def _():
        ref[...] += v


def _all_gather8(arrs, name):
    n = len(arrs)

    def body(*refs):
        ins, outs = refs[:n], refs[n:2 * n]
        send, recv = refs[2 * n:]
        x, y, c = _me()
        dev = 4 * x + 2 * y + c
        for a in range(n):
            outs[a][dev] = ins[a][...]
        sends = []
        for r in range(1, N_DEV):
            fx, fy, fc = (r >> 2) & 1, (r >> 1) & 1, r & 1
            peer = (_flip(x, fx), _flip(y, fy), _flip(c, fc))
            for a in range(n):
                cp = pltpu.make_async_remote_copy(
                    src_ref=ins[a], dst_ref=outs[a].at[dev],
                    send_sem=send.at[r - 1, a], recv_sem=recv.at[r - 1, a],
                    device_id=peer, device_id_type=MESH)
                cp.start()
                sends.append(cp)
        for r in range(1, N_DEV):
            fx, fy, fc = (r >> 2) & 1, (r >> 1) & 1, r & 1
            pdev = 4 * _flip(x, fx) + 2 * _flip(y, fy) + _flip(c, fc)
            for a in range(n):
                pltpu.make_async_remote_copy(
                    src_ref=ins[a], dst_ref=outs[a].at[pdev],
                    send_sem=send.at[r - 1, a], recv_sem=recv.at[r - 1, a],
                    device_id=(x, y, c), device_id_type=MESH).wait_recv()
        for cp in sends:
            cp.wait_send()

    return pl.pallas_call(
        body, name=name,
        out_shape=[jax.ShapeDtypeStruct((N_DEV,) + a.shape, a.dtype) for a in arrs],
        in_specs=[VMEM] * n, out_specs=[VMEM] * n,
        scratch_shapes=[pltpu.SemaphoreType.DMA((N_DEV - 1, n)),
                        pltpu.SemaphoreType.DMA((N_DEV - 1, n))],
    )(*arrs)


def _gather_mod(mod_shard):
    nb, w = mod_shard.shape

    def body(in_ref, out_ref, send, recv):
        x, y, c = _me()
        chip = 2 * x + y
        out_ref[:, pl.ds(pl.multiple_of(chip * w, 128), w)] = in_ref[...]
        sends = []
        for r, (fx, fy) in enumerate(CHIP_RELS):
            cp = pltpu.make_async_remote_copy(
                src_ref=in_ref,
                dst_ref=out_ref.at[:, pl.ds(pl.multiple_of(chip * w, 128), w)],
                send_sem=send.at[r], recv_sem=recv.at[r],
                device_id=(_flip(x, fx), _flip(y, fy), c), device_id_type=MESH)
            cp.start()
            sends.append(cp)
        for r, (fx, fy) in enumerate(CHIP_RELS):
            pchip = 2 * _flip(x, fx) + _flip(y, fy)
            pltpu.make_async_remote_copy(
                src_ref=in_ref,
                dst_ref=out_ref.at[:, pl.ds(pl.multiple_of(pchip * w, 128), w)],
                send_sem=send.at[r], recv_sem=recv.at[r],
                device_id=(x, y, c), device_id_type=MESH).wait_recv()
        for cp in sends:
            cp.wait_send()

    return pl.pallas_call(
        body, name="gather_mod",
        out_shape=jax.ShapeDtypeStruct((nb, N_CHIPS * w), mod_shard.dtype),
        in_specs=[VMEM], out_specs=VMEM,
        scratch_shapes=[pltpu.SemaphoreType.DMA((3,)), pltpu.SemaphoreType.DMA((3,))],
    )(mod_shard)


def _wlayout(d):
    d_in = 5 * d // 2
    return (
        (d, d_in // N_CHIPS, True),
        (d // N_CHIPS, d, False),
        (d, 4 * d // N_CHIPS, True),
        (4 * d // N_CHIPS, d, False),
    )


def _full_shape(lay):
    r, c, by_col = lay
    return (r, c * N_CHIPS) if by_col else (r * N_CHIPS, c)


def _full_view(ref, lay, k, h):
    r, c, by_col = lay
    hr = r // 2
    if by_col:
        return ref.at[pl.ds(pl.multiple_of(h * hr, 16), hr), pl.ds(pl.multiple_of(k * c, 128), c)]
    return ref.at[pl.ds(pl.multiple_of(k * r + h * hr, 16), hr), :]


def _half_view(ref, lay, h):
    hr = lay[0] // 2
    return ref.at[pl.ds(pl.multiple_of(h * hr, 16), hr), :]


def _half_shape(lay):
    return (lay[0] // 2, lay[1])


def _hbm(a):
    return pltpu.with_memory_space_constraint(a, pltpu.HBM)


def _remote(src, dst, send, recv, idx, to):
    return lambda: pltpu.make_async_remote_copy(src_ref=src, dst_ref=dst, send_sem=send.at[idx], recv_sem=recv.at[idx],
                                                device_id=to, device_id_type=MESH)


def _copy_start(name, bufs, n_sems, plan, after=()):
    nb, na = len(bufs), len(after)

    def body(*refs):
        sends, _ = plan(refs[:nb], refs[nb + na], refs[nb + na + 1])
        for mk in sends:
            mk().start()
        refs[-1][...] = jnp.zeros((8, 128), F32)

    outs = pl.pallas_call(
        body, name=name,
        out_shape=(pltpu.SemaphoreType.DMA((n_sems,)), pltpu.SemaphoreType.DMA((n_sems,)))
        + tuple(pltpu.HBM(b.shape, b.dtype) for b in bufs) + (jax.ShapeDtypeStruct((8, 128), F32),),
        in_specs=(HBM,) * nb + (ANY,) * na, out_specs=(SEM, SEM) + (HBM,) * nb + (VMEM,),
        input_output_aliases={i: 2 + i for i in range(nb)},
        compiler_params=pltpu.CompilerParams(has_side_effects=EFFECT),
    )(*[_hbm(b) for b in bufs], *after)
    return (outs[0], outs[1]), list(outs[2:2 + nb]), outs[-1]


def _copy_wait(name, bufs, sems, plan, after):
    nb, na = len(bufs), len(after)

    def body(*refs):
        sends, recvs = plan(refs[:nb], refs[nb], refs[nb + 1])
        for mk in sends:
            mk().wait_send()
        for mk in recvs:
            mk().wait_recv()

    outs = pl.pallas_call(
        body, name=name,
        out_shape=tuple(pltpu.HBM(b.shape, b.dtype) for b in bufs),
        in_specs=(HBM,) * nb + (SEM, SEM) + (ANY,) * na, out_specs=(HBM,) * nb,
        input_output_aliases={i: i for i in range(nb)},
        compiler_params=pltpu.CompilerParams(has_side_effects=EFFECT),
    )(*bufs, *sems, *after)
    return list(outs)


def _copy_blocking(name, bufs, n_sems, plan, after=()):
    nb, na = len(bufs), len(after)

    def body(*refs):
        sends, recvs = plan(refs[:nb], refs[2 * nb + na], refs[2 * nb + na + 1])
        started = [mk() for mk in sends]
        for cp in started:
            cp.start()
        for mk in recvs:
            mk().wait_recv()
        for cp in started:
            cp.wait_send()

    return list(pl.pallas_call(
        body, name=name,
        out_shape=tuple(jax.ShapeDtypeStruct(b.shape, b.dtype) for b in bufs),
        in_specs=(ANY,) * (nb + na), out_specs=(ANY,) * nb,
        input_output_aliases={i: i for i in range(nb)},
        scratch_shapes=[pltpu.SemaphoreType.DMA((n_sems,)), pltpu.SemaphoreType.DMA((n_sems,))],
    )(*bufs, *after))


def _exchange(name, bufs, n_sems, plan, between):
    if between is None:
        return _copy_blocking(name, bufs, n_sems, plan)
    sems, bufs, tok = _copy_start(name + "_start", bufs, n_sems, plan)
    return _copy_wait(name + "_wait", bufs, sems, plan, between(tok))


def _gather_direct_plan(lays):
    def plan(full, send, recv):
        x, y, c = _me()
        chip = 2 * x + y
        sends, recvs = [], []
        for r, (fx, fy) in enumerate(CHIP_RELS):
            px, py = _flip(x, fx), _flip(y, fy)
            for i, lay in enumerate(lays):
                for q in range(2):
                    oc = _flip(c, q)
                    mine = _full_view(full[i], lay, chip, c)
                    idx = (r * len(lays) + i) * 2 + q
                    sends.append(_remote(mine, mine, send, recv, idx, (px, py, oc)))
                    theirs = _full_view(full[i], lay, 2 * px + py, oc)
                    recvs.append(_remote(theirs, theirs, send, recv, idx, (x, y, c)))
        return sends, recvs
    return plan


def _gather_chip_plan(lays):
    def plan(full, send, recv):
        x, y, c = _me()
        chip = 2 * x + y
        sends, recvs = [], []
        for r, (fx, fy) in enumerate(CHIP_RELS):
            px, py = _flip(x, fx), _flip(y, fy)
            for i, lay in enumerate(lays):
                mine = _full_view(full[i], lay, chip, c)
                sends.append(_remote(mine, mine, send, recv, r * len(lays) + i, (px, py, c)))
                theirs = _full_view(full[i], lay, 2 * px + py, c)
                recvs.append(_remote(theirs, theirs, send, recv, r * len(lays) + i, (x, y, c)))
        return sends, recvs
    return plan


def _gather_pass_plan(lays):
    def plan(full, send, recv):
        x, y, c = _me()
        sends, recvs = [], []
        for r, (fx, fy) in enumerate(CHIP_RELS):
            pchip = 2 * _flip(x, fx) + _flip(y, fy)
            for i, lay in enumerate(lays):
                landed = _full_view(full[i], lay, pchip, c)
                sends.append(_remote(landed, landed, send, recv, r * len(lays) + i, (x, y, 1 - c)))
                other = _full_view(full[i], lay, pchip, 1 - c)
                recvs.append(_remote(other, other, send, recv, r * len(lays) + i, (x, y, c)))
        return sends, recvs
    return plan


def _gather8_plan(na):
    def plan(bufs, send, recv):
        x, y, c = _me()
        dev = 4 * x + 2 * y + c
        sends, recvs = [], []
        for r in range(1, N_DEV):
            fx, fy, fc = (r >> 2) & 1, (r >> 1) & 1, r & 1
            px, py, pc = _flip(x, fx), _flip(y, fy), _flip(c, fc)
            for a in range(na):
                idx = (r - 1) * na + a
                sends.append(_remote(bufs[a].at[dev], bufs[a].at[dev], send, recv, idx, (px, py, pc)))
                theirs = bufs[a].at[4 * px + 2 * py + pc]
                recvs.append(_remote(theirs, theirs, send, recv, idx, (x, y, c)))
        return sends, recvs
    return plan


def _pair_exchange_plan(lays):
    nw = len(lays)

    def plan(bufs, send, recv):
        x, y, c = _me()
        sends, recvs = [], []
        for i, lay in enumerate(lays):
            for k in range(N_CHIPS):
                sends.append(_remote(_full_view(bufs[i], lay, k, 1 - c), bufs[nw + i].at[k],
                                     send, recv, i * N_CHIPS + k, (x, y, 1 - c)))
                recvs.append(_remote(_full_view(bufs[i], lay, k, c), bufs[nw + i].at[k],
                                     send, recv, i * N_CHIPS + k, (x, y, c)))
        return sends, recvs
    return plan


def _chip_exchange_plan(nw):
    def plan(bufs, send, recv):
        x, y, c = _me()
        sends, recvs = [], []
        for r, (fx, fy) in enumerate(CHIP_RELS):
            px, py = _flip(x, fx), _flip(y, fy)
            for i in range(nw):
                sends.append(_remote(bufs[i].at[2 * px + py], bufs[nw + i].at[r], send, recv, r * nw + i, (px, py, c)))
                recvs.append(_remote(bufs[i].at[2 * px + py], bufs[nw + i].at[r], send, recv, r * nw + i, (x, y, c)))
        return sends, recvs
    return plan


def _pair_share_plan(lays):
    def plan(bufs, send, recv):
        x, y, c = _me()
        sends, recvs = [], []
        for i, lay in enumerate(lays):
            mine = _half_view(bufs[i], lay, c)
            sends.append(_remote(mine, mine, send, recv, i, (x, y, 1 - c)))
            other = _half_view(bufs[i], lay, 1 - c)
            recvs.append(_remote(other, other, send, recv, i, (x, y, c)))
        return sends, recvs
    return plan


def _pair_sum(ids, g, got, lay, name):
    r, c, by_col = lay
    hr = r // 2
    tr = min(hr, 256)
    nt = hr // tr

    def body(ids_ref, g_ref, got_ref, s32_ref, s16_ref):
        s = g_ref[...] + got_ref[...]
        s16_ref[...] = s.astype(BF16)

        @pl.when(pl.program_id(1) == ids_ref[0])
        def _():
            s32_ref[...] = s

    if by_col:
        gspec = pl.BlockSpec((tr, c), lambda t, k, ids: (ids[1] * nt + t, k))
    else:
        gspec = pl.BlockSpec((tr, c), lambda t, k, ids: ((2 * k + ids[1]) * nt + t, 0))
    hspec = pl.BlockSpec((None, tr, c), lambda t, k, ids: (k, t, 0))
    return pl.pallas_call(
        body, name=name,
        grid_spec=pltpu.PrefetchScalarGridSpec(
            num_scalar_prefetch=1, grid=(nt, N_CHIPS),
            in_specs=[gspec, hspec], out_specs=[pl.BlockSpec((tr, c), lambda t, k, ids: (t, 0)), hspec]),
        out_shape=[jax.ShapeDtypeStruct((hr, c), F32),
                   jax.ShapeDtypeStruct((N_CHIPS, hr, c), BF16)],
    )(ids, g, got)


def _chip_sum(ids, s32, got, lay, name):
    hr, c = _half_shape(lay)
    tr = min(hr, 256)
    nt = hr // tr

    def body(ids_ref, s_ref, got_ref, out_ref):
        t = s_ref[...]
        for r in range(3):
            t = t + got_ref[r].astype(F32)
        out_ref[...] = t

    return pl.pallas_call(
        body, name=name,
        grid_spec=pltpu.PrefetchScalarGridSpec(
            num_scalar_prefetch=1, grid=(nt,),
            in_specs=[pl.BlockSpec((tr, c), lambda t, ids: (t, 0)),
                      pl.BlockSpec((3, tr, c), lambda t, ids: (0, t, 0))],
            out_specs=pl.BlockSpec((tr, c), lambda t, ids: (ids[1] * nt + t, 0))),
        out_shape=jax.ShapeDtypeStruct((2 * hr, c), F32),
    )(ids, s32, got)


def _reduce_scatter(ids, grads, lays, names, tag, between):
    nw = len(lays)
    got1 = [lax.empty((N_CHIPS,) + _half_shape(l), F32) for l in lays]
    bufs = _exchange("pair_exchange_" + tag, list(grads) + got1, nw * N_CHIPS, _pair_exchange_plan(lays), between[0])
    sums = [_pair_sum(ids, bufs[i], bufs[nw + i], lays[i], "pair_sum_" + names[i]) for i in range(nw)]
    got2 = [lax.empty((3,) + _half_shape(l), BF16) for l in lays]
    bufs = _exchange("chip_exchange_" + tag, [s[1] for s in sums] + got2, 3 * nw, _chip_exchange_plan(nw), between[1])
    mine = [_chip_sum(ids, sums[i][0], bufs[nw + i], lays[i], "chip_sum_" + names[i]) for i in range(nw)]
    return _exchange("pair_share_" + tag, mine, nw, _pair_share_plan(lays), between[2])


def _cast_place(ids, ws, lays):
    nw = len(ws)

    def body(ids_ref, *refs):
        for i in range(nw):
            refs[nw + i][...] = refs[i][...].astype(BF16)

    by_col_map = lambda i, ids: (0, ids[0])
    by_row_map = lambda i, ids: (ids[0], 0)
    return pl.pallas_call(
        body, name="cast_place",
        grid_spec=pltpu.PrefetchScalarGridSpec(
            num_scalar_prefetch=1, grid=(1,),
            in_specs=[pl.BlockSpec(l[:2], lambda i, ids: (0, 0)) for l in lays],
            out_specs=[pl.BlockSpec(l[:2], by_col_map if l[2] else by_row_map) for l in lays]),
        out_shape=[jax.ShapeDtypeStruct(_full_shape(l), BF16) for l in lays],
    )(ids, *ws)


def _ada_mod(c_all, w_ada, b_ada):
    def body(c_ref, w_ref, b_ref, act_ref, mod_ref):
        cv = c_ref[...]
        act = cv * _sigmoid(cv)
        act_ref[...] = act
        mod_ref[...] = jnp.dot(act, w_ref[...], preferred_element_type=F32, precision=HIGHEST) + b_ref[...]

    nb = c_all.shape[0]
    return pl.pallas_call(
        body, name="ada_mod",
        out_shape=[jax.ShapeDtypeStruct(c_all.shape, F32),
                   jax.ShapeDtypeStruct((nb, w_ada.shape[1]), F32)],
        in_specs=[VMEM] * 3, out_specs=[VMEM] * 2,
    )(c_all, w_ada, b_ada)


def _run_units(*unit_lists):
    total = max(len(u) for u in unit_lists)
    done = [0] * len(unit_lists)
    for step in range(1, total + 1):
        for li, units in enumerate(unit_lists):
            upto = (step * len(units) + total - 1) // total
            while done[li] < upto:
                units[done[li]]()
                done[li] += 1


def _in_mix_fwd(x, mod3, w_in, wb, dwb, lng, lnb, scw, t, dep):
    n, d = x.shape
    d_in = w_in.shape[1]
    dc = d_in // 5
    tm = min(TM_MIX, t)
    tps = t // tm
    nt = n // tm
    rb = min(RB_CONV, tm)
    ncol = 256
    ng = dc // 128
    assert tps % 2 == 0 and nt % 2 == 0

    def proj_units(x_ref, mod_ref, w_ref, p_ref, h_ref):
        def head():
            xn, _ = _rms(x_ref[...])
            h_ref[...] = (xn * (1.0 + mod_ref[:, d:2 * d]) + mod_ref[:, 0:d]).astype(BF16)
        units = [head]
        for c0 in range(0, d_in, ncol):
            def chunk(c0=c0):
                p_ref[:, c0:c0 + ncol] = _dot(h_ref[...], w_ref[:, c0:c0 + ncol])
            units.append(chunk)
        return units

    def mix_units(first, p_ref, h_ref, r0, wb_ref, dwb_ref, lng_ref, lnb_ref, scw_ref,
                  proj_ref, h1_ref, mixed_ref, a1_ref, q_ref, ext_ref, e_ref, extp_ref, a1s_ref):
        rows = slice(r0, r0 + tm)
        units = []

        def glu():
            halo = ext_ref[tm:tm + HALO, :]
            ext_ref[0:HALO, :] = halo if first is False else jnp.where(first, 0.0, halo)
            ext_ref[HALO:HALO + tm, :] = p_ref[:, 0:dc] * _sigmoid(p_ref[:, dc:2 * dc])
        units.append(glu)
        for g in range(ng):
            def shift(g=g):
                for r in range(8):
                    e_ref[g, r, 0:tm + HALO, :] = ext_ref[r:r + tm + HALO, 128 * g:128 * g + 128]
            units.append(shift)
        for g in range(ng):
            lanes = slice(128 * g, 128 * g + 128)
            for i0 in range(0, tm, rb):
                def conv(g=g, lanes=lanes, i0=i0):
                    acc = jnp.zeros((rb, 128), F32)
                    for k in range(CONF_K):
                        m, r = divmod(k + HALO - CONF_K + 1, 8)
                        acc = acc + e_ref[g, r, i0 + 8 * m:i0 + 8 * m + rb, :] * wb_ref[k:k + 1, lanes]
                    a1s_ref[i0:i0 + rb, lanes] = acc + dwb_ref[:, lanes]
                units.append(conv)

        def norm():
            a1 = a1s_ref[...]
            a1_ref[rows, :] = a1
            mu = jnp.mean(a1, axis=-1, keepdims=True)
            ac = a1 - mu
            rstd = lax.rsqrt(jnp.mean(ac * ac, axis=-1, keepdims=True) + EPS)
            a2 = ac * rstd * lng_ref[...] + lnb_ref[...]
            mixed_ref[rows, 0:dc] = (a2 * _sigmoid(a2)).astype(BF16)
        units.append(norm)

        def short():
            halo = extp_ref[tm:tm + 8, :]
            extp_ref[0:8, :] = halo if first is False else jnp.where(first, 0.0, halo)
            extp_ref[8:8 + tm, :] = p_ref[:, 3 * dc:4 * dc] * p_ref[:, 4 * dc:5 * dc]
            q = jnp.zeros((tm, dc), F32)
            for k in range(SC_K):
                q = q + extp_ref[6 + k:6 + k + tm, :] * scw_ref[k:k + 1, :]
            q_ref[rows, :] = q
            mixed_ref[rows, dc:2 * dc] = (p_ref[:, 2 * dc:3 * dc] * q).astype(BF16)
        units.append(short)

        def keep():
            proj_ref[rows, :] = p_ref[...]
            h1_ref[rows, :] = h_ref[...]
        units.append(keep)
        return units

    def body(x0_ref, xa_ref, xb_ref, mod0_ref, moda_ref, modb_ref, w_ref,
             wb_ref, dwb_ref, lng_ref, lnb_ref, scw_ref, dep_ref,
             proj_ref, h1_ref, mixed_ref, a1_ref, q_ref,
             p0_ref, p1_ref, h0_ref, hh1_ref, ext_ref, e_ref, extp_ref, a1s_ref):
        j = pl.program_id(0)

        @pl.when(j == 0)
        def _():
            ext_ref[...] = jnp.zeros(ext_ref.shape, F32)
            extp_ref[...] = jnp.zeros(extp_ref.shape, F32)
            _run_units(proj_units(x0_ref, mod0_ref, w_ref, p0_ref, h0_ref))

        common = (wb_ref, dwb_ref, lng_ref, lnb_ref, scw_ref, proj_ref, h1_ref, mixed_ref, a1_ref, q_ref,
                  ext_ref, e_ref, extp_ref, a1s_ref)
        _run_units(mix_units((2 * j) % tps == 0, p0_ref, h0_ref, 0, *common),
                   proj_units(xa_ref, moda_ref, w_ref, p1_ref, hh1_ref))
        _run_units(mix_units(False, p1_ref, hh1_ref, tm, *common),
                   proj_units(xb_ref, modb_ref, w_ref, p0_ref, h0_ref))

    last = nt - 1
    xspec = lambda f: pl.BlockSpec((tm, d), lambda j: (f(j), 0))
    mspec = lambda f: pl.BlockSpec((None, 1, 6 * d), lambda j: (f(j) // tps, 0, 0))
    out2 = lambda c: pl.BlockSpec((2 * tm, c), lambda j: (j, 0))
    return pl.pallas_call(
        body, name="in_mix_fwd", grid=(nt // 2,),
        in_specs=[xspec(lambda j: 0), xspec(lambda j: 2 * j + 1), xspec(lambda j: jnp.minimum(2 * j + 2, last)),
                  mspec(lambda j: 0), mspec(lambda j: 2 * j + 1), mspec(lambda j: jnp.minimum(2 * j + 2, last)),
                  _resident((d, d_in)),
                  _const(wb.shape), _const(dwb.shape), _const(lng.shape), _const(lnb.shape), _const(scw.shape), ANY],
        out_specs=[out2(d_in), out2(d), out2(2 * dc), out2(dc), out2(dc)],
        out_shape=[jax.ShapeDtypeStruct((n, d_in), F32), jax.ShapeDtypeStruct((n, d), BF16),
                   jax.ShapeDtypeStruct((n, 2 * dc), BF16), jax.ShapeDtypeStruct((n, dc), F32),
                   jax.ShapeDtypeStruct((n, dc), F32)],
        scratch_shapes=[pltpu.VMEM((tm, d_in), F32), pltpu.VMEM((tm, d_in), F32),
                        pltpu.VMEM((tm, d), BF16), pltpu.VMEM((tm, d), BF16),
                        pltpu.VMEM((tm + HALO + 8, dc), F32), _shift_scratch(tm, dc),
                        pltpu.VMEM((tm + 8, dc), F32), pltpu.VMEM((tm, dc), F32)],
        compiler_params=pltpu.CompilerParams(dimension_semantics=("arbitrary",)),
    )(x, x, x, mod3, mod3, mod3, w_in, wb, dwb, lng, lnb, scw, dep)


def _shift_scratch(tm, dc):
    return pltpu.VMEM((dc // 128, 8, tm + HALO + 8, 128), F32)


def _out_proj(mixed, w_out, x, mod3, t, dep):
    n, d = x.shape
    tm = min(TM_MM, t)
    tps = t // tm

    def body(m_ref, w_ref, x_ref, mod_ref, dep_ref, x1_ref, y1_ref, h2_ref):
        y1 = _dot(m_ref[...], w_ref[...])
        y1_ref[...] = y1.astype(BF16)
        x1 = x_ref[...] + mod_ref[:, 2 * d:3 * d] * y1
        x1_ref[...] = x1
        xn, _ = _rms(x1)
        h2_ref[...] = (xn * (1.0 + mod_ref[:, 4 * d:5 * d]) + mod_ref[:, 3 * d:4 * d]).astype(BF16)

    return pl.pallas_call(
        body, name="out_proj", grid=(n // tm,),
        in_specs=[_rowblk(tm, d), _resident((d, d)), _rowblk(tm, d), _modspec(tps, 6 * d), ANY],
        out_specs=[_rowblk(tm, d), _rowblk(tm, d), _rowblk(tm, d)],
        out_shape=[jax.ShapeDtypeStruct((n, d), F32), jax.ShapeDtypeStruct((n, d), BF16),
                   jax.ShapeDtypeStruct((n, d), BF16)],
        compiler_params=pltpu.CompilerParams(dimension_semantics=("parallel",)),
    )(mixed, w_out, x, mod3, dep)


def _mlp_fwd(h2, x1, tgt, mod3, gfin, w1, w2, t):
    n, d = x1.shape
    dff = w1.shape[1]
    tm = min(TM_MLP, t)
    tps = t // tm
    nt = n // tm

    def body(h_ref, x1_ref, tg_ref, mod_ref, gf_ref, w1_ref, w2_ref,
             z_ref, dx2_ref, dy2_ref, dg2_ref, ggf_ref, loss_ref):
        i = pl.program_id(0)
        g2 = mod_ref[:, 5 * d:6 * d]
        gf = gf_ref[...]
        sub = tm // MLP_SUB
        sums = None
        for part in range(MLP_SUB):
            rs = slice(part * sub, (part + 1) * sub)
            hv = h_ref[rs, :]
            y2 = jnp.zeros((sub, d), F32)
            for j in range(dff // FF_CHUNK):
                cols = slice(j * FF_CHUNK, (j + 1) * FF_CHUNK)
                z = _dot(hv, w1_ref[:, cols])
                z_ref[rs, cols] = z.astype(BF16)
                zr = jnp.maximum(z, 0.0)
                y2 = y2 + _dot((zr * zr).astype(BF16), w2_ref[cols, :])
            x2n, r3 = _rms(x1_ref[rs, :] + g2 * y2)
            diff = x2n * gf - tg_ref[rs, :]
            dout = diff * (1.0 / d)
            dx2 = _rms_bwd(dout * gf, x2n, r3)
            dx2_ref[rs, :] = dx2
            dy2_ref[rs, :] = (g2 * dx2).astype(BF16)
            p = (_rows8(dx2 * y2), _rows8(dout * x2n), _rows8(diff * diff))
            sums = p if sums is None else tuple(a + b for a, b in zip(sums, p))
        _acc_add(dg2_ref, i % tps == 0, sums[0])
        _acc_add(ggf_ref, i == 0, sums[1])
        _acc_add(loss_ref, i == 0, sums[2])

    return pl.pallas_call(
        body, name="mlp_fwd", grid=(nt,),
        in_specs=[_rowblk(tm, d), _rowblk(tm, d), _rowblk(tm, d), _modspec(tps, 6 * d), _const((1, d)),
                  _resident((d, dff)), _resident((dff, d))],
        out_specs=[_rowblk(tm, dff), _rowblk(tm, d), _rowblk(tm, d), _accspec(tps, d),
                   _const((8, d)), _const((8, d))],
        out_shape=[jax.ShapeDtypeStruct((n, dff), BF16), jax.ShapeDtypeStruct((n, d), F32),
                   jax.ShapeDtypeStruct((n, d), BF16), jax.ShapeDtypeStruct((n // t, 8, d), F32),
                   jax.ShapeDtypeStruct((8, d), F32), jax.ShapeDtypeStruct((8, d), F32)],
        compiler_params=pltpu.CompilerParams(dimension_semantics=("arbitrary",)),
    )(h2, x1, tgt, mod3, gfin, w1, w2)


def _mlp_bwd(dy2, z, x1, dx2, y1, mod3, w1, w2, t):
    n, d = x1.shape
    dff = w1.shape[1]
    tm = min(TM_MLP, t)
    tps = t // tm

    def body(dy2_ref, z_ref, x1_ref, dx2_ref, y1_ref, mod_ref, w1_ref, w2_ref,
             dz_ref, dx1_ref, dy1_ref, dsh_ref, dsc_ref, dg1_ref):
        first = pl.program_id(0) % tps == 0
        sub = tm // MLP_SUB
        sums = None
        for part in range(MLP_SUB):
            rs = slice(part * sub, (part + 1) * sub)
            dy2 = dy2_ref[rs, :]
            dh2 = jnp.zeros((sub, d), F32)
            for j in range(dff // FF_CHUNK):
                cols = slice(j * FF_CHUNK, (j + 1) * FF_CHUNK)
                du = _dot_nt(dy2, w2_ref[cols, :])
                dz = (du * (2.0 * jnp.maximum(z_ref[rs, cols].astype(F32), 0.0))).astype(BF16)
                dz_ref[rs, cols] = dz
                dh2 = dh2 + _dot_nt(dz, w1_ref[:, cols])
            x1n, r2 = _rms(x1_ref[rs, :])
            dx1 = dx2_ref[rs, :] + _rms_bwd(dh2 * (1.0 + mod_ref[:, 4 * d:5 * d]), x1n, r2)
            dx1_ref[rs, :] = dx1
            dy1_ref[rs, :] = (mod_ref[:, 2 * d:3 * d] * dx1).astype(BF16)
            p = (_rows8(dh2), _rows8(dh2 * x1n), _rows8(dx1 * y1_ref[rs, :].astype(F32)))
            sums = p if sums is None else tuple(a + b for a, b in zip(sums, p))
        _acc_add(dsh_ref, first, sums[0])
        _acc_add(dsc_ref, first, sums[1])
        _acc_add(dg1_ref, first, sums[2])

    acc = jax.ShapeDtypeStruct((n // t, 8, d), F32)
    return pl.pallas_call(
        body, name="mlp_bwd", grid=(n // tm,),
        in_specs=[_rowblk(tm, d), _rowblk(tm, dff), _rowblk(tm, d), _rowblk(tm, d), _rowblk(tm, d),
                  _modspec(tps, 6 * d), _resident((d, dff)), _resident((dff, d))],
        out_specs=[_rowblk(tm, dff), _rowblk(tm, d), _rowblk(tm, d),
                   _accspec(tps, d), _accspec(tps, d), _accspec(tps, d)],
        out_shape=[jax.ShapeDtypeStruct((n, dff), BF16), jax.ShapeDtypeStruct((n, d), F32),
                   jax.ShapeDtypeStruct((n, d), BF16), acc, acc, acc],
        compiler_params=pltpu.CompilerParams(dimension_semantics=("arbitrary",)),
    )(dy2, z, x1, dx2, y1, mod3, w1, w2)


def _wgrad(a, b, name, relu2=False, bn=None, deps=()):
    n, ka = a.shape
    nb = b.shape[1]
    tk = min(TK_WG, n)
    bm = min(ka, 1024)
    if bn is None:
        bn = nb if nb <= 2048 else nb // 2

    def body(a_ref, b_ref, *rest):
        out_ref = rest[-1]
        av = a_ref[...]
        if relu2:
            ar = jnp.maximum(av, 0.0)
            av = ar * ar
        p = _dot_tn(av, b_ref[...])
        _acc_add(out_ref, pl.program_id(2) == 0, p)

    return pl.pallas_call(
        body, name=name, grid=(ka // bm, nb // bn, n // tk),
        in_specs=[pl.BlockSpec((tk, bm), lambda i, j, k: (k, i)),
                  pl.BlockSpec((tk, bn), lambda i, j, k: (k, j))] + [ANY] * len(deps),
        out_specs=pl.BlockSpec((bm, bn), lambda i, j, k: (i, j)),
        out_shape=jax.ShapeDtypeStruct((ka, nb), F32),
        compiler_params=pltpu.CompilerParams(dimension_semantics=("parallel", "parallel", "arbitrary")),
    )(a, b, *deps)


SG_DWW = 0
SG_DWB = CONF_K
SG_LNG = CONF_K + 1
SG_LNB = CONF_K + 2
SG_SCW = CONF_K + 3
SG_N = CONF_K + 3 + SC_K


def _mix_in_bwd(dy1, w_out, a1, q, proj, wb, lng, lnb, scw, w_in, x, dx1, mod3, t, dep):
    n, d_in = proj.shape
    d = x.shape[1]
    dc = d_in // 5
    tm = min(TM_MIX, t)
    tps = t // tm
    nt = n // tm
    rw = min(RB_WG, tm)
    ng = dc // 128
    ncol = 256

    def ln_bwd(a1v, da3, lng_v, lnb_v):
        mu = jnp.mean(a1v, axis=-1, keepdims=True)
        ac = a1v - mu
        rstd = lax.rsqrt(jnp.mean(ac * ac, axis=-1, keepdims=True) + EPS)
        ah = ac * rstd
        a2 = ah * lng_v + lnb_v
        s2 = _sigmoid(a2)
        da2 = da3 * (s2 * (1.0 + a2 * (1.0 - s2)))
        dah = da2 * lng_v
        da1 = rstd * (dah - jnp.mean(dah, axis=-1, keepdims=True)
                      - ah * jnp.mean(dah * ah, axis=-1, keepdims=True))
        return da1, da2, ah

    def dmixed_units(dy_ref, wo_ref, dm_ref):
        units = []
        for c0 in range(0, 2 * dc, ncol):
            def chunk(c0=c0):
                dm_ref[:, c0:c0 + ncol] = _dot_nt(dy_ref[...], wo_ref[c0:c0 + ncol, :])
            units.append(chunk)
        return units

    def mix_units(k, dm_ref, a1_ref, q_ref, p_ref, wb_ref, lng_ref, lnb_ref, scw_ref,
                  dproj_ref, sg_ref, extd_ref, ed_ref, a0_ref, da0_ref, extq_ref, cda_ref, cdq_ref):
        keep_next = jnp.where(k % tps == tps - 1, 0.0, 1.0)
        units = []

        def head():
            lng_v, lnb_v = lng_ref[...], lnb_ref[...]
            da1, da2, ah = ln_bwd(a1_ref[...], dm_ref[:, 0:dc], lng_v, lnb_v)
            sg_ref[8 * SG_LNG:8 * SG_LNG + 8, :] += _rows8(da2 * ah)
            sg_ref[8 * SG_LNB:8 * SG_LNB + 8, :] += _rows8(da2)
            sg_ref[8 * SG_DWB:8 * SG_DWB + 8, :] += _rows8(da1)
            a0_ref[...] = p_ref[:, 0:dc] * _sigmoid(p_ref[:, dc:2 * dc])
            extd_ref[0:tm, :] = da1
            extd_ref[tm:tm + HALO, :] = cda_ref[...] * keep_next
            extd_ref[tm + HALO:tm + HALO + 8, :] = jnp.zeros((8, dc), F32)
            cda_ref[...] = da1[0:HALO, :]
            ds = dm_ref[:, dc:2 * dc]
            dproj_ref[:, 2 * dc:3 * dc] = (ds * q_ref[...]).astype(BF16)
            dq = ds * p_ref[:, 2 * dc:3 * dc]
            extq_ref[0:tm, :] = dq
            extq_ref[tm:tm + 8, :] = cdq_ref[...] * keep_next
            cdq_ref[...] = dq[0:8, :]
        units.append(head)
        for g in range(ng):
            def shift(g=g):
                for r in range(8):
                    ed_ref[g, r, 0:tm + HALO, :] = extd_ref[r:r + tm + HALO, 128 * g:128 * g + 128]
            units.append(shift)
        for g in range(ng):
            lanes = slice(128 * g, 128 * g + 128)
            accs = [None] * CONF_K
            for i0 in range(0, tm, rw):
                def conv(g=g, lanes=lanes, i0=i0, accs=accs):
                    a0v = a0_ref[i0:i0 + rw, lanes]
                    acc = jnp.zeros((rw, 128), F32)
                    for s in range(CONF_K):
                        m, r = divmod(s, 8)
                        e = ed_ref[g, r, i0 + 8 * m:i0 + 8 * m + rw, :]
                        acc = acc + e * wb_ref[CONF_K - 1 - s:CONF_K - s, lanes]
                        part = _rows8(e * a0v)
                        accs[s] = part if accs[s] is None else accs[s] + part
                    da0_ref[i0:i0 + rw, lanes] = acc
                units.append(conv)

            def flush(lanes=lanes, accs=accs):
                for s in range(CONF_K):
                    kk = CONF_K - 1 - s
                    sg_ref[8 * (SG_DWW + kk):8 * (SG_DWW + kk) + 8, lanes] += accs[s]
            units.append(flush)

        def emit():
            da0 = da0_ref[...]
            sig = _sigmoid(p_ref[:, dc:2 * dc])
            dproj_ref[:, 0:dc] = (da0 * sig).astype(BF16)
            dproj_ref[:, dc:2 * dc] = (da0 * a0_ref[...] * (1.0 - sig)).astype(BF16)
        units.append(emit)

        def short():
            scc, sch = p_ref[:, 3 * dc:4 * dc], p_ref[:, 4 * dc:5 * dc]
            pv = scc * sch
            dp = jnp.zeros((tm, dc), F32)
            for kk in range(SC_K):
                dqs = extq_ref[SC_K - 1 - kk:SC_K - 1 - kk + tm, :]
                dp = dp + dqs * scw_ref[kk:kk + 1, :]
                sg_ref[8 * (SG_SCW + kk):8 * (SG_SCW + kk) + 8, :] += _rows8(pv * dqs)
            dproj_ref[:, 3 * dc:4 * dc] = (dp * sch).astype(BF16)
            dproj_ref[:, 4 * dc:5 * dc] = (dp * scc).astype(BF16)
        units.append(short)
        return units

    def proj_units(first, dp_ref, w_ref, x_ref, dx1_ref, mod_ref, dh_ref, gx_ref, dsh_ref, dsc_ref):
        units = []
        for c0 in range(0, d, ncol):
            def chunk(c0=c0):
                dh_ref[:, c0:c0 + ncol] = _dot_nt(dp_ref[...], w_ref[c0:c0 + ncol, :])
            units.append(chunk)

        def tail():
            dh1 = dh_ref[...]
            xn, r1 = _rms(x_ref[...])
            v1, v2 = _rows8(dh1), _rows8(dh1 * xn)
            dsh_ref[...] = jnp.where(first, v1, dsh_ref[...] + v1)
            dsc_ref[...] = jnp.where(first, v2, dsc_ref[...] + v2)
            gx_ref[...] = dx1_ref[...] + _rms_bwd(dh1 * (1.0 + mod_ref[:, d:2 * d]), xn, r1)
        units.append(tail)
        return units

    def body(dyl_ref, dy_ref, wo_ref, a1_ref, q_ref, p_ref, wb_ref, lng_ref, lnb_ref, scw_ref,
             w_ref, x_ref, dx1_ref, mod_ref, dep_ref,
             dproj_ref, sg_ref, gx_ref, dsh_ref, dsc_ref,
             extd_ref, ed_ref, a0_ref, da0_ref, extq_ref, dp_ref, dh_ref, dm_ref, cda_ref, cdq_ref):
        k = pl.program_id(0)

        @pl.when(k == 0)
        def _():
            sg_ref[...] = jnp.zeros(sg_ref.shape, F32)
            dp_ref[...] = jnp.zeros(dp_ref.shape, BF16)
            cda_ref[...] = jnp.zeros(cda_ref.shape, F32)
            cdq_ref[...] = jnp.zeros(cdq_ref.shape, F32)
            _run_units(dmixed_units(dyl_ref, wo_ref, dm_ref))

        first = jnp.logical_or(k <= 1, (nt - k) % tps == tps - 1)
        after = lambda: proj_units(first, dp_ref, w_ref, x_ref, dx1_ref, mod_ref, dh_ref, gx_ref, dsh_ref, dsc_ref)

        @pl.when(k < nt)
        def _():
            _run_units(mix_units(nt - 1 - k, dm_ref, a1_ref, q_ref, p_ref, wb_ref, lng_ref, lnb_ref, scw_ref,
                                 dproj_ref, sg_ref, extd_ref, ed_ref, a0_ref, da0_ref, extq_ref, cda_ref, cdq_ref),
                       after(), dmixed_units(dy_ref, wo_ref, dm_ref))
            dp_ref[...] = dproj_ref[...]

        @pl.when(k == nt)
        def _():
            _run_units(after())

    cur = lambda k: jnp.maximum(nt - 1 - k, 0)
    prev = lambda k: jnp.minimum(nt - k, nt - 1)
    acc = jax.ShapeDtypeStruct((n // t, 8, d), F32)
    return pl.pallas_call(
        body, name="mix_in_bwd", grid=(nt + 1,),
        in_specs=[pl.BlockSpec((tm, d), lambda k: (nt - 1, 0)),
                  pl.BlockSpec((tm, d), lambda k: (jnp.maximum(nt - 2 - k, 0), 0)), _resident(w_out.shape),
                  pl.BlockSpec((tm, dc), lambda k: (cur(k), 0)),
                  pl.BlockSpec((tm, dc), lambda k: (cur(k), 0)),
                  pl.BlockSpec((tm, d_in), lambda k: (cur(k), 0)),
                  _const(wb.shape), _const(lng.shape), _const(lnb.shape), _const(scw.shape),
                  _resident((d, d_in)),
                  pl.BlockSpec((tm, d), lambda k: (prev(k), 0)), pl.BlockSpec((tm, d), lambda k: (prev(k), 0)),
                  pl.BlockSpec((None, 1, 6 * d), lambda k: (prev(k) // tps, 0, 0)), ANY],
        out_specs=[pl.BlockSpec((tm, d_in), lambda k: (cur(k), 0)), _const((8 * SG_N, dc)),
                   pl.BlockSpec((tm, d), lambda k: (prev(k), 0)),
                   pl.BlockSpec((None, 8, d), lambda k: (prev(k) // tps, 0, 0)),
                   pl.BlockSpec((None, 8, d), lambda k: (prev(k) // tps, 0, 0))],
        out_shape=[jax.ShapeDtypeStruct((n, d_in), BF16), jax.ShapeDtypeStruct((8 * SG_N, dc), F32),
                   jax.ShapeDtypeStruct((n, d), F32), acc, acc],
        scratch_shapes=[pltpu.VMEM((tm + HALO + 8, dc), F32), _shift_scratch(tm, dc),
                        pltpu.VMEM((tm, dc), F32), pltpu.VMEM((tm, dc), F32),
                        pltpu.VMEM((tm + 8, dc), F32),
                        pltpu.VMEM((tm, d_in), BF16), pltpu.VMEM((tm, d), F32), pltpu.VMEM((tm, 2 * dc), F32),
                        pltpu.VMEM((HALO, dc), F32), pltpu.VMEM((8, dc), F32)],
        compiler_params=pltpu.CompilerParams(dimension_semantics=("arbitrary",)),
    )(dy1, dy1, w_out, a1, q, proj, wb, lng, lnb, scw, w_in, x, dx1, mod3, dep)


SMALL_ROWS = 40


def _pack_small(ids, sg, ggf, loss, accs, d, dep):
    dc = d // 2
    nb = accs[0].shape[0]

    def body(ids_ref, sg_ref, ggf_ref, loss_ref, dsh1, dsc1, dg1, dsh2, dsc2, dg2, dep_ref, pack_ref, dmod_ref):
        pack_ref[...] = jnp.zeros(pack_ref.shape, F32)
        for k in range(SG_N):
            pack_ref[k:k + 1, :] = jnp.sum(sg_ref[8 * k:8 * k + 8, :], axis=0, keepdims=True)
        gf = jnp.sum(ggf_ref[...], axis=0, keepdims=True)
        pack_ref[SG_N:SG_N + 1, :] = gf[:, 0:dc]
        pack_ref[SG_N + 1:SG_N + 2, :] = gf[:, dc:d]
        tot = jnp.sum(jnp.sum(loss_ref[...], axis=0, keepdims=True), axis=1, keepdims=True) * (0.5 / d)
        pack_ref[SG_N + 2:SG_N + 3, :] = jnp.broadcast_to(tot, (1, dc))
        for b in range(nb):
            row = jnp.concatenate([jnp.sum(ref[b], axis=0, keepdims=True)
                                   for ref in (dsh1, dsc1, dg1, dsh2, dsc2, dg2)], axis=1)
            for f in range(fold):
                dmod_ref[b * fold + f:b * fold + f + 1, :] = row[:, f * wf:(f + 1) * wf]

    fold = 8 // nb
    wf = 6 * d // fold
    assert nb * fold == 8 and wf % 128 == 0
    whole = lambda a: pl.BlockSpec(a.shape, lambda i, ids: (0,) * a.ndim)
    mine = lambda r, c: pl.BlockSpec((None, r, c), lambda i, ids: (ids[2], 0, 0))
    return pl.pallas_call(
        body, name="pack_small",
        grid_spec=pltpu.PrefetchScalarGridSpec(
            num_scalar_prefetch=1, grid=(1,),
            in_specs=[whole(a) for a in (sg, ggf, loss, *accs)] + [ANY],
            out_specs=[mine(SMALL_ROWS, dc), mine(8, wf)]),
        out_shape=[jax.ShapeDtypeStruct((N_DEV, SMALL_ROWS, dc), F32), jax.ShapeDtypeStruct((N_DEV, 8, wf), F32)],
    )(ids, sg, ggf, loss, *accs, dep)


def _small_reduce(pack_all, dmod_all, nb, dep):
    def body(pk_ref, dm_ref, dep_ref, red_ref, dmod_ref, gb_ref):
        tot = pk_ref[0]
        for dev in range(1, N_DEV):
            tot = tot + pk_ref[dev]
        red_ref[...] = tot
        for f in range(fold):
            gb = jnp.zeros((1, wf), F32)
            for dev in range(N_DEV):
                for b in range(nb):
                    seg = dm_ref[dev, b * fold + f:b * fold + f + 1, :]
                    dmod_ref[dev * nb + b:dev * nb + b + 1, f * wf:(f + 1) * wf] = seg
                    gb = gb + seg
            gb_ref[:, f * wf:(f + 1) * wf] = gb

    fold = 8 // nb
    wf = dmod_all.shape[2]
    return pl.pallas_call(
        body, name="small_reduce",
        out_shape=[jax.ShapeDtypeStruct(pack_all.shape[1:], F32),
                   jax.ShapeDtypeStruct((N_DEV * nb, fold * wf), F32),
                   jax.ShapeDtypeStruct((1, fold * wf), F32)],
        in_specs=[VMEM] * 2 + [ANY], out_specs=[VMEM] * 3,
    )(pack_all, dmod_all, dep)


def _adam(w, g, m, v):
    m = ADAM_B1 * m + (1.0 - ADAM_B1) * g
    v = ADAM_B2 * v + (1.0 - ADAM_B2) * (g * g)
    m_hat = m / (1.0 - ADAM_B1 ** ADAM_STEP)
    v_hat = v / (1.0 - ADAM_B2 ** ADAM_STEP)
    delta = -ADAM_LR * (m_hat / (jnp.sqrt(v_hat) + ADAM_EPS) + ADAM_WD * w)
    return delta, m, v


def _adamw_big(w, g, m, v, name, deps=()):
    r, c = w.shape
    tr = min(r, 256)

    def body(w_ref, g_ref, m_ref, v_ref, *rest):
        go_ref, d_ref, nm_ref, nv_ref = rest[len(deps):]
        g = g_ref[...]
        go_ref[...] = g
        d_ref[...], nm_ref[...], nv_ref[...] = _adam(w_ref[...], g, m_ref[...], v_ref[...])

    s = jax.ShapeDtypeStruct((r, c), F32)
    return pl.pallas_call(
        body, name=name, grid=(r // tr,),
        in_specs=[_rowblk(tr, c)] * 4 + [ANY] * len(deps), out_specs=[_rowblk(tr, c)] * 4, out_shape=[s, s, s, s],
        compiler_params=pltpu.CompilerParams(dimension_semantics=("parallel",)),
    )(w, g, m, v, *deps)


def _adamw_ada(act_t, dmod_cols, w, m, v):
    r, c = w.shape
    tr = min(r, 256)
    nb = act_t.shape[1]

    def body(a_ref, dm_ref, w_ref, m_ref, v_ref, g_ref, d_ref, nm_ref, nv_ref):
        g = jnp.dot(a_ref[...], dm_ref[...], preferred_element_type=F32, precision=HIGHEST)
        g_ref[...] = g
        d_ref[...], nm_ref[...], nv_ref[...] = _adam(w_ref[...], g, m_ref[...], v_ref[...])

    s = jax.ShapeDtypeStruct((r, c), F32)
    return pl.pallas_call(
        body, name="adamw_w_ada", grid=(r // tr,),
        in_specs=[_rowblk(tr, nb), _const((nb, c))] + [_rowblk(tr, c)] * 3,
        out_specs=[_rowblk(tr, c)] * 4, out_shape=[s, s, s, s],
        compiler_params=pltpu.CompilerParams(dimension_semantics=("parallel",)),
    )(act_t, dmod_cols, w, m, v)


def _adamw_small(ids, red, g_bada, ws, ms, vs):
    n = len(ws)
    cw = ws[1].shape[1]

    def body(ids_ref, red_ref, gb_ref, *refs):
        cols = pl.ds(pl.multiple_of(ids_ref[0] * cw, 128), cw)
        gs = [gb_ref[...], red_ref[SG_DWW:SG_DWW + CONF_K, cols], red_ref[SG_DWB:SG_DWB + 1, :],
              red_ref[SG_LNG:SG_LNG + 1, :], red_ref[SG_LNB:SG_LNB + 1, :], red_ref[SG_SCW:SG_SCW + SC_K, cols],
              jnp.concatenate([red_ref[SG_N:SG_N + 1, :], red_ref[SG_N + 1:SG_N + 2, :]], axis=1)]
        for i in range(n):
            w, m, v = (refs[j * n + i][...] for j in range(3))
            dl, nm, nv = _adam(w, gs[i], m, v)
            refs[3 * n + i][...] = gs[i]
            refs[4 * n + i][...] = dl
            refs[5 * n + i][...] = nm
            refs[6 * n + i][...] = nv

    whole = lambda a: pl.BlockSpec(a.shape, lambda i, ids: (0,) * a.ndim)
    shapes = [jax.ShapeDtypeStruct(w.shape, F32) for w in ws]
    return pl.pallas_call(
        body, name="adamw_small",
        grid_spec=pltpu.PrefetchScalarGridSpec(
            num_scalar_prefetch=1, grid=(1,),
            in_specs=[whole(a) for a in (red, g_bada, *ws, *ms, *vs)],
            out_specs=[whole(a) for a in ws] * 4),
        out_shape=shapes * 4,
    )(ids, red, g_bada, *ws, *ms, *vs)


def kernel(x, c, w_ada, b_ada, w_in, conf_dw_w, conf_dw_b, conf_ln_g, conf_ln_b, sc_conv_w, w_out, w_mlp1, w_mlp2, g_final, loss_target, m_w_ada, m_b_ada, m_w_in, m_conf_dw_w, m_conf_dw_b, m_conf_ln_g, m_conf_ln_b, m_sc_conv_w, m_w_out, m_w_mlp1, m_w_mlp2, m_g_final, v_w_ada, v_b_ada, v_w_in, v_conf_dw_w, v_conf_dw_b, v_conf_ln_g, v_conf_ln_b, v_sc_conv_w, v_w_out, v_w_mlp1, v_w_mlp2, v_g_final):
    nb, t, d = x.shape
    n = nb * t
    dc = d // 2
    ada_w = w_ada.shape[2]
    ax, ay, ac = _me()
    chip = 2 * ax + ay
    dev = 2 * chip + ac
    ids = jnp.stack([chip, ac, dev]).astype(jnp.int32)

    lays = _wlayout(d)
    names = ("in", "out", "mlp1", "mlp2")
    fulls = _cast_place(ids, [w_in[0], w_out[0], w_mlp1[0], w_mlp2[0]], lays)

    c_pad = jnp.zeros((8, d), F32).at[0:nb].set(c)
    cw_pad = jnp.zeros((SMALL_ROWS, dc // N_CHIPS), F32)
    cw_pad = cw_pad.at[0:CONF_K].set(conf_dw_w[0]).at[HALO:HALO + SC_K].set(sc_conv_w[0])
    c_all8, cw_all8 = _all_gather8([c_pad, cw_pad], "gather_c")
    plan_i = _gather_chip_plan(lays[0:1])
    sems_i, bufs_i, tok_i = _copy_start("gather_in_start", [fulls[0]], 3, plan_i, [c_all8])
    c_all = c_all8[:, 0:nb].reshape(N_DEV * nb, d) + tok_i[0, 0]
    cw_full = jnp.concatenate([cw_all8[2 * k] for k in range(N_CHIPS)], axis=1)
    dww, scw = cw_full[0:CONF_K], cw_full[HALO:HALO + SC_K]
    b_cols = lax.dynamic_slice(b_ada, (0, chip * ada_w), (1, ada_w))
    c_act, mod_shard = _ada_mod(c_all, w_ada[0], b_cols)
    mod_all = _gather_mod(mod_shard)
    mod3 = lax.dynamic_slice(mod_all, (dev * nb, 0), (nb, 6 * d)).reshape(nb, 1, 6 * d)

    x2 = x.reshape(n, d)
    tgt = loss_target.reshape(n, d)
    (wf_in,) = _copy_wait("gather_in_wait", bufs_i, sems_i, plan_i, [mod3])
    (wf_in,) = _copy_blocking("gather_in_pass", [wf_in], 3, _gather_pass_plan(lays[0:1]))
    plan_o, plan_b, plan_p = _gather_direct_plan(lays[1:2]), _gather_chip_plan(lays[2:4]), _gather_pass_plan(lays[2:4])
    sems_o, bufs_o, tok_o = _copy_start("gather_out_start", [fulls[1]], 6, plan_o, [wf_in, mod3])
    sems_b, bufs_b, tok_b = _copy_start("gather_mlp_start", fulls[2:4], 6, plan_b, [tok_o])
    proj, h1, mixed, a1, q = _in_mix_fwd(x2, mod3, wf_in, dww, conf_dw_b, conf_ln_g, conf_ln_b, scw, t, tok_b)
    (wf_out,) = _copy_wait("gather_out_wait", bufs_o, sems_o, plan_o, [mixed])
    bufs_b = _copy_wait("gather_mlp_wait", bufs_b, sems_b, plan_b, [mixed])
    sems_p, bufs_p, tok_p = _copy_start("gather_pass_start", bufs_b, 6, plan_p)
    x1, y1, h2 = _out_proj(mixed, wf_out, x2, mod3, t, tok_p)
    wf_1, wf_2 = _copy_wait("gather_pass_wait", bufs_p, sems_p, plan_p, [h2])
    z, dx2, dy2, dg2, ggf, loss_p = _mlp_fwd(h2, x1, tgt, mod3, g_final.reshape(1, d), wf_1, wf_2, t)

    dz, dx1, dy1, dsh2, dsc2, dg1 = _mlp_bwd(dy2, z, x1, dx2, y1, mod3, wf_1, wf_2, t)
    g_w2 = _wgrad(z, dy2, "wgrad_mlp2", relu2=True)
    g_w1 = _wgrad(h2, dz, "wgrad_mlp1")
    made = {}

    def behind_pair_exchange(tok):
        made["g_wout"] = _wgrad(mixed, dy1, "wgrad_out", deps=[tok])
        return [made["g_wout"]]

    def behind_chip_exchange(tok):
        made["dproj"], made["sg"], made["grad_x"], made["dsh1"], made["dsc1"] = _mix_in_bwd(
            dy1, wf_out, a1, q, proj, dww, conf_ln_g, conf_ln_b, scw, wf_in, x2, dx1, mod3, t, tok)
        return [made["dproj"]]

    def behind_pair_share(tok):
        made["g_win"] = _wgrad(h1, made["dproj"], "wgrad_in", deps=[tok])
        return [made["g_win"]]

    gr_1, gr_2 = _reduce_scatter(ids, [g_w1, g_w2], lays[2:4], names[2:4], "m",
                                 (behind_pair_exchange, behind_chip_exchange, behind_pair_share))

    big = {}

    plan_g = _gather8_plan(2)

    def behind_pair_exchange_in(tok):
        packs = _pack_small(ids, made["sg"], ggf, loss_p, (made["dsh1"], made["dsc1"], dg1, dsh2, dsc2, dg2), d, tok)
        made["gather"] = _copy_start("gather_small_start", list(packs), 2 * (N_DEV - 1), plan_g)
        return [made["gather"][2]]

    def behind_chip_exchange_in(tok):
        sems_g, bufs_g, _ = made["gather"]
        gathered = _copy_wait("gather_small_wait", bufs_g, sems_g, plan_g, [tok])
        red, dmod_all, g_bada = _small_reduce(*gathered, nb, tok)
        dmod_cols = lax.dynamic_slice(dmod_all, (0, chip * ada_w), (N_DEV * nb, ada_w))
        made["ada"] = _adamw_ada(c_act.T, dmod_cols, w_ada[0], m_w_ada[0], v_w_ada[0])
        big["w_mlp1"] = _adamw_big(w_mlp1[0], gr_1, m_w_mlp1[0], v_w_mlp1[0], "adamw_w_mlp1")
        small_w = [b_ada, conf_dw_w[0], conf_dw_b, conf_ln_g, conf_ln_b, sc_conv_w[0], g_final.reshape(1, d)]
        small_m = [m_b_ada, m_conf_dw_w[0], m_conf_dw_b, m_conf_ln_g, m_conf_ln_b, m_sc_conv_w[0],
                   m_g_final.reshape(1, d)]
        small_v = [v_b_ada, v_conf_dw_w[0], v_conf_dw_b, v_conf_ln_g, v_conf_ln_b, v_sc_conv_w[0],
                   v_g_final.reshape(1, d)]
        made["upd"] = _adamw_small(ids, red, g_bada, small_w, small_m, small_v)
        made["loss"] = red[SG_N + 2, 0]
        return [big["w_mlp1"][0], made["upd"][0]]

    def behind_pair_share_in(tok):
        big["w_mlp2"] = _adamw_big(w_mlp2[0], gr_2, m_w_mlp2[0], v_w_mlp2[0], "adamw_w_mlp2", deps=[tok])
        return [big["w_mlp2"][0]]

    gr_in, gr_out = _reduce_scatter(ids, [made["g_win"], made["g_wout"]], lays[0:2], names[0:2], "i",
                                    (behind_pair_exchange_in, behind_chip_exchange_in, behind_pair_share_in))
    big["w_in"] = _adamw_big(w_in[0], gr_in, m_w_in[0], v_w_in[0], "adamw_w_in")
    big["w_out"] = _adamw_big(w_out[0], gr_out, m_w_out[0], v_w_out[0], "adamw_w_out")
    grad_x, upd, loss = made["grad_x"], made["upd"], made["loss"]
    g_wada, d_wada, nm_wada, nv_wada = made["ada"]
    ns = len(upd) // 4
    small_g, s_delta, s_m, s_v = upd[0:ns], upd[ns:2 * ns], upd[2 * ns:3 * ns], upd[3 * ns:4 * ns]

    def outs(kind_big, kind_small, wada):
        sm = kind_small
        return (wada[None], sm[0], kind_big["w_in"][None], sm[1][None], sm[2], sm[3], sm[4], sm[5][None],
                kind_big["w_out"][None], kind_big["w_mlp1"][None], kind_big["w_mlp2"][None], sm[6].reshape(d))

    grads_out = outs({k: v[0] for k, v in big.items()}, small_g, g_wada)
    delta_out = outs({k: v[1] for k, v in big.items()}, s_delta, d_wada)
    m_out = outs({k: v[2] for k, v in big.items()}, s_m, nm_wada)
    v_out = outs({k: v[3] for k, v in big.items()}, s_v, nv_wada)
    return (loss, grad_x.reshape(nb, t, d), *grads_out, *delta_out, *m_out, *v_out)
```

```python
import functools

import jax
import jax.numpy as jnp
from jax import lax
from jax.experimental import pallas as pl
from jax.experimental.pallas import tpu as pltpu

F32 = jnp.float32
BF16 = jnp.bfloat16
MESH = pl.DeviceIdType.MESH
HIGHEST = lax.Precision.HIGHEST

EPS = 1e-6
CONF_K = 31
SC_K = 3
HALO = 32
N_CHIPS = 4
N_DEV = 8

ADAM_LR = 0.001
ADAM_B1 = 0.9
ADAM_B2 = 0.999
ADAM_EPS = 1e-08
ADAM_WD = 0.01
ADAM_STEP = 10

TM_MM = 1024
TM_MIX = 256
TM_MLP = 512
FF_CHUNK = 1024
MLP_SUB = 2
TK_WG = 4096
RB_CONV = 64
RB_WG = 32
CHIP_RELS = ((1, 0), (0, 1), (1, 1))

ANY = pl.BlockSpec(memory_space=pl.ANY)
VMEM = pl.BlockSpec(memory_space=pltpu.VMEM)
HBM = pl.BlockSpec(memory_space=pltpu.HBM)
SEM = pl.BlockSpec(memory_space=pltpu.SEMAPHORE)
EFFECT = pltpu.SideEffectType.DATAFLOW_SIDE_EFFECTING


def _me():
    return lax.axis_index("x"), lax.axis_index("y"), lax.axis_index("c")


def _flip(v, f):
    return 1 - v if f else v


def _rows8(v):
    r, c = v.shape
    return v.reshape(r // 8, 8, c).sum(axis=0)


def _rms(x):
    r = lax.rsqrt(jnp.mean(x * x, axis=-1, keepdims=True) + EPS)
    return x * r, r


def _rms_bwd(dxn, xn, r):
    return r * (dxn - xn * jnp.mean(dxn * xn, axis=-1, keepdims=True))


def _sigmoid(x):
    return 1.0 / (1.0 + jnp.exp(-x))


def _dot(a, b):
    return jnp.dot(a, b, preferred_element_type=F32)


def _dot_nt(a, b):
    return lax.dot_general(a, b, (((1,), (1,)), ((), ())), preferred_element_type=F32)


def _dot_tn(a, b):
    return lax.dot_general(a, b, (((0,), (0,)), ((), ())), preferred_element_type=F32)


def _const(shape):
    nd = len(shape)
    return pl.BlockSpec(shape, lambda i: (0,) * nd)


def _resident(shape):
    nd = len(shape)
    return pl.BlockSpec(shape, lambda i: (0,) * nd, pipeline_mode=pl.Buffered(1))


def _rowblk(tm, c):
    return pl.BlockSpec((tm, c), lambda i: (i, 0))


def _modspec(tps, width):
    return pl.BlockSpec((None, 1, width), lambda i: (i // tps, 0, 0))


def _accspec(tps, c):
    return pl.BlockSpec((None, 8, c), lambda i: (i // tps, 0, 0))


def _acc_add(ref, first, v):
    @pl.when(first)
    def _():
        ref[...] = v

    @pl.when(jnp.logical_not(first))
    def _():
        ref[...] += v


def _all_gather8(arrs, name):
    n = len(arrs)

    def body(*refs):
        ins, outs = refs[:n], refs[n:2 * n]
        send, recv = refs[2 * n:]
        x, y, c = _me()
        dev = 4 * x + 2 * y + c
        for a in range(n):
            outs[a][dev] = ins[a][...]
        sends = []
        for r in range(1, N_DEV):
            fx, fy, fc = (r >> 2) & 1, (r >> 1) & 1, r & 1
            peer = (_flip(x, fx), _flip(y, fy), _flip(c, fc))
            for a in range(n):
                cp = pltpu.make_async_remote_copy(
                    src_ref=ins[a], dst_ref=outs[a].at[dev],
                    send_sem=send.at[r - 1, a], recv_sem=recv.at[r - 1, a],
                    device_id=peer, device_id_type=MESH)
                cp.start()
                sends.append(cp)
        for r in range(1, N_DEV):
            fx, fy, fc = (r >> 2) & 1, (r >> 1) & 1, r & 1
            pdev = 4 * _flip(x, fx) + 2 * _flip(y, fy) + _flip(c, fc)
            for a in range(n):
                pltpu.make_async_remote_copy(
                    src_ref=ins[a], dst_ref=outs[a].at[pdev],
                    send_sem=send.at[r - 1, a], recv_sem=recv.at[r - 1, a],
                    device_id=(x, y, c), device_id_type=MESH).wait_recv()
        for cp in sends:
            cp.wait_send()

    return pl.pallas_call(
        body, name=name,
        out_shape=[jax.ShapeDtypeStruct((N_DEV,) + a.shape, a.dtype) for a in arrs],
        in_specs=[VMEM] * n, out_specs=[VMEM] * n,
        scratch_shapes=[pltpu.SemaphoreType.DMA((N_DEV - 1, n)),
                        pltpu.SemaphoreType.DMA((N_DEV - 1, n))],
    )(*arrs)


def _gather_mod(mod_shard):
    nb, w = mod_shard.shape

    def body(in_ref, out_ref, send, recv):
        x, y, c = _me()
        chip = 2 * x + y
        out_ref[:, pl.ds(pl.multiple_of(chip * w, 128), w)] = in_ref[...]
        sends = []
        for r, (fx, fy) in enumerate(CHIP_RELS):
            cp = pltpu.make_async_remote_copy(
                src_ref=in_ref,
                dst_ref=out_ref.at[:, pl.ds(pl.multiple_of(chip * w, 128), w)],
                send_sem=send.at[r], recv_sem=recv.at[r],
                device_id=(_flip(x, fx), _flip(y, fy), c), device_id_type=MESH)
            cp.start()
            sends.append(cp)
        for r, (fx, fy) in enumerate(CHIP_RELS):
            pchip = 2 * _flip(x, fx) + _flip(y, fy)
            pltpu.make_async_remote_copy(
                src_ref=in_ref,
                dst_ref=out_ref.at[:, pl.ds(pl.multiple_of(pchip * w, 128), w)],
                send_sem=send.at[r], recv_sem=recv.at[r],
                device_id=(x, y, c), device_id_type=MESH).wait_recv()
        for cp in sends:
            cp.wait_send()

    return pl.pallas_call(
        body, name="gather_mod",
        out_shape=jax.ShapeDtypeStruct((nb, N_CHIPS * w), mod_shard.dtype),
        in_specs=[VMEM], out_specs=VMEM,
        scratch_shapes=[pltpu.SemaphoreType.DMA((3,)), pltpu.SemaphoreType.DMA((3,))],
    )(mod_shard)


def _wlayout(d):
    d_in = 5 * d // 2
    return (
        (d, d_in // N_CHIPS, True),
        (d // N_CHIPS, d, False),
        (d, 4 * d // N_CHIPS, True),
        (4 * d // N_CHIPS, d, False),
    )


def _full_shape(lay):
    r, c, by_col = lay
    return (r, c * N_CHIPS) if by_col else (r * N_CHIPS, c)


def _full_view(ref, lay, k, h):
    r, c, by_col = lay
    hr = r // 2
    if by_col:
        return ref.at[pl.ds(pl.multiple_of(h * hr, 16), hr), pl.ds(pl.multiple_of(k * c, 128), c)]
    return ref.at[pl.ds(pl.multiple_of(k * r + h * hr, 16), hr), :]


def _half_view(ref, lay, h):
    hr = lay[0] // 2
    return ref.at[pl.ds(pl.multiple_of(h * hr, 16), hr), :]


def _half_shape(lay):
    return (lay[0] // 2, lay[1])


def _hbm(a):
    return pltpu.with_memory_space_constraint(a, pltpu.HBM)


def _remote(src, dst, send, recv, idx, to):
    return lambda: pltpu.make_async_remote_copy(src_ref=src, dst_ref=dst, send_sem=send.at[idx], recv_sem=recv.at[idx],
                                                device_id=to, device_id_type=MESH)


def _copy_start(name, bufs, n_sems, plan, after=()):
    nb, na = len(bufs), len(after)

    def body(*refs):
        sends, _ = plan(refs[:nb], refs[nb + na], refs[nb + na + 1])
        for mk in sends:
            mk().start()
        refs[-1][...] = jnp.zeros((8, 128), F32)

    outs = pl.pallas_call(
        body, name=name,
        out_shape=(pltpu.SemaphoreType.DMA((n_sems,)), pltpu.SemaphoreType.DMA((n_sems,)))
        + tuple(pltpu.HBM(b.shape, b.dtype) for b in bufs) + (jax.ShapeDtypeStruct((8, 128), F32),),
        in_specs=(HBM,) * nb + (ANY,) * na, out_specs=(SEM, SEM) + (HBM,) * nb + (VMEM,),
        input_output_aliases={i: 2 + i for i in range(nb)},
        compiler_params=pltpu.CompilerParams(has_side_effects=EFFECT),
    )(*[_hbm(b) for b in bufs], *after)
    return (outs[0], outs[1]), list(outs[2:2 + nb]), outs[-1]


def _copy_wait(name, bufs, sems, plan, after):
    nb, na = len(bufs), len(after)

    def body(*refs):
        sends, recvs = plan(refs[:nb], refs[nb], refs[nb + 1])
        for mk in sends:
            mk().wait_send()
        for mk in recvs:
            mk().wait_recv()

    outs = pl.pallas_call(
        body, name=name,
        out_shape=tuple(pltpu.HBM(b.shape, b.dtype) for b in bufs),
        in_specs=(HBM,) * nb + (SEM, SEM) + (ANY,) * na, out_specs=(HBM,) * nb,
        input_output_aliases={i: i for i in range(nb)},
        compiler_params=pltpu.CompilerParams(has_side_effects=EFFECT),
    )(*bufs, *sems, *after)
    return list(outs)


def _copy_blocking(name, bufs, n_sems, plan, after=()):
    nb, na = len(bufs), len(after)

    def body(*refs):
        sends, recvs = plan(refs[:nb], refs[2 * nb + na], refs[2 * nb + na + 1])
        started = [mk() for mk in sends]
        for cp in started:
            cp.start()
        for mk in recvs:
            mk().wait_recv()
        for cp in started:
            cp.wait_send()

    return list(pl.pallas_call(
        body, name=name,
        out_shape=tuple(jax.ShapeDtypeStruct(b.shape, b.dtype) for b in bufs),
        in_specs=(ANY,) * (nb + na), out_specs=(ANY,) * nb,
        input_output_aliases={i: i for i in range(nb)},
        scratch_shapes=[pltpu.SemaphoreType.DMA((n_sems,)), pltpu.SemaphoreType.DMA((n_sems,))],
    )(*bufs, *after))


def _exchange(name, bufs, n_sems, plan, between):
    if between is None:
        return _copy_blocking(name, bufs, n_sems, plan)
    sems, bufs, tok = _copy_start(name + "_start", bufs, n_sems, plan)
    return _copy_wait(name + "_wait", bufs, sems, plan, between(tok))


def _gather_direct_plan(lays):
    def plan(full, send, recv):
        x, y, c = _me()
        chip = 2 * x + y
        sends, recvs = [], []
        for r, (fx, fy) in enumerate(CHIP_RELS):
            px, py = _flip(x, fx), _flip(y, fy)
            for i, lay in enumerate(lays):
                for q in range(2):
                    oc = _flip(c, q)
                    mine = _full_view(full[i], lay, chip, c)
                    idx = (r * len(lays) + i) * 2 + q
                    sends.append(_remote(mine, mine, send, recv, idx, (px, py, oc)))
                    theirs = _full_view(full[i], lay, 2 * px + py, oc)
                    recvs.append(_remote(theirs, theirs, send, recv, idx, (x, y, c)))
        return sends, recvs
    return plan


def _gather_chip_plan(lays):
    def plan(full, send, recv):
        x, y, c = _me()
        chip = 2 * x + y
        sends, recvs = [], []
        for r, (fx, fy) in enumerate(CHIP_RELS):
            px, py = _flip(x, fx), _flip(y, fy)
            for i, lay in enumerate(lays):
                mine = _full_view(full[i], lay, chip, c)
                sends.append(_remote(mine, mine, send, recv, r * len(lays) + i, (px, py, c)))
                theirs = _full_view(full[i], lay, 2 * px + py, c)
                recvs.append(_remote(theirs, theirs, send, recv, r * len(lays) + i, (x, y, c)))
        return sends, recvs
    return plan


def _gather_pass_plan(lays):
    def plan(full, send, recv):
        x, y, c = _me()
        sends, recvs = [], []
        for r, (fx, fy) in enumerate(CHIP_RELS):
            pchip = 2 * _flip(x, fx) + _flip(y, fy)
            for i, lay in enumerate(lays):
                landed = _full_view(full[i], lay, pchip, c)
                sends.append(_remote(landed, landed, send, recv, r * len(lays) + i, (x, y, 1 - c)))
                other = _full_view(full[i], lay, pchip, 1 - c)
                recvs.append(_remote(other, other, send, recv, r * len(lays) + i, (x, y, c)))
        return sends, recvs
    return plan


def _gather8_plan(na):
    def plan(bufs, send, recv):
        x, y, c = _me()
        dev = 4 * x + 2 * y + c
        sends, recvs = [], []
        for r in range(1, N_DEV):
            fx, fy, fc = (r >> 2) & 1, (r >> 1) & 1, r & 1
            px, py, pc = _flip(x, fx), _flip(y, fy), _flip(c, fc)
            for a in range(na):
                idx = (r - 1) * na + a
                sends.append(_remote(bufs[a].at[dev], bufs[a].at[dev], send, recv, idx, (px, py, pc)))
                theirs = bufs[a].at[4 * px + 2 * py + pc]
                recvs.append(_remote(theirs, theirs, send, recv, idx, (x, y, c)))
        return sends, recvs
    return plan


def _pair_exchange_plan(lays):
    nw = len(lays)

    def plan(bufs, send, recv):
        x, y, c = _me()
        sends, recvs = [], []
        for i, lay in enumerate(lays):
            for k in range(N_CHIPS):
                sends.append(_remote(_full_view(bufs[i], lay, k, 1 - c), bufs[nw + i].at[k],
                                     send, recv, i * N_CHIPS + k, (x, y, 1 - c)))
                recvs.append(_remote(_full_view(bufs[i], lay, k, c), bufs[nw + i].at[k],
                                     send, recv, i * N_CHIPS + k, (x, y, c)))
        return sends, recvs
    return plan


def _chip_exchange_plan(nw):
    def plan(bufs, send, recv):
        x, y, c = _me()
        sends, recvs = [], []
        for r, (fx, fy) in enumerate(CHIP_RELS):
            px, py = _flip(x, fx), _flip(y, fy)
            for i in range(nw):
                sends.append(_remote(bufs[i].at[2 * px + py], bufs[nw + i].at[r], send, recv, r * nw + i, (px, py, c)))
                recvs.append(_remote(bufs[i].at[2 * px + py], bufs[nw + i].at[r], send, recv, r * nw + i, (x, y, c)))
        return sends, recvs
    return plan


def _pair_share_plan(lays):
    def plan(bufs, send, recv):
        x, y, c = _me()
        sends, recvs = [], []
        for i, lay in enumerate(lays):
            mine = _half_view(bufs[i], lay, c)
            sends.append(_remote(mine, mine, send, recv, i, (x, y, 1 - c)))
            other = _half_view(bufs[i], lay, 1 - c)
            recvs.append(_remote(other, other, send, recv, i, (x, y, c)))
        return sends, recvs
    return plan


def _pair_sum(ids, g, got, lay, name):
    r, c, by_col = lay
    hr = r // 2
    tr = min(hr, 256)
    nt = hr // tr

    def body(ids_ref, g_ref, got_ref, s32_ref, s16_ref):
        s = g_ref[...] + got_ref[...]
        s16_ref[...] = s.astype(BF16)

        @pl.when(pl.program_id(1) == ids_ref[0])
        def _():
            s32_ref[...] = s

    if by_col:
        gspec = pl.BlockSpec((tr, c), lambda t, k, ids: (ids[1] * nt + t, k))
    else:
        gspec = pl.BlockSpec((tr, c), lambda t, k, ids: ((2 * k + ids[1]) * nt + t, 0))
    hspec = pl.BlockSpec((None, tr, c), lambda t, k, ids: (k, t, 0))
    return pl.pallas_call(
        body, name=name,
        grid_spec=pltpu.PrefetchScalarGridSpec(
            num_scalar_prefetch=1, grid=(nt, N_CHIPS),
            in_specs=[gspec, hspec], out_specs=[pl.BlockSpec((tr, c), lambda t, k, ids: (t, 0)), hspec]),
        out_shape=[jax.ShapeDtypeStruct((hr, c), F32),
                   jax.ShapeDtypeStruct((N_CHIPS, hr, c), BF16)],
    )(ids, g, got)


def _chip_sum(ids, s32, got, lay, name):
    hr, c = _half_shape(lay)
    tr = min(hr, 256)
    nt = hr // tr

    def body(ids_ref, s_ref, got_ref, out_ref):
        t = s_ref[...]
        for r in range(3):
            t = t + got_ref[r].astype(F32)
        out_ref[...] = t

    return pl.pallas_call(
        body, name=name,
        grid_spec=pltpu.PrefetchScalarGridSpec(
            num_scalar_prefetch=1, grid=(nt,),
            in_specs=[pl.BlockSpec((tr, c), lambda t, ids: (t, 0)),
                      pl.BlockSpec((3, tr, c), lambda t, ids: (0, t, 0))],
            out_specs=pl.BlockSpec((tr, c), lambda t, ids: (ids[1] * nt + t, 0))),
        out_shape=jax.ShapeDtypeStruct((2 * hr, c), F32),
    )(ids, s32, got)


def _reduce_scatter(ids, grads, lays, names, tag, between):
    nw = len(lays)
    got1 = [lax.empty((N_CHIPS,) + _half_shape(l), F32) for l in lays]
    bufs = _exchange("pair_exchange_" + tag, list(grads) + got1, nw * N_CHIPS, _pair_exchange_plan(lays), between[0])
    sums = [_pair_sum(ids, bufs[i], bufs[nw + i], lays[i], "pair_sum_" + names[i]) for i in range(nw)]
    got2 = [lax.empty((3,) + _half_shape(l), BF16) for l in lays]
    bufs = _exchange("chip_exchange_" + tag, [s[1] for s in sums] + got2, 3 * nw, _chip_exchange_plan(nw), between[1])
    mine = [_chip_sum(ids, sums[i][0], bufs[nw + i], lays[i], "chip_sum_" + names[i]) for i in range(nw)]
    return _exchange("pair_share_" + tag, mine, nw, _pair_share_plan(lays), between[2])


def _cast_place(ids, ws, lays):
    nw = len(ws)

    def body(ids_ref, *refs):
        for i in range(nw):
            refs[nw + i][...] = refs[i][...].astype(BF16)

    by_col_map = lambda i, ids: (0, ids[0])
    by_row_map = lambda i, ids: (ids[0], 0)
    return pl.pallas_call(
        body, name="cast_place",
        grid_spec=pltpu.PrefetchScalarGridSpec(
            num_scalar_prefetch=1, grid=(1,),
            in_specs=[pl.BlockSpec(l[:2], lambda i, ids: (0, 0)) for l in lays],
            out_specs=[pl.BlockSpec(l[:2], by_col_map if l[2] else by_row_map) for l in lays]),
        out_shape=[jax.ShapeDtypeStruct(_full_shape(l), BF16) for l in lays],
    )(ids, *ws)


def _ada_mod(c_all, w_ada, b_ada):
    def body(c_ref, w_ref, b_ref, act_ref, mod_ref):
        cv = c_ref[...]
        act = cv * _sigmoid(cv)
        act_ref[...] = act
        mod_ref[...] = jnp.dot(act, w_ref[...], preferred_element_type=F32, precision=HIGHEST) + b_ref[...]

    nb = c_all.shape[0]
    return pl.pallas_call(
        body, name="ada_mod",
        out_shape=[jax.ShapeDtypeStruct(c_all.shape, F32),
                   jax.ShapeDtypeStruct((nb, w_ada.shape[1]), F32)],
        in_specs=[VMEM] * 3, out_specs=[VMEM] * 2,
    )(c_all, w_ada, b_ada)


def _run_units(*unit_lists):
    total = max(len(u) for u in unit_lists)
    done = [0] * len(unit_lists)
    for step in range(1, total + 1):
        for li, units in enumerate(unit_lists):
            upto = (step * len(units) + total - 1) // total
            while done[li] < upto:
                units[done[li]]()
                done[li] += 1


def _in_mix_fwd(x, mod3, w_in, wb, dwb, lng, lnb, scw, t, dep):
    n, d = x.shape
    d_in = w_in.shape[1]
    dc = d_in // 5
    tm = min(TM_MIX, t)
    tps = t // tm
    nt = n // tm
    rb = min(RB_CONV, tm)
    ncol = 256
    ng = dc // 128
    assert tps % 2 == 0 and nt % 2 == 0

    def proj_units(x_ref, mod_ref, w_ref, p_ref, h_ref):
        def head():
            xn, _ = _rms(x_ref[...])
            h_ref[...] = (xn * (1.0 + mod_ref[:, d:2 * d]) + mod_ref[:, 0:d]).astype(BF16)
        units = [head]
        for c0 in range(0, d_in, ncol):
            def chunk(c0=c0):
                p_ref[:, c0:c0 + ncol] = _dot(h_ref[...], w_ref[:, c0:c0 + ncol])
            units.append(chunk)
        return units

    def mix_units(first, p_ref, h_ref, r0, wb_ref, dwb_ref, lng_ref, lnb_ref, scw_ref,
                  proj_ref, h1_ref, mixed_ref, a1_ref, q_ref, ext_ref, e_ref, extp_ref, a1s_ref):
        rows = slice(r0, r0 + tm)
        units = []

        def glu():
            halo = ext_ref[tm:tm + HALO, :]
            ext_ref[0:HALO, :] = halo if first is False else jnp.where(first, 0.0, halo)
            ext_ref[HALO:HALO + tm, :] = p_ref[:, 0:dc] * _sigmoid(p_ref[:, dc:2 * dc])
        units.append(glu)
        for g in range(ng):
            def shift(g=g):
                for r in range(8):
                    e_ref[g, r, 0:tm + HALO, :] = ext_ref[r:r + tm + HALO, 128 * g:128 * g + 128]
            units.append(shift)
        for g in range(ng):
            lanes = slice(128 * g, 128 * g + 128)
            for i0 in range(0, tm, rb):
                def conv(g=g, lanes=lanes, i0=i0):
                    acc = jnp.zeros((rb, 128), F32)
                    for k in range(CONF_K):
                        m, r = divmod(k + HALO - CONF_K + 1, 8)
                        acc = acc + e_ref[g, r, i0 + 8 * m:i0 + 8 * m + rb, :] * wb_ref[k:k + 1, lanes]
                    a1s_ref[i0:i0 + rb, lanes] = acc + dwb_ref[:, lanes]
                units.append(conv)

        def norm():
            a1 = a1s_ref[...]
            a1_ref[rows, :] = a1
            mu = jnp.mean(a1, axis=-1, keepdims=True)
            ac = a1 - mu
            rstd = lax.rsqrt(jnp.mean(ac * ac, axis=-1, keepdims=True) + EPS)
            a2 = ac * rstd * lng_ref[...] + lnb_ref[...]
            mixed_ref[rows, 0:dc] = (a2 * _sigmoid(a2)).astype(BF16)
        units.append(norm)

        def short():
            halo = extp_ref[tm:tm + 8, :]
            extp_ref[0:8, :] = halo if first is False else jnp.where(first, 0.0, halo)
            extp_ref[8:8 + tm, :] = p_ref[:, 3 * dc:4 * dc] * p_ref[:, 4 * dc:5 * dc]
            q = jnp.zeros((tm, dc), F32)
            for k in range(SC_K):
                q = q + extp_ref[6 + k:6 + k + tm, :] * scw_ref[k:k + 1, :]
            q_ref[rows, :] = q
            mixed_ref[rows, dc:2 * dc] = (p_ref[:, 2 * dc:3 * dc] * q).astype(BF16)
        units.append(short)

        def keep():
            proj_ref[rows, :] = p_ref[...]
            h1_ref[rows, :] = h_ref[...]
        units.append(keep)
        return units

    def body(x0_ref, xa_ref, xb_ref, mod0_ref, moda_ref, modb_ref, w_ref,
             wb_ref, dwb_ref, lng_ref, lnb_ref, scw_ref, dep_ref,
             proj_ref, h1_ref, mixed_ref, a1_ref, q_ref,
             p0_ref, p1_ref, h0_ref, hh1_ref, ext_ref, e_ref, extp_ref, a1s_ref):
        j = pl.program_id(0)

        @pl.when(j == 0)
        def _():
            ext_ref[...] = jnp.zeros(ext_ref.shape, F32)
            extp_ref[...] = jnp.zeros(extp_ref.shape, F32)
            _run_units(proj_units(x0_ref, mod0_ref, w_ref, p0_ref, h0_ref))

        common = (wb_ref, dwb_ref, lng_ref, lnb_ref, scw_ref, proj_ref, h1_ref, mixed_ref, a1_ref, q_ref,
                  ext_ref, e_ref, extp_ref, a1s_ref)
        _run_units(mix_units((2 * j) % tps == 0, p0_ref, h0_ref, 0, *common),
                   proj_units(xa_ref, moda_ref, w_ref, p1_ref, hh1_ref))
        _run_units(mix_units(False, p1_ref, hh1_ref, tm, *common),
                   proj_units(xb_ref, modb_ref, w_ref, p0_ref, h0_ref))

    last = nt - 1
    xspec = lambda f: pl.BlockSpec((tm, d), lambda j: (f(j), 0))
    mspec = lambda f: pl.BlockSpec((None, 1, 6 * d), lambda j: (f(j) // tps, 0, 0))
    out2 = lambda c: pl.BlockSpec((2 * tm, c), lambda j: (j, 0))
    return pl.pallas_call(
        body, name="in_mix_fwd", grid=(nt // 2,),
        in_specs=[xspec(lambda j: 0), xspec(lambda j: 2 * j + 1), xspec(lambda j: jnp.minimum(2 * j + 2, last)),
                  mspec(lambda j: 0), mspec(lambda j: 2 * j + 1), mspec(lambda j: jnp.minimum(2 * j + 2, last)),
                  _resident((d, d_in)),
                  _const(wb.shape), _const(dwb.shape), _const(lng.shape), _const(lnb.shape), _const(scw.shape), ANY],
        out_specs=[out2(d_in), out2(d), out2(2 * dc), out2(dc), out2(dc)],
        out_shape=[jax.ShapeDtypeStruct((n, d_in), F32), jax.ShapeDtypeStruct((n, d), BF16),
                   jax.ShapeDtypeStruct((n, 2 * dc), BF16), jax.ShapeDtypeStruct((n, dc), F32),
                   jax.ShapeDtypeStruct((n, dc), F32)],
        scratch_shapes=[pltpu.VMEM((tm, d_in), F32), pltpu.VMEM((tm, d_in), F32),
                        pltpu.VMEM((tm, d), BF16), pltpu.VMEM((tm, d), BF16),
                        pltpu.VMEM((tm + HALO + 8, dc), F32), _shift_scratch(tm, dc),
                        pltpu.VMEM((tm + 8, dc), F32), pltpu.VMEM((tm, dc), F32)],
        compiler_params=pltpu.CompilerParams(dimension_semantics=("arbitrary",)),
    )(x, x, x, mod3, mod3, mod3, w_in, wb, dwb, lng, lnb, scw, dep)


def _shift_scratch(tm, dc):
    return pltpu.VMEM((dc // 128, 8, tm + HALO + 8, 128), F32)


def _out_proj(mixed, w_out, x, mod3, t, dep):
    n, d = x.shape
    tm = min(TM_MM, t)
    tps = t // tm

    def body(m_ref, w_ref, x_ref, mod_ref, dep_ref, x1_ref, y1_ref, h2_ref):
        y1 = _dot(m_ref[...], w_ref[...])
        y1_ref[...] = y1.astype(BF16)
        x1 = x_ref[...] + mod_ref[:, 2 * d:3 * d] * y1
        x1_ref[...] = x1
        xn, _ = _rms(x1)
        h2_ref[...] = (xn * (1.0 + mod_ref[:, 4 * d:5 * d]) + mod_ref[:, 3 * d:4 * d]).astype(BF16)

    return pl.pallas_call(
        body, name="out_proj", grid=(n // tm,),
        in_specs=[_rowblk(tm, d), _resident((d, d)), _rowblk(tm, d), _modspec(tps, 6 * d), ANY],
        out_specs=[_rowblk(tm, d), _rowblk(tm, d), _rowblk(tm, d)],
        out_shape=[jax.ShapeDtypeStruct((n, d), F32), jax.ShapeDtypeStruct((n, d), BF16),
                   jax.ShapeDtypeStruct((n, d), BF16)],
        compiler_params=pltpu.CompilerParams(dimension_semantics=("parallel",)),
    )(mixed, w_out, x, mod3, dep)


def _mlp_fwd(h2, x1, tgt, mod3, gfin, w1, w2, t):
    n, d = x1.shape
    dff = w1.shape[1]
    tm = min(TM_MLP, t)
    tps = t // tm
    nt = n // tm

    def body(h_ref, x1_ref, tg_ref, mod_ref, gf_ref, w1_ref, w2_ref,
             z_ref, dx2_ref, dy2_ref, dg2_ref, ggf_ref, loss_ref):
        i = pl.program_id(0)
        g2 = mod_ref[:, 5 * d:6 * d]
        gf = gf_ref[...]
        sub = tm // MLP_SUB
        sums = None
        for part in range(MLP_SUB):
            rs = slice(part * sub, (part + 1) * sub)
            hv = h_ref[rs, :]
            y2 = jnp.zeros((sub, d), F32)
            for j in range(dff // FF_CHUNK):
                cols = slice(j * FF_CHUNK, (j + 1) * FF_CHUNK)
                z = _dot(hv, w1_ref[:, cols])
                z_ref[rs, cols] = z.astype(BF16)
                zr = jnp.maximum(z, 0.0)
                y2 = y2 + _dot((zr * zr).astype(BF16), w2_ref[cols, :])
            x2n, r3 = _rms(x1_ref[rs, :] + g2 * y2)
            diff = x2n * gf - tg_ref[rs, :]
            dout = diff * (1.0 / d)
            dx2 = _rms_bwd(dout * gf, x2n, r3)
            dx2_ref[rs, :] = dx2
            dy2_ref[rs, :] = (g2 * dx2).astype(BF16)
            p = (_rows8(dx2 * y2), _rows8(dout * x2n), _rows8(diff * diff))
            sums = p if sums is None else tuple(a + b for a, b in zip(sums, p))
        _acc_add(dg2_ref, i % tps == 0, sums[0])
        _acc_add(ggf_ref, i == 0, sums[1])
        _acc_add(loss_ref, i == 0, sums[2])

    return pl.pallas_call(
        body, name="mlp_fwd", grid=(nt,),
        in_specs=[_rowblk(tm, d), _rowblk(tm, d), _rowblk(tm, d), _modspec(tps, 6 * d), _const((1, d)),
                  _resident((d, dff)), _resident((dff, d))],
        out_specs=[_rowblk(tm, dff), _rowblk(tm, d), _rowblk(tm, d), _accspec(tps, d),
                   _const((8, d)), _const((8, d))],
        out_shape=[jax.ShapeDtypeStruct((n, dff), BF16), jax.ShapeDtypeStruct((n, d), F32),
                   jax.ShapeDtypeStruct((n, d), BF16), jax.ShapeDtypeStruct((n // t, 8, d), F32),
                   jax.ShapeDtypeStruct((8, d), F32), jax.ShapeDtypeStruct((8, d), F32)],
        compiler_params=pltpu.CompilerParams(dimension_semantics=("arbitrary",)),
    )(h2, x1, tgt, mod3, gfin, w1, w2)


def _mlp_bwd(dy2, z, x1, dx2, y1, mod3, w1, w2, t):
    n, d = x1.shape
    dff = w1.shape[1]
    tm = min(TM_MLP, t)
    tps = t // tm

    def body(dy2_ref, z_ref, x1_ref, dx2_ref, y1_ref, mod_ref, w1_ref, w2_ref,
             dz_ref, dx1_ref, dy1_ref, dsh_ref, dsc_ref, dg1_ref):
        first = pl.program_id(0) % tps == 0
        sub = tm // MLP_SUB
        sums = None
        for part in range(MLP_SUB):
            rs = slice(part * sub, (part + 1) * sub)
            dy2 = dy2_ref[rs, :]
            dh2 = jnp.zeros((sub, d), F32)
            for j in range(dff // FF_CHUNK):
                cols = slice(j * FF_CHUNK, (j + 1) * FF_CHUNK)
                du = _dot_nt(dy2, w2_ref[cols, :])
                dz = (du * (2.0 * jnp.maximum(z_ref[rs, cols].astype(F32), 0.0))).astype(BF16)
                dz_ref[rs, cols] = dz
                dh2 = dh2 + _dot_nt(dz, w1_ref[:, cols])
            x1n, r2 = _rms(x1_ref[rs, :])
            dx1 = dx2_ref[rs, :] + _rms_bwd(dh2 * (1.0 + mod_ref[:, 4 * d:5 * d]), x1n, r2)
            dx1_ref[rs, :] = dx1
            dy1_ref[rs, :] = (mod_ref[:, 2 * d:3 * d] * dx1).astype(BF16)
            p = (_rows8(dh2), _rows8(dh2 * x1n), _rows8(dx1 * y1_ref[rs, :].astype(F32)))
            sums = p if sums is None else tuple(a + b for a, b in zip(sums, p))
        _acc_add(dsh_ref, first, sums[0])
        _acc_add(dsc_ref, first, sums[1])
        _acc_add(dg1_ref, first, sums[2])

    acc = jax.ShapeDtypeStruct((n // t, 8, d), F32)
    return pl.pallas_call(
        body, name="mlp_bwd", grid=(n // tm,),
        in_specs=[_rowblk(tm, d), _rowblk(tm, dff), _rowblk(tm, d), _rowblk(tm, d), _rowblk(tm, d),
                  _modspec(tps, 6 * d), _resident((d, dff)), _resident((dff, d))],
        out_specs=[_rowblk(tm, dff), _rowblk(tm, d), _rowblk(tm, d),
                   _accspec(tps, d), _accspec(tps, d), _accspec(tps, d)],
        out_shape=[jax.ShapeDtypeStruct((n, dff), BF16), jax.ShapeDtypeStruct((n, d), F32),
                   jax.ShapeDtypeStruct((n, d), BF16), acc, acc, acc],
        compiler_params=pltpu.CompilerParams(dimension_semantics=("arbitrary",)),
    )(dy2, z, x1, dx2, y1, mod3, w1, w2)


def _wgrad(a, b, name, relu2=False, bn=None, deps=()):
    n, ka = a.shape
    nb = b.shape[1]
    tk = min(TK_WG, n)
    bm = min(ka, 1024)
    if bn is None:
        bn = nb if nb <= 1024 else (1280 if nb % 1280 == 0 else 1024)

    def body(a_ref, b_ref, *rest):
        out_ref = rest[-1]
        av = a_ref[...]
        if relu2:
            ar = jnp.maximum(av, 0.0)
            av = ar * ar
        p = _dot_tn(av, b_ref[...])
        _acc_add(out_ref, pl.program_id(2) == 0, p)

    return pl.pallas_call(
        body, name=name, grid=(ka // bm, nb // bn, n // tk),
        in_specs=[pl.BlockSpec((tk, bm), lambda i, j, k: (k, i)),
                  pl.BlockSpec((tk, bn), lambda i, j, k: (k, j))] + [ANY] * len(deps),
        out_specs=pl.BlockSpec((bm, bn), lambda i, j, k: (i, j)),
        out_shape=jax.ShapeDtypeStruct((ka, nb), F32),
        compiler_params=pltpu.CompilerParams(dimension_semantics=("parallel", "parallel", "arbitrary")),
    )(a, b, *deps)


SG_DWW = 0
SG_DWB = CONF_K
SG_LNG = CONF_K + 1
SG_LNB = CONF_K + 2
SG_SCW = CONF_K + 3
SG_N = CONF_K + 3 + SC_K


def _mix_in_bwd(dy1, w_out, a1, q, proj, wb, lng, lnb, scw, w_in, x, dx1, mod3, t, dep):
    n, d_in = proj.shape
    d = x.shape[1]
    dc = d_in // 5
    tm = min(TM_MIX, t)
    tps = t // tm
    nt = n // tm
    rw = min(RB_WG, tm)
    ng = dc // 128
    ncol = 256

    def ln_bwd(a1v, da3, lng_v, lnb_v):
        mu = jnp.mean(a1v, axis=-1, keepdims=True)
        ac = a1v - mu
        rstd = lax.rsqrt(jnp.mean(ac * ac, axis=-1, keepdims=True) + EPS)
        ah = ac * rstd
        a2 = ah * lng_v + lnb_v
        s2 = _sigmoid(a2)
        da2 = da3 * (s2 * (1.0 + a2 * (1.0 - s2)))
        dah = da2 * lng_v
        da1 = rstd * (dah - jnp.mean(dah, axis=-1, keepdims=True)
                      - ah * jnp.mean(dah * ah, axis=-1, keepdims=True))
        return da1, da2, ah

    def dmixed_units(dy_ref, wo_ref, dm_ref):
        units = []
        for c0 in range(0, 2 * dc, ncol):
            def chunk(c0=c0):
                dm_ref[:, c0:c0 + ncol] = _dot_nt(dy_ref[...], wo_ref[c0:c0 + ncol, :])
            units.append(chunk)
        return units

    def mix_units(k, dm_ref, a1_ref, q_ref, p_ref, wb_ref, lng_ref, lnb_ref, scw_ref,
                  dproj_ref, sg_ref, extd_ref, ed_ref, a0_ref, da0_ref, extq_ref, cda_ref, cdq_ref):
        keep_next = jnp.where(k % tps == tps - 1, 0.0, 1.0)
        units = []

        def head():
            lng_v, lnb_v = lng_ref[...], lnb_ref[...]
            da1, da2, ah = ln_bwd(a1_ref[...], dm_ref[:, 0:dc], lng_v, lnb_v)
            sg_ref[8 * SG_LNG:8 * SG_LNG + 8, :] += _rows8(da2 * ah)
            sg_ref[8 * SG_LNB:8 * SG_LNB + 8, :] += _rows8(da2)
            sg_ref[8 * SG_DWB:8 * SG_DWB + 8, :] += _rows8(da1)
            a0_ref[...] = p_ref[:, 0:dc] * _sigmoid(p_ref[:, dc:2 * dc])
            extd_ref[0:tm, :] = da1
            extd_ref[tm:tm + HALO, :] = cda_ref[...] * keep_next
            extd_ref[tm + HALO:tm + HALO + 8, :] = jnp.zeros((8, dc), F32)
            cda_ref[...] = da1[0:HALO, :]
            ds = dm_ref[:, dc:2 * dc]
            dproj_ref[:, 2 * dc:3 * dc] = (ds * q_ref[...]).astype(BF16)
            dq = ds * p_ref[:, 2 * dc:3 * dc]
            extq_ref[0:tm, :] = dq
            extq_ref[tm:tm + 8, :] = cdq_ref[...] * keep_next
            cdq_ref[...] = dq[0:8, :]
        units.append(head)
        for g in range(ng):
            def shift(g=g):
                for r in range(8):
                    ed_ref[g, r, 0:tm + HALO, :] = extd_ref[r:r + tm + HALO, 128 * g:128 * g + 128]
            units.append(shift)
        for g in range(ng):
            lanes = slice(128 * g, 128 * g + 128)
            accs = [None] * CONF_K
            for i0 in range(0, tm, rw):
                def conv(g=g, lanes=lanes, i0=i0, accs=accs):
                    a0v = a0_ref[i0:i0 + rw, lanes]
                    acc = jnp.zeros((rw, 128), F32)
                    for s in range(CONF_K):
                        m, r = divmod(s, 8)
                        e = ed_ref[g, r, i0 + 8 * m:i0 + 8 * m + rw, :]
                        acc = acc + e * wb_ref[CONF_K - 1 - s:CONF_K - s, lanes]
                        part = _rows8(e * a0v)
                        accs[s] = part if accs[s] is None else accs[s] + part
                    da0_ref[i0:i0 + rw, lanes] = acc
                units.append(conv)

            def flush(lanes=lanes, accs=accs):
                for s in range(CONF_K):
                    kk = CONF_K - 1 - s
                    sg_ref[8 * (SG_DWW + kk):8 * (SG_DWW + kk) + 8, lanes] += accs[s]
            units.append(flush)

        def emit():
            da0 = da0_ref[...]
            sig = _sigmoid(p_ref[:, dc:2 * dc])
            dproj_ref[:, 0:dc] = (da0 * sig).astype(BF16)
            dproj_ref[:, dc:2 * dc] = (da0 * a0_ref[...] * (1.0 - sig)).astype(BF16)
        units.append(emit)

        def short():
            scc, sch = p_ref[:, 3 * dc:4 * dc], p_ref[:, 4 * dc:5 * dc]
            pv = scc * sch
            dp = jnp.zeros((tm, dc), F32)
            for kk in range(SC_K):
                dqs = extq_ref[SC_K - 1 - kk:SC_K - 1 - kk + tm, :]
                dp = dp + dqs * scw_ref[kk:kk + 1, :]
                sg_ref[8 * (SG_SCW + kk):8 * (SG_SCW + kk) + 8, :] += _rows8(pv * dqs)
            dproj_ref[:, 3 * dc:4 * dc] = (dp * sch).astype(BF16)
            dproj_ref[:, 4 * dc:5 * dc] = (dp * scc).astype(BF16)
        units.append(short)
        return units

    def proj_units(first, dp_ref, w_ref, x_ref, dx1_ref, mod_ref, dh_ref, gx_ref, dsh_ref, dsc_ref):
        units = []
        for c0 in range(0, d, ncol):
            def chunk(c0=c0):
                dh_ref[:, c0:c0 + ncol] = _dot_nt(dp_ref[...], w_ref[c0:c0 + ncol, :])
            units.append(chunk)

        def tail():
            dh1 = dh_ref[...]
            xn, r1 = _rms(x_ref[...])
            v1, v2 = _rows8(dh1), _rows8(dh1 * xn)
            dsh_ref[...] = jnp.where(first, v1, dsh_ref[...] + v1)
            dsc_ref[...] = jnp.where(first, v2, dsc_ref[...] + v2)
            gx_ref[...] = dx1_ref[...] + _rms_bwd(dh1 * (1.0 + mod_ref[:, d:2 * d]), xn, r1)
        units.append(tail)
        return units

    def body(dyl_ref, dy_ref, wo_ref, a1_ref, q_ref, p_ref, wb_ref, lng_ref, lnb_ref, scw_ref,
             w_ref, x_ref, dx1_ref, mod_ref, dep_ref,
             dproj_ref, sg_ref, gx_ref, dsh_ref, dsc_ref,
             extd_ref, ed_ref, a0_ref, da0_ref, extq_ref, dp_ref, dh_ref, dm_ref, cda_ref, cdq_ref):
        k = pl.program_id(0)

        @pl.when(k == 0)
        def _():
            sg_ref[...] = jnp.zeros(sg_ref.shape, F32)
            dp_ref[...] = jnp.zeros(dp_ref.shape, BF16)
            cda_ref[...] = jnp.zeros(cda_ref.shape, F32)
            cdq_ref[...] = jnp.zeros(cdq_ref.shape, F32)
            _run_units(dmixed_units(dyl_ref, wo_ref, dm_ref))

        first = jnp.logical_or(k <= 1, (nt - k) % tps == tps - 1)
        after = lambda: proj_units(first, dp_ref, w_ref, x_ref, dx1_ref, mod_ref, dh_ref, gx_ref, dsh_ref, dsc_ref)

        @pl.when(k < nt)
        def _():
            _run_units(mix_units(nt - 1 - k, dm_ref, a1_ref, q_ref, p_ref, wb_ref, lng_ref, lnb_ref, scw_ref,
                                 dproj_ref, sg_ref, extd_ref, ed_ref, a0_ref, da0_ref, extq_ref, cda_ref, cdq_ref),
                       after(), dmixed_units(dy_ref, wo_ref, dm_ref))
            dp_ref[...] = dproj_ref[...]

        @pl.when(k == nt)
        def _():
            _run_units(after())

    cur = lambda k: jnp.maximum(nt - 1 - k, 0)
    prev = lambda k: jnp.minimum(nt - k, nt - 1)
    acc = jax.ShapeDtypeStruct((n // t, 8, d), F32)
    return pl.pallas_call(
        body, name="mix_in_bwd", grid=(nt + 1,),
        in_specs=[pl.BlockSpec((tm, d), lambda k: (nt - 1, 0)),
                  pl.BlockSpec((tm, d), lambda k: (jnp.maximum(nt - 2 - k, 0), 0)), _resident(w_out.shape),
                  pl.BlockSpec((tm, dc), lambda k: (cur(k), 0)),
                  pl.BlockSpec((tm, dc), lambda k: (cur(k), 0)),
                  pl.BlockSpec((tm, d_in), lambda k: (cur(k), 0)),
                  _const(wb.shape), _const(lng.shape), _const(lnb.shape), _const(scw.shape),
                  _resident((d, d_in)),
                  pl.BlockSpec((tm, d), lambda k: (prev(k), 0)), pl.BlockSpec((tm, d), lambda k: (prev(k), 0)),
                  pl.BlockSpec((None, 1, 6 * d), lambda k: (prev(k) // tps, 0, 0)), ANY],
        out_specs=[pl.BlockSpec((tm, d_in), lambda k: (cur(k), 0)), _const((8 * SG_N, dc)),
                   pl.BlockSpec((tm, d), lambda k: (prev(k), 0)),
                   pl.BlockSpec((None, 8, d), lambda k: (prev(k) // tps, 0, 0)),
                   pl.BlockSpec((None, 8, d), lambda k: (prev(k) // tps, 0, 0))],
        out_shape=[jax.ShapeDtypeStruct((n, d_in), BF16), jax.ShapeDtypeStruct((8 * SG_N, dc), F32),
                   jax.ShapeDtypeStruct((n, d), F32), acc, acc],
        scratch_shapes=[pltpu.VMEM((tm + HALO + 8, dc), F32), _shift_scratch(tm, dc),
                        pltpu.VMEM((tm, dc), F32), pltpu.VMEM((tm, dc), F32),
                        pltpu.VMEM((tm + 8, dc), F32),
                        pltpu.VMEM((tm, d_in), BF16), pltpu.VMEM((tm, d), F32), pltpu.VMEM((tm, 2 * dc), F32),
                        pltpu.VMEM((HALO, dc), F32), pltpu.VMEM((8, dc), F32)],
        compiler_params=pltpu.CompilerParams(dimension_semantics=("arbitrary",)),
    )(dy1, dy1, w_out, a1, q, proj, wb, lng, lnb, scw, w_in, x, dx1, mod3, dep)


SMALL_ROWS = 40


def _pack_small(ids, sg, ggf, loss, accs, d, dep):
    dc = d // 2
    nb = accs[0].shape[0]

    def body(ids_ref, sg_ref, ggf_ref, loss_ref, dsh1, dsc1, dg1, dsh2, dsc2, dg2, dep_ref, pack_ref, dmod_ref):
        pack_ref[...] = jnp.zeros(pack_ref.shape, F32)
        for k in range(SG_N):
            pack_ref[k:k + 1, :] = jnp.sum(sg_ref[8 * k:8 * k + 8, :], axis=0, keepdims=True)
        gf = jnp.sum(ggf_ref[...], axis=0, keepdims=True)
        pack_ref[SG_N:SG_N + 1, :] = gf[:, 0:dc]
        pack_ref[SG_N + 1:SG_N + 2, :] = gf[:, dc:d]
        tot = jnp.sum(jnp.sum(loss_ref[...], axis=0, keepdims=True), axis=1, keepdims=True) * (0.5 / d)
        pack_ref[SG_N + 2:SG_N + 3, :] = jnp.broadcast_to(tot, (1, dc))
        for b in range(nb):
            row = jnp.concatenate([jnp.sum(ref[b], axis=0, keepdims=True)
                                   for ref in (dsh1, dsc1, dg1, dsh2, dsc2, dg2)], axis=1)
            for f in range(fold):
                dmod_ref[b * fold + f:b * fold + f + 1, :] = row[:, f * wf:(f + 1) * wf]

    fold = 8 // nb
    wf = 6 * d // fold
    assert nb * fold == 8 and wf % 128 == 0
    whole = lambda a: pl.BlockSpec(a.shape, lambda i, ids: (0,) * a.ndim)
    mine = lambda r, c: pl.BlockSpec((None, r, c), lambda i, ids: (ids[2], 0, 0))
    return pl.pallas_call(
        body, name="pack_small",
        grid_spec=pltpu.PrefetchScalarGridSpec(
            num_scalar_prefetch=1, grid=(1,),
            in_specs=[whole(a) for a in (sg, ggf, loss, *accs)] + [ANY],
            out_specs=[mine(SMALL_ROWS, dc), mine(8, wf)]),
        out_shape=[jax.ShapeDtypeStruct((N_DEV, SMALL_ROWS, dc), F32), jax.ShapeDtypeStruct((N_DEV, 8, wf), F32)],
    )(ids, sg, ggf, loss, *accs, dep)


def _small_reduce(pack_all, dmod_all, nb, dep):
    def body(pk_ref, dm_ref, dep_ref, red_ref, dmod_ref, gb_ref):
        tot = pk_ref[0]
        for dev in range(1, N_DEV):
            tot = tot + pk_ref[dev]
        red_ref[...] = tot
        for f in range(fold):
            gb = jnp.zeros((1, wf), F32)
            for dev in range(N_DEV):
                for b in range(nb):
                    seg = dm_ref[dev, b * fold + f:b * fold + f + 1, :]
                    dmod_ref[dev * nb + b:dev * nb + b + 1, f * wf:(f + 1) * wf] = seg
                    gb = gb + seg
            gb_ref[:, f * wf:(f + 1) * wf] = gb

    fold = 8 // nb
    wf = dmod_all.shape[2]
    return pl.pallas_call(
        body, name="small_reduce",
        out_shape=[jax.ShapeDtypeStruct(pack_all.shape[1:], F32),
                   jax.ShapeDtypeStruct((N_DEV * nb, fold * wf), F32),
                   jax.ShapeDtypeStruct((1, fold * wf), F32)],
        in_specs=[VMEM] * 2 + [ANY], out_specs=[VMEM] * 3,
    )(pack_all, dmod_all, dep)


def _adam(w, g, m, v):
    m = ADAM_B1 * m + (1.0 - ADAM_B1) * g
    v = ADAM_B2 * v + (1.0 - ADAM_B2) * (g * g)
    m_hat = m / (1.0 - ADAM_B1 ** ADAM_STEP)
    v_hat = v / (1.0 - ADAM_B2 ** ADAM_STEP)
    delta = -ADAM_LR * (m_hat / (jnp.sqrt(v_hat) + ADAM_EPS) + ADAM_WD * w)
    return delta, m, v


def _adamw_big(w, g, m, v, name, deps=()):
    r, c = w.shape
    tr = min(r, 256)

    def body(w_ref, g_ref, m_ref, v_ref, *rest):
        go_ref, d_ref, nm_ref, nv_ref = rest[len(deps):]
        g = g_ref[...]
        go_ref[...] = g
        d_ref[...], nm_ref[...], nv_ref[...] = _adam(w_ref[...], g, m_ref[...], v_ref[...])

    s = jax.ShapeDtypeStruct((r, c), F32)
    return pl.pallas_call(
        body, name=name, grid=(r // tr,),
        in_specs=[_rowblk(tr, c)] * 4 + [ANY] * len(deps), out_specs=[_rowblk(tr, c)] * 4, out_shape=[s, s, s, s],
        compiler_params=pltpu.CompilerParams(dimension_semantics=("parallel",)),
    )(w, g, m, v, *deps)


def _adamw_ada(act_t, dmod_cols, w, m, v):
    r, c = w.shape
    tr = min(r, 256)
    nb = act_t.shape[1]

    def body(a_ref, dm_ref, w_ref, m_ref, v_ref, g_ref, d_ref, nm_ref, nv_ref):
        g = jnp.dot(a_ref[...], dm_ref[...], preferred_element_type=F32, precision=HIGHEST)
        g_ref[...] = g
        d_ref[...], nm_ref[...], nv_ref[...] = _adam(w_ref[...], g, m_ref[...], v_ref[...])

    s = jax.ShapeDtypeStruct((r, c), F32)
    return pl.pallas_call(
        body, name="adamw_w_ada", grid=(r // tr,),
        in_specs=[_rowblk(tr, nb), _const((nb, c))] + [_rowblk(tr, c)] * 3,
        out_specs=[_rowblk(tr, c)] * 4, out_shape=[s, s, s, s],
        compiler_params=pltpu.CompilerParams(dimension_semantics=("parallel",)),
    )(act_t, dmod_cols, w, m, v)


def _adamw_small(ids, red, g_bada, ws, ms, vs):
    n = len(ws)
    cw = ws[1].shape[1]

    def body(ids_ref, red_ref, gb_ref, *refs):
        cols = pl.ds(pl.multiple_of(ids_ref[0] * cw, 128), cw)
        gs = [gb_ref[...], red_ref[SG_DWW:SG_DWW + CONF_K, cols], red_ref[SG_DWB:SG_DWB + 1, :],
              red_ref[SG_LNG:SG_LNG + 1, :], red_ref[SG_LNB:SG_LNB + 1, :], red_ref[SG_SCW:SG_SCW + SC_K, cols],
              jnp.concatenate([red_ref[SG_N:SG_N + 1, :], red_ref[SG_N + 1:SG_N + 2, :]], axis=1)]
        for i in range(n):
            w, m, v = (refs[j * n + i][...] for j in range(3))
            dl, nm, nv = _adam(w, gs[i], m, v)
            refs[3 * n + i][...] = gs[i]
            refs[4 * n + i][...] = dl
            refs[5 * n + i][...] = nm
            refs[6 * n + i][...] = nv

    whole = lambda a: pl.BlockSpec(a.shape, lambda i, ids: (0,) * a.ndim)
    shapes = [jax.ShapeDtypeStruct(w.shape, F32) for w in ws]
    return pl.pallas_call(
        body, name="adamw_small",
        grid_spec=pltpu.PrefetchScalarGridSpec(
            num_scalar_prefetch=1, grid=(1,),
            in_specs=[whole(a) for a in (red, g_bada, *ws, *ms, *vs)],
            out_specs=[whole(a) for a in ws] * 4),
        out_shape=shapes * 4,
    )(ids, red, g_bada, *ws, *ms, *vs)


def kernel(x, c, w_ada, b_ada, w_in, conf_dw_w, conf_dw_b, conf_ln_g, conf_ln_b, sc_conv_w, w_out, w_mlp1, w_mlp2, g_final, loss_target, m_w_ada, m_b_ada, m_w_in, m_conf_dw_w, m_conf_dw_b, m_conf_ln_g, m_conf_ln_b, m_sc_conv_w, m_w_out, m_w_mlp1, m_w_mlp2, m_g_final, v_w_ada, v_b_ada, v_w_in, v_conf_dw_w, v_conf_dw_b, v_conf_ln_g, v_conf_ln_b, v_sc_conv_w, v_w_out, v_w_mlp1, v_w_mlp2, v_g_final):
    nb, t, d = x.shape
    n = nb * t
    dc = d // 2
    ada_w = w_ada.shape[2]
    ax, ay, ac = _me()
    chip = 2 * ax + ay
    dev = 2 * chip + ac
    ids = jnp.stack([chip, ac, dev]).astype(jnp.int32)

    lays = _wlayout(d)
    names = ("in", "out", "mlp1", "mlp2")
    fulls = _cast_place(ids, [w_in[0], w_out[0], w_mlp1[0], w_mlp2[0]], lays)

    c_pad = jnp.zeros((8, d), F32).at[0:nb].set(c)
    cw_pad = jnp.zeros((SMALL_ROWS, dc // N_CHIPS), F32)
    cw_pad = cw_pad.at[0:CONF_K].set(conf_dw_w[0]).at[HALO:HALO + SC_K].set(sc_conv_w[0])
    c_all8, cw_all8 = _all_gather8([c_pad, cw_pad], "gather_c")
    plan_i = _gather_chip_plan(lays[0:1])
    sems_i, bufs_i, tok_i = _copy_start("gather_in_start", [fulls[0]], 3, plan_i, [c_all8])
    c_all = c_all8[:, 0:nb].reshape(N_DEV * nb, d) + tok_i[0, 0]
    cw_full = jnp.concatenate([cw_all8[2 * k] for k in range(N_CHIPS)], axis=1)
    dww, scw = cw_full[0:CONF_K], cw_full[HALO:HALO + SC_K]
    b_cols = lax.dynamic_slice(b_ada, (0, chip * ada_w), (1, ada_w))
    c_act, mod_shard = _ada_mod(c_all, w_ada[0], b_cols)
    mod_all = _gather_mod(mod_shard)
    mod3 = lax.dynamic_slice(mod_all, (dev * nb, 0), (nb, 6 * d)).reshape(nb, 1, 6 * d)

    x2 = x.reshape(n, d)
    tgt = loss_target.reshape(n, d)
    (wf_in,) = _copy_wait("gather_in_wait", bufs_i, sems_i, plan_i, [mod3])
    (wf_in,) = _copy_blocking("gather_in_pass", [wf_in], 3, _gather_pass_plan(lays[0:1]))
    plan_o, plan_b, plan_p = _gather_direct_plan(lays[1:2]), _gather_chip_plan(lays[2:4]), _gather_pass_plan(lays[2:4])
    sems_o, bufs_o, tok_o = _copy_start("gather_out_start", [fulls[1]], 6, plan_o, [wf_in, mod3])
    sems_b, bufs_b, tok_b = _copy_start("gather_mlp_start", fulls[2:4], 6, plan_b, [tok_o])
    proj, h1, mixed, a1, q = _in_mix_fwd(x2, mod3, wf_in, dww, conf_dw_b, conf_ln_g, conf_ln_b, scw, t, tok_b)
    (wf_out,) = _copy_wait("gather_out_wait", bufs_o, sems_o, plan_o, [mixed])
    bufs_b = _copy_wait("gather_mlp_wait", bufs_b, sems_b, plan_b, [mixed])
    sems_p, bufs_p, tok_p = _copy_start("gather_pass_start", bufs_b, 6, plan_p)
    x1, y1, h2 = _out_proj(mixed, wf_out, x2, mod3, t, tok_p)
    wf_1, wf_2 = _copy_wait("gather_pass_wait", bufs_p, sems_p, plan_p, [h2])
    z, dx2, dy2, dg2, ggf, loss_p = _mlp_fwd(h2, x1, tgt, mod3, g_final.reshape(1, d), wf_1, wf_2, t)

    dz, dx1, dy1, dsh2, dsc2, dg1 = _mlp_bwd(dy2, z, x1, dx2, y1, mod3, wf_1, wf_2, t)
    g_w2 = _wgrad(z, dy2, "wgrad_mlp2", relu2=True)
    g_w1 = _wgrad(h2, dz, "wgrad_mlp1")
    made = {}

    def behind_pair_exchange(tok):
        made["g_wout"] = _wgrad(mixed, dy1, "wgrad_out", deps=[tok])
        return [made["g_wout"]]

    def behind_chip_exchange(tok):
        made["dproj"], made["sg"], made["grad_x"], made["dsh1"], made["dsc1"] = _mix_in_bwd(
            dy1, wf_out, a1, q, proj, dww, conf_ln_g, conf_ln_b, scw, wf_in, x2, dx1, mod3, t, tok)
        return [made["dproj"]]

    def behind_pair_share(tok):
        made["g_win"] = _wgrad(h1, made["dproj"], "wgrad_in", deps=[tok])
        return [made["g_win"]]

    gr_1, gr_2 = _reduce_scatter(ids, [g_w1, g_w2], lays[2:4], names[2:4], "m",
                                 (behind_pair_exchange, behind_chip_exchange, behind_pair_share))

    big = {}

    plan_g = _gather8_plan(2)

    def behind_pair_exchange_in(tok):
        packs = _pack_small(ids, made["sg"], ggf, loss_p, (made["dsh1"], made["dsc1"], dg1, dsh2, dsc2, dg2), d, tok)
        made["gather"] = _copy_start("gather_small_start", list(packs), 2 * (N_DEV - 1), plan_g)
        return [made["gather"][2]]

    def behind_chip_exchange_in(tok):
        sems_g, bufs_g, _ = made["gather"]
        gathered = _copy_wait("gather_small_wait", bufs_g, sems_g, plan_g, [tok])
        red, dmod_all, g_bada = _small_reduce(*gathered, nb, tok)
        dmod_cols = lax.dynamic_slice(dmod_all, (0, chip * ada_w), (N_DEV * nb, ada_w))
        made["ada"] = _adamw_ada(c_act.T, dmod_cols, w_ada[0], m_w_ada[0], v_w_ada[0])
        big["w_mlp1"] = _adamw_big(w_mlp1[0], gr_1, m_w_mlp1[0], v_w_mlp1[0], "adamw_w_mlp1")
        small_w = [b_ada, conf_dw_w[0], conf_dw_b, conf_ln_g, conf_ln_b, sc_conv_w[0], g_final.reshape(1, d)]
        small_m = [m_b_ada, m_conf_dw_w[0], m_conf_dw_b, m_conf_ln_g, m_conf_ln_b, m_sc_conv_w[0],
                   m_g_final.reshape(1, d)]
        small_v = [v_b_ada, v_conf_dw_w[0], v_conf_dw_b, v_conf_ln_g, v_conf_ln_b, v_sc_conv_w[0],
                   v_g_final.reshape(1, d)]
        made["upd"] = _adamw_small(ids, red, g_bada, small_w, small_m, small_v)
        made["loss"] = red[SG_N + 2, 0]
        return [big["w_mlp1"][0], made["upd"][0]]

    def behind_pair_share_in(tok):
        big["w_mlp2"] = _adamw_big(w_mlp2[0], gr_2, m_w_mlp2[0], v_w_mlp2[0], "adamw_w_mlp2", deps=[tok])
        return [big["w_mlp2"][0]]

    gr_in, gr_out = _reduce_scatter(ids, [made["g_win"], made["g_wout"]], lays[0:2], names[0:2], "i",
                                    (behind_pair_exchange_in, behind_chip_exchange_in, behind_pair_share_in))
    big["w_in"] = _adamw_big(w_in[0], gr_in, m_w_in[0], v_w_in[0], "adamw_w_in")
    big["w_out"] = _adamw_big(w_out[0], gr_out, m_w_out[0], v_w_out[0], "adamw_w_out")
    grad_x, upd, loss = made["grad_x"], made["upd"], made["loss"]
    g_wada, d_wada, nm_wada, nv_wada = made["ada"]
    ns = len(upd) // 4
    small_g, s_delta, s_m, s_v = upd[0:ns], upd[ns:2 * ns], upd[2 * ns:3 * ns], upd[3 * ns:4 * ns]

    def outs(kind_big, kind_small, wada):
        sm = kind_small
        return (wada[None], sm[0], kind_big["w_in"][None], sm[1][None], sm[2], sm[3], sm[4], sm[5][None],
                kind_big["w_out"][None], kind_big["w_mlp1"][None], kind_big["w_mlp2"][None], sm[6].reshape(d))

    grads_out = outs({k: v[0] for k, v in big.items()}, small_g, g_wada)
    delta_out = outs({k: v[1] for k, v in big.items()}, s_delta, d_wada)
    m_out = outs({k: v[2] for k, v in big.items()}, s_m, nm_wada)
    v_out = outs({k: v[3] for k, v in big.items()}, s_v, nv_wada)
    return (loss, grad_x.reshape(nb, t, d), *grads_out, *delta_out, *m_out, *v_out)
```

```python
import jax
import jax.numpy as jnp
from jax import lax
from jax.experimental import pallas as pl
from jax.experimental.pallas import tpu as pltpu

F32 = jnp.float32
BF16 = jnp.bfloat16
MESH = pl.DeviceIdType.MESH
HIGHEST = lax.Precision.HIGHEST

EPS = 1e-6
CONF_K = 31
SC_K = 3
HALO = 32
N_CHIPS = 4
N_DEV = 8

ADAM_LR = 0.001
ADAM_B1 = 0.9
ADAM_B2 = 0.999
ADAM_EPS = 1e-08
ADAM_WD = 0.01
ADAM_STEP = 10

TM_MM = 1024
TM_MIX = 256
TM_MLP = 512
FF_CHUNK = 1024
MLP_SUB = 2
TK_WG = 4096
RB_CONV = 32
RB_WG = 32
CHIP_RELS = ((1, 0), (0, 1), (1, 1))

ANY = pl.BlockSpec(memory_space=pl.ANY)
VMEM = pl.BlockSpec(memory_space=pltpu.VMEM)
HBM = pl.BlockSpec(memory_space=pltpu.HBM)
SEM = pl.BlockSpec(memory_space=pltpu.SEMAPHORE)
EFFECT = pltpu.SideEffectType.DATAFLOW_SIDE_EFFECTING


def _me():
    return lax.axis_index("x"), lax.axis_index("y"), lax.axis_index("c")


def _flip(v, f):
    return 1 - v if f else v


def _rows8(v):
    r, c = v.shape
    return v.reshape(r // 8, 8, c).sum(axis=0)


def _rms(x):
    r = lax.rsqrt(jnp.mean(x * x, axis=-1, keepdims=True) + EPS)
    return x * r, r


def _rms_bwd(dxn, xn, r):
    return r * (dxn - xn * jnp.mean(dxn * xn, axis=-1, keepdims=True))


def _sigmoid(x):
    return 1.0 / (1.0 + jnp.exp(-x))


def _dot(a, b):
    return jnp.dot(a, b, preferred_element_type=F32)


def _dot_nt(a, b):
    return lax.dot_general(a, b, (((1,), (1,)), ((), ())), preferred_element_type=F32)


def _dot_tn(a, b):
    return lax.dot_general(a, b, (((0,), (0,)), ((), ())), preferred_element_type=F32)


def _const(shape):
    nd = len(shape)
    return pl.BlockSpec(shape, lambda i: (0,) * nd)


def _resident(shape):
    nd = len(shape)
    return pl.BlockSpec(shape, lambda i: (0,) * nd, pipeline_mode=pl.Buffered(1))


def _rowblk(tm, c):
    return pl.BlockSpec((tm, c), lambda i: (i, 0))


def _modspec(tps, width):
    return pl.BlockSpec((None, 1, width), lambda i: (i // tps, 0, 0))


def _accspec(tps, c):
    return pl.BlockSpec((None, 8, c), lambda i: (i // tps, 0, 0))


def _acc_add(ref, first, v):
    @pl.when(first)
    def _():
        ref[...] = v

    @pl.when(jnp.logical_not(first))
    def _():
        ref[...] += v


def _all_gather8(arrs, name):
    n = len(arrs)

    def body(*refs):
        ins, outs = refs[:n], refs[n:2 * n]
        send, recv = refs[2 * n:]
        x, y, c = _me()
        dev = 4 * x + 2 * y + c
        for a in range(n):
            outs[a][dev] = ins[a][...]
        sends = []
        for r in range(1, N_DEV):
            fx, fy, fc = (r >> 2) & 1, (r >> 1) & 1, r & 1
            peer = (_flip(x, fx), _flip(y, fy), _flip(c, fc))
            for a in range(n):
                cp = pltpu.make_async_remote_copy(
                    src_ref=ins[a], dst_ref=outs[a].at[dev],
                    send_sem=send.at[r - 1, a], recv_sem=recv.at[r - 1, a],
                    device_id=peer, device_id_type=MESH)
                cp.start()
                sends.append(cp)
        for r in range(1, N_DEV):
            fx, fy, fc = (r >> 2) & 1, (r >> 1) & 1, r & 1
            pdev = 4 * _flip(x, fx) + 2 * _flip(y, fy) + _flip(c, fc)
            for a in range(n):
                pltpu.make_async_remote_copy(
                    src_ref=ins[a], dst_ref=outs[a].at[pdev],
                    send_sem=send.at[r - 1, a], recv_sem=recv.at[r - 1, a],
                    device_id=(x, y, c), device_id_type=MESH).wait_recv()
        for cp in sends:
            cp.wait_send()

    return pl.pallas_call(
        body, name=name,
        out_shape=[jax.ShapeDtypeStruct((N_DEV,) + a.shape, a.dtype) for a in arrs],
        in_specs=[VMEM] * n, out_specs=[VMEM] * n,
        scratch_shapes=[pltpu.SemaphoreType.DMA((N_DEV - 1, n)),
                        pltpu.SemaphoreType.DMA((N_DEV - 1, n))],
    )(*arrs)


def _gather_mod(mod_shard):
    nb, w = mod_shard.shape

    def body(in_ref, out_ref, send, recv):
        x, y, c = _me()
        chip = 2 * x + y
        out_ref[:, pl.ds(pl.multiple_of(chip * w, 128), w)] = in_ref[...]
        sends = []
        for r, (fx, fy) in enumerate(CHIP_RELS):
            cp = pltpu.make_async_remote_copy(
                src_ref=in_ref,
                dst_ref=out_ref.at[:, pl.ds(pl.multiple_of(chip * w, 128), w)],
                send_sem=send.at[r], recv_sem=recv.at[r],
                device_id=(_flip(x, fx), _flip(y, fy), c), device_id_type=MESH)
            cp.start()
            sends.append(cp)
        for r, (fx, fy) in enumerate(CHIP_RELS):
            pchip = 2 * _flip(x, fx) + _flip(y, fy)
            pltpu.make_async_remote_copy(
                src_ref=in_ref,
                dst_ref=out_ref.at[:, pl.ds(pl.multiple_of(pchip * w, 128), w)],
                send_sem=send.at[r], recv_sem=recv.at[r],
                device_id=(x, y, c), device_id_type=MESH).wait_recv()
        for cp in sends:
            cp.wait_send()

    return pl.pallas_call(
        body, name="gather_mod",
        out_shape=jax.ShapeDtypeStruct((nb, N_CHIPS * w), mod_shard.dtype),
        in_specs=[VMEM], out_specs=VMEM,
        scratch_shapes=[pltpu.SemaphoreType.DMA((3,)), pltpu.SemaphoreType.DMA((3,))],
    )(mod_shard)


def _wlayout(d):
    d_in = 5 * d // 2
    return (
        (d, d_in // N_CHIPS, True),
        (d // N_CHIPS, d, False),
        (d, 4 * d // N_CHIPS, True),
        (4 * d // N_CHIPS, d, False),
    )


def _full_shape(lay):
    r, c, by_col = lay
    return (r, c * N_CHIPS) if by_col else (r * N_CHIPS, c)


def _full_view(ref, lay, k, h):
    r, c, by_col = lay
    hr = r // 2
    if by_col:
        return ref.at[pl.ds(pl.multiple_of(h * hr, 16), hr), pl.ds(pl.multiple_of(k * c, 128), c)]
    return ref.at[pl.ds(pl.multiple_of(k * r + h * hr, 16), hr), :]


def _half_view(ref, lay, h):
    hr = lay[0] // 2
    return ref.at[pl.ds(pl.multiple_of(h * hr, 16), hr), :]


def _half_shape(lay):
    return (lay[0] // 2, lay[1])


def _hbm(a):
    return pltpu.with_memory_space_constraint(a, pltpu.HBM)


def _remote(src, dst, send, recv, idx, to):
    return lambda: pltpu.make_async_remote_copy(src_ref=src, dst_ref=dst, send_sem=send.at[idx], recv_sem=recv.at[idx],
                                                device_id=to, device_id_type=MESH)


def _copy_start(name, bufs, n_sems, plan, after=()):
    nb, na = len(bufs), len(after)

    def body(*refs):
        sends, _ = plan(refs[:nb], refs[nb + na], refs[nb + na + 1])
        for mk in sends:
            mk().start()
        refs[-1][...] = jnp.zeros((8, 128), F32)

    outs = pl.pallas_call(
        body, name=name,
        out_shape=(pltpu.SemaphoreType.DMA((n_sems,)), pltpu.SemaphoreType.DMA((n_sems,)))
        + tuple(pltpu.HBM(b.shape, b.dtype) for b in bufs) + (jax.ShapeDtypeStruct((8, 128), F32),),
        in_specs=(HBM,) * nb + (ANY,) * na, out_specs=(SEM, SEM) + (HBM,) * nb + (VMEM,),
        input_output_aliases={i: 2 + i for i in range(nb)},
        compiler_params=pltpu.CompilerParams(has_side_effects=EFFECT),
    )(*[_hbm(b) for b in bufs], *after)
    return (outs[0], outs[1]), list(outs[2:2 + nb]), outs[-1]


def _copy_wait(name, bufs, sems, plan, after):
    nb, na = len(bufs), len(after)

    def body(*refs):
        sends, recvs = plan(refs[:nb], refs[nb], refs[nb + 1])
        for mk in sends:
            mk().wait_send()
        for mk in recvs:
            mk().wait_recv()

    outs = pl.pallas_call(
        body, name=name,
        out_shape=tuple(pltpu.HBM(b.shape, b.dtype) for b in bufs),
        in_specs=(HBM,) * nb + (SEM, SEM) + (ANY,) * na, out_specs=(HBM,) * nb,
        input_output_aliases={i: i for i in range(nb)},
        compiler_params=pltpu.CompilerParams(has_side_effects=EFFECT),
    )(*bufs, *sems, *after)
    return list(outs)


def _copy_blocking(name, bufs, n_sems, plan, after=()):
    nb, na = len(bufs), len(after)

    def body(*refs):
        sends, recvs = plan(refs[:nb], refs[2 * nb + na], refs[2 * nb + na + 1])
        started = [mk() for mk in sends]
        for cp in started:
            cp.start()
        for mk in recvs:
            mk().wait_recv()
        for cp in started:
            cp.wait_send()

    return list(pl.pallas_call(
        body, name=name,
        out_shape=tuple(jax.ShapeDtypeStruct(b.shape, b.dtype) for b in bufs),
        in_specs=(ANY,) * (nb + na), out_specs=(ANY,) * nb,
        input_output_aliases={i: i for i in range(nb)},
        scratch_shapes=[pltpu.SemaphoreType.DMA((n_sems,)), pltpu.SemaphoreType.DMA((n_sems,))],
    )(*bufs, *after))


def _exchange(name, bufs, n_sems, plan, between):
    if between is None:
        return _copy_blocking(name, bufs, n_sems, plan)
    sems, bufs, tok = _copy_start(name + "_start", bufs, n_sems, plan)
    return _copy_wait(name + "_wait", bufs, sems, plan, between(tok))


def _gather_direct_plan(lays):
    def plan(full, send, recv):
        x, y, c = _me()
        chip = 2 * x + y
        sends, recvs = [], []
        for r, (fx, fy) in enumerate(CHIP_RELS):
            px, py = _flip(x, fx), _flip(y, fy)
            for i, lay in enumerate(lays):
                for q in range(2):
                    oc = _flip(c, q)
                    mine = _full_view(full[i], lay, chip, c)
                    idx = (r * len(lays) + i) * 2 + q
                    sends.append(_remote(mine, mine, send, recv, idx, (px, py, oc)))
                    theirs = _full_view(full[i], lay, 2 * px + py, oc)
                    recvs.append(_remote(theirs, theirs, send, recv, idx, (x, y, c)))
        return sends, recvs
    return plan


def _gather_chip_plan(lays):
    def plan(full, send, recv):
        x, y, c = _me()
        chip = 2 * x + y
        sends, recvs = [], []
        for r, (fx, fy) in enumerate(CHIP_RELS):
            px, py = _flip(x, fx), _flip(y, fy)
            for i, lay in enumerate(lays):
                mine = _full_view(full[i], lay, chip, c)
                sends.append(_remote(mine, mine, send, recv, r * len(lays) + i, (px, py, c)))
                theirs = _full_view(full[i], lay, 2 * px + py, c)
                recvs.append(_remote(theirs, theirs, send, recv, r * len(lays) + i, (x, y, c)))
        return sends, recvs
    return plan


def _gather_pass_plan(lays):
    def plan(full, send, recv):
        x, y, c = _me()
        sends, recvs = [], []
        for r, (fx, fy) in enumerate(CHIP_RELS):
            pchip = 2 * _flip(x, fx) + _flip(y, fy)
            for i, lay in enumerate(lays):
                landed = _full_view(full[i], lay, pchip, c)
                sends.append(_remote(landed, landed, send, recv, r * len(lays) + i, (x, y, 1 - c)))
                other = _full_view(full[i], lay, pchip, 1 - c)
                recvs.append(_remote(other, other, send, recv, r * len(lays) + i, (x, y, c)))
        return sends, recvs
    return plan


def _gather8_plan(na):
    def plan(bufs, send, recv):
        x, y, c = _me()
        dev = 4 * x + 2 * y + c
        sends, recvs = [], []
        for r in range(1, N_DEV):
            fx, fy, fc = (r >> 2) & 1, (r >> 1) & 1, r & 1
            px, py, pc = _flip(x, fx), _flip(y, fy), _flip(c, fc)
            for a in range(na):
                idx = (r - 1) * na + a
                sends.append(_remote(bufs[a].at[dev], bufs[a].at[dev], send, recv, idx, (px, py, pc)))
                theirs = bufs[a].at[4 * px + 2 * py + pc]
                recvs.append(_remote(theirs, theirs, send, recv, idx, (x, y, c)))
        return sends, recvs
    return plan


def _pair_exchange_plan(lays):
    nw = len(lays)

    def plan(bufs, send, recv):
        x, y, c = _me()
        sends, recvs = [], []
        for i, lay in enumerate(lays):
            for k in range(N_CHIPS):
                sends.append(_remote(_full_view(bufs[i], lay, k, 1 - c), bufs[nw + i].at[k],
                                     send, recv, i * N_CHIPS + k, (x, y, 1 - c)))
                recvs.append(_remote(_full_view(bufs[i], lay, k, c), bufs[nw + i].at[k],
                                     send, recv, i * N_CHIPS + k, (x, y, c)))
        return sends, recvs
    return plan


def _chip_exchange_plan(nw):
    def plan(bufs, send, recv):
        x, y, c = _me()
        sends, recvs = [], []
        for r, (fx, fy) in enumerate(CHIP_RELS):
            px, py = _flip(x, fx), _flip(y, fy)
            for i in range(nw):
                sends.append(_remote(bufs[i].at[2 * px + py], bufs[nw + i].at[r], send, recv, r * nw + i, (px, py, c)))
                recvs.append(_remote(bufs[i].at[2 * px + py], bufs[nw + i].at[r], send, recv, r * nw + i, (x, y, c)))
        return sends, recvs
    return plan


def _pair_share_plan(lays):
    def plan(bufs, send, recv):
        x, y, c = _me()
        sends, recvs = [], []
        for i, lay in enumerate(lays):
            mine = _half_view(bufs[i], lay, c)
            sends.append(_remote(mine, mine, send, recv, i, (x, y, 1 - c)))
            other = _half_view(bufs[i], lay, 1 - c)
            recvs.append(_remote(other, other, send, recv, i, (x, y, c)))
        return sends, recvs
    return plan


def _pair_sum(ids, g, got, lay, name):
    r, c, by_col = lay
    hr = r // 2
    tr = min(hr, 256)
    nt = hr // tr

    def body(ids_ref, g_ref, got_ref, s32_ref, s16_ref):
        s = g_ref[...] + got_ref[...]
        s16_ref[...] = s.astype(BF16)

        @pl.when(pl.program_id(1) == ids_ref[0])
        def _():
            s32_ref[...] = s

    if by_col:
        gspec = pl.BlockSpec((tr, c), lambda t, k, ids: (ids[1] * nt + t, k))
    else:
        gspec = pl.BlockSpec((tr, c), lambda t, k, ids: ((2 * k + ids[1]) * nt + t, 0))
    hspec = pl.BlockSpec((None, tr, c), lambda t, k, ids: (k, t, 0))
    return pl.pallas_call(
        body, name=name,
        grid_spec=pltpu.PrefetchScalarGridSpec(
            num_scalar_prefetch=1, grid=(nt, N_CHIPS),
            in_specs=[gspec, hspec], out_specs=[pl.BlockSpec((tr, c), lambda t, k, ids: (t, 0)), hspec]),
        out_shape=[jax.ShapeDtypeStruct((hr, c), F32),
                   jax.ShapeDtypeStruct((N_CHIPS, hr, c), BF16)],
    )(ids, g, got)


def _chip_sum(ids, s32, got, lay, name):
    hr, c = _half_shape(lay)
    tr = min(hr, 256)
    nt = hr // tr

    def body(ids_ref, s_ref, got_ref, out_ref):
        t = s_ref[...]
        for r in range(3):
            t = t + got_ref[r].astype(F32)
        out_ref[...] = t

    return pl.pallas_call(
        body, name=name,
        grid_spec=pltpu.PrefetchScalarGridSpec(
            num_scalar_prefetch=1, grid=(nt,),
            in_specs=[pl.BlockSpec((tr, c), lambda t, ids: (t, 0)),
                      pl.BlockSpec((3, tr, c), lambda t, ids: (0, t, 0))],
            out_specs=pl.BlockSpec((tr, c), lambda t, ids: (ids[1] * nt + t, 0))),
        out_shape=jax.ShapeDtypeStruct((2 * hr, c), F32),
    )(ids, s32, got)


def _reduce_scatter(ids, grads, lays, names, tag, between):
    nw = len(lays)
    got1 = [lax.empty((N_CHIPS,) + _half_shape(l), F32) for l in lays]
    bufs = _exchange("pair_exchange_" + tag, list(grads) + got1, nw * N_CHIPS, _pair_exchange_plan(lays), between[0])
    sums = [_pair_sum(ids, bufs[i], bufs[nw + i], lays[i], "pair_sum_" + names[i]) for i in range(nw)]
    got2 = [lax.empty((3,) + _half_shape(l), BF16) for l in lays]
    bufs = _exchange("chip_exchange_" + tag, [s[1] for s in sums] + got2, 3 * nw, _chip_exchange_plan(nw), between[1])
    mine = [_chip_sum(ids, sums[i][0], bufs[nw + i], lays[i], "chip_sum_" + names[i]) for i in range(nw)]
    return _exchange("pair_share_" + tag, mine, nw, _pair_share_plan(lays), between[2])


def _cast_place(ids, ws, lays):
    nw = len(ws)

    def body(ids_ref, *refs):
        for i in range(nw):
            refs[nw + i][...] = refs[i][...].astype(BF16)

    by_col_map = lambda i, ids: (0, ids[0])
    by_row_map = lambda i, ids: (ids[0], 0)
    return pl.pallas_call(
        body, name="cast_place",
        grid_spec=pltpu.PrefetchScalarGridSpec(
            num_scalar_prefetch=1, grid=(1,),
            in_specs=[pl.BlockSpec(l[:2], lambda i, ids: (0, 0)) for l in lays],
            out_specs=[pl.BlockSpec(l[:2], by_col_map if l[2] else by_row_map) for l in lays]),
        out_shape=[jax.ShapeDtypeStruct(_full_shape(l), BF16) for l in lays],
    )(ids, *ws)


def _ada_mod(c_all, w_ada, b_ada):
    def body(c_ref, w_ref, b_ref, act_ref, mod_ref):
        cv = c_ref[...]
        act = cv * _sigmoid(cv)
        act_ref[...] = act
        mod_ref[...] = jnp.dot(act, w_ref[...], preferred_element_type=F32, precision=HIGHEST) + b_ref[...]

    nb = c_all.shape[0]
    return pl.pallas_call(
        body, name="ada_mod",
        out_shape=[jax.ShapeDtypeStruct(c_all.shape, F32),
                   jax.ShapeDtypeStruct((nb, w_ada.shape[1]), F32)],
        in_specs=[VMEM] * 3, out_specs=[VMEM] * 2,
    )(c_all, w_ada, b_ada)


def _run_units(*unit_lists):
    total = max(len(u) for u in unit_lists)
    done = [0] * len(unit_lists)
    for step in range(1, total + 1):
        for li, units in enumerate(unit_lists):
            upto = (step * len(units) + total - 1) // total
            while done[li] < upto:
                units[done[li]]()
                done[li] += 1


def _in_mix_fwd(x, mod3, w_in, wb, dwb, lng, lnb, scw, t, dep):
    n, d = x.shape
    d_in = w_in.shape[1]
    dc = d_in // 5
    tm = min(TM_MIX, t)
    tps = t // tm
    nt = n // tm
    rb = min(RB_CONV, tm)
    ncol = 256
    ng = dc // 128
    assert tps % 2 == 0 and nt % 2 == 0

    def proj_units(x_ref, mod_ref, w_ref, p_ref, h_ref):
        def head():
            xn, _ = _rms(x_ref[...])
            h_ref[...] = (xn * (1.0 + mod_ref[:, d:2 * d]) + mod_ref[:, 0:d]).astype(BF16)
        units = [head]
        for c0 in range(0, d_in, ncol):
            def chunk(c0=c0):
                p_ref[:, c0:c0 + ncol] = _dot(h_ref[...], w_ref[:, c0:c0 + ncol])
            units.append(chunk)
        return units

    def mix_units(first, p_ref, h_ref, r0, wb_ref, dwb_ref, lng_ref, lnb_ref, scw_ref,
                  proj_ref, h1_ref, mixed_ref, a1_ref, q_ref, ext_ref, e_ref, extp_ref, a1s_ref):
        rows = slice(r0, r0 + tm)
        units = []

        def glu():
            halo = ext_ref[tm:tm + HALO, :]
            ext_ref[0:HALO, :] = halo if first is False else jnp.where(first, 0.0, halo)
            ext_ref[HALO:HALO + tm, :] = p_ref[:, 0:dc] * _sigmoid(p_ref[:, dc:2 * dc])
        units.append(glu)
        for g in range(ng):
            def shift(g=g):
                for r in range(8):
                    e_ref[g, r, 0:tm + HALO, :] = ext_ref[r:r + tm + HALO, 128 * g:128 * g + 128]
            units.append(shift)
        for g in range(ng):
            lanes = slice(128 * g, 128 * g + 128)
            for i0 in range(0, tm, rb):
                def conv(g=g, lanes=lanes, i0=i0):
                    acc = jnp.zeros((rb, 128), F32)
                    for k in range(CONF_K):
                        m, r = divmod(k + HALO - CONF_K + 1, 8)
                        acc = acc + e_ref[g, r, i0 + 8 * m:i0 + 8 * m + rb, :] * wb_ref[k:k + 1, lanes]
                    a1s_ref[i0:i0 + rb, lanes] = acc + dwb_ref[:, lanes]
                units.append(conv)

        def norm():
            a1 = a1s_ref[...]
            a1_ref[rows, :] = a1
            mu = jnp.mean(a1, axis=-1, keepdims=True)
            ac = a1 - mu
            rstd = lax.rsqrt(jnp.mean(ac * ac, axis=-1, keepdims=True) + EPS)
            a2 = ac * rstd * lng_ref[...] + lnb_ref[...]
            mixed_ref[rows, 0:dc] = (a2 * _sigmoid(a2)).astype(BF16)
        units.append(norm)

        def short():
            halo = extp_ref[tm:tm + 8, :]
            extp_ref[0:8, :] = halo if first is False else jnp.where(first, 0.0, halo)
            extp_ref[8:8 + tm, :] = p_ref[:, 3 * dc:4 * dc] * p_ref[:, 4 * dc:5 * dc]
            q = jnp.zeros((tm, dc), F32)
            for k in range(SC_K):
                q = q + extp_ref[6 + k:6 + k + tm, :] * scw_ref[k:k + 1, :]
            q_ref[rows, :] = q
            mixed_ref[rows, dc:2 * dc] = (p_ref[:, 2 * dc:3 * dc] * q).astype(BF16)
        units.append(short)

        def keep():
            proj_ref[rows, :] = p_ref[...]
            h1_ref[rows, :] = h_ref[...]
        units.append(keep)
        return units

    def body(x0_ref, xa_ref, xb_ref, mod0_ref, moda_ref, modb_ref, w_ref,
             wb_ref, dwb_ref, lng_ref, lnb_ref, scw_ref, dep_ref,
             proj_ref, h1_ref, mixed_ref, a1_ref, q_ref,
             p0_ref, p1_ref, h0_ref, hh1_ref, ext_ref, e_ref, extp_ref, a1s_ref):
        j = pl.program_id(0)

        @pl.when(j == 0)
        def _():
            ext_ref[...] = jnp.zeros(ext_ref.shape, F32)
            extp_ref[...] = jnp.zeros(extp_ref.shape, F32)
            _run_units(proj_units(x0_ref, mod0_ref, w_ref, p0_ref, h0_ref))

        common = (wb_ref, dwb_ref, lng_ref, lnb_ref, scw_ref, proj_ref, h1_ref, mixed_ref, a1_ref, q_ref,
                  ext_ref, e_ref, extp_ref, a1s_ref)
        _run_units(mix_units((2 * j) % tps == 0, p0_ref, h0_ref, 0, *common),
                   proj_units(xa_ref, moda_ref, w_ref, p1_ref, hh1_ref))
        _run_units(mix_units(False, p1_ref, hh1_ref, tm, *common),
                   proj_units(xb_ref, modb_ref, w_ref, p0_ref, h0_ref))

    last = nt - 1
    xspec = lambda f: pl.BlockSpec((tm, d), lambda j: (f(j), 0))
    mspec = lambda f: pl.BlockSpec((None, 1, 6 * d), lambda j: (f(j) // tps, 0, 0))
    out2 = lambda c: pl.BlockSpec((2 * tm, c), lambda j: (j, 0))
    return pl.pallas_call(
        body, name="in_mix_fwd", grid=(nt // 2,),
        in_specs=[xspec(lambda j: 0), xspec(lambda j: 2 * j + 1), xspec(lambda j: jnp.minimum(2 * j + 2, last)),
                  mspec(lambda j: 0), mspec(lambda j: 2 * j + 1), mspec(lambda j: jnp.minimum(2 * j + 2, last)),
                  _resident((d, d_in)),
                  _const(wb.shape), _const(dwb.shape), _const(lng.shape), _const(lnb.shape), _const(scw.shape), ANY],
        out_specs=[out2(d_in), out2(d), out2(2 * dc), out2(dc), out2(dc)],
        out_shape=[jax.ShapeDtypeStruct((n, d_in), F32), jax.ShapeDtypeStruct((n, d), BF16),
                   jax.ShapeDtypeStruct((n, 2 * dc), BF16), jax.ShapeDtypeStruct((n, dc), F32),
                   jax.ShapeDtypeStruct((n, dc), F32)],
        scratch_shapes=[pltpu.VMEM((tm, d_in), F32), pltpu.VMEM((tm, d_in), F32),
                        pltpu.VMEM((tm, d), BF16), pltpu.VMEM((tm, d), BF16),
                        pltpu.VMEM((tm + HALO + 8, dc), F32), _shift_scratch(tm, dc),
                        pltpu.VMEM((tm + 8, dc), F32), pltpu.VMEM((tm, dc), F32)],
        compiler_params=pltpu.CompilerParams(dimension_semantics=("arbitrary",)),
    )(x, x, x, mod3, mod3, mod3, w_in, wb, dwb, lng, lnb, scw, dep)


def _shift_scratch(tm, dc):
    return pltpu.VMEM((dc // 128, 8, tm + HALO + 8, 128), F32)


def _out_proj(mixed, w_out, x, mod3, t, dep):
    n, d = x.shape
    tm = min(TM_MM, t)
    tps = t // tm

    def body(m_ref, w_ref, x_ref, mod_ref, dep_ref, x1_ref, y1_ref, h2_ref):
        y1 = _dot(m_ref[...], w_ref[...])
        y1_ref[...] = y1.astype(BF16)
        x1 = x_ref[...] + mod_ref[:, 2 * d:3 * d] * y1
        x1_ref[...] = x1
        xn, _ = _rms(x1)
        h2_ref[...] = (xn * (1.0 + mod_ref[:, 4 * d:5 * d]) + mod_ref[:, 3 * d:4 * d]).astype(BF16)

    return pl.pallas_call(
        body, name="out_proj", grid=(n // tm,),
        in_specs=[_rowblk(tm, d), _resident((d, d)), _rowblk(tm, d), _modspec(tps, 6 * d), ANY],
        out_specs=[_rowblk(tm, d), _rowblk(tm, d), _rowblk(tm, d)],
        out_shape=[jax.ShapeDtypeStruct((n, d), F32), jax.ShapeDtypeStruct((n, d), BF16),
                   jax.ShapeDtypeStruct((n, d), BF16)],
        compiler_params=pltpu.CompilerParams(dimension_semantics=("parallel",)),
    )(mixed, w_out, x, mod3, dep)


def _mlp_fwd(h2, x1, tgt, mod3, gfin, w1, w2, t):
    n, d = x1.shape
    dff = w1.shape[1]
    tm = min(TM_MLP, t)
    tps = t // tm
    nt = n // tm

    def body(h_ref, x1_ref, tg_ref, mod_ref, gf_ref, w1_ref, w2_ref,
             z_ref, dx2_ref, dy2_ref, dg2_ref, ggf_ref, loss_ref):
        i = pl.program_id(0)
        g2 = mod_ref[:, 5 * d:6 * d]
        gf = gf_ref[...]
        sub = tm // MLP_SUB
        sums = None
        for part in range(MLP_SUB):
            rs = slice(part * sub, (part + 1) * sub)
            hv = h_ref[rs, :]
            y2 = jnp.zeros((sub, d), F32)
            for j in range(dff // FF_CHUNK):
                cols = slice(j * FF_CHUNK, (j + 1) * FF_CHUNK)
                z = _dot(hv, w1_ref[:, cols])
                z_ref[rs, cols] = z.astype(BF16)
                zr = jnp.maximum(z, 0.0)
                y2 = y2 + _dot((zr * zr).astype(BF16), w2_ref[cols, :])
            x2n, r3 = _rms(x1_ref[rs, :] + g2 * y2)
            diff = x2n * gf - tg_ref[rs, :]
            dout = diff * (1.0 / d)
            dx2 = _rms_bwd(dout * gf, x2n, r3)
            dx2_ref[rs, :] = dx2
            dy2_ref[rs, :] = (g2 * dx2).astype(BF16)
            p = (_rows8(dx2 * y2), _rows8(dout * x2n), _rows8(diff * diff))
            sums = p if sums is None else tuple(a + b for a, b in zip(sums, p))
        _acc_add(dg2_ref, i % tps == 0, sums[0])
        _acc_add(ggf_ref, i == 0, sums[1])
        _acc_add(loss_ref, i == 0, sums[2])

    return pl.pallas_call(
        body, name="mlp_fwd", grid=(nt,),
        in_specs=[_rowblk(tm, d), _rowblk(tm, d), _rowblk(tm, d), _modspec(tps, 6 * d), _const((1, d)),
                  _resident((d, dff)), _resident((dff, d))],
        out_specs=[_rowblk(tm, dff), _rowblk(tm, d), _rowblk(tm, d), _accspec(tps, d),
                   _const((8, d)), _const((8, d))],
        out_shape=[jax.ShapeDtypeStruct((n, dff), BF16), jax.ShapeDtypeStruct((n, d), F32),
                   jax.ShapeDtypeStruct((n, d), BF16), jax.ShapeDtypeStruct((n // t, 8, d), F32),
                   jax.ShapeDtypeStruct((8, d), F32), jax.ShapeDtypeStruct((8, d), F32)],
        compiler_params=pltpu.CompilerParams(dimension_semantics=("arbitrary",)),
    )(h2, x1, tgt, mod3, gfin, w1, w2)


def _mlp_bwd(dy2, z, x1, dx2, y1, mod3, w1, w2, t):
    n, d = x1.shape
    dff = w1.shape[1]
    tm = min(TM_MLP, t)
    tps = t // tm

    def body(dy2_ref, z_ref, x1_ref, dx2_ref, y1_ref, mod_ref, w1_ref, w2_ref,
             dz_ref, dx1_ref, dy1_ref, dsh_ref, dsc_ref, dg1_ref):
        first = pl.program_id(0) % tps == 0
        sub = tm // MLP_SUB
        sums = None
        for part in range(MLP_SUB):
            rs = slice(part * sub, (part + 1) * sub)
            dy2 = dy2_ref[rs, :]
            dh2 = jnp.zeros((sub, d), F32)
            for j in range(dff // FF_CHUNK):
                cols = slice(j * FF_CHUNK, (j + 1) * FF_CHUNK)
                du = _dot_nt(dy2, w2_ref[cols, :])
                dz = (du * (2.0 * jnp.maximum(z_ref[rs, cols].astype(F32), 0.0))).astype(BF16)
                dz_ref[rs, cols] = dz
                dh2 = dh2 + _dot_nt(dz, w1_ref[:, cols])
            x1n, r2 = _rms(x1_ref[rs, :])
            dx1 = dx2_ref[rs, :] + _rms_bwd(dh2 * (1.0 + mod_ref[:, 4 * d:5 * d]), x1n, r2)
            dx1_ref[rs, :] = dx1
            dy1_ref[rs, :] = (mod_ref[:, 2 * d:3 * d] * dx1).astype(BF16)
            p = (_rows8(dh2), _rows8(dh2 * x1n), _rows8(dx1 * y1_ref[rs, :].astype(F32)))
            sums = p if sums is None else tuple(a + b for a, b in zip(sums, p))
        _acc_add(dsh_ref, first, sums[0])
        _acc_add(dsc_ref, first, sums[1])
        _acc_add(dg1_ref, first, sums[2])

    acc = jax.ShapeDtypeStruct((n // t, 8, d), F32)
    return pl.pallas_call(
        body, name="mlp_bwd", grid=(n // tm,),
        in_specs=[_rowblk(tm, d), _rowblk(tm, dff), _rowblk(tm, d), _rowblk(tm, d), _rowblk(tm, d),
                  _modspec(tps, 6 * d), _resident((d, dff)), _resident((dff, d))],
        out_specs=[_rowblk(tm, dff), _rowblk(tm, d), _rowblk(tm, d),
                   _accspec(tps, d), _accspec(tps, d), _accspec(tps, d)],
        out_shape=[jax.ShapeDtypeStruct((n, dff), BF16), jax.ShapeDtypeStruct((n, d), F32),
                   jax.ShapeDtypeStruct((n, d), BF16), acc, acc, acc],
        compiler_params=pltpu.CompilerParams(dimension_semantics=("arbitrary",)),
    )(dy2, z, x1, dx2, y1, mod3, w1, w2)


def _wgrad(a, b, name, relu2=False, bn=None, deps=()):
    n, ka = a.shape
    nb = b.shape[1]
    tk = min(TK_WG if ka * nb >= TK_WG * 1024 else TK_WG // 2, n)
    bm = min(ka, 1024)
    if bn is None:
        bn = nb if nb <= 1024 else (1280 if nb % 1280 == 0 else 1024)

    def body(a_ref, b_ref, *rest):
        out_ref = rest[-1]
        av = a_ref[...]
        if relu2:
            ar = jnp.maximum(av, 0.0)
            av = ar * ar
        p = _dot_tn(av, b_ref[...])
        _acc_add(out_ref, pl.program_id(2) == 0, p)

    return pl.pallas_call(
        body, name=name, grid=(ka // bm, nb // bn, n // tk),
        in_specs=[pl.BlockSpec((tk, bm), lambda i, j, k: (k, i)),
                  pl.BlockSpec((tk, bn), lambda i, j, k: (k, j))] + [ANY] * len(deps),
        out_specs=pl.BlockSpec((bm, bn), lambda i, j, k: (i, j)),
        out_shape=jax.ShapeDtypeStruct((ka, nb), F32),
        compiler_params=pltpu.CompilerParams(dimension_semantics=("parallel", "parallel", "arbitrary")),
    )(a, b, *deps)


SG_DWW = 0
SG_DWB = CONF_K
SG_LNG = CONF_K + 1
SG_LNB = CONF_K + 2
SG_SCW = CONF_K + 3
SG_N = CONF_K + 3 + SC_K


def _mix_in_bwd(dy1, w_out, a1, q, proj, wb, lng, lnb, scw, w_in, x, dx1, mod3, t, dep):
    n, d_in = proj.shape
    d = x.shape[1]
    dc = d_in // 5
    tm = min(TM_MIX, t)
    tps = t // tm
    nt = n // tm
    rw = min(RB_WG, tm)
    ng = dc // 128
    ncol = 256

    def ln_bwd(a1v, da3, lng_v, lnb_v):
        mu = jnp.mean(a1v, axis=-1, keepdims=True)
        ac = a1v - mu
        rstd = lax.rsqrt(jnp.mean(ac * ac, axis=-1, keepdims=True) + EPS)
        ah = ac * rstd
        a2 = ah * lng_v + lnb_v
        s2 = _sigmoid(a2)
        da2 = da3 * (s2 * (1.0 + a2 * (1.0 - s2)))
        dah = da2 * lng_v
        da1 = rstd * (dah - jnp.mean(dah, axis=-1, keepdims=True)
                      - ah * jnp.mean(dah * ah, axis=-1, keepdims=True))
        return da1, da2, ah

    def dmixed_units(dy_ref, wo_ref, dm_ref):
        units = []
        for c0 in range(0, 2 * dc, ncol):
            def chunk(c0=c0):
                dm_ref[:, c0:c0 + ncol] = _dot_nt(dy_ref[...], wo_ref[c0:c0 + ncol, :])
            units.append(chunk)
        return units

    def mix_units(k, dm_ref, a1_ref, q_ref, p_ref, wb_ref, lng_ref, lnb_ref, scw_ref,
                  dproj_ref, sg_ref, extd_ref, ed_ref, a0_ref, da0_ref, extq_ref, cda_ref, cdq_ref):
        keep_next = jnp.where(k % tps == tps - 1, 0.0, 1.0)
        units = []

        def head():
            lng_v, lnb_v = lng_ref[...], lnb_ref[...]
            da1, da2, ah = ln_bwd(a1_ref[...], dm_ref[:, 0:dc], lng_v, lnb_v)
            sg_ref[8 * SG_LNG:8 * SG_LNG + 8, :] += _rows8(da2 * ah)
            sg_ref[8 * SG_LNB:8 * SG_LNB + 8, :] += _rows8(da2)
            sg_ref[8 * SG_DWB:8 * SG_DWB + 8, :] += _rows8(da1)
            a0_ref[...] = p_ref[:, 0:dc] * _sigmoid(p_ref[:, dc:2 * dc])
            extd_ref[0:tm, :] = da1
            extd_ref[tm:tm + HALO, :] = cda_ref[...] * keep_next
            extd_ref[tm + HALO:tm + HALO + 8, :] = jnp.zeros((8, dc), F32)
            cda_ref[...] = da1[0:HALO, :]
            ds = dm_ref[:, dc:2 * dc]
            dproj_ref[:, 2 * dc:3 * dc] = (ds * q_ref[...]).astype(BF16)
            dq = ds * p_ref[:, 2 * dc:3 * dc]
            extq_ref[0:tm, :] = dq
            extq_ref[tm:tm + 8, :] = cdq_ref[...] * keep_next
            cdq_ref[...] = dq[0:8, :]
        units.append(head)
        for g in range(ng):
            def shift(g=g):
                for r in range(8):
                    ed_ref[g, r, 0:tm + HALO, :] = extd_ref[r:r + tm + HALO, 128 * g:128 * g + 128]
            units.append(shift)
        for g in range(ng):
            lanes = slice(128 * g, 128 * g + 128)
            accs = [None] * CONF_K
            for i0 in range(0, tm, rw):
                def conv(g=g, lanes=lanes, i0=i0, accs=accs):
                    a0v = a0_ref[i0:i0 + rw, lanes]
                    acc = jnp.zeros((rw, 128), F32)
                    for s in range(CONF_K):
                        m, r = divmod(s, 8)
                        e = ed_ref[g, r, i0 + 8 * m:i0 + 8 * m + rw, :]
                        acc = acc + e * wb_ref[CONF_K - 1 - s:CONF_K - s, lanes]
                        part = _rows8(e * a0v)
                        accs[s] = part if accs[s] is None else accs[s] + part
                    da0_ref[i0:i0 + rw, lanes] = acc
                units.append(conv)

            def flush(lanes=lanes, accs=accs):
                for s in range(CONF_K):
                    kk = CONF_K - 1 - s
                    sg_ref[8 * (SG_DWW + kk):8 * (SG_DWW + kk) + 8, lanes] += accs[s]
            units.append(flush)

        def emit():
            da0 = da0_ref[...]
            sig = _sigmoid(p_ref[:, dc:2 * dc])
            dproj_ref[:, 0:dc] = (da0 * sig).astype(BF16)
            dproj_ref[:, dc:2 * dc] = (da0 * a0_ref[...] * (1.0 - sig)).astype(BF16)
        units.append(emit)

        def short():
            scc, sch = p_ref[:, 3 * dc:4 * dc], p_ref[:, 4 * dc:5 * dc]
            pv = scc * sch
            dp = jnp.zeros((tm, dc), F32)
            for kk in range(SC_K):
                dqs = extq_ref[SC_K - 1 - kk:SC_K - 1 - kk + tm, :]
                dp = dp + dqs * scw_ref[kk:kk + 1, :]
                sg_ref[8 * (SG_SCW + kk):8 * (SG_SCW + kk) + 8, :] += _rows8(pv * dqs)
            dproj_ref[:, 3 * dc:4 * dc] = (dp * sch).astype(BF16)
            dproj_ref[:, 4 * dc:5 * dc] = (dp * scc).astype(BF16)
        units.append(short)
        return units

    def proj_units(first, dp_ref, w_ref, x_ref, dx1_ref, mod_ref, dh_ref, gx_ref, dsh_ref, dsc_ref):
        units = []
        for c0 in range(0, d, ncol):
            def chunk(c0=c0):
                dh_ref[:, c0:c0 + ncol] = _dot_nt(dp_ref[...], w_ref[c0:c0 + ncol, :])
            units.append(chunk)

        def tail():
            dh1 = dh_ref[...]
            xn, r1 = _rms(x_ref[...])
            v1, v2 = _rows8(dh1), _rows8(dh1 * xn)
            dsh_ref[...] = jnp.where(first, v1, dsh_ref[...] + v1)
            dsc_ref[...] = jnp.where(first, v2, dsc_ref[...] + v2)
            gx_ref[...] = dx1_ref[...] + _rms_bwd(dh1 * (1.0 + mod_ref[:, d:2 * d]), xn, r1)
        units.append(tail)
        return units

    def body(dyl_ref, dy_ref, wo_ref, a1_ref, q_ref, p_ref, wb_ref, lng_ref, lnb_ref, scw_ref,
             w_ref, x_ref, dx1_ref, mod_ref, dep_ref,
             dproj_ref, sg_ref, gx_ref, dsh_ref, dsc_ref,
             extd_ref, ed_ref, a0_ref, da0_ref, extq_ref, dp_ref, dh_ref, dm_ref, cda_ref, cdq_ref):
        k = pl.program_id(0)

        @pl.when(k == 0)
        def _():
            sg_ref[...] = jnp.zeros(sg_ref.shape, F32)
            dp_ref[...] = jnp.zeros(dp_ref.shape, BF16)
            cda_ref[...] = jnp.zeros(cda_ref.shape, F32)
            cdq_ref[...] = jnp.zeros(cdq_ref.shape, F32)
            _run_units(dmixed_units(dyl_ref, wo_ref, dm_ref))

        first = jnp.logical_or(k <= 1, (nt - k) % tps == tps - 1)
        after = lambda: proj_units(first, dp_ref, w_ref, x_ref, dx1_ref, mod_ref, dh_ref, gx_ref, dsh_ref, dsc_ref)

        @pl.when(k < nt)
        def _():
            _run_units(mix_units(nt - 1 - k, dm_ref, a1_ref, q_ref, p_ref, wb_ref, lng_ref, lnb_ref, scw_ref,
                                 dproj_ref, sg_ref, extd_ref, ed_ref, a0_ref, da0_ref, extq_ref, cda_ref, cdq_ref),
                       after(), dmixed_units(dy_ref, wo_ref, dm_ref))
            dp_ref[...] = dproj_ref[...]

        @pl.when(k == nt)
        def _():
            _run_units(after())

    cur = lambda k: jnp.maximum(nt - 1 - k, 0)
    prev = lambda k: jnp.minimum(nt - k, nt - 1)
    acc = jax.ShapeDtypeStruct((n // t, 8, d), F32)
    return pl.pallas_call(
        body, name="mix_in_bwd", grid=(nt + 1,),
        in_specs=[pl.BlockSpec((tm, d), lambda k: (nt - 1, 0)),
                  pl.BlockSpec((tm, d), lambda k: (jnp.maximum(nt - 2 - k, 0), 0)), _resident(w_out.shape),
                  pl.BlockSpec((tm, dc), lambda k: (cur(k), 0)),
                  pl.BlockSpec((tm, dc), lambda k: (cur(k), 0)),
                  pl.BlockSpec((tm, d_in), lambda k: (cur(k), 0)),
                  _const(wb.shape), _const(lng.shape), _const(lnb.shape), _const(scw.shape),
                  _resident((d, d_in)),
                  pl.BlockSpec((tm, d), lambda k: (prev(k), 0)), pl.BlockSpec((tm, d), lambda k: (prev(k), 0)),
                  pl.BlockSpec((None, 1, 6 * d), lambda k: (prev(k) // tps, 0, 0)), ANY],
        out_specs=[pl.BlockSpec((tm, d_in), lambda k: (cur(k), 0)), _const((8 * SG_N, dc)),
                   pl.BlockSpec((tm, d), lambda k: (prev(k), 0)),
                   pl.BlockSpec((None, 8, d), lambda k: (prev(k) // tps, 0, 0)),
                   pl.BlockSpec((None, 8, d), lambda k: (prev(k) // tps, 0, 0))],
        out_shape=[jax.ShapeDtypeStruct((n, d_in), BF16), jax.ShapeDtypeStruct((8 * SG_N, dc), F32),
                   jax.ShapeDtypeStruct((n, d), F32), acc, acc],
        scratch_shapes=[pltpu.VMEM((tm + HALO + 8, dc), F32), _shift_scratch(tm, dc),
                        pltpu.VMEM((tm, dc), F32), pltpu.VMEM((tm, dc), F32),
                        pltpu.VMEM((tm + 8, dc), F32),
                        pltpu.VMEM((tm, d_in), BF16), pltpu.VMEM((tm, d), F32), pltpu.VMEM((tm, 2 * dc), F32),
                        pltpu.VMEM((HALO, dc), F32), pltpu.VMEM((8, dc), F32)],
        compiler_params=pltpu.CompilerParams(dimension_semantics=("arbitrary",)),
    )(dy1, dy1, w_out, a1, q, proj, wb, lng, lnb, scw, w_in, x, dx1, mod3, dep)


SMALL_ROWS = 40


def _pack_small(ids, sg, ggf, loss, accs, d, dep):
    dc = d // 2
    nb = accs[0].shape[0]

    def body(ids_ref, sg_ref, ggf_ref, loss_ref, dsh1, dsc1, dg1, dsh2, dsc2, dg2, dep_ref, pack_ref, dmod_ref):
        pack_ref[...] = jnp.zeros(pack_ref.shape, F32)
        for k in range(SG_N):
            pack_ref[k:k + 1, :] = jnp.sum(sg_ref[8 * k:8 * k + 8, :], axis=0, keepdims=True)
        gf = jnp.sum(ggf_ref[...], axis=0, keepdims=True)
        pack_ref[SG_N:SG_N + 1, :] = gf[:, 0:dc]
        pack_ref[SG_N + 1:SG_N + 2, :] = gf[:, dc:d]
        tot = jnp.sum(jnp.sum(loss_ref[...], axis=0, keepdims=True), axis=1, keepdims=True) * (0.5 / d)
        pack_ref[SG_N + 2:SG_N + 3, :] = jnp.broadcast_to(tot, (1, dc))
        for b in range(nb):
            row = jnp.concatenate([jnp.sum(ref[b], axis=0, keepdims=True)
                                   for ref in (dsh1, dsc1, dg1, dsh2, dsc2, dg2)], axis=1)
            for f in range(fold):
                dmod_ref[b * fold + f:b * fold + f + 1, :] = row[:, f * wf:(f + 1) * wf]

    fold = 8 // nb
    wf = 6 * d // fold
    assert nb * fold == 8 and wf % 128 == 0
    whole = lambda a: pl.BlockSpec(a.shape, lambda i, ids: (0,) * a.ndim)
    mine = lambda r, c: pl.BlockSpec((None, r, c), lambda i, ids: (ids[2], 0, 0))
    return pl.pallas_call(
        body, name="pack_small",
        grid_spec=pltpu.PrefetchScalarGridSpec(
            num_scalar_prefetch=1, grid=(1,),
            in_specs=[whole(a) for a in (sg, ggf, loss, *accs)] + [ANY],
            out_specs=[mine(SMALL_ROWS, dc), mine(8, wf)]),
        out_shape=[jax.ShapeDtypeStruct((N_DEV, SMALL_ROWS, dc), F32), jax.ShapeDtypeStruct((N_DEV, 8, wf), F32)],
    )(ids, sg, ggf, loss, *accs, dep)


def _small_reduce(pack_all, dmod_all, nb, dep):
    def body(pk_ref, dm_ref, dep_ref, red_ref, dmod_ref, gb_ref):
        tot = pk_ref[0]
        for dev in range(1, N_DEV):
            tot = tot + pk_ref[dev]
        red_ref[...] = tot
        for f in range(fold):
            gb = jnp.zeros((1, wf), F32)
            for dev in range(N_DEV):
                for b in range(nb):
                    seg = dm_ref[dev, b * fold + f:b * fold + f + 1, :]
                    dmod_ref[dev * nb + b:dev * nb + b + 1, f * wf:(f + 1) * wf] = seg
                    gb = gb + seg
            gb_ref[:, f * wf:(f + 1) * wf] = gb

    fold = 8 // nb
    wf = dmod_all.shape[2]
    return pl.pallas_call(
        body, name="small_reduce",
        out_shape=[jax.ShapeDtypeStruct(pack_all.shape[1:], F32),
                   jax.ShapeDtypeStruct((N_DEV * nb, fold * wf), F32),
                   jax.ShapeDtypeStruct((1, fold * wf), F32)],
        in_specs=[VMEM] * 2 + [ANY], out_specs=[VMEM] * 3,
    )(pack_all, dmod_all, dep)


def _adam(w, g, m, v):
    m = ADAM_B1 * m + (1.0 - ADAM_B1) * g
    v = ADAM_B2 * v + (1.0 - ADAM_B2) * (g * g)
    m_hat = m / (1.0 - ADAM_B1 ** ADAM_STEP)
    v_hat = v / (1.0 - ADAM_B2 ** ADAM_STEP)
    delta = -ADAM_LR * (m_hat / (jnp.sqrt(v_hat) + ADAM_EPS) + ADAM_WD * w)
    return delta, m, v


def _adamw_big(w, g, m, v, name, deps=()):
    r, c = w.shape
    tr = min(r, 256)

    def body(w_ref, g_ref, m_ref, v_ref, *rest):
        go_ref, d_ref, nm_ref, nv_ref = rest[len(deps):]
        g = g_ref[...]
        go_ref[...] = g
        d_ref[...], nm_ref[...], nv_ref[...] = _adam(w_ref[...], g, m_ref[...], v_ref[...])

    s = jax.ShapeDtypeStruct((r, c), F32)
    return pl.pallas_call(
        body, name=name, grid=(r // tr,),
        in_specs=[_rowblk(tr, c)] * 4 + [ANY] * len(deps), out_specs=[_rowblk(tr, c)] * 4, out_shape=[s, s, s, s],
        compiler_params=pltpu.CompilerParams(dimension_semantics=("parallel",)),
    )(w, g, m, v, *deps)


def _adamw_ada(act_t, dmod_cols, w, m, v):
    r, c = w.shape
    tr = min(r, 256)
    nb = act_t.shape[1]

    def body(a_ref, dm_ref, w_ref, m_ref, v_ref, g_ref, d_ref, nm_ref, nv_ref):
        g = jnp.dot(a_ref[...], dm_ref[...], preferred_element_type=F32, precision=HIGHEST)
        g_ref[...] = g
        d_ref[...], nm_ref[...], nv_ref[...] = _adam(w_ref[...], g, m_ref[...], v_ref[...])

    s = jax.ShapeDtypeStruct((r, c), F32)
    return pl.pallas_call(
        body, name="adamw_w_ada", grid=(r // tr,),
        in_specs=[_rowblk(tr, nb), _const((nb, c))] + [_rowblk(tr, c)] * 3,
        out_specs=[_rowblk(tr, c)] * 4, out_shape=[s, s, s, s],
        compiler_params=pltpu.CompilerParams(dimension_semantics=("parallel",)),
    )(act_t, dmod_cols, w, m, v)


def _adamw_small(ids, red, g_bada, ws, ms, vs):
    n = len(ws)
    cw = ws[1].shape[1]

    def body(ids_ref, red_ref, gb_ref, *refs):
        cols = pl.ds(pl.multiple_of(ids_ref[0] * cw, 128), cw)
        gs = [gb_ref[...], red_ref[SG_DWW:SG_DWW + CONF_K, cols], red_ref[SG_DWB:SG_DWB + 1, :],
              red_ref[SG_LNG:SG_LNG + 1, :], red_ref[SG_LNB:SG_LNB + 1, :], red_ref[SG_SCW:SG_SCW + SC_K, cols],
              jnp.concatenate([red_ref[SG_N:SG_N + 1, :], red_ref[SG_N + 1:SG_N + 2, :]], axis=1)]
        for i in range(n):
            w, m, v = (refs[j * n + i][...] for j in range(3))
            dl, nm, nv = _adam(w, gs[i], m, v)
            refs[3 * n + i][...] = gs[i]
            refs[4 * n + i][...] = dl
            refs[5 * n + i][...] = nm
            refs[6 * n + i][...] = nv

    whole = lambda a: pl.BlockSpec(a.shape, lambda i, ids: (0,) * a.ndim)
    shapes = [jax.ShapeDtypeStruct(w.shape, F32) for w in ws]
    return pl.pallas_call(
        body, name="adamw_small",
        grid_spec=pltpu.PrefetchScalarGridSpec(
            num_scalar_prefetch=1, grid=(1,),
            in_specs=[whole(a) for a in (red, g_bada, *ws, *ms, *vs)],
            out_specs=[whole(a) for a in ws] * 4),
        out_shape=shapes * 4,
    )(ids, red, g_bada, *ws, *ms, *vs)


def kernel(x, c, w_ada, b_ada, w_in, conf_dw_w, conf_dw_b, conf_ln_g, conf_ln_b, sc_conv_w, w_out, w_mlp1, w_mlp2, g_final, loss_target, m_w_ada, m_b_ada, m_w_in, m_conf_dw_w, m_conf_dw_b, m_conf_ln_g, m_conf_ln_b, m_sc_conv_w, m_w_out, m_w_mlp1, m_w_mlp2, m_g_final, v_w_ada, v_b_ada, v_w_in, v_conf_dw_w, v_conf_dw_b, v_conf_ln_g, v_conf_ln_b, v_sc_conv_w, v_w_out, v_w_mlp1, v_w_mlp2, v_g_final):
    nb, t, d = x.shape
    n = nb * t
    dc = d // 2
    ada_w = w_ada.shape[2]
    ax, ay, ac = _me()
    chip = 2 * ax + ay
    dev = 2 * chip + ac
    ids = jnp.stack([chip, ac, dev]).astype(jnp.int32)

    lays = _wlayout(d)
    names = ("in", "out", "mlp1", "mlp2")
    fulls = _cast_place(ids, [w_in[0], w_out[0], w_mlp1[0], w_mlp2[0]], lays)

    c_pad = jnp.zeros((8, d), F32).at[0:nb].set(c)
    cw_pad = jnp.zeros((SMALL_ROWS, dc // N_CHIPS), F32)
    cw_pad = cw_pad.at[0:CONF_K].set(conf_dw_w[0]).at[HALO:HALO + SC_K].set(sc_conv_w[0])
    c_all8, cw_all8 = _all_gather8([c_pad, cw_pad], "gather_c")
    plan_i = _gather_chip_plan(lays[0:1])
    sems_i, bufs_i, tok_i = _copy_start("gather_in_start", [fulls[0]], 3, plan_i, [c_all8])
    c_all = c_all8[:, 0:nb].reshape(N_DEV * nb, d) + tok_i[0, 0]
    cw_full = jnp.concatenate([cw_all8[2 * k] for k in range(N_CHIPS)], axis=1)
    dww, scw = cw_full[0:CONF_K], cw_full[HALO:HALO + SC_K]
    b_cols = lax.dynamic_slice(b_ada, (0, chip * ada_w), (1, ada_w))
    c_act, mod_shard = _ada_mod(c_all, w_ada[0], b_cols)
    mod_all = _gather_mod(mod_shard)
    mod3 = lax.dynamic_slice(mod_all, (dev * nb, 0), (nb, 6 * d)).reshape(nb, 1, 6 * d)

    x2 = x.reshape(n, d)
    tgt = loss_target.reshape(n, d)
    (wf_in,) = _copy_wait("gather_in_wait", bufs_i, sems_i, plan_i, [mod3])
    (wf_in,) = _copy_blocking("gather_in_pass", [wf_in], 3, _gather_pass_plan(lays[0:1]))
    plan_o, plan_b, plan_p = _gather_direct_plan(lays[1:2]), _gather_chip_plan(lays[2:4]), _gather_pass_plan(lays[2:4])
    sems_o, bufs_o, tok_o = _copy_start("gather_out_start", [fulls[1]], 6, plan_o, [wf_in, mod3])
    sems_b, bufs_b, tok_b = _copy_start("gather_mlp_start", fulls[2:4], 6, plan_b, [tok_o])
    proj, h1, mixed, a1, q = _in_mix_fwd(x2, mod3, wf_in, dww, conf_dw_b, conf_ln_g, conf_ln_b, scw, t, tok_b)
    (wf_out,) = _copy_wait("gather_out_wait", bufs_o, sems_o, plan_o, [mixed])
    bufs_b = _copy_wait("gather_mlp_wait", bufs_b, sems_b, plan_b, [mixed])
    sems_p, bufs_p, tok_p = _copy_start("gather_pass_start", bufs_b, 6, plan_p)
    x1, y1, h2 = _out_proj(mixed, wf_out, x2, mod3, t, tok_p)
    wf_1, wf_2 = _copy_wait("gather_pass_wait", bufs_p, sems_p, plan_p, [h2])
    z, dx2, dy2, dg2, ggf, loss_p = _mlp_fwd(h2, x1, tgt, mod3, g_final.reshape(1, d), wf_1, wf_2, t)

    dz, dx1, dy1, dsh2, dsc2, dg1 = _mlp_bwd(dy2, z, x1, dx2, y1, mod3, wf_1, wf_2, t)
    g_w2 = _wgrad(z, dy2, "wgrad_mlp2", relu2=True)
    g_w1 = _wgrad(h2, dz, "wgrad_mlp1")
    made = {}

    def behind_pair_exchange(tok):
        made["g_wout"] = _wgrad(mixed, dy1, "wgrad_out", deps=[tok])
        return [made["g_wout"]]

    def behind_chip_exchange(tok):
        made["dproj"], made["sg"], made["grad_x"], made["dsh1"], made["dsc1"] = _mix_in_bwd(
            dy1, wf_out, a1, q, proj, dww, conf_ln_g, conf_ln_b, scw, wf_in, x2, dx1, mod3, t, tok)
        return [made["dproj"]]

    def behind_pair_share(tok):
        made["g_win"] = _wgrad(h1, made["dproj"], "wgrad_in", deps=[tok])
        return [made["g_win"]]

    gr_1, gr_2 = _reduce_scatter(ids, [g_w1, g_w2], lays[2:4], names[2:4], "m",
                                 (behind_pair_exchange, behind_chip_exchange, behind_pair_share))

    big = {}

    plan_g = _gather8_plan(2)

    def behind_pair_exchange_in(tok):
        packs = _pack_small(ids, made["sg"], ggf, loss_p, (made["dsh1"], made["dsc1"], dg1, dsh2, dsc2, dg2), d, tok)
        made["gather"] = _copy_start("gather_small_start", list(packs), 2 * (N_DEV - 1), plan_g)
        return [made["gather"][2]]

    def behind_chip_exchange_in(tok):
        sems_g, bufs_g, _ = made["gather"]
        gathered = _copy_wait("gather_small_wait", bufs_g, sems_g, plan_g, [tok])
        red, dmod_all, g_bada = _small_reduce(*gathered, nb, tok)
        dmod_cols = lax.dynamic_slice(dmod_all, (0, chip * ada_w), (N_DEV * nb, ada_w))
        made["ada"] = _adamw_ada(c_act.T, dmod_cols, w_ada[0], m_w_ada[0], v_w_ada[0])
        big["w_mlp1"] = _adamw_big(w_mlp1[0], gr_1, m_w_mlp1[0], v_w_mlp1[0], "adamw_w_mlp1")
        small_w = [b_ada, conf_dw_w[0], conf_dw_b, conf_ln_g, conf_ln_b, sc_conv_w[0], g_final.reshape(1, d)]
        small_m = [m_b_ada, m_conf_dw_w[0], m_conf_dw_b, m_conf_ln_g, m_conf_ln_b, m_sc_conv_w[0],
                   m_g_final.reshape(1, d)]
        small_v = [v_b_ada, v_conf_dw_w[0], v_conf_dw_b, v_conf_ln_g, v_conf_ln_b, v_sc_conv_w[0],
                   v_g_final.reshape(1, d)]
        made["upd"] = _adamw_small(ids, red, g_bada, small_w, small_m, small_v)
        made["loss"] = red[SG_N + 2, 0]
        return [big["w_mlp1"][0], made["upd"][0]]

    def behind_pair_share_in(tok):
        big["w_mlp2"] = _adamw_big(w_mlp2[0], gr_2, m_w_mlp2[0], v_w_mlp2[0], "adamw_w_mlp2", deps=[tok])
        return [big["w_mlp2"][0]]

    gr_in, gr_out = _reduce_scatter(ids, [made["g_win"], made["g_wout"]], lays[0:2], names[0:2], "i",
                                    (behind_pair_exchange_in, behind_chip_exchange_in, behind_pair_share_in))
    big["w_in"] = _adamw_big(w_in[0], gr_in, m_w_in[0], v_w_in[0], "adamw_w_in")
    big["w_out"] = _adamw_big(w_out[0], gr_out, m_w_out[0], v_w_out[0], "adamw_w_out")
    grad_x, upd, loss = made["grad_x"], made["upd"], made["loss"]
    g_wada, d_wada, nm_wada, nv_wada = made["ada"]
    ns = len(upd) // 4
    small_g, s_delta, s_m, s_v = upd[0:ns], upd[ns:2 * ns], upd[2 * ns:3 * ns], upd[3 * ns:4 * ns]

    def outs(kind_big, kind_small, wada):
        sm = kind_small
        return (wada[None], sm[0], kind_big["w_in"][None], sm[1][None], sm[2], sm[3], sm[4], sm[5][None],
                kind_big["w_out"][None], kind_big["w_mlp1"][None], kind_big["w_mlp2"][None], sm[6].reshape(d))

    grads_out = outs({k: v[0] for k, v in big.items()}, small_g, g_wada)
    delta_out = outs({k: v[1] for k, v in big.items()}, s_delta, d_wada)
    m_out = outs({k: v[2] for k, v in big.items()}, s_m, nm_wada)
    v_out = outs({k: v[3] for k, v in big.items()}, s_v, nv_wada)
    return (loss, grad_x.reshape(nb, t, d), *grads_out, *delta_out, *m_out, *v_out)
```

```python
import jax
import jax.numpy as jnp
from jax import lax
from jax.experimental import pallas as pl
from jax.experimental.pallas import tpu as pltpu

F32 = jnp.float32
BF16 = jnp.bfloat16
MESH = pl.DeviceIdType.MESH
HIGHEST = lax.Precision.HIGHEST

EPS = 1e-6
CONF_K = 31
SC_K = 3
HALO = 32
N_CHIPS = 4
N_DEV = 8

ADAM_LR = 0.001
ADAM_B1 = 0.9
ADAM_B2 = 0.999
ADAM_EPS = 1e-08
ADAM_WD = 0.01
ADAM_STEP = 10

TM_MM = 1024
TM_MIX = 256
TM_MLP = 512
FF_CHUNK = 1024
MLP_SUB = 2
TK_WG = 4096
RB_CONV = 32
RB_WG = 32
CHIP_RELS = ((1, 0), (0, 1), (1, 1))

ANY = pl.BlockSpec(memory_space=pl.ANY)
VMEM = pl.BlockSpec(memory_space=pltpu.VMEM)
HBM = pl.BlockSpec(memory_space=pltpu.HBM)
SEM = pl.BlockSpec(memory_space=pltpu.SEMAPHORE)
EFFECT = pltpu.SideEffectType.DATAFLOW_SIDE_EFFECTING


def _me():
    return lax.axis_index("x"), lax.axis_index("y"), lax.axis_index("c")


def _flip(v, f):
    return 1 - v if f else v


def _rows8(v):
    r, c = v.shape
    return v.reshape(r // 8, 8, c).sum(axis=0)


def _rms(x):
    r = lax.rsqrt(jnp.mean(x * x, axis=-1, keepdims=True) + EPS)
    return x * r, r


def _rms_bwd(dxn, xn, r):
    return r * (dxn - xn * jnp.mean(dxn * xn, axis=-1, keepdims=True))


def _sigmoid(x):
    return 1.0 / (1.0 + jnp.exp(-x))


def _dot(a, b):
    return jnp.dot(a, b, preferred_element_type=F32)


def _dot_nt(a, b):
    return lax.dot_general(a, b, (((1,), (1,)), ((), ())), preferred_element_type=F32)


def _dot_tn(a, b):
    return lax.dot_general(a, b, (((0,), (0,)), ((), ())), preferred_element_type=F32)


def _const(shape):
    nd = len(shape)
    return pl.BlockSpec(shape, lambda i: (0,) * nd)


def _resident(shape):
    nd = len(shape)
    return pl.BlockSpec(shape, lambda i: (0,) * nd, pipeline_mode=pl.Buffered(1))


def _rowblk(tm, c):
    return pl.BlockSpec((tm, c), lambda i: (i, 0))


def _modspec(tps, width):
    return pl.BlockSpec((None, 1, width), lambda i: (i // tps, 0, 0))


def _accspec(tps, c):
    return pl.BlockSpec((None, 8, c), lambda i: (i // tps, 0, 0))


def _acc_add(ref, first, v):
    @pl.when(first)
    def _():
        ref[...] = v

    @pl.when(jnp.logical_not(first))
    def _():
        ref[...] += v


def _all_gather8(arrs, name):
    n = len(arrs)

    def body(*refs):
        ins, outs = refs[:n], refs[n:2 * n]
        send, recv = refs[2 * n:]
        x, y, c = _me()
        dev = 4 * x + 2 * y + c
        for a in range(n):
            outs[a][dev] = ins[a][...]
        sends = []
        for r in range(1, N_DEV):
            fx, fy, fc = (r >> 2) & 1, (r >> 1) & 1, r & 1
            peer = (_flip(x, fx), _flip(y, fy), _flip(c, fc))
            for a in range(n):
                cp = pltpu.make_async_remote_copy(
                    src_ref=ins[a], dst_ref=outs[a].at[dev],
                    send_sem=send.at[r - 1, a], recv_sem=recv.at[r - 1, a],
                    device_id=peer, device_id_type=MESH)
                cp.start()
                sends.append(cp)
        for r in range(1, N_DEV):
            fx, fy, fc = (r >> 2) & 1, (r >> 1) & 1, r & 1
            pdev = 4 * _flip(x, fx) + 2 * _flip(y, fy) + _flip(c, fc)
            for a in range(n):
                pltpu.make_async_remote_copy(
                    src_ref=ins[a], dst_ref=outs[a].at[pdev],
                    send_sem=send.at[r - 1, a], recv_sem=recv.at[r - 1, a],
                    device_id=(x, y, c), device_id_type=MESH).wait_recv()
        for cp in sends:
            cp.wait_send()

    return pl.pallas_call(
        body, name=name,
        out_shape=[jax.ShapeDtypeStruct((N_DEV,) + a.shape, a.dtype) for a in arrs],
        in_specs=[VMEM] * n, out_specs=[VMEM] * n,
        scratch_shapes=[pltpu.SemaphoreType.DMA((N_DEV - 1, n)),
                        pltpu.SemaphoreType.DMA((N_DEV - 1, n))],
    )(*arrs)


def _gather_mod(mod_shard):
    nb, w = mod_shard.shape

    def body(in_ref, out_ref, send, recv):
        x, y, c = _me()
        chip = 2 * x + y
        out_ref[:, pl.ds(pl.multiple_of(chip * w, 128), w)] = in_ref[...]
        sends = []
        for r, (fx, fy) in enumerate(CHIP_RELS):
            cp = pltpu.make_async_remote_copy(
                src_ref=in_ref,
                dst_ref=out_ref.at[:, pl.ds(pl.multiple_of(chip * w, 128), w)],
                send_sem=send.at[r], recv_sem=recv.at[r],
                device_id=(_flip(x, fx), _flip(y, fy), c), device_id_type=MESH)
            cp.start()
            sends.append(cp)
        for r, (fx, fy) in enumerate(CHIP_RELS):
            pchip = 2 * _flip(x, fx) + _flip(y, fy)
            pltpu.make_async_remote_copy(
                src_ref=in_ref,
                dst_ref=out_ref.at[:, pl.ds(pl.multiple_of(pchip * w, 128), w)],
                send_sem=send.at[r], recv_sem=recv.at[r],
                device_id=(x, y, c), device_id_type=MESH).wait_recv()
        for cp in sends:
            cp.wait_send()

    return pl.pallas_call(
        body, name="gather_mod",
        out_shape=jax.ShapeDtypeStruct((nb, N_CHIPS * w), mod_shard.dtype),
        in_specs=[VMEM], out_specs=VMEM,
        scratch_shapes=[pltpu.SemaphoreType.DMA((3,)), pltpu.SemaphoreType.DMA((3,))],
    )(mod_shard)


def _wlayout(d):
    d_in = 5 * d // 2
    return (
        (d, d_in // N_CHIPS, True),
        (d // N_CHIPS, d, False),
        (d, 4 * d // N_CHIPS, True),
        (4 * d // N_CHIPS, d, False),
    )


def _full_shape(lay):
    r, c, by_col = lay
    return (r, c * N_CHIPS) if by_col else (r * N_CHIPS, c)


def _full_view(ref, lay, k, h):
    r, c, by_col = lay
    hr = r // 2
    if by_col:
        return ref.at[pl.ds(pl.multiple_of(h * hr, 16), hr), pl.ds(pl.multiple_of(k * c, 128), c)]
    return ref.at[pl.ds(pl.multiple_of(k * r + h * hr, 16), hr), :]


def _half_view(ref, lay, h):
    hr = lay[0] // 2
    return ref.at[pl.ds(pl.multiple_of(h * hr, 16), hr), :]


def _half_shape(lay):
    return (lay[0] // 2, lay[1])


def _hbm(a):
    return pltpu.with_memory_space_constraint(a, pltpu.HBM)


def _remote(src, dst, send, recv, idx, to):
    return lambda: pltpu.make_async_remote_copy(src_ref=src, dst_ref=dst, send_sem=send.at[idx], recv_sem=recv.at[idx],
                                                device_id=to, device_id_type=MESH)


def _copy_start(name, bufs, n_sems, plan, after=()):
    nb, na = len(bufs), len(after)

    def body(*refs):
        sends, _ = plan(refs[:nb], refs[nb + na], refs[nb + na + 1])
        for mk in sends:
            mk().start()
        refs[-1][...] = jnp.zeros((8, 128), F32)

    outs = pl.pallas_call(
        body, name=name,
        out_shape=(pltpu.SemaphoreType.DMA((n_sems,)), pltpu.SemaphoreType.DMA((n_sems,)))
        + tuple(pltpu.HBM(b.shape, b.dtype) for b in bufs) + (jax.ShapeDtypeStruct((8, 128), F32),),
        in_specs=(HBM,) * nb + (ANY,) * na, out_specs=(SEM, SEM) + (HBM,) * nb + (VMEM,),
        input_output_aliases={i: 2 + i for i in range(nb)},
        compiler_params=pltpu.CompilerParams(has_side_effects=EFFECT),
    )(*[_hbm(b) for b in bufs], *after)
    return (outs[0], outs[1]), list(outs[2:2 + nb]), outs[-1]


def _copy_wait(name, bufs, sems, plan, after):
    nb, na = len(bufs), len(after)

    def body(*refs):
        sends, recvs = plan(refs[:nb], refs[nb], refs[nb + 1])
        for mk in sends:
            mk().wait_send()
        for mk in recvs:
            mk().wait_recv()

    outs = pl.pallas_call(
        body, name=name,
        out_shape=tuple(pltpu.HBM(b.shape, b.dtype) for b in bufs),
        in_specs=(HBM,) * nb + (SEM, SEM) + (ANY,) * na, out_specs=(HBM,) * nb,
        input_output_aliases={i: i for i in range(nb)},
        compiler_params=pltpu.CompilerParams(has_side_effects=EFFECT),
    )(*bufs, *sems, *after)
    return list(outs)


def _copy_blocking(name, bufs, n_sems, plan, after=()):
    nb, na = len(bufs), len(after)

    def body(*refs):
        sends, recvs = plan(refs[:nb], refs[2 * nb + na], refs[2 * nb + na + 1])
        started = [mk() for mk in sends]
        for cp in started:
            cp.start()
        for mk in recvs:
            mk().wait_recv()
        for cp in started:
            cp.wait_send()

    return list(pl.pallas_call(
        body, name=name,
        out_shape=tuple(jax.ShapeDtypeStruct(b.shape, b.dtype) for b in bufs),
        in_specs=(ANY,) * (nb + na), out_specs=(ANY,) * nb,
        input_output_aliases={i: i for i in range(nb)},
        scratch_shapes=[pltpu.SemaphoreType.DMA((n_sems,)), pltpu.SemaphoreType.DMA((n_sems,))],
    )(*bufs, *after))


def _exchange(name, bufs, n_sems, plan, between):
    if between is None:
        return _copy_blocking(name, bufs, n_sems, plan)
    sems, bufs, tok = _copy_start(name + "_start", bufs, n_sems, plan)
    return _copy_wait(name + "_wait", bufs, sems, plan, between(tok))


def _gather_direct_plan(lays):
    def plan(full, send, recv):
        x, y, c = _me()
        chip = 2 * x + y
        sends, recvs = [], []
        for r, (fx, fy) in enumerate(CHIP_RELS):
            px, py = _flip(x, fx), _flip(y, fy)
            for i, lay in enumerate(lays):
                for q in range(2):
                    oc = _flip(c, q)
                    mine = _full_view(full[i], lay, chip, c)
                    idx = (r * len(lays) + i) * 2 + q
                    sends.append(_remote(mine, mine, send, recv, idx, (px, py, oc)))
                    theirs = _full_view(full[i], lay, 2 * px + py, oc)
                    recvs.append(_remote(theirs, theirs, send, recv, idx, (x, y, c)))
        return sends, recvs
    return plan


def _gather_chip_plan(lays):
    def plan(full, send, recv):
        x, y, c = _me()
        chip = 2 * x + y
        sends, recvs = [], []
        for r, (fx, fy) in enumerate(CHIP_RELS):
            px, py = _flip(x, fx), _flip(y, fy)
            for i, lay in enumerate(lays):
                mine = _full_view(full[i], lay, chip, c)
                sends.append(_remote(mine, mine, send, recv, r * len(lays) + i, (px, py, c)))
                theirs = _full_view(full[i], lay, 2 * px + py, c)
                recvs.append(_remote(theirs, theirs, send, recv, r * len(lays) + i, (x, y, c)))
        return sends, recvs
    return plan


def _gather_pass_plan(lays):
    def plan(full, send, recv):
        x, y, c = _me()
        sends, recvs = [], []
        for r, (fx, fy) in enumerate(CHIP_RELS):
            pchip = 2 * _flip(x, fx) + _flip(y, fy)
            for i, lay in enumerate(lays):
                landed = _full_view(full[i], lay, pchip, c)
                sends.append(_remote(landed, landed, send, recv, r * len(lays) + i, (x, y, 1 - c)))
                other = _full_view(full[i], lay, pchip, 1 - c)
                recvs.append(_remote(other, other, send, recv, r * len(lays) + i, (x, y, c)))
        return sends, recvs
    return plan


def _gather8_plan(na):
    def plan(bufs, send, recv):
        x, y, c = _me()
        dev = 4 * x + 2 * y + c
        sends, recvs = [], []
        for r in range(1, N_DEV):
            fx, fy, fc = (r >> 2) & 1, (r >> 1) & 1, r & 1
            px, py, pc = _flip(x, fx), _flip(y, fy), _flip(c, fc)
            for a in range(na):
                idx = (r - 1) * na + a
                sends.append(_remote(bufs[a].at[dev], bufs[a].at[dev], send, recv, idx, (px, py, pc)))
                theirs = bufs[a].at[4 * px + 2 * py + pc]
                recvs.append(_remote(theirs, theirs, send, recv, idx, (x, y, c)))
        return sends, recvs
    return plan


def _pair_exchange_plan(lays):
    nw = len(lays)

    def plan(bufs, send, recv):
        x, y, c = _me()
        sends, recvs = [], []
        for i, lay in enumerate(lays):
            for k in range(N_CHIPS):
                sends.append(_remote(_full_view(bufs[i], lay, k, 1 - c), bufs[nw + i].at[k],
                                     send, recv, i * N_CHIPS + k, (x, y, 1 - c)))
                recvs.append(_remote(_full_view(bufs[i], lay, k, c), bufs[nw + i].at[k],
                                     send, recv, i * N_CHIPS + k, (x, y, c)))
        return sends, recvs
    return plan


def _chip_exchange_plan(nw):
    def plan(bufs, send, recv):
        x, y, c = _me()
        sends, recvs = [], []
        for r, (fx, fy) in enumerate(CHIP_RELS):
            px, py = _flip(x, fx), _flip(y, fy)
            for i in range(nw):
                sends.append(_remote(bufs[i].at[2 * px + py], bufs[nw + i].at[r], send, recv, r * nw + i, (px, py, c)))
                recvs.append(_remote(bufs[i].at[2 * px + py], bufs[nw + i].at[r], send, recv, r * nw + i, (x, y, c)))
        return sends, recvs
    return plan


def _pair_share_plan(lays):
    def plan(bufs, send, recv):
        x, y, c = _me()
        sends, recvs = [], []
        for i, lay in enumerate(lays):
            mine = _half_view(bufs[i], lay, c)
            sends.append(_remote(mine, mine, send, recv, i, (x, y, 1 - c)))
            other = _half_view(bufs[i], lay, 1 - c)
            recvs.append(_remote(other, other, send, recv, i, (x, y, c)))
        return sends, recvs
    return plan


def _pair_sum(ids, g, got, lay, name):
    r, c, by_col = lay
    hr = r // 2
    tr = min(hr, 256)
    nt = hr // tr

    def body(ids_ref, g_ref, got_ref, s32_ref, s16_ref):
        s = g_ref[...] + got_ref[...]
        s16_ref[...] = s.astype(BF16)

        @pl.when(pl.program_id(1) == ids_ref[0])
        def _():
            s32_ref[...] = s

    if by_col:
        gspec = pl.BlockSpec((tr, c), lambda t, k, ids: (ids[1] * nt + t, k))
    else:
        gspec = pl.BlockSpec((tr, c), lambda t, k, ids: ((2 * k + ids[1]) * nt + t, 0))
    hspec = pl.BlockSpec((None, tr, c), lambda t, k, ids: (k, t, 0))
    return pl.pallas_call(
        body, name=name,
        grid_spec=pltpu.PrefetchScalarGridSpec(
            num_scalar_prefetch=1, grid=(nt, N_CHIPS),
            in_specs=[gspec, hspec], out_specs=[pl.BlockSpec((tr, c), lambda t, k, ids: (t, 0)), hspec]),
        out_shape=[jax.ShapeDtypeStruct((hr, c), F32),
                   jax.ShapeDtypeStruct((N_CHIPS, hr, c), BF16)],
    )(ids, g, got)


def _chip_sum(ids, s32, got, lay, name):
    hr, c = _half_shape(lay)
    tr = min(hr, 256)
    nt = hr // tr

    def body(ids_ref, s_ref, got_ref, out_ref):
        t = s_ref[...]
        for r in range(3):
            t = t + got_ref[r].astype(F32)
        out_ref[...] = t

    return pl.pallas_call(
        body, name=name,
        grid_spec=pltpu.PrefetchScalarGridSpec(
            num_scalar_prefetch=1, grid=(nt,),
            in_specs=[pl.BlockSpec((tr, c), lambda t, ids: (t, 0)),
                      pl.BlockSpec((3, tr, c), lambda t, ids: (0, t, 0))],
            out_specs=pl.BlockSpec((tr, c), lambda t, ids: (ids[1] * nt + t, 0))),
        out_shape=jax.ShapeDtypeStruct((2 * hr, c), F32),
    )(ids, s32, got)


def _reduce_scatter(ids, grads, lays, names, tag, between):
    nw = len(lays)
    got1 = [lax.empty((N_CHIPS,) + _half_shape(l), F32) for l in lays]
    bufs = _exchange("pair_exchange_" + tag, list(grads) + got1, nw * N_CHIPS, _pair_exchange_plan(lays), between[0])
    sums = [_pair_sum(ids, bufs[i], bufs[nw + i], lays[i], "pair_sum_" + names[i]) for i in range(nw)]
    got2 = [lax.empty((3,) + _half_shape(l), BF16) for l in lays]
    bufs = _exchange("chip_exchange_" + tag, [s[1] for s in sums] + got2, 3 * nw, _chip_exchange_plan(nw), between[1])
    mine = [_chip_sum(ids, sums[i][0], bufs[nw + i], lays[i], "chip_sum_" + names[i]) for i in range(nw)]
    return _exchange("pair_share_" + tag, mine, nw, _pair_share_plan(lays), between[2])


def _cast_place(ids, ws, lays):
    nw = len(ws)

    def body(ids_ref, *refs):
        for i in range(nw):
            refs[nw + i][...] = refs[i][...].astype(BF16)

    by_col_map = lambda i, ids: (0, ids[0])
    by_row_map = lambda i, ids: (ids[0], 0)
    return pl.pallas_call(
        body, name="cast_place",
        grid_spec=pltpu.PrefetchScalarGridSpec(
            num_scalar_prefetch=1, grid=(1,),
            in_specs=[pl.BlockSpec(l[:2], lambda i, ids: (0, 0)) for l in lays],
            out_specs=[pl.BlockSpec(l[:2], by_col_map if l[2] else by_row_map) for l in lays]),
        out_shape=[jax.ShapeDtypeStruct(_full_shape(l), BF16) for l in lays],
    )(ids, *ws)


def _ada_mod(c_all, w_ada, b_ada):
    def body(c_ref, w_ref, b_ref, act_ref, mod_ref):
        cv = c_ref[...]
        act = cv * _sigmoid(cv)
        act_ref[...] = act
        mod_ref[...] = jnp.dot(act, w_ref[...], preferred_element_type=F32, precision=HIGHEST) + b_ref[...]

    nb = c_all.shape[0]
    return pl.pallas_call(
        body, name="ada_mod",
        out_shape=[jax.ShapeDtypeStruct(c_all.shape, F32),
                   jax.ShapeDtypeStruct((nb, w_ada.shape[1]), F32)],
        in_specs=[VMEM] * 3, out_specs=[VMEM] * 2,
    )(c_all, w_ada, b_ada)


def _run_units(*unit_lists):
    total = max(len(u) for u in unit_lists)
    done = [0] * len(unit_lists)
    for step in range(1, total + 1):
        for li, units in enumerate(unit_lists):
            upto = (step * len(units) + total - 1) // total
            while done[li] < upto:
                units[done[li]]()
                done[li] += 1


def _in_mix_fwd(x, mod3, w_in, wb, dwb, lng, lnb, scw, t, dep):
    n, d = x.shape
    d_in = w_in.shape[1]
    dc = d_in // 5
    tm = min(TM_MIX, t)
    tps = t // tm
    nt = n // tm
    rb = min(RB_CONV, tm)
    ncol = 256
    ng = dc // 128
    assert tps % 2 == 0 and nt % 2 == 0

    def proj_units(x_ref, mod_ref, w_ref, p_ref, h_ref):
        def head():
            xn, _ = _rms(x_ref[...])
            h_ref[...] = (xn * (1.0 + mod_ref[:, d:2 * d]) + mod_ref[:, 0:d]).astype(BF16)
        units = [head]
        for c0 in range(0, d_in, ncol):
            def chunk(c0=c0):
                p_ref[:, c0:c0 + ncol] = _dot(h_ref[...], w_ref[:, c0:c0 + ncol])
            units.append(chunk)
        return units

    def mix_units(first, p_ref, h_ref, r0, wb_ref, dwb_ref, lng_ref, lnb_ref, scw_ref,
                  proj_ref, h1_ref, mixed_ref, a1_ref, q_ref, ext_ref, e_ref, extp_ref, a1s_ref):
        rows = slice(r0, r0 + tm)
        units = []

        def glu():
            halo = ext_ref[tm:tm + HALO, :]
            ext_ref[0:HALO, :] = halo if first is False else jnp.where(first, 0.0, halo)
            ext_ref[HALO:HALO + tm, :] = p_ref[:, 0:dc] * _sigmoid(p_ref[:, dc:2 * dc])
        units.append(glu)
        for g in range(ng):
            def shift(g=g):
                for r in range(8):
                    e_ref[g, r, 0:tm + HALO, :] = ext_ref[r:r + tm + HALO, 128 * g:128 * g + 128]
            units.append(shift)
        for g in range(ng):
            lanes = slice(128 * g, 128 * g + 128)
            for i0 in range(0, tm, rb):
                def conv(g=g, lanes=lanes, i0=i0):
                    acc = jnp.zeros((rb, 128), F32)
                    for k in range(CONF_K):
                        m, r = divmod(k + HALO - CONF_K + 1, 8)
                        acc = acc + e_ref[g, r, i0 + 8 * m:i0 + 8 * m + rb, :] * wb_ref[k:k + 1, lanes]
                    a1s_ref[i0:i0 + rb, lanes] = acc + dwb_ref[:, lanes]
                units.append(conv)

        def norm():
            a1 = a1s_ref[...]
            a1_ref[rows, :] = a1
            mu = jnp.mean(a1, axis=-1, keepdims=True)
            ac = a1 - mu
            rstd = lax.rsqrt(jnp.mean(ac * ac, axis=-1, keepdims=True) + EPS)
            a2 = ac * rstd * lng_ref[...] + lnb_ref[...]
            mixed_ref[rows, 0:dc] = (a2 * _sigmoid(a2)).astype(BF16)
        units.append(norm)

        def short():
            halo = extp_ref[tm:tm + 8, :]
            extp_ref[0:8, :] = halo if first is False else jnp.where(first, 0.0, halo)
            extp_ref[8:8 + tm, :] = p_ref[:, 3 * dc:4 * dc] * p_ref[:, 4 * dc:5 * dc]
            q = jnp.zeros((tm, dc), F32)
            for k in range(SC_K):
                q = q + extp_ref[6 + k:6 + k + tm, :] * scw_ref[k:k + 1, :]
            q_ref[rows, :] = q
            mixed_ref[rows, dc:2 * dc] = (p_ref[:, 2 * dc:3 * dc] * q).astype(BF16)
        units.append(short)

        def keep():
            proj_ref[rows, :] = p_ref[...]
            h1_ref[rows, :] = h_ref[...]
        units.append(keep)
        return units

    def body(x0_ref, xa_ref, xb_ref, mod0_ref, moda_ref, modb_ref, w_ref,
             wb_ref, dwb_ref, lng_ref, lnb_ref, scw_ref, dep_ref,
             proj_ref, h1_ref, mixed_ref, a1_ref, q_ref,
             p0_ref, p1_ref, h0_ref, hh1_ref, ext_ref, e_ref, extp_ref, a1s_ref):
        j = pl.program_id(0)

        @pl.when(j == 0)
        def _():
            ext_ref[...] = jnp.zeros(ext_ref.shape, F32)
            extp_ref[...] = jnp.zeros(extp_ref.shape, F32)
            _run_units(proj_units(x0_ref, mod0_ref, w_ref, p0_ref, h0_ref))

        common = (wb_ref, dwb_ref, lng_ref, lnb_ref, scw_ref, proj_ref, h1_ref, mixed_ref, a1_ref, q_ref,
                  ext_ref, e_ref, extp_ref, a1s_ref)
        _run_units(mix_units((2 * j) % tps == 0, p0_ref, h0_ref, 0, *common),
                   proj_units(xa_ref, moda_ref, w_ref, p1_ref, hh1_ref))
        _run_units(mix_units(False, p1_ref, hh1_ref, tm, *common),
                   proj_units(xb_ref, modb_ref, w_ref, p0_ref, h0_ref))

    last = nt - 1
    xspec = lambda f: pl.BlockSpec((tm, d), lambda j: (f(j), 0))
    mspec = lambda f: pl.BlockSpec((None, 1, 6 * d), lambda j: (f(j) // tps, 0, 0))
    out2 = lambda c: pl.BlockSpec((2 * tm, c), lambda j: (j, 0))
    return pl.pallas_call(
        body, name="in_mix_fwd", grid=(nt // 2,),
        in_specs=[xspec(lambda j: 0), xspec(lambda j: 2 * j + 1), xspec(lambda j: jnp.minimum(2 * j + 2, last)),
                  mspec(lambda j: 0), mspec(lambda j: 2 * j + 1), mspec(lambda j: jnp.minimum(2 * j + 2, last)),
                  _resident((d, d_in)),
                  _const(wb.shape), _const(dwb.shape), _const(lng.shape), _const(lnb.shape), _const(scw.shape), ANY],
        out_specs=[out2(d_in), out2(d), out2(2 * dc), out2(dc), out2(dc)],
        out_shape=[jax.ShapeDtypeStruct((n, d_in), F32), jax.ShapeDtypeStruct((n, d), BF16),
                   jax.ShapeDtypeStruct((n, 2 * dc), BF16), jax.ShapeDtypeStruct((n, dc), F32),
                   jax.ShapeDtypeStruct((n, dc), F32)],
        scratch_shapes=[pltpu.VMEM((tm, d_in), F32), pltpu.VMEM((tm, d_in), F32),
                        pltpu.VMEM((tm, d), BF16), pltpu.VMEM((tm, d), BF16),
                        pltpu.VMEM((tm + HALO + 8, dc), F32), _shift_scratch(tm, dc),
                        pltpu.VMEM((tm + 8, dc), F32), pltpu.VMEM((tm, dc), F32)],
        compiler_params=pltpu.CompilerParams(dimension_semantics=("arbitrary",)),
    )(x, x, x, mod3, mod3, mod3, w_in, wb, dwb, lng, lnb, scw, dep)


def _shift_scratch(tm, dc):
    return pltpu.VMEM((dc // 128, 8, tm + HALO + 8, 128), F32)


def _out_proj(mixed, w_out, x, mod3, t, dep):
    n, d = x.shape
    tm = min(TM_MM, t)
    tps = t // tm

    nt = n // tm
    nbuf = 3

    def body(m_hbm, w_ref, x_hbm, mod_ref, dep_ref, x1_ref, y1_ref, h2_ref, m_buf, x_buf, sems):
        i = pl.program_id(0)

        def reads(tile, slot):
            rows = pl.ds(pl.multiple_of(tile * tm, tm), tm)
            return (pltpu.make_async_copy(m_hbm.at[rows, :], m_buf.at[slot], sems.at[0, slot]),
                    pltpu.make_async_copy(x_hbm.at[rows, :], x_buf.at[slot], sems.at[1, slot]))

        @pl.when(i == 0)
        def _():
            for tile in range(min(nbuf - 1, nt)):
                for cp in reads(tile, tile):
                    cp.start()

        ahead = i + nbuf - 1

        @pl.when(ahead < nt)
        def _():
            for cp in reads(ahead, ahead % nbuf):
                cp.start()

        slot = i % nbuf
        for cp in reads(i, slot):
            cp.wait()
        y1 = _dot(m_buf[slot], w_ref[...])
        y1_ref[...] = y1.astype(BF16)
        x1 = x_buf[slot] + mod_ref[:, 2 * d:3 * d] * y1
        x1_ref[...] = x1
        xn, _ = _rms(x1)
        h2_ref[...] = (xn * (1.0 + mod_ref[:, 4 * d:5 * d]) + mod_ref[:, 3 * d:4 * d]).astype(BF16)

    return pl.pallas_call(
        body, name="out_proj", grid=(nt,),
        in_specs=[ANY, _resident((d, d)), ANY, _modspec(tps, 6 * d), ANY],
        out_specs=[_rowblk(tm, d), _rowblk(tm, d), _rowblk(tm, d)],
        out_shape=[jax.ShapeDtypeStruct((n, d), F32), jax.ShapeDtypeStruct((n, d), BF16),
                   jax.ShapeDtypeStruct((n, d), BF16)],
        scratch_shapes=[pltpu.VMEM((nbuf, tm, d), BF16), pltpu.VMEM((nbuf, tm, d), F32),
                        pltpu.SemaphoreType.DMA((2, nbuf))],
        compiler_params=pltpu.CompilerParams(dimension_semantics=("arbitrary",)),
    )(mixed, w_out, x, mod3, dep)


def _mlp_fwd(h2, x1, tgt, mod3, gfin, w1, w2, t):
    n, d = x1.shape
    dff = w1.shape[1]
    tm = min(TM_MLP, t)
    tps = t // tm
    nt = n // tm

    def body(h_ref, x1_ref, tg_ref, mod_ref, gf_ref, w1_ref, w2_ref,
             z_ref, dx2_ref, dy2_ref, dg2_ref, ggf_ref, loss_ref):
        i = pl.program_id(0)
        g2 = mod_ref[:, 5 * d:6 * d]
        gf = gf_ref[...]
        sub = tm // MLP_SUB
        sums = None
        for part in range(MLP_SUB):
            rs = slice(part * sub, (part + 1) * sub)
            hv = h_ref[rs, :]
            y2 = jnp.zeros((sub, d), F32)
            for j in range(dff // FF_CHUNK):
                cols = slice(j * FF_CHUNK, (j + 1) * FF_CHUNK)
                z = _dot(hv, w1_ref[:, cols])
                z_ref[rs, cols] = z.astype(BF16)
                zr = jnp.maximum(z, 0.0)
                y2 = y2 + _dot((zr * zr).astype(BF16), w2_ref[cols, :])
            x2n, r3 = _rms(x1_ref[rs, :] + g2 * y2)
            diff = x2n * gf - tg_ref[rs, :]
            dout = diff * (1.0 / d)
            dx2 = _rms_bwd(dout * gf, x2n, r3)
            dx2_ref[rs, :] = dx2
            dy2_ref[rs, :] = (g2 * dx2).astype(BF16)
            p = (_rows8(dx2 * y2), _rows8(dout * x2n), _rows8(diff * diff))
            sums = p if sums is None else tuple(a + b for a, b in zip(sums, p))
        _acc_add(dg2_ref, i % tps == 0, sums[0])
        _acc_add(ggf_ref, i == 0, sums[1])
        _acc_add(loss_ref, i == 0, sums[2])

    return pl.pallas_call(
        body, name="mlp_fwd", grid=(nt,),
        in_specs=[_rowblk(tm, d), _rowblk(tm, d), _rowblk(tm, d), _modspec(tps, 6 * d), _const((1, d)),
                  _resident((d, dff)), _resident((dff, d))],
        out_specs=[_rowblk(tm, dff), _rowblk(tm, d), _rowblk(tm, d), _accspec(tps, d),
                   _const((8, d)), _const((8, d))],
        out_shape=[jax.ShapeDtypeStruct((n, dff), BF16), jax.ShapeDtypeStruct((n, d), F32),
                   jax.ShapeDtypeStruct((n, d), BF16), jax.ShapeDtypeStruct((n // t, 8, d), F32),
                   jax.ShapeDtypeStruct((8, d), F32), jax.ShapeDtypeStruct((8, d), F32)],
        compiler_params=pltpu.CompilerParams(dimension_semantics=("arbitrary",)),
    )(h2, x1, tgt, mod3, gfin, w1, w2)


def _mlp_bwd(dy2, z, x1, dx2, y1, mod3, w1, w2, t):
    n, d = x1.shape
    dff = w1.shape[1]
    tm = min(TM_MLP, t)
    tps = t // tm

    def body(dy2_ref, z_ref, x1_ref, dx2_ref, y1_ref, mod_ref, w1_ref, w2_ref,
             dz_ref, dx1_ref, dy1_ref, dsh_ref, dsc_ref, dg1_ref):
        first = pl.program_id(0) % tps == 0
        sub = tm // MLP_SUB
        sums = None
        for part in range(MLP_SUB):
            rs = slice(part * sub, (part + 1) * sub)
            dy2 = dy2_ref[rs, :]
            dh2 = jnp.zeros((sub, d), F32)
            for j in range(dff // FF_CHUNK):
                cols = slice(j * FF_CHUNK, (j + 1) * FF_CHUNK)
                du = _dot_nt(dy2, w2_ref[cols, :])
                dz = (du * (2.0 * jnp.maximum(z_ref[rs, cols].astype(F32), 0.0))).astype(BF16)
                dz_ref[rs, cols] = dz
                dh2 = dh2 + _dot_nt(dz, w1_ref[:, cols])
            x1n, r2 = _rms(x1_ref[rs, :])
            dx1 = dx2_ref[rs, :] + _rms_bwd(dh2 * (1.0 + mod_ref[:, 4 * d:5 * d]), x1n, r2)
            dx1_ref[rs, :] = dx1
            dy1_ref[rs, :] = (mod_ref[:, 2 * d:3 * d] * dx1).astype(BF16)
            p = (_rows8(dh2), _rows8(dh2 * x1n), _rows8(dx1 * y1_ref[rs, :].astype(F32)))
            sums = p if sums is None else tuple(a + b for a, b in zip(sums, p))
        _acc_add(dsh_ref, first, sums[0])
        _acc_add(dsc_ref, first, sums[1])
        _acc_add(dg1_ref, first, sums[2])

    acc = jax.ShapeDtypeStruct((n // t, 8, d), F32)
    return pl.pallas_call(
        body, name="mlp_bwd", grid=(n // tm,),
        in_specs=[_rowblk(tm, d), _rowblk(tm, dff), _rowblk(tm, d), _rowblk(tm, d), _rowblk(tm, d),
                  _modspec(tps, 6 * d), _resident((d, dff)), _resident((dff, d))],
        out_specs=[_rowblk(tm, dff), _rowblk(tm, d), _rowblk(tm, d),
                   _accspec(tps, d), _accspec(tps, d), _accspec(tps, d)],
        out_shape=[jax.ShapeDtypeStruct((n, dff), BF16), jax.ShapeDtypeStruct((n, d), F32),
                   jax.ShapeDtypeStruct((n, d), BF16), acc, acc, acc],
        compiler_params=pltpu.CompilerParams(dimension_semantics=("arbitrary",)),
    )(dy2, z, x1, dx2, y1, mod3, w1, w2)


def _wgrad(a, b, name, relu2=False, bn=None, deps=()):
    n, ka = a.shape
    nb = b.shape[1]
    tk = min(TK_WG if ka * nb >= TK_WG * 1024 else TK_WG // 2, n)
    bm = min(ka, 1024)
    if bn is None:
        bn = nb if nb <= 1024 else (1280 if nb % 1280 == 0 else 1024)

    def body(a_ref, b_ref, *rest):
        out_ref = rest[-1]
        av = a_ref[...]
        if relu2:
            ar = jnp.maximum(av, 0.0)
            av = ar * ar
        p = _dot_tn(av, b_ref[...])
        _acc_add(out_ref, pl.program_id(2) == 0, p)

    return pl.pallas_call(
        body, name=name, grid=(ka // bm, nb // bn, n // tk),
        in_specs=[pl.BlockSpec((tk, bm), lambda i, j, k: (k, i)),
                  pl.BlockSpec((tk, bn), lambda i, j, k: (k, j))] + [ANY] * len(deps),
        out_specs=pl.BlockSpec((bm, bn), lambda i, j, k: (i, j)),
        out_shape=jax.ShapeDtypeStruct((ka, nb), F32),
        compiler_params=pltpu.CompilerParams(dimension_semantics=("parallel", "parallel", "arbitrary")),
    )(a, b, *deps)


SG_DWW = 0
SG_DWB = CONF_K
SG_LNG = CONF_K + 1
SG_LNB = CONF_K + 2
SG_SCW = CONF_K + 3
SG_N = CONF_K + 3 + SC_K


def _mix_in_bwd(dy1, w_out, a1, q, proj, wb, lng, lnb, scw, w_in, x, dx1, mod3, t, dep):
    n, d_in = proj.shape
    d = x.shape[1]
    dc = d_in // 5
    tm = min(TM_MIX, t)
    tps = t // tm
    nt = n // tm
    rw = min(RB_WG, tm)
    ng = dc // 128
    ncol = 256

    def ln_bwd(a1v, da3, lng_v, lnb_v):
        mu = jnp.mean(a1v, axis=-1, keepdims=True)
        ac = a1v - mu
        rstd = lax.rsqrt(jnp.mean(ac * ac, axis=-1, keepdims=True) + EPS)
        ah = ac * rstd
        a2 = ah * lng_v + lnb_v
        s2 = _sigmoid(a2)
        da2 = da3 * (s2 * (1.0 + a2 * (1.0 - s2)))
        dah = da2 * lng_v
        da1 = rstd * (dah - jnp.mean(dah, axis=-1, keepdims=True)
                      - ah * jnp.mean(dah * ah, axis=-1, keepdims=True))
        return da1, da2, ah

    def dmixed_units(dy_ref, wo_ref, dm_ref):
        units = []
        for c0 in range(0, 2 * dc, ncol):
            def chunk(c0=c0):
                dm_ref[:, c0:c0 + ncol] = _dot_nt(dy_ref[...], wo_ref[c0:c0 + ncol, :])
            units.append(chunk)
        return units

    def mix_units(k, dm_ref, a1_ref, q_ref, p_ref, wb_ref, lng_ref, lnb_ref, scw_ref,
                  dproj_ref, sg_ref, extd_ref, ed_ref, a0_ref, da0_ref, extq_ref, cda_ref, cdq_ref):
        keep_next = jnp.where(k % tps == tps - 1, 0.0, 1.0)
        units = []

        def head():
            lng_v, lnb_v = lng_ref[...], lnb_ref[...]
            da1, da2, ah = ln_bwd(a1_ref[...], dm_ref[:, 0:dc], lng_v, lnb_v)
            sg_ref[8 * SG_LNG:8 * SG_LNG + 8, :] += _rows8(da2 * ah)
            sg_ref[8 * SG_LNB:8 * SG_LNB + 8, :] += _rows8(da2)
            sg_ref[8 * SG_DWB:8 * SG_DWB + 8, :] += _rows8(da1)
            a0_ref[...] = p_ref[:, 0:dc] * _sigmoid(p_ref[:, dc:2 * dc])
            extd_ref[0:tm, :] = da1
            extd_ref[tm:tm + HALO, :] = cda_ref[...] * keep_next
            extd_ref[tm + HALO:tm + HALO + 8, :] = jnp.zeros((8, dc), F32)
            cda_ref[...] = da1[0:HALO, :]
            ds = dm_ref[:, dc:2 * dc]
            dproj_ref[:, 2 * dc:3 * dc] = (ds * q_ref[...]).astype(BF16)
            dq = ds * p_ref[:, 2 * dc:3 * dc]
            extq_ref[0:tm, :] = dq
            extq_ref[tm:tm + 8, :] = cdq_ref[...] * keep_next
            cdq_ref[...] = dq[0:8, :]
        units.append(head)
        for g in range(ng):
            def shift(g=g):
                for r in range(8):
                    ed_ref[g, r, 0:tm + HALO, :] = extd_ref[r:r + tm + HALO, 128 * g:128 * g + 128]
            units.append(shift)
        for g in range(ng):
            lanes = slice(128 * g, 128 * g + 128)
            accs = [None] * CONF_K
            for i0 in range(0, tm, rw):
                def conv(g=g, lanes=lanes, i0=i0, accs=accs):
                    a0v = a0_ref[i0:i0 + rw, lanes]
                    acc = jnp.zeros((rw, 128), F32)
                    for s in range(CONF_K):
                        m, r = divmod(s, 8)
                        e = ed_ref[g, r, i0 + 8 * m:i0 + 8 * m + rw, :]
                        acc = acc + e * wb_ref[CONF_K - 1 - s:CONF_K - s, lanes]
                        part = _rows8(e * a0v)
                        accs[s] = part if accs[s] is None else accs[s] + part
                    da0_ref[i0:i0 + rw, lanes] = acc
                units.append(conv)

            def flush(lanes=lanes, accs=accs):
                for s in range(CONF_K):
                    kk = CONF_K - 1 - s
                    sg_ref[8 * (SG_DWW + kk):8 * (SG_DWW + kk) + 8, lanes] += accs[s]
            units.append(flush)

        def emit():
            da0 = da0_ref[...]
            sig = _sigmoid(p_ref[:, dc:2 * dc])
            dproj_ref[:, 0:dc] = (da0 * sig).astype(BF16)
            dproj_ref[:, dc:2 * dc] = (da0 * a0_ref[...] * (1.0 - sig)).astype(BF16)
        units.append(emit)

        def short():
            scc, sch = p_ref[:, 3 * dc:4 * dc], p_ref[:, 4 * dc:5 * dc]
            pv = scc * sch
            dp = jnp.zeros((tm, dc), F32)
            for kk in range(SC_K):
                dqs = extq_ref[SC_K - 1 - kk:SC_K - 1 - kk + tm, :]
                dp = dp + dqs * scw_ref[kk:kk + 1, :]
                sg_ref[8 * (SG_SCW + kk):8 * (SG_SCW + kk) + 8, :] += _rows8(pv * dqs)
            dproj_ref[:, 3 * dc:4 * dc] = (dp * sch).astype(BF16)
            dproj_ref[:, 4 * dc:5 * dc] = (dp * scc).astype(BF16)
        units.append(short)
        return units

    def proj_units(first, dp_ref, w_ref, x_ref, dx1_ref, mod_ref, dh_ref, gx_ref, dsh_ref, dsc_ref):
        units = []
        for c0 in range(0, d, ncol):
            def chunk(c0=c0):
                dh_ref[:, c0:c0 + ncol] = _dot_nt(dp_ref[...], w_ref[c0:c0 + ncol, :])
            units.append(chunk)

        def tail():
            dh1 = dh_ref[...]
            xn, r1 = _rms(x_ref[...])
            v1, v2 = _rows8(dh1), _rows8(dh1 * xn)
            dsh_ref[...] = jnp.where(first, v1, dsh_ref[...] + v1)
            dsc_ref[...] = jnp.where(first, v2, dsc_ref[...] + v2)
            gx_ref[...] = dx1_ref[...] + _rms_bwd(dh1 * (1.0 + mod_ref[:, d:2 * d]), xn, r1)
        units.append(tail)
        return units

    def body(dyl_ref, dy_ref, wo_ref, a1_ref, q_ref, p_ref, wb_ref, lng_ref, lnb_ref, scw_ref,
             w_ref, x_ref, dx1_ref, mod_ref, dep_ref,
             dproj_ref, sg_ref, gx_ref, dsh_ref, dsc_ref,
             extd_ref, ed_ref, a0_ref, da0_ref, extq_ref, dp_ref, dh_ref, dm_ref, cda_ref, cdq_ref):
        k = pl.program_id(0)

        @pl.when(k == 0)
        def _():
            sg_ref[...] = jnp.zeros(sg_ref.shape, F32)
            dp_ref[...] = jnp.zeros(dp_ref.shape, BF16)
            cda_ref[...] = jnp.zeros(cda_ref.shape, F32)
            cdq_ref[...] = jnp.zeros(cdq_ref.shape, F32)
            _run_units(dmixed_units(dyl_ref, wo_ref, dm_ref))

        first = jnp.logical_or(k <= 1, (nt - k) % tps == tps - 1)
        after = lambda: proj_units(first, dp_ref, w_ref, x_ref, dx1_ref, mod_ref, dh_ref, gx_ref, dsh_ref, dsc_ref)

        @pl.when(k < nt)
        def _():
            _run_units(mix_units(nt - 1 - k, dm_ref, a1_ref, q_ref, p_ref, wb_ref, lng_ref, lnb_ref, scw_ref,
                                 dproj_ref, sg_ref, extd_ref, ed_ref, a0_ref, da0_ref, extq_ref, cda_ref, cdq_ref),
                       after(), dmixed_units(dy_ref, wo_ref, dm_ref))
            dp_ref[...] = dproj_ref[...]

        @pl.when(k == nt)
        def _():
            _run_units(after())

    cur = lambda k: jnp.maximum(nt - 1 - k, 0)
    prev = lambda k: jnp.minimum(nt - k, nt - 1)
    acc = jax.ShapeDtypeStruct((n // t, 8, d), F32)
    return pl.pallas_call(
        body, name="mix_in_bwd", grid=(nt + 1,),
        in_specs=[pl.BlockSpec((tm, d), lambda k: (nt - 1, 0)),
                  pl.BlockSpec((tm, d), lambda k: (jnp.maximum(nt - 2 - k, 0), 0)), _resident(w_out.shape),
                  pl.BlockSpec((tm, dc), lambda k: (cur(k), 0)),
                  pl.BlockSpec((tm, dc), lambda k: (cur(k), 0)),
                  pl.BlockSpec((tm, d_in), lambda k: (cur(k), 0)),
                  _const(wb.shape), _const(lng.shape), _const(lnb.shape), _const(scw.shape),
                  _resident((d, d_in)),
                  pl.BlockSpec((tm, d), lambda k: (prev(k), 0)), pl.BlockSpec((tm, d), lambda k: (prev(k), 0)),
                  pl.BlockSpec((None, 1, 6 * d), lambda k: (prev(k) // tps, 0, 0)), ANY],
        out_specs=[pl.BlockSpec((tm, d_in), lambda k: (cur(k), 0)), _const((8 * SG_N, dc)),
                   pl.BlockSpec((tm, d), lambda k: (prev(k), 0)),
                   pl.BlockSpec((None, 8, d), lambda k: (prev(k) // tps, 0, 0)),
                   pl.BlockSpec((None, 8, d), lambda k: (prev(k) // tps, 0, 0))],
        out_shape=[jax.ShapeDtypeStruct((n, d_in), BF16), jax.ShapeDtypeStruct((8 * SG_N, dc), F32),
                   jax.ShapeDtypeStruct((n, d), F32), acc, acc],
        scratch_shapes=[pltpu.VMEM((tm + HALO + 8, dc), F32), _shift_scratch(tm, dc),
                        pltpu.VMEM((tm, dc), F32), pltpu.VMEM((tm, dc), F32),
                        pltpu.VMEM((tm + 8, dc), F32),
                        pltpu.VMEM((tm, d_in), BF16), pltpu.VMEM((tm, d), F32), pltpu.VMEM((tm, 2 * dc), F32),
                        pltpu.VMEM((HALO, dc), F32), pltpu.VMEM((8, dc), F32)],
        compiler_params=pltpu.CompilerParams(dimension_semantics=("arbitrary",)),
    )(dy1, dy1, w_out, a1, q, proj, wb, lng, lnb, scw, w_in, x, dx1, mod3, dep)


SMALL_ROWS = 40


def _pack_small(ids, sg, ggf, loss, accs, d, dep):
    dc = d // 2
    nb = accs[0].shape[0]

    def body(ids_ref, sg_ref, ggf_ref, loss_ref, dsh1, dsc1, dg1, dsh2, dsc2, dg2, dep_ref, pack_ref, dmod_ref):
        pack_ref[...] = jnp.zeros(pack_ref.shape, F32)
        for k in range(SG_N):
            pack_ref[k:k + 1, :] = jnp.sum(sg_ref[8 * k:8 * k + 8, :], axis=0, keepdims=True)
        gf = jnp.sum(ggf_ref[...], axis=0, keepdims=True)
        pack_ref[SG_N:SG_N + 1, :] = gf[:, 0:dc]
        pack_ref[SG_N + 1:SG_N + 2, :] = gf[:, dc:d]
        tot = jnp.sum(jnp.sum(loss_ref[...], axis=0, keepdims=True), axis=1, keepdims=True) * (0.5 / d)
        pack_ref[SG_N + 2:SG_N + 3, :] = jnp.broadcast_to(tot, (1, dc))
        for b in range(nb):
            row = jnp.concatenate([jnp.sum(ref[b], axis=0, keepdims=True)
                                   for ref in (dsh1, dsc1, dg1, dsh2, dsc2, dg2)], axis=1)
            for f in range(fold):
                dmod_ref[b * fold + f:b * fold + f + 1, :] = row[:, f * wf:(f + 1) * wf]

    fold = 8 // nb
    wf = 6 * d // fold
    assert nb * fold == 8 and wf % 128 == 0
    whole = lambda a: pl.BlockSpec(a.shape, lambda i, ids: (0,) * a.ndim)
    mine = lambda r, c: pl.BlockSpec((None, r, c), lambda i, ids: (ids[2], 0, 0))
    return pl.pallas_call(
        body, name="pack_small",
        grid_spec=pltpu.PrefetchScalarGridSpec(
            num_scalar_prefetch=1, grid=(1,),
            in_specs=[whole(a) for a in (sg, ggf, loss, *accs)] + [ANY],
            out_specs=[mine(SMALL_ROWS, dc), mine(8, wf)]),
        out_shape=[jax.ShapeDtypeStruct((N_DEV, SMALL_ROWS, dc), F32), jax.ShapeDtypeStruct((N_DEV, 8, wf), F32)],
    )(ids, sg, ggf, loss, *accs, dep)


def _small_reduce(pack_all, dmod_all, nb, dep):
    def body(pk_ref, dm_ref, dep_ref, red_ref, dmod_ref, gb_ref):
        tot = pk_ref[0]
        for dev in range(1, N_DEV):
            tot = tot + pk_ref[dev]
        red_ref[...] = tot
        for f in range(fold):
            gb = jnp.zeros((1, wf), F32)
            for dev in range(N_DEV):
                for b in range(nb):
                    seg = dm_ref[dev, b * fold + f:b * fold + f + 1, :]
                    dmod_ref[dev * nb + b:dev * nb + b + 1, f * wf:(f + 1) * wf] = seg
                    gb = gb + seg
            gb_ref[:, f * wf:(f + 1) * wf] = gb

    fold = 8 // nb
    wf = dmod_all.shape[2]
    return pl.pallas_call(
        body, name="small_reduce",
        out_shape=[jax.ShapeDtypeStruct(pack_all.shape[1:], F32),
                   jax.ShapeDtypeStruct((N_DEV * nb, fold * wf), F32),
                   jax.ShapeDtypeStruct((1, fold * wf), F32)],
        in_specs=[VMEM] * 2 + [ANY], out_specs=[VMEM] * 3,
    )(pack_all, dmod_all, dep)


def _adam(w, g, m, v):
    m = ADAM_B1 * m + (1.0 - ADAM_B1) * g
    v = ADAM_B2 * v + (1.0 - ADAM_B2) * (g * g)
    m_hat = m / (1.0 - ADAM_B1 ** ADAM_STEP)
    v_hat = v / (1.0 - ADAM_B2 ** ADAM_STEP)
    delta = -ADAM_LR * (m_hat / (jnp.sqrt(v_hat) + ADAM_EPS) + ADAM_WD * w)
    return delta, m, v


def _adamw_big(w, g, m, v, name, deps=()):
    r, c = w.shape
    tr = min(r, 256)

    def body(w_ref, g_ref, m_ref, v_ref, *rest):
        go_ref, d_ref, nm_ref, nv_ref = rest[len(deps):]
        g = g_ref[...]
        go_ref[...] = g
        d_ref[...], nm_ref[...], nv_ref[...] = _adam(w_ref[...], g, m_ref[...], v_ref[...])

    s = jax.ShapeDtypeStruct((r, c), F32)
    return pl.pallas_call(
        body, name=name, grid=(r // tr,),
        in_specs=[_rowblk(tr, c)] * 4 + [ANY] * len(deps), out_specs=[_rowblk(tr, c)] * 4, out_shape=[s, s, s, s],
        compiler_params=pltpu.CompilerParams(dimension_semantics=("parallel",)),
    )(w, g, m, v, *deps)


def _adamw_ada(act_t, dmod_cols, w, m, v):
    r, c = w.shape
    tr = min(r, 256)
    nb = act_t.shape[1]

    def body(a_ref, dm_ref, w_ref, m_ref, v_ref, g_ref, d_ref, nm_ref, nv_ref):
        g = jnp.dot(a_ref[...], dm_ref[...], preferred_element_type=F32, precision=HIGHEST)
        g_ref[...] = g
        d_ref[...], nm_ref[...], nv_ref[...] = _adam(w_ref[...], g, m_ref[...], v_ref[...])

    s = jax.ShapeDtypeStruct((r, c), F32)
    return pl.pallas_call(
        body, name="adamw_w_ada", grid=(r // tr,),
        in_specs=[_rowblk(tr, nb), _const((nb, c))] + [_rowblk(tr, c)] * 3,
        out_specs=[_rowblk(tr, c)] * 4, out_shape=[s, s, s, s],
        compiler_params=pltpu.CompilerParams(dimension_semantics=("parallel",)),
    )(act_t, dmod_cols, w, m, v)


def _adamw_small(ids, red, g_bada, ws, ms, vs):
    n = len(ws)
    cw = ws[1].shape[1]

    def body(ids_ref, red_ref, gb_ref, *refs):
        cols = pl.ds(pl.multiple_of(ids_ref[0] * cw, 128), cw)
        gs = [gb_ref[...], red_ref[SG_DWW:SG_DWW + CONF_K, cols], red_ref[SG_DWB:SG_DWB + 1, :],
              red_ref[SG_LNG:SG_LNG + 1, :], red_ref[SG_LNB:SG_LNB + 1, :], red_ref[SG_SCW:SG_SCW + SC_K, cols],
              jnp.concatenate([red_ref[SG_N:SG_N + 1, :], red_ref[SG_N + 1:SG_N + 2, :]], axis=1)]
        for i in range(n):
            w, m, v = (refs[j * n + i][...] for j in range(3))
            dl, nm, nv = _adam(w, gs[i], m, v)
            refs[3 * n + i][...] = gs[i]
            refs[4 * n + i][...] = dl
            refs[5 * n + i][...] = nm
            refs[6 * n + i][...] = nv

    whole = lambda a: pl.BlockSpec(a.shape, lambda i, ids: (0,) * a.ndim)
    shapes = [jax.ShapeDtypeStruct(w.shape, F32) for w in ws]
    return pl.pallas_call(
        body, name="adamw_small",
        grid_spec=pltpu.PrefetchScalarGridSpec(
            num_scalar_prefetch=1, grid=(1,),
            in_specs=[whole(a) for a in (red, g_bada, *ws, *ms, *vs)],
            out_specs=[whole(a) for a in ws] * 4),
        out_shape=shapes * 4,
    )(ids, red, g_bada, *ws, *ms, *vs)


def kernel(x, c, w_ada, b_ada, w_in, conf_dw_w, conf_dw_b, conf_ln_g, conf_ln_b, sc_conv_w, w_out, w_mlp1, w_mlp2, g_final, loss_target, m_w_ada, m_b_ada, m_w_in, m_conf_dw_w, m_conf_dw_b, m_conf_ln_g, m_conf_ln_b, m_sc_conv_w, m_w_out, m_w_mlp1, m_w_mlp2, m_g_final, v_w_ada, v_b_ada, v_w_in, v_conf_dw_w, v_conf_dw_b, v_conf_ln_g, v_conf_ln_b, v_sc_conv_w, v_w_out, v_w_mlp1, v_w_mlp2, v_g_final):
    nb, t, d = x.shape
    n = nb * t
    dc = d // 2
    ada_w = w_ada.shape[2]
    ax, ay, ac = _me()
    chip = 2 * ax + ay
    dev = 2 * chip + ac
    ids = jnp.stack([chip, ac, dev]).astype(jnp.int32)

    lays = _wlayout(d)
    names = ("in", "out", "mlp1", "mlp2")
    fulls = _cast_place(ids, [w_in[0], w_out[0], w_mlp1[0], w_mlp2[0]], lays)

    c_pad = jnp.zeros((8, d), F32).at[0:nb].set(c)
    cw_pad = jnp.zeros((SMALL_ROWS, dc // N_CHIPS), F32)
    cw_pad = cw_pad.at[0:CONF_K].set(conf_dw_w[0]).at[HALO:HALO + SC_K].set(sc_conv_w[0])
    c_all8, cw_all8 = _all_gather8([c_pad, cw_pad], "gather_c")
    plan_i = _gather_chip_plan(lays[0:1])
    sems_i, bufs_i, tok_i = _copy_start("gather_in_start", [fulls[0]], 3, plan_i, [c_all8])
    c_all = c_all8[:, 0:nb].reshape(N_DEV * nb, d) + tok_i[0, 0]
    cw_full = jnp.concatenate([cw_all8[2 * k] for k in range(N_CHIPS)], axis=1)
    dww, scw = cw_full[0:CONF_K], cw_full[HALO:HALO + SC_K]
    b_cols = lax.dynamic_slice(b_ada, (0, chip * ada_w), (1, ada_w))
    c_act, mod_shard = _ada_mod(c_all, w_ada[0], b_cols)
    mod_all = _gather_mod(mod_shard)
    mod3 = lax.dynamic_slice(mod_all, (dev * nb, 0), (nb, 6 * d)).reshape(nb, 1, 6 * d)

    x2 = x.reshape(n, d)
    tgt = loss_target.reshape(n, d)
    (wf_in,) = _copy_wait("gather_in_wait", bufs_i, sems_i, plan_i, [mod3])
    (wf_in,) = _copy_blocking("gather_in_pass", [wf_in], 3, _gather_pass_plan(lays[0:1]))
    plan_o, plan_b, plan_p = _gather_direct_plan(lays[1:2]), _gather_chip_plan(lays[2:4]), _gather_pass_plan(lays[2:4])
    sems_o, bufs_o, tok_o = _copy_start("gather_out_start", [fulls[1]], 6, plan_o, [wf_in, mod3])
    sems_b, bufs_b, tok_b = _copy_start("gather_mlp_start", fulls[2:4], 6, plan_b, [tok_o])
    proj, h1, mixed, a1, q = _in_mix_fwd(x2, mod3, wf_in, dww, conf_dw_b, conf_ln_g, conf_ln_b, scw, t, tok_b)
    (wf_out,) = _copy_wait("gather_out_wait", bufs_o, sems_o, plan_o, [mixed])
    bufs_b = _copy_wait("gather_mlp_wait", bufs_b, sems_b, plan_b, [mixed])
    sems_p, bufs_p, tok_p = _copy_start("gather_pass_start", bufs_b, 6, plan_p)
    x1, y1, h2 = _out_proj(mixed, wf_out, x2, mod3, t, tok_p)
    wf_1, wf_2 = _copy_wait("gather_pass_wait", bufs_p, sems_p, plan_p, [h2])
    z, dx2, dy2, dg2, ggf, loss_p = _mlp_fwd(h2, x1, tgt, mod3, g_final.reshape(1, d), wf_1, wf_2, t)

    dz, dx1, dy1, dsh2, dsc2, dg1 = _mlp_bwd(dy2, z, x1, dx2, y1, mod3, wf_1, wf_2, t)
    g_w2 = _wgrad(z, dy2, "wgrad_mlp2", relu2=True)
    g_w1 = _wgrad(h2, dz, "wgrad_mlp1")
    made = {}

    def behind_pair_exchange(tok):
        made["g_wout"] = _wgrad(mixed, dy1, "wgrad_out", deps=[tok])
        return [made["g_wout"]]

    def behind_chip_exchange(tok):
        made["dproj"], made["sg"], made["grad_x"], made["dsh1"], made["dsc1"] = _mix_in_bwd(
            dy1, wf_out, a1, q, proj, dww, conf_ln_g, conf_ln_b, scw, wf_in, x2, dx1, mod3, t, tok)
        return [made["dproj"]]

    def behind_pair_share(tok):
        made["g_win"] = _wgrad(h1, made["dproj"], "wgrad_in", deps=[tok])
        return [made["g_win"]]

    gr_1, gr_2 = _reduce_scatter(ids, [g_w1, g_w2], lays[2:4], names[2:4], "m",
                                 (behind_pair_exchange, behind_chip_exchange, behind_pair_share))

    big = {}

    plan_g = _gather8_plan(2)

    def behind_pair_exchange_in(tok):
        packs = _pack_small(ids, made["sg"], ggf, loss_p, (made["dsh1"], made["dsc1"], dg1, dsh2, dsc2, dg2), d, tok)
        made["gather"] = _copy_start("gather_small_start", list(packs), 2 * (N_DEV - 1), plan_g)
        return [made["gather"][2]]

    def behind_chip_exchange_in(tok):
        sems_g, bufs_g, _ = made["gather"]
        gathered = _copy_wait("gather_small_wait", bufs_g, sems_g, plan_g, [tok])
        red, dmod_all, g_bada = _small_reduce(*gathered, nb, tok)
        dmod_cols = lax.dynamic_slice(dmod_all, (0, chip * ada_w), (N_DEV * nb, ada_w))
        made["ada"] = _adamw_ada(c_act.T, dmod_cols, w_ada[0], m_w_ada[0], v_w_ada[0])
        big["w_mlp1"] = _adamw_big(w_mlp1[0], gr_1, m_w_mlp1[0], v_w_mlp1[0], "adamw_w_mlp1")
        small_w = [b_ada, conf_dw_w[0], conf_dw_b, conf_ln_g, conf_ln_b, sc_conv_w[0], g_final.reshape(1, d)]
        small_m = [m_b_ada, m_conf_dw_w[0], m_conf_dw_b, m_conf_ln_g, m_conf_ln_b, m_sc_conv_w[0],
                   m_g_final.reshape(1, d)]
        small_v = [v_b_ada, v_conf_dw_w[0], v_conf_dw_b, v_conf_ln_g, v_conf_ln_b, v_sc_conv_w[0],
                   v_g_final.reshape(1, d)]
        made["upd"] = _adamw_small(ids, red, g_bada, small_w, small_m, small_v)
        made["loss"] = red[SG_N + 2, 0]
        return [big["w_mlp1"][0], made["upd"][0]]

    def behind_pair_share_in(tok):
        big["w_mlp2"] = _adamw_big(w_mlp2[0], gr_2, m_w_mlp2[0], v_w_mlp2[0], "adamw_w_mlp2", deps=[tok])
        return [big["w_mlp2"][0]]

    gr_in, gr_out = _reduce_scatter(ids, [made["g_win"], made["g_wout"]], lays[0:2], names[0:2], "i",
                                    (behind_pair_exchange_in, behind_chip_exchange_in, behind_pair_share_in))
    big["w_in"] = _adamw_big(w_in[0], gr_in, m_w_in[0], v_w_in[0], "adamw_w_in")
    big["w_out"] = _adamw_big(w_out[0], gr_out, m_w_out[0], v_w_out[0], "adamw_w_out")
    grad_x, upd, loss = made["grad_x"], made["upd"], made["loss"]
    g_wada, d_wada, nm_wada, nv_wada = made["ada"]
    ns = len(upd) // 4
    small_g, s_delta, s_m, s_v = upd[0:ns], upd[ns:2 * ns], upd[2 * ns:3 * ns], upd[3 * ns:4 * ns]

    def outs(kind_big, kind_small, wada):
        sm = kind_small
        return (wada[None], sm[0], kind_big["w_in"][None], sm[1][None], sm[2], sm[3], sm[4], sm[5][None],
                kind_big["w_out"][None], kind_big["w_mlp1"][None], kind_big["w_mlp2"][None], sm[6].reshape(d))

    grads_out = outs({k: v[0] for k, v in big.items()}, small_g, g_wada)
    delta_out = outs({k: v[1] for k, v in big.items()}, s_delta, d_wada)
    m_out = outs({k: v[2] for k, v in big.items()}, s_m, nm_wada)
    v_out = outs({k: v[3] for k, v in big.items()}, s_v, nv_wada)
    return (loss, grad_x.reshape(nb, t, d), *grads_out, *delta_out, *m_out, *v_out)
```
